```python
import math
import jax, jax.numpy as jnp
from jax import lax
import numpy as np

D_MODEL = 1024
BATCH = 8
SEQ = 4096
DEPTH = 1

PLE_DIM = 256
D_FF = 2816
FFN_RES_WEIGHT = 0.5
NORM_EPS = 1e-6
Q_BLOCK = 128

MLA_HEADS = 8
MLA_NOPE = 64
MLA_ROPE = 32
MLA_QK = MLA_NOPE + MLA_ROPE
MLA_V = 64
Q_LORA = 384
KV_LORA = 256
ROPE_BASE = 10000.0

SB_HEADS = 8
SB_HEAD_DIM = 64
SB_WIDTH = SB_HEADS * SB_HEAD_DIM
MLA_WIDTH = MLA_HEADS * MLA_V

COL_CQ = Q_LORA
COL_CKV = KV_LORA
COL_KROPE = MLA_ROPE
COL_SB = 3 * SB_WIDTH
COL_GATES = 2 * D_MODEL
IN_COLS = COL_CQ + COL_CKV + COL_KROPE + COL_SB + COL_GATES
SPLITS = list(np.cumsum([COL_CQ, COL_CKV, COL_KROPE, COL_SB])[:])

kernel_name = "hybrid_mla_stickbreaking_macaron_ple"


def rms_norm(x, g):
    xf = x.astype(jnp.float32)
    r = lax.rsqrt(jnp.mean(xf * xf, axis=-1, keepdims=True) + NORM_EPS)
    return (xf * r).astype(x.dtype) * g


def apply_rope(x, positions):
    r = x.shape[-1]
    inv_freq = ROPE_BASE ** (-jnp.arange(0, r, 2, dtype=jnp.float32) / r)
    ang = positions.astype(jnp.float32)[..., None] * inv_freq
    cos = jnp.cos(ang)[:, :, None, :].astype(x.dtype)
    sin = jnp.sin(ang)[:, :, None, :].astype(x.dtype)
    x1, x2 = x[..., : r // 2], x[..., r // 2:]
    return jnp.concatenate([x1 * cos - x2 * sin, x2 * cos + x1 * sin], axis=-1)


def swiglu(u, w_in, w_out):
    a, b = jnp.split(u @ w_in, 2, axis=-1)
    return (jax.nn.silu(a) * b) @ w_out


def causal_softmax_attention(q, k, v):
    s_len = q.shape[2]
    scale = 1.0 / math.sqrt(q.shape[-1])
    outs = []
    for i in range(s_len // Q_BLOCK):
        k_len = (i + 1) * Q_BLOCK
        qb = q[:, :, i * Q_BLOCK:(i + 1) * Q_BLOCK]
        kb, vb = k[:, :, :k_len], v[:, :, :k_len]
        sc = jnp.einsum('bhqd,bhkd->bhqk', qb, kb).astype(jnp.float32) * scale
        q_pos = i * Q_BLOCK + jnp.arange(Q_BLOCK)
        mask = jnp.arange(k_len)[None, :] <= q_pos[:, None]
        w = jax.nn.softmax(jnp.where(mask, sc, -jnp.inf), axis=-1)
        outs.append(jnp.einsum('bhqk,bhkd->bhqd', w.astype(vb.dtype), vb))
    return jnp.concatenate(outs, axis=2)


def stick_breaking_attention(q, k, v):
    s_len = q.shape[2]
    scale = 1.0 / math.sqrt(q.shape[-1])
    outs = []
    for i in range(s_len // Q_BLOCK):
        k_len = (i + 1) * Q_BLOCK
        qb = q[:, :, i * Q_BLOCK:(i + 1) * Q_BLOCK]
        kb, vb = k[:, :, :k_len], v[:, :, :k_len]
        z = jnp.einsum('bhqd,bhkd->bhqk', qb, kb).astype(jnp.float32) * scale
        q_pos = i * Q_BLOCK + jnp.arange(Q_BLOCK)
        mask = jnp.arange(k_len)[None, :] < q_pos[:, None]
        log_1m = jnp.where(mask, jax.nn.log_sigmoid(-z), 0.0)
        suffix = lax.cumsum(log_1m, axis=3, reverse=True) - log_1m
        a = jnp.where(mask, jnp.exp(jax.nn.log_sigmoid(z) + suffix), 0.0)
        outs.append(jnp.einsum('bhqk,bhkd->bhqd', a.astype(vb.dtype), vb))
    return jnp.concatenate(outs, axis=2)


def _fwd_setup_inputs(seed: int = 0) -> dict:
    key = jax.random.key(seed)
    ks = iter(jax.random.split(key, 32))
    f32 = jnp.float32

    def w(shape, fan_in):
        return jax.random.normal(next(ks), (DEPTH,) + shape, f32) * fan_in ** -0.5

    def gain(n):
        return 1.0 + 0.02 * jax.random.normal(next(ks), (DEPTH, n), f32)

    x = jax.random.normal(next(ks), (BATCH, SEQ, D_MODEL), f32)
    p = jax.random.normal(next(ks), (DEPTH, BATCH, SEQ, PLE_DIM), f32)
    offset = jax.random.randint(next(ks), (BATCH, 1), 0, 1024, dtype=jnp.int32)
    positions = offset + jnp.arange(SEQ, dtype=jnp.int32)[None, :]
    return {
        "x": x,
        "p": p,
        "positions": positions,
        "ffn1_norm": gain(D_MODEL),
        "ffn1_w_in": w((D_MODEL, 2 * D_FF), D_MODEL),
        "ffn1_w_out": w((D_FF, D_MODEL), D_FF),
        "mix_norm": gain(D_MODEL),
        "w_in": w((D_MODEL, IN_COLS), D_MODEL),
        "q_latent_norm": gain(Q_LORA),
        "w_q_up": w((Q_LORA, MLA_HEADS * MLA_QK), Q_LORA),
        "kv_latent_norm": gain(KV_LORA),
        "w_kv_up": w((KV_LORA, MLA_HEADS * (MLA_NOPE + MLA_V)), KV_LORA),
        "q_head_norm": gain(MLA_QK),
        "k_head_norm": gain(MLA_QK),
        "w_branch_mla": w((MLA_WIDTH, D_MODEL), MLA_WIDTH),
        "w_branch_sb": w((SB_WIDTH, D_MODEL), SB_WIDTH),
        "w_out": w((D_MODEL, D_MODEL), D_MODEL),
        "ffn2_norm": gain(D_MODEL),
        "ffn2_w_in": w((D_MODEL, 2 * D_FF), D_MODEL),
        "ffn2_w_out": w((D_FF, D_MODEL), D_FF),
        "ple_norm": gain(D_MODEL),
        "w_ple_gate": w((D_MODEL, D_MODEL), D_MODEL),
        "w_ple_proj": w((PLE_DIM, D_MODEL), PLE_DIM),
    }


def _fwd_reference(x, p, positions, ffn1_norm, ffn1_w_in, ffn1_w_out, mix_norm, w_in,
              q_latent_norm, w_q_up, kv_latent_norm, w_kv_up, q_head_norm, k_head_norm,
              w_branch_mla, w_branch_sb, w_out, ffn2_norm, ffn2_w_in, ffn2_w_out,
              ple_norm, w_ple_gate, w_ple_proj):
    b, s, _ = x.shape
    h = x
    for i in range(DEPTH):
        h = h + FFN_RES_WEIGHT * swiglu(rms_norm(h, ffn1_norm[i]), ffn1_w_in[i], ffn1_w_out[i])

        u = rms_norm(h, mix_norm[i])
        proj = u @ w_in[i]
        c_q, c_kv, k_rope, sb_qkv, gates = jnp.split(proj, SPLITS, axis=-1)

        q = (rms_norm(c_q, q_latent_norm[i]) @ w_q_up[i]).reshape(b, s, MLA_HEADS, MLA_QK)
        kv = (rms_norm(c_kv, kv_latent_norm[i]) @ w_kv_up[i]).reshape(b, s, MLA_HEADS, MLA_NOPE + MLA_V)
        k_nope, v_mla = kv[..., :MLA_NOPE], kv[..., MLA_NOPE:]
        k_r = jnp.broadcast_to(k_rope[:, :, None, :], (b, s, MLA_HEADS, MLA_ROPE))
        k = jnp.concatenate([k_nope, k_r], axis=-1)
        q = rms_norm(q, q_head_norm[i])
        k = rms_norm(k, k_head_norm[i])
        q = jnp.concatenate([q[..., :MLA_NOPE], apply_rope(q[..., MLA_NOPE:], positions)], axis=-1)
        k = jnp.concatenate([k[..., :MLA_NOPE], apply_rope(k[..., MLA_NOPE:], positions)], axis=-1)
        o_mla = causal_softmax_attention(q.transpose(0, 2, 1, 3), k.transpose(0, 2, 1, 3),
                                         v_mla.transpose(0, 2, 1, 3))
        o_mla = o_mla.transpose(0, 2, 1, 3).reshape(b, s, MLA_WIDTH)

        sq, sk, sv = [t.reshape(b, s, SB_HEADS, SB_HEAD_DIM).transpose(0, 2, 1, 3)
                      for t in jnp.split(sb_qkv, 3, axis=-1)]
        o_sb = stick_breaking_attention(sq, sk, sv)
        o_sb = o_sb.transpose(0, 2, 1, 3).reshape(b, s, SB_WIDTH)

        g_mla, g_sb = jnp.split(jax.nn.sigmoid(gates), 2, axis=-1)
        merged = g_mla * (o_mla @ w_branch_mla[i]) + g_sb * (o_sb @ w_branch_sb[i])
        h = h + merged @ w_out[i]

        h = h + FFN_RES_WEIGHT * swiglu(rms_norm(h, ffn2_norm[i]), ffn2_w_in[i], ffn2_w_out[i])

        ple_gate = jax.nn.sigmoid(rms_norm(h, ple_norm[i]) @ w_ple_gate[i])
        h = h + ple_gate * (p[i] @ w_ple_proj[i])
    return h


import jax as _jax
import jax.numpy as _jnp

TWIN_FORMAT = 'train_step'
FWD_PARAMS = ['x', 'p', 'positions', 'ffn1_norm', 'ffn1_w_in', 'ffn1_w_out', 'mix_norm', 'w_in', 'q_latent_norm', 'w_q_up', 'kv_latent_norm', 'w_kv_up', 'q_head_norm', 'k_head_norm', 'w_branch_mla', 'w_branch_sb', 'w_out', 'ffn2_norm', 'ffn2_w_in', 'ffn2_w_out', 'ple_norm', 'w_ple_gate', 'w_ple_proj']
TWIN_WEIGHTS = ['ffn1_norm', 'ffn1_w_in', 'ffn1_w_out', 'mix_norm', 'w_in', 'q_latent_norm', 'w_q_up', 'kv_latent_norm', 'w_kv_up', 'q_head_norm', 'k_head_norm', 'w_branch_mla', 'w_branch_sb', 'w_out', 'ffn2_norm', 'ffn2_w_in', 'ffn2_w_out', 'ple_norm', 'w_ple_gate', 'w_ple_proj']
TWIN_DIFF_INPUT = 'x'
TWIN_INPUTS = ['x', 'p', 'positions', 'ffn1_norm', 'ffn1_w_in', 'ffn1_w_out', 'mix_norm', 'w_in', 'q_latent_norm', 'w_q_up', 'kv_latent_norm', 'w_kv_up', 'q_head_norm', 'k_head_norm', 'w_branch_mla', 'w_branch_sb', 'w_out', 'ffn2_norm', 'ffn2_w_in', 'ffn2_w_out', 'ple_norm', 'w_ple_gate', 'w_ple_proj', 'loss_target', 'm_ffn1_norm', 'm_ffn1_w_in', 'm_ffn1_w_out', 'm_mix_norm', 'm_w_in', 'm_q_latent_norm', 'm_w_q_up', 'm_kv_latent_norm', 'm_w_kv_up', 'm_q_head_norm', 'm_k_head_norm', 'm_w_branch_mla', 'm_w_branch_sb', 'm_w_out', 'm_ffn2_norm', 'm_ffn2_w_in', 'm_ffn2_w_out', 'm_ple_norm', 'm_w_ple_gate', 'm_w_ple_proj', 'v_ffn1_norm', 'v_ffn1_w_in', 'v_ffn1_w_out', 'v_mix_norm', 'v_w_in', 'v_q_latent_norm', 'v_w_q_up', 'v_kv_latent_norm', 'v_w_kv_up', 'v_q_head_norm', 'v_k_head_norm', 'v_w_branch_mla', 'v_w_branch_sb', 'v_w_out', 'v_ffn2_norm', 'v_ffn2_w_in', 'v_ffn2_w_out', 'v_ple_norm', 'v_w_ple_gate', 'v_w_ple_proj']
TWIN_OUTPUTS = ['loss', 'grad_x', 'grad_ffn1_norm', 'grad_ffn1_w_in', 'grad_ffn1_w_out', 'grad_mix_norm', 'grad_w_in', 'grad_q_latent_norm', 'grad_w_q_up', 'grad_kv_latent_norm', 'grad_w_kv_up', 'grad_q_head_norm', 'grad_k_head_norm', 'grad_w_branch_mla', 'grad_w_branch_sb', 'grad_w_out', 'grad_ffn2_norm', 'grad_ffn2_w_in', 'grad_ffn2_w_out', 'grad_ple_norm', 'grad_w_ple_gate', 'grad_w_ple_proj', 'delta_ffn1_norm', 'delta_ffn1_w_in', 'delta_ffn1_w_out', 'delta_mix_norm', 'delta_w_in', 'delta_q_latent_norm', 'delta_w_q_up', 'delta_kv_latent_norm', 'delta_w_kv_up', 'delta_q_head_norm', 'delta_k_head_norm', 'delta_w_branch_mla', 'delta_w_branch_sb', 'delta_w_out', 'delta_ffn2_norm', 'delta_ffn2_w_in', 'delta_ffn2_w_out', 'delta_ple_norm', 'delta_w_ple_gate', 'delta_w_ple_proj', 'new_m_ffn1_norm', 'new_m_ffn1_w_in', 'new_m_ffn1_w_out', 'new_m_mix_norm', 'new_m_w_in', 'new_m_q_latent_norm', 'new_m_w_q_up', 'new_m_kv_latent_norm', 'new_m_w_kv_up', 'new_m_q_head_norm', 'new_m_k_head_norm', 'new_m_w_branch_mla', 'new_m_w_branch_sb', 'new_m_w_out', 'new_m_ffn2_norm', 'new_m_ffn2_w_in', 'new_m_ffn2_w_out', 'new_m_ple_norm', 'new_m_w_ple_gate', 'new_m_w_ple_proj', 'new_v_ffn1_norm', 'new_v_ffn1_w_in', 'new_v_ffn1_w_out', 'new_v_mix_norm', 'new_v_w_in', 'new_v_q_latent_norm', 'new_v_w_q_up', 'new_v_kv_latent_norm', 'new_v_w_kv_up', 'new_v_q_head_norm', 'new_v_k_head_norm', 'new_v_w_branch_mla', 'new_v_w_branch_sb', 'new_v_w_out', 'new_v_ffn2_norm', 'new_v_ffn2_w_in', 'new_v_ffn2_w_out', 'new_v_ple_norm', 'new_v_w_ple_gate', 'new_v_w_ple_proj']
TWIN_LEAF_KINDS = {'loss': 'loss', 'grad_x': 'grad_x', 'grad_ffn1_norm': 'grad_w', 'grad_ffn1_w_in': 'grad_w', 'grad_ffn1_w_out': 'grad_w', 'grad_mix_norm': 'grad_w', 'grad_w_in': 'grad_w', 'grad_q_latent_norm': 'grad_w', 'grad_w_q_up': 'grad_w', 'grad_kv_latent_norm': 'grad_w', 'grad_w_kv_up': 'grad_w', 'grad_q_head_norm': 'grad_w', 'grad_k_head_norm': 'grad_w', 'grad_w_branch_mla': 'grad_w', 'grad_w_branch_sb': 'grad_w', 'grad_w_out': 'grad_w', 'grad_ffn2_norm': 'grad_w', 'grad_ffn2_w_in': 'grad_w', 'grad_ffn2_w_out': 'grad_w', 'grad_ple_norm': 'grad_w', 'grad_w_ple_gate': 'grad_w', 'grad_w_ple_proj': 'grad_w', 'delta_ffn1_norm': 'delta_w', 'delta_ffn1_w_in': 'delta_w', 'delta_ffn1_w_out': 'delta_w', 'delta_mix_norm': 'delta_w', 'delta_w_in': 'delta_w', 'delta_q_latent_norm': 'delta_w', 'delta_w_q_up': 'delta_w', 'delta_kv_latent_norm': 'delta_w', 'delta_w_kv_up': 'delta_w', 'delta_q_head_norm': 'delta_w', 'delta_k_head_norm': 'delta_w', 'delta_w_branch_mla': 'delta_w', 'delta_w_branch_sb': 'delta_w', 'delta_w_out': 'delta_w', 'delta_ffn2_norm': 'delta_w', 'delta_ffn2_w_in': 'delta_w', 'delta_ffn2_w_out': 'delta_w', 'delta_ple_norm': 'delta_w', 'delta_w_ple_gate': 'delta_w', 'delta_w_ple_proj': 'delta_w', 'new_m_ffn1_norm': 'new_m', 'new_m_ffn1_w_in': 'new_m', 'new_m_ffn1_w_out': 'new_m', 'new_m_mix_norm': 'new_m', 'new_m_w_in': 'new_m', 'new_m_q_latent_norm': 'new_m', 'new_m_w_q_up': 'new_m', 'new_m_kv_latent_norm': 'new_m', 'new_m_w_kv_up': 'new_m', 'new_m_q_head_norm': 'new_m', 'new_m_k_head_norm': 'new_m', 'new_m_w_branch_mla': 'new_m', 'new_m_w_branch_sb': 'new_m', 'new_m_w_out': 'new_m', 'new_m_ffn2_norm': 'new_m', 'new_m_ffn2_w_in': 'new_m', 'new_m_ffn2_w_out': 'new_m', 'new_m_ple_norm': 'new_m', 'new_m_w_ple_gate': 'new_m', 'new_m_w_ple_proj': 'new_m', 'new_v_ffn1_norm': 'new_v', 'new_v_ffn1_w_in': 'new_v', 'new_v_ffn1_w_out': 'new_v', 'new_v_mix_norm': 'new_v', 'new_v_w_in': 'new_v', 'new_v_q_latent_norm': 'new_v', 'new_v_w_q_up': 'new_v', 'new_v_kv_latent_norm': 'new_v', 'new_v_w_kv_up': 'new_v', 'new_v_q_head_norm': 'new_v', 'new_v_k_head_norm': 'new_v', 'new_v_w_branch_mla': 'new_v', 'new_v_w_branch_sb': 'new_v', 'new_v_w_out': 'new_v', 'new_v_ffn2_norm': 'new_v', 'new_v_ffn2_w_in': 'new_v', 'new_v_ffn2_w_out': 'new_v', 'new_v_ple_norm': 'new_v', 'new_v_w_ple_gate': 'new_v', 'new_v_w_ple_proj': 'new_v'}


def _forward(args):
    return _fwd_reference(*[args[k] for k in FWD_PARAMS])


def _output_shape():
    out = _jax.eval_shape(lambda: _forward(_fwd_setup_inputs(0)))
    return out.shape, out.dtype

N_MICROBATCH = 1
ADAM_LR = 0.001
ADAM_B1 = 0.9
ADAM_B2 = 0.999
ADAM_EPS = 1e-08
ADAM_WD = 0.01
ADAM_STEP = 10
PER_EXAMPLE_BATCH_AXIS = {'x': 0, 'p': 1, 'positions': 0, 'loss_target': 0}
SHARED_INPUTS = []
_WEIGHT_DTYPES = {'ffn1_norm': _jnp.float32, 'ffn1_w_in': _jnp.float32, 'ffn1_w_out': _jnp.float32, 'mix_norm': _jnp.float32, 'w_in': _jnp.float32, 'q_latent_norm': _jnp.float32, 'w_q_up': _jnp.float32, 'kv_latent_norm': _jnp.float32, 'w_kv_up': _jnp.float32, 'q_head_norm': _jnp.float32, 'k_head_norm': _jnp.float32, 'w_branch_mla': _jnp.float32, 'w_branch_sb': _jnp.float32, 'w_out': _jnp.float32, 'ffn2_norm': _jnp.float32, 'ffn2_w_in': _jnp.float32, 'ffn2_w_out': _jnp.float32, 'ple_norm': _jnp.float32, 'w_ple_gate': _jnp.float32, 'w_ple_proj': _jnp.float32}
MOMENT_SCALE = {'ffn1_norm': 6.190592e+00, 'ffn1_w_in': 7.348701e-02, 'ffn1_w_out': 1.234279e-01, 'mix_norm': 5.366336e+00, 'w_in': 9.728345e-02, 'q_latent_norm': 5.813599e-02, 'w_q_up': 3.985810e-02, 'kv_latent_norm': 3.552644e-01, 'w_kv_up': 5.521700e-02, 'q_head_norm': 5.012124e-01, 'k_head_norm': 5.005813e-01, 'w_branch_mla': 4.389359e-02, 'w_branch_sb': 1.939475e-01, 'w_out': 1.607492e-01, 'ffn2_norm': 6.185922e+00, 'ffn2_w_in': 6.535552e-02, 'ffn2_w_out': 1.132197e-01, 'ple_norm': 9.448769e-01, 'w_ple_gate': 7.188010e-02, 'w_ple_proj': 4.223282e-01}


def _to_microbatches(a, axis):
    t = _jnp.moveaxis(a, axis, 0)
    t = t.reshape((N_MICROBATCH, t.shape[0] // N_MICROBATCH) + t.shape[1:])
    return _jnp.moveaxis(t, 1, axis + 1)


def setup_inputs(seed: int = 0) -> dict:
    inp = _fwd_setup_inputs(seed)
    key = _jax.random.fold_in(_jax.random.key(seed), 7919)
    shape, _ = _output_shape()
    out = dict(inp)
    out["loss_target"] = _jax.random.normal(_jax.random.fold_in(key, 0), shape, _jnp.float32)
    for i, name in enumerate(TWIN_WEIGHTS):
        w = inp[name].astype(_jnp.float32)
        if MOMENT_SCALE is None:
            s = _jnp.sqrt(_jnp.mean(_jnp.square(w)) + 1e-30)
        else:
            s = MOMENT_SCALE[name]
        km, kv = _jax.random.split(_jax.random.fold_in(key, i + 1))
        out[name] = w
        out["m_" + name] = s * _jax.random.normal(km, w.shape, _jnp.float32)
        out["v_" + name] = (s * s) * _jax.random.uniform(kv, w.shape, _jnp.float32, 0.5, 1.5)
    if N_MICROBATCH > 1:
        for name, axis in PER_EXAMPLE_BATCH_AXIS.items():
            out[name] = _to_microbatches(out[name], axis)
    return {'x': out['x'], 'p': out['p'], 'positions': out['positions'], 'ffn1_norm': out['ffn1_norm'], 'ffn1_w_in': out['ffn1_w_in'], 'ffn1_w_out': out['ffn1_w_out'], 'mix_norm': out['mix_norm'], 'w_in': out['w_in'], 'q_latent_norm': out['q_latent_norm'], 'w_q_up': out['w_q_up'], 'kv_latent_norm': out['kv_latent_norm'], 'w_kv_up': out['w_kv_up'], 'q_head_norm': out['q_head_norm'], 'k_head_norm': out['k_head_norm'], 'w_branch_mla': out['w_branch_mla'], 'w_branch_sb': out['w_branch_sb'], 'w_out': out['w_out'], 'ffn2_norm': out['ffn2_norm'], 'ffn2_w_in': out['ffn2_w_in'], 'ffn2_w_out': out['ffn2_w_out'], 'ple_norm': out['ple_norm'], 'w_ple_gate': out['w_ple_gate'], 'w_ple_proj': out['w_ple_proj'], 'loss_target': out['loss_target'], 'm_ffn1_norm': out['m_ffn1_norm'], 'm_ffn1_w_in': out['m_ffn1_w_in'], 'm_ffn1_w_out': out['m_ffn1_w_out'], 'm_mix_norm': out['m_mix_norm'], 'm_w_in': out['m_w_in'], 'm_q_latent_norm': out['m_q_latent_norm'], 'm_w_q_up': out['m_w_q_up'], 'm_kv_latent_norm': out['m_kv_latent_norm'], 'm_w_kv_up': out['m_w_kv_up'], 'm_q_head_norm': out['m_q_head_norm'], 'm_k_head_norm': out['m_k_head_norm'], 'm_w_branch_mla': out['m_w_branch_mla'], 'm_w_branch_sb': out['m_w_branch_sb'], 'm_w_out': out['m_w_out'], 'm_ffn2_norm': out['m_ffn2_norm'], 'm_ffn2_w_in': out['m_ffn2_w_in'], 'm_ffn2_w_out': out['m_ffn2_w_out'], 'm_ple_norm': out['m_ple_norm'], 'm_w_ple_gate': out['m_w_ple_gate'], 'm_w_ple_proj': out['m_w_ple_proj'], 'v_ffn1_norm': out['v_ffn1_norm'], 'v_ffn1_w_in': out['v_ffn1_w_in'], 'v_ffn1_w_out': out['v_ffn1_w_out'], 'v_mix_norm': out['v_mix_norm'], 'v_w_in': out['v_w_in'], 'v_q_latent_norm': out['v_q_latent_norm'], 'v_w_q_up': out['v_w_q_up'], 'v_kv_latent_norm': out['v_kv_latent_norm'], 'v_w_kv_up': out['v_w_kv_up'], 'v_q_head_norm': out['v_q_head_norm'], 'v_k_head_norm': out['v_k_head_norm'], 'v_w_branch_mla': out['v_w_branch_mla'], 'v_w_branch_sb': out['v_w_branch_sb'], 'v_w_out': out['v_w_out'], 'v_ffn2_norm': out['v_ffn2_norm'], 'v_ffn2_w_in': out['v_ffn2_w_in'], 'v_ffn2_w_out': out['v_ffn2_w_out'], 'v_ple_norm': out['v_ple_norm'], 'v_w_ple_gate': out['v_w_ple_gate'], 'v_w_ple_proj': out['v_w_ple_proj']}


def _loss(weights, diff, rest, loss_target):
    with _jax.named_scope("forward"):
        args = {**rest, TWIN_DIFF_INPUT: diff, **{k: w.astype(_WEIGHT_DTYPES[k]) for k, w in weights.items()}}
        y = _forward(args)
    with _jax.named_scope("loss_head"):
        err = _jnp.square(y.astype(_jnp.float32) - loss_target)
        return 0.5 * _jnp.sum(_jnp.mean(err, axis=-1)) if err.ndim else 0.5 * err


def _adamw(w, g, m, v):
    m = ADAM_B1 * m + (1.0 - ADAM_B1) * g
    v = ADAM_B2 * v + (1.0 - ADAM_B2) * _jnp.square(g)
    m_hat = m / (1.0 - ADAM_B1 ** ADAM_STEP)
    v_hat = v / (1.0 - ADAM_B2 ** ADAM_STEP)
    delta = -ADAM_LR * (m_hat / (_jnp.sqrt(v_hat) + ADAM_EPS) + ADAM_WD * w)
    return delta, m, v


def reference(x, p, positions, ffn1_norm, ffn1_w_in, ffn1_w_out, mix_norm, w_in, q_latent_norm, w_q_up, kv_latent_norm, w_kv_up, q_head_norm, k_head_norm, w_branch_mla, w_branch_sb, w_out, ffn2_norm, ffn2_w_in, ffn2_w_out, ple_norm, w_ple_gate, w_ple_proj, loss_target, m_ffn1_norm, m_ffn1_w_in, m_ffn1_w_out, m_mix_norm, m_w_in, m_q_latent_norm, m_w_q_up, m_kv_latent_norm, m_w_kv_up, m_q_head_norm, m_k_head_norm, m_w_branch_mla, m_w_branch_sb, m_w_out, m_ffn2_norm, m_ffn2_w_in, m_ffn2_w_out, m_ple_norm, m_w_ple_gate, m_w_ple_proj, v_ffn1_norm, v_ffn1_w_in, v_ffn1_w_out, v_mix_norm, v_w_in, v_q_latent_norm, v_w_q_up, v_kv_latent_norm, v_w_kv_up, v_q_head_norm, v_k_head_norm, v_w_branch_mla, v_w_branch_sb, v_w_out, v_ffn2_norm, v_ffn2_w_in, v_ffn2_w_out, v_ple_norm, v_w_ple_gate, v_w_ple_proj):
    given = dict(x=x, p=p, positions=positions, ffn1_norm=ffn1_norm, ffn1_w_in=ffn1_w_in, ffn1_w_out=ffn1_w_out, mix_norm=mix_norm, w_in=w_in, q_latent_norm=q_latent_norm, w_q_up=w_q_up, kv_latent_norm=kv_latent_norm, w_kv_up=w_kv_up, q_head_norm=q_head_norm, k_head_norm=k_head_norm, w_branch_mla=w_branch_mla, w_branch_sb=w_branch_sb, w_out=w_out, ffn2_norm=ffn2_norm, ffn2_w_in=ffn2_w_in, ffn2_w_out=ffn2_w_out, ple_norm=ple_norm, w_ple_gate=w_ple_gate, w_ple_proj=w_ple_proj, loss_target=loss_target, m_ffn1_norm=m_ffn1_norm, m_ffn1_w_in=m_ffn1_w_in, m_ffn1_w_out=m_ffn1_w_out, m_mix_norm=m_mix_norm, m_w_in=m_w_in, m_q_latent_norm=m_q_latent_norm, m_w_q_up=m_w_q_up, m_kv_latent_norm=m_kv_latent_norm, m_w_kv_up=m_w_kv_up, m_q_head_norm=m_q_head_norm, m_k_head_norm=m_k_head_norm, m_w_branch_mla=m_w_branch_mla, m_w_branch_sb=m_w_branch_sb, m_w_out=m_w_out, m_ffn2_norm=m_ffn2_norm, m_ffn2_w_in=m_ffn2_w_in, m_ffn2_w_out=m_ffn2_w_out, m_ple_norm=m_ple_norm, m_w_ple_gate=m_w_ple_gate, m_w_ple_proj=m_w_ple_proj, v_ffn1_norm=v_ffn1_norm, v_ffn1_w_in=v_ffn1_w_in, v_ffn1_w_out=v_ffn1_w_out, v_mix_norm=v_mix_norm, v_w_in=v_w_in, v_q_latent_norm=v_q_latent_norm, v_w_q_up=v_w_q_up, v_kv_latent_norm=v_kv_latent_norm, v_w_kv_up=v_w_kv_up, v_q_head_norm=v_q_head_norm, v_k_head_norm=v_k_head_norm, v_w_branch_mla=v_w_branch_mla, v_w_branch_sb=v_w_branch_sb, v_w_out=v_w_out, v_ffn2_norm=v_ffn2_norm, v_ffn2_w_in=v_ffn2_w_in, v_ffn2_w_out=v_ffn2_w_out, v_ple_norm=v_ple_norm, v_w_ple_gate=v_w_ple_gate, v_w_ple_proj=v_w_ple_proj)
    weights = {n: given[n] for n in TWIN_WEIGHTS}
    shared = {n: given[n] for n in SHARED_INPUTS}
    per_example = {n: given[n] for n in ['x', 'p', 'positions']}
    grad_fn = _jax.value_and_grad(_loss, argnums=(0, 1))

    def one_microbatch(ex, loss_target):
        ex = dict(ex)
        diff = ex.pop(TWIN_DIFF_INPUT)
        return grad_fn(weights, diff, {**shared, **ex}, loss_target)

    if N_MICROBATCH == 1:
        loss, (grad_w, grad_x) = one_microbatch(per_example, given["loss_target"])
    else:
        def body(carry, xs):
            loss_sum, grad_sum = carry
            l_k, (gw_k, gx_k) = one_microbatch(xs[0], xs[1])
            with _jax.named_scope("update"):
                return (loss_sum + l_k, _jax.tree.map(_jnp.add, grad_sum, gw_k)), gx_k

        init = (_jnp.zeros((), _jnp.float32), _jax.tree.map(_jnp.zeros_like, weights))
        (loss, grad_w), grad_x = _jax.lax.scan(body, init, (per_example, given["loss_target"]))
    with _jax.named_scope("update"):
        delta_w, new_m, new_v = {}, {}, {}
        for n in TWIN_WEIGHTS:
            delta_w[n], new_m[n], new_v[n] = _adamw(weights[n], grad_w[n], given["m_" + n], given["v_" + n])
    return (loss, grad_x, *[grad_w[n] for n in TWIN_WEIGHTS], *[delta_w[n] for n in TWIN_WEIGHTS],
            *[new_m[n] for n in TWIN_WEIGHTS], *[new_v[n] for n in TWIN_WEIGHTS])
```

```python
import functools
import math

import jax
import jax.numpy as jnp
from jax import lax
from jax.experimental import pallas as pl
from jax.experimental.pallas import tpu as pltpu

F32 = jnp.float32
BF16 = jnp.bfloat16

N_DEV = 8
D_MODEL = 1024
D_FF = 2816
PLE_DIM = 256
NORM_EPS = 1e-6
N_HEADS = 8
HEAD_PAD = 128
MLA_NOPE = 64
MLA_ROPE = 32
MLA_QK = 96
Q_LORA = 384
KV_LORA = 256
SB_DIM = 64
SB_WIDTH = 512
ROPE_BASE = 10000.0
IN_COLS = 4256

PROJ_W = 4608
P_CQ, P_CKV, P_KR, P_GM, P_GS, P_SBQ, P_SBK, P_SBV = 0, 384, 640, 1024, 2048, 3072, 3584, 4096

ADAM_LR, ADAM_B1, ADAM_B2, ADAM_EPS, ADAM_WD, ADAM_STEP = 0.001, 0.9, 0.999, 1e-08, 0.01, 10

VMEM_LIMIT = 52 * 1024 * 1024

WEIGHTS = ['ffn1_norm', 'ffn1_w_in', 'ffn1_w_out', 'mix_norm', 'w_in', 'q_latent_norm', 'w_q_up',
           'kv_latent_norm', 'w_kv_up', 'q_head_norm', 'k_head_norm', 'w_branch_mla', 'w_branch_sb',
           'w_out', 'ffn2_norm', 'ffn2_w_in', 'ffn2_w_out', 'ple_norm', 'w_ple_gate', 'w_ple_proj']
NORMS = ['ffn1_norm', 'mix_norm', 'q_latent_norm', 'kv_latent_norm', 'q_head_norm', 'k_head_norm',
         'ffn2_norm', 'ple_norm']
MATS = [n for n in WEIGHTS if n not in NORMS]
SMALL_ROWS = 48

NT_DIMS = (((1,), (1,)), ((), ()))
NN_DIMS = (((1,), (0,)), ((), ()))
TN_DIMS = (((0,), (0,)), ((), ()))


def _params(sem=None, vmem=VMEM_LIMIT):
    return pltpu.CompilerParams(dimension_semantics=sem, vmem_limit_bytes=vmem)


def _pick(n, cap):
    if n <= cap:
        return n
    best = None
    for t in range(128, cap + 1, 128):
        if n % t == 0:
            best = t
    assert best is not None, (n, cap)
    return best


def _dot(a, b, dims):
    return lax.dot_general(a, b, dims, preferred_element_type=F32)


def _matmul(a, b, *, mode, out_dtype, name, tm=None, tn=None, tk=None, res=None, alpha=1.0):
    if mode == 'nn':
        (m, k), (k2, n) = a.shape, b.shape
    elif mode == 'nt':
        (m, k), (n, k2) = a.shape, b.shape
    else:
        (k, m), (k2, n) = a.shape, b.shape
    assert k == k2, (name, a.shape, b.shape)
    tm = tm or _pick(m, 1024)
    tn = tn or _pick(n, 512)
    tk = tk or _pick(k, 2048)
    assert m % tm == 0 and n % tn == 0 and k % tk == 0, (name, m, n, k, tm, tn, tk)
    nk = k // tk
    dims = {'nn': NN_DIMS, 'nt': NT_DIMS, 'tn': TN_DIMS}[mode]
    has_res = res is not None

    def epilogue(acc, r_ref, o_ref):
        if alpha != 1.0:
            acc = acc * alpha
        if has_res:
            acc = r_ref[...] + acc
        o_ref[...] = acc.astype(out_dtype)

    def body(*refs):
        a_ref, b_ref = refs[0], refs[1]
        r_ref = refs[2] if has_res else None
        o_ref = refs[3] if has_res else refs[2]
        if nk == 1:
            epilogue(_dot(a_ref[...], b_ref[...], dims), r_ref, o_ref)
            return
        acc_ref = refs[-1]
        kk = pl.program_id(2)

        @pl.when(kk == 0)
        def _():
            acc_ref[...] = jnp.zeros_like(acc_ref)

        acc_ref[...] += _dot(a_ref[...], b_ref[...], dims)

        @pl.when(kk == nk - 1)
        def _():
            epilogue(acc_ref[...], r_ref, o_ref)

    if mode == 'tn':
        a_spec = pl.BlockSpec((tk, tm), lambda i, j, kk: (kk, i))
    else:
        a_spec = pl.BlockSpec((tm, tk), lambda i, j, kk: (i, kk))
    if mode == 'nt':
        b_spec = pl.BlockSpec((tn, tk), lambda i, j, kk: (j, kk))
    else:
        b_spec = pl.BlockSpec((tk, tn), lambda i, j, kk: (kk, j))
    o_spec = pl.BlockSpec((tm, tn), lambda i, j, kk: (i, j))
    in_specs = [a_spec, b_spec] + ([o_spec] if has_res else [])
    args = (a, b) + ((res,) if has_res else ())
    return pl.pallas_call(
        body, name=name, grid=(m // tm, n // tn, nk),
        in_specs=in_specs, out_specs=o_spec,
        out_shape=jax.ShapeDtypeStruct((m, n), out_dtype),
        scratch_shapes=[pltpu.VMEM((tm, tn), F32)] if nk > 1 else [],
        compiler_params=_params(("parallel", "parallel", "arbitrary")),
    )(*args)


def _row_tile(t, cap=512):
    return min(t, cap)


def _rms(x, width):
    return lax.rsqrt(jnp.sum(x * x, axis=-1, keepdims=True) * (1.0 / width) + NORM_EPS)


def _rmsnorm_fwd(x, g, name):
    t, d = x.shape
    tr = _row_tile(t)

    def body(x_ref, g_ref, o_ref):
        xv = x_ref[...]
        o_ref[...] = ((xv * _rms(xv, d)) * g_ref[...]).astype(BF16)

    return pl.pallas_call(
        body, name=name, grid=(t // tr,),
        in_specs=[pl.BlockSpec((tr, d), lambda i: (i, 0)), pl.BlockSpec((1, d), lambda i: (0, 0))],
        out_specs=pl.BlockSpec((tr, d), lambda i: (i, 0)),
        out_shape=jax.ShapeDtypeStruct((t, d), BF16),
        compiler_params=_params(("parallel",)),
    )(x, g)


def _rmsnorm_bwd(dn, x, g, dh_in, name, out_scale):
    t, d = x.shape
    tr = _row_tile(t, 256)

    def body(dn_ref, x_ref, g_ref, dhin_ref, dh_ref, dhb_ref, dg_ref):
        i = pl.program_id(0)
        xv = x_ref[...]
        dnv = dn_ref[...]
        r = _rms(xv, d)
        y = xv * r
        dy = dnv * g_ref[...]
        dx = r * (dy - y * (jnp.sum(dy * y, axis=-1, keepdims=True) * (1.0 / d)))
        dh = dhin_ref[...] + dx
        dh_ref[...] = dh
        dhb_ref[...] = (dh * out_scale).astype(BF16)
        part = jnp.sum(dnv * y, axis=0, keepdims=True)

        @pl.when(i == 0)
        def _():
            dg_ref[...] = part

        @pl.when(i > 0)
        def _():
            dg_ref[...] += part

    row = pl.BlockSpec((tr, d), lambda i: (i, 0))
    vec = pl.BlockSpec((1, d), lambda i: (0, 0))
    return pl.pallas_call(
        body, name=name, grid=(t // tr,),
        in_specs=[row, row, vec, row], out_specs=[row, row, vec],
        out_shape=[jax.ShapeDtypeStruct((t, d), F32), jax.ShapeDtypeStruct((t, d), BF16),
                   jax.ShapeDtypeStruct((1, d), F32)],
        compiler_params=_params(("arbitrary",)),
    )(dn, x, g, dh_in)


def _sigmoid(x):
    return 1.0 / (1.0 + jnp.exp(-x))


def _swiglu_fwd(ab, name):
    t = ab.shape[0]
    tr = _row_tile(t)

    def body(a_ref, b_ref, o_ref):
        a = a_ref[...].astype(F32)
        o_ref[...] = (a * _sigmoid(a) * b_ref[...].astype(F32)).astype(BF16)

    return pl.pallas_call(
        body, name=name, grid=(t // tr,),
        in_specs=[pl.BlockSpec((tr, D_FF), lambda i: (i, 0)), pl.BlockSpec((tr, D_FF), lambda i: (i, 1))],
        out_specs=pl.BlockSpec((tr, D_FF), lambda i: (i, 0)),
        out_shape=jax.ShapeDtypeStruct((t, D_FF), BF16),
        compiler_params=_params(("parallel",)),
    )(ab, ab)


def _swiglu_bwd(ab, dact, name):
    t = ab.shape[0]
    tr = _row_tile(t, 256)

    def body(ab_ref, d_ref, o_ref):
        a = ab_ref[:, :D_FF].astype(F32)
        b = ab_ref[:, D_FF:].astype(F32)
        dv = d_ref[...].astype(F32)
        s = _sigmoid(a)
        o_ref[:, :D_FF] = (dv * b * s * (1.0 + a * (1.0 - s))).astype(BF16)
        o_ref[:, D_FF:] = (dv * a * s).astype(BF16)

    return pl.pallas_call(
        body, name=name, grid=(t // tr,),
        in_specs=[pl.BlockSpec((tr, 2 * D_FF), lambda i: (i, 0)), pl.BlockSpec((tr, D_FF), lambda i: (i, 0))],
        out_specs=pl.BlockSpec((tr, 2 * D_FF), lambda i: (i, 0)),
        out_shape=jax.ShapeDtypeStruct((t, 2 * D_FF), BF16),
        compiler_params=_params(("parallel",)),
    )(ab, dact)


def _latent_fwd(proj, gq, gkv):
    t = proj.shape[0]
    tr = _row_tile(t)

    def body(p_ref, gq_ref, gkv_ref, cq_ref, ckv_ref):
        cq = p_ref[:, P_CQ:P_CQ + Q_LORA].astype(F32)
        ckv = p_ref[:, P_CKV:P_CKV + KV_LORA].astype(F32)
        cq_ref[...] = ((cq * _rms(cq, Q_LORA)) * gq_ref[...]).astype(BF16)
        ckv_ref[...] = ((ckv * _rms(ckv, KV_LORA)) * gkv_ref[...]).astype(BF16)

    return pl.pallas_call(
        body, name="latent_fwd", grid=(t // tr,),
        in_specs=[pl.BlockSpec((tr, 1024), lambda i: (i, 0)), pl.BlockSpec((1, Q_LORA), lambda i: (0, 0)),
                  pl.BlockSpec((1, KV_LORA), lambda i: (0, 0))],
        out_specs=[pl.BlockSpec((tr, Q_LORA), lambda i: (i, 0)), pl.BlockSpec((tr, KV_LORA), lambda i: (i, 0))],
        out_shape=[jax.ShapeDtypeStruct((t, Q_LORA), BF16), jax.ShapeDtypeStruct((t, KV_LORA), BF16)],
        compiler_params=_params(("parallel",)),
    )(proj, gq, gkv)


def _latent_bwd(dcqn, dckvn, proj, dkr, gq, gkv):
    t = proj.shape[0]
    tr = _row_tile(t, 256)

    def norm_bwd(dn, x, g, width):
        r = _rms(x, width)
        y = x * r
        dy = dn * g
        dx = r * (dy - y * (jnp.sum(dy * y, axis=-1, keepdims=True) * (1.0 / width)))
        return dx, jnp.sum(dn * y, axis=0, keepdims=True)

    def body(dcq_ref, dckv_ref, p_ref, dkr_ref, gq_ref, gkv_ref, o_ref, dgq_ref, dgkv_ref):
        i = pl.program_id(0)
        dcq, pq = norm_bwd(dcq_ref[...], p_ref[:, P_CQ:P_CQ + Q_LORA].astype(F32), gq_ref[...], Q_LORA)
        dckv, pkv = norm_bwd(dckv_ref[...], p_ref[:, P_CKV:P_CKV + KV_LORA].astype(F32), gkv_ref[...], KV_LORA)
        o_ref[:, P_CQ:P_CQ + Q_LORA] = dcq.astype(BF16)
        o_ref[:, P_CKV:P_CKV + KV_LORA] = dckv.astype(BF16)
        o_ref[:, P_KR:P_KR + 128] = dkr_ref[...].astype(BF16)
        o_ref[:, P_KR + 128:1024] = jnp.zeros((tr, 1024 - P_KR - 128), BF16)

        @pl.when(i == 0)
        def _():
            dgq_ref[...] = pq
            dgkv_ref[...] = pkv

        @pl.when(i > 0)
        def _():
            dgq_ref[...] += pq
            dgkv_ref[...] += pkv

    def row(w):
        return pl.BlockSpec((tr, w), lambda i: (i, 0))

    def vec(w):
        return pl.BlockSpec((1, w), lambda i: (0, 0))

    return pl.pallas_call(
        body, name="latent_bwd", grid=(t // tr,),
        in_specs=[row(Q_LORA), row(KV_LORA), row(1024), row(128), vec(Q_LORA), vec(KV_LORA)],
        out_specs=[row(1024), vec(Q_LORA), vec(KV_LORA)],
        out_shape=[jax.ShapeDtypeStruct((t, 1024), BF16), jax.ShapeDtypeStruct((1, Q_LORA), F32),
                   jax.ShapeDtypeStruct((1, KV_LORA), F32)],
        compiler_params=_params(("arbitrary",)),
    )(dcqn, dckvn, proj, dkr, gq, gkv)


def _rope(y, cosf, sin_a, sin_b):
    return y * cosf + pltpu.roll(y, 112, 1) * sin_a + pltpu.roll(y, 16, 1) * sin_b


def _rope_t(d, cosf, sin_a, sin_b):
    return d * cosf + pltpu.roll(d * sin_a, 16, 1) + pltpu.roll(d * sin_b, 112, 1)


def _headprep_fwd(qraw, kvraw, proj, cosf, sin_a, sin_b, gqh, gkh):
    t = qraw.shape[0]
    tr = _row_tile(t, 256)

    def body(q_ref, kv_ref, kr_ref, c_ref, sa_ref, sb_ref, gq_ref, gk_ref, qh_ref, kh_ref, kvb_ref):
        cv, sa, sb = c_ref[...], sa_ref[...], sb_ref[...]
        kr = kr_ref[...].astype(F32)
        lane = lax.broadcasted_iota(jnp.int32, (tr, HEAD_PAD), 1)
        for h in range(N_HEADS):
            cols = slice(h * HEAD_PAD, (h + 1) * HEAD_PAD)
            xq = q_ref[:, cols]
            yq = (xq * _rms(xq, MLA_QK)) * gq_ref[...]
            qh_ref[:, cols] = _rope(yq, cv, sa, sb).astype(BF16)
            kvh = kv_ref[:, cols]
            kvb_ref[:, cols] = kvh.astype(BF16)
            xk = jnp.where(lane < MLA_NOPE, kvh, kr)
            yk = (xk * _rms(xk, MLA_QK)) * gk_ref[...]
            kh_ref[:, cols] = _rope(yk, cv, sa, sb).astype(BF16)

    wide = pl.BlockSpec((tr, 1024), lambda i: (i, 0))
    lanes = pl.BlockSpec((tr, HEAD_PAD), lambda i: (i, 0))
    vec = pl.BlockSpec((1, HEAD_PAD), lambda i: (0, 0))
    return pl.pallas_call(
        body, name="headprep_fwd", grid=(t // tr,),
        in_specs=[wide, wide, pl.BlockSpec((tr, HEAD_PAD), lambda i: (i, P_KR // HEAD_PAD)), lanes, lanes, lanes, vec, vec],
        out_specs=[wide, wide, wide],
        out_shape=[jax.ShapeDtypeStruct((t, 1024), BF16)] * 3,
        compiler_params=_params(("parallel",)),
    )(qraw, kvraw, proj, cosf, sin_a, sin_b, gqh, gkh)


def _headprep_bwd(dqh, dkh, dvp, qraw, kvraw, proj, cosf, sin_a, sin_b, gqh, gkh):
    t = qraw.shape[0]
    tr = _row_tile(t, 256)

    def norm_bwd(dn, x, g):
        r = _rms(x, MLA_QK)
        y = x * r
        dy = dn * g
        dx = r * (dy - y * (jnp.sum(dy * y, axis=-1, keepdims=True) * (1.0 / MLA_QK)))
        return dx, jnp.sum(dn * y, axis=0, keepdims=True)

    def body(dq_ref, dk_ref, dv_ref, q_ref, kv_ref, kr_ref, c_ref, sa_ref, sb_ref, gq_ref, gk_ref,
             dqr_ref, dkvr_ref, dkr_ref, dgq_ref, dgk_ref):
        i = pl.program_id(0)
        cv, sa, sb = c_ref[...], sa_ref[...], sb_ref[...]
        kr = kr_ref[...].astype(F32)
        lane = lax.broadcasted_iota(jnp.int32, (tr, HEAD_PAD), 1)
        dkr = jnp.zeros((tr, HEAD_PAD), F32)
        pq = jnp.zeros((1, HEAD_PAD), F32)
        pk = jnp.zeros((1, HEAD_PAD), F32)
        for h in range(N_HEADS):
            cols = slice(h * HEAD_PAD, (h + 1) * HEAD_PAD)
            dxq, pqh = norm_bwd(_rope_t(dq_ref[:, cols], cv, sa, sb), q_ref[:, cols], gq_ref[...])
            dqr_ref[:, cols] = dxq.astype(BF16)
            pq = pq + pqh
            kvh = kv_ref[:, cols]
            xk = jnp.where(lane < MLA_NOPE, kvh, kr)
            dxk, pkh = norm_bwd(_rope_t(dk_ref[:, cols], cv, sa, sb), xk, gk_ref[...])
            pk = pk + pkh
            dkvr_ref[:, cols] = jnp.where(lane < MLA_NOPE, dxk, dv_ref[:, cols]).astype(BF16)
            dkr = dkr + jnp.where(lane < MLA_NOPE, 0.0, dxk)
        dkr_ref[...] = dkr

        @pl.when(i == 0)
        def _():
            dgq_ref[...] = pq
            dgk_ref[...] = pk

        @pl.when(i > 0)
        def _():
            dgq_ref[...] += pq
            dgk_ref[...] += pk

    wide = pl.BlockSpec((tr, 1024), lambda i: (i, 0))
    lanes = pl.BlockSpec((tr, HEAD_PAD), lambda i: (i, 0))
    vec = pl.BlockSpec((1, HEAD_PAD), lambda i: (0, 0))
    return pl.pallas_call(
        body, name="headprep_bwd", grid=(t // tr,),
        in_specs=[wide, wide, wide, wide, wide, pl.BlockSpec((tr, HEAD_PAD), lambda i: (i, P_KR // HEAD_PAD)),
                  lanes, lanes, lanes, vec, vec],
        out_specs=[wide, wide, lanes, vec, vec],
        out_shape=[jax.ShapeDtypeStruct((t, 1024), BF16), jax.ShapeDtypeStruct((t, 1024), BF16),
                   jax.ShapeDtypeStruct((t, HEAD_PAD), F32), jax.ShapeDtypeStruct((1, HEAD_PAD), F32),
                   jax.ShapeDtypeStruct((1, HEAD_PAD), F32)],
        compiler_params=_params(("arbitrary",)),
    )(dqh, dkh, dvp, qraw, kvraw, proj, cosf, sin_a, sin_b, gqh, gkh)


def _merge_fwd(proj, bm, bs):
    t = proj.shape[0]
    tr = _row_tile(t, 256)

    def body(gm_ref, gs_ref, bm_ref, bs_ref, o_ref):
        gm = _sigmoid(gm_ref[...].astype(F32))
        gs = _sigmoid(gs_ref[...].astype(F32))
        o_ref[...] = (gm * bm_ref[...] + gs * bs_ref[...]).astype(BF16)

    row = pl.BlockSpec((tr, 1024), lambda i: (i, 0))
    return pl.pallas_call(
        body, name="merge_fwd", grid=(t // tr,),
        in_specs=[pl.BlockSpec((tr, 1024), lambda i: (i, P_GM // 1024)),
                  pl.BlockSpec((tr, 1024), lambda i: (i, P_GS // 1024)), row, row],
        out_specs=row, out_shape=jax.ShapeDtypeStruct((t, 1024), BF16),
        compiler_params=_params(("parallel",)),
    )(proj, proj, bm, bs)


def _merge_bwd(dmerged, proj, bm, bs):
    t = proj.shape[0]
    tr = _row_tile(t, 256)

    def body(dm_ref, gm_ref, gs_ref, bm_ref, bs_ref, dbm_ref, dbs_ref, dg_ref):
        dm = dm_ref[...]
        gm = _sigmoid(gm_ref[...].astype(F32))
        gs = _sigmoid(gs_ref[...].astype(F32))
        dbm_ref[...] = (dm * gm).astype(BF16)
        dbs_ref[...] = (dm * gs).astype(BF16)
        dg_ref[:, :1024] = (dm * bm_ref[...] * gm * (1.0 - gm)).astype(BF16)
        dg_ref[:, 1024:] = (dm * bs_ref[...] * gs * (1.0 - gs)).astype(BF16)

    row = pl.BlockSpec((tr, 1024), lambda i: (i, 0))
    return pl.pallas_call(
        body, name="merge_bwd", grid=(t // tr,),
        in_specs=[row, pl.BlockSpec((tr, 1024), lambda i: (i, P_GM // 1024)),
                  pl.BlockSpec((tr, 1024), lambda i: (i, P_GS // 1024)), row, row],
        out_specs=[row, row, pl.BlockSpec((tr, 2048), lambda i: (i, 0))],
        out_shape=[jax.ShapeDtypeStruct((t, 1024), BF16), jax.ShapeDtypeStruct((t, 1024), BF16),
                   jax.ShapeDtypeStruct((t, 2048), BF16)],
        compiler_params=_params(("parallel",)),
    )(dmerged, proj, proj, bm, bs)


def _ple_loss(h3, zg, pp, tgt):
    t = h3.shape[0]
    tr = _row_tile(t, 256)

    def body(h_ref, z_ref, p_ref, t_ref, dh_ref, dz_ref, dp_ref, l_ref):
        i = pl.program_id(0)
        pg = _sigmoid(z_ref[...])
        ppv = p_ref[...]
        diff = (h_ref[...] + pg * ppv) - t_ref[...]
        dh = diff * (1.0 / D_MODEL)
        dh_ref[...] = dh
        dp_ref[...] = (dh * pg).astype(BF16)
        dz_ref[...] = (dh * ppv * pg * (1.0 - pg)).astype(BF16)
        sq = jnp.sum(diff * diff, axis=0, keepdims=True)
        part = sq[:, 0:128]
        for c in range(1, D_MODEL // 128):
            part = part + sq[:, c * 128:(c + 1) * 128]

        @pl.when(i == 0)
        def _():
            l_ref[...] = part

        @pl.when(i > 0)
        def _():
            l_ref[...] += part

    row = pl.BlockSpec((tr, 1024), lambda i: (i, 0))
    return pl.pallas_call(
        body, name="ple_loss", grid=(t // tr,),
        in_specs=[row, row, row, row],
        out_specs=[row, row, row, pl.BlockSpec((1, 128), lambda i: (0, 0))],
        out_shape=[jax.ShapeDtypeStruct((t, 1024), F32), jax.ShapeDtypeStruct((t, 1024), BF16),
                   jax.ShapeDtypeStruct((t, 1024), BF16), jax.ShapeDtypeStruct((1, 128), F32)],
        compiler_params=_params(("arbitrary",)),
    )(h3, zg, pp, tgt)


ATT_BLOCK = 256


def _split_bf16(x):
    hi = x.astype(BF16)
    return hi, (x - hi.astype(F32)).astype(BF16)


def _tri(kind):
    r = lax.broadcasted_iota(jnp.int32, (ATT_BLOCK, ATT_BLOCK), 0)
    c = lax.broadcasted_iota(jnp.int32, (ATT_BLOCK, ATT_BLOCK), 1)
    cond = {'gt': r > c, 'le': r <= c, 'lt': r < c}[kind]
    return jnp.where(cond, 1.0, 0.0).astype(BF16)


def _causal(strict):
    r = lax.broadcasted_iota(jnp.int32, (ATT_BLOCK, ATT_BLOCK), 0)
    c = lax.broadcasted_iota(jnp.int32, (ATT_BLOCK, ATT_BLOCK), 1)
    return (c < r) if strict else (c <= r)


def _rows(ref, j):
    return ref[pl.ds(pl.multiple_of(j * ATT_BLOCK, ATT_BLOCK), ATT_BLOCK), :]


def _mla_fwd(qh, kh, kvb):
    t = qh.shape[0]
    bq = ATT_BLOCK
    scale = 1.0 / math.sqrt(MLA_QK)

    def body(q_ref, k_ref, v_ref, o_ref, lse_ref):
        i = pl.program_id(1)
        q = q_ref[...]

        def step(j, carry, masked):
            m, l, acc = carry
            s = _dot(q, _rows(k_ref, j), NT_DIMS) * scale
            if masked:
                s = jnp.where(_causal(False), s, -1e30)
            m_new = jnp.maximum(m, jnp.max(s, axis=-1, keepdims=True))
            p = jnp.exp(s - m_new)
            alpha = jnp.exp(m - m_new)
            l = alpha * l + jnp.sum(p, axis=-1, keepdims=True)
            acc = alpha * acc + _dot(p.astype(BF16), _rows(v_ref, j), NN_DIMS)
            return m_new, l, acc

        init = (jnp.full((bq, 1), -1e30, F32), jnp.zeros((bq, 1), F32), jnp.zeros((bq, HEAD_PAD), F32))
        carry = lax.fori_loop(0, i, lambda j, c: step(j, c, False), init)
        m, l, acc = step(i, carry, True)
        o_ref[...] = (acc / l).astype(BF16)
        lse_ref[0] = m + jnp.log(l)

    full = pl.BlockSpec((t, HEAD_PAD), lambda h, i: (0, h))
    blk = pl.BlockSpec((bq, HEAD_PAD), lambda h, i: (i, h))
    return pl.pallas_call(
        body, name="mla_fwd", grid=(N_HEADS, t // bq),
        in_specs=[blk, full, full],
        out_specs=[blk, pl.BlockSpec((1, bq, 1), lambda h, i: (h, i, 0))],
        out_shape=[jax.ShapeDtypeStruct((t, N_HEADS * HEAD_PAD), BF16), jax.ShapeDtypeStruct((N_HEADS, t, 1), F32)],
        compiler_params=_params(("parallel", "arbitrary")),
    )(qh, kh, kvb)


def _mla_bwd(qh, kh, kvb, o, do, lse):
    t = qh.shape[0]
    bq = ATT_BLOCK
    scale = 1.0 / math.sqrt(MLA_QK)

    def body(q_ref, k_ref, v_ref, o_ref, do_ref, lse_ref, dq_ref, dk_ref, dv_ref):
        i = pl.program_id(1)

        @pl.when(i == 0)
        def _():
            dk_ref[...] = jnp.zeros_like(dk_ref)
            dv_ref[...] = jnp.zeros_like(dv_ref)

        q = q_ref[...]
        dov = do_ref[...]
        delta = jnp.sum(dov.astype(F32) * o_ref[...].astype(F32), axis=-1, keepdims=True)
        lse = lse_ref[0]

        def step(j, dq, masked):
            kb = _rows(k_ref, j)
            s = _dot(q, kb, NT_DIMS) * scale
            p = jnp.exp(s - lse)
            if masked:
                p = jnp.where(_causal(False), p, 0.0)
            dp = _dot(dov, _rows(v_ref, j), NT_DIMS)
            ds = (p * (dp - delta) * scale).astype(BF16)
            rows = pl.ds(pl.multiple_of(j * ATT_BLOCK, ATT_BLOCK), ATT_BLOCK)
            dk_ref[rows, :] += _dot(ds, q, TN_DIMS)
            dv_ref[rows, :] += _dot(p.astype(BF16), dov, TN_DIMS)
            return dq + _dot(ds, kb, NN_DIMS)

        dq = lax.fori_loop(0, i, lambda j, c: step(j, c, False), jnp.zeros((bq, HEAD_PAD), F32))
        dq_ref[...] = step(i, dq, True)

    full = pl.BlockSpec((t, HEAD_PAD), lambda h, i: (0, h))
    blk = pl.BlockSpec((bq, HEAD_PAD), lambda h, i: (i, h))
    wide = jax.ShapeDtypeStruct((t, N_HEADS * HEAD_PAD), F32)
    return pl.pallas_call(
        body, name="mla_bwd", grid=(N_HEADS, t // bq),
        in_specs=[blk, full, full, blk, blk, pl.BlockSpec((1, bq, 1), lambda h, i: (h, i, 0))],
        out_specs=[blk, full, full],
        out_shape=[wide, wide, wide],
        compiler_params=_params(("parallel", "arbitrary")),
    )(qh, kh, kvb, o, do, lse)


def _log_sigmoids(z):
    e = jnp.exp(-jnp.abs(z))
    lg = jnp.log(1.0 + e)
    return jnp.minimum(z, 0.0) - lg, jnp.minimum(-z, 0.0) - lg, e


def _sb_fwd(proj):
    t = proj.shape[0]
    bq = ATT_BLOCK
    scale = 1.0 / math.sqrt(SB_DIM)
    pairs = SB_WIDTH // HEAD_PAD

    def body(q_ref, k_ref, v_ref, o_ref, r_ref):
        i = pl.program_id(1)
        q = q_ref[...]
        lane = lax.broadcasted_iota(jnp.int32, (bq, HEAD_PAD), 1)
        upper = _tri('gt')
        outs = []
        for u in range(2):
            head = (lane >= u * SB_DIM) & (lane < (u + 1) * SB_DIM)
            qm = jnp.where(head, q, jnp.zeros_like(q))

            def step(j, carry, masked, qm=qm):
                c, acc = carry
                z = _dot(qm, _rows(k_ref, j), NT_DIMS) * scale
                ls_pos, ls_neg, _ = _log_sigmoids(z)
                if masked:
                    ls_neg = jnp.where(_causal(True), ls_neg, 0.0)
                hi, lo = _split_bf16(ls_neg)
                suffix = _dot(hi, upper, NN_DIMS) + _dot(lo, upper, NN_DIMS)
                a = jnp.exp(ls_pos + suffix + c)
                if masked:
                    a = jnp.where(_causal(True), a, 0.0)
                acc = acc + _dot(a.astype(BF16), _rows(v_ref, j), NN_DIMS)
                return c + jnp.sum(ls_neg, axis=-1, keepdims=True), acc

            carry = step(i, (jnp.zeros((bq, 1), F32), jnp.zeros((bq, HEAD_PAD), F32)), True)
            c, acc = lax.fori_loop(0, i, lambda s, cr: step(i - 1 - s, cr, False), carry)
            outs.append(acc)
            r_ref[u] = c
        o_ref[...] = jnp.where(lane < SB_DIM, outs[0], outs[1]).astype(BF16)

    def full(c0):
        return pl.BlockSpec((t, HEAD_PAD), lambda g, i: (0, c0 // HEAD_PAD + g))

    return pl.pallas_call(
        body, name="sb_fwd", grid=(pairs, t // bq),
        in_specs=[pl.BlockSpec((bq, HEAD_PAD), lambda g, i: (i, P_SBQ // HEAD_PAD + g)), full(P_SBK), full(P_SBV)],
        out_specs=[pl.BlockSpec((bq, HEAD_PAD), lambda g, i: (i, g)), pl.BlockSpec((2, bq, 1), lambda g, i: (g, i, 0))],
        out_shape=[jax.ShapeDtypeStruct((t, SB_WIDTH), BF16), jax.ShapeDtypeStruct((N_HEADS, t, 1), F32)],
        compiler_params=_params(("parallel", "arbitrary")),
    )(proj, proj, proj)


def _sb_bwd(proj, do, rtot):
    t = proj.shape[0]
    bq = ATT_BLOCK
    scale = 1.0 / math.sqrt(SB_DIM)
    pairs = SB_WIDTH // HEAD_PAD

    def body(q_ref, k_ref, v_ref, do_ref, r_ref, dq_ref, dk_ref, dv_ref):
        i = pl.program_id(1)

        @pl.when(i == 0)
        def _():
            dk_ref[...] = jnp.zeros_like(dk_ref)
            dv_ref[...] = jnp.zeros_like(dv_ref)

        q = q_ref[...]
        dov = do_ref[...]
        lane = lax.broadcasted_iota(jnp.int32, (bq, HEAD_PAD), 1)
        incl = _tri('le')
        excl = _tri('lt')
        dqs = []
        for u in range(2):
            head = (lane >= u * SB_DIM) & (lane < (u + 1) * SB_DIM)
            qm = jnp.where(head, q, jnp.zeros_like(q))
            dom = jnp.where(head, dov, jnp.zeros_like(dov))
            rt = r_ref[u]

            def step(j, carry, masked, qm=qm, dom=dom, rt=rt):
                c_l, c_e, dq = carry
                kb = _rows(k_ref, j)
                z = _dot(qm, kb, NT_DIMS) * scale
                ls_pos, ls_neg, e = _log_sigmoids(z)
                if masked:
                    ls_neg = jnp.where(_causal(True), ls_neg, 0.0)
                hi, lo = _split_bf16(ls_neg)
                prefix = _dot(hi, incl, NN_DIMS) + _dot(lo, incl, NN_DIMS)
                a = jnp.exp(ls_pos + (rt - (c_l + prefix)))
                if masked:
                    a = jnp.where(_causal(True), a, 0.0)
                ev = a * _dot(dom, _rows(v_ref, j), NT_DIMS)
                ehi, elo = _split_bf16(ev)
                before = c_e + _dot(ehi, excl, NN_DIMS) + _dot(elo, excl, NN_DIMS)
                sig = jnp.where(z >= 0.0, 1.0, e) / (1.0 + e)
                dz = (ev * (1.0 - sig) - sig * before) * scale
                if masked:
                    dz = jnp.where(_causal(True), dz, 0.0)
                dzb = dz.astype(BF16)
                rows = pl.ds(pl.multiple_of(j * ATT_BLOCK, ATT_BLOCK), ATT_BLOCK)
                dk_ref[rows, :] += _dot(dzb, qm, TN_DIMS)
                dv_ref[rows, :] += _dot(a.astype(BF16), dom, TN_DIMS)
                return (c_l + jnp.sum(ls_neg, axis=-1, keepdims=True),
                        c_e + jnp.sum(ev, axis=-1, keepdims=True),
                        dq + _dot(dzb, kb, NN_DIMS))

            init = (jnp.zeros((bq, 1), F32), jnp.zeros((bq, 1), F32), jnp.zeros((bq, HEAD_PAD), F32))
            carry = lax.fori_loop(0, i, lambda j, cr: step(j, cr, False), init)
            dqs.append(step(i, carry, True)[2])
        dq_ref[...] = jnp.where(lane < SB_DIM, dqs[0], dqs[1])

    def full(c0):
        return pl.BlockSpec((t, HEAD_PAD), lambda g, i: (0, c0 // HEAD_PAD + g))

    blk = pl.BlockSpec((bq, HEAD_PAD), lambda g, i: (i, g))
    acc = pl.BlockSpec((t, HEAD_PAD), lambda g, i: (0, g))
    wide = jax.ShapeDtypeStruct((t, SB_WIDTH), F32)
    return pl.pallas_call(
        body, name="sb_bwd", grid=(pairs, t // bq),
        in_specs=[pl.BlockSpec((bq, HEAD_PAD), lambda g, i: (i, P_SBQ // HEAD_PAD + g)), full(P_SBK), full(P_SBV),
                  blk, pl.BlockSpec((2, bq, 1), lambda g, i: (g, i, 0))],
        out_specs=[blk, acc, acc],
        out_shape=[wide, wide, wide],
        compiler_params=_params(("parallel", "arbitrary")),
    )(proj, proj, proj, do, rtot)


def _cols_to_full(g):
    n, r, c = g.shape
    return jnp.transpose(g, (1, 0, 2)).reshape(r, n * c)


def _full_to_cols(w):
    r, c = w.shape
    return jnp.transpose(w.reshape(r, N_DEV, c // N_DEV), (1, 0, 2))


def _layout_weights(g):
    w = {}
    for f in ('ffn1', 'ffn2'):
        w[f + '_w_in'] = _cols_to_full(g[f + '_w_in'])
        w[f + '_w_out'] = g[f + '_w_out'].reshape(D_FF, D_MODEL)
    win = _cols_to_full(g['w_in'])
    z = lambda n: jnp.zeros((D_MODEL, n), BF16)
    w['w_in'] = jnp.concatenate(
        [win[:, 0:384], win[:, 384:640], z(64), win[:, 640:672], z(32), z(256), win[:, 2208:4256], win[:, 672:2208]], axis=1)
    w['w_q_up'] = _cols_to_full(jnp.pad(g['w_q_up'], ((0, 0), (0, 0), (0, HEAD_PAD - MLA_QK))))
    w['w_kv_up'] = _cols_to_full(g['w_kv_up'])
    bm = _cols_to_full(g['w_branch_mla']).reshape(N_HEADS, MLA_NOPE, D_MODEL)
    w['w_branch_mla'] = jnp.pad(bm, ((0, 0), (HEAD_PAD - MLA_NOPE, 0), (0, 0))).reshape(N_HEADS * HEAD_PAD, D_MODEL)
    w['w_branch_sb'] = _cols_to_full(g['w_branch_sb'])
    w['w_out'] = g['w_out'].reshape(D_MODEL, D_MODEL)
    w['w_ple_gate'] = g['w_ple_gate'].reshape(D_MODEL, D_MODEL)
    w['w_ple_proj'] = _cols_to_full(g['w_ple_proj'])
    return w


def _unlayout_grads(d):
    g = {}
    for f in ('ffn1', 'ffn2'):
        g[f + '_w_in'] = _full_to_cols(d[f + '_w_in'])
        g[f + '_w_out'] = d[f + '_w_out'].reshape(N_DEV, D_FF // N_DEV, D_MODEL)
    p = d['w_in']
    win = jnp.concatenate([p[:, 0:640], p[:, 704:736], p[:, P_SBQ:PROJ_W], p[:, P_GM:P_SBQ]], axis=1)
    g['w_in'] = _full_to_cols(win)
    g['w_q_up'] = _full_to_cols(d['w_q_up'])[:, :, :MLA_QK]
    g['w_kv_up'] = _full_to_cols(d['w_kv_up'])
    bm = d['w_branch_mla'].reshape(N_HEADS, HEAD_PAD, D_MODEL)[:, HEAD_PAD - MLA_NOPE:, :].reshape(SB_WIDTH, D_MODEL)
    g['w_branch_mla'] = _full_to_cols(bm)
    g['w_branch_sb'] = _full_to_cols(d['w_branch_sb'])
    g['w_out'] = d['w_out'].reshape(N_DEV, D_MODEL // N_DEV, D_MODEL)
    g['w_ple_gate'] = d['w_ple_gate'].reshape(N_DEV, D_MODEL // N_DEV, D_MODEL)
    g['w_ple_proj'] = _full_to_cols(d['w_ple_proj'])
    return g


def _rope_tables(positions):
    half = MLA_ROPE // 2
    inv_freq = ROPE_BASE ** (-jnp.arange(0, MLA_ROPE, 2, dtype=F32) / MLA_ROPE)
    ang = positions.astype(F32)[:, None] * inv_freq
    cos, sin = jnp.cos(ang), jnp.sin(ang)
    t = positions.shape[0]
    ones = lambda n: jnp.ones((t, n), F32)
    zeros = lambda n: jnp.zeros((t, n), F32)
    cosf = jnp.concatenate([ones(MLA_NOPE), cos, cos, ones(HEAD_PAD - MLA_QK)], axis=1)
    sin_a = jnp.concatenate([zeros(MLA_NOPE), -sin, zeros(half), zeros(HEAD_PAD - MLA_QK)], axis=1)
    sin_b = jnp.concatenate([zeros(MLA_NOPE), zeros(half), sin, zeros(HEAD_PAD - MLA_QK)], axis=1)
    return cosf, sin_a, sin_b


def _local_step(x, p, positions, tgt, norms, w):
    mm = _matmul
    cosf, sin_a, sin_b = _rope_tables(positions)
    pad_head = lambda g: jnp.pad(g, ((0, 0), (0, HEAD_PAD - MLA_QK)))
    gqh, gkh = pad_head(norms['q_head_norm']), pad_head(norms['k_head_norm'])
    pb = p.astype(BF16)

    def ffn_fwd(h, tag):
        n = _rmsnorm_fwd(h, norms[tag + '_norm'], tag + "_norm_fwd")
        ab = mm(n, w[tag + '_w_in'], mode='nn', out_dtype=BF16, name=tag + "_in_fwd")
        act = _swiglu_fwd(ab, tag + "_swiglu_fwd")
        out = mm(act, w[tag + '_w_out'], mode='nn', out_dtype=F32, name=tag + "_out_fwd", res=h, alpha=0.5, tk=1408)
        return out, (n, ab, act)

    h1, ffn1_saved = ffn_fwd(x, 'ffn1')
    u = _rmsnorm_fwd(h1, norms['mix_norm'], "mix_norm_fwd")
    proj = mm(u, w['w_in'], mode='nn', out_dtype=BF16, name="proj_fwd")
    cqn, ckvn = _latent_fwd(proj, norms['q_latent_norm'], norms['kv_latent_norm'])
    qraw = mm(cqn, w['w_q_up'], mode='nn', out_dtype=F32, name="q_up_fwd")
    kvraw = mm(ckvn, w['w_kv_up'], mode='nn', out_dtype=F32, name="kv_up_fwd")
    qh, kh, kvb = _headprep_fwd(qraw, kvraw, proj, cosf, sin_a, sin_b, gqh, gkh)
    o_mla, lse = _mla_fwd(qh, kh, kvb)
    o_sb, rtot = _sb_fwd(proj)
    bm = mm(o_mla, w['w_branch_mla'], mode='nn', out_dtype=F32, name="branch_mla_fwd")
    bs = mm(o_sb, w['w_branch_sb'], mode='nn', out_dtype=F32, name="branch_sb_fwd")
    merged = _merge_fwd(proj, bm, bs)
    h2 = mm(merged, w['w_out'], mode='nn', out_dtype=F32, name="mix_out_fwd", res=h1)
    h3, ffn2_saved = ffn_fwd(h2, 'ffn2')
    n3 = _rmsnorm_fwd(h3, norms['ple_norm'], "ple_norm_fwd")
    zg = mm(n3, w['w_ple_gate'], mode='nn', out_dtype=F32, name="ple_gate_fwd")
    pp = mm(pb, w['w_ple_proj'], mode='nn', out_dtype=F32, name="ple_proj_fwd")
    dh4, dzg, dpp, loss_lanes = _ple_loss(h3, zg, pp, tgt)

    dw, dn = {}, {}
    dw['w_ple_gate'] = mm(n3, dzg, mode='tn', out_dtype=BF16, name="ple_gate_dw")
    dw['w_ple_proj'] = mm(pb, dpp, mode='tn', out_dtype=BF16, name="ple_proj_dw")
    dn3 = mm(dzg, w['w_ple_gate'], mode='nt', out_dtype=F32, name="ple_gate_dx")
    dh3, dhb3, dn['ple_norm'] = _rmsnorm_bwd(dn3, h3, norms['ple_norm'], dh4, "ple_norm_bwd", 0.5)

    def ffn_bwd(h, dh, dhb, saved, tag, out_scale):
        n, ab, act = saved
        dw[tag + '_w_out'] = mm(act, dhb, mode='tn', out_dtype=BF16, name=tag + "_out_dw", tm=1408)
        dact = mm(dhb, w[tag + '_w_out'], mode='nt', out_dtype=BF16, name=tag + "_out_dx", tn=1408)
        dab = _swiglu_bwd(ab, dact, tag + "_swiglu_bwd")
        dw[tag + '_w_in'] = mm(n, dab, mode='tn', out_dtype=BF16, name=tag + "_in_dw", tn=1408)
        dnn = mm(dab, w[tag + '_w_in'], mode='nt', out_dtype=F32, name=tag + "_in_dx", tk=1408)
        dh_prev, dhb_prev, dn[tag + '_norm'] = _rmsnorm_bwd(dnn, h, norms[tag + '_norm'], dh, tag + "_norm_bwd", out_scale)
        return dh_prev, dhb_prev

    dh2, dhb2 = ffn_bwd(h2, dh3, dhb3, ffn2_saved, 'ffn2', 1.0)
    dw['w_out'] = mm(merged, dhb2, mode='tn', out_dtype=BF16, name="mix_out_dw")
    dmerged = mm(dhb2, w['w_out'], mode='nt', out_dtype=F32, name="mix_out_dx")
    dbm, dbs, dgates = _merge_bwd(dmerged, proj, bm, bs)
    dw['w_branch_mla'] = mm(o_mla, dbm, mode='tn', out_dtype=BF16, name="branch_mla_dw")
    dw['w_branch_sb'] = mm(o_sb, dbs, mode='tn', out_dtype=BF16, name="branch_sb_dw")
    do_mla = mm(dbm, w['w_branch_mla'], mode='nt', out_dtype=BF16, name="branch_mla_dx")
    do_sb = mm(dbs, w['w_branch_sb'], mode='nt', out_dtype=BF16, name="branch_sb_dx")
    dqh, dkh, dvp = _mla_bwd(qh, kh, kvb, o_mla, do_mla, lse)
    dsq, dsk, dsv = _sb_bwd(proj, do_sb, rtot)
    dqraw, dkvraw, dkr, dgq, dgk = _headprep_bwd(dqh, dkh, dvp, qraw, kvraw, proj, cosf, sin_a, sin_b, gqh, gkh)
    dn['q_head_norm'], dn['k_head_norm'] = dgq[:, :MLA_QK], dgk[:, :MLA_QK]
    dw['w_q_up'] = mm(cqn, dqraw, mode='tn', out_dtype=BF16, name="q_up_dw")
    dw['w_kv_up'] = mm(ckvn, dkvraw, mode='tn', out_dtype=BF16, name="kv_up_dw")
    dcqn = mm(dqraw, w['w_q_up'], mode='nt', out_dtype=F32, name="q_up_dx")
    dckvn = mm(dkvraw, w['w_kv_up'], mode='nt', out_dtype=F32, name="kv_up_dx")
    dlat, dn['q_latent_norm'], dn['kv_latent_norm'] = _latent_bwd(
        dcqn, dckvn, proj, dkr, norms['q_latent_norm'], norms['kv_latent_norm'])
    dproj = jnp.concatenate([dlat, dgates, dsq.astype(BF16), dsk.astype(BF16), dsv.astype(BF16)], axis=1)
    dw['w_in'] = mm(u, dproj, mode='tn', out_dtype=BF16, name="proj_dw", tn=1536)
    du = mm(dproj, w['w_in'], mode='nt', out_dtype=F32, name="proj_dx", tk=1536)
    dh1, dhb1, dn['mix_norm'] = _rmsnorm_bwd(du, h1, norms['mix_norm'], dh2, "mix_norm_bwd", 0.5)
    dx, _ = ffn_bwd(x, dh1, dhb1, ffn1_saved, 'ffn1', 1.0)
    return dx, loss_lanes, dw, dn


MESH = pl.DeviceIdType.MESH
HBM_SPEC = pl.BlockSpec(memory_space=pl.ANY)


def _position():
    return lax.axis_index("x"), lax.axis_index("y"), lax.axis_index("c")


def _index(px, py, pc):
    return 4 * px + 2 * py + pc


def _all_gather(shards):
    n = len(shards)

    def body(*refs):
        ins, outs = refs[:n], refs[n:2 * n]
        send_sems, recv_sems, local_sems = refs[2 * n:]
        x, y, c = _position()
        me, sibling = (x, y, c), (x, y, 1 - c)
        chips = [(1 - x, y), (x, 1 - y), (1 - x, 1 - y)]

        def copy(a, k, block, to, own=False):
            dst = outs[a].at[_index(*block)]
            return pltpu.make_async_remote_copy(
                src_ref=ins[a] if own else dst, dst_ref=dst,
                send_sem=send_sems.at[a, k], recv_sem=recv_sems.at[a, k], device_id=to, device_id_type=MESH)

        mine = [pltpu.make_async_copy(ins[a], outs[a].at[_index(*me)], local_sems.at[a]) for a in range(n)]
        for cp in mine:
            cp.start()
        first = []
        for a in range(n):
            first.append(copy(a, 0, me, sibling, own=True))
            first += [copy(a, 1 + j, me, (*chip, c), own=True) for j, chip in enumerate(chips)]
        for cp in first:
            cp.start()
        passed = []
        for j, chip in enumerate(chips):
            for a in range(n):
                copy(a, 1 + j, (*chip, c), me).wait_recv()
                fwd = copy(a, 4 + j, (*chip, c), sibling)
                fwd.start()
                passed.append(fwd)
        for a in range(n):
            copy(a, 0, sibling, me).wait_recv()
            for j, chip in enumerate(chips):
                copy(a, 4 + j, (*chip, 1 - c), me).wait_recv()
        for cp in first + passed:
            cp.wait_send()
        for cp in mine:
            cp.wait()

    return pl.pallas_call(
        body, name="weights_all_gather",
        in_specs=[HBM_SPEC] * n, out_specs=[HBM_SPEC] * n,
        out_shape=[jax.ShapeDtypeStruct((N_DEV,) + s.shape, s.dtype) for s in shards],
        scratch_shapes=[pltpu.SemaphoreType.DMA((n, 7)), pltpu.SemaphoreType.DMA((n, 7)), pltpu.SemaphoreType.DMA((n,))],
    )(*shards)


def _exchange(parts):
    n = len(parts)
    masks = [(mx, my, mc) for mx in (0, 1) for my in (0, 1) for mc in (0, 1)][1:]

    def body(*refs):
        ins, outs = refs[:n], refs[n:2 * n]
        send_sems, recv_sems, local_sems = refs[2 * n:]
        x, y, c = _position()
        me = _index(x, y, c)

        def peer_of(mask):
            mx, my, mc = mask
            return (x + mx - 2 * x * mx, y + my - 2 * y * my, c + mc - 2 * c * mc)

        def copy(a, k):
            peer = peer_of(masks[k])
            return pltpu.make_async_remote_copy(
                src_ref=ins[a].at[_index(*peer)], dst_ref=outs[a].at[me],
                send_sem=send_sems.at[a, k], recv_sem=recv_sems.at[a, k], device_id=peer, device_id_type=MESH)

        def landed(a, k):
            peer = peer_of(masks[k])
            return pltpu.make_async_remote_copy(
                src_ref=ins[a].at[me], dst_ref=outs[a].at[_index(*peer)],
                send_sem=send_sems.at[a, k], recv_sem=recv_sems.at[a, k], device_id=peer, device_id_type=MESH)

        mine = [pltpu.make_async_copy(ins[a].at[me], outs[a].at[me], local_sems.at[a]) for a in range(n)]
        for cp in mine:
            cp.start()
        sent = [copy(a, k) for k in range(7) for a in range(n)]
        for cp in sent:
            cp.start()
        for k in range(7):
            for a in range(n):
                landed(a, k).wait_recv()
        for cp in sent:
            cp.wait_send()
        for cp in mine:
            cp.wait()

    return pl.pallas_call(
        body, name="grads_exchange",
        in_specs=[HBM_SPEC] * n, out_specs=[HBM_SPEC] * n,
        out_shape=[jax.ShapeDtypeStruct(s.shape, s.dtype) for s in parts],
        scratch_shapes=[pltpu.SemaphoreType.DMA((n, 7)), pltpu.SemaphoreType.DMA((n, 7)), pltpu.SemaphoreType.DMA((n,))],
    )(*parts)


def _adamw(parts, w, m, v, name):
    r, c = w.shape
    tr = r if r <= 512 else 256
    assert r % tr == 0
    bc1 = 1.0 - ADAM_B1 ** ADAM_STEP
    bc2 = 1.0 - ADAM_B2 ** ADAM_STEP

    def body(p_ref, w_ref, m_ref, v_ref, g_ref, d_ref, nm_ref, nv_ref):
        g = p_ref[0].astype(F32)
        for s in range(1, N_DEV):
            g = g + p_ref[s].astype(F32)
        nm = ADAM_B1 * m_ref[...] + (1.0 - ADAM_B1) * g
        nv = ADAM_B2 * v_ref[...] + (1.0 - ADAM_B2) * (g * g)
        g_ref[...] = g
        nm_ref[...] = nm
        nv_ref[...] = nv
        d_ref[...] = -ADAM_LR * ((nm / bc1) / (jnp.sqrt(nv / bc2) + ADAM_EPS) + ADAM_WD * w_ref[...])

    row = pl.BlockSpec((tr, c), lambda i: (i, 0))
    out = jax.ShapeDtypeStruct((r, c), F32)
    return pl.pallas_call(
        body, name=name, grid=(r // tr,),
        in_specs=[pl.BlockSpec((N_DEV, tr, c), lambda i: (0, i, 0)), row, row, row],
        out_specs=[row] * 4, out_shape=[out] * 4,
        compiler_params=_params(("parallel",)),
    )(parts, w, m, v)


def _pack_small(vecs):
    flat = jnp.concatenate([v.reshape(-1) for v in vecs])
    return jnp.pad(flat, (0, SMALL_ROWS * 128 - flat.shape[0])).reshape(SMALL_ROWS, 128)


def _unpack_small(packed, sizes):
    flat = packed.reshape(-1)
    out, at = [], 0
    for n in sizes:
        out.append(flat[at:at + n])
        at += n
    return out


def kernel(x, p, positions, ffn1_norm, ffn1_w_in, ffn1_w_out, mix_norm, w_in, q_latent_norm, w_q_up, kv_latent_norm, w_kv_up, q_head_norm, k_head_norm, w_branch_mla, w_branch_sb, w_out, ffn2_norm, ffn2_w_in, ffn2_w_out, ple_norm, w_ple_gate, w_ple_proj, loss_target, m_ffn1_norm, m_ffn1_w_in, m_ffn1_w_out, m_mix_norm, m_w_in, m_q_latent_norm, m_w_q_up, m_kv_latent_norm, m_w_kv_up, m_q_head_norm, m_k_head_norm, m_w_branch_mla, m_w_branch_sb, m_w_out, m_ffn2_norm, m_ffn2_w_in, m_ffn2_w_out, m_ple_norm, m_w_ple_gate, m_w_ple_proj, v_ffn1_norm, v_ffn1_w_in, v_ffn1_w_out, v_mix_norm, v_w_in, v_q_latent_norm, v_w_q_up, v_kv_latent_norm, v_w_kv_up, v_q_head_norm, v_k_head_norm, v_w_branch_mla, v_w_branch_sb, v_w_out, v_ffn2_norm, v_ffn2_w_in, v_ffn2_w_out, v_ple_norm, v_w_ple_gate, v_w_ple_proj):
    given = dict(locals())
    wts = {n: given[n] for n in WEIGHTS}
    mom = {n: given['m_' + n] for n in WEIGHTS}
    var = {n: given['v_' + n] for n in WEIGHTS}

    gathered = _all_gather([wts[n][0].astype(BF16) for n in MATS])
    w = _layout_weights(dict(zip(MATS, gathered)))

    norms = {n: wts[n] for n in NORMS}
    dx, loss_lanes, dw, dn = _local_step(x[0], p[0, 0], positions[0], loss_target[0], norms, w)

    small = _pack_small([dn[n] for n in NORMS] + [0.5 / D_MODEL * jnp.sum(loss_lanes)[None]])
    shard_major = _unlayout_grads(dw)
    received = _exchange([shard_major[n] for n in MATS] + [jnp.broadcast_to(small[None], (N_DEV, SMALL_ROWS, 128))])

    out = {}
    for n, parts in zip(MATS, received[:-1]):
        res = _adamw(parts, wts[n][0], mom[n][0], var[n][0], "adamw_" + n)
        out[n] = [r[None] for r in res]
    sizes = [wts[n].shape[1] for n in NORMS]
    pack = lambda d: _pack_small([d[n] for n in NORMS])
    small_res = _adamw(received[-1], pack(wts), pack(mom), pack(var), "adamw_norms")
    loss = small_res[0].reshape(-1)[sum(sizes)]
    for i, res in enumerate(small_res):
        for n, vec in zip(NORMS, _unpack_small(res, sizes)):
            out.setdefault(n, [None] * 4)[i] = vec[None]

    return (loss, dx[None], *[out[n][0] for n in WEIGHTS], *[out[n][1] for n in WEIGHTS],
            *[out[n][2] for n in WEIGHTS], *[out[n][3] for n in WEIGHTS])
```

```python
import functools
import math

import jax
import jax.numpy as jnp
from jax import lax
from jax.experimental import pallas as pl
from jax.experimental.pallas import tpu as pltpu

F32 = jnp.float32
BF16 = jnp.bfloat16

N_DEV = 8
D_MODEL = 1024
D_FF = 2816
PLE_DIM = 256
NORM_EPS = 1e-6
N_HEADS = 8
HEAD_PAD = 128
MLA_NOPE = 64
MLA_ROPE = 32
MLA_QK = 96
Q_LORA = 384
KV_LORA = 256
SB_DIM = 64
SB_WIDTH = 512
ROPE_BASE = 10000.0
IN_COLS = 4256

PROJ_W = 4608
P_CQ, P_CKV, P_KR, P_GM, P_GS, P_SBQ, P_SBK, P_SBV = 0, 384, 640, 1024, 2048, 3072, 3584, 4096

ADAM_LR, ADAM_B1, ADAM_B2, ADAM_EPS, ADAM_WD, ADAM_STEP = 0.001, 0.9, 0.999, 1e-08, 0.01, 10

VMEM_LIMIT = 52 * 1024 * 1024

WEIGHTS = ['ffn1_norm', 'ffn1_w_in', 'ffn1_w_out', 'mix_norm', 'w_in', 'q_latent_norm', 'w_q_up',
           'kv_latent_norm', 'w_kv_up', 'q_head_norm', 'k_head_norm', 'w_branch_mla', 'w_branch_sb',
           'w_out', 'ffn2_norm', 'ffn2_w_in', 'ffn2_w_out', 'ple_norm', 'w_ple_gate', 'w_ple_proj']
NORMS = ['ffn1_norm', 'mix_norm', 'q_latent_norm', 'kv_latent_norm', 'q_head_norm', 'k_head_norm',
         'ffn2_norm', 'ple_norm']
MATS = [n for n in WEIGHTS if n not in NORMS]
SMALL_ROWS = 48

NT_DIMS = (((1,), (1,)), ((), ()))
NN_DIMS = (((1,), (0,)), ((), ()))
TN_DIMS = (((0,), (0,)), ((), ()))


def _params(sem=None, vmem=VMEM_LIMIT):
    return pltpu.CompilerParams(dimension_semantics=sem, vmem_limit_bytes=vmem)


def _pick(n, cap):
    if n <= cap:
        return n
    best = None
    for t in range(128, cap + 1, 128):
        if n % t == 0:
            best = t
    assert best is not None, (n, cap)
    return best


def _dot(a, b, dims):
    return lax.dot_general(a, b, dims, preferred_element_type=F32)


def _matmul(a, b, *, mode, out_dtype, name, tm=None, tn=None, tk=None, res=None, alpha=1.0):
    if mode == 'nn':
        (m, k), (k2, n) = a.shape, b.shape
    elif mode == 'nt':
        (m, k), (n, k2) = a.shape, b.shape
    else:
        (k, m), (k2, n) = a.shape, b.shape
    assert k == k2, (name, a.shape, b.shape)
    tm = tm or _pick(m, 1024)
    tn = tn or _pick(n, 512)
    tk = tk or _pick(k, 2048)
    assert m % tm == 0 and n % tn == 0 and k % tk == 0, (name, m, n, k, tm, tn, tk)
    nk = k // tk
    dims = {'nn': NN_DIMS, 'nt': NT_DIMS, 'tn': TN_DIMS}[mode]
    has_res = res is not None

    def epilogue(acc, r_ref, o_ref):
        if alpha != 1.0:
            acc = acc * alpha
        if has_res:
            acc = r_ref[...] + acc
        o_ref[...] = acc.astype(out_dtype)

    def body(*refs):
        a_ref, b_ref = refs[0], refs[1]
        r_ref = refs[2] if has_res else None
        o_ref = refs[3] if has_res else refs[2]
        if nk == 1:
            epilogue(_dot(a_ref[...], b_ref[...], dims), r_ref, o_ref)
            return
        acc_ref = refs[-1]
        kk = pl.program_id(2)

        @pl.when(kk == 0)
        def _():
            acc_ref[...] = jnp.zeros_like(acc_ref)

        acc_ref[...] += _dot(a_ref[...], b_ref[...], dims)

        @pl.when(kk == nk - 1)
        def _():
            epilogue(acc_ref[...], r_ref, o_ref)

    if mode == 'tn':
        a_spec = pl.BlockSpec((tk, tm), lambda i, j, kk: (kk, i))
    else:
        a_spec = pl.BlockSpec((tm, tk), lambda i, j, kk: (i, kk))
    if mode == 'nt':
        b_spec = pl.BlockSpec((tn, tk), lambda i, j, kk: (j, kk))
    else:
        b_spec = pl.BlockSpec((tk, tn), lambda i, j, kk: (kk, j))
    o_spec = pl.BlockSpec((tm, tn), lambda i, j, kk: (i, j))
    in_specs = [a_spec, b_spec] + ([o_spec] if has_res else [])
    args = (a, b) + ((res,) if has_res else ())
    return pl.pallas_call(
        body, name=name, grid=(m // tm, n // tn, nk),
        in_specs=in_specs, out_specs=o_spec,
        out_shape=jax.ShapeDtypeStruct((m, n), out_dtype),
        scratch_shapes=[pltpu.VMEM((tm, tn), F32)] if nk > 1 else [],
        compiler_params=_params(("parallel", "parallel", "arbitrary")),
    )(*args)


def _row_tile(t, cap=512):
    return min(t, cap)


def _rms(x, width):
    return lax.rsqrt(jnp.sum(x * x, axis=-1, keepdims=True) * (1.0 / width) + NORM_EPS)


def _rmsnorm_fwd(x, g, name):
    t, d = x.shape
    tr = _row_tile(t)

    def body(x_ref, g_ref, o_ref):
        xv = x_ref[...]
        o_ref[...] = ((xv * _rms(xv, d)) * g_ref[...]).astype(BF16)

    return pl.pallas_call(
        body, name=name, grid=(t // tr,),
        in_specs=[pl.BlockSpec((tr, d), lambda i: (i, 0)), pl.BlockSpec((1, d), lambda i: (0, 0))],
        out_specs=pl.BlockSpec((tr, d), lambda i: (i, 0)),
        out_shape=jax.ShapeDtypeStruct((t, d), BF16),
        compiler_params=_params(("parallel",)),
    )(x, g)


def _rmsnorm_bwd(dn, x, g, dh_in, name, out_scale):
    t, d = x.shape
    tr = _row_tile(t, 256)

    def body(dn_ref, x_ref, g_ref, dhin_ref, dh_ref, dhb_ref, dg_ref):
        i = pl.program_id(0)
        xv = x_ref[...]
        dnv = dn_ref[...]
        r = _rms(xv, d)
        y = xv * r
        dy = dnv * g_ref[...]
        dx = r * (dy - y * (jnp.sum(dy * y, axis=-1, keepdims=True) * (1.0 / d)))
        dh = dhin_ref[...] + dx
        dh_ref[...] = dh
        dhb_ref[...] = (dh * out_scale).astype(BF16)
        part = jnp.sum(dnv * y, axis=0, keepdims=True)

        @pl.when(i == 0)
        def _():
            dg_ref[...] = part

        @pl.when(i > 0)
        def _():
            dg_ref[...] += part

    row = pl.BlockSpec((tr, d), lambda i: (i, 0))
    vec = pl.BlockSpec((1, d), lambda i: (0, 0))
    return pl.pallas_call(
        body, name=name, grid=(t // tr,),
        in_specs=[row, row, vec, row], out_specs=[row, row, vec],
        out_shape=[jax.ShapeDtypeStruct((t, d), F32), jax.ShapeDtypeStruct((t, d), BF16),
                   jax.ShapeDtypeStruct((1, d), F32)],
        compiler_params=_params(("arbitrary",)),
    )(dn, x, g, dh_in)


def _sigmoid(x):
    return 1.0 / (1.0 + jnp.exp(-x))


def _swiglu_fwd(ab, name):
    t = ab.shape[0]
    tr = _row_tile(t)

    def body(a_ref, b_ref, o_ref):
        a = a_ref[...].astype(F32)
        o_ref[...] = (a * _sigmoid(a) * b_ref[...].astype(F32)).astype(BF16)

    return pl.pallas_call(
        body, name=name, grid=(t // tr,),
        in_specs=[pl.BlockSpec((tr, D_FF), lambda i: (i, 0)), pl.BlockSpec((tr, D_FF), lambda i: (i, 1))],
        out_specs=pl.BlockSpec((tr, D_FF), lambda i: (i, 0)),
        out_shape=jax.ShapeDtypeStruct((t, D_FF), BF16),
        compiler_params=_params(("parallel",)),
    )(ab, ab)


def _swiglu_bwd(ab, dact, name):
    t = ab.shape[0]
    tr = _row_tile(t, 256)

    def body(ab_ref, d_ref, o_ref):
        a = ab_ref[:, :D_FF].astype(F32)
        b = ab_ref[:, D_FF:].astype(F32)
        dv = d_ref[...].astype(F32)
        s = _sigmoid(a)
        o_ref[:, :D_FF] = (dv * b * s * (1.0 + a * (1.0 - s))).astype(BF16)
        o_ref[:, D_FF:] = (dv * a * s).astype(BF16)

    return pl.pallas_call(
        body, name=name, grid=(t // tr,),
        in_specs=[pl.BlockSpec((tr, 2 * D_FF), lambda i: (i, 0)), pl.BlockSpec((tr, D_FF), lambda i: (i, 0))],
        out_specs=pl.BlockSpec((tr, 2 * D_FF), lambda i: (i, 0)),
        out_shape=jax.ShapeDtypeStruct((t, 2 * D_FF), BF16),
        compiler_params=_params(("parallel",)),
    )(ab, dact)


def _latent_fwd(proj, gq, gkv):
    t = proj.shape[0]
    tr = _row_tile(t)

    def body(p_ref, gq_ref, gkv_ref, cq_ref, ckv_ref):
        cq = p_ref[:, P_CQ:P_CQ + Q_LORA].astype(F32)
        ckv = p_ref[:, P_CKV:P_CKV + KV_LORA].astype(F32)
        cq_ref[...] = ((cq * _rms(cq, Q_LORA)) * gq_ref[...]).astype(BF16)
        ckv_ref[...] = ((ckv * _rms(ckv, KV_LORA)) * gkv_ref[...]).astype(BF16)

    return pl.pallas_call(
        body, name="latent_fwd", grid=(t // tr,),
        in_specs=[pl.BlockSpec((tr, 1024), lambda i: (i, 0)), pl.BlockSpec((1, Q_LORA), lambda i: (0, 0)),
                  pl.BlockSpec((1, KV_LORA), lambda i: (0, 0))],
        out_specs=[pl.BlockSpec((tr, Q_LORA), lambda i: (i, 0)), pl.BlockSpec((tr, KV_LORA), lambda i: (i, 0))],
        out_shape=[jax.ShapeDtypeStruct((t, Q_LORA), BF16), jax.ShapeDtypeStruct((t, KV_LORA), BF16)],
        compiler_params=_params(("parallel",)),
    )(proj, gq, gkv)


def _latent_bwd(dcqn, dckvn, proj, dkr, gq, gkv):
    t = proj.shape[0]
    tr = _row_tile(t, 256)

    def norm_bwd(dn, x, g, width):
        r = _rms(x, width)
        y = x * r
        dy = dn * g
        dx = r * (dy - y * (jnp.sum(dy * y, axis=-1, keepdims=True) * (1.0 / width)))
        return dx, jnp.sum(dn * y, axis=0, keepdims=True)

    def body(dcq_ref, dckv_ref, p_ref, dkr_ref, gq_ref, gkv_ref, o_ref, dgq_ref, dgkv_ref):
        i = pl.program_id(0)
        dcq, pq = norm_bwd(dcq_ref[...], p_ref[:, P_CQ:P_CQ + Q_LORA].astype(F32), gq_ref[...], Q_LORA)
        dckv, pkv = norm_bwd(dckv_ref[...], p_ref[:, P_CKV:P_CKV + KV_LORA].astype(F32), gkv_ref[...], KV_LORA)
        o_ref[:, P_CQ:P_CQ + Q_LORA] = dcq.astype(BF16)
        o_ref[:, P_CKV:P_CKV + KV_LORA] = dckv.astype(BF16)
        o_ref[:, P_KR:P_KR + 128] = dkr_ref[...].astype(BF16)
        o_ref[:, P_KR + 128:1024] = jnp.zeros((tr, 1024 - P_KR - 128), BF16)

        @pl.when(i == 0)
        def _():
            dgq_ref[...] = pq
            dgkv_ref[...] = pkv

        @pl.when(i > 0)
        def _():
            dgq_ref[...] += pq
            dgkv_ref[...] += pkv

    def row(w):
        return pl.BlockSpec((tr, w), lambda i: (i, 0))

    def vec(w):
        return pl.BlockSpec((1, w), lambda i: (0, 0))

    return pl.pallas_call(
        body, name="latent_bwd", grid=(t // tr,),
        in_specs=[row(Q_LORA), row(KV_LORA), row(1024), row(128), vec(Q_LORA), vec(KV_LORA)],
        out_specs=[row(1024), vec(Q_LORA), vec(KV_LORA)],
        out_shape=[jax.ShapeDtypeStruct((t, 1024), BF16), jax.ShapeDtypeStruct((1, Q_LORA), F32),
                   jax.ShapeDtypeStruct((1, KV_LORA), F32)],
        compiler_params=_params(("arbitrary",)),
    )(dcqn, dckvn, proj, dkr, gq, gkv)


def _rope(y, cosf, sin_a, sin_b):
    return y * cosf + pltpu.roll(y, 112, 1) * sin_a + pltpu.roll(y, 16, 1) * sin_b


def _rope_t(d, cosf, sin_a, sin_b):
    return d * cosf + pltpu.roll(d * sin_a, 16, 1) + pltpu.roll(d * sin_b, 112, 1)


def _headprep_fwd(qraw, kvraw, proj, cosf, sin_a, sin_b, gqh, gkh):
    t = qraw.shape[0]
    tr = _row_tile(t, 256)

    def body(q_ref, kv_ref, kr_ref, c_ref, sa_ref, sb_ref, gq_ref, gk_ref, qh_ref, kh_ref, kvb_ref):
        cv, sa, sb = c_ref[...], sa_ref[...], sb_ref[...]
        kr = kr_ref[...].astype(F32)
        lane = lax.broadcasted_iota(jnp.int32, (tr, HEAD_PAD), 1)
        for h in range(N_HEADS):
            cols = slice(h * HEAD_PAD, (h + 1) * HEAD_PAD)
            xq = q_ref[:, cols]
            yq = (xq * _rms(xq, MLA_QK)) * gq_ref[...]
            qh_ref[:, cols] = _rope(yq, cv, sa, sb).astype(BF16)
            kvh = kv_ref[:, cols]
            kvb_ref[:, cols] = jnp.where(lane < MLA_NOPE, 1.0, kvh).astype(BF16)
            xk = jnp.where(lane < MLA_NOPE, kvh, kr)
            yk = (xk * _rms(xk, MLA_QK)) * gk_ref[...]
            kh_ref[:, cols] = _rope(yk, cv, sa, sb).astype(BF16)

    wide = pl.BlockSpec((tr, 1024), lambda i: (i, 0))
    lanes = pl.BlockSpec((tr, HEAD_PAD), lambda i: (i, 0))
    vec = pl.BlockSpec((1, HEAD_PAD), lambda i: (0, 0))
    return pl.pallas_call(
        body, name="headprep_fwd", grid=(t // tr,),
        in_specs=[wide, wide, pl.BlockSpec((tr, HEAD_PAD), lambda i: (i, P_KR // HEAD_PAD)), lanes, lanes, lanes, vec, vec],
        out_specs=[wide, wide, wide],
        out_shape=[jax.ShapeDtypeStruct((t, 1024), BF16)] * 3,
        compiler_params=_params(("parallel",)),
    )(qraw, kvraw, proj, cosf, sin_a, sin_b, gqh, gkh)


def _headprep_bwd(dqh, dkh, dvp, qraw, kvraw, proj, cosf, sin_a, sin_b, gqh, gkh):
    t = qraw.shape[0]
    tr = _row_tile(t, 256)

    def norm_bwd(dn, x, g):
        r = _rms(x, MLA_QK)
        y = x * r
        dy = dn * g
        dx = r * (dy - y * (jnp.sum(dy * y, axis=-1, keepdims=True) * (1.0 / MLA_QK)))
        return dx, jnp.sum(dn * y, axis=0, keepdims=True)

    def body(dq_ref, dk_ref, dv_ref, q_ref, kv_ref, kr_ref, c_ref, sa_ref, sb_ref, gq_ref, gk_ref,
             dqr_ref, dkvr_ref, dkr_ref, dgq_ref, dgk_ref):
        i = pl.program_id(0)
        cv, sa, sb = c_ref[...], sa_ref[...], sb_ref[...]
        kr = kr_ref[...].astype(F32)
        lane = lax.broadcasted_iota(jnp.int32, (tr, HEAD_PAD), 1)
        dkr = jnp.zeros((tr, HEAD_PAD), F32)
        pq = jnp.zeros((1, HEAD_PAD), F32)
        pk = jnp.zeros((1, HEAD_PAD), F32)
        for h in range(N_HEADS):
            cols = slice(h * HEAD_PAD, (h + 1) * HEAD_PAD)
            dxq, pqh = norm_bwd(_rope_t(dq_ref[:, cols], cv, sa, sb), q_ref[:, cols], gq_ref[...])
            dqr_ref[:, cols] = dxq.astype(BF16)
            pq = pq + pqh
            kvh = kv_ref[:, cols]
            xk = jnp.where(lane < MLA_NOPE, kvh, kr)
            dxk, pkh = norm_bwd(_rope_t(dk_ref[:, cols], cv, sa, sb), xk, gk_ref[...])
            pk = pk + pkh
            dkvr_ref[:, cols] = jnp.where(lane < MLA_NOPE, dxk, dv_ref[:, cols]).astype(BF16)
            dkr = dkr + jnp.where(lane < MLA_NOPE, 0.0, dxk)
        dkr_ref[...] = dkr

        @pl.when(i == 0)
        def _():
            dgq_ref[...] = pq
            dgk_ref[...] = pk

        @pl.when(i > 0)
        def _():
            dgq_ref[...] += pq
            dgk_ref[...] += pk

    wide = pl.BlockSpec((tr, 1024), lambda i: (i, 0))
    lanes = pl.BlockSpec((tr, HEAD_PAD), lambda i: (i, 0))
    vec = pl.BlockSpec((1, HEAD_PAD), lambda i: (0, 0))
    return pl.pallas_call(
        body, name="headprep_bwd", grid=(t // tr,),
        in_specs=[wide, wide, wide, wide, wide, pl.BlockSpec((tr, HEAD_PAD), lambda i: (i, P_KR // HEAD_PAD)),
                  lanes, lanes, lanes, vec, vec],
        out_specs=[wide, wide, lanes, vec, vec],
        out_shape=[jax.ShapeDtypeStruct((t, 1024), BF16), jax.ShapeDtypeStruct((t, 1024), BF16),
                   jax.ShapeDtypeStruct((t, HEAD_PAD), F32), jax.ShapeDtypeStruct((1, HEAD_PAD), F32),
                   jax.ShapeDtypeStruct((1, HEAD_PAD), F32)],
        compiler_params=_params(("arbitrary",)),
    )(dqh, dkh, dvp, qraw, kvraw, proj, cosf, sin_a, sin_b, gqh, gkh)


def _merge_fwd(proj, bm, bs):
    t = proj.shape[0]
    tr = _row_tile(t, 256)

    def body(gm_ref, gs_ref, bm_ref, bs_ref, o_ref):
        gm = _sigmoid(gm_ref[...].astype(F32))
        gs = _sigmoid(gs_ref[...].astype(F32))
        o_ref[...] = (gm * bm_ref[...] + gs * bs_ref[...]).astype(BF16)

    row = pl.BlockSpec((tr, 1024), lambda i: (i, 0))
    return pl.pallas_call(
        body, name="merge_fwd", grid=(t // tr,),
        in_specs=[pl.BlockSpec((tr, 1024), lambda i: (i, P_GM // 1024)),
                  pl.BlockSpec((tr, 1024), lambda i: (i, P_GS // 1024)), row, row],
        out_specs=row, out_shape=jax.ShapeDtypeStruct((t, 1024), BF16),
        compiler_params=_params(("parallel",)),
    )(proj, proj, bm, bs)


def _merge_bwd(dmerged, proj, bm, bs):
    t = proj.shape[0]
    tr = _row_tile(t, 256)

    def body(dm_ref, gm_ref, gs_ref, bm_ref, bs_ref, dbm_ref, dbs_ref, dg_ref):
        dm = dm_ref[...]
        gm = _sigmoid(gm_ref[...].astype(F32))
        gs = _sigmoid(gs_ref[...].astype(F32))
        dbm_ref[...] = (dm * gm).astype(BF16)
        dbs_ref[...] = (dm * gs).astype(BF16)
        dg_ref[:, :1024] = (dm * bm_ref[...] * gm * (1.0 - gm)).astype(BF16)
        dg_ref[:, 1024:] = (dm * bs_ref[...] * gs * (1.0 - gs)).astype(BF16)

    row = pl.BlockSpec((tr, 1024), lambda i: (i, 0))
    return pl.pallas_call(
        body, name="merge_bwd", grid=(t // tr,),
        in_specs=[row, pl.BlockSpec((tr, 1024), lambda i: (i, P_GM // 1024)),
                  pl.BlockSpec((tr, 1024), lambda i: (i, P_GS // 1024)), row, row],
        out_specs=[row, row, pl.BlockSpec((tr, 2048), lambda i: (i, 0))],
        out_shape=[jax.ShapeDtypeStruct((t, 1024), BF16), jax.ShapeDtypeStruct((t, 1024), BF16),
                   jax.ShapeDtypeStruct((t, 2048), BF16)],
        compiler_params=_params(("parallel",)),
    )(dmerged, proj, proj, bm, bs)


def _ple_loss(h3, zg, pp, tgt):
    t = h3.shape[0]
    tr = _row_tile(t, 256)

    def body(h_ref, z_ref, p_ref, t_ref, dh_ref, dz_ref, dp_ref, l_ref):
        i = pl.program_id(0)
        pg = _sigmoid(z_ref[...])
        ppv = p_ref[...]
        diff = (h_ref[...] + pg * ppv) - t_ref[...]
        dh = diff * (1.0 / D_MODEL)
        dh_ref[...] = dh
        dp_ref[...] = (dh * pg).astype(BF16)
        dz_ref[...] = (dh * ppv * pg * (1.0 - pg)).astype(BF16)
        sq = jnp.sum(diff * diff, axis=0, keepdims=True)
        part = sq[:, 0:128]
        for c in range(1, D_MODEL // 128):
            part = part + sq[:, c * 128:(c + 1) * 128]

        @pl.when(i == 0)
        def _():
            l_ref[...] = part

        @pl.when(i > 0)
        def _():
            l_ref[...] += part

    row = pl.BlockSpec((tr, 1024), lambda i: (i, 0))
    return pl.pallas_call(
        body, name="ple_loss", grid=(t // tr,),
        in_specs=[row, row, row, row],
        out_specs=[row, row, row, pl.BlockSpec((1, 128), lambda i: (0, 0))],
        out_shape=[jax.ShapeDtypeStruct((t, 1024), F32), jax.ShapeDtypeStruct((t, 1024), BF16),
                   jax.ShapeDtypeStruct((t, 1024), BF16), jax.ShapeDtypeStruct((1, 128), F32)],
        compiler_params=_params(("arbitrary",)),
    )(h3, zg, pp, tgt)


ATT_BLOCK = 256
ATT_COLS = 2


def _split_bf16(x):
    hi = x.astype(BF16)
    return hi, (x - hi.astype(F32)).astype(BF16)


def _tri(kind):
    r = lax.broadcasted_iota(jnp.int32, (ATT_BLOCK, ATT_BLOCK), 0)
    c = lax.broadcasted_iota(jnp.int32, (ATT_BLOCK, ATT_BLOCK), 1)
    cond = {'gt': r > c, 'le': r <= c, 'lt': r < c}[kind]
    return jnp.where(cond, 1.0, 0.0).astype(BF16)


def _causal(strict):
    r = lax.broadcasted_iota(jnp.int32, (ATT_BLOCK, ATT_BLOCK), 0)
    c = lax.broadcasted_iota(jnp.int32, (ATT_BLOCK, ATT_BLOCK), 1)
    return (c < r) if strict else (c <= r)


def _lanes(c):
    return slice(c * HEAD_PAD, (c + 1) * HEAD_PAD)


def _row_block(j):
    return pl.ds(pl.multiple_of(j * ATT_BLOCK, ATT_BLOCK), ATT_BLOCK)


def _rows(ref, j, c):
    return ref[_row_block(j), _lanes(c)]


def _mla_fwd(qh, kh, kvb):
    t = qh.shape[0]
    bq = ATT_BLOCK
    scale = 1.0 / math.sqrt(MLA_QK)

    def body(q_ref, k_ref, v_ref, o_ref, lse_ref):
        i = pl.program_id(1)
        qs = [q_ref[:, _lanes(c)] for c in range(ATT_COLS)]

        def step(j, carry, masked):
            cols = range(ATT_COLS)
            scores = [_dot(qs[c], _rows(k_ref, j, c), NT_DIMS) for c in cols]
            ms, ps, alphas = [], [], []
            for c in cols:
                s = scores[c] * scale
                if masked:
                    s = jnp.where(_causal(False), s, -1e30)
                m_new = jnp.maximum(carry[c][0], jnp.max(s, axis=-1, keepdims=True))
                ps.append(jnp.exp(s - m_new).astype(BF16))
                alphas.append(jnp.exp(carry[c][0] - m_new))
                ms.append(m_new)
            return tuple((ms[c], alphas[c] * carry[c][1] + _dot(ps[c], _rows(v_ref, j, c), NN_DIMS)) for c in cols)

        init = tuple((jnp.full((bq, 1), -1e30, F32), jnp.zeros((bq, HEAD_PAD), F32)) for _ in range(ATT_COLS))
        carry = lax.fori_loop(0, i, lambda j, cr: step(j, cr, False), init)
        for c, (m, acc) in enumerate(step(i, carry, True)):
            l = acc[:, 0:1]
            o_ref[:, _lanes(c)] = (acc / l).astype(BF16)
            lse_ref[c] = m + jnp.log(l)

    width = ATT_COLS * HEAD_PAD
    full = pl.BlockSpec((t, width), lambda h, i: (0, h))
    blk = pl.BlockSpec((bq, width), lambda h, i: (i, h))
    return pl.pallas_call(
        body, name="mla_fwd", grid=(N_HEADS // ATT_COLS, t // bq),
        in_specs=[blk, full, full],
        out_specs=[blk, pl.BlockSpec((ATT_COLS, bq, 1), lambda h, i: (h, i, 0))],
        out_shape=[jax.ShapeDtypeStruct((t, N_HEADS * HEAD_PAD), BF16), jax.ShapeDtypeStruct((N_HEADS, t, 1), F32)],
        compiler_params=_params(("parallel", "arbitrary")),
    )(qh, kh, kvb)


def _mla_bwd(qh, kh, kvb, o, do, lse):
    t = qh.shape[0]
    bq = ATT_BLOCK
    scale = 1.0 / math.sqrt(MLA_QK)

    def body(q_ref, k_ref, v_ref, o_ref, do_ref, lse_ref, dq_ref, dk_ref, dv_ref):
        i = pl.program_id(1)

        @pl.when(i == 0)
        def _():
            dk_ref[...] = jnp.zeros_like(dk_ref)
            dv_ref[...] = jnp.zeros_like(dv_ref)

        qs = [q_ref[:, _lanes(c)] for c in range(ATT_COLS)]
        dos = [do_ref[:, _lanes(c)] for c in range(ATT_COLS)]
        deltas = [jnp.sum(dos[c].astype(F32) * o_ref[:, _lanes(c)].astype(F32), axis=-1, keepdims=True)
                  for c in range(ATT_COLS)]
        lses = [lse_ref[c] for c in range(ATT_COLS)]

        def step(j, dqs, masked):
            cols = range(ATT_COLS)
            kbs = [_rows(k_ref, j, c) for c in cols]
            scores = [_dot(qs[c], kbs[c], NT_DIMS) for c in cols]
            dps = [_dot(dos[c], _rows(v_ref, j, c), NT_DIMS) for c in cols]
            pbs, dss = [], []
            for c in cols:
                p = jnp.exp(scores[c] * scale - lses[c])
                if masked:
                    p = jnp.where(_causal(False), p, 0.0)
                pbs.append(p.astype(BF16))
                dss.append((p * (dps[c] - deltas[c]) * scale).astype(BF16))
            for c in cols:
                dv_ref[_row_block(j), _lanes(c)] += _dot(pbs[c], dos[c], TN_DIMS)
                dk_ref[_row_block(j), _lanes(c)] += _dot(dss[c], qs[c], TN_DIMS)
            return tuple(dqs[c] + _dot(dss[c], kbs[c], NN_DIMS) for c in cols)

        init = tuple(jnp.zeros((bq, HEAD_PAD), F32) for _ in range(ATT_COLS))
        dqs = lax.fori_loop(0, i, lambda j, cr: step(j, cr, False), init)
        for c, dq in enumerate(step(i, dqs, True)):
            dq_ref[:, _lanes(c)] = dq

    width = ATT_COLS * HEAD_PAD
    full = pl.BlockSpec((t, width), lambda h, i: (0, h))
    blk = pl.BlockSpec((bq, width), lambda h, i: (i, h))
    wide = jax.ShapeDtypeStruct((t, N_HEADS * HEAD_PAD), F32)
    return pl.pallas_call(
        body, name="mla_bwd", grid=(N_HEADS // ATT_COLS, t // bq),
        in_specs=[blk, full, full, blk, blk, pl.BlockSpec((ATT_COLS, bq, 1), lambda h, i: (h, i, 0))],
        out_specs=[blk, full, full],
        out_shape=[wide, wide, wide],
        compiler_params=_params(("parallel", "arbitrary")),
    )(qh, kh, kvb, o, do, lse)


def _head_only(x, lane, u):
    return jnp.where((lane >= u * SB_DIM) & (lane < (u + 1) * SB_DIM), x, jnp.zeros_like(x))


def _log_sigmoids(z):
    e = jnp.exp(-jnp.abs(z))
    lg = jnp.log(1.0 + e)
    ls_pos = jnp.minimum(z, 0.0) - lg
    return ls_pos, ls_pos - z, e


def _sb_fwd(proj):
    t = proj.shape[0]
    bq = ATT_BLOCK
    scale = 1.0 / math.sqrt(SB_DIM)
    pairs = SB_WIDTH // HEAD_PAD

    def body(q_ref, k_ref, v_ref, o_ref, r_ref):
        i = pl.program_id(1)
        lane = lax.broadcasted_iota(jnp.int32, (bq, HEAD_PAD), 1)
        upper = _tri('gt')
        chains = [(c, u) for c in range(ATT_COLS) for u in range(2)]
        qms = [_head_only(q_ref[:, _lanes(c)], lane, u) * scale for c, u in chains]

        def step(j, carry, masked):
            ids = range(len(chains))
            zs = [_dot(qms[n], _rows(k_ref, j, chains[n][0]), NT_DIMS) for n in ids]
            pos, neg, parts = [], [], []
            for n in ids:
                ls_pos, ls_neg, _ = _log_sigmoids(zs[n])
                if masked:
                    ls_neg = jnp.where(_causal(True), ls_neg, 0.0)
                pos.append(ls_pos)
                neg.append(ls_neg)
                parts.append(_split_bf16(ls_neg))
            suffix = [_dot(parts[n][0], upper, NN_DIMS) + _dot(parts[n][1], upper, NN_DIMS) for n in ids]
            weights = []
            for n in ids:
                a = jnp.exp(pos[n] + suffix[n] + carry[n][0])
                if masked:
                    a = jnp.where(_causal(True), a, 0.0)
                weights.append(a.astype(BF16))
            return tuple((carry[n][0] + jnp.sum(neg[n], axis=-1, keepdims=True),
                          carry[n][1] + _dot(weights[n], _rows(v_ref, j, chains[n][0]), NN_DIMS)) for n in ids)

        init = tuple((jnp.zeros((bq, 1), F32), jnp.zeros((bq, HEAD_PAD), F32)) for _ in chains)
        carry = step(i, init, True)
        carry = lax.fori_loop(0, i, lambda s, cr: step(i - 1 - s, cr, False), carry)
        for n, (c, u) in enumerate(chains):
            r_ref[2 * c + u] = carry[n][0]
        for c in range(ATT_COLS):
            o_ref[:, _lanes(c)] = jnp.where(lane < SB_DIM, carry[2 * c][1], carry[2 * c + 1][1]).astype(BF16)

    width = ATT_COLS * HEAD_PAD

    def full(c0):
        return pl.BlockSpec((t, width), lambda g, i: (0, c0 // width + g))

    return pl.pallas_call(
        body, name="sb_fwd", grid=(pairs // ATT_COLS, t // bq),
        in_specs=[pl.BlockSpec((bq, width), lambda g, i: (i, P_SBQ // width + g)), full(P_SBK), full(P_SBV)],
        out_specs=[pl.BlockSpec((bq, width), lambda g, i: (i, g)),
                   pl.BlockSpec((2 * ATT_COLS, bq, 1), lambda g, i: (g, i, 0))],
        out_shape=[jax.ShapeDtypeStruct((t, SB_WIDTH), BF16), jax.ShapeDtypeStruct((N_HEADS, t, 1), F32)],
        compiler_params=_params(("parallel", "arbitrary")),
    )(proj, proj, proj)


def _sb_bwd(proj, do, rtot):
    t = proj.shape[0]
    bq = ATT_BLOCK
    scale = 1.0 / math.sqrt(SB_DIM)
    pairs = SB_WIDTH // HEAD_PAD

    def body(q_ref, k_ref, v_ref, do_ref, r_ref, dq_ref, dk_ref, dv_ref):
        i = pl.program_id(1)

        @pl.when(i == 0)
        def _():
            dk_ref[...] = jnp.zeros_like(dk_ref)
            dv_ref[...] = jnp.zeros_like(dv_ref)

        lane = lax.broadcasted_iota(jnp.int32, (bq, HEAD_PAD), 1)
        incl = _tri('le')
        excl = _tri('lt')
        chains = [(c, u) for c in range(ATT_COLS) for u in range(2)]
        qms = [_head_only(q_ref[:, _lanes(c)], lane, u) * scale for c, u in chains]
        doms = [_head_only(do_ref[:, _lanes(c)], lane, u) for c, u in chains]
        rts = [r_ref[2 * c + u] for c, u in chains]

        def step(j, carry, masked):
            ids = range(len(chains))
            kbs = [_rows(k_ref, j, c) for c in range(ATT_COLS)]
            zs = [_dot(qms[n], kbs[chains[n][0]], NT_DIMS) for n in ids]
            das = [_dot(doms[n], _rows(v_ref, j, chains[n][0]), NT_DIMS) for n in ids]
            pos, neg, sigs, parts = [], [], [], []
            for n in ids:
                ls_pos, ls_neg, e = _log_sigmoids(zs[n])
                if masked:
                    ls_neg = jnp.where(_causal(True), ls_neg, 0.0)
                pos.append(ls_pos)
                neg.append(ls_neg)
                sigs.append(jnp.where(zs[n] >= 0.0, 1.0, e) * pl.reciprocal(1.0 + e, approx=True))
                parts.append(_split_bf16(ls_neg))
            prefix = [_dot(parts[n][0], incl, NN_DIMS) + _dot(parts[n][1], incl, NN_DIMS) for n in ids]
            evs, eparts, dvs = [], [], []
            for n in ids:
                a = jnp.exp(pos[n] + (rts[n] - (carry[n][0] + prefix[n])))
                if masked:
                    a = jnp.where(_causal(True), a, 0.0)
                dvs.append(_dot(a.astype(BF16), doms[n], TN_DIMS))
                evs.append(a * das[n])
                eparts.append(evs[n].astype(BF16))
            before = [_dot(eparts[n], excl, NN_DIMS) for n in ids]
            out, dks = [], []
            for n in ids:
                dz = evs[n] - sigs[n] * (evs[n] + (carry[n][1] + before[n]))
                if masked:
                    dz = jnp.where(_causal(True), dz, 0.0)
                dzb = dz.astype(BF16)
                dks.append(_dot(dzb, qms[n], TN_DIMS))
                out.append((carry[n][0] + jnp.sum(neg[n], axis=-1, keepdims=True),
                            carry[n][1] + jnp.sum(evs[n], axis=-1, keepdims=True),
                            carry[n][2] + _dot(dzb, kbs[chains[n][0]], NN_DIMS)))
            for c in range(ATT_COLS):
                dv_ref[_row_block(j), _lanes(c)] += dvs[2 * c] + dvs[2 * c + 1]
                dk_ref[_row_block(j), _lanes(c)] += dks[2 * c] + dks[2 * c + 1]
            return tuple(out)

        init = tuple((jnp.zeros((bq, 1), F32), jnp.zeros((bq, 1), F32), jnp.zeros((bq, HEAD_PAD), F32)) for _ in chains)
        carry = lax.fori_loop(0, i, lambda j, cr: step(j, cr, False), init)
        carry = step(i, carry, True)
        for c in range(ATT_COLS):
            dq_ref[:, _lanes(c)] = jnp.where(lane < SB_DIM, carry[2 * c][2], carry[2 * c + 1][2]) * scale

    width = ATT_COLS * HEAD_PAD

    def full(c0):
        return pl.BlockSpec((t, width), lambda g, i: (0, c0 // width + g))

    blk = pl.BlockSpec((bq, width), lambda g, i: (i, g))
    acc = pl.BlockSpec((t, width), lambda g, i: (0, g))
    wide = jax.ShapeDtypeStruct((t, SB_WIDTH), F32)
    return pl.pallas_call(
        body, name="sb_bwd", grid=(pairs // ATT_COLS, t // bq),
        in_specs=[pl.BlockSpec((bq, width), lambda g, i: (i, P_SBQ // width + g)), full(P_SBK), full(P_SBV),
                  blk, pl.BlockSpec((2 * ATT_COLS, bq, 1), lambda g, i: (g, i, 0))],
        out_specs=[blk, acc, acc],
        out_shape=[wide, wide, wide],
        compiler_params=_params(("parallel", "arbitrary")),
    )(proj, proj, proj, do, rtot)


def _cols_to_full(g):
    n, r, c = g.shape
    return jnp.transpose(g, (1, 0, 2)).reshape(r, n * c)


def _full_to_cols(w):
    r, c = w.shape
    return jnp.transpose(w.reshape(r, N_DEV, c // N_DEV), (1, 0, 2))


def _layout_weights(g):
    w = {}
    for f in ('ffn1', 'ffn2'):
        w[f + '_w_in'] = _cols_to_full(g[f + '_w_in'])
        w[f + '_w_out'] = g[f + '_w_out'].reshape(D_FF, D_MODEL)
    win = _cols_to_full(g['w_in'])
    z = lambda n: jnp.zeros((D_MODEL, n), BF16)
    w['w_in'] = jnp.concatenate(
        [win[:, 0:384], win[:, 384:640], z(64), win[:, 640:672], z(32), z(256), win[:, 2208:4256], win[:, 672:2208]], axis=1)
    w['w_q_up'] = _cols_to_full(jnp.pad(g['w_q_up'], ((0, 0), (0, 0), (0, HEAD_PAD - MLA_QK))))
    w['w_kv_up'] = _cols_to_full(g['w_kv_up'])
    bm = _cols_to_full(g['w_branch_mla']).reshape(N_HEADS, MLA_NOPE, D_MODEL)
    w['w_branch_mla'] = jnp.pad(bm, ((0, 0), (HEAD_PAD - MLA_NOPE, 0), (0, 0))).reshape(N_HEADS * HEAD_PAD, D_MODEL)
    w['w_branch_sb'] = _cols_to_full(g['w_branch_sb'])
    w['w_out'] = g['w_out'].reshape(D_MODEL, D_MODEL)
    w['w_ple_gate'] = g['w_ple_gate'].reshape(D_MODEL, D_MODEL)
    w['w_ple_proj'] = _cols_to_full(g['w_ple_proj'])
    return w


def _unlayout_grads(d):
    g = {}
    for f in ('ffn1', 'ffn2'):
        g[f + '_w_in'] = _full_to_cols(d[f + '_w_in'])
        g[f + '_w_out'] = d[f + '_w_out'].reshape(N_DEV, D_FF // N_DEV, D_MODEL)
    p = d['w_in']
    win = jnp.concatenate([p[:, 0:640], p[:, 704:736], p[:, P_SBQ:PROJ_W], p[:, P_GM:P_SBQ]], axis=1)
    g['w_in'] = _full_to_cols(win)
    g['w_q_up'] = _full_to_cols(d['w_q_up'])[:, :, :MLA_QK]
    g['w_kv_up'] = _full_to_cols(d['w_kv_up'])
    bm = d['w_branch_mla'].reshape(N_HEADS, HEAD_PAD, D_MODEL)[:, HEAD_PAD - MLA_NOPE:, :].reshape(SB_WIDTH, D_MODEL)
    g['w_branch_mla'] = _full_to_cols(bm)
    g['w_branch_sb'] = _full_to_cols(d['w_branch_sb'])
    g['w_out'] = d['w_out'].reshape(N_DEV, D_MODEL // N_DEV, D_MODEL)
    g['w_ple_gate'] = d['w_ple_gate'].reshape(N_DEV, D_MODEL // N_DEV, D_MODEL)
    g['w_ple_proj'] = _full_to_cols(d['w_ple_proj'])
    return g


def _rope_tables(positions):
    half = MLA_ROPE // 2
    inv_freq = ROPE_BASE ** (-jnp.arange(0, MLA_ROPE, 2, dtype=F32) / MLA_ROPE)
    ang = positions.astype(F32)[:, None] * inv_freq
    cos, sin = jnp.cos(ang), jnp.sin(ang)
    t = positions.shape[0]
    ones = lambda n: jnp.ones((t, n), F32)
    zeros = lambda n: jnp.zeros((t, n), F32)
    cosf = jnp.concatenate([ones(MLA_NOPE), cos, cos, ones(HEAD_PAD - MLA_QK)], axis=1)
    sin_a = jnp.concatenate([zeros(MLA_NOPE), -sin, zeros(half), zeros(HEAD_PAD - MLA_QK)], axis=1)
    sin_b = jnp.concatenate([zeros(MLA_NOPE), zeros(half), sin, zeros(HEAD_PAD - MLA_QK)], axis=1)
    return cosf, sin_a, sin_b


def _local_step(x, p, positions, tgt, norms, w):
    mm = _matmul
    cosf, sin_a, sin_b = _rope_tables(positions)
    pad_head = lambda g: jnp.pad(g, ((0, 0), (0, HEAD_PAD - MLA_QK)))
    gqh, gkh = pad_head(norms['q_head_norm']), pad_head(norms['k_head_norm'])
    pb = p.astype(BF16)

    def ffn_fwd(h, tag):
        n = _rmsnorm_fwd(h, norms[tag + '_norm'], tag + "_norm_fwd")
        ab = mm(n, w[tag + '_w_in'], mode='nn', out_dtype=BF16, name=tag + "_in_fwd")
        act = _swiglu_fwd(ab, tag + "_swiglu_fwd")
        out = mm(act, w[tag + '_w_out'], mode='nn', out_dtype=F32, name=tag + "_out_fwd", res=h, alpha=0.5, tk=1408)
        return out, (n, ab, act)

    h1, ffn1_saved = ffn_fwd(x, 'ffn1')
    u = _rmsnorm_fwd(h1, norms['mix_norm'], "mix_norm_fwd")
    proj = mm(u, w['w_in'], mode='nn', out_dtype=BF16, name="proj_fwd")
    cqn, ckvn = _latent_fwd(proj, norms['q_latent_norm'], norms['kv_latent_norm'])
    qraw = mm(cqn, w['w_q_up'], mode='nn', out_dtype=F32, name="q_up_fwd")
    kvraw = mm(ckvn, w['w_kv_up'], mode='nn', out_dtype=F32, name="kv_up_fwd")
    qh, kh, kvb = _headprep_fwd(qraw, kvraw, proj, cosf, sin_a, sin_b, gqh, gkh)
    o_mla, lse = _mla_fwd(qh, kh, kvb)
    o_sb, rtot = _sb_fwd(proj)
    bm = mm(o_mla, w['w_branch_mla'], mode='nn', out_dtype=F32, name="branch_mla_fwd")
    bs = mm(o_sb, w['w_branch_sb'], mode='nn', out_dtype=F32, name="branch_sb_fwd")
    merged = _merge_fwd(proj, bm, bs)
    h2 = mm(merged, w['w_out'], mode='nn', out_dtype=F32, name="mix_out_fwd", res=h1)
    h3, ffn2_saved = ffn_fwd(h2, 'ffn2')
    n3 = _rmsnorm_fwd(h3, norms['ple_norm'], "ple_norm_fwd")
    zg = mm(n3, w['w_ple_gate'], mode='nn', out_dtype=F32, name="ple_gate_fwd")
    pp = mm(pb, w['w_ple_proj'], mode='nn', out_dtype=F32, name="ple_proj_fwd")
    dh4, dzg, dpp, loss_lanes = _ple_loss(h3, zg, pp, tgt)

    dw, dn = {}, {}
    dw['w_ple_gate'] = mm(n3, dzg, mode='tn', out_dtype=BF16, name="ple_gate_dw")
    dw['w_ple_proj'] = mm(pb, dpp, mode='tn', out_dtype=BF16, name="ple_proj_dw")
    dn3 = mm(dzg, w['w_ple_gate'], mode='nt', out_dtype=F32, name="ple_gate_dx")
    dh3, dhb3, dn['ple_norm'] = _rmsnorm_bwd(dn3, h3, norms['ple_norm'], dh4, "ple_norm_bwd", 0.5)

    def ffn_bwd(h, dh, dhb, saved, tag, out_scale):
        n, ab, act = saved
        dw[tag + '_w_out'] = mm(act, dhb, mode='tn', out_dtype=BF16, name=tag + "_out_dw", tm=1408)
        dact = mm(dhb, w[tag + '_w_out'], mode='nt', out_dtype=BF16, name=tag + "_out_dx", tn=1408)
        dab = _swiglu_bwd(ab, dact, tag + "_swiglu_bwd")
        dw[tag + '_w_in'] = mm(n, dab, mode='tn', out_dtype=BF16, name=tag + "_in_dw", tn=1408)
        dnn = mm(dab, w[tag + '_w_in'], mode='nt', out_dtype=F32, name=tag + "_in_dx", tk=1408)
        dh_prev, dhb_prev, dn[tag + '_norm'] = _rmsnorm_bwd(dnn, h, norms[tag + '_norm'], dh, tag + "_norm_bwd", out_scale)
        return dh_prev, dhb_prev

    dh2, dhb2 = ffn_bwd(h2, dh3, dhb3, ffn2_saved, 'ffn2', 1.0)
    dw['w_out'] = mm(merged, dhb2, mode='tn', out_dtype=BF16, name="mix_out_dw")
    dmerged = mm(dhb2, w['w_out'], mode='nt', out_dtype=F32, name="mix_out_dx")
    dbm, dbs, dgates = _merge_bwd(dmerged, proj, bm, bs)
    dw['w_branch_mla'] = mm(o_mla, dbm, mode='tn', out_dtype=BF16, name="branch_mla_dw")
    dw['w_branch_sb'] = mm(o_sb, dbs, mode='tn', out_dtype=BF16, name="branch_sb_dw")
    do_mla = mm(dbm, w['w_branch_mla'], mode='nt', out_dtype=BF16, name="branch_mla_dx")
    do_sb = mm(dbs, w['w_branch_sb'], mode='nt', out_dtype=BF16, name="branch_sb_dx")
    dqh, dkh, dvp = _mla_bwd(qh, kh, kvb, o_mla, do_mla, lse)
    dsq, dsk, dsv = _sb_bwd(proj, do_sb, rtot)
    dqraw, dkvraw, dkr, dgq, dgk = _headprep_bwd(dqh, dkh, dvp, qraw, kvraw, proj, cosf, sin_a, sin_b, gqh, gkh)
    dn['q_head_norm'], dn['k_head_norm'] = dgq[:, :MLA_QK], dgk[:, :MLA_QK]
    dw['w_q_up'] = mm(cqn, dqraw, mode='tn', out_dtype=BF16, name="q_up_dw")
    dw['w_kv_up'] = mm(ckvn, dkvraw, mode='tn', out_dtype=BF16, name="kv_up_dw")
    dcqn = mm(dqraw, w['w_q_up'], mode='nt', out_dtype=F32, name="q_up_dx")
    dckvn = mm(dkvraw, w['w_kv_up'], mode='nt', out_dtype=F32, name="kv_up_dx")
    dlat, dn['q_latent_norm'], dn['kv_latent_norm'] = _latent_bwd(
        dcqn, dckvn, proj, dkr, norms['q_latent_norm'], norms['kv_latent_norm'])
    dproj = jnp.concatenate([dlat, dgates, dsq.astype(BF16), dsk.astype(BF16), dsv.astype(BF16)], axis=1)
    dw['w_in'] = mm(u, dproj, mode='tn', out_dtype=BF16, name="proj_dw", tn=1536)
    du = mm(dproj, w['w_in'], mode='nt', out_dtype=F32, name="proj_dx", tk=1536)
    dh1, dhb1, dn['mix_norm'] = _rmsnorm_bwd(du, h1, norms['mix_norm'], dh2, "mix_norm_bwd", 0.5)
    dx, _ = ffn_bwd(x, dh1, dhb1, ffn1_saved, 'ffn1', 1.0)
    return dx, loss_lanes, dw, dn


MESH = pl.DeviceIdType.MESH
HBM_SPEC = pl.BlockSpec(memory_space=pl.ANY)


def _position():
    return lax.axis_index("x"), lax.axis_index("y"), lax.axis_index("c")


def _index(px, py, pc):
    return 4 * px + 2 * py + pc


def _all_gather(shards):
    n = len(shards)

    def body(*refs):
        ins, outs = refs[:n], refs[n:2 * n]
        send_sems, recv_sems, local_sems = refs[2 * n:]
        x, y, c = _position()
        me, sibling = (x, y, c), (x, y, 1 - c)
        chips = [(1 - x, y), (x, 1 - y), (1 - x, 1 - y)]

        def copy(a, k, block, to, own=False):
            dst = outs[a].at[_index(*block)]
            return pltpu.make_async_remote_copy(
                src_ref=ins[a] if own else dst, dst_ref=dst,
                send_sem=send_sems.at[a, k], recv_sem=recv_sems.at[a, k], device_id=to, device_id_type=MESH)

        mine = [pltpu.make_async_copy(ins[a], outs[a].at[_index(*me)], local_sems.at[a]) for a in range(n)]
        for cp in mine:
            cp.start()
        first = []
        for a in range(n):
            first.append(copy(a, 0, me, sibling, own=True))
            first += [copy(a, 1 + j, me, (*chip, c), own=True) for j, chip in enumerate(chips)]
        for cp in first:
            cp.start()
        passed = []
        for j, chip in enumerate(chips):
            for a in range(n):
                copy(a, 1 + j, (*chip, c), me).wait_recv()
                fwd = copy(a, 4 + j, (*chip, c), sibling)
                fwd.start()
                passed.append(fwd)
        for a in range(n):
            copy(a, 0, sibling, me).wait_recv()
            for j, chip in enumerate(chips):
                copy(a, 4 + j, (*chip, 1 - c), me).wait_recv()
        for cp in first + passed:
            cp.wait_send()
        for cp in mine:
            cp.wait()

    return pl.pallas_call(
        body, name="weights_all_gather",
        in_specs=[HBM_SPEC] * n, out_specs=[HBM_SPEC] * n,
        out_shape=[jax.ShapeDtypeStruct((N_DEV,) + s.shape, s.dtype) for s in shards],
        scratch_shapes=[pltpu.SemaphoreType.DMA((n, 7)), pltpu.SemaphoreType.DMA((n, 7)), pltpu.SemaphoreType.DMA((n,))],
    )(*shards)


def _exchange(parts):
    n = len(parts)
    masks = [(mx, my, mc) for mx in (0, 1) for my in (0, 1) for mc in (0, 1)][1:]

    def body(*refs):
        ins, outs = refs[:n], refs[n:2 * n]
        send_sems, recv_sems, local_sems = refs[2 * n:]
        x, y, c = _position()
        me = _index(x, y, c)

        def peer_of(mask):
            mx, my, mc = mask
            return (x + mx - 2 * x * mx, y + my - 2 * y * my, c + mc - 2 * c * mc)

        def copy(a, k):
            peer = peer_of(masks[k])
            return pltpu.make_async_remote_copy(
                src_ref=ins[a].at[_index(*peer)], dst_ref=outs[a].at[me],
                send_sem=send_sems.at[a, k], recv_sem=recv_sems.at[a, k], device_id=peer, device_id_type=MESH)

        def landed(a, k):
            peer = peer_of(masks[k])
            return pltpu.make_async_remote_copy(
                src_ref=ins[a].at[me], dst_ref=outs[a].at[_index(*peer)],
                send_sem=send_sems.at[a, k], recv_sem=recv_sems.at[a, k], device_id=peer, device_id_type=MESH)

        mine = [pltpu.make_async_copy(ins[a].at[me], outs[a].at[me], local_sems.at[a]) for a in range(n)]
        for cp in mine:
            cp.start()
        sent = [copy(a, k) for k in range(7) for a in range(n)]
        for cp in sent:
            cp.start()
        for k in range(7):
            for a in range(n):
                landed(a, k).wait_recv()
        for cp in sent:
            cp.wait_send()
        for cp in mine:
            cp.wait()

    return pl.pallas_call(
        body, name="grads_exchange",
        in_specs=[HBM_SPEC] * n, out_specs=[HBM_SPEC] * n,
        out_shape=[jax.ShapeDtypeStruct(s.shape, s.dtype) for s in parts],
        scratch_shapes=[pltpu.SemaphoreType.DMA((n, 7)), pltpu.SemaphoreType.DMA((n, 7)), pltpu.SemaphoreType.DMA((n,))],
    )(*parts)


def _adamw(parts, w, m, v, name):
    r, c = w.shape
    tr = r if r <= 512 else 256
    assert r % tr == 0
    bc1 = 1.0 - ADAM_B1 ** ADAM_STEP
    bc2 = 1.0 - ADAM_B2 ** ADAM_STEP

    def body(p_ref, w_ref, m_ref, v_ref, g_ref, d_ref, nm_ref, nv_ref):
        g = p_ref[0].astype(F32)
        for s in range(1, N_DEV):
            g = g + p_ref[s].astype(F32)
        nm = ADAM_B1 * m_ref[...] + (1.0 - ADAM_B1) * g
        nv = ADAM_B2 * v_ref[...] + (1.0 - ADAM_B2) * (g * g)
        g_ref[...] = g
        nm_ref[...] = nm
        nv_ref[...] = nv
        d_ref[...] = -ADAM_LR * ((nm / bc1) / (jnp.sqrt(nv / bc2) + ADAM_EPS) + ADAM_WD * w_ref[...])

    row = pl.BlockSpec((tr, c), lambda i: (i, 0))
    out = jax.ShapeDtypeStruct((r, c), F32)
    return pl.pallas_call(
        body, name=name, grid=(r // tr,),
        in_specs=[pl.BlockSpec((N_DEV, tr, c), lambda i: (0, i, 0)), row, row, row],
        out_specs=[row] * 4, out_shape=[out] * 4,
        compiler_params=_params(("parallel",)),
    )(parts, w, m, v)


def _pack_small(vecs):
    flat = jnp.concatenate([v.reshape(-1) for v in vecs])
    return jnp.pad(flat, (0, SMALL_ROWS * 128 - flat.shape[0])).reshape(SMALL_ROWS, 128)


def _unpack_small(packed, sizes):
    flat = packed.reshape(-1)
    out, at = [], 0
    for n in sizes:
        out.append(flat[at:at + n])
        at += n
    return out


def kernel(x, p, positions, ffn1_norm, ffn1_w_in, ffn1_w_out, mix_norm, w_in, q_latent_norm, w_q_up, kv_latent_norm, w_kv_up, q_head_norm, k_head_norm, w_branch_mla, w_branch_sb, w_out, ffn2_norm, ffn2_w_in, ffn2_w_out, ple_norm, w_ple_gate, w_ple_proj, loss_target, m_ffn1_norm, m_ffn1_w_in, m_ffn1_w_out, m_mix_norm, m_w_in, m_q_latent_norm, m_w_q_up, m_kv_latent_norm, m_w_kv_up, m_q_head_norm, m_k_head_norm, m_w_branch_mla, m_w_branch_sb, m_w_out, m_ffn2_norm, m_ffn2_w_in, m_ffn2_w_out, m_ple_norm, m_w_ple_gate, m_w_ple_proj, v_ffn1_norm, v_ffn1_w_in, v_ffn1_w_out, v_mix_norm, v_w_in, v_q_latent_norm, v_w_q_up, v_kv_latent_norm, v_w_kv_up, v_q_head_norm, v_k_head_norm, v_w_branch_mla, v_w_branch_sb, v_w_out, v_ffn2_norm, v_ffn2_w_in, v_ffn2_w_out, v_ple_norm, v_w_ple_gate, v_w_ple_proj):
    given = dict(locals())
    wts = {n: given[n] for n in WEIGHTS}
    mom = {n: given['m_' + n] for n in WEIGHTS}
    var = {n: given['v_' + n] for n in WEIGHTS}

    gathered = _all_gather([wts[n][0].astype(BF16) for n in MATS])
    w = _layout_weights(dict(zip(MATS, gathered)))

    norms = {n: wts[n] for n in NORMS}
    dx, loss_lanes, dw, dn = _local_step(x[0], p[0, 0], positions[0], loss_target[0], norms, w)

    small = _pack_small([dn[n] for n in NORMS] + [0.5 / D_MODEL * jnp.sum(loss_lanes)[None]])
    shard_major = _unlayout_grads(dw)
    received = _exchange([shard_major[n] for n in MATS] + [jnp.broadcast_to(small[None], (N_DEV, SMALL_ROWS, 128))])

    out = {}
    for n, parts in zip(MATS, received[:-1]):
        res = _adamw(parts, wts[n][0], mom[n][0], var[n][0], "adamw_" + n)
        out[n] = [r[None] for r in res]
    sizes = [wts[n].shape[1] for n in NORMS]
    pack = lambda d: _pack_small([d[n] for n in NORMS])
    small_res = _adamw(received[-1], pack(wts), pack(mom), pack(var), "adamw_norms")
    loss = small_res[0].reshape(-1)[sum(sizes)]
    for i, res in enumerate(small_res):
        for n, vec in zip(NORMS, _unpack_small(res, sizes)):
            out.setdefault(n, [None] * 4)[i] = vec[None]

    return (loss, dx[None], *[out[n][0] for n in WEIGHTS], *[out[n][1] for n in WEIGHTS],
            *[out[n][2] for n in WEIGHTS], *[out[n][3] for n in WEIGHTS])
```

```python
import functools
import math

import jax
import jax.numpy as jnp
from jax import lax
from jax.experimental import pallas as pl
from jax.experimental.pallas import tpu as pltpu

F32 = jnp.float32
BF16 = jnp.bfloat16

N_DEV = 8
D_MODEL = 1024
D_FF = 2816
PLE_DIM = 256
NORM_EPS = 1e-6
N_HEADS = 8
HEAD_PAD = 128
MLA_NOPE = 64
MLA_ROPE = 32
MLA_QK = 96
Q_LORA = 384
KV_LORA = 256
SB_DIM = 64
SB_WIDTH = 512
ROPE_BASE = 10000.0
IN_COLS = 4256

PROJ_W = 4608
P_CQ, P_CKV, P_KR, P_GM, P_GS, P_SBQ, P_SBK, P_SBV = 0, 384, 640, 1024, 2048, 3072, 3584, 4096

ADAM_LR, ADAM_B1, ADAM_B2, ADAM_EPS, ADAM_WD, ADAM_STEP = 0.001, 0.9, 0.999, 1e-08, 0.01, 10

VMEM_LIMIT = 52 * 1024 * 1024

WEIGHTS = ['ffn1_norm', 'ffn1_w_in', 'ffn1_w_out', 'mix_norm', 'w_in', 'q_latent_norm', 'w_q_up',
           'kv_latent_norm', 'w_kv_up', 'q_head_norm', 'k_head_norm', 'w_branch_mla', 'w_branch_sb',
           'w_out', 'ffn2_norm', 'ffn2_w_in', 'ffn2_w_out', 'ple_norm', 'w_ple_gate', 'w_ple_proj']
NORMS = ['ffn1_norm', 'mix_norm', 'q_latent_norm', 'kv_latent_norm', 'q_head_norm', 'k_head_norm',
         'ffn2_norm', 'ple_norm']
MATS = [n for n in WEIGHTS if n not in NORMS]
SMALL_ROWS = 48

NT_DIMS = (((1,), (1,)), ((), ()))
NN_DIMS = (((1,), (0,)), ((), ()))
TN_DIMS = (((0,), (0,)), ((), ()))


def _params(sem=None, vmem=VMEM_LIMIT):
    return pltpu.CompilerParams(dimension_semantics=sem, vmem_limit_bytes=vmem)


def _pick(n, cap):
    if n <= cap:
        return n
    best = None
    for t in range(128, cap + 1, 128):
        if n % t == 0:
            best = t
    assert best is not None, (n, cap)
    return best


def _dot(a, b, dims):
    return lax.dot_general(a, b, dims, preferred_element_type=F32)


def _matmul(a, b, *, mode, out_dtype, name, tm=None, tn=None, tk=None, res=None, alpha=1.0):
    if mode == 'nn':
        (m, k), (k2, n) = a.shape, b.shape
    elif mode == 'nt':
        (m, k), (n, k2) = a.shape, b.shape
    else:
        (k, m), (k2, n) = a.shape, b.shape
    assert k == k2, (name, a.shape, b.shape)
    tm = tm or _pick(m, 1024)
    tn = tn or _pick(n, 512)
    tk = tk or _pick(k, 2048)
    assert m % tm == 0 and n % tn == 0 and k % tk == 0, (name, m, n, k, tm, tn, tk)
    nk = k // tk
    dims = {'nn': NN_DIMS, 'nt': NT_DIMS, 'tn': TN_DIMS}[mode]
    has_res = res is not None

    def epilogue(acc, r_ref, o_ref):
        if alpha != 1.0:
            acc = acc * alpha
        if has_res:
            acc = r_ref[...] + acc
        o_ref[...] = acc.astype(out_dtype)

    def body(*refs):
        a_ref, b_ref = refs[0], refs[1]
        r_ref = refs[2] if has_res else None
        o_ref = refs[3] if has_res else refs[2]
        if nk == 1:
            epilogue(_dot(a_ref[...], b_ref[...], dims), r_ref, o_ref)
            return
        acc_ref = refs[-1]
        kk = pl.program_id(2)

        @pl.when(kk == 0)
        def _():
            acc_ref[...] = jnp.zeros_like(acc_ref)

        acc_ref[...] += _dot(a_ref[...], b_ref[...], dims)

        @pl.when(kk == nk - 1)
        def _():
            epilogue(acc_ref[...], r_ref, o_ref)

    if mode == 'tn':
        a_spec = pl.BlockSpec((tk, tm), lambda i, j, kk: (kk, i))
    else:
        a_spec = pl.BlockSpec((tm, tk), lambda i, j, kk: (i, kk))
    if mode == 'nt':
        b_spec = pl.BlockSpec((tn, tk), lambda i, j, kk: (j, kk))
    else:
        b_spec = pl.BlockSpec((tk, tn), lambda i, j, kk: (kk, j))
    o_spec = pl.BlockSpec((tm, tn), lambda i, j, kk: (i, j))
    in_specs = [a_spec, b_spec] + ([o_spec] if has_res else [])
    args = (a, b) + ((res,) if has_res else ())
    return pl.pallas_call(
        body, name=name, grid=(m // tm, n // tn, nk),
        in_specs=in_specs, out_specs=o_spec,
        out_shape=jax.ShapeDtypeStruct((m, n), out_dtype),
        scratch_shapes=[pltpu.VMEM((tm, tn), F32)] if nk > 1 else [],
        compiler_params=_params(("parallel", "parallel", "arbitrary")),
    )(*args)


def _row_tile(t, cap=512):
    return min(t, cap)


def _rms(x, width):
    return lax.rsqrt(jnp.sum(x * x, axis=-1, keepdims=True) * (1.0 / width) + NORM_EPS)


def _rmsnorm_fwd(x, g, name):
    t, d = x.shape
    tr = _row_tile(t)

    def body(x_ref, g_ref, o_ref):
        xv = x_ref[...]
        o_ref[...] = ((xv * _rms(xv, d)) * g_ref[...]).astype(BF16)

    return pl.pallas_call(
        body, name=name, grid=(t // tr,),
        in_specs=[pl.BlockSpec((tr, d), lambda i: (i, 0)), pl.BlockSpec((1, d), lambda i: (0, 0))],
        out_specs=pl.BlockSpec((tr, d), lambda i: (i, 0)),
        out_shape=jax.ShapeDtypeStruct((t, d), BF16),
        compiler_params=_params(("parallel",)),
    )(x, g)


def _rmsnorm_bwd(dn, x, g, dh_in, name, out_scale):
    t, d = x.shape
    tr = _row_tile(t, 256)

    def body(dn_ref, x_ref, g_ref, dhin_ref, dh_ref, dhb_ref, dg_ref):
        i = pl.program_id(0)
        xv = x_ref[...]
        dnv = dn_ref[...]
        r = _rms(xv, d)
        y = xv * r
        dy = dnv * g_ref[...]
        dx = r * (dy - y * (jnp.sum(dy * y, axis=-1, keepdims=True) * (1.0 / d)))
        dh = dhin_ref[...] + dx
        dh_ref[...] = dh
        dhb_ref[...] = (dh * out_scale).astype(BF16)
        part = jnp.sum(dnv * y, axis=0, keepdims=True)

        @pl.when(i == 0)
        def _():
            dg_ref[...] = part

        @pl.when(i > 0)
        def _():
            dg_ref[...] += part

    row = pl.BlockSpec((tr, d), lambda i: (i, 0))
    vec = pl.BlockSpec((1, d), lambda i: (0, 0))
    return pl.pallas_call(
        body, name=name, grid=(t // tr,),
        in_specs=[row, row, vec, row], out_specs=[row, row, vec],
        out_shape=[jax.ShapeDtypeStruct((t, d), F32), jax.ShapeDtypeStruct((t, d), BF16),
                   jax.ShapeDtypeStruct((1, d), F32)],
        compiler_params=_params(("arbitrary",)),
    )(dn, x, g, dh_in)


def _sigmoid(x):
    return 1.0 / (1.0 + jnp.exp(-x))


def _swiglu_fwd(ab, name):
    t = ab.shape[0]
    tr = _row_tile(t)

    def body(a_ref, b_ref, o_ref):
        a = a_ref[...].astype(F32)
        o_ref[...] = (a * _sigmoid(a) * b_ref[...].astype(F32)).astype(BF16)

    return pl.pallas_call(
        body, name=name, grid=(t // tr,),
        in_specs=[pl.BlockSpec((tr, D_FF), lambda i: (i, 0)), pl.BlockSpec((tr, D_FF), lambda i: (i, 1))],
        out_specs=pl.BlockSpec((tr, D_FF), lambda i: (i, 0)),
        out_shape=jax.ShapeDtypeStruct((t, D_FF), BF16),
        compiler_params=_params(("parallel",)),
    )(ab, ab)


def _swiglu_bwd(ab, dact, name):
    t = ab.shape[0]
    tr = _row_tile(t, 256)

    def body(ab_ref, d_ref, o_ref):
        a = ab_ref[:, :D_FF].astype(F32)
        b = ab_ref[:, D_FF:].astype(F32)
        dv = d_ref[...].astype(F32)
        s = _sigmoid(a)
        o_ref[:, :D_FF] = (dv * b * s * (1.0 + a * (1.0 - s))).astype(BF16)
        o_ref[:, D_FF:] = (dv * a * s).astype(BF16)

    return pl.pallas_call(
        body, name=name, grid=(t // tr,),
        in_specs=[pl.BlockSpec((tr, 2 * D_FF), lambda i: (i, 0)), pl.BlockSpec((tr, D_FF), lambda i: (i, 0))],
        out_specs=pl.BlockSpec((tr, 2 * D_FF), lambda i: (i, 0)),
        out_shape=jax.ShapeDtypeStruct((t, 2 * D_FF), BF16),
        compiler_params=_params(("parallel",)),
    )(ab, dact)


def _latent_fwd(proj, gq, gkv):
    t = proj.shape[0]
    tr = _row_tile(t)

    def body(p_ref, gq_ref, gkv_ref, cq_ref, ckv_ref):
        cq = p_ref[:, P_CQ:P_CQ + Q_LORA].astype(F32)
        ckv = p_ref[:, P_CKV:P_CKV + KV_LORA].astype(F32)
        cq_ref[...] = ((cq * _rms(cq, Q_LORA)) * gq_ref[...]).astype(BF16)
        ckv_ref[...] = ((ckv * _rms(ckv, KV_LORA)) * gkv_ref[...]).astype(BF16)

    return pl.pallas_call(
        body, name="latent_fwd", grid=(t // tr,),
        in_specs=[pl.BlockSpec((tr, 1024), lambda i: (i, 0)), pl.BlockSpec((1, Q_LORA), lambda i: (0, 0)),
                  pl.BlockSpec((1, KV_LORA), lambda i: (0, 0))],
        out_specs=[pl.BlockSpec((tr, Q_LORA), lambda i: (i, 0)), pl.BlockSpec((tr, KV_LORA), lambda i: (i, 0))],
        out_shape=[jax.ShapeDtypeStruct((t, Q_LORA), BF16), jax.ShapeDtypeStruct((t, KV_LORA), BF16)],
        compiler_params=_params(("parallel",)),
    )(proj, gq, gkv)


def _latent_bwd(dcqn, dckvn, proj, dkr, gq, gkv):
    t = proj.shape[0]
    tr = _row_tile(t, 256)

    def norm_bwd(dn, x, g, width):
        r = _rms(x, width)
        y = x * r
        dy = dn * g
        dx = r * (dy - y * (jnp.sum(dy * y, axis=-1, keepdims=True) * (1.0 / width)))
        return dx, jnp.sum(dn * y, axis=0, keepdims=True)

    def body(dcq_ref, dckv_ref, p_ref, dkr_ref, gq_ref, gkv_ref, o_ref, dgq_ref, dgkv_ref):
        i = pl.program_id(0)
        dcq, pq = norm_bwd(dcq_ref[...], p_ref[:, P_CQ:P_CQ + Q_LORA].astype(F32), gq_ref[...], Q_LORA)
        dckv, pkv = norm_bwd(dckv_ref[...], p_ref[:, P_CKV:P_CKV + KV_LORA].astype(F32), gkv_ref[...], KV_LORA)
        o_ref[:, P_CQ:P_CQ + Q_LORA] = dcq.astype(BF16)
        o_ref[:, P_CKV:P_CKV + KV_LORA] = dckv.astype(BF16)
        o_ref[:, P_KR:P_KR + 128] = dkr_ref[...].astype(BF16)
        o_ref[:, P_KR + 128:1024] = jnp.zeros((tr, 1024 - P_KR - 128), BF16)

        @pl.when(i == 0)
        def _():
            dgq_ref[...] = pq
            dgkv_ref[...] = pkv

        @pl.when(i > 0)
        def _():
            dgq_ref[...] += pq
            dgkv_ref[...] += pkv

    def row(w):
        return pl.BlockSpec((tr, w), lambda i: (i, 0))

    def vec(w):
        return pl.BlockSpec((1, w), lambda i: (0, 0))

    return pl.pallas_call(
        body, name="latent_bwd", grid=(t // tr,),
        in_specs=[row(Q_LORA), row(KV_LORA), row(1024), row(128), vec(Q_LORA), vec(KV_LORA)],
        out_specs=[row(1024), vec(Q_LORA), vec(KV_LORA)],
        out_shape=[jax.ShapeDtypeStruct((t, 1024), BF16), jax.ShapeDtypeStruct((1, Q_LORA), F32),
                   jax.ShapeDtypeStruct((1, KV_LORA), F32)],
        compiler_params=_params(("arbitrary",)),
    )(dcqn, dckvn, proj, dkr, gq, gkv)


def _rope(y, cosf, sin_a, sin_b):
    return y * cosf + pltpu.roll(y, 112, 1) * sin_a + pltpu.roll(y, 16, 1) * sin_b


def _rope_t(d, cosf, sin_a, sin_b):
    return d * cosf + pltpu.roll(d * sin_a, 16, 1) + pltpu.roll(d * sin_b, 112, 1)


def _headprep_fwd(qraw, kvraw, proj, cosf, sin_a, sin_b, gqh, gkh):
    t = qraw.shape[0]
    tr = _row_tile(t, 256)

    def body(q_ref, kv_ref, kr_ref, c_ref, sa_ref, sb_ref, gq_ref, gk_ref, qh_ref, kh_ref, kvb_ref):
        cv, sa, sb = c_ref[...], sa_ref[...], sb_ref[...]
        kr = kr_ref[...].astype(F32)
        lane = lax.broadcasted_iota(jnp.int32, (tr, HEAD_PAD), 1)
        for h in range(N_HEADS):
            cols = slice(h * HEAD_PAD, (h + 1) * HEAD_PAD)
            xq = q_ref[:, cols]
            yq = (xq * _rms(xq, MLA_QK)) * gq_ref[...]
            qh_ref[:, cols] = _rope(yq, cv, sa, sb).astype(BF16)
            kvh = kv_ref[:, cols]
            kvb_ref[:, cols] = jnp.where(lane < MLA_NOPE, 1.0, kvh).astype(BF16)
            xk = jnp.where(lane < MLA_NOPE, kvh, kr)
            yk = (xk * _rms(xk, MLA_QK)) * gk_ref[...]
            kh_ref[:, cols] = _rope(yk, cv, sa, sb).astype(BF16)

    wide = pl.BlockSpec((tr, 1024), lambda i: (i, 0))
    lanes = pl.BlockSpec((tr, HEAD_PAD), lambda i: (i, 0))
    vec = pl.BlockSpec((1, HEAD_PAD), lambda i: (0, 0))
    return pl.pallas_call(
        body, name="headprep_fwd", grid=(t // tr,),
        in_specs=[wide, wide, pl.BlockSpec((tr, HEAD_PAD), lambda i: (i, P_KR // HEAD_PAD)), lanes, lanes, lanes, vec, vec],
        out_specs=[wide, wide, wide],
        out_shape=[jax.ShapeDtypeStruct((t, 1024), BF16)] * 3,
        compiler_params=_params(("parallel",)),
    )(qraw, kvraw, proj, cosf, sin_a, sin_b, gqh, gkh)


def _headprep_bwd(dqh, dkh, dvp, qraw, kvraw, proj, cosf, sin_a, sin_b, gqh, gkh):
    t = qraw.shape[0]
    tr = _row_tile(t, 256)

    def norm_bwd(dn, x, g):
        r = _rms(x, MLA_QK)
        y = x * r
        dy = dn * g
        dx = r * (dy - y * (jnp.sum(dy * y, axis=-1, keepdims=True) * (1.0 / MLA_QK)))
        return dx, jnp.sum(dn * y, axis=0, keepdims=True)

    def body(dq_ref, dk_ref, dv_ref, q_ref, kv_ref, kr_ref, c_ref, sa_ref, sb_ref, gq_ref, gk_ref,
             dqr_ref, dkvr_ref, dkr_ref, dgq_ref, dgk_ref):
        i = pl.program_id(0)
        cv, sa, sb = c_ref[...], sa_ref[...], sb_ref[...]
        kr = kr_ref[...].astype(F32)
        lane = lax.broadcasted_iota(jnp.int32, (tr, HEAD_PAD), 1)
        dkr = jnp.zeros((tr, HEAD_PAD), F32)
        pq = jnp.zeros((1, HEAD_PAD), F32)
        pk = jnp.zeros((1, HEAD_PAD), F32)
        for h in range(N_HEADS):
            cols = slice(h * HEAD_PAD, (h + 1) * HEAD_PAD)
            dxq, pqh = norm_bwd(_rope_t(dq_ref[:, cols], cv, sa, sb), q_ref[:, cols], gq_ref[...])
            dqr_ref[:, cols] = dxq.astype(BF16)
            pq = pq + pqh
            kvh = kv_ref[:, cols]
            xk = jnp.where(lane < MLA_NOPE, kvh, kr)
            dxk, pkh = norm_bwd(_rope_t(dk_ref[:, cols], cv, sa, sb), xk, gk_ref[...])
            pk = pk + pkh
            dkvr_ref[:, cols] = jnp.where(lane < MLA_NOPE, dxk, dv_ref[:, cols]).astype(BF16)
            dkr = dkr + jnp.where(lane < MLA_NOPE, 0.0, dxk)
        dkr_ref[...] = dkr

        @pl.when(i == 0)
        def _():
            dgq_ref[...] = pq
            dgk_ref[...] = pk

        @pl.when(i > 0)
        def _():
            dgq_ref[...] += pq
            dgk_ref[...] += pk

    wide = pl.BlockSpec((tr, 1024), lambda i: (i, 0))
    lanes = pl.BlockSpec((tr, HEAD_PAD), lambda i: (i, 0))
    vec = pl.BlockSpec((1, HEAD_PAD), lambda i: (0, 0))
    return pl.pallas_call(
        body, name="headprep_bwd", grid=(t // tr,),
        in_specs=[wide, wide, wide, wide, wide, pl.BlockSpec((tr, HEAD_PAD), lambda i: (i, P_KR // HEAD_PAD)),
                  lanes, lanes, lanes, vec, vec],
        out_specs=[wide, wide, lanes, vec, vec],
        out_shape=[jax.ShapeDtypeStruct((t, 1024), BF16), jax.ShapeDtypeStruct((t, 1024), BF16),
                   jax.ShapeDtypeStruct((t, HEAD_PAD), F32), jax.ShapeDtypeStruct((1, HEAD_PAD), F32),
                   jax.ShapeDtypeStruct((1, HEAD_PAD), F32)],
        compiler_params=_params(("arbitrary",)),
    )(dqh, dkh, dvp, qraw, kvraw, proj, cosf, sin_a, sin_b, gqh, gkh)


def _merge_fwd(proj, bm, bs):
    t = proj.shape[0]
    tr = _row_tile(t, 256)

    def body(gm_ref, gs_ref, bm_ref, bs_ref, o_ref):
        gm = _sigmoid(gm_ref[...].astype(F32))
        gs = _sigmoid(gs_ref[...].astype(F32))
        o_ref[...] = (gm * bm_ref[...] + gs * bs_ref[...]).astype(BF16)

    row = pl.BlockSpec((tr, 1024), lambda i: (i, 0))
    return pl.pallas_call(
        body, name="merge_fwd", grid=(t // tr,),
        in_specs=[pl.BlockSpec((tr, 1024), lambda i: (i, P_GM // 1024)),
                  pl.BlockSpec((tr, 1024), lambda i: (i, P_GS // 1024)), row, row],
        out_specs=row, out_shape=jax.ShapeDtypeStruct((t, 1024), BF16),
        compiler_params=_params(("parallel",)),
    )(proj, proj, bm, bs)


def _merge_bwd(dmerged, proj, bm, bs):
    t = proj.shape[0]
    tr = _row_tile(t, 256)

    def body(dm_ref, gm_ref, gs_ref, bm_ref, bs_ref, dbm_ref, dbs_ref, dg_ref):
        dm = dm_ref[...]
        gm = _sigmoid(gm_ref[...].astype(F32))
        gs = _sigmoid(gs_ref[...].astype(F32))
        dbm_ref[...] = (dm * gm).astype(BF16)
        dbs_ref[...] = (dm * gs).astype(BF16)
        dg_ref[:, :1024] = (dm * bm_ref[...] * gm * (1.0 - gm)).astype(BF16)
        dg_ref[:, 1024:] = (dm * bs_ref[...] * gs * (1.0 - gs)).astype(BF16)

    row = pl.BlockSpec((tr, 1024), lambda i: (i, 0))
    return pl.pallas_call(
        body, name="merge_bwd", grid=(t // tr,),
        in_specs=[row, pl.BlockSpec((tr, 1024), lambda i: (i, P_GM // 1024)),
                  pl.BlockSpec((tr, 1024), lambda i: (i, P_GS // 1024)), row, row],
        out_specs=[row, row, pl.BlockSpec((tr, 2048), lambda i: (i, 0))],
        out_shape=[jax.ShapeDtypeStruct((t, 1024), BF16), jax.ShapeDtypeStruct((t, 1024), BF16),
                   jax.ShapeDtypeStruct((t, 2048), BF16)],
        compiler_params=_params(("parallel",)),
    )(dmerged, proj, proj, bm, bs)


def _ple_loss(h3, zg, pp, tgt):
    t = h3.shape[0]
    tr = _row_tile(t, 256)

    def body(h_ref, z_ref, p_ref, t_ref, dh_ref, dz_ref, dp_ref, l_ref):
        i = pl.program_id(0)
        pg = _sigmoid(z_ref[...])
        ppv = p_ref[...]
        diff = (h_ref[...] + pg * ppv) - t_ref[...]
        dh = diff * (1.0 / D_MODEL)
        dh_ref[...] = dh
        dp_ref[...] = (dh * pg).astype(BF16)
        dz_ref[...] = (dh * ppv * pg * (1.0 - pg)).astype(BF16)
        sq = jnp.sum(diff * diff, axis=0, keepdims=True)
        part = sq[:, 0:128]
        for c in range(1, D_MODEL // 128):
            part = part + sq[:, c * 128:(c + 1) * 128]

        @pl.when(i == 0)
        def _():
            l_ref[...] = part

        @pl.when(i > 0)
        def _():
            l_ref[...] += part

    row = pl.BlockSpec((tr, 1024), lambda i: (i, 0))
    return pl.pallas_call(
        body, name="ple_loss", grid=(t // tr,),
        in_specs=[row, row, row, row],
        out_specs=[row, row, row, pl.BlockSpec((1, 128), lambda i: (0, 0))],
        out_shape=[jax.ShapeDtypeStruct((t, 1024), F32), jax.ShapeDtypeStruct((t, 1024), BF16),
                   jax.ShapeDtypeStruct((t, 1024), BF16), jax.ShapeDtypeStruct((1, 128), F32)],
        compiler_params=_params(("arbitrary",)),
    )(h3, zg, pp, tgt)


ATT_BLOCK = 256
ATT_COLS = 2


def _split_bf16(x):
    hi = x.astype(BF16)
    return hi, (x - hi.astype(F32)).astype(BF16)


def _tri(kind):
    r = lax.broadcasted_iota(jnp.int32, (ATT_BLOCK, ATT_BLOCK), 0)
    c = lax.broadcasted_iota(jnp.int32, (ATT_BLOCK, ATT_BLOCK), 1)
    cond = {'gt': r > c, 'le': r <= c, 'lt': r < c}[kind]
    return jnp.where(cond, 1.0, 0.0).astype(BF16)


def _causal(strict):
    r = lax.broadcasted_iota(jnp.int32, (ATT_BLOCK, ATT_BLOCK), 0)
    c = lax.broadcasted_iota(jnp.int32, (ATT_BLOCK, ATT_BLOCK), 1)
    return (c < r) if strict else (c <= r)


def _lanes(c):
    return slice(c * HEAD_PAD, (c + 1) * HEAD_PAD)


def _row_block(j):
    return pl.ds(pl.multiple_of(j * ATT_BLOCK, ATT_BLOCK), ATT_BLOCK)


def _rows(ref, j, c):
    return ref[_row_block(j), _lanes(c)]


def _mla_fwd(qh, kh, kvb):
    t = qh.shape[0]
    bq = ATT_BLOCK
    scale = 1.0 / math.sqrt(MLA_QK)

    def body(q_ref, k_ref, v_ref, o_ref, lse_ref):
        i = pl.program_id(1)
        qs = [q_ref[:, _lanes(c)] for c in range(ATT_COLS)]

        def step(j, carry, masked):
            cols = range(ATT_COLS)
            scores = [_dot(qs[c], _rows(k_ref, j, c), NT_DIMS) for c in cols]
            ms, ps, alphas = [], [], []
            for c in cols:
                s = scores[c] * scale
                if masked:
                    s = jnp.where(_causal(False), s, -1e30)
                m_new = jnp.maximum(carry[c][0], jnp.max(s, axis=-1, keepdims=True))
                ps.append(jnp.exp(s - m_new).astype(BF16))
                alphas.append(jnp.exp(carry[c][0] - m_new))
                ms.append(m_new)
            return tuple((ms[c], alphas[c] * carry[c][1] + _dot(ps[c], _rows(v_ref, j, c), NN_DIMS)) for c in cols)

        init = tuple((jnp.full((bq, 1), -1e30, F32), jnp.zeros((bq, HEAD_PAD), F32)) for _ in range(ATT_COLS))
        carry = lax.fori_loop(0, i, lambda j, cr: step(j, cr, False), init)
        for c, (m, acc) in enumerate(step(i, carry, True)):
            l = acc[:, 0:1]
            o_ref[:, _lanes(c)] = (acc / l).astype(BF16)
            lse_ref[c] = m + jnp.log(l)

    width = ATT_COLS * HEAD_PAD
    full = pl.BlockSpec((t, width), lambda h, i: (0, h))
    blk = pl.BlockSpec((bq, width), lambda h, i: (i, h))
    return pl.pallas_call(
        body, name="mla_fwd", grid=(N_HEADS // ATT_COLS, t // bq),
        in_specs=[blk, full, full],
        out_specs=[blk, pl.BlockSpec((ATT_COLS, bq, 1), lambda h, i: (h, i, 0))],
        out_shape=[jax.ShapeDtypeStruct((t, N_HEADS * HEAD_PAD), BF16), jax.ShapeDtypeStruct((N_HEADS, t, 1), F32)],
        compiler_params=_params(("parallel", "arbitrary")),
    )(qh, kh, kvb)


def _mla_bwd(qh, kh, kvb, o, do, lse):
    t = qh.shape[0]
    bq = ATT_BLOCK
    scale = 1.0 / math.sqrt(MLA_QK)

    def body(q_ref, k_ref, v_ref, o_ref, do_ref, lse_ref, dq_ref, dk_ref, dv_ref):
        i = pl.program_id(1)

        @pl.when(i == 0)
        def _():
            dk_ref[...] = jnp.zeros_like(dk_ref)
            dv_ref[...] = jnp.zeros_like(dv_ref)

        qs = [q_ref[:, _lanes(c)] for c in range(ATT_COLS)]
        dos = [do_ref[:, _lanes(c)] for c in range(ATT_COLS)]
        deltas = [jnp.sum(dos[c].astype(F32) * o_ref[:, _lanes(c)].astype(F32), axis=-1, keepdims=True)
                  for c in range(ATT_COLS)]
        lses = [lse_ref[c] for c in range(ATT_COLS)]

        def step(j, dqs, masked):
            cols = range(ATT_COLS)
            kbs = [_rows(k_ref, j, c) for c in cols]
            scores = [_dot(qs[c], kbs[c], NT_DIMS) for c in cols]
            dps = [_dot(dos[c], _rows(v_ref, j, c), NT_DIMS) for c in cols]
            pbs, dss = [], []
            for c in cols:
                p = jnp.exp(scores[c] * scale - lses[c])
                if masked:
                    p = jnp.where(_causal(False), p, 0.0)
                pbs.append(p.astype(BF16))
                dss.append((p * (dps[c] - deltas[c]) * scale).astype(BF16))
            for c in cols:
                dv_ref[_row_block(j), _lanes(c)] += _dot(pbs[c], dos[c], TN_DIMS)
                dk_ref[_row_block(j), _lanes(c)] += _dot(dss[c], qs[c], TN_DIMS)
            return tuple(dqs[c] + _dot(dss[c], kbs[c], NN_DIMS) for c in cols)

        init = tuple(jnp.zeros((bq, HEAD_PAD), F32) for _ in range(ATT_COLS))
        dqs = lax.fori_loop(0, i, lambda j, cr: step(j, cr, False), init)
        for c, dq in enumerate(step(i, dqs, True)):
            dq_ref[:, _lanes(c)] = dq

    width = ATT_COLS * HEAD_PAD
    full = pl.BlockSpec((t, width), lambda h, i: (0, h))
    blk = pl.BlockSpec((bq, width), lambda h, i: (i, h))
    wide = jax.ShapeDtypeStruct((t, N_HEADS * HEAD_PAD), F32)
    return pl.pallas_call(
        body, name="mla_bwd", grid=(N_HEADS // ATT_COLS, t // bq),
        in_specs=[blk, full, full, blk, blk, pl.BlockSpec((ATT_COLS, bq, 1), lambda h, i: (h, i, 0))],
        out_specs=[blk, full, full],
        out_shape=[wide, wide, wide],
        compiler_params=_params(("parallel", "arbitrary")),
    )(qh, kh, kvb, o, do, lse)


def _head_only(x, lane, u):
    return jnp.where((lane >= u * SB_DIM) & (lane < (u + 1) * SB_DIM), x, jnp.zeros_like(x))


def _log_sigmoids(z):
    e = jnp.exp(-jnp.abs(z))
    lg = jnp.log(1.0 + e)
    ls_pos = jnp.minimum(z, 0.0) - lg
    return ls_pos, ls_pos - z, e


def _sb_fwd(proj):
    t = proj.shape[0]
    bq = ATT_BLOCK
    scale = 1.0 / math.sqrt(SB_DIM)
    pairs = SB_WIDTH // HEAD_PAD

    def body(q_ref, k_ref, v_ref, o_ref, r_ref):
        i = pl.program_id(1)
        lane = lax.broadcasted_iota(jnp.int32, (bq, HEAD_PAD), 1)
        upper = _tri('gt')
        chains = [(c, u) for c in range(ATT_COLS) for u in range(2)]
        qms = [_head_only(q_ref[:, _lanes(c)], lane, u) * scale for c, u in chains]

        def step(j, carry, masked):
            ids = range(len(chains))
            zs = [_dot(qms[n], _rows(k_ref, j, chains[n][0]), NT_DIMS) for n in ids]
            pos, neg, parts = [], [], []
            for n in ids:
                ls_pos, ls_neg, _ = _log_sigmoids(zs[n])
                if masked:
                    ls_neg = jnp.where(_causal(True), ls_neg, 0.0)
                pos.append(ls_pos)
                neg.append(ls_neg)
                parts.append(_split_bf16(ls_neg))
            suffix = [_dot(parts[n][0], upper, NN_DIMS) + _dot(parts[n][1], upper, NN_DIMS) for n in ids]
            weights = []
            for n in ids:
                a = jnp.exp(pos[n] + suffix[n] + carry[n][0])
                if masked:
                    a = jnp.where(_causal(True), a, 0.0)
                weights.append(a.astype(BF16))
            return tuple((carry[n][0] + jnp.sum(neg[n], axis=-1, keepdims=True),
                          carry[n][1] + _dot(weights[n], _rows(v_ref, j, chains[n][0]), NN_DIMS)) for n in ids)

        init = tuple((jnp.zeros((bq, 1), F32), jnp.zeros((bq, HEAD_PAD), F32)) for _ in chains)
        carry = step(i, init, True)
        carry = lax.fori_loop(0, i, lambda s, cr: step(i - 1 - s, cr, False), carry)
        for n, (c, u) in enumerate(chains):
            r_ref[2 * c + u] = carry[n][0]
        for c in range(ATT_COLS):
            o_ref[:, _lanes(c)] = jnp.where(lane < SB_DIM, carry[2 * c][1], carry[2 * c + 1][1]).astype(BF16)

    width = ATT_COLS * HEAD_PAD

    def full(c0):
        return pl.BlockSpec((t, width), lambda g, i: (0, c0 // width + g))

    return pl.pallas_call(
        body, name="sb_fwd", grid=(pairs // ATT_COLS, t // bq),
        in_specs=[pl.BlockSpec((bq, width), lambda g, i: (i, P_SBQ // width + g)), full(P_SBK), full(P_SBV)],
        out_specs=[pl.BlockSpec((bq, width), lambda g, i: (i, g)),
                   pl.BlockSpec((2 * ATT_COLS, bq, 1), lambda g, i: (g, i, 0))],
        out_shape=[jax.ShapeDtypeStruct((t, SB_WIDTH), BF16), jax.ShapeDtypeStruct((N_HEADS, t, 1), F32)],
        compiler_params=_params(("parallel", "arbitrary")),
    )(proj, proj, proj)


def _sb_bwd(proj, do, rtot):
    t = proj.shape[0]
    bq = ATT_BLOCK
    scale = 1.0 / math.sqrt(SB_DIM)
    pairs = SB_WIDTH // HEAD_PAD

    def body(q_ref, k_ref, v_ref, do_ref, r_ref, dq_ref, dk_ref, dv_ref):
        i = pl.program_id(1)

        @pl.when(i == 0)
        def _():
            dk_ref[...] = jnp.zeros_like(dk_ref)
            dv_ref[...] = jnp.zeros_like(dv_ref)

        lane = lax.broadcasted_iota(jnp.int32, (bq, HEAD_PAD), 1)
        incl = _tri('le')
        excl = _tri('lt')
        chains = [(c, u) for c in range(ATT_COLS) for u in range(2)]
        qms = [_head_only(q_ref[:, _lanes(c)], lane, u) * scale for c, u in chains]
        doms = [_head_only(do_ref[:, _lanes(c)], lane, u) for c, u in chains]
        rts = [r_ref[2 * c + u] for c, u in chains]

        def step(j, carry, masked):
            ids = range(len(chains))
            kbs = [_rows(k_ref, j, c) for c in range(ATT_COLS)]
            zs = [_dot(qms[n], kbs[chains[n][0]], NT_DIMS) for n in ids]
            das = [_dot(doms[n], _rows(v_ref, j, chains[n][0]), NT_DIMS) for n in ids]
            pos, neg, sigs, parts = [], [], [], []
            for n in ids:
                ls_pos, ls_neg, e = _log_sigmoids(zs[n])
                if masked:
                    ls_neg = jnp.where(_causal(True), ls_neg, 0.0)
                pos.append(ls_pos)
                neg.append(ls_neg)
                sigs.append(jnp.where(zs[n] >= 0.0, 1.0, e) * pl.reciprocal(1.0 + e, approx=True))
                parts.append(_split_bf16(ls_neg))
            prefix = [_dot(parts[n][0], incl, NN_DIMS) + _dot(parts[n][1], incl, NN_DIMS) for n in ids]
            evs, eparts, dvs = [], [], []
            for n in ids:
                a = jnp.exp(pos[n] + (rts[n] - (carry[n][0] + prefix[n])))
                if masked:
                    a = jnp.where(_causal(True), a, 0.0)
                dvs.append(_dot(a.astype(BF16), doms[n], TN_DIMS))
                evs.append(a * das[n])
                eparts.append(evs[n].astype(BF16))
            before = [_dot(eparts[n], excl, NN_DIMS) for n in ids]
            out, dks = [], []
            for n in ids:
                dz = evs[n] - sigs[n] * (evs[n] + (carry[n][1] + before[n]))
                if masked:
                    dz = jnp.where(_causal(True), dz, 0.0)
                dzb = dz.astype(BF16)
                dks.append(_dot(dzb, qms[n], TN_DIMS))
                out.append((carry[n][0] + jnp.sum(neg[n], axis=-1, keepdims=True),
                            carry[n][1] + jnp.sum(evs[n], axis=-1, keepdims=True),
                            carry[n][2] + _dot(dzb, kbs[chains[n][0]], NN_DIMS)))
            for c in range(ATT_COLS):
                dv_ref[_row_block(j), _lanes(c)] += dvs[2 * c] + dvs[2 * c + 1]
                dk_ref[_row_block(j), _lanes(c)] += dks[2 * c] + dks[2 * c + 1]
            return tuple(out)

        init = tuple((jnp.zeros((bq, 1), F32), jnp.zeros((bq, 1), F32), jnp.zeros((bq, HEAD_PAD), F32)) for _ in chains)
        carry = lax.fori_loop(0, i, lambda j, cr: step(j, cr, False), init)
        carry = step(i, carry, True)
        for c in range(ATT_COLS):
            dq_ref[:, _lanes(c)] = jnp.where(lane < SB_DIM, carry[2 * c][2], carry[2 * c + 1][2]) * scale

    width = ATT_COLS * HEAD_PAD

    def full(c0):
        return pl.BlockSpec((t, width), lambda g, i: (0, c0 // width + g))

    blk = pl.BlockSpec((bq, width), lambda g, i: (i, g))
    acc = pl.BlockSpec((t, width), lambda g, i: (0, g))
    wide = jax.ShapeDtypeStruct((t, SB_WIDTH), F32)
    return pl.pallas_call(
        body, name="sb_bwd", grid=(pairs // ATT_COLS, t // bq),
        in_specs=[pl.BlockSpec((bq, width), lambda g, i: (i, P_SBQ // width + g)), full(P_SBK), full(P_SBV),
                  blk, pl.BlockSpec((2 * ATT_COLS, bq, 1), lambda g, i: (g, i, 0))],
        out_specs=[blk, acc, acc],
        out_shape=[wide, wide, wide],
        compiler_params=_params(("parallel", "arbitrary")),
    )(proj, proj, proj, do, rtot)


def _cols_to_full(g):
    n, r, c = g.shape
    return jnp.transpose(g, (1, 0, 2)).reshape(r, n * c)


def _full_to_cols(w):
    r, c = w.shape
    return jnp.transpose(w.reshape(r, N_DEV, c // N_DEV), (1, 0, 2))


def _layout_weight(name, g):
    if name in ('ffn1_w_out', 'ffn2_w_out'):
        return g.reshape(D_FF, D_MODEL)
    if name in ('w_out', 'w_ple_gate'):
        return g.reshape(D_MODEL, D_MODEL)
    if name == 'w_in':
        win = _cols_to_full(g)
        z = lambda n: jnp.zeros((D_MODEL, n), BF16)
        return jnp.concatenate(
            [win[:, 0:384], win[:, 384:640], z(64), win[:, 640:672], z(32), z(256), win[:, 2208:4256], win[:, 672:2208]], axis=1)
    if name == 'w_q_up':
        return _cols_to_full(jnp.pad(g, ((0, 0), (0, 0), (0, HEAD_PAD - MLA_QK))))
    if name == 'w_branch_mla':
        bm = _cols_to_full(g).reshape(N_HEADS, MLA_NOPE, D_MODEL)
        return jnp.pad(bm, ((0, 0), (HEAD_PAD - MLA_NOPE, 0), (0, 0))).reshape(N_HEADS * HEAD_PAD, D_MODEL)
    return _cols_to_full(g)


def _layout_weights(g):
    return {n: _layout_weight(n, a) for n, a in g.items()}


def _unlayout_grad(name, d):
    if name in ('ffn1_w_out', 'ffn2_w_out', 'w_out', 'w_ple_gate'):
        return d.reshape(N_DEV, d.shape[0] // N_DEV, D_MODEL)
    if name == 'w_in':
        d = jnp.concatenate([d[:, 0:640], d[:, 704:736], d[:, P_SBQ:PROJ_W], d[:, P_GM:P_SBQ]], axis=1)
    if name == 'w_q_up':
        return _full_to_cols(d)[:, :, :MLA_QK]
    if name == 'w_branch_mla':
        d = d.reshape(N_HEADS, HEAD_PAD, D_MODEL)[:, HEAD_PAD - MLA_NOPE:, :].reshape(SB_WIDTH, D_MODEL)
    return _full_to_cols(d)


def _unlayout_grads(d):
    return {n: _unlayout_grad(n, a) for n, a in d.items()}


def _rope_tables(positions):
    half = MLA_ROPE // 2
    inv_freq = ROPE_BASE ** (-jnp.arange(0, MLA_ROPE, 2, dtype=F32) / MLA_ROPE)
    ang = positions.astype(F32)[:, None] * inv_freq
    cos, sin = jnp.cos(ang), jnp.sin(ang)
    t = positions.shape[0]
    ones = lambda n: jnp.ones((t, n), F32)
    zeros = lambda n: jnp.zeros((t, n), F32)
    cosf = jnp.concatenate([ones(MLA_NOPE), cos, cos, ones(HEAD_PAD - MLA_QK)], axis=1)
    sin_a = jnp.concatenate([zeros(MLA_NOPE), -sin, zeros(half), zeros(HEAD_PAD - MLA_QK)], axis=1)
    sin_b = jnp.concatenate([zeros(MLA_NOPE), zeros(half), sin, zeros(HEAD_PAD - MLA_QK)], axis=1)
    return cosf, sin_a, sin_b


def _local_step(x, p, positions, tgt, norms, plan):
    mm = _matmul
    cosf, sin_a, sin_b = _rope_tables(positions)
    pad_head = lambda g: jnp.pad(g, ((0, 0), (0, HEAD_PAD - MLA_QK)))
    gqh, gkh = pad_head(norms['q_head_norm']), pad_head(norms['k_head_norm'])
    pb = p.astype(BF16)
    w, token = plan.ffn1_weights()
    w = dict(w)

    def ffn_fwd(h, tag, gain):
        n = _rmsnorm_fwd(h, gain, tag + "_norm_fwd")
        ab = mm(n, w[tag + '_w_in'], mode='nn', out_dtype=BF16, name=tag + "_in_fwd")
        act = _swiglu_fwd(ab, tag + "_swiglu_fwd")
        out = mm(act, w[tag + '_w_out'], mode='nn', out_dtype=F32, name=tag + "_out_fwd", res=h, alpha=0.5, tk=1408)
        return out, (n, ab, act)

    h1, ffn1_saved = ffn_fwd(x, 'ffn1', norms['ffn1_norm'] + token)
    more, token = plan.after_ffn1(h1)
    w.update(more)
    u = _rmsnorm_fwd(h1, norms['mix_norm'] + token, "mix_norm_fwd")
    proj = mm(u, w['w_in'], mode='nn', out_dtype=BF16, name="proj_fwd")
    cqn, ckvn = _latent_fwd(proj, norms['q_latent_norm'], norms['kv_latent_norm'])
    qraw = mm(cqn, w['w_q_up'], mode='nn', out_dtype=F32, name="q_up_fwd")
    kvraw = mm(ckvn, w['w_kv_up'], mode='nn', out_dtype=F32, name="kv_up_fwd")
    qh, kh, kvb = _headprep_fwd(qraw, kvraw, proj, cosf, sin_a, sin_b, gqh, gkh)
    o_mla, lse = _mla_fwd(qh, kh, kvb)
    o_sb, rtot = _sb_fwd(proj)
    bm = mm(o_mla, w['w_branch_mla'], mode='nn', out_dtype=F32, name="branch_mla_fwd")
    bs = mm(o_sb, w['w_branch_sb'], mode='nn', out_dtype=F32, name="branch_sb_fwd")
    merged = _merge_fwd(proj, bm, bs)
    h2 = mm(merged, w['w_out'], mode='nn', out_dtype=F32, name="mix_out_fwd", res=h1)
    w.update(plan.after_mixer(h2))
    h3, ffn2_saved = ffn_fwd(h2, 'ffn2', norms['ffn2_norm'])
    n3 = _rmsnorm_fwd(h3, norms['ple_norm'], "ple_norm_fwd")
    zg = mm(n3, w['w_ple_gate'], mode='nn', out_dtype=F32, name="ple_gate_fwd")
    pp = mm(pb, w['w_ple_proj'], mode='nn', out_dtype=F32, name="ple_proj_fwd")
    dh4, dzg, dpp, loss_lanes = _ple_loss(h3, zg, pp, tgt)

    dw, dn = {}, {}
    dw['w_ple_gate'] = mm(n3, dzg, mode='tn', out_dtype=BF16, name="ple_gate_dw")
    dw['w_ple_proj'] = mm(pb, dpp, mode='tn', out_dtype=BF16, name="ple_proj_dw")
    dn3 = mm(dzg, w['w_ple_gate'], mode='nt', out_dtype=F32, name="ple_gate_dx")
    dh3, dhb3, dn['ple_norm'] = _rmsnorm_bwd(dn3, h3, norms['ple_norm'], dh4, "ple_norm_bwd", 0.5)

    def ffn_bwd(h, dh, dhb, saved, tag, out_scale):
        n, ab, act = saved
        dw[tag + '_w_out'] = mm(act, dhb, mode='tn', out_dtype=BF16, name=tag + "_out_dw", tm=1408)
        dact = mm(dhb, w[tag + '_w_out'], mode='nt', out_dtype=BF16, name=tag + "_out_dx", tn=1408)
        dab = _swiglu_bwd(ab, dact, tag + "_swiglu_bwd")
        dw[tag + '_w_in'] = mm(n, dab, mode='tn', out_dtype=BF16, name=tag + "_in_dw", tn=1408)
        gain = norms[tag + '_norm']
        if tag == 'ffn1':
            gain = gain + plan.grads_third({n: dw.pop(n) for n in list(dw)})
        dnn = mm(dab, w[tag + '_w_in'], mode='nt', out_dtype=F32, name=tag + "_in_dx", tk=1408)
        dh_prev, dhb_prev, dn[tag + '_norm'] = _rmsnorm_bwd(dnn, h, gain, dh, tag + "_norm_bwd", out_scale)
        return dh_prev, dhb_prev

    dh2, dhb2 = ffn_bwd(h2, dh3, dhb3, ffn2_saved, 'ffn2', 1.0)
    dw['w_out'] = mm(merged, dhb2, mode='tn', out_dtype=BF16, name="mix_out_dw")
    dmerged = mm(dhb2, w['w_out'], mode='nt', out_dtype=F32, name="mix_out_dx")
    dbm, dbs, dgates = _merge_bwd(dmerged, proj, bm, bs)
    dw['w_branch_mla'] = mm(o_mla, dbm, mode='tn', out_dtype=BF16, name="branch_mla_dw")
    dw['w_branch_sb'] = mm(o_sb, dbs, mode='tn', out_dtype=BF16, name="branch_sb_dw")
    do_mla = mm(dbm, w['w_branch_mla'], mode='nt', out_dtype=BF16, name="branch_mla_dx")
    do_sb = mm(dbs, w['w_branch_sb'], mode='nt', out_dtype=BF16, name="branch_sb_dx")
    token = plan.grads_first({n: dw.pop(n) for n in list(dw)})
    dqh, dkh, dvp = _mla_bwd(qh, kh, kvb, o_mla, do_mla, lse + token)
    dsq, dsk, dsv = _sb_bwd(proj, do_sb, rtot + token)
    dqraw, dkvraw, dkr, dgq, dgk = _headprep_bwd(dqh, dkh, dvp, qraw, kvraw, proj, cosf, sin_a, sin_b, gqh, gkh)
    dn['q_head_norm'], dn['k_head_norm'] = dgq[:, :MLA_QK], dgk[:, :MLA_QK]
    dw['w_q_up'] = mm(cqn, dqraw, mode='tn', out_dtype=BF16, name="q_up_dw")
    dw['w_kv_up'] = mm(ckvn, dkvraw, mode='tn', out_dtype=BF16, name="kv_up_dw")
    dcqn = mm(dqraw, w['w_q_up'], mode='nt', out_dtype=F32, name="q_up_dx")
    dckvn = mm(dkvraw, w['w_kv_up'], mode='nt', out_dtype=F32, name="kv_up_dx")
    dlat, dn['q_latent_norm'], dn['kv_latent_norm'] = _latent_bwd(
        dcqn, dckvn, proj, dkr, norms['q_latent_norm'], norms['kv_latent_norm'])
    dproj = jnp.concatenate([dlat, dgates, dsq.astype(BF16), dsk.astype(BF16), dsv.astype(BF16)], axis=1)
    dw['w_in'] = mm(u, dproj, mode='tn', out_dtype=BF16, name="proj_dw", tn=1536)
    token = plan.grads_second({n: dw.pop(n) for n in list(dw)})
    du = mm(dproj, w['w_in'], mode='nt', out_dtype=F32, name="proj_dx", tk=1536)
    dh1, dhb1, dn['mix_norm'] = _rmsnorm_bwd(du, h1, norms['mix_norm'] + token, dh2, "mix_norm_bwd", 0.5)
    dx, _ = ffn_bwd(x, dh1, dhb1, ffn1_saved, 'ffn1', 1.0)
    return dx, loss_lanes, dw, dn


MESH = pl.DeviceIdType.MESH
HBM_SPEC = pl.BlockSpec(memory_space=pl.ANY)


def _position():
    return lax.axis_index("x"), lax.axis_index("y"), lax.axis_index("c")


def _index(px, py, pc):
    return 4 * px + 2 * py + pc


def _all_gather(shards):
    n = len(shards)

    def body(*refs):
        ins, outs = refs[:n], refs[n:2 * n]
        send_sems, recv_sems, local_sems = refs[2 * n:]
        x, y, c = _position()
        me, sibling = (x, y, c), (x, y, 1 - c)
        chips = [(1 - x, y), (x, 1 - y), (1 - x, 1 - y)]

        def copy(a, k, block, to, own=False):
            dst = outs[a].at[_index(*block)]
            return pltpu.make_async_remote_copy(
                src_ref=ins[a] if own else dst, dst_ref=dst,
                send_sem=send_sems.at[a, k], recv_sem=recv_sems.at[a, k], device_id=to, device_id_type=MESH)

        mine = [pltpu.make_async_copy(ins[a], outs[a].at[_index(*me)], local_sems.at[a]) for a in range(n)]
        for cp in mine:
            cp.start()
        first = []
        for a in range(n):
            first.append(copy(a, 0, me, sibling, own=True))
            first += [copy(a, 1 + j, me, (*chip, c), own=True) for j, chip in enumerate(chips)]
        for cp in first:
            cp.start()
        passed = []
        for j, chip in enumerate(chips):
            for a in range(n):
                copy(a, 1 + j, (*chip, c), me).wait_recv()
                fwd = copy(a, 4 + j, (*chip, c), sibling)
                fwd.start()
                passed.append(fwd)
        for a in range(n):
            copy(a, 0, sibling, me).wait_recv()
            for j, chip in enumerate(chips):
                copy(a, 4 + j, (*chip, 1 - c), me).wait_recv()
        for cp in first + passed:
            cp.wait_send()
        for cp in mine:
            cp.wait()

    return pl.pallas_call(
        body, name="weights_all_gather",
        in_specs=[HBM_SPEC] * n, out_specs=[HBM_SPEC] * n,
        out_shape=[jax.ShapeDtypeStruct((N_DEV,) + s.shape, s.dtype) for s in shards],
        scratch_shapes=[pltpu.SemaphoreType.DMA((n, 7)), pltpu.SemaphoreType.DMA((n, 7)), pltpu.SemaphoreType.DMA((n,))],
    )(*shards)


def _exchange(parts):
    n = len(parts)
    masks = [(mx, my, mc) for mx in (0, 1) for my in (0, 1) for mc in (0, 1)][1:]

    def body(*refs):
        ins, outs = refs[:n], refs[n:2 * n]
        send_sems, recv_sems, local_sems = refs[2 * n:]
        x, y, c = _position()
        me = _index(x, y, c)

        def peer_of(mask):
            mx, my, mc = mask
            return (x + mx - 2 * x * mx, y + my - 2 * y * my, c + mc - 2 * c * mc)

        def copy(a, k):
            peer = peer_of(masks[k])
            return pltpu.make_async_remote_copy(
                src_ref=ins[a].at[_index(*peer)], dst_ref=outs[a].at[me],
                send_sem=send_sems.at[a, k], recv_sem=recv_sems.at[a, k], device_id=peer, device_id_type=MESH)

        def landed(a, k):
            peer = peer_of(masks[k])
            return pltpu.make_async_remote_copy(
                src_ref=ins[a].at[me], dst_ref=outs[a].at[_index(*peer)],
                send_sem=send_sems.at[a, k], recv_sem=recv_sems.at[a, k], device_id=peer, device_id_type=MESH)

        mine = [pltpu.make_async_copy(ins[a].at[me], outs[a].at[me], local_sems.at[a]) for a in range(n)]
        for cp in mine:
            cp.start()
        sent = [copy(a, k) for k in range(7) for a in range(n)]
        for cp in sent:
            cp.start()
        for k in range(7):
            for a in range(n):
                landed(a, k).wait_recv()
        for cp in sent:
            cp.wait_send()
        for cp in mine:
            cp.wait()

    return pl.pallas_call(
        body, name="grads_exchange",
        in_specs=[HBM_SPEC] * n, out_specs=[HBM_SPEC] * n,
        out_shape=[jax.ShapeDtypeStruct(s.shape, s.dtype) for s in parts],
        scratch_shapes=[pltpu.SemaphoreType.DMA((n, 7)), pltpu.SemaphoreType.DMA((n, 7)), pltpu.SemaphoreType.DMA((n,))],
    )(*parts)


HBM_ONLY = pl.BlockSpec(memory_space=pltpu.HBM)
SEM_SPEC = pl.BlockSpec(memory_space=pltpu.SEMAPHORE)
PEER_MASKS = [(mx, my, mc) for mx in (0, 1) for my in (0, 1) for mc in (0, 1)][1:]


def _peer(mask):
    x, y, c = _position()
    mx, my, mc = mask
    return (x + mx - 2 * x * mx, y + my - 2 * y * my, c + mc - 2 * c * mc)


def _push_copy(srcs, lands, send_sems, recv_sems, a, k, scatter):
    peer = _peer(PEER_MASKS[k])
    src = srcs[a].at[_index(*peer)] if scatter else srcs[a]
    sem = a * len(PEER_MASKS) + k
    return pltpu.make_async_remote_copy(
        src_ref=src, dst_ref=lands[a].at[_index(*_position())],
        send_sem=send_sems.at[sem], recv_sem=recv_sems.at[sem], device_id=peer, device_id_type=MESH)


def _push_start(srcs, scatter, after, name):
    n = len(srcs)
    land_shapes = [s.shape if scatter else (N_DEV,) + s.shape for s in srcs]

    def body(*refs):
        src_refs, land_refs = refs[:n], refs[n:2 * n]
        send_sems, recv_sems = refs[2 * n + 1], refs[2 * n + 2]
        token = refs[-1]
        for k in range(len(PEER_MASKS)):
            for a in range(n):
                _push_copy(src_refs, land_refs, send_sems, recv_sems, a, k, scatter).start()
        token[...] = jnp.zeros_like(token)

    sems = pltpu.SemaphoreType.DMA((n * len(PEER_MASKS),))
    outs = pl.pallas_call(
        body, name=name,
        out_shape=(sems, sems, *[pltpu.HBM(s.shape, s.dtype) for s in srcs],
                   *[pltpu.HBM(shape, s.dtype) for shape, s in zip(land_shapes, srcs)],
                   jax.ShapeDtypeStruct((8, 128), F32)),
        in_specs=[HBM_ONLY] * (2 * n) + [pl.BlockSpec(memory_space=pl.ANY)],
        out_specs=(SEM_SPEC, SEM_SPEC, *[HBM_ONLY] * (2 * n), pl.BlockSpec(memory_space=pltpu.VMEM)),
        input_output_aliases={i: 2 + i for i in range(2 * n)},
        compiler_params=pltpu.CompilerParams(has_side_effects=pltpu.SideEffectType.DATAFLOW_SIDE_EFFECTING),
    )(*[pltpu.with_memory_space_constraint(s, pltpu.HBM) for s in srcs],
      *[pltpu.with_memory_space_constraint(lax.empty(shape, s.dtype), pltpu.HBM) for shape, s in zip(land_shapes, srcs)],
      after)
    handle = (outs[0], outs[1], list(outs[2:2 + n]), list(outs[2 + n:2 + 2 * n]))
    return handle, outs[-1][0:1, 0:1]


def _push_wait(handle, scatter, after, name):
    send_sems, recv_sems, srcs, lands = handle
    n = len(srcs)

    def body(*refs):
        src_refs, land_refs = refs[:n], refs[n:2 * n]
        send_ref, recv_ref = refs[2 * n], refs[2 * n + 1]
        for k in range(len(PEER_MASKS)):
            for a in range(n):
                cp = _push_copy(src_refs, land_refs, send_ref, recv_ref, a, k, scatter)
                cp.wait_send()
                cp.wait_recv()

    outs = pl.pallas_call(
        body, name=name,
        out_shape=(*[pltpu.HBM(s.shape, s.dtype) for s in srcs], *[pltpu.HBM(l.shape, l.dtype) for l in lands]),
        in_specs=[HBM_ONLY] * (2 * n) + [SEM_SPEC, SEM_SPEC, pl.BlockSpec(memory_space=pl.ANY)],
        out_specs=tuple([HBM_ONLY] * (2 * n)),
        input_output_aliases={i: i for i in range(2 * n)},
        compiler_params=pltpu.CompilerParams(has_side_effects=pltpu.SideEffectType.DATAFLOW_SIDE_EFFECTING),
    )(*srcs, *lands, send_sems, recv_sems, after)
    return list(outs[:n]), list(outs[n:])


def _fill_own(land, own):
    return lax.dynamic_update_index_in_dim(land, own, _index(*_position()), 0)


def _adamw(parts, w, m, v, name):
    r, c = w.shape
    tr = r if r <= 512 else 256
    assert r % tr == 0
    bc1 = 1.0 - ADAM_B1 ** ADAM_STEP
    bc2 = 1.0 - ADAM_B2 ** ADAM_STEP

    def body(p_ref, w_ref, m_ref, v_ref, g_ref, d_ref, nm_ref, nv_ref):
        g = p_ref[0].astype(F32)
        for s in range(1, N_DEV):
            g = g + p_ref[s].astype(F32)
        nm = ADAM_B1 * m_ref[...] + (1.0 - ADAM_B1) * g
        nv = ADAM_B2 * v_ref[...] + (1.0 - ADAM_B2) * (g * g)
        g_ref[...] = g
        nm_ref[...] = nm
        nv_ref[...] = nv
        d_ref[...] = -ADAM_LR * ((nm / bc1) / (jnp.sqrt(nv / bc2) + ADAM_EPS) + ADAM_WD * w_ref[...])

    row = pl.BlockSpec((tr, c), lambda i: (i, 0))
    out = jax.ShapeDtypeStruct((r, c), F32)
    return pl.pallas_call(
        body, name=name, grid=(r // tr,),
        in_specs=[pl.BlockSpec((N_DEV, tr, c), lambda i: (0, i, 0)), row, row, row],
        out_specs=[row] * 4, out_shape=[out] * 4,
        compiler_params=_params(("parallel",)),
    )(parts, w, m, v)


GATHER_FIRST = ['ffn1_w_in', 'ffn1_w_out']
GATHER_SECOND = ['w_in', 'w_q_up', 'w_kv_up', 'w_branch_mla', 'w_branch_sb', 'w_out']
GATHER_THIRD = ['ffn2_w_in', 'ffn2_w_out', 'w_ple_gate', 'w_ple_proj']
GRADS_FIRST = ['w_ple_gate', 'w_ple_proj', 'ffn2_w_out', 'ffn2_w_in', 'w_out', 'w_branch_mla', 'w_branch_sb']
GRADS_SECOND = ['w_q_up', 'w_kv_up', 'w_in']
GRADS_THIRD = ['ffn1_w_out', 'ffn1_w_in']


class _Plan:
    def __init__(self, shards):
        self.shards = shards
        self.received = {}

    def _own(self, parts):
        return lax.dynamic_index_in_dim(parts, _index(*_position()), 0, keepdims=False)

    def _weights(self, names, handle, after, name):
        shards, lands = _push_wait(handle, False, after, name)
        return {n: _layout_weight(n, _fill_own(land, s)) for n, s, land in zip(names, shards, lands)}, lands

    def ffn1_weights(self):
        gathered = _all_gather([self.shards[n] for n in GATHER_FIRST])
        self.gather2, token = _push_start([self.shards[n] for n in GATHER_SECOND], False, gathered[1], "gather_second_start")
        return {n: _layout_weight(n, g) for n, g in zip(GATHER_FIRST, gathered)}, token

    def after_ffn1(self, h1):
        w, lands = self._weights(GATHER_SECOND, self.gather2, h1, "gather_second_wait")
        self.gather3, token = _push_start([self.shards[n] for n in GATHER_THIRD], False, lands[1], "gather_third_start")
        return w, token

    def after_mixer(self, h2):
        return self._weights(GATHER_THIRD, self.gather3, h2, "gather_third_wait")[0]

    def _send(self, names, dw, after, name):
        assert list(dw) == names, (list(dw), names)
        return _push_start([_unlayout_grad(n, dw[n]) for n in names], True, after, name)

    def _land(self, names, handle, after, name):
        parts, lands = _push_wait(handle, True, after, name)
        got = {n: _fill_own(land, self._own(mine)) for n, mine, land in zip(names, parts, lands)}
        self.received.update(got)
        return got

    def grads_first(self, dw):
        self.first, token = self._send(GRADS_FIRST, dw, jnp.zeros((8, 128), F32), "grads_first_start")
        return token

    def grads_second(self, dw):
        got = self._land(GRADS_FIRST, self.first, dw['w_in'], "grads_first_wait")
        self.second, token = self._send(GRADS_SECOND, dw, got['w_ple_proj'], "grads_second_start")
        return token

    def grads_third(self, dw):
        self.third, token = self._send(GRADS_THIRD, dw, jnp.zeros((8, 128), F32), "grads_third_start")
        return token

    def wait_second(self, after):
        return self._land(GRADS_SECOND, self.second, after, "grads_second_wait")

    def wait_third(self, after):
        return self._land(GRADS_THIRD, self.third, after, "grads_third_wait")


def _pack_small(vecs):
    flat = jnp.concatenate([v.reshape(-1) for v in vecs])
    return jnp.pad(flat, (0, SMALL_ROWS * 128 - flat.shape[0])).reshape(SMALL_ROWS, 128)


def _unpack_small(packed, sizes):
    flat = packed.reshape(-1)
    out, at = [], 0
    for n in sizes:
        out.append(flat[at:at + n])
        at += n
    return out


def kernel(x, p, positions, ffn1_norm, ffn1_w_in, ffn1_w_out, mix_norm, w_in, q_latent_norm, w_q_up, kv_latent_norm, w_kv_up, q_head_norm, k_head_norm, w_branch_mla, w_branch_sb, w_out, ffn2_norm, ffn2_w_in, ffn2_w_out, ple_norm, w_ple_gate, w_ple_proj, loss_target, m_ffn1_norm, m_ffn1_w_in, m_ffn1_w_out, m_mix_norm, m_w_in, m_q_latent_norm, m_w_q_up, m_kv_latent_norm, m_w_kv_up, m_q_head_norm, m_k_head_norm, m_w_branch_mla, m_w_branch_sb, m_w_out, m_ffn2_norm, m_ffn2_w_in, m_ffn2_w_out, m_ple_norm, m_w_ple_gate, m_w_ple_proj, v_ffn1_norm, v_ffn1_w_in, v_ffn1_w_out, v_mix_norm, v_w_in, v_q_latent_norm, v_w_q_up, v_kv_latent_norm, v_w_kv_up, v_q_head_norm, v_k_head_norm, v_w_branch_mla, v_w_branch_sb, v_w_out, v_ffn2_norm, v_ffn2_w_in, v_ffn2_w_out, v_ple_norm, v_w_ple_gate, v_w_ple_proj):
    given = dict(locals())
    wts = {n: given[n] for n in WEIGHTS}
    mom = {n: given['m_' + n] for n in WEIGHTS}
    var = {n: given['v_' + n] for n in WEIGHTS}

    plan = _Plan({n: wts[n][0].astype(BF16) for n in MATS})
    norms = {n: wts[n] for n in NORMS}
    dx, loss_lanes, dw, dn = _local_step(x[0], p[0, 0], positions[0], loss_target[0], norms, plan)
    assert not dw

    out = {}

    def update(names, received):
        last = None
        for n in names:
            res = _adamw(received[n], wts[n][0], mom[n][0], var[n][0], "adamw_" + n)
            out[n] = [r[None] for r in res]
            last = res[0]
        return last

    update(GRADS_FIRST, plan.received)
    done = update(GRADS_SECOND, plan.wait_second(dx))
    small = _pack_small([dn[n] for n in NORMS] + [0.5 / D_MODEL * jnp.sum(loss_lanes)[None]])
    small_parts = _exchange([jnp.broadcast_to(small[None], (N_DEV, SMALL_ROWS, 128))])[0]
    update(GRADS_THIRD, plan.wait_third(done))
    sizes = [wts[n].shape[1] for n in NORMS]
    pack = lambda d: _pack_small([d[n] for n in NORMS])
    small_res = _adamw(small_parts, pack(wts), pack(mom), pack(var), "adamw_norms")
    loss = small_res[0].reshape(-1)[sum(sizes)]
    for i, res in enumerate(small_res):
        for n, vec in zip(NORMS, _unpack_small(res, sizes)):
            out.setdefault(n, [None] * 4)[i] = vec[None]

    return (loss, dx[None], *[out[n][0] for n in WEIGHTS], *[out[n][1] for n in WEIGHTS],
            *[out[n][2] for n in WEIGHTS], *[out[n][3] for n in WEIGHTS])
```

```python
import functools
import math

import jax
import jax.numpy as jnp
from jax import lax
from jax.experimental import pallas as pl
from jax.experimental.pallas import tpu as pltpu

F32 = jnp.float32
BF16 = jnp.bfloat16

N_DEV = 8
D_MODEL = 1024
D_FF = 2816
PLE_DIM = 256
NORM_EPS = 1e-6
N_HEADS = 8
HEAD_PAD = 128
MLA_NOPE = 64
MLA_ROPE = 32
MLA_QK = 96
Q_LORA = 384
KV_LORA = 256
SB_DIM = 64
SB_WIDTH = 512
ROPE_BASE = 10000.0
IN_COLS = 4256

PROJ_W = 4608
P_CQ, P_CKV, P_KR, P_GM, P_GS, P_SBQ, P_SBK, P_SBV = 0, 384, 640, 1024, 2048, 3072, 3584, 4096

ADAM_LR, ADAM_B1, ADAM_B2, ADAM_EPS, ADAM_WD, ADAM_STEP = 0.001, 0.9, 0.999, 1e-08, 0.01, 10

VMEM_LIMIT = 52 * 1024 * 1024

WEIGHTS = ['ffn1_norm', 'ffn1_w_in', 'ffn1_w_out', 'mix_norm', 'w_in', 'q_latent_norm', 'w_q_up',
           'kv_latent_norm', 'w_kv_up', 'q_head_norm', 'k_head_norm', 'w_branch_mla', 'w_branch_sb',
           'w_out', 'ffn2_norm', 'ffn2_w_in', 'ffn2_w_out', 'ple_norm', 'w_ple_gate', 'w_ple_proj']
NORMS = ['ffn1_norm', 'mix_norm', 'q_latent_norm', 'kv_latent_norm', 'q_head_norm', 'k_head_norm',
         'ffn2_norm', 'ple_norm']
MATS = [n for n in WEIGHTS if n not in NORMS]
SMALL_ROWS = 48

NT_DIMS = (((1,), (1,)), ((), ()))
NN_DIMS = (((1,), (0,)), ((), ()))
TN_DIMS = (((0,), (0,)), ((), ()))


def _params(sem=None, vmem=VMEM_LIMIT):
    return pltpu.CompilerParams(dimension_semantics=sem, vmem_limit_bytes=vmem)


def _pick(n, cap):
    if n <= cap:
        return n
    best = None
    for t in range(128, cap + 1, 128):
        if n % t == 0:
            best = t
    assert best is not None, (n, cap)
    return best


def _dot(a, b, dims):
    return lax.dot_general(a, b, dims, preferred_element_type=F32)


def _matmul(a, b, *, mode, out_dtype, name, tm=None, tn=None, tk=None, res=None, alpha=1.0, dep=None):
    if mode == 'nn':
        (m, k), (k2, n) = a.shape, b.shape
    elif mode == 'nt':
        (m, k), (n, k2) = a.shape, b.shape
    else:
        (k, m), (k2, n) = a.shape, b.shape
    assert k == k2, (name, a.shape, b.shape)
    tm = tm or _pick(m, 1024)
    tn = tn or _pick(n, 512)
    tk = tk or _pick(k, 2048)
    assert m % tm == 0 and n % tn == 0 and k % tk == 0, (name, m, n, k, tm, tn, tk)
    nk = k // tk
    dims = {'nn': NN_DIMS, 'nt': NT_DIMS, 'tn': TN_DIMS}[mode]
    has_res = res is not None

    def epilogue(acc, r_ref, o_ref):
        if alpha != 1.0:
            acc = acc * alpha
        if has_res:
            acc = r_ref[...] + acc
        o_ref[...] = acc.astype(out_dtype)

    def body(*refs):
        a_ref, b_ref = refs[0], refs[1]
        r_ref = refs[2] if has_res else None
        o_ref = refs[2 + has_res + (dep is not None)]
        if nk == 1:
            epilogue(_dot(a_ref[...], b_ref[...], dims), r_ref, o_ref)
            return
        acc_ref = refs[-1]
        kk = pl.program_id(2)

        @pl.when(kk == 0)
        def _():
            acc_ref[...] = jnp.zeros_like(acc_ref)

        acc_ref[...] += _dot(a_ref[...], b_ref[...], dims)

        @pl.when(kk == nk - 1)
        def _():
            epilogue(acc_ref[...], r_ref, o_ref)

    if mode == 'tn':
        a_spec = pl.BlockSpec((tk, tm), lambda i, j, kk: (kk, i))
    else:
        a_spec = pl.BlockSpec((tm, tk), lambda i, j, kk: (i, kk))
    if mode == 'nt':
        b_spec = pl.BlockSpec((tn, tk), lambda i, j, kk: (j, kk))
    else:
        b_spec = pl.BlockSpec((tk, tn), lambda i, j, kk: (kk, j))
    o_spec = pl.BlockSpec((tm, tn), lambda i, j, kk: (i, j))
    in_specs = [a_spec, b_spec] + ([o_spec] if has_res else [])
    args = (a, b) + ((res,) if has_res else ())
    if dep is not None:
        in_specs.append(pl.BlockSpec((1, 1), lambda i, j, kk: (0, 0)))
        args += (dep,)
    return pl.pallas_call(
        body, name=name, grid=(m // tm, n // tn, nk),
        in_specs=in_specs, out_specs=o_spec,
        out_shape=jax.ShapeDtypeStruct((m, n), out_dtype),
        scratch_shapes=[pltpu.VMEM((tm, tn), F32)] if nk > 1 else [],
        compiler_params=_params(("parallel", "parallel", "arbitrary")),
    )(*args)


def _row_tile(t, cap=512):
    return min(t, cap)


def _rms(x, width):
    return lax.rsqrt(jnp.sum(x * x, axis=-1, keepdims=True) * (1.0 / width) + NORM_EPS)


def _rmsnorm_fwd(x, g, name):
    t, d = x.shape
    tr = _row_tile(t)

    def body(x_ref, g_ref, o_ref):
        xv = x_ref[...]
        o_ref[...] = ((xv * _rms(xv, d)) * g_ref[...]).astype(BF16)

    return pl.pallas_call(
        body, name=name, grid=(t // tr,),
        in_specs=[pl.BlockSpec((tr, d), lambda i: (i, 0)), pl.BlockSpec((1, d), lambda i: (0, 0))],
        out_specs=pl.BlockSpec((tr, d), lambda i: (i, 0)),
        out_shape=jax.ShapeDtypeStruct((t, d), BF16),
        compiler_params=_params(("parallel",)),
    )(x, g)


def _rmsnorm_bwd(dn, x, g, dh_in, name, out_scale):
    t, d = x.shape
    tr = _row_tile(t, 256)

    def body(dn_ref, x_ref, g_ref, dhin_ref, dh_ref, dhb_ref, dg_ref):
        i = pl.program_id(0)
        xv = x_ref[...]
        dnv = dn_ref[...]
        r = _rms(xv, d)
        y = xv * r
        dy = dnv * g_ref[...]
        dx = r * (dy - y * (jnp.sum(dy * y, axis=-1, keepdims=True) * (1.0 / d)))
        dh = dhin_ref[...] + dx
        dh_ref[...] = dh
        dhb_ref[...] = (dh * out_scale).astype(BF16)
        part = jnp.sum(dnv * y, axis=0, keepdims=True)

        @pl.when(i == 0)
        def _():
            dg_ref[...] = part

        @pl.when(i > 0)
        def _():
            dg_ref[...] += part

    row = pl.BlockSpec((tr, d), lambda i: (i, 0))
    vec = pl.BlockSpec((1, d), lambda i: (0, 0))
    return pl.pallas_call(
        body, name=name, grid=(t // tr,),
        in_specs=[row, row, vec, row], out_specs=[row, row, vec],
        out_shape=[jax.ShapeDtypeStruct((t, d), F32), jax.ShapeDtypeStruct((t, d), BF16),
                   jax.ShapeDtypeStruct((1, d), F32)],
        compiler_params=_params(("arbitrary",)),
    )(dn, x, g, dh_in)


def _sigmoid(x):
    return 1.0 / (1.0 + jnp.exp(-x))


def _swiglu_fwd(ab, name):
    t = ab.shape[0]
    tr = _row_tile(t)

    def body(a_ref, b_ref, o_ref):
        a = a_ref[...].astype(F32)
        o_ref[...] = (a * _sigmoid(a) * b_ref[...].astype(F32)).astype(BF16)

    return pl.pallas_call(
        body, name=name, grid=(t // tr,),
        in_specs=[pl.BlockSpec((tr, D_FF), lambda i: (i, 0)), pl.BlockSpec((tr, D_FF), lambda i: (i, 1))],
        out_specs=pl.BlockSpec((tr, D_FF), lambda i: (i, 0)),
        out_shape=jax.ShapeDtypeStruct((t, D_FF), BF16),
        compiler_params=_params(("parallel",)),
    )(ab, ab)


def _swiglu_bwd(ab, dact, name):
    t = ab.shape[0]
    tr = _row_tile(t, 256)

    def body(ab_ref, d_ref, o_ref):
        a = ab_ref[:, :D_FF].astype(F32)
        b = ab_ref[:, D_FF:].astype(F32)
        dv = d_ref[...].astype(F32)
        s = _sigmoid(a)
        o_ref[:, :D_FF] = (dv * b * s * (1.0 + a * (1.0 - s))).astype(BF16)
        o_ref[:, D_FF:] = (dv * a * s).astype(BF16)

    return pl.pallas_call(
        body, name=name, grid=(t // tr,),
        in_specs=[pl.BlockSpec((tr, 2 * D_FF), lambda i: (i, 0)), pl.BlockSpec((tr, D_FF), lambda i: (i, 0))],
        out_specs=pl.BlockSpec((tr, 2 * D_FF), lambda i: (i, 0)),
        out_shape=jax.ShapeDtypeStruct((t, 2 * D_FF), BF16),
        compiler_params=_params(("parallel",)),
    )(ab, dact)


def _latent_fwd(proj, gq, gkv):
    t = proj.shape[0]
    tr = _row_tile(t)

    def body(p_ref, gq_ref, gkv_ref, cq_ref, ckv_ref):
        cq = p_ref[:, P_CQ:P_CQ + Q_LORA].astype(F32)
        ckv = p_ref[:, P_CKV:P_CKV + KV_LORA].astype(F32)
        cq_ref[...] = ((cq * _rms(cq, Q_LORA)) * gq_ref[...]).astype(BF16)
        ckv_ref[...] = ((ckv * _rms(ckv, KV_LORA)) * gkv_ref[...]).astype(BF16)

    return pl.pallas_call(
        body, name="latent_fwd", grid=(t // tr,),
        in_specs=[pl.BlockSpec((tr, 1024), lambda i: (i, 0)), pl.BlockSpec((1, Q_LORA), lambda i: (0, 0)),
                  pl.BlockSpec((1, KV_LORA), lambda i: (0, 0))],
        out_specs=[pl.BlockSpec((tr, Q_LORA), lambda i: (i, 0)), pl.BlockSpec((tr, KV_LORA), lambda i: (i, 0))],
        out_shape=[jax.ShapeDtypeStruct((t, Q_LORA), BF16), jax.ShapeDtypeStruct((t, KV_LORA), BF16)],
        compiler_params=_params(("parallel",)),
    )(proj, gq, gkv)


def _latent_bwd(dcqn, dckvn, proj, dkr, gq, gkv):
    t = proj.shape[0]
    tr = _row_tile(t, 256)

    def norm_bwd(dn, x, g, width):
        r = _rms(x, width)
        y = x * r
        dy = dn * g
        dx = r * (dy - y * (jnp.sum(dy * y, axis=-1, keepdims=True) * (1.0 / width)))
        return dx, jnp.sum(dn * y, axis=0, keepdims=True)

    def body(dcq_ref, dckv_ref, p_ref, dkr_ref, gq_ref, gkv_ref, o_ref, dgq_ref, dgkv_ref):
        i = pl.program_id(0)
        dcq, pq = norm_bwd(dcq_ref[...], p_ref[:, P_CQ:P_CQ + Q_LORA].astype(F32), gq_ref[...], Q_LORA)
        dckv, pkv = norm_bwd(dckv_ref[...], p_ref[:, P_CKV:P_CKV + KV_LORA].astype(F32), gkv_ref[...], KV_LORA)
        o_ref[:, P_CQ:P_CQ + Q_LORA] = dcq.astype(BF16)
        o_ref[:, P_CKV:P_CKV + KV_LORA] = dckv.astype(BF16)
        o_ref[:, P_KR:P_KR + 128] = dkr_ref[...].astype(BF16)
        o_ref[:, P_KR + 128:1024] = jnp.zeros((tr, 1024 - P_KR - 128), BF16)

        @pl.when(i == 0)
        def _():
            dgq_ref[...] = pq
            dgkv_ref[...] = pkv

        @pl.when(i > 0)
        def _():
            dgq_ref[...] += pq
            dgkv_ref[...] += pkv

    def row(w):
        return pl.BlockSpec((tr, w), lambda i: (i, 0))

    def vec(w):
        return pl.BlockSpec((1, w), lambda i: (0, 0))

    return pl.pallas_call(
        body, name="latent_bwd", grid=(t // tr,),
        in_specs=[row(Q_LORA), row(KV_LORA), row(1024), row(128), vec(Q_LORA), vec(KV_LORA)],
        out_specs=[row(1024), vec(Q_LORA), vec(KV_LORA)],
        out_shape=[jax.ShapeDtypeStruct((t, 1024), BF16), jax.ShapeDtypeStruct((1, Q_LORA), F32),
                   jax.ShapeDtypeStruct((1, KV_LORA), F32)],
        compiler_params=_params(("arbitrary",)),
    )(dcqn, dckvn, proj, dkr, gq, gkv)


def _rope(y, cosf, sin_a, sin_b):
    return y * cosf + pltpu.roll(y, 112, 1) * sin_a + pltpu.roll(y, 16, 1) * sin_b


def _rope_t(d, cosf, sin_a, sin_b):
    return d * cosf + pltpu.roll(d * sin_a, 16, 1) + pltpu.roll(d * sin_b, 112, 1)


def _headprep_fwd(qraw, kvraw, proj, cosf, sin_a, sin_b, gqh, gkh):
    t = qraw.shape[0]
    tr = _row_tile(t, 256)

    def body(q_ref, kv_ref, kr_ref, c_ref, sa_ref, sb_ref, gq_ref, gk_ref, qh_ref, kh_ref, kvb_ref):
        cv, sa, sb = c_ref[...], sa_ref[...], sb_ref[...]
        kr = kr_ref[...].astype(F32)
        lane = lax.broadcasted_iota(jnp.int32, (tr, HEAD_PAD), 1)
        for h in range(N_HEADS):
            cols = slice(h * HEAD_PAD, (h + 1) * HEAD_PAD)
            xq = q_ref[:, cols]
            yq = (xq * _rms(xq, MLA_QK)) * gq_ref[...]
            qh_ref[:, cols] = _rope(yq, cv, sa, sb).astype(BF16)
            kvh = kv_ref[:, cols]
            kvb_ref[:, cols] = jnp.where(lane < MLA_NOPE, 1.0, kvh).astype(BF16)
            xk = jnp.where(lane < MLA_NOPE, kvh, kr)
            yk = (xk * _rms(xk, MLA_QK)) * gk_ref[...]
            kh_ref[:, cols] = _rope(yk, cv, sa, sb).astype(BF16)

    wide = pl.BlockSpec((tr, 1024), lambda i: (i, 0))
    lanes = pl.BlockSpec((tr, HEAD_PAD), lambda i: (i, 0))
    vec = pl.BlockSpec((1, HEAD_PAD), lambda i: (0, 0))
    return pl.pallas_call(
        body, name="headprep_fwd", grid=(t // tr,),
        in_specs=[wide, wide, pl.BlockSpec((tr, HEAD_PAD), lambda i: (i, P_KR // HEAD_PAD)), lanes, lanes, lanes, vec, vec],
        out_specs=[wide, wide, wide],
        out_shape=[jax.ShapeDtypeStruct((t, 1024), BF16)] * 3,
        compiler_params=_params(("parallel",)),
    )(qraw, kvraw, proj, cosf, sin_a, sin_b, gqh, gkh)


def _headprep_bwd(dqh, dkh, dvp, qraw, kvraw, proj, cosf, sin_a, sin_b, gqh, gkh):
    t = qraw.shape[0]
    tr = _row_tile(t, 256)

    def norm_bwd(dn, x, g):
        r = _rms(x, MLA_QK)
        y = x * r
        dy = dn * g
        dx = r * (dy - y * (jnp.sum(dy * y, axis=-1, keepdims=True) * (1.0 / MLA_QK)))
        return dx, jnp.sum(dn * y, axis=0, keepdims=True)

    def body(dq_ref, dk_ref, dv_ref, q_ref, kv_ref, kr_ref, c_ref, sa_ref, sb_ref, gq_ref, gk_ref,
             dqr_ref, dkvr_ref, dkr_ref, dgq_ref, dgk_ref):
        i = pl.program_id(0)
        cv, sa, sb = c_ref[...], sa_ref[...], sb_ref[...]
        kr = kr_ref[...].astype(F32)
        lane = lax.broadcasted_iota(jnp.int32, (tr, HEAD_PAD), 1)
        dkr = jnp.zeros((tr, HEAD_PAD), F32)
        pq = jnp.zeros((1, HEAD_PAD), F32)
        pk = jnp.zeros((1, HEAD_PAD), F32)
        for h in range(N_HEADS):
            cols = slice(h * HEAD_PAD, (h + 1) * HEAD_PAD)
            dxq, pqh = norm_bwd(_rope_t(dq_ref[:, cols], cv, sa, sb), q_ref[:, cols], gq_ref[...])
            dqr_ref[:, cols] = dxq.astype(BF16)
            pq = pq + pqh
            kvh = kv_ref[:, cols]
            xk = jnp.where(lane < MLA_NOPE, kvh, kr)
            dxk, pkh = norm_bwd(_rope_t(dk_ref[:, cols], cv, sa, sb), xk, gk_ref[...])
            pk = pk + pkh
            dkvr_ref[:, cols] = jnp.where(lane < MLA_NOPE, dxk, dv_ref[:, cols]).astype(BF16)
            dkr = dkr + jnp.where(lane < MLA_NOPE, 0.0, dxk)
        dkr_ref[...] = dkr

        @pl.when(i == 0)
        def _():
            dgq_ref[...] = pq
            dgk_ref[...] = pk

        @pl.when(i > 0)
        def _():
            dgq_ref[...] += pq
            dgk_ref[...] += pk

    wide = pl.BlockSpec((tr, 1024), lambda i: (i, 0))
    lanes = pl.BlockSpec((tr, HEAD_PAD), lambda i: (i, 0))
    vec = pl.BlockSpec((1, HEAD_PAD), lambda i: (0, 0))
    return pl.pallas_call(
        body, name="headprep_bwd", grid=(t // tr,),
        in_specs=[wide, wide, wide, wide, wide, pl.BlockSpec((tr, HEAD_PAD), lambda i: (i, P_KR // HEAD_PAD)),
                  lanes, lanes, lanes, vec, vec],
        out_specs=[wide, wide, lanes, vec, vec],
        out_shape=[jax.ShapeDtypeStruct((t, 1024), BF16), jax.ShapeDtypeStruct((t, 1024), BF16),
                   jax.ShapeDtypeStruct((t, HEAD_PAD), F32), jax.ShapeDtypeStruct((1, HEAD_PAD), F32),
                   jax.ShapeDtypeStruct((1, HEAD_PAD), F32)],
        compiler_params=_params(("arbitrary",)),
    )(dqh, dkh, dvp, qraw, kvraw, proj, cosf, sin_a, sin_b, gqh, gkh)


def _merge_fwd(proj, bm, bs):
    t = proj.shape[0]
    tr = _row_tile(t, 256)

    def body(gm_ref, gs_ref, bm_ref, bs_ref, o_ref):
        gm = _sigmoid(gm_ref[...].astype(F32))
        gs = _sigmoid(gs_ref[...].astype(F32))
        o_ref[...] = (gm * bm_ref[...] + gs * bs_ref[...]).astype(BF16)

    row = pl.BlockSpec((tr, 1024), lambda i: (i, 0))
    return pl.pallas_call(
        body, name="merge_fwd", grid=(t // tr,),
        in_specs=[pl.BlockSpec((tr, 1024), lambda i: (i, P_GM // 1024)),
                  pl.BlockSpec((tr, 1024), lambda i: (i, P_GS // 1024)), row, row],
        out_specs=row, out_shape=jax.ShapeDtypeStruct((t, 1024), BF16),
        compiler_params=_params(("parallel",)),
    )(proj, proj, bm, bs)


def _merge_bwd(dmerged, proj, bm, bs):
    t = proj.shape[0]
    tr = _row_tile(t, 256)

    def body(dm_ref, gm_ref, gs_ref, bm_ref, bs_ref, dbm_ref, dbs_ref, dg_ref):
        dm = dm_ref[...]
        gm = _sigmoid(gm_ref[...].astype(F32))
        gs = _sigmoid(gs_ref[...].astype(F32))
        dbm_ref[...] = (dm * gm).astype(BF16)
        dbs_ref[...] = (dm * gs).astype(BF16)
        dg_ref[:, :1024] = (dm * bm_ref[...] * gm * (1.0 - gm)).astype(BF16)
        dg_ref[:, 1024:] = (dm * bs_ref[...] * gs * (1.0 - gs)).astype(BF16)

    row = pl.BlockSpec((tr, 1024), lambda i: (i, 0))
    return pl.pallas_call(
        body, name="merge_bwd", grid=(t // tr,),
        in_specs=[row, pl.BlockSpec((tr, 1024), lambda i: (i, P_GM // 1024)),
                  pl.BlockSpec((tr, 1024), lambda i: (i, P_GS // 1024)), row, row],
        out_specs=[row, row, pl.BlockSpec((tr, 2048), lambda i: (i, 0))],
        out_shape=[jax.ShapeDtypeStruct((t, 1024), BF16), jax.ShapeDtypeStruct((t, 1024), BF16),
                   jax.ShapeDtypeStruct((t, 2048), BF16)],
        compiler_params=_params(("parallel",)),
    )(dmerged, proj, proj, bm, bs)


def _ple_loss(h3, zg, pp, tgt):
    t = h3.shape[0]
    tr = _row_tile(t, 256)

    def body(h_ref, z_ref, p_ref, t_ref, dh_ref, dz_ref, dp_ref, l_ref):
        i = pl.program_id(0)
        pg = _sigmoid(z_ref[...])
        ppv = p_ref[...]
        diff = (h_ref[...] + pg * ppv) - t_ref[...]
        dh = diff * (1.0 / D_MODEL)
        dh_ref[...] = dh
        dp_ref[...] = (dh * pg).astype(BF16)
        dz_ref[...] = (dh * ppv * pg * (1.0 - pg)).astype(BF16)
        sq = jnp.sum(diff * diff, axis=0, keepdims=True)
        part = sq[:, 0:128]
        for c in range(1, D_MODEL // 128):
            part = part + sq[:, c * 128:(c + 1) * 128]

        @pl.when(i == 0)
        def _():
            l_ref[...] = part

        @pl.when(i > 0)
        def _():
            l_ref[...] += part

    row = pl.BlockSpec((tr, 1024), lambda i: (i, 0))
    return pl.pallas_call(
        body, name="ple_loss", grid=(t // tr,),
        in_specs=[row, row, row, row],
        out_specs=[row, row, row, pl.BlockSpec((1, 128), lambda i: (0, 0))],
        out_shape=[jax.ShapeDtypeStruct((t, 1024), F32), jax.ShapeDtypeStruct((t, 1024), BF16),
                   jax.ShapeDtypeStruct((t, 1024), BF16), jax.ShapeDtypeStruct((1, 128), F32)],
        compiler_params=_params(("arbitrary",)),
    )(h3, zg, pp, tgt)


ATT_BLOCK = 256
ATT_COLS = 2


def _split_bf16(x):
    hi = x.astype(BF16)
    return hi, (x - hi.astype(F32)).astype(BF16)


def _tri(kind):
    r = lax.broadcasted_iota(jnp.int32, (ATT_BLOCK, ATT_BLOCK), 0)
    c = lax.broadcasted_iota(jnp.int32, (ATT_BLOCK, ATT_BLOCK), 1)
    cond = {'gt': r > c, 'le': r <= c, 'lt': r < c}[kind]
    return jnp.where(cond, 1.0, 0.0).astype(BF16)


def _causal(strict):
    r = lax.broadcasted_iota(jnp.int32, (ATT_BLOCK, ATT_BLOCK), 0)
    c = lax.broadcasted_iota(jnp.int32, (ATT_BLOCK, ATT_BLOCK), 1)
    return (c < r) if strict else (c <= r)


def _lanes(c):
    return slice(c * HEAD_PAD, (c + 1) * HEAD_PAD)


def _row_block(j):
    return pl.ds(pl.multiple_of(j * ATT_BLOCK, ATT_BLOCK), ATT_BLOCK)


def _rows(ref, j, c):
    return ref[_row_block(j), _lanes(c)]


def _mla_fwd(qh, kh, kvb):
    t = qh.shape[0]
    bq = ATT_BLOCK
    scale = 1.0 / math.sqrt(MLA_QK)

    def body(q_ref, k_ref, v_ref, o_ref, lse_ref):
        i = pl.program_id(1)
        qs = [q_ref[:, _lanes(c)] for c in range(ATT_COLS)]

        def step(j, carry, masked):
            cols = range(ATT_COLS)
            scores = [_dot(qs[c], _rows(k_ref, j, c), NT_DIMS) for c in cols]
            ms, ps, alphas = [], [], []
            for c in cols:
                s = scores[c] * scale
                if masked:
                    s = jnp.where(_causal(False), s, -1e30)
                m_new = jnp.maximum(carry[c][0], jnp.max(s, axis=-1, keepdims=True))
                ps.append(jnp.exp(s - m_new).astype(BF16))
                alphas.append(jnp.exp(carry[c][0] - m_new))
                ms.append(m_new)
            return tuple((ms[c], alphas[c] * carry[c][1] + _dot(ps[c], _rows(v_ref, j, c), NN_DIMS)) for c in cols)

        init = tuple((jnp.full((bq, 1), -1e30, F32), jnp.zeros((bq, HEAD_PAD), F32)) for _ in range(ATT_COLS))
        carry = lax.fori_loop(0, i, lambda j, cr: step(j, cr, False), init)
        for c, (m, acc) in enumerate(step(i, carry, True)):
            l = acc[:, 0:1]
            o_ref[:, _lanes(c)] = (acc / l).astype(BF16)
            lse_ref[c] = m + jnp.log(l)

    width = ATT_COLS * HEAD_PAD
    full = pl.BlockSpec((t, width), lambda h, i: (0, h))
    blk = pl.BlockSpec((bq, width), lambda h, i: (i, h))
    return pl.pallas_call(
        body, name="mla_fwd", grid=(N_HEADS // ATT_COLS, t // bq),
        in_specs=[blk, full, full],
        out_specs=[blk, pl.BlockSpec((ATT_COLS, bq, 1), lambda h, i: (h, i, 0))],
        out_shape=[jax.ShapeDtypeStruct((t, N_HEADS * HEAD_PAD), BF16), jax.ShapeDtypeStruct((N_HEADS, t, 1), F32)],
        compiler_params=_params(("parallel", "arbitrary")),
    )(qh, kh, kvb)


def _mla_bwd(qh, kh, kvb, o, do, lse, dep):
    t = qh.shape[0]
    bq = ATT_BLOCK
    scale = 1.0 / math.sqrt(MLA_QK)

    def body(q_ref, k_ref, v_ref, o_ref, do_ref, lse_ref, dep_ref, dq_ref, dk_ref, dv_ref):
        i = pl.program_id(1)

        @pl.when(i == 0)
        def _():
            dk_ref[...] = jnp.zeros_like(dk_ref)
            dv_ref[...] = jnp.zeros_like(dv_ref)

        qs = [q_ref[:, _lanes(c)] for c in range(ATT_COLS)]
        dos = [do_ref[:, _lanes(c)] for c in range(ATT_COLS)]
        deltas = [jnp.sum(dos[c].astype(F32) * o_ref[:, _lanes(c)].astype(F32), axis=-1, keepdims=True)
                  for c in range(ATT_COLS)]
        lses = [lse_ref[c] for c in range(ATT_COLS)]

        def step(j, dqs, masked):
            cols = range(ATT_COLS)
            kbs = [_rows(k_ref, j, c) for c in cols]
            scores = [_dot(qs[c], kbs[c], NT_DIMS) for c in cols]
            dps = [_dot(dos[c], _rows(v_ref, j, c), NT_DIMS) for c in cols]
            pbs, dss = [], []
            for c in cols:
                p = jnp.exp(scores[c] * scale - lses[c])
                if masked:
                    p = jnp.where(_causal(False), p, 0.0)
                pbs.append(p.astype(BF16))
                dss.append((p * (dps[c] - deltas[c]) * scale).astype(BF16))
            for c in cols:
                dv_ref[_row_block(j), _lanes(c)] += _dot(pbs[c], dos[c], TN_DIMS)
                dk_ref[_row_block(j), _lanes(c)] += _dot(dss[c], qs[c], TN_DIMS)
            return tuple(dqs[c] + _dot(dss[c], kbs[c], NN_DIMS) for c in cols)

        init = tuple(jnp.zeros((bq, HEAD_PAD), F32) for _ in range(ATT_COLS))
        dqs = lax.fori_loop(0, i, lambda j, cr: step(j, cr, False), init)
        for c, dq in enumerate(step(i, dqs, True)):
            dq_ref[:, _lanes(c)] = dq

    width = ATT_COLS * HEAD_PAD
    full = pl.BlockSpec((t, width), lambda h, i: (0, h))
    blk = pl.BlockSpec((bq, width), lambda h, i: (i, h))
    wide = jax.ShapeDtypeStruct((t, N_HEADS * HEAD_PAD), F32)
    return pl.pallas_call(
        body, name="mla_bwd", grid=(N_HEADS // ATT_COLS, t // bq),
        in_specs=[blk, full, full, blk, blk, pl.BlockSpec((ATT_COLS, bq, 1), lambda h, i: (h, i, 0)),
                  pl.BlockSpec((1, 1), lambda h, i: (0, 0))],
        out_specs=[blk, full, full],
        out_shape=[wide, wide, wide],
        compiler_params=_params(("parallel", "arbitrary")),
    )(qh, kh, kvb, o, do, lse, dep)


def _head_only(x, lane, u):
    return jnp.where((lane >= u * SB_DIM) & (lane < (u + 1) * SB_DIM), x, jnp.zeros_like(x))


def _log_sigmoids(z):
    e = jnp.exp(-jnp.abs(z))
    lg = jnp.log(1.0 + e)
    ls_pos = jnp.minimum(z, 0.0) - lg
    return ls_pos, ls_pos - z, e


def _sb_fwd(proj):
    t = proj.shape[0]
    bq = ATT_BLOCK
    scale = 1.0 / math.sqrt(SB_DIM)
    pairs = SB_WIDTH // HEAD_PAD

    def body(q_ref, k_ref, v_ref, o_ref, r_ref):
        i = pl.program_id(1)
        lane = lax.broadcasted_iota(jnp.int32, (bq, HEAD_PAD), 1)
        upper = _tri('gt')
        chains = [(c, u) for c in range(ATT_COLS) for u in range(2)]
        qms = [_head_only(q_ref[:, _lanes(c)], lane, u) * scale for c, u in chains]

        def step(j, carry, masked):
            ids = range(len(chains))
            zs = [_dot(qms[n], _rows(k_ref, j, chains[n][0]), NT_DIMS) for n in ids]
            pos, neg, parts = [], [], []
            for n in ids:
                ls_pos, ls_neg, _ = _log_sigmoids(zs[n])
                if masked:
                    ls_neg = jnp.where(_causal(True), ls_neg, 0.0)
                pos.append(ls_pos)
                neg.append(ls_neg)
                parts.append(_split_bf16(ls_neg))
            suffix = [_dot(parts[n][0], upper, NN_DIMS) + _dot(parts[n][1], upper, NN_DIMS) for n in ids]
            weights = []
            for n in ids:
                a = jnp.exp(pos[n] + suffix[n] + carry[n][0])
                if masked:
                    a = jnp.where(_causal(True), a, 0.0)
                weights.append(a.astype(BF16))
            return tuple((carry[n][0] + jnp.sum(neg[n], axis=-1, keepdims=True),
                          carry[n][1] + _dot(weights[n], _rows(v_ref, j, chains[n][0]), NN_DIMS)) for n in ids)

        init = tuple((jnp.zeros((bq, 1), F32), jnp.zeros((bq, HEAD_PAD), F32)) for _ in chains)
        carry = step(i, init, True)
        carry = lax.fori_loop(0, i, lambda s, cr: step(i - 1 - s, cr, False), carry)
        for n, (c, u) in enumerate(chains):
            r_ref[2 * c + u] = carry[n][0]
        for c in range(ATT_COLS):
            o_ref[:, _lanes(c)] = jnp.where(lane < SB_DIM, carry[2 * c][1], carry[2 * c + 1][1]).astype(BF16)

    width = ATT_COLS * HEAD_PAD

    def full(c0):
        return pl.BlockSpec((t, width), lambda g, i: (0, c0 // width + g))

    return pl.pallas_call(
        body, name="sb_fwd", grid=(pairs // ATT_COLS, t // bq),
        in_specs=[pl.BlockSpec((bq, width), lambda g, i: (i, P_SBQ // width + g)), full(P_SBK), full(P_SBV)],
        out_specs=[pl.BlockSpec((bq, width), lambda g, i: (i, g)),
                   pl.BlockSpec((2 * ATT_COLS, bq, 1), lambda g, i: (g, i, 0))],
        out_shape=[jax.ShapeDtypeStruct((t, SB_WIDTH), BF16), jax.ShapeDtypeStruct((N_HEADS, t, 1), F32)],
        compiler_params=_params(("parallel", "arbitrary")),
    )(proj, proj, proj)


def _sb_bwd(proj, do, rtot, dep):
    t = proj.shape[0]
    bq = ATT_BLOCK
    scale = 1.0 / math.sqrt(SB_DIM)
    pairs = SB_WIDTH // HEAD_PAD

    def body(q_ref, k_ref, v_ref, do_ref, r_ref, dep_ref, dq_ref, dk_ref, dv_ref):
        i = pl.program_id(1)

        @pl.when(i == 0)
        def _():
            dk_ref[...] = jnp.zeros_like(dk_ref)
            dv_ref[...] = jnp.zeros_like(dv_ref)

        lane = lax.broadcasted_iota(jnp.int32, (bq, HEAD_PAD), 1)
        incl = _tri('le')
        excl = _tri('lt')
        chains = [(c, u) for c in range(ATT_COLS) for u in range(2)]
        qms = [_head_only(q_ref[:, _lanes(c)], lane, u) * scale for c, u in chains]
        doms = [_head_only(do_ref[:, _lanes(c)], lane, u) for c, u in chains]
        rts = [r_ref[2 * c + u] for c, u in chains]

        def step(j, carry, masked):
            ids = range(len(chains))
            kbs = [_rows(k_ref, j, c) for c in range(ATT_COLS)]
            zs = [_dot(qms[n], kbs[chains[n][0]], NT_DIMS) for n in ids]
            das = [_dot(doms[n], _rows(v_ref, j, chains[n][0]), NT_DIMS) for n in ids]
            pos, neg, sigs, parts = [], [], [], []
            for n in ids:
                ls_pos, ls_neg, e = _log_sigmoids(zs[n])
                if masked:
                    ls_neg = jnp.where(_causal(True), ls_neg, 0.0)
                pos.append(ls_pos)
                neg.append(ls_neg)
                sigs.append(jnp.where(zs[n] >= 0.0, 1.0, e) * pl.reciprocal(1.0 + e, approx=True))
                parts.append(_split_bf16(ls_neg))
            prefix = [_dot(parts[n][0], incl, NN_DIMS) + _dot(parts[n][1], incl, NN_DIMS) for n in ids]
            evs, eparts, dvs = [], [], []
            for n in ids:
                a = jnp.exp(pos[n] + (rts[n] - (carry[n][0] + prefix[n])))
                if masked:
                    a = jnp.where(_causal(True), a, 0.0)
                dvs.append(_dot(a.astype(BF16), doms[n], TN_DIMS))
                evs.append(a * das[n])
                eparts.append(evs[n].astype(BF16))
            before = [_dot(eparts[n], excl, NN_DIMS) for n in ids]
            out, dks = [], []
            for n in ids:
                dz = evs[n] - sigs[n] * (evs[n] + (carry[n][1] + before[n]))
                if masked:
                    dz = jnp.where(_causal(True), dz, 0.0)
                dzb = dz.astype(BF16)
                dks.append(_dot(dzb, qms[n], TN_DIMS))
                out.append((carry[n][0] + jnp.sum(neg[n], axis=-1, keepdims=True),
                            carry[n][1] + jnp.sum(evs[n], axis=-1, keepdims=True),
                            carry[n][2] + _dot(dzb, kbs[chains[n][0]], NN_DIMS)))
            for c in range(ATT_COLS):
                dv_ref[_row_block(j), _lanes(c)] += dvs[2 * c] + dvs[2 * c + 1]
                dk_ref[_row_block(j), _lanes(c)] += dks[2 * c] + dks[2 * c + 1]
            return tuple(out)

        init = tuple((jnp.zeros((bq, 1), F32), jnp.zeros((bq, 1), F32), jnp.zeros((bq, HEAD_PAD), F32)) for _ in chains)
        carry = lax.fori_loop(0, i, lambda j, cr: step(j, cr, False), init)
        carry = step(i, carry, True)
        for c in range(ATT_COLS):
            dq_ref[:, _lanes(c)] = jnp.where(lane < SB_DIM, carry[2 * c][2], carry[2 * c + 1][2]) * scale

    width = ATT_COLS * HEAD_PAD

    def full(c0):
        return pl.BlockSpec((t, width), lambda g, i: (0, c0 // width + g))

    blk = pl.BlockSpec((bq, width), lambda g, i: (i, g))
    acc = pl.BlockSpec((t, width), lambda g, i: (0, g))
    wide = jax.ShapeDtypeStruct((t, SB_WIDTH), F32)
    return pl.pallas_call(
        body, name="sb_bwd", grid=(pairs // ATT_COLS, t // bq),
        in_specs=[pl.BlockSpec((bq, width), lambda g, i: (i, P_SBQ // width + g)), full(P_SBK), full(P_SBV),
                  blk, pl.BlockSpec((2 * ATT_COLS, bq, 1), lambda g, i: (g, i, 0)), pl.BlockSpec((1, 1), lambda g, i: (0, 0))],
        out_specs=[blk, acc, acc],
        out_shape=[wide, wide, wide],
        compiler_params=_params(("parallel", "arbitrary")),
    )(proj, proj, proj, do, rtot, dep)


def _cols_to_full(g):
    n, r, c = g.shape
    return jnp.transpose(g, (1, 0, 2)).reshape(r, n * c)


def _full_to_cols(w):
    r, c = w.shape
    return jnp.transpose(w.reshape(r, N_DEV, c // N_DEV), (1, 0, 2))


TRANSPOSED = ('ffn1_w_in', 'ffn2_w_in', 'w_in', 'w_q_up')


def _layout_weight(name, g):
    if name in ('ffn1_w_out', 'ffn2_w_out', 'w_out', 'w_ple_gate', 'ffn1_w_in', 'ffn2_w_in'):
        return g.reshape(g.shape[0] * g.shape[1], g.shape[2])
    if name == 'w_in':
        wt = g.reshape(IN_COLS, D_MODEL)
        z = lambda n: jnp.zeros((n, D_MODEL), BF16)
        return jnp.concatenate([wt[0:640], z(64), wt[640:672], z(32), z(256), wt[2208:4256], wt[672:2208]], axis=0)
    if name == 'w_q_up':
        return jnp.pad(g, ((0, 0), (0, HEAD_PAD - MLA_QK), (0, 0))).reshape(N_HEADS * HEAD_PAD, Q_LORA)
    if name == 'w_branch_mla':
        bm = _cols_to_full(g).reshape(N_HEADS, MLA_NOPE, D_MODEL)
        return jnp.pad(bm, ((0, 0), (HEAD_PAD - MLA_NOPE, 0), (0, 0))).reshape(N_HEADS * HEAD_PAD, D_MODEL)
    return _cols_to_full(g)


def _layout_weights(g):
    return {n: _layout_weight(n, a) for n, a in g.items()}


def _unlayout_grad(name, d):
    if name == 'w_in':
        d = jnp.concatenate([d[0:640], d[704:736], d[P_SBQ:PROJ_W], d[P_GM:P_SBQ]], axis=0)
    if name == 'w_q_up':
        return d.reshape(N_HEADS, HEAD_PAD, Q_LORA)[:, :MLA_QK, :]
    if name in ('ffn1_w_out', 'ffn2_w_out', 'w_out', 'w_ple_gate', 'ffn1_w_in', 'ffn2_w_in', 'w_in'):
        return d.reshape(N_DEV, d.shape[0] // N_DEV, d.shape[1])
    if name == 'w_branch_mla':
        d = d.reshape(N_HEADS, HEAD_PAD, D_MODEL)[:, HEAD_PAD - MLA_NOPE:, :].reshape(SB_WIDTH, D_MODEL)
    return _full_to_cols(d)


def _unlayout_grads(d):
    return {n: _unlayout_grad(n, a) for n, a in d.items()}


def _rope_tables(positions):
    half = MLA_ROPE // 2
    inv_freq = ROPE_BASE ** (-jnp.arange(0, MLA_ROPE, 2, dtype=F32) / MLA_ROPE)
    ang = positions.astype(F32)[:, None] * inv_freq
    cos, sin = jnp.cos(ang), jnp.sin(ang)
    t = positions.shape[0]
    ones = lambda n: jnp.ones((t, n), F32)
    zeros = lambda n: jnp.zeros((t, n), F32)
    cosf = jnp.concatenate([ones(MLA_NOPE), cos, cos, ones(HEAD_PAD - MLA_QK)], axis=1)
    sin_a = jnp.concatenate([zeros(MLA_NOPE), -sin, zeros(half), zeros(HEAD_PAD - MLA_QK)], axis=1)
    sin_b = jnp.concatenate([zeros(MLA_NOPE), zeros(half), sin, zeros(HEAD_PAD - MLA_QK)], axis=1)
    return cosf, sin_a, sin_b


def _local_step(x, p, positions, tgt, norms, plan):
    mm = _matmul
    cosf, sin_a, sin_b = _rope_tables(positions)
    pad_head = lambda g: jnp.pad(g, ((0, 0), (0, HEAD_PAD - MLA_QK)))
    gqh, gkh = pad_head(norms['q_head_norm']), pad_head(norms['k_head_norm'])
    pb = p.astype(BF16)
    w, token = plan.ffn1_weights()
    w = dict(w)

    def ffn_fwd(h, tag, gain):
        n = _rmsnorm_fwd(h, gain, tag + "_norm_fwd")
        ab = mm(n, w[tag + '_w_in'], mode='nt', out_dtype=BF16, name=tag + "_in_fwd")
        act = _swiglu_fwd(ab, tag + "_swiglu_fwd")
        out = mm(act, w[tag + '_w_out'], mode='nn', out_dtype=F32, name=tag + "_out_fwd", res=h, alpha=0.5, tk=1408)
        return out, (n, ab, act)

    h1, ffn1_saved = ffn_fwd(x, 'ffn1', norms['ffn1_norm'] + token)
    more, token = plan.after_ffn1(h1)
    w.update(more)
    u = _rmsnorm_fwd(h1, norms['mix_norm'] + token, "mix_norm_fwd")
    proj = mm(u, w['w_in'], mode='nt', out_dtype=BF16, name="proj_fwd")
    cqn, ckvn = _latent_fwd(proj, norms['q_latent_norm'], norms['kv_latent_norm'])
    qraw = mm(cqn, w['w_q_up'], mode='nt', out_dtype=F32, name="q_up_fwd")
    kvraw = mm(ckvn, w['w_kv_up'], mode='nn', out_dtype=F32, name="kv_up_fwd")
    qh, kh, kvb = _headprep_fwd(qraw, kvraw, proj, cosf, sin_a, sin_b, gqh, gkh)
    o_mla, lse = _mla_fwd(qh, kh, kvb)
    o_sb, rtot = _sb_fwd(proj)
    bm = mm(o_mla, w['w_branch_mla'], mode='nn', out_dtype=F32, name="branch_mla_fwd")
    bs = mm(o_sb, w['w_branch_sb'], mode='nn', out_dtype=F32, name="branch_sb_fwd")
    merged = _merge_fwd(proj, bm, bs)
    h2 = mm(merged, w['w_out'], mode='nn', out_dtype=F32, name="mix_out_fwd", res=h1)
    w.update(plan.after_mixer(h2))
    h3, ffn2_saved = ffn_fwd(h2, 'ffn2', norms['ffn2_norm'])
    n3 = _rmsnorm_fwd(h3, norms['ple_norm'], "ple_norm_fwd")
    zg = mm(n3, w['w_ple_gate'], mode='nn', out_dtype=F32, name="ple_gate_fwd")
    pp = mm(pb, w['w_ple_proj'], mode='nn', out_dtype=F32, name="ple_proj_fwd")
    dh4, dzg, dpp, loss_lanes = _ple_loss(h3, zg, pp, tgt)

    dw, dn = {}, {}
    dw['w_ple_gate'] = mm(n3, dzg, mode='tn', out_dtype=BF16, name="ple_gate_dw")
    dw['w_ple_proj'] = mm(pb, dpp, mode='tn', out_dtype=BF16, name="ple_proj_dw")
    dn3 = mm(dzg, w['w_ple_gate'], mode='nt', out_dtype=F32, name="ple_gate_dx")
    dh3, dhb3, dn['ple_norm'] = _rmsnorm_bwd(dn3, h3, norms['ple_norm'], dh4, "ple_norm_bwd", 0.5)

    def ffn_bwd(h, dh, dhb, saved, tag, out_scale):
        n, ab, act = saved
        last = tag == 'ffn1'
        dw[tag + '_w_out'] = mm(act, dhb, mode='tn', out_dtype=BF16, name=tag + "_out_dw", tm=1408)
        token = plan.grads_third({n: dw.pop(n) for n in list(dw)}) if last else None
        dact = mm(dhb, w[tag + '_w_out'], mode='nt', out_dtype=BF16, name=tag + "_out_dx", tn=1408, dep=token)
        dab = _swiglu_bwd(ab, dact, tag + "_swiglu_bwd")
        dw[tag + '_w_in'] = mm(dab, n, mode='tn', out_dtype=BF16, name=tag + "_in_dw", tm=1408)
        token = plan.grads_fourth({n: dw.pop(n) for n in list(dw)}) if last else None
        dnn = mm(dab, w[tag + '_w_in'], mode='nn', out_dtype=F32, name=tag + "_in_dx", tk=1408, dep=token)
        dh_prev, dhb_prev, dn[tag + '_norm'] = _rmsnorm_bwd(dnn, h, norms[tag + '_norm'], dh, tag + "_norm_bwd", out_scale)
        return dh_prev, dhb_prev

    dh2, dhb2 = ffn_bwd(h2, dh3, dhb3, ffn2_saved, 'ffn2', 1.0)
    dw['w_out'] = mm(merged, dhb2, mode='tn', out_dtype=BF16, name="mix_out_dw")
    dmerged = mm(dhb2, w['w_out'], mode='nt', out_dtype=F32, name="mix_out_dx")
    dbm, dbs, dgates = _merge_bwd(dmerged, proj, bm, bs)
    dw['w_branch_mla'] = mm(o_mla, dbm, mode='tn', out_dtype=BF16, name="branch_mla_dw")
    dw['w_branch_sb'] = mm(o_sb, dbs, mode='tn', out_dtype=BF16, name="branch_sb_dw")
    do_mla = mm(dbm, w['w_branch_mla'], mode='nt', out_dtype=BF16, name="branch_mla_dx")
    do_sb = mm(dbs, w['w_branch_sb'], mode='nt', out_dtype=BF16, name="branch_sb_dx")
    token = plan.grads_first({n: dw.pop(n) for n in list(dw)})
    dqh, dkh, dvp = _mla_bwd(qh, kh, kvb, o_mla, do_mla, lse, token)
    dsq, dsk, dsv = _sb_bwd(proj, do_sb, rtot, token)
    dqraw, dkvraw, dkr, dgq, dgk = _headprep_bwd(dqh, dkh, dvp, qraw, kvraw, proj, cosf, sin_a, sin_b, gqh, gkh)
    dn['q_head_norm'], dn['k_head_norm'] = dgq[:, :MLA_QK], dgk[:, :MLA_QK]
    dw['w_q_up'] = mm(dqraw, cqn, mode='tn', out_dtype=BF16, name="q_up_dw")
    dw['w_kv_up'] = mm(ckvn, dkvraw, mode='tn', out_dtype=BF16, name="kv_up_dw")
    dcqn = mm(dqraw, w['w_q_up'], mode='nn', out_dtype=F32, name="q_up_dx")
    dckvn = mm(dkvraw, w['w_kv_up'], mode='nt', out_dtype=F32, name="kv_up_dx")
    dlat, dn['q_latent_norm'], dn['kv_latent_norm'] = _latent_bwd(
        dcqn, dckvn, proj, dkr, norms['q_latent_norm'], norms['kv_latent_norm'])
    dproj = jnp.concatenate([dlat, dgates, dsq.astype(BF16), dsk.astype(BF16), dsv.astype(BF16)], axis=1)
    dw['w_in'] = mm(dproj, u, mode='tn', out_dtype=BF16, name="proj_dw", tm=1536)
    token = plan.grads_second({n: dw.pop(n) for n in list(dw)})
    du = mm(dproj, w['w_in'], mode='nn', out_dtype=F32, name="proj_dx", tk=1536)
    dh1, dhb1, dn['mix_norm'] = _rmsnorm_bwd(du, h1, norms['mix_norm'] + token, dh2, "mix_norm_bwd", 0.5)
    dx, _ = ffn_bwd(x, dh1, dhb1, ffn1_saved, 'ffn1', 1.0)
    return dx, loss_lanes, dw, dn


MESH = pl.DeviceIdType.MESH
HBM_SPEC = pl.BlockSpec(memory_space=pl.ANY)


def _position():
    return lax.axis_index("x"), lax.axis_index("y"), lax.axis_index("c")


def _index(px, py, pc):
    return 4 * px + 2 * py + pc


def _all_gather(shards):
    n = len(shards)

    def body(*refs):
        ins, outs = refs[:n], refs[n:2 * n]
        send_sems, recv_sems, local_sems = refs[2 * n:]
        x, y, c = _position()
        me, sibling = (x, y, c), (x, y, 1 - c)
        chips = [(1 - x, y), (x, 1 - y), (1 - x, 1 - y)]

        def copy(a, k, block, to, own=False):
            dst = outs[a].at[_index(*block)]
            return pltpu.make_async_remote_copy(
                src_ref=ins[a] if own else dst, dst_ref=dst,
                send_sem=send_sems.at[a, k], recv_sem=recv_sems.at[a, k], device_id=to, device_id_type=MESH)

        mine = [pltpu.make_async_copy(ins[a], outs[a].at[_index(*me)], local_sems.at[a]) for a in range(n)]
        for cp in mine:
            cp.start()
        first = []
        for a in range(n):
            first.append(copy(a, 0, me, sibling, own=True))
            first += [copy(a, 1 + j, me, (*chip, c), own=True) for j, chip in enumerate(chips)]
        for cp in first:
            cp.start()
        passed = []
        for j, chip in enumerate(chips):
            for a in range(n):
                copy(a, 1 + j, (*chip, c), me).wait_recv()
                fwd = copy(a, 4 + j, (*chip, c), sibling)
                fwd.start()
                passed.append(fwd)
        for a in range(n):
            copy(a, 0, sibling, me).wait_recv()
            for j, chip in enumerate(chips):
                copy(a, 4 + j, (*chip, 1 - c), me).wait_recv()
        for cp in first + passed:
            cp.wait_send()
        for cp in mine:
            cp.wait()

    return pl.pallas_call(
        body, name="weights_all_gather",
        in_specs=[HBM_SPEC] * n, out_specs=[HBM_SPEC] * n,
        out_shape=[jax.ShapeDtypeStruct((N_DEV,) + s.shape, s.dtype) for s in shards],
        scratch_shapes=[pltpu.SemaphoreType.DMA((n, 7)), pltpu.SemaphoreType.DMA((n, 7)), pltpu.SemaphoreType.DMA((n,))],
    )(*shards)


def _exchange(parts):
    n = len(parts)
    masks = [(mx, my, mc) for mx in (0, 1) for my in (0, 1) for mc in (0, 1)][1:]

    def body(*refs):
        ins, outs = refs[:n], refs[n:2 * n]
        send_sems, recv_sems, local_sems = refs[2 * n:]
        x, y, c = _position()
        me = _index(x, y, c)

        def peer_of(mask):
            mx, my, mc = mask
            return (x + mx - 2 * x * mx, y + my - 2 * y * my, c + mc - 2 * c * mc)

        def copy(a, k):
            peer = peer_of(masks[k])
            return pltpu.make_async_remote_copy(
                src_ref=ins[a].at[_index(*peer)], dst_ref=outs[a].at[me],
                send_sem=send_sems.at[a, k], recv_sem=recv_sems.at[a, k], device_id=peer, device_id_type=MESH)

        def landed(a, k):
            peer = peer_of(masks[k])
            return pltpu.make_async_remote_copy(
                src_ref=ins[a].at[me], dst_ref=outs[a].at[_index(*peer)],
                send_sem=send_sems.at[a, k], recv_sem=recv_sems.at[a, k], device_id=peer, device_id_type=MESH)

        mine = [pltpu.make_async_copy(ins[a].at[me], outs[a].at[me], local_sems.at[a]) for a in range(n)]
        for cp in mine:
            cp.start()
        sent = [copy(a, k) for k in range(7) for a in range(n)]
        for cp in sent:
            cp.start()
        for k in range(7):
            for a in range(n):
                landed(a, k).wait_recv()
        for cp in sent:
            cp.wait_send()
        for cp in mine:
            cp.wait()

    return pl.pallas_call(
        body, name="grads_exchange",
        in_specs=[HBM_SPEC] * n, out_specs=[HBM_SPEC] * n,
        out_shape=[jax.ShapeDtypeStruct(s.shape, s.dtype) for s in parts],
        scratch_shapes=[pltpu.SemaphoreType.DMA((n, 7)), pltpu.SemaphoreType.DMA((n, 7)), pltpu.SemaphoreType.DMA((n,))],
    )(*parts)


HBM_ONLY = pl.BlockSpec(memory_space=pltpu.HBM)
SEM_SPEC = pl.BlockSpec(memory_space=pltpu.SEMAPHORE)
PEER_MASKS = [(mx, my, mc) for mx in (0, 1) for my in (0, 1) for mc in (0, 1)][1:]


def _peer(mask):
    x, y, c = _position()
    mx, my, mc = mask
    return (x + mx - 2 * x * mx, y + my - 2 * y * my, c + mc - 2 * c * mc)


def _push_copy(srcs, lands, send_sems, recv_sems, a, k, scatter):
    peer = _peer(PEER_MASKS[k])
    src = srcs[a].at[_index(*peer)] if scatter else srcs[a]
    sem = a * len(PEER_MASKS) + k
    return pltpu.make_async_remote_copy(
        src_ref=src, dst_ref=lands[a].at[_index(*_position())],
        send_sem=send_sems.at[sem], recv_sem=recv_sems.at[sem], device_id=peer, device_id_type=MESH)


def _push_start(srcs, scatter, after, name):
    n = len(srcs)
    land_shapes = [s.shape if scatter else (N_DEV,) + s.shape for s in srcs]

    def body(*refs):
        src_refs, land_refs = refs[:n], refs[n:2 * n]
        send_sems, recv_sems = refs[2 * n + 1], refs[2 * n + 2]
        token = refs[-1]
        for k in range(len(PEER_MASKS)):
            for a in range(n):
                _push_copy(src_refs, land_refs, send_sems, recv_sems, a, k, scatter).start()
        token[...] = jnp.zeros_like(token)

    sems = pltpu.SemaphoreType.DMA((n * len(PEER_MASKS),))
    outs = pl.pallas_call(
        body, name=name,
        out_shape=(sems, sems, *[pltpu.HBM(s.shape, s.dtype) for s in srcs],
                   *[pltpu.HBM(shape, s.dtype) for shape, s in zip(land_shapes, srcs)],
                   jax.ShapeDtypeStruct((8, 128), F32)),
        in_specs=[HBM_ONLY] * (2 * n) + [pl.BlockSpec(memory_space=pl.ANY)],
        out_specs=(SEM_SPEC, SEM_SPEC, *[HBM_ONLY] * (2 * n), pl.BlockSpec(memory_space=pltpu.VMEM)),
        input_output_aliases={i: 2 + i for i in range(2 * n)},
        compiler_params=pltpu.CompilerParams(has_side_effects=pltpu.SideEffectType.DATAFLOW_SIDE_EFFECTING),
    )(*[pltpu.with_memory_space_constraint(s, pltpu.HBM) for s in srcs],
      *[pltpu.with_memory_space_constraint(lax.empty(shape, s.dtype), pltpu.HBM) for shape, s in zip(land_shapes, srcs)],
      after)
    handle = (outs[0], outs[1], list(outs[2:2 + n]), list(outs[2 + n:2 + 2 * n]))
    return handle, outs[-1][0:1, 0:1]


def _push_wait(handle, scatter, after, name):
    send_sems, recv_sems, srcs, lands = handle
    n = len(srcs)

    def body(*refs):
        src_refs, land_refs = refs[:n], refs[n:2 * n]
        send_ref, recv_ref = refs[2 * n], refs[2 * n + 1]
        for k in range(len(PEER_MASKS)):
            for a in range(n):
                cp = _push_copy(src_refs, land_refs, send_ref, recv_ref, a, k, scatter)
                cp.wait_send()
                cp.wait_recv()

    outs = pl.pallas_call(
        body, name=name,
        out_shape=(*[pltpu.HBM(s.shape, s.dtype) for s in srcs], *[pltpu.HBM(l.shape, l.dtype) for l in lands]),
        in_specs=[HBM_ONLY] * (2 * n) + [SEM_SPEC, SEM_SPEC, pl.BlockSpec(memory_space=pl.ANY)],
        out_specs=tuple([HBM_ONLY] * (2 * n)),
        input_output_aliases={i: i for i in range(2 * n)},
        compiler_params=pltpu.CompilerParams(has_side_effects=pltpu.SideEffectType.DATAFLOW_SIDE_EFFECTING),
    )(*srcs, *lands, send_sems, recv_sems, after)
    return list(outs[:n]), list(outs[n:])


def _fill_own(land, own):
    return lax.dynamic_update_index_in_dim(land, own, _index(*_position()), 0)


def _adamw(parts, w, m, v, name):
    r, c = w.shape
    tr = next((t for t in (512, 384, 352, 256, 128) if r % t == 0), r) if r > 512 else r
    tc = c if tr < r or r <= 512 else 256
    assert r % tr == 0 and c % tc == 0
    bc1 = 1.0 - ADAM_B1 ** ADAM_STEP
    bc2 = 1.0 - ADAM_B2 ** ADAM_STEP

    def body(p_ref, w_ref, m_ref, v_ref, g_ref, d_ref, nm_ref, nv_ref):
        g = p_ref[0].astype(F32)
        for s in range(1, N_DEV):
            g = g + p_ref[s].astype(F32)
        nm = ADAM_B1 * m_ref[...] + (1.0 - ADAM_B1) * g
        nv = ADAM_B2 * v_ref[...] + (1.0 - ADAM_B2) * (g * g)
        g_ref[...] = g
        nm_ref[...] = nm
        nv_ref[...] = nv
        d_ref[...] = -ADAM_LR * ((nm / bc1) / (jnp.sqrt(nv / bc2) + ADAM_EPS) + ADAM_WD * w_ref[...])

    tile = pl.BlockSpec((tr, tc), lambda i, j: (i, j))
    out = jax.ShapeDtypeStruct((r, c), F32)
    return pl.pallas_call(
        body, name=name, grid=(r // tr, c // tc),
        in_specs=[pl.BlockSpec((N_DEV, tr, tc), lambda i, j: (0, i, j)), tile, tile, tile],
        out_specs=[tile] * 4, out_shape=[out] * 4,
        compiler_params=_params(("parallel", "parallel")),
    )(parts, w, m, v)


GATHER_FIRST = ['ffn1_w_in', 'ffn1_w_out']
GATHER_SECOND = ['w_in', 'w_q_up', 'w_kv_up', 'w_branch_mla', 'w_branch_sb', 'w_out']
GATHER_THIRD = ['ffn2_w_in', 'ffn2_w_out', 'w_ple_gate', 'w_ple_proj']
GRADS_FIRST = ['w_ple_gate', 'w_ple_proj', 'ffn2_w_out', 'ffn2_w_in', 'w_out', 'w_branch_mla', 'w_branch_sb']
GRADS_SECOND = ['w_q_up', 'w_kv_up', 'w_in']
GRADS_THIRD = ['ffn1_w_out']
GRADS_FOURTH = ['ffn1_w_in']


class _Plan:
    def __init__(self, shards):
        self.shards = shards
        self.received = {}

    def _own(self, parts):
        return lax.dynamic_index_in_dim(parts, _index(*_position()), 0, keepdims=False)

    def _weights(self, names, handle, after, name):
        shards, lands = _push_wait(handle, False, after, name)
        return {n: _layout_weight(n, _fill_own(land, s)) for n, s, land in zip(names, shards, lands)}, lands

    def ffn1_weights(self):
        gathered = _all_gather([self.shards[n] for n in GATHER_FIRST])
        self.gather2, token = _push_start([self.shards[n] for n in GATHER_SECOND], False, gathered[1], "gather_second_start")
        return {n: _layout_weight(n, g) for n, g in zip(GATHER_FIRST, gathered)}, token

    def after_ffn1(self, h1):
        w, lands = self._weights(GATHER_SECOND, self.gather2, h1, "gather_second_wait")
        self.gather3, token = _push_start([self.shards[n] for n in GATHER_THIRD], False, lands[1], "gather_third_start")
        return w, token

    def after_mixer(self, h2):
        return self._weights(GATHER_THIRD, self.gather3, h2, "gather_third_wait")[0]

    def _send(self, names, dw, after, name):
        assert list(dw) == names, (list(dw), names)
        return _push_start([_unlayout_grad(n, dw[n]) for n in names], True, after, name)

    def _land(self, names, handle, after, name):
        parts, lands = _push_wait(handle, True, after, name)
        got = {n: _fill_own(land, self._own(mine)) for n, mine, land in zip(names, parts, lands)}
        self.received.update(got)
        return got

    def grads_first(self, dw):
        self.first, token = self._send(GRADS_FIRST, dw, jnp.zeros((8, 128), F32), "grads_first_start")
        return token

    def grads_second(self, dw):
        got = self._land(GRADS_FIRST, self.first, dw['w_in'], "grads_first_wait")
        self.second, token = self._send(GRADS_SECOND, dw, got['w_ple_proj'], "grads_second_start")
        return token

    def grads_third(self, dw):
        self.third, token = self._send(GRADS_THIRD, dw, jnp.zeros((8, 128), F32), "grads_third_start")
        return token

    def wait_second(self, after):
        return self._land(GRADS_SECOND, self.second, after, "grads_second_wait")

    def grads_fourth(self, dw):
        self.fourth, token = self._send(GRADS_FOURTH, dw, jnp.zeros((8, 128), F32), "grads_fourth_start")
        return token

    def wait_third(self, after):
        return self._land(GRADS_THIRD, self.third, after, "grads_third_wait")

    def wait_fourth(self, after):
        return self._land(GRADS_FOURTH, self.fourth, after, "grads_fourth_wait")


def _pack_small(vecs):
    flat = jnp.concatenate([v.reshape(-1) for v in vecs])
    return jnp.pad(flat, (0, SMALL_ROWS * 128 - flat.shape[0])).reshape(SMALL_ROWS, 128)


def _unpack_small(packed, sizes):
    flat = packed.reshape(-1)
    out, at = [], 0
    for n in sizes:
        out.append(flat[at:at + n])
        at += n
    return out


def kernel(x, p, positions, ffn1_norm, ffn1_w_in, ffn1_w_out, mix_norm, w_in, q_latent_norm, w_q_up, kv_latent_norm, w_kv_up, q_head_norm, k_head_norm, w_branch_mla, w_branch_sb, w_out, ffn2_norm, ffn2_w_in, ffn2_w_out, ple_norm, w_ple_gate, w_ple_proj, loss_target, m_ffn1_norm, m_ffn1_w_in, m_ffn1_w_out, m_mix_norm, m_w_in, m_q_latent_norm, m_w_q_up, m_kv_latent_norm, m_w_kv_up, m_q_head_norm, m_k_head_norm, m_w_branch_mla, m_w_branch_sb, m_w_out, m_ffn2_norm, m_ffn2_w_in, m_ffn2_w_out, m_ple_norm, m_w_ple_gate, m_w_ple_proj, v_ffn1_norm, v_ffn1_w_in, v_ffn1_w_out, v_mix_norm, v_w_in, v_q_latent_norm, v_w_q_up, v_kv_latent_norm, v_w_kv_up, v_q_head_norm, v_k_head_norm, v_w_branch_mla, v_w_branch_sb, v_w_out, v_ffn2_norm, v_ffn2_w_in, v_ffn2_w_out, v_ple_norm, v_w_ple_gate, v_w_ple_proj):
    given = dict(locals())
    wts = {n: given[n] for n in WEIGHTS}
    mom = {n: given['m_' + n] for n in WEIGHTS}
    var = {n: given['v_' + n] for n in WEIGHTS}

    def local(a, n):
        return jnp.swapaxes(a[0], 0, 1) if n in TRANSPOSED else a[0]

    plan = _Plan({n: local(wts[n], n).astype(BF16) for n in MATS})
    norms = {n: wts[n] for n in NORMS}
    dx, loss_lanes, dw, dn = _local_step(x[0], p[0, 0], positions[0], loss_target[0], norms, plan)
    assert not dw

    out = {}

    def update(names, received):
        last = None
        for n in names:
            res = _adamw(received[n], local(wts[n], n), local(mom[n], n), local(var[n], n), "adamw_" + n)
            out[n] = [local(r[None], n)[None] for r in res]
            last = res[0]
        return last

    update(GRADS_FIRST, plan.received)
    done = update(GRADS_SECOND, plan.wait_second(dx))
    small = _pack_small([dn[n] for n in NORMS] + [0.5 / D_MODEL * jnp.sum(loss_lanes)[None]])
    small_parts = _exchange([jnp.broadcast_to(small[None], (N_DEV, SMALL_ROWS, 128))])[0]
    done = update(GRADS_THIRD, plan.wait_third(done))
    update(GRADS_FOURTH, plan.wait_fourth(done))
    sizes = [wts[n].shape[1] for n in NORMS]
    pack = lambda d: _pack_small([d[n] for n in NORMS])
    small_res = _adamw(small_parts, pack(wts), pack(mom), pack(var), "adamw_norms")
    loss = small_res[0].reshape(-1)[sum(sizes)]
    for i, res in enumerate(small_res):
        for n, vec in zip(NORMS, _unpack_small(res, sizes)):
            out.setdefault(n, [None] * 4)[i] = vec[None]

    return (loss, dx[None], *[out[n][0] for n in WEIGHTS], *[out[n][1] for n in WEIGHTS],
            *[out[n][2] for n in WEIGHTS], *[out[n][3] for n in WEIGHTS])
```

```python
import functools
import math

import jax
import jax.numpy as jnp
from jax import lax
from jax.experimental import pallas as pl
from jax.experimental.pallas import tpu as pltpu

F32 = jnp.float32
BF16 = jnp.bfloat16

N_DEV = 8
D_MODEL = 1024
D_FF = 2816
PLE_DIM = 256
NORM_EPS = 1e-6
N_HEADS = 8
HEAD_PAD = 128
MLA_NOPE = 64
MLA_ROPE = 32
MLA_QK = 96
Q_LORA = 384
KV_LORA = 256
SB_DIM = 64
SB_WIDTH = 512
ROPE_BASE = 10000.0
IN_COLS = 4256

PROJ_W = 4608
P_CQ, P_CKV, P_KR, P_GM, P_GS, P_SBQ, P_SBK, P_SBV = 0, 384, 640, 1024, 2048, 3072, 3584, 4096

ADAM_LR, ADAM_B1, ADAM_B2, ADAM_EPS, ADAM_WD, ADAM_STEP = 0.001, 0.9, 0.999, 1e-08, 0.01, 10

VMEM_LIMIT = 52 * 1024 * 1024

WEIGHTS = ['ffn1_norm', 'ffn1_w_in', 'ffn1_w_out', 'mix_norm', 'w_in', 'q_latent_norm', 'w_q_up',
           'kv_latent_norm', 'w_kv_up', 'q_head_norm', 'k_head_norm', 'w_branch_mla', 'w_branch_sb',
           'w_out', 'ffn2_norm', 'ffn2_w_in', 'ffn2_w_out', 'ple_norm', 'w_ple_gate', 'w_ple_proj']
NORMS = ['ffn1_norm', 'mix_norm', 'q_latent_norm', 'kv_latent_norm', 'q_head_norm', 'k_head_norm',
         'ffn2_norm', 'ple_norm']
MATS = [n for n in WEIGHTS if n not in NORMS]
SMALL_ROWS = 48

NT_DIMS = (((1,), (1,)), ((), ()))
NN_DIMS = (((1,), (0,)), ((), ()))
TN_DIMS = (((0,), (0,)), ((), ()))


def _params(sem=None, vmem=VMEM_LIMIT):
    return pltpu.CompilerParams(dimension_semantics=sem, vmem_limit_bytes=vmem)


def _pick(n, cap):
    if n <= cap:
        return n
    best = None
    for t in range(128, cap + 1, 128):
        if n % t == 0:
            best = t
    assert best is not None, (n, cap)
    return best


def _dot(a, b, dims):
    return lax.dot_general(a, b, dims, preferred_element_type=F32)


def _matmul(a, b, *, mode, out_dtype, name, tm=None, tn=None, tk=None, res=None, alpha=1.0, cargo=None):
    if mode == 'nn':
        (m, k), (k2, n) = a.shape, b.shape
    elif mode == 'nt':
        (m, k), (n, k2) = a.shape, b.shape
    else:
        (k, m), (k2, n) = a.shape, b.shape
    assert k == k2, (name, a.shape, b.shape)
    tm = tm or _pick(m, 1024)
    tn = tn or _pick(n, 512)
    tk = tk or _pick(k, 2048)
    assert m % tm == 0 and n % tn == 0 and k % tk == 0, (name, m, n, k, tm, tn, tk)
    nk = k // tk
    dims = {'nn': NN_DIMS, 'nt': NT_DIMS, 'tn': TN_DIMS}[mode]
    has_res = res is not None

    def epilogue(acc, r_ref, o_ref):
        if alpha != 1.0:
            acc = acc * alpha
        if has_res:
            acc = r_ref[...] + acc
        o_ref[...] = acc.astype(out_dtype)

    grid = (m // tm, n // tn, nk)

    def body(*refs):
        steps = [pl.program_id(d) for d in range(3)]

        def compute(own):
            a_ref, b_ref = own[0], own[1]
            r_ref = own[2] if has_res else None
            o_ref = own[2 + has_res]
            if nk == 1:
                epilogue(_dot(a_ref[...], b_ref[...], dims), r_ref, o_ref)
                return
            acc_ref = own[-1]

            @pl.when(steps[2] == 0)
            def _():
                acc_ref[...] = jnp.zeros_like(acc_ref)

            acc_ref[...] += _dot(a_ref[...], b_ref[...], dims)

            @pl.when(steps[2] == nk - 1)
            def _():
                epilogue(acc_ref[...], r_ref, o_ref)

        _with_cargo(cargo, refs, 2 + has_res, 1, steps, grid, compute)

    if mode == 'tn':
        a_spec = pl.BlockSpec((tk, tm), lambda i, j, kk: (kk, i))
    else:
        a_spec = pl.BlockSpec((tm, tk), lambda i, j, kk: (i, kk))
    if mode == 'nt':
        b_spec = pl.BlockSpec((tn, tk), lambda i, j, kk: (j, kk))
    else:
        b_spec = pl.BlockSpec((tk, tn), lambda i, j, kk: (kk, j))
    o_spec = pl.BlockSpec((tm, tn), lambda i, j, kk: (i, j))
    in_specs = [a_spec, b_spec] + ([o_spec] if has_res else [])
    args = (a, b) + ((res,) if has_res else ())
    out_shape = jax.ShapeDtypeStruct((m, n), out_dtype)
    scratch = [pltpu.VMEM((tm, tn), F32)] if nk > 1 else []
    if cargo is None:
        return pl.pallas_call(
            body, name=name, grid=grid, in_specs=in_specs, out_specs=o_spec, out_shape=out_shape,
            scratch_shapes=scratch, compiler_params=_params(("parallel", "parallel", "arbitrary")),
        )(*args)
    outs = pl.pallas_call(
        body, name=name, grid=grid, in_specs=in_specs + cargo.specs(), out_specs=[o_spec] + cargo.specs(),
        out_shape=[out_shape] + cargo.out_shape(), scratch_shapes=scratch + cargo.scratch(),
        compiler_params=_params(("arbitrary", "arbitrary", "arbitrary")),
    )(*args, *cargo.srcs)
    return outs[0], list(outs[1:])


def _row_tile(t, cap=512):
    return min(t, cap)


def _rms(x, width):
    return lax.rsqrt(jnp.sum(x * x, axis=-1, keepdims=True) * (1.0 / width) + NORM_EPS)


def _rmsnorm_fwd(x, g, name):
    t, d = x.shape
    tr = _row_tile(t)

    def body(x_ref, g_ref, o_ref):
        xv = x_ref[...]
        o_ref[...] = ((xv * _rms(xv, d)) * g_ref[...]).astype(BF16)

    return pl.pallas_call(
        body, name=name, grid=(t // tr,),
        in_specs=[pl.BlockSpec((tr, d), lambda i: (i, 0)), pl.BlockSpec((1, d), lambda i: (0, 0))],
        out_specs=pl.BlockSpec((tr, d), lambda i: (i, 0)),
        out_shape=jax.ShapeDtypeStruct((t, d), BF16),
        compiler_params=_params(("parallel",)),
    )(x, g)


def _rmsnorm_bwd(dn, x, g, dh_in, name, out_scale):
    t, d = x.shape
    tr = _row_tile(t, 256)

    def body(dn_ref, x_ref, g_ref, dhin_ref, dh_ref, dhb_ref, dg_ref):
        i = pl.program_id(0)
        xv = x_ref[...]
        dnv = dn_ref[...]
        r = _rms(xv, d)
        y = xv * r
        dy = dnv * g_ref[...]
        dx = r * (dy - y * (jnp.sum(dy * y, axis=-1, keepdims=True) * (1.0 / d)))
        dh = dhin_ref[...] + dx
        dh_ref[...] = dh
        dhb_ref[...] = (dh * out_scale).astype(BF16)
        part = jnp.sum(dnv * y, axis=0, keepdims=True)

        @pl.when(i == 0)
        def _():
            dg_ref[...] = part

        @pl.when(i > 0)
        def _():
            dg_ref[...] += part

    row = pl.BlockSpec((tr, d), lambda i: (i, 0))
    vec = pl.BlockSpec((1, d), lambda i: (0, 0))
    return pl.pallas_call(
        body, name=name, grid=(t // tr,),
        in_specs=[row, row, vec, row], out_specs=[row, row, vec],
        out_shape=[jax.ShapeDtypeStruct((t, d), F32), jax.ShapeDtypeStruct((t, d), BF16),
                   jax.ShapeDtypeStruct((1, d), F32)],
        compiler_params=_params(("arbitrary",)),
    )(dn, x, g, dh_in)


def _sigmoid(x):
    return 1.0 / (1.0 + jnp.exp(-x))


def _swiglu_fwd(ab, name):
    t = ab.shape[0]
    tr = _row_tile(t)

    def body(a_ref, b_ref, o_ref):
        a = a_ref[...].astype(F32)
        o_ref[...] = (a * _sigmoid(a) * b_ref[...].astype(F32)).astype(BF16)

    return pl.pallas_call(
        body, name=name, grid=(t // tr,),
        in_specs=[pl.BlockSpec((tr, D_FF), lambda i: (i, 0)), pl.BlockSpec((tr, D_FF), lambda i: (i, 1))],
        out_specs=pl.BlockSpec((tr, D_FF), lambda i: (i, 0)),
        out_shape=jax.ShapeDtypeStruct((t, D_FF), BF16),
        compiler_params=_params(("parallel",)),
    )(ab, ab)


def _swiglu_bwd(ab, dact, name):
    t = ab.shape[0]
    tr = _row_tile(t, 256)

    def body(ab_ref, d_ref, o_ref):
        a = ab_ref[:, :D_FF].astype(F32)
        b = ab_ref[:, D_FF:].astype(F32)
        dv = d_ref[...].astype(F32)
        s = _sigmoid(a)
        o_ref[:, :D_FF] = (dv * b * s * (1.0 + a * (1.0 - s))).astype(BF16)
        o_ref[:, D_FF:] = (dv * a * s).astype(BF16)

    return pl.pallas_call(
        body, name=name, grid=(t // tr,),
        in_specs=[pl.BlockSpec((tr, 2 * D_FF), lambda i: (i, 0)), pl.BlockSpec((tr, D_FF), lambda i: (i, 0))],
        out_specs=pl.BlockSpec((tr, 2 * D_FF), lambda i: (i, 0)),
        out_shape=jax.ShapeDtypeStruct((t, 2 * D_FF), BF16),
        compiler_params=_params(("parallel",)),
    )(ab, dact)


def _latent_fwd(proj, gq, gkv):
    t = proj.shape[0]
    tr = _row_tile(t)

    def body(p_ref, gq_ref, gkv_ref, cq_ref, ckv_ref):
        cq = p_ref[:, P_CQ:P_CQ + Q_LORA].astype(F32)
        ckv = p_ref[:, P_CKV:P_CKV + KV_LORA].astype(F32)
        cq_ref[...] = ((cq * _rms(cq, Q_LORA)) * gq_ref[...]).astype(BF16)
        ckv_ref[...] = ((ckv * _rms(ckv, KV_LORA)) * gkv_ref[...]).astype(BF16)

    return pl.pallas_call(
        body, name="latent_fwd", grid=(t // tr,),
        in_specs=[pl.BlockSpec((tr, 1024), lambda i: (i, 0)), pl.BlockSpec((1, Q_LORA), lambda i: (0, 0)),
                  pl.BlockSpec((1, KV_LORA), lambda i: (0, 0))],
        out_specs=[pl.BlockSpec((tr, Q_LORA), lambda i: (i, 0)), pl.BlockSpec((tr, KV_LORA), lambda i: (i, 0))],
        out_shape=[jax.ShapeDtypeStruct((t, Q_LORA), BF16), jax.ShapeDtypeStruct((t, KV_LORA), BF16)],
        compiler_params=_params(("parallel",)),
    )(proj, gq, gkv)


def _latent_bwd(dcqn, dckvn, proj, dkr, gq, gkv):
    t = proj.shape[0]
    tr = _row_tile(t, 256)

    def norm_bwd(dn, x, g, width):
        r = _rms(x, width)
        y = x * r
        dy = dn * g
        dx = r * (dy - y * (jnp.sum(dy * y, axis=-1, keepdims=True) * (1.0 / width)))
        return dx, jnp.sum(dn * y, axis=0, keepdims=True)

    def body(dcq_ref, dckv_ref, p_ref, dkr_ref, gq_ref, gkv_ref, o_ref, dgq_ref, dgkv_ref):
        i = pl.program_id(0)
        dcq, pq = norm_bwd(dcq_ref[...], p_ref[:, P_CQ:P_CQ + Q_LORA].astype(F32), gq_ref[...], Q_LORA)
        dckv, pkv = norm_bwd(dckv_ref[...], p_ref[:, P_CKV:P_CKV + KV_LORA].astype(F32), gkv_ref[...], KV_LORA)
        o_ref[:, P_CQ:P_CQ + Q_LORA] = dcq.astype(BF16)
        o_ref[:, P_CKV:P_CKV + KV_LORA] = dckv.astype(BF16)
        o_ref[:, P_KR:P_KR + 128] = dkr_ref[...].astype(BF16)
        o_ref[:, P_KR + 128:1024] = jnp.zeros((tr, 1024 - P_KR - 128), BF16)

        @pl.when(i == 0)
        def _():
            dgq_ref[...] = pq
            dgkv_ref[...] = pkv

        @pl.when(i > 0)
        def _():
            dgq_ref[...] += pq
            dgkv_ref[...] += pkv

    def row(w):
        return pl.BlockSpec((tr, w), lambda i: (i, 0))

    def vec(w):
        return pl.BlockSpec((1, w), lambda i: (0, 0))

    return pl.pallas_call(
        body, name="latent_bwd", grid=(t // tr,),
        in_specs=[row(Q_LORA), row(KV_LORA), row(1024), row(128), vec(Q_LORA), vec(KV_LORA)],
        out_specs=[row(1024), vec(Q_LORA), vec(KV_LORA)],
        out_shape=[jax.ShapeDtypeStruct((t, 1024), BF16), jax.ShapeDtypeStruct((1, Q_LORA), F32),
                   jax.ShapeDtypeStruct((1, KV_LORA), F32)],
        compiler_params=_params(("arbitrary",)),
    )(dcqn, dckvn, proj, dkr, gq, gkv)


def _rope(y, cosf, sin_a, sin_b):
    return y * cosf + pltpu.roll(y, 112, 1) * sin_a + pltpu.roll(y, 16, 1) * sin_b


def _rope_t(d, cosf, sin_a, sin_b):
    return d * cosf + pltpu.roll(d * sin_a, 16, 1) + pltpu.roll(d * sin_b, 112, 1)


def _headprep_fwd(qraw, kvraw, proj, cosf, sin_a, sin_b, gqh, gkh):
    t = qraw.shape[0]
    tr = _row_tile(t, 256)

    def body(q_ref, kv_ref, kr_ref, c_ref, sa_ref, sb_ref, gq_ref, gk_ref, qh_ref, kh_ref, kvb_ref):
        cv, sa, sb = c_ref[...], sa_ref[...], sb_ref[...]
        kr = kr_ref[...].astype(F32)
        lane = lax.broadcasted_iota(jnp.int32, (tr, HEAD_PAD), 1)
        for h in range(N_HEADS):
            cols = slice(h * HEAD_PAD, (h + 1) * HEAD_PAD)
            xq = q_ref[:, cols]
            yq = (xq * _rms(xq, MLA_QK)) * gq_ref[...]
            qh_ref[:, cols] = _rope(yq, cv, sa, sb).astype(BF16)
            kvh = kv_ref[:, cols]
            kvb_ref[:, cols] = jnp.where(lane < MLA_NOPE, 1.0, kvh).astype(BF16)
            xk = jnp.where(lane < MLA_NOPE, kvh, kr)
            yk = (xk * _rms(xk, MLA_QK)) * gk_ref[...]
            kh_ref[:, cols] = _rope(yk, cv, sa, sb).astype(BF16)

    wide = pl.BlockSpec((tr, 1024), lambda i: (i, 0))
    lanes = pl.BlockSpec((tr, HEAD_PAD), lambda i: (i, 0))
    vec = pl.BlockSpec((1, HEAD_PAD), lambda i: (0, 0))
    return pl.pallas_call(
        body, name="headprep_fwd", grid=(t // tr,),
        in_specs=[wide, wide, pl.BlockSpec((tr, HEAD_PAD), lambda i: (i, P_KR // HEAD_PAD)), lanes, lanes, lanes, vec, vec],
        out_specs=[wide, wide, wide],
        out_shape=[jax.ShapeDtypeStruct((t, 1024), BF16)] * 3,
        compiler_params=_params(("parallel",)),
    )(qraw, kvraw, proj, cosf, sin_a, sin_b, gqh, gkh)


def _headprep_bwd(dqh, dkh, dvp, qraw, kvraw, proj, cosf, sin_a, sin_b, gqh, gkh):
    t = qraw.shape[0]
    tr = _row_tile(t, 256)

    def norm_bwd(dn, x, g):
        r = _rms(x, MLA_QK)
        y = x * r
        dy = dn * g
        dx = r * (dy - y * (jnp.sum(dy * y, axis=-1, keepdims=True) * (1.0 / MLA_QK)))
        return dx, jnp.sum(dn * y, axis=0, keepdims=True)

    def body(dq_ref, dk_ref, dv_ref, q_ref, kv_ref, kr_ref, c_ref, sa_ref, sb_ref, gq_ref, gk_ref,
             dqr_ref, dkvr_ref, dkr_ref, dgq_ref, dgk_ref):
        i = pl.program_id(0)
        cv, sa, sb = c_ref[...], sa_ref[...], sb_ref[...]
        kr = kr_ref[...].astype(F32)
        lane = lax.broadcasted_iota(jnp.int32, (tr, HEAD_PAD), 1)
        dkr = jnp.zeros((tr, HEAD_PAD), F32)
        pq = jnp.zeros((1, HEAD_PAD), F32)
        pk = jnp.zeros((1, HEAD_PAD), F32)
        for h in range(N_HEADS):
            cols = slice(h * HEAD_PAD, (h + 1) * HEAD_PAD)
            dxq, pqh = norm_bwd(_rope_t(dq_ref[:, cols], cv, sa, sb), q_ref[:, cols], gq_ref[...])
            dqr_ref[:, cols] = dxq.astype(BF16)
            pq = pq + pqh
            kvh = kv_ref[:, cols]
            xk = jnp.where(lane < MLA_NOPE, kvh, kr)
            dxk, pkh = norm_bwd(_rope_t(dk_ref[:, cols], cv, sa, sb), xk, gk_ref[...])
            pk = pk + pkh
            dkvr_ref[:, cols] = jnp.where(lane < MLA_NOPE, dxk, dv_ref[:, cols]).astype(BF16)
            dkr = dkr + jnp.where(lane < MLA_NOPE, 0.0, dxk)
        dkr_ref[...] = dkr

        @pl.when(i == 0)
        def _():
            dgq_ref[...] = pq
            dgk_ref[...] = pk

        @pl.when(i > 0)
        def _():
            dgq_ref[...] += pq
            dgk_ref[...] += pk

    wide = pl.BlockSpec((tr, 1024), lambda i: (i, 0))
    lanes = pl.BlockSpec((tr, HEAD_PAD), lambda i: (i, 0))
    vec = pl.BlockSpec((1, HEAD_PAD), lambda i: (0, 0))
    return pl.pallas_call(
        body, name="headprep_bwd", grid=(t // tr,),
        in_specs=[wide, wide, wide, wide, wide, pl.BlockSpec((tr, HEAD_PAD), lambda i: (i, P_KR // HEAD_PAD)),
                  lanes, lanes, lanes, vec, vec],
        out_specs=[wide, wide, lanes, vec, vec],
        out_shape=[jax.ShapeDtypeStruct((t, 1024), BF16), jax.ShapeDtypeStruct((t, 1024), BF16),
                   jax.ShapeDtypeStruct((t, HEAD_PAD), F32), jax.ShapeDtypeStruct((1, HEAD_PAD), F32),
                   jax.ShapeDtypeStruct((1, HEAD_PAD), F32)],
        compiler_params=_params(("arbitrary",)),
    )(dqh, dkh, dvp, qraw, kvraw, proj, cosf, sin_a, sin_b, gqh, gkh)


def _merge_fwd(proj, bm, bs):
    t = proj.shape[0]
    tr = _row_tile(t, 256)

    def body(gm_ref, gs_ref, bm_ref, bs_ref, o_ref):
        gm = _sigmoid(gm_ref[...].astype(F32))
        gs = _sigmoid(gs_ref[...].astype(F32))
        o_ref[...] = (gm * bm_ref[...] + gs * bs_ref[...]).astype(BF16)

    row = pl.BlockSpec((tr, 1024), lambda i: (i, 0))
    return pl.pallas_call(
        body, name="merge_fwd", grid=(t // tr,),
        in_specs=[pl.BlockSpec((tr, 1024), lambda i: (i, P_GM // 1024)),
                  pl.BlockSpec((tr, 1024), lambda i: (i, P_GS // 1024)), row, row],
        out_specs=row, out_shape=jax.ShapeDtypeStruct((t, 1024), BF16),
        compiler_params=_params(("parallel",)),
    )(proj, proj, bm, bs)


def _merge_bwd(dmerged, proj, bm, bs):
    t = proj.shape[0]
    tr = _row_tile(t, 256)

    def body(dm_ref, gm_ref, gs_ref, bm_ref, bs_ref, dbm_ref, dbs_ref, dg_ref):
        dm = dm_ref[...]
        gm = _sigmoid(gm_ref[...].astype(F32))
        gs = _sigmoid(gs_ref[...].astype(F32))
        dbm_ref[...] = (dm * gm).astype(BF16)
        dbs_ref[...] = (dm * gs).astype(BF16)
        dg_ref[:, :1024] = (dm * bm_ref[...] * gm * (1.0 - gm)).astype(BF16)
        dg_ref[:, 1024:] = (dm * bs_ref[...] * gs * (1.0 - gs)).astype(BF16)

    row = pl.BlockSpec((tr, 1024), lambda i: (i, 0))
    return pl.pallas_call(
        body, name="merge_bwd", grid=(t // tr,),
        in_specs=[row, pl.BlockSpec((tr, 1024), lambda i: (i, P_GM // 1024)),
                  pl.BlockSpec((tr, 1024), lambda i: (i, P_GS // 1024)), row, row],
        out_specs=[row, row, pl.BlockSpec((tr, 2048), lambda i: (i, 0))],
        out_shape=[jax.ShapeDtypeStruct((t, 1024), BF16), jax.ShapeDtypeStruct((t, 1024), BF16),
                   jax.ShapeDtypeStruct((t, 2048), BF16)],
        compiler_params=_params(("parallel",)),
    )(dmerged, proj, proj, bm, bs)


def _ple_loss(h3, zg, pp, tgt):
    t = h3.shape[0]
    tr = _row_tile(t, 256)

    def body(h_ref, z_ref, p_ref, t_ref, dh_ref, dz_ref, dp_ref, l_ref):
        i = pl.program_id(0)
        pg = _sigmoid(z_ref[...])
        ppv = p_ref[...]
        diff = (h_ref[...] + pg * ppv) - t_ref[...]
        dh = diff * (1.0 / D_MODEL)
        dh_ref[...] = dh
        dp_ref[...] = (dh * pg).astype(BF16)
        dz_ref[...] = (dh * ppv * pg * (1.0 - pg)).astype(BF16)
        sq = jnp.sum(diff * diff, axis=0, keepdims=True)
        part = sq[:, 0:128]
        for c in range(1, D_MODEL // 128):
            part = part + sq[:, c * 128:(c + 1) * 128]

        @pl.when(i == 0)
        def _():
            l_ref[...] = part

        @pl.when(i > 0)
        def _():
            l_ref[...] += part

    row = pl.BlockSpec((tr, 1024), lambda i: (i, 0))
    return pl.pallas_call(
        body, name="ple_loss", grid=(t // tr,),
        in_specs=[row, row, row, row],
        out_specs=[row, row, row, pl.BlockSpec((1, 128), lambda i: (0, 0))],
        out_shape=[jax.ShapeDtypeStruct((t, 1024), F32), jax.ShapeDtypeStruct((t, 1024), BF16),
                   jax.ShapeDtypeStruct((t, 1024), BF16), jax.ShapeDtypeStruct((1, 128), F32)],
        compiler_params=_params(("arbitrary",)),
    )(h3, zg, pp, tgt)


ATT_BLOCK = 256
ATT_COLS = 2


def _split_bf16(x):
    hi = x.astype(BF16)
    return hi, (x - hi.astype(F32)).astype(BF16)


def _tri(kind):
    r = lax.broadcasted_iota(jnp.int32, (ATT_BLOCK, ATT_BLOCK), 0)
    c = lax.broadcasted_iota(jnp.int32, (ATT_BLOCK, ATT_BLOCK), 1)
    cond = {'gt': r > c, 'le': r <= c, 'lt': r < c}[kind]
    return jnp.where(cond, 1.0, 0.0).astype(BF16)


def _causal(strict):
    r = lax.broadcasted_iota(jnp.int32, (ATT_BLOCK, ATT_BLOCK), 0)
    c = lax.broadcasted_iota(jnp.int32, (ATT_BLOCK, ATT_BLOCK), 1)
    return (c < r) if strict else (c <= r)


def _lanes(c):
    return slice(c * HEAD_PAD, (c + 1) * HEAD_PAD)


def _row_block(j):
    return pl.ds(pl.multiple_of(j * ATT_BLOCK, ATT_BLOCK), ATT_BLOCK)


def _rows(ref, j, c):
    return ref[_row_block(j), _lanes(c)]


def _mla_fwd(qh, kh, kvb, cargo=None):
    t = qh.shape[0]
    bq = ATT_BLOCK
    scale = 1.0 / math.sqrt(MLA_QK)
    grid = (N_HEADS // ATT_COLS, t // bq)

    def body(*refs):
        steps = [pl.program_id(0), pl.program_id(1)]
        _with_cargo(cargo, refs, 3, 2, steps, grid, lambda own: work(steps[1], *own))

    def work(i, q_ref, k_ref, v_ref, o_ref, lse_ref):
        qs = [q_ref[:, _lanes(c)] for c in range(ATT_COLS)]

        def step(j, carry, masked):
            cols = range(ATT_COLS)
            scores = [_dot(qs[c], _rows(k_ref, j, c), NT_DIMS) for c in cols]
            ms, ps, alphas = [], [], []
            for c in cols:
                s = scores[c] * scale
                if masked:
                    s = jnp.where(_causal(False), s, -1e30)
                m_new = jnp.maximum(carry[c][0], jnp.max(s, axis=-1, keepdims=True))
                ps.append(jnp.exp(s - m_new).astype(BF16))
                alphas.append(jnp.exp(carry[c][0] - m_new))
                ms.append(m_new)
            return tuple((ms[c], alphas[c] * carry[c][1] + _dot(ps[c], _rows(v_ref, j, c), NN_DIMS)) for c in cols)

        init = tuple((jnp.full((bq, 1), -1e30, F32), jnp.zeros((bq, HEAD_PAD), F32)) for _ in range(ATT_COLS))
        carry = lax.fori_loop(0, i, lambda j, cr: step(j, cr, False), init)
        for c, (m, acc) in enumerate(step(i, carry, True)):
            l = acc[:, 0:1]
            o_ref[:, _lanes(c)] = (acc / l).astype(BF16)
            lse_ref[c] = m + jnp.log(l)

    width = ATT_COLS * HEAD_PAD
    full = pl.BlockSpec((t, width), lambda h, i: (0, h))
    blk = pl.BlockSpec((bq, width), lambda h, i: (i, h))
    extra = cargo.specs() if cargo else []
    outs = pl.pallas_call(
        body, name="mla_fwd", grid=grid,
        in_specs=[blk, full, full] + extra,
        out_specs=[blk, pl.BlockSpec((ATT_COLS, bq, 1), lambda h, i: (h, i, 0))] + extra,
        out_shape=[jax.ShapeDtypeStruct((t, N_HEADS * HEAD_PAD), BF16), jax.ShapeDtypeStruct((N_HEADS, t, 1), F32)]
        + (cargo.out_shape() if cargo else []),
        scratch_shapes=cargo.scratch() if cargo else [],
        compiler_params=_params(("arbitrary", "arbitrary")),
    )(qh, kh, kvb, *(cargo.srcs if cargo else []))
    return (outs[0], outs[1], list(outs[2:])) if cargo else outs


def _mla_bwd(qh, kh, kvb, o, do, lse, cargo=None):
    t = qh.shape[0]
    bq = ATT_BLOCK
    scale = 1.0 / math.sqrt(MLA_QK)
    grid = (N_HEADS // ATT_COLS, t // bq)

    def body(*refs):
        steps = [pl.program_id(0), pl.program_id(1)]
        _with_cargo(cargo, refs, 6, 3, steps, grid, lambda own: work(steps[1], *own))

    def work(i, q_ref, k_ref, v_ref, o_ref, do_ref, lse_ref, dq_ref, dk_ref, dv_ref):

        @pl.when(i == 0)
        def _():
            dk_ref[...] = jnp.zeros_like(dk_ref)
            dv_ref[...] = jnp.zeros_like(dv_ref)

        qs = [q_ref[:, _lanes(c)] for c in range(ATT_COLS)]
        dos = [do_ref[:, _lanes(c)] for c in range(ATT_COLS)]
        deltas = [jnp.sum(dos[c].astype(F32) * o_ref[:, _lanes(c)].astype(F32), axis=-1, keepdims=True)
                  for c in range(ATT_COLS)]
        lses = [lse_ref[c] for c in range(ATT_COLS)]

        def step(j, dqs, masked):
            cols = range(ATT_COLS)
            kbs = [_rows(k_ref, j, c) for c in cols]
            scores = [_dot(qs[c], kbs[c], NT_DIMS) for c in cols]
            dps = [_dot(dos[c], _rows(v_ref, j, c), NT_DIMS) for c in cols]
            pbs, dss = [], []
            for c in cols:
                p = jnp.exp(scores[c] * scale - lses[c])
                if masked:
                    p = jnp.where(_causal(False), p, 0.0)
                pbs.append(p.astype(BF16))
                dss.append((p * (dps[c] - deltas[c]) * scale).astype(BF16))
            for c in cols:
                dv_ref[_row_block(j), _lanes(c)] += _dot(pbs[c], dos[c], TN_DIMS)
                dk_ref[_row_block(j), _lanes(c)] += _dot(dss[c], qs[c], TN_DIMS)
            return tuple(dqs[c] + _dot(dss[c], kbs[c], NN_DIMS) for c in cols)

        init = tuple(jnp.zeros((bq, HEAD_PAD), F32) for _ in range(ATT_COLS))
        dqs = lax.fori_loop(0, i, lambda j, cr: step(j, cr, False), init)
        for c, dq in enumerate(step(i, dqs, True)):
            dq_ref[:, _lanes(c)] = dq

    width = ATT_COLS * HEAD_PAD
    full = pl.BlockSpec((t, width), lambda h, i: (0, h))
    blk = pl.BlockSpec((bq, width), lambda h, i: (i, h))
    wide = jax.ShapeDtypeStruct((t, N_HEADS * HEAD_PAD), F32)
    extra = cargo.specs() if cargo else []
    outs = pl.pallas_call(
        body, name="mla_bwd", grid=grid,
        in_specs=[blk, full, full, blk, blk, pl.BlockSpec((ATT_COLS, bq, 1), lambda h, i: (h, i, 0))] + extra,
        out_specs=[blk, full, full] + extra,
        out_shape=[wide, wide, wide] + (cargo.out_shape() if cargo else []),
        scratch_shapes=cargo.scratch() if cargo else [],
        compiler_params=_params(("arbitrary", "arbitrary")),
    )(qh, kh, kvb, o, do, lse, *(cargo.srcs if cargo else []))
    return (outs[0], outs[1], outs[2], list(outs[3:])) if cargo else outs


def _head_only(x, lane, u):
    return jnp.where((lane >= u * SB_DIM) & (lane < (u + 1) * SB_DIM), x, jnp.zeros_like(x))


SB_DEAD = -104.0


def _log_sigmoids(z):
    e = jnp.exp(-jnp.abs(z))
    lg = jnp.log(1.0 + e)
    ls_pos = jnp.minimum(z, 0.0) - lg
    return ls_pos, ls_pos - z, e


def _sb_fwd(proj):
    t = proj.shape[0]
    bq = ATT_BLOCK
    nq = t // bq
    scale = 1.0 / math.sqrt(SB_DIM)
    pairs = SB_WIDTH // HEAD_PAD

    def body(q_ref, k_ref, v_ref, o_ref, r_ref, first_ref):
        g, i = pl.program_id(0), pl.program_id(1)
        lane = lax.broadcasted_iota(jnp.int32, (bq, HEAD_PAD), 1)
        upper = _tri('gt')
        chains = [(c, u) for c in range(ATT_COLS) for u in range(2)]
        qms = [_head_only(q_ref[:, _lanes(c)], lane, u) * scale for c, u in chains]

        def step(j, carry, masked):
            ids = range(len(chains))
            zs = [_dot(qms[n], _rows(k_ref, j, chains[n][0]), NT_DIMS) for n in ids]
            pos, neg, parts = [], [], []
            for n in ids:
                ls_pos, ls_neg, _ = _log_sigmoids(zs[n])
                if masked:
                    ls_neg = jnp.where(_causal(True), ls_neg, 0.0)
                pos.append(ls_pos)
                neg.append(ls_neg)
                parts.append(_split_bf16(ls_neg))
            suffix = [_dot(parts[n][0], upper, NN_DIMS) + _dot(parts[n][1], upper, NN_DIMS) for n in ids]
            weights = []
            for n in ids:
                a = jnp.exp(pos[n] + suffix[n] + carry[n][0])
                if masked:
                    a = jnp.where(_causal(True), a, 0.0)
                weights.append(a.astype(BF16))
            return tuple((carry[n][0] + jnp.sum(neg[n], axis=-1, keepdims=True),
                          carry[n][1] + _dot(weights[n], _rows(v_ref, j, chains[n][0]), NN_DIMS)) for n in ids)

        init = tuple((jnp.zeros((bq, 1), F32), jnp.zeros((bq, HEAD_PAD), F32)) for _ in chains)
        carry = step(i, init, True)

        def more(state):
            s, cr = state
            live = cr[0][0]
            for n in range(1, len(chains)):
                live = jnp.maximum(live, cr[n][0])
            return jnp.logical_and(s < i, jnp.max(live) > SB_DEAD)

        walked, carry = lax.while_loop(more, lambda st: (st[0] + 1, step(i - 1 - st[0], st[1], False)),
                                       (jnp.int32(0), carry))
        first_ref[g * nq + i] = i - walked
        for n, (c, u) in enumerate(chains):
            r_ref[2 * c + u] = carry[n][0]
        for c in range(ATT_COLS):
            o_ref[:, _lanes(c)] = jnp.where(lane < SB_DIM, carry[2 * c][1], carry[2 * c + 1][1]).astype(BF16)

    width = ATT_COLS * HEAD_PAD

    def full(c0):
        return pl.BlockSpec((t, width), lambda g, i: (0, c0 // width + g))

    return pl.pallas_call(
        body, name="sb_fwd", grid=(pairs // ATT_COLS, t // bq),
        in_specs=[pl.BlockSpec((bq, width), lambda g, i: (i, P_SBQ // width + g)), full(P_SBK), full(P_SBV)],
        out_specs=[pl.BlockSpec((bq, width), lambda g, i: (i, g)),
                   pl.BlockSpec((2 * ATT_COLS, bq, 1), lambda g, i: (g, i, 0)),
                   pl.BlockSpec(memory_space=pltpu.SMEM)],
        out_shape=[jax.ShapeDtypeStruct((t, SB_WIDTH), BF16), jax.ShapeDtypeStruct((N_HEADS, t, 1), F32),
                   jax.ShapeDtypeStruct((pairs // ATT_COLS * nq,), jnp.int32)],
        compiler_params=_params(("arbitrary", "arbitrary")),
    )(proj, proj, proj)


def _sb_bwd(proj, do, rtot, first):
    t = proj.shape[0]
    bq = ATT_BLOCK
    nq = t // bq
    scale = 1.0 / math.sqrt(SB_DIM)
    pairs = SB_WIDTH // HEAD_PAD

    def body(first_ref, q_ref, k_ref, v_ref, do_ref, r_ref, dq_ref, dk_ref, dv_ref):
        g, i = pl.program_id(0), pl.program_id(1)

        @pl.when(i == 0)
        def _():
            dk_ref[...] = jnp.zeros_like(dk_ref)
            dv_ref[...] = jnp.zeros_like(dv_ref)

        lane = lax.broadcasted_iota(jnp.int32, (bq, HEAD_PAD), 1)
        incl = _tri('le')
        excl = _tri('lt')
        chains = [(c, u) for c in range(ATT_COLS) for u in range(2)]
        qms = [_head_only(q_ref[:, _lanes(c)], lane, u) * scale for c, u in chains]
        doms = [_head_only(do_ref[:, _lanes(c)], lane, u) for c, u in chains]
        rts = [r_ref[2 * c + u] for c, u in chains]

        def step(j, carry, masked):
            ids = range(len(chains))
            kbs = [_rows(k_ref, j, c) for c in range(ATT_COLS)]
            zs = [_dot(qms[n], kbs[chains[n][0]], NT_DIMS) for n in ids]
            das = [_dot(doms[n], _rows(v_ref, j, chains[n][0]), NT_DIMS) for n in ids]
            pos, neg, sigs, parts = [], [], [], []
            for n in ids:
                ls_pos, ls_neg, e = _log_sigmoids(zs[n])
                if masked:
                    ls_neg = jnp.where(_causal(True), ls_neg, 0.0)
                pos.append(ls_pos)
                neg.append(ls_neg)
                sigs.append(jnp.where(zs[n] >= 0.0, 1.0, e) * pl.reciprocal(1.0 + e, approx=True))
                parts.append(_split_bf16(ls_neg))
            prefix = [_dot(parts[n][0], incl, NN_DIMS) + _dot(parts[n][1], incl, NN_DIMS) for n in ids]
            evs, eparts, dvs = [], [], []
            for n in ids:
                a = jnp.exp(pos[n] + (rts[n] - (carry[n][0] + prefix[n])))
                if masked:
                    a = jnp.where(_causal(True), a, 0.0)
                dvs.append(_dot(a.astype(BF16), doms[n], TN_DIMS))
                evs.append(a * das[n])
                eparts.append(evs[n].astype(BF16))
            before = [_dot(eparts[n], excl, NN_DIMS) for n in ids]
            out, dks = [], []
            for n in ids:
                dz = evs[n] - sigs[n] * (evs[n] + (carry[n][1] + before[n]))
                if masked:
                    dz = jnp.where(_causal(True), dz, 0.0)
                dzb = dz.astype(BF16)
                dks.append(_dot(dzb, qms[n], TN_DIMS))
                out.append((carry[n][0] + jnp.sum(neg[n], axis=-1, keepdims=True),
                            carry[n][1] + jnp.sum(evs[n], axis=-1, keepdims=True),
                            carry[n][2] + _dot(dzb, kbs[chains[n][0]], NN_DIMS)))
            for c in range(ATT_COLS):
                dv_ref[_row_block(j), _lanes(c)] += dvs[2 * c] + dvs[2 * c + 1]
                dk_ref[_row_block(j), _lanes(c)] += dks[2 * c] + dks[2 * c + 1]
            return tuple(out)

        init = tuple((jnp.zeros((bq, 1), F32), jnp.zeros((bq, 1), F32), jnp.zeros((bq, HEAD_PAD), F32)) for _ in chains)
        carry = lax.fori_loop(first_ref[g * nq + i], i, lambda j, cr: step(j, cr, False), init)
        carry = step(i, carry, True)
        for c in range(ATT_COLS):
            dq_ref[:, _lanes(c)] = jnp.where(lane < SB_DIM, carry[2 * c][2], carry[2 * c + 1][2]) * scale

    width = ATT_COLS * HEAD_PAD

    def full(c0):
        return pl.BlockSpec((t, width), lambda g, i, first: (0, c0 // width + g))

    blk = pl.BlockSpec((bq, width), lambda g, i, first: (i, g))
    acc = pl.BlockSpec((t, width), lambda g, i, first: (0, g))
    wide = jax.ShapeDtypeStruct((t, SB_WIDTH), F32)
    return pl.pallas_call(
        body, name="sb_bwd",
        grid_spec=pltpu.PrefetchScalarGridSpec(
            num_scalar_prefetch=1, grid=(pairs // ATT_COLS, nq),
            in_specs=[pl.BlockSpec((bq, width), lambda g, i, first: (i, P_SBQ // width + g)), full(P_SBK), full(P_SBV),
                      blk, pl.BlockSpec((2 * ATT_COLS, bq, 1), lambda g, i, first: (g, i, 0))],
            out_specs=[blk, acc, acc]),
        out_shape=[wide, wide, wide],
        compiler_params=_params(("arbitrary", "arbitrary")),
    )(first, proj, proj, proj, do, rtot)


def _cols_to_full(g):
    n, r, c = g.shape
    return jnp.transpose(g, (1, 0, 2)).reshape(r, n * c)


def _full_to_cols(w):
    r, c = w.shape
    return jnp.transpose(w.reshape(r, N_DEV, c // N_DEV), (1, 0, 2))


TRANSPOSED = ('ffn1_w_in', 'ffn2_w_in', 'w_in', 'w_q_up')


def _layout_weight(name, g):
    if name in ('ffn1_w_out', 'ffn2_w_out', 'w_out', 'w_ple_gate', 'ffn1_w_in', 'ffn2_w_in'):
        return g.reshape(g.shape[0] * g.shape[1], g.shape[2])
    if name == 'w_in':
        wt = g.reshape(IN_COLS, D_MODEL)
        z = lambda n: jnp.zeros((n, D_MODEL), BF16)
        return jnp.concatenate([wt[0:640], z(64), wt[640:672], z(32), z(256), wt[2208:4256], wt[672:2208]], axis=0)
    if name == 'w_q_up':
        return jnp.pad(g, ((0, 0), (0, HEAD_PAD - MLA_QK), (0, 0))).reshape(N_HEADS * HEAD_PAD, Q_LORA)
    if name == 'w_branch_mla':
        bm = _cols_to_full(g).reshape(N_HEADS, MLA_NOPE, D_MODEL)
        return jnp.pad(bm, ((0, 0), (HEAD_PAD - MLA_NOPE, 0), (0, 0))).reshape(N_HEADS * HEAD_PAD, D_MODEL)
    return _cols_to_full(g)


def _layout_weights(g):
    return {n: _layout_weight(n, a) for n, a in g.items()}


def _unlayout_grad(name, d):
    if name == 'w_in':
        d = jnp.concatenate([d[0:640], d[704:736], d[P_SBQ:PROJ_W], d[P_GM:P_SBQ]], axis=0)
    if name == 'w_q_up':
        return d.reshape(N_HEADS, HEAD_PAD, Q_LORA)[:, :MLA_QK, :]
    if name in ('ffn1_w_out', 'ffn2_w_out', 'w_out', 'w_ple_gate', 'ffn1_w_in', 'ffn2_w_in', 'w_in'):
        return d.reshape(N_DEV, d.shape[0] // N_DEV, d.shape[1])
    if name == 'w_branch_mla':
        d = d.reshape(N_HEADS, HEAD_PAD, D_MODEL)[:, HEAD_PAD - MLA_NOPE:, :].reshape(SB_WIDTH, D_MODEL)
    return _full_to_cols(d)


def _unlayout_grads(d):
    return {n: _unlayout_grad(n, a) for n, a in d.items()}


def _rope_tables(positions):
    half = MLA_ROPE // 2
    inv_freq = ROPE_BASE ** (-jnp.arange(0, MLA_ROPE, 2, dtype=F32) / MLA_ROPE)
    ang = positions.astype(F32)[:, None] * inv_freq
    cos, sin = jnp.cos(ang), jnp.sin(ang)
    t = positions.shape[0]
    ones = lambda n: jnp.ones((t, n), F32)
    zeros = lambda n: jnp.zeros((t, n), F32)
    cosf = jnp.concatenate([ones(MLA_NOPE), cos, cos, ones(HEAD_PAD - MLA_QK)], axis=1)
    sin_a = jnp.concatenate([zeros(MLA_NOPE), -sin, zeros(half), zeros(HEAD_PAD - MLA_QK)], axis=1)
    sin_b = jnp.concatenate([zeros(MLA_NOPE), zeros(half), sin, zeros(HEAD_PAD - MLA_QK)], axis=1)
    return cosf, sin_a, sin_b


def _local_step(x, p, positions, tgt, norms, plan):
    mm = _matmul
    cosf, sin_a, sin_b = _rope_tables(positions)
    pad_head = lambda g: jnp.pad(g, ((0, 0), (0, HEAD_PAD - MLA_QK)))
    gqh, gkh = pad_head(norms['q_head_norm']), pad_head(norms['k_head_norm'])
    pb = p.astype(BF16)
    w = dict(plan.first_weights())
    dw, dn = {}, {}

    def ride(host, call):
        cargo = plan.cargo(host, dw)
        res, lands = call(cargo), None
        if cargo is not None:
            *res, lands = res
            res = res[0] if len(res) == 1 else tuple(res)
        w.update(plan.landed(host, lands))
        return res

    def ffn_fwd(h, tag):
        n = _rmsnorm_fwd(h, norms[tag + '_norm'], tag + "_norm_fwd")
        ab = ride(tag + "_in_fwd", lambda cargo: mm(
            n, w[tag + '_w_in'], mode='nt', out_dtype=BF16, name=tag + "_in_fwd", cargo=cargo))
        act = _swiglu_fwd(ab, tag + "_swiglu_fwd")
        out = ride(tag + "_out_fwd", lambda cargo: mm(
            act, w[tag + '_w_out'], mode='nn', out_dtype=F32, name=tag + "_out_fwd", res=h, alpha=0.5, tk=1408, cargo=cargo))
        return out, (n, ab, act)

    h1, ffn1_saved = ffn_fwd(x, 'ffn1')
    u = _rmsnorm_fwd(h1, norms['mix_norm'], "mix_norm_fwd")
    proj = mm(u, w['w_in'], mode='nt', out_dtype=BF16, name="proj_fwd")
    cqn, ckvn = _latent_fwd(proj, norms['q_latent_norm'], norms['kv_latent_norm'])
    qraw = mm(cqn, w['w_q_up'], mode='nt', out_dtype=F32, name="q_up_fwd")
    kvraw = mm(ckvn, w['w_kv_up'], mode='nn', out_dtype=F32, name="kv_up_fwd")
    qh, kh, kvb = _headprep_fwd(qraw, kvraw, proj, cosf, sin_a, sin_b, gqh, gkh)
    o_mla, lse = ride("mla_fwd", lambda cargo: _mla_fwd(qh, kh, kvb, cargo))
    o_sb, rtot, sb_first = _sb_fwd(proj)
    bm = mm(o_mla, w['w_branch_mla'], mode='nn', out_dtype=F32, name="branch_mla_fwd")
    bs = mm(o_sb, w['w_branch_sb'], mode='nn', out_dtype=F32, name="branch_sb_fwd")
    merged = _merge_fwd(proj, bm, bs)
    h2 = mm(merged, w['w_out'], mode='nn', out_dtype=F32, name="mix_out_fwd", res=h1)
    h3, ffn2_saved = ffn_fwd(h2, 'ffn2')
    n3 = _rmsnorm_fwd(h3, norms['ple_norm'], "ple_norm_fwd")
    zg = mm(n3, w['w_ple_gate'], mode='nn', out_dtype=F32, name="ple_gate_fwd")
    pp = mm(pb, w['w_ple_proj'], mode='nn', out_dtype=F32, name="ple_proj_fwd")
    dh4, dzg, dpp, loss_lanes = _ple_loss(h3, zg, pp, tgt)

    dw['w_ple_gate'] = mm(n3, dzg, mode='tn', out_dtype=BF16, name="ple_gate_dw")
    dw['w_ple_proj'] = mm(pb, dpp, mode='tn', out_dtype=BF16, name="ple_proj_dw")
    dn3 = mm(dzg, w['w_ple_gate'], mode='nt', out_dtype=F32, name="ple_gate_dx")
    dh3, dhb3, dn['ple_norm'] = _rmsnorm_bwd(dn3, h3, norms['ple_norm'], dh4, "ple_norm_bwd", 0.5)

    def ffn_bwd(h, dh, dhb, saved, tag, out_scale):
        n, ab, act = saved
        dw[tag + '_w_out'] = mm(act, dhb, mode='tn', out_dtype=BF16, name=tag + "_out_dw", tm=1408)
        dact = mm(dhb, w[tag + '_w_out'], mode='nt', out_dtype=BF16, name=tag + "_out_dx", tn=1408)
        dab = _swiglu_bwd(ab, dact, tag + "_swiglu_bwd")
        dw[tag + '_w_in'] = ride(tag + "_in_dw", lambda cargo: mm(
            dab, n, mode='tn', out_dtype=BF16, name=tag + "_in_dw", tm=1408, cargo=cargo))
        dnn = ride(tag + "_in_dx", lambda cargo: mm(
            dab, w[tag + '_w_in'], mode='nn', out_dtype=F32, name=tag + "_in_dx", tk=1408, cargo=cargo))
        dh_prev, dhb_prev, dn[tag + '_norm'] = _rmsnorm_bwd(dnn, h, norms[tag + '_norm'], dh, tag + "_norm_bwd", out_scale)
        return dh_prev, dhb_prev

    dh2, dhb2 = ffn_bwd(h2, dh3, dhb3, ffn2_saved, 'ffn2', 1.0)
    dw['w_out'] = mm(merged, dhb2, mode='tn', out_dtype=BF16, name="mix_out_dw")
    dmerged = mm(dhb2, w['w_out'], mode='nt', out_dtype=F32, name="mix_out_dx")
    dbm, dbs, dgates = _merge_bwd(dmerged, proj, bm, bs)
    dw['w_branch_mla'] = mm(o_mla, dbm, mode='tn', out_dtype=BF16, name="branch_mla_dw")
    dw['w_branch_sb'] = mm(o_sb, dbs, mode='tn', out_dtype=BF16, name="branch_sb_dw")
    do_mla = mm(dbm, w['w_branch_mla'], mode='nt', out_dtype=BF16, name="branch_mla_dx")
    do_sb = mm(dbs, w['w_branch_sb'], mode='nt', out_dtype=BF16, name="branch_sb_dx")
    dqh, dkh, dvp = ride("mla_bwd", lambda cargo: _mla_bwd(qh, kh, kvb, o_mla, do_mla, lse, cargo))
    dsq, dsk, dsv = _sb_bwd(proj, do_sb, rtot, sb_first)
    dqraw, dkvraw, dkr, dgq, dgk = _headprep_bwd(dqh, dkh, dvp, qraw, kvraw, proj, cosf, sin_a, sin_b, gqh, gkh)
    dn['q_head_norm'], dn['k_head_norm'] = dgq[:, :MLA_QK], dgk[:, :MLA_QK]
    dw['w_q_up'] = mm(dqraw, cqn, mode='tn', out_dtype=BF16, name="q_up_dw")
    dw['w_kv_up'] = mm(ckvn, dkvraw, mode='tn', out_dtype=BF16, name="kv_up_dw")
    dcqn = mm(dqraw, w['w_q_up'], mode='nn', out_dtype=F32, name="q_up_dx")
    dckvn = mm(dkvraw, w['w_kv_up'], mode='nt', out_dtype=F32, name="kv_up_dx")
    dlat, dn['q_latent_norm'], dn['kv_latent_norm'] = _latent_bwd(
        dcqn, dckvn, proj, dkr, norms['q_latent_norm'], norms['kv_latent_norm'])
    dproj = jnp.concatenate([dlat, dgates, dsq.astype(BF16), dsk.astype(BF16), dsv.astype(BF16)], axis=1)
    dw['w_in'] = ride("proj_dw", lambda cargo: mm(dproj, u, mode='tn', out_dtype=BF16, name="proj_dw", tm=1536, cargo=cargo))
    du = ride("proj_dx", lambda cargo: mm(dproj, w['w_in'], mode='nn', out_dtype=F32, name="proj_dx", tk=1536, cargo=cargo))
    dh1, dhb1, dn['mix_norm'] = _rmsnorm_bwd(du, h1, norms['mix_norm'], dh2, "mix_norm_bwd", 0.5)
    dx, _ = ffn_bwd(x, dh1, dhb1, ffn1_saved, 'ffn1', 1.0)
    return dx, loss_lanes, dw, dn


MESH = pl.DeviceIdType.MESH
HBM_SPEC = pl.BlockSpec(memory_space=pl.ANY)


def _position():
    return lax.axis_index("x"), lax.axis_index("y"), lax.axis_index("c")


def _index(px, py, pc):
    return 4 * px + 2 * py + pc


def _all_gather(shards):
    n = len(shards)

    def body(*refs):
        ins, outs = refs[:n], refs[n:2 * n]
        send_sems, recv_sems, local_sems = refs[2 * n:]
        x, y, c = _position()
        me, sibling = (x, y, c), (x, y, 1 - c)
        chips = [(1 - x, y), (x, 1 - y), (1 - x, 1 - y)]

        def copy(a, k, block, to, own=False):
            dst = outs[a].at[_index(*block)]
            return pltpu.make_async_remote_copy(
                src_ref=ins[a] if own else dst, dst_ref=dst,
                send_sem=send_sems.at[a, k], recv_sem=recv_sems.at[a, k], device_id=to, device_id_type=MESH)

        mine = [pltpu.make_async_copy(ins[a], outs[a].at[_index(*me)], local_sems.at[a]) for a in range(n)]
        for cp in mine:
            cp.start()
        first = []
        for a in range(n):
            first.append(copy(a, 0, me, sibling, own=True))
            first += [copy(a, 1 + j, me, (*chip, c), own=True) for j, chip in enumerate(chips)]
        for cp in first:
            cp.start()
        passed = []
        for j, chip in enumerate(chips):
            for a in range(n):
                copy(a, 1 + j, (*chip, c), me).wait_recv()
                fwd = copy(a, 4 + j, (*chip, c), sibling)
                fwd.start()
                passed.append(fwd)
        for a in range(n):
            copy(a, 0, sibling, me).wait_recv()
            for j, chip in enumerate(chips):
                copy(a, 4 + j, (*chip, 1 - c), me).wait_recv()
        for cp in first + passed:
            cp.wait_send()
        for cp in mine:
            cp.wait()

    return pl.pallas_call(
        body, name="weights_all_gather",
        in_specs=[HBM_SPEC] * n, out_specs=[HBM_SPEC] * n,
        out_shape=[jax.ShapeDtypeStruct((N_DEV,) + s.shape, s.dtype) for s in shards],
        scratch_shapes=[pltpu.SemaphoreType.DMA((n, 7)), pltpu.SemaphoreType.DMA((n, 7)), pltpu.SemaphoreType.DMA((n,))],
    )(*shards)


def _exchange(parts):
    n = len(parts)
    masks = [(mx, my, mc) for mx in (0, 1) for my in (0, 1) for mc in (0, 1)][1:]

    def body(*refs):
        ins, outs = refs[:n], refs[n:2 * n]
        send_sems, recv_sems, local_sems = refs[2 * n:]
        x, y, c = _position()
        me = _index(x, y, c)

        def peer_of(mask):
            mx, my, mc = mask
            return (x + mx - 2 * x * mx, y + my - 2 * y * my, c + mc - 2 * c * mc)

        def copy(a, k):
            peer = peer_of(masks[k])
            return pltpu.make_async_remote_copy(
                src_ref=ins[a].at[_index(*peer)], dst_ref=outs[a].at[me],
                send_sem=send_sems.at[a, k], recv_sem=recv_sems.at[a, k], device_id=peer, device_id_type=MESH)

        def landed(a, k):
            peer = peer_of(masks[k])
            return pltpu.make_async_remote_copy(
                src_ref=ins[a].at[me], dst_ref=outs[a].at[_index(*peer)],
                send_sem=send_sems.at[a, k], recv_sem=recv_sems.at[a, k], device_id=peer, device_id_type=MESH)

        mine = [pltpu.make_async_copy(ins[a].at[me], outs[a].at[me], local_sems.at[a]) for a in range(n)]
        for cp in mine:
            cp.start()
        sent = [copy(a, k) for k in range(7) for a in range(n)]
        for cp in sent:
            cp.start()
        for k in range(7):
            for a in range(n):
                landed(a, k).wait_recv()
        for cp in sent:
            cp.wait_send()
        for cp in mine:
            cp.wait()

    return pl.pallas_call(
        body, name="grads_exchange",
        in_specs=[HBM_SPEC] * n, out_specs=[HBM_SPEC] * n,
        out_shape=[jax.ShapeDtypeStruct(s.shape, s.dtype) for s in parts],
        scratch_shapes=[pltpu.SemaphoreType.DMA((n, 7)), pltpu.SemaphoreType.DMA((n, 7)), pltpu.SemaphoreType.DMA((n,))],
    )(*parts)


PEER_MASKS = [(mx, my, mc) for mx in (0, 1) for my in (0, 1) for mc in (0, 1)][1:]


def _peer(mask):
    x, y, c = _position()
    mx, my, mc = mask
    return (x + mx - 2 * x * mx, y + my - 2 * y * my, c + mc - 2 * c * mc)


class _Cargo:
    def __init__(self, srcs, scatter):
        self.srcs, self.scatter, self.n = list(srcs), scatter, len(srcs)

    def specs(self):
        return [HBM_SPEC] * self.n

    def out_shape(self):
        return [jax.ShapeDtypeStruct(s.shape if self.scatter else (N_DEV,) + s.shape, s.dtype) for s in self.srcs]

    def scratch(self):
        per_copy = pltpu.SemaphoreType.DMA((self.n, len(PEER_MASKS)))
        return [per_copy, per_copy, pltpu.SemaphoreType.DMA((self.n,))]

    def _mine(self, src_refs, a, to):
        return src_refs[a].at[to] if self.scatter else src_refs[a]

    def start(self, src_refs, land_refs, sems):
        send, recv, local = sems
        me = _index(*_position())
        for a in range(self.n):
            pltpu.make_async_copy(self._mine(src_refs, a, me), land_refs[a].at[me], local.at[a]).start()
        for k, mask in enumerate(PEER_MASKS):
            peer = _peer(mask)
            for a in range(self.n):
                pltpu.make_async_remote_copy(
                    src_ref=self._mine(src_refs, a, _index(*peer)), dst_ref=land_refs[a].at[me],
                    send_sem=send.at[a, k], recv_sem=recv.at[a, k], device_id=peer, device_id_type=MESH).start()

    def wait(self, src_refs, land_refs, sems):
        send, recv, local = sems
        me = _index(*_position())
        for k, mask in enumerate(PEER_MASKS):
            peer = _peer(mask)
            there = _index(*peer)
            for a in range(self.n):
                pltpu.make_async_remote_copy(
                    src_ref=self._mine(src_refs, a, me), dst_ref=land_refs[a].at[there],
                    send_sem=send.at[a, k], recv_sem=recv.at[a, k], device_id=peer, device_id_type=MESH).wait_recv()
                pltpu.make_async_remote_copy(
                    src_ref=self._mine(src_refs, a, there), dst_ref=land_refs[a].at[me],
                    send_sem=send.at[a, k], recv_sem=recv.at[a, k], device_id=peer, device_id_type=MESH).wait_send()
        for a in range(self.n):
            pltpu.make_async_copy(self._mine(src_refs, a, me), land_refs[a].at[me], local.at[a]).wait()


def _with_cargo(cargo, refs, n_in, n_out, steps, counts, compute):
    if cargo is None:
        compute(refs)
        return
    n = cargo.n
    src_refs = refs[n_in:n_in + n]
    land_refs = refs[n_in + n + n_out:n_in + 2 * n + n_out]
    sems = refs[-3:]
    first = functools.reduce(jnp.logical_and, [s == 0 for s in steps])
    last = functools.reduce(jnp.logical_and, [s == c - 1 for s, c in zip(steps, counts)])

    @pl.when(first)
    def _():
        cargo.start(src_refs, land_refs, sems)

    compute(refs[:n_in] + refs[n_in + n:n_in + n + n_out] + refs[n_in + 2 * n + n_out:-3])

    @pl.when(last)
    def _():
        cargo.wait(src_refs, land_refs, sems)


def _adamw(parts, w, m, v, name):
    r, c = w.shape
    tr = next((t for t in (512, 384, 352, 256, 128) if r % t == 0), r) if r > 512 else r
    tc = c if tr < r or r <= 512 else 256
    assert r % tr == 0 and c % tc == 0
    bc1 = 1.0 - ADAM_B1 ** ADAM_STEP
    bc2 = 1.0 - ADAM_B2 ** ADAM_STEP

    def body(p_ref, w_ref, m_ref, v_ref, g_ref, d_ref, nm_ref, nv_ref):
        g = p_ref[0].astype(F32)
        for s in range(1, N_DEV):
            g = g + p_ref[s].astype(F32)
        nm = ADAM_B1 * m_ref[...] + (1.0 - ADAM_B1) * g
        nv = ADAM_B2 * v_ref[...] + (1.0 - ADAM_B2) * (g * g)
        g_ref[...] = g
        nm_ref[...] = nm
        nv_ref[...] = nv
        d_ref[...] = -ADAM_LR * ((nm / bc1) / (jnp.sqrt(nv / bc2) + ADAM_EPS) + ADAM_WD * w_ref[...])

    tile = pl.BlockSpec((tr, tc), lambda i, j: (i, j))
    out = jax.ShapeDtypeStruct((r, c), F32)
    return pl.pallas_call(
        body, name=name, grid=(r // tr, c // tc),
        in_specs=[pl.BlockSpec((N_DEV, tr, tc), lambda i, j: (0, i, j)), tile, tile, tile],
        out_specs=[tile] * 4, out_shape=[out] * 4,
        compiler_params=_params(("parallel", "parallel")),
    )(parts, w, m, v)


GATHER_FIRST = ['ffn1_w_in', 'ffn1_w_out']
RIDES = {
    'ffn1_in_fwd': ('weights', ['w_in']),
    'ffn1_out_fwd': ('weights', ['w_q_up', 'w_kv_up', 'w_branch_mla', 'w_branch_sb', 'w_out']),
    'mla_fwd': ('weights', ['ffn2_w_in', 'ffn2_w_out', 'w_ple_gate', 'w_ple_proj']),
    'mla_bwd': ('grads', ['w_ple_gate', 'w_ple_proj', 'ffn2_w_out', 'ffn2_w_in', 'w_out', 'w_branch_mla', 'w_branch_sb']),
    'proj_dw': ('grads', ['w_q_up', 'w_kv_up']),
    'proj_dx': ('grads', ['w_in']),
    'ffn1_in_dw': ('grads', ['ffn1_w_out']),
    'ffn1_in_dx': ('grads', ['ffn1_w_in']),
}


class _Plan:
    def __init__(self, shards):
        self.shards = shards
        self.received = {}

    def first_weights(self):
        gathered = _all_gather([self.shards[n] for n in GATHER_FIRST])
        return {n: _layout_weight(n, g) for n, g in zip(GATHER_FIRST, gathered)}

    def cargo(self, host, dw):
        if host not in RIDES:
            return None
        kind, names = RIDES[host]
        if kind == 'weights':
            return _Cargo([self.shards[n] for n in names], False)
        return _Cargo([_unlayout_grad(n, dw.pop(n)) for n in names], True)

    def landed(self, host, lands):
        if host not in RIDES:
            return {}
        kind, names = RIDES[host]
        if kind == 'weights':
            return {n: _layout_weight(n, land) for n, land in zip(names, lands)}
        self.received.update(zip(names, lands))
        return {}


def _pack_small(vecs):
    flat = jnp.concatenate([v.reshape(-1) for v in vecs])
    return jnp.pad(flat, (0, SMALL_ROWS * 128 - flat.shape[0])).reshape(SMALL_ROWS, 128)


def _unpack_small(packed, sizes):
    flat = packed.reshape(-1)
    out, at = [], 0
    for n in sizes:
        out.append(flat[at:at + n])
        at += n
    return out


def kernel(x, p, positions, ffn1_norm, ffn1_w_in, ffn1_w_out, mix_norm, w_in, q_latent_norm, w_q_up, kv_latent_norm, w_kv_up, q_head_norm, k_head_norm, w_branch_mla, w_branch_sb, w_out, ffn2_norm, ffn2_w_in, ffn2_w_out, ple_norm, w_ple_gate, w_ple_proj, loss_target, m_ffn1_norm, m_ffn1_w_in, m_ffn1_w_out, m_mix_norm, m_w_in, m_q_latent_norm, m_w_q_up, m_kv_latent_norm, m_w_kv_up, m_q_head_norm, m_k_head_norm, m_w_branch_mla, m_w_branch_sb, m_w_out, m_ffn2_norm, m_ffn2_w_in, m_ffn2_w_out, m_ple_norm, m_w_ple_gate, m_w_ple_proj, v_ffn1_norm, v_ffn1_w_in, v_ffn1_w_out, v_mix_norm, v_w_in, v_q_latent_norm, v_w_q_up, v_kv_latent_norm, v_w_kv_up, v_q_head_norm, v_k_head_norm, v_w_branch_mla, v_w_branch_sb, v_w_out, v_ffn2_norm, v_ffn2_w_in, v_ffn2_w_out, v_ple_norm, v_w_ple_gate, v_w_ple_proj):
    given = dict(locals())
    wts = {n: given[n] for n in WEIGHTS}
    mom = {n: given['m_' + n] for n in WEIGHTS}
    var = {n: given['v_' + n] for n in WEIGHTS}

    def local(a, n):
        return jnp.swapaxes(a[0], 0, 1) if n in TRANSPOSED else a[0]

    plan = _Plan({n: local(wts[n], n).astype(BF16) for n in MATS})
    norms = {n: wts[n] for n in NORMS}
    dx, loss_lanes, dw, dn = _local_step(x[0], p[0, 0], positions[0], loss_target[0], norms, plan)
    assert not dw

    out = {}
    for n in MATS:
        res = _adamw(plan.received[n], local(wts[n], n), local(mom[n], n), local(var[n], n), "adamw_" + n)
        out[n] = [local(r[None], n)[None] for r in res]
    small = _pack_small([dn[n] for n in NORMS] + [0.5 / D_MODEL * jnp.sum(loss_lanes)[None]])
    small_parts = _exchange([jnp.broadcast_to(small[None], (N_DEV, SMALL_ROWS, 128))])[0]
    sizes = [wts[n].shape[1] for n in NORMS]
    pack = lambda d: _pack_small([d[n] for n in NORMS])
    small_res = _adamw(small_parts, pack(wts), pack(mom), pack(var), "adamw_norms")
    loss = small_res[0].reshape(-1)[sum(sizes)]
    for i, res in enumerate(small_res):
        for n, vec in zip(NORMS, _unpack_small(res, sizes)):
            out.setdefault(n, [None] * 4)[i] = vec[None]

    return (loss, dx[None], *[out[n][0] for n in WEIGHTS], *[out[n][1] for n in WEIGHTS],
            *[out[n][2] for n in WEIGHTS], *[out[n][3] for n in WEIGHTS])
```

```python
import functools
import math

import jax
import jax.numpy as jnp
from jax import lax
from jax.experimental import pallas as pl
from jax.experimental.pallas import tpu as pltpu

F32 = jnp.float32
BF16 = jnp.bfloat16

N_DEV = 8
D_MODEL = 1024
D_FF = 2816
PLE_DIM = 256
NORM_EPS = 1e-6
N_HEADS = 8
HEAD_PAD = 128
MLA_NOPE = 64
MLA_ROPE = 32
MLA_QK = 96
Q_LORA = 384
KV_LORA = 256
SB_DIM = 64
SB_WIDTH = 512
ROPE_BASE = 10000.0
IN_COLS = 4256

PROJ_W = 4608
P_CQ, P_CKV, P_KR, P_GM, P_GS, P_SBQ, P_SBK, P_SBV = 0, 384, 640, 1024, 2048, 3072, 3584, 4096

ADAM_LR, ADAM_B1, ADAM_B2, ADAM_EPS, ADAM_WD, ADAM_STEP = 0.001, 0.9, 0.999, 1e-08, 0.01, 10

VMEM_LIMIT = 52 * 1024 * 1024
MATMUL_VMEM = 40 * 1024 * 1024

WEIGHTS = ['ffn1_norm', 'ffn1_w_in', 'ffn1_w_out', 'mix_norm', 'w_in', 'q_latent_norm', 'w_q_up',
           'kv_latent_norm', 'w_kv_up', 'q_head_norm', 'k_head_norm', 'w_branch_mla', 'w_branch_sb',
           'w_out', 'ffn2_norm', 'ffn2_w_in', 'ffn2_w_out', 'ple_norm', 'w_ple_gate', 'w_ple_proj']
NORMS = ['ffn1_norm', 'mix_norm', 'q_latent_norm', 'kv_latent_norm', 'q_head_norm', 'k_head_norm',
         'ffn2_norm', 'ple_norm']
MATS = [n for n in WEIGHTS if n not in NORMS]
SMALL_ROWS = 48

NT_DIMS = (((1,), (1,)), ((), ()))
NN_DIMS = (((1,), (0,)), ((), ()))
TN_DIMS = (((0,), (0,)), ((), ()))


def _params(sem=None, vmem=VMEM_LIMIT):
    return pltpu.CompilerParams(dimension_semantics=sem, vmem_limit_bytes=vmem)


def _pick(n, cap):
    if n <= cap:
        return n
    best = None
    for t in range(128, cap + 1, 128):
        if n % t == 0:
            best = t
    assert best is not None, (n, cap)
    return best


def _dot(a, b, dims):
    return lax.dot_general(a, b, dims, preferred_element_type=F32)


def _matmul(a, b, *, mode, out_dtype, name, tm=None, tn=None, tk=None, res=None, alpha=1.0, cargo=None):
    if mode == 'nn':
        (m, k), (k2, n) = a.shape, b.shape
    elif mode == 'nt':
        (m, k), (n, k2) = a.shape, b.shape
    else:
        (k, m), (k2, n) = a.shape, b.shape
    assert k == k2, (name, a.shape, b.shape)
    has_res = res is not None
    tn = tn or _pick(n, 512)

    def vmem(tm_, tk_):
        io = 2 * 2 * (tm_ * tk_ + tk_ * tn) + 2 * tm_ * tn * (jnp.dtype(out_dtype).itemsize + 4 * has_res)
        return io + (4 * tm_ * tn if tk_ < k else 0)

    tries = [(tm_, tk_) for tk_ in ([tk] if tk else [k, _pick(k, 2048)])
             for tm_ in ([tm] if tm else [_pick(m, 2048), _pick(m, 1024), _pick(m, 512)])]
    tm, tk = next((c for c in tries if vmem(*c) <= MATMUL_VMEM), tries[-1])
    assert m % tm == 0 and n % tn == 0 and k % tk == 0, (name, m, n, k, tm, tn, tk)
    nk = k // tk
    dims = {'nn': NN_DIMS, 'nt': NT_DIMS, 'tn': TN_DIMS}[mode]

    def epilogue(acc, r_ref, o_ref):
        if alpha != 1.0:
            acc = acc * alpha
        if has_res:
            acc = r_ref[...] + acc
        o_ref[...] = acc.astype(out_dtype)

    grid = (m // tm, n // tn, nk)

    def body(*refs):
        steps = [pl.program_id(d) for d in range(3)]

        def compute(own):
            a_ref, b_ref = own[0], own[1]
            r_ref = own[2] if has_res else None
            o_ref = own[2 + has_res]
            if nk == 1:
                epilogue(_dot(a_ref[...], b_ref[...], dims), r_ref, o_ref)
                return
            acc_ref = own[-1]

            @pl.when(steps[2] == 0)
            def _():
                acc_ref[...] = jnp.zeros_like(acc_ref)

            acc_ref[...] += _dot(a_ref[...], b_ref[...], dims)

            @pl.when(steps[2] == nk - 1)
            def _():
                epilogue(acc_ref[...], r_ref, o_ref)

        _with_cargo(cargo, refs, 2 + has_res, 1, steps, grid, compute)

    if mode == 'tn':
        a_spec = pl.BlockSpec((tk, tm), lambda i, j, kk: (kk, i))
    else:
        a_spec = pl.BlockSpec((tm, tk), lambda i, j, kk: (i, kk))
    if mode == 'nt':
        b_spec = pl.BlockSpec((tn, tk), lambda i, j, kk: (j, kk))
    else:
        b_spec = pl.BlockSpec((tk, tn), lambda i, j, kk: (kk, j))
    o_spec = pl.BlockSpec((tm, tn), lambda i, j, kk: (i, j))
    in_specs = [a_spec, b_spec] + ([o_spec] if has_res else [])
    args = (a, b) + ((res,) if has_res else ())
    out_shape = jax.ShapeDtypeStruct((m, n), out_dtype)
    scratch = [pltpu.VMEM((tm, tn), F32)] if nk > 1 else []
    if cargo is None:
        return pl.pallas_call(
            body, name=name, grid=grid, in_specs=in_specs, out_specs=o_spec, out_shape=out_shape,
            scratch_shapes=scratch, compiler_params=_params(("parallel", "parallel", "arbitrary")),
        )(*args)
    outs = pl.pallas_call(
        body, name=name, grid=grid, in_specs=in_specs + cargo.specs(), out_specs=[o_spec] + cargo.specs(),
        out_shape=[out_shape] + cargo.out_shape(), scratch_shapes=scratch + cargo.scratch(),
        compiler_params=_params(("arbitrary", "arbitrary", "arbitrary")),
    )(*args, *cargo.srcs)
    return outs[0], list(outs[1:])


def _row_tile(t, cap=512):
    return min(t, cap)


def _rms(x, width):
    return lax.rsqrt(jnp.sum(x * x, axis=-1, keepdims=True) * (1.0 / width) + NORM_EPS)


def _rmsnorm_fwd(x, g, name):
    t, d = x.shape
    tr = _row_tile(t)

    def body(x_ref, g_ref, o_ref):
        xv = x_ref[...]
        o_ref[...] = ((xv * _rms(xv, d)) * g_ref[...]).astype(BF16)

    return pl.pallas_call(
        body, name=name, grid=(t // tr,),
        in_specs=[pl.BlockSpec((tr, d), lambda i: (i, 0)), pl.BlockSpec((1, d), lambda i: (0, 0))],
        out_specs=pl.BlockSpec((tr, d), lambda i: (i, 0)),
        out_shape=jax.ShapeDtypeStruct((t, d), BF16),
        compiler_params=_params(("parallel",)),
    )(x, g)


def _rmsnorm_bwd(dn, x, g, dh_in, name, out_scale):
    t, d = x.shape
    tr = _row_tile(t, 256)

    def body(dn_ref, x_ref, g_ref, dhin_ref, dh_ref, dhb_ref, dg_ref):
        i = pl.program_id(0)
        xv = x_ref[...]
        dnv = dn_ref[...]
        r = _rms(xv, d)
        y = xv * r
        dy = dnv * g_ref[...]
        dx = r * (dy - y * (jnp.sum(dy * y, axis=-1, keepdims=True) * (1.0 / d)))
        dh = dhin_ref[...] + dx
        dh_ref[...] = dh
        dhb_ref[...] = (dh * out_scale).astype(BF16)
        part = jnp.sum(dnv * y, axis=0, keepdims=True)

        @pl.when(i == 0)
        def _():
            dg_ref[...] = part

        @pl.when(i > 0)
        def _():
            dg_ref[...] += part

    row = pl.BlockSpec((tr, d), lambda i: (i, 0))
    vec = pl.BlockSpec((1, d), lambda i: (0, 0))
    return pl.pallas_call(
        body, name=name, grid=(t // tr,),
        in_specs=[row, row, vec, row], out_specs=[row, row, vec],
        out_shape=[jax.ShapeDtypeStruct((t, d), F32), jax.ShapeDtypeStruct((t, d), BF16),
                   jax.ShapeDtypeStruct((1, d), F32)],
        compiler_params=_params(("arbitrary",)),
    )(dn, x, g, dh_in)


def _sigmoid(x):
    return 1.0 / (1.0 + jnp.exp(-x))


def _swiglu_fwd(ab, name):
    t = ab.shape[0]
    tr = _row_tile(t)

    def body(a_ref, b_ref, o_ref):
        a = a_ref[...].astype(F32)
        o_ref[...] = (a * _sigmoid(a) * b_ref[...].astype(F32)).astype(BF16)

    return pl.pallas_call(
        body, name=name, grid=(t // tr,),
        in_specs=[pl.BlockSpec((tr, D_FF), lambda i: (i, 0)), pl.BlockSpec((tr, D_FF), lambda i: (i, 1))],
        out_specs=pl.BlockSpec((tr, D_FF), lambda i: (i, 0)),
        out_shape=jax.ShapeDtypeStruct((t, D_FF), BF16),
        compiler_params=_params(("parallel",)),
    )(ab, ab)


def _swiglu_bwd(ab, dact, name):
    t = ab.shape[0]
    tr = _row_tile(t, 256)

    def body(ab_ref, d_ref, o_ref):
        a = ab_ref[:, :D_FF].astype(F32)
        b = ab_ref[:, D_FF:].astype(F32)
        dv = d_ref[...].astype(F32)
        s = _sigmoid(a)
        o_ref[:, :D_FF] = (dv * b * s * (1.0 + a * (1.0 - s))).astype(BF16)
        o_ref[:, D_FF:] = (dv * a * s).astype(BF16)

    return pl.pallas_call(
        body, name=name, grid=(t // tr,),
        in_specs=[pl.BlockSpec((tr, 2 * D_FF), lambda i: (i, 0)), pl.BlockSpec((tr, D_FF), lambda i: (i, 0))],
        out_specs=pl.BlockSpec((tr, 2 * D_FF), lambda i: (i, 0)),
        out_shape=jax.ShapeDtypeStruct((t, 2 * D_FF), BF16),
        compiler_params=_params(("parallel",)),
    )(ab, dact)


def _latent_fwd(proj, gq, gkv):
    t = proj.shape[0]
    tr = _row_tile(t)

    def body(p_ref, gq_ref, gkv_ref, cq_ref, ckv_ref):
        cq = p_ref[:, P_CQ:P_CQ + Q_LORA].astype(F32)
        ckv = p_ref[:, P_CKV:P_CKV + KV_LORA].astype(F32)
        cq_ref[...] = ((cq * _rms(cq, Q_LORA)) * gq_ref[...]).astype(BF16)
        ckv_ref[...] = ((ckv * _rms(ckv, KV_LORA)) * gkv_ref[...]).astype(BF16)

    return pl.pallas_call(
        body, name="latent_fwd", grid=(t // tr,),
        in_specs=[pl.BlockSpec((tr, 1024), lambda i: (i, 0)), pl.BlockSpec((1, Q_LORA), lambda i: (0, 0)),
                  pl.BlockSpec((1, KV_LORA), lambda i: (0, 0))],
        out_specs=[pl.BlockSpec((tr, Q_LORA), lambda i: (i, 0)), pl.BlockSpec((tr, KV_LORA), lambda i: (i, 0))],
        out_shape=[jax.ShapeDtypeStruct((t, Q_LORA), BF16), jax.ShapeDtypeStruct((t, KV_LORA), BF16)],
        compiler_params=_params(("parallel",)),
    )(proj, gq, gkv)


def _latent_bwd(dcqn, dckvn, proj, dkr, gq, gkv):
    t = proj.shape[0]
    tr = _row_tile(t, 256)

    def norm_bwd(dn, x, g, width):
        r = _rms(x, width)
        y = x * r
        dy = dn * g
        dx = r * (dy - y * (jnp.sum(dy * y, axis=-1, keepdims=True) * (1.0 / width)))
        return dx, jnp.sum(dn * y, axis=0, keepdims=True)

    def body(dcq_ref, dckv_ref, p_ref, dkr_ref, gq_ref, gkv_ref, o_ref, dgq_ref, dgkv_ref):
        i = pl.program_id(0)
        dcq, pq = norm_bwd(dcq_ref[...], p_ref[:, P_CQ:P_CQ + Q_LORA].astype(F32), gq_ref[...], Q_LORA)
        dckv, pkv = norm_bwd(dckv_ref[...], p_ref[:, P_CKV:P_CKV + KV_LORA].astype(F32), gkv_ref[...], KV_LORA)
        o_ref[:, P_CQ:P_CQ + Q_LORA] = dcq.astype(BF16)
        o_ref[:, P_CKV:P_CKV + KV_LORA] = dckv.astype(BF16)
        o_ref[:, P_KR:P_KR + 128] = dkr_ref[...].astype(BF16)
        o_ref[:, P_KR + 128:1024] = jnp.zeros((tr, 1024 - P_KR - 128), BF16)

        @pl.when(i == 0)
        def _():
            dgq_ref[...] = pq
            dgkv_ref[...] = pkv

        @pl.when(i > 0)
        def _():
            dgq_ref[...] += pq
            dgkv_ref[...] += pkv

    def row(w):
        return pl.BlockSpec((tr, w), lambda i: (i, 0))

    def vec(w):
        return pl.BlockSpec((1, w), lambda i: (0, 0))

    return pl.pallas_call(
        body, name="latent_bwd", grid=(t // tr,),
        in_specs=[row(Q_LORA), row(KV_LORA), row(1024), row(128), vec(Q_LORA), vec(KV_LORA)],
        out_specs=[row(1024), vec(Q_LORA), vec(KV_LORA)],
        out_shape=[jax.ShapeDtypeStruct((t, 1024), BF16), jax.ShapeDtypeStruct((1, Q_LORA), F32),
                   jax.ShapeDtypeStruct((1, KV_LORA), F32)],
        compiler_params=_params(("arbitrary",)),
    )(dcqn, dckvn, proj, dkr, gq, gkv)


def _rope(y, cosf, sin_a, sin_b):
    return y * cosf + pltpu.roll(y, 112, 1) * sin_a + pltpu.roll(y, 16, 1) * sin_b


def _rope_t(d, cosf, sin_a, sin_b):
    return d * cosf + pltpu.roll(d * sin_a, 16, 1) + pltpu.roll(d * sin_b, 112, 1)


def _headprep_fwd(qraw, kvraw, proj, cosf, sin_a, sin_b, gqh, gkh):
    t = qraw.shape[0]
    tr = _row_tile(t, 256)

    def body(q_ref, kv_ref, kr_ref, c_ref, sa_ref, sb_ref, gq_ref, gk_ref, qh_ref, kh_ref, kvb_ref):
        cv, sa, sb = c_ref[...], sa_ref[...], sb_ref[...]
        kr = kr_ref[...].astype(F32)
        lane = lax.broadcasted_iota(jnp.int32, (tr, HEAD_PAD), 1)
        for h in range(N_HEADS):
            cols = slice(h * HEAD_PAD, (h + 1) * HEAD_PAD)
            xq = q_ref[:, cols]
            yq = (xq * _rms(xq, MLA_QK)) * gq_ref[...]
            qh_ref[:, cols] = _rope(yq, cv, sa, sb).astype(BF16)
            kvh = kv_ref[:, cols]
            kvb_ref[:, cols] = jnp.where(lane < MLA_NOPE, 1.0, kvh).astype(BF16)
            xk = jnp.where(lane < MLA_NOPE, kvh, kr)
            yk = (xk * _rms(xk, MLA_QK)) * gk_ref[...]
            kh_ref[:, cols] = _rope(yk, cv, sa, sb).astype(BF16)

    wide = pl.BlockSpec((tr, 1024), lambda i: (i, 0))
    lanes = pl.BlockSpec((tr, HEAD_PAD), lambda i: (i, 0))
    vec = pl.BlockSpec((1, HEAD_PAD), lambda i: (0, 0))
    return pl.pallas_call(
        body, name="headprep_fwd", grid=(t // tr,),
        in_specs=[wide, wide, pl.BlockSpec((tr, HEAD_PAD), lambda i: (i, P_KR // HEAD_PAD)), lanes, lanes, lanes, vec, vec],
        out_specs=[wide, wide, wide],
        out_shape=[jax.ShapeDtypeStruct((t, 1024), BF16)] * 3,
        compiler_params=_params(("parallel",)),
    )(qraw, kvraw, proj, cosf, sin_a, sin_b, gqh, gkh)


def _headprep_bwd(dqh, dkh, dvp, qraw, kvraw, proj, cosf, sin_a, sin_b, gqh, gkh):
    t = qraw.shape[0]
    tr = _row_tile(t, 256)

    def norm_bwd(dn, x, g):
        r = _rms(x, MLA_QK)
        y = x * r
        dy = dn * g
        dx = r * (dy - y * (jnp.sum(dy * y, axis=-1, keepdims=True) * (1.0 / MLA_QK)))
        return dx, jnp.sum(dn * y, axis=0, keepdims=True)

    def body(dq_ref, dk_ref, dv_ref, q_ref, kv_ref, kr_ref, c_ref, sa_ref, sb_ref, gq_ref, gk_ref,
             dqr_ref, dkvr_ref, dkr_ref, dgq_ref, dgk_ref):
        i = pl.program_id(0)
        cv, sa, sb = c_ref[...], sa_ref[...], sb_ref[...]
        kr = kr_ref[...].astype(F32)
        lane = lax.broadcasted_iota(jnp.int32, (tr, HEAD_PAD), 1)
        dkr = jnp.zeros((tr, HEAD_PAD), F32)
        pq = jnp.zeros((1, HEAD_PAD), F32)
        pk = jnp.zeros((1, HEAD_PAD), F32)
        for h in range(N_HEADS):
            cols = slice(h * HEAD_PAD, (h + 1) * HEAD_PAD)
            dxq, pqh = norm_bwd(_rope_t(dq_ref[:, cols], cv, sa, sb), q_ref[:, cols], gq_ref[...])
            dqr_ref[:, cols] = dxq.astype(BF16)
            pq = pq + pqh
            kvh = kv_ref[:, cols]
            xk = jnp.where(lane < MLA_NOPE, kvh, kr)
            dxk, pkh = norm_bwd(_rope_t(dk_ref[:, cols], cv, sa, sb), xk, gk_ref[...])
            pk = pk + pkh
            dkvr_ref[:, cols] = jnp.where(lane < MLA_NOPE, dxk, dv_ref[:, cols]).astype(BF16)
            dkr = dkr + jnp.where(lane < MLA_NOPE, 0.0, dxk)
        dkr_ref[...] = dkr

        @pl.when(i == 0)
        def _():
            dgq_ref[...] = pq
            dgk_ref[...] = pk

        @pl.when(i > 0)
        def _():
            dgq_ref[...] += pq
            dgk_ref[...] += pk

    wide = pl.BlockSpec((tr, 1024), lambda i: (i, 0))
    lanes = pl.BlockSpec((tr, HEAD_PAD), lambda i: (i, 0))
    vec = pl.BlockSpec((1, HEAD_PAD), lambda i: (0, 0))
    return pl.pallas_call(
        body, name="headprep_bwd", grid=(t // tr,),
        in_specs=[wide, wide, wide, wide, wide, pl.BlockSpec((tr, HEAD_PAD), lambda i: (i, P_KR // HEAD_PAD)),
                  lanes, lanes, lanes, vec, vec],
        out_specs=[wide, wide, lanes, vec, vec],
        out_shape=[jax.ShapeDtypeStruct((t, 1024), BF16), jax.ShapeDtypeStruct((t, 1024), BF16),
                   jax.ShapeDtypeStruct((t, HEAD_PAD), F32), jax.ShapeDtypeStruct((1, HEAD_PAD), F32),
                   jax.ShapeDtypeStruct((1, HEAD_PAD), F32)],
        compiler_params=_params(("arbitrary",)),
    )(dqh, dkh, dvp, qraw, kvraw, proj, cosf, sin_a, sin_b, gqh, gkh)


def _merge_fwd(proj, bm, bs):
    t = proj.shape[0]
    tr = _row_tile(t, 256)

    def body(gm_ref, gs_ref, bm_ref, bs_ref, o_ref):
        gm = _sigmoid(gm_ref[...].astype(F32))
        gs = _sigmoid(gs_ref[...].astype(F32))
        o_ref[...] = (gm * bm_ref[...] + gs * bs_ref[...]).astype(BF16)

    row = pl.BlockSpec((tr, 1024), lambda i: (i, 0))
    return pl.pallas_call(
        body, name="merge_fwd", grid=(t // tr,),
        in_specs=[pl.BlockSpec((tr, 1024), lambda i: (i, P_GM // 1024)),
                  pl.BlockSpec((tr, 1024), lambda i: (i, P_GS // 1024)), row, row],
        out_specs=row, out_shape=jax.ShapeDtypeStruct((t, 1024), BF16),
        compiler_params=_params(("parallel",)),
    )(proj, proj, bm, bs)


def _merge_bwd(dmerged, proj, bm, bs):
    t = proj.shape[0]
    tr = _row_tile(t, 256)

    def body(dm_ref, gm_ref, gs_ref, bm_ref, bs_ref, dbm_ref, dbs_ref, dg_ref):
        dm = dm_ref[...]
        gm = _sigmoid(gm_ref[...].astype(F32))
        gs = _sigmoid(gs_ref[...].astype(F32))
        dbm_ref[...] = (dm * gm).astype(BF16)
        dbs_ref[...] = (dm * gs).astype(BF16)
        dg_ref[:, :1024] = (dm * bm_ref[...] * gm * (1.0 - gm)).astype(BF16)
        dg_ref[:, 1024:] = (dm * bs_ref[...] * gs * (1.0 - gs)).astype(BF16)

    row = pl.BlockSpec((tr, 1024), lambda i: (i, 0))
    return pl.pallas_call(
        body, name="merge_bwd", grid=(t // tr,),
        in_specs=[row, pl.BlockSpec((tr, 1024), lambda i: (i, P_GM // 1024)),
                  pl.BlockSpec((tr, 1024), lambda i: (i, P_GS // 1024)), row, row],
        out_specs=[row, row, pl.BlockSpec((tr, 2048), lambda i: (i, 0))],
        out_shape=[jax.ShapeDtypeStruct((t, 1024), BF16), jax.ShapeDtypeStruct((t, 1024), BF16),
                   jax.ShapeDtypeStruct((t, 2048), BF16)],
        compiler_params=_params(("parallel",)),
    )(dmerged, proj, proj, bm, bs)


def _ple_loss(h3, zg, pp, tgt):
    t = h3.shape[0]
    tr = _row_tile(t, 256)

    def body(h_ref, z_ref, p_ref, t_ref, dh_ref, dz_ref, dp_ref, l_ref):
        i = pl.program_id(0)
        pg = _sigmoid(z_ref[...])
        ppv = p_ref[...]
        diff = (h_ref[...] + pg * ppv) - t_ref[...]
        dh = diff * (1.0 / D_MODEL)
        dh_ref[...] = dh
        dp_ref[...] = (dh * pg).astype(BF16)
        dz_ref[...] = (dh * ppv * pg * (1.0 - pg)).astype(BF16)
        sq = jnp.sum(diff * diff, axis=0, keepdims=True)
        part = sq[:, 0:128]
        for c in range(1, D_MODEL // 128):
            part = part + sq[:, c * 128:(c + 1) * 128]

        @pl.when(i == 0)
        def _():
            l_ref[...] = part

        @pl.when(i > 0)
        def _():
            l_ref[...] += part

    row = pl.BlockSpec((tr, 1024), lambda i: (i, 0))
    return pl.pallas_call(
        body, name="ple_loss", grid=(t // tr,),
        in_specs=[row, row, row, row],
        out_specs=[row, row, row, pl.BlockSpec((1, 128), lambda i: (0, 0))],
        out_shape=[jax.ShapeDtypeStruct((t, 1024), F32), jax.ShapeDtypeStruct((t, 1024), BF16),
                   jax.ShapeDtypeStruct((t, 1024), BF16), jax.ShapeDtypeStruct((1, 128), F32)],
        compiler_params=_params(("arbitrary",)),
    )(h3, zg, pp, tgt)


ATT_BLOCK = 256
ATT_COLS = 2
MLA_FWD_COLS = 4


def _split_bf16(x):
    hi = x.astype(BF16)
    return hi, (x - hi.astype(F32)).astype(BF16)


def _tri(kind):
    r = lax.broadcasted_iota(jnp.int32, (ATT_BLOCK, ATT_BLOCK), 0)
    c = lax.broadcasted_iota(jnp.int32, (ATT_BLOCK, ATT_BLOCK), 1)
    cond = {'gt': r > c, 'le': r <= c, 'lt': r < c}[kind]
    return jnp.where(cond, 1.0, 0.0).astype(BF16)


def _causal(strict):
    r = lax.broadcasted_iota(jnp.int32, (ATT_BLOCK, ATT_BLOCK), 0)
    c = lax.broadcasted_iota(jnp.int32, (ATT_BLOCK, ATT_BLOCK), 1)
    return (c < r) if strict else (c <= r)


def _lanes(c):
    return slice(c * HEAD_PAD, (c + 1) * HEAD_PAD)


def _row_block(j):
    return pl.ds(pl.multiple_of(j * ATT_BLOCK, ATT_BLOCK), ATT_BLOCK)


def _rows(ref, j, c):
    return ref[_row_block(j), _lanes(c)]


def _mla_fwd(qh, kh, kvb, cargo=None):
    t = qh.shape[0]
    bq = ATT_BLOCK
    scale = 1.0 / math.sqrt(MLA_QK)
    ncol = MLA_FWD_COLS
    grid = (N_HEADS // ncol, t // bq)

    def body(*refs):
        steps = [pl.program_id(0), pl.program_id(1)]
        _with_cargo(cargo, refs, 3, 2, steps, grid, lambda own: work(steps[1], *own))

    def work(i, q_ref, k_ref, v_ref, o_ref, lse_ref):
        qs = [q_ref[:, _lanes(c)] for c in range(ncol)]

        def step(j, carry, masked):
            cols = range(ncol)
            scores = [_dot(qs[c], _rows(k_ref, j, c), NT_DIMS) for c in cols]
            ms, ps, alphas = [], [], []
            for c in cols:
                s = scores[c] * scale
                if masked:
                    s = jnp.where(_causal(False), s, -1e30)
                m_new = jnp.maximum(carry[c][0], jnp.max(s, axis=-1, keepdims=True))
                ps.append(jnp.exp(s - m_new).astype(BF16))
                alphas.append(jnp.exp(carry[c][0] - m_new))
                ms.append(m_new)
            return tuple((ms[c], alphas[c] * carry[c][1] + _dot(ps[c], _rows(v_ref, j, c), NN_DIMS)) for c in cols)

        init = tuple((jnp.full((bq, 1), -1e30, F32), jnp.zeros((bq, HEAD_PAD), F32)) for _ in range(ncol))
        carry = lax.fori_loop(0, i, lambda j, cr: step(j, cr, False), init)
        for c, (m, acc) in enumerate(step(i, carry, True)):
            l = acc[:, 0:1]
            o_ref[:, _lanes(c)] = (acc / l).astype(BF16)
            lse_ref[c] = m + jnp.log(l)

    width = ncol * HEAD_PAD
    full = pl.BlockSpec((t, width), lambda h, i: (0, h))
    blk = pl.BlockSpec((bq, width), lambda h, i: (i, h))
    extra = cargo.specs() if cargo else []
    outs = pl.pallas_call(
        body, name="mla_fwd", grid=grid,
        in_specs=[blk, full, full] + extra,
        out_specs=[blk, pl.BlockSpec((ncol, bq, 1), lambda h, i: (h, i, 0))] + extra,
        out_shape=[jax.ShapeDtypeStruct((t, N_HEADS * HEAD_PAD), BF16), jax.ShapeDtypeStruct((N_HEADS, t, 1), F32)]
        + (cargo.out_shape() if cargo else []),
        scratch_shapes=cargo.scratch() if cargo else [],
        compiler_params=_params(("arbitrary", "arbitrary")),
    )(qh, kh, kvb, *(cargo.srcs if cargo else []))
    return (outs[0], outs[1], list(outs[2:])) if cargo else outs


def _mla_bwd(qh, kh, kvb, o, do, lse, cargo=None):
    t = qh.shape[0]
    bq = ATT_BLOCK
    scale = 1.0 / math.sqrt(MLA_QK)
    grid = (N_HEADS // ATT_COLS, t // bq)

    def body(*refs):
        steps = [pl.program_id(0), pl.program_id(1)]
        _with_cargo(cargo, refs, 6, 3, steps, grid, lambda own: work(steps[1], *own))

    def work(i, q_ref, k_ref, v_ref, o_ref, do_ref, lse_ref, dq_ref, dk_ref, dv_ref):

        @pl.when(i == 0)
        def _():
            dk_ref[...] = jnp.zeros_like(dk_ref)
            dv_ref[...] = jnp.zeros_like(dv_ref)

        qs = [q_ref[:, _lanes(c)] for c in range(ATT_COLS)]
        dos = [do_ref[:, _lanes(c)] for c in range(ATT_COLS)]
        deltas = [jnp.sum(dos[c].astype(F32) * o_ref[:, _lanes(c)].astype(F32), axis=-1, keepdims=True)
                  for c in range(ATT_COLS)]
        lses = [lse_ref[c] for c in range(ATT_COLS)]

        def step(j, dqs, masked):
            cols = range(ATT_COLS)
            kbs = [_rows(k_ref, j, c) for c in cols]
            scores = [_dot(qs[c], kbs[c], NT_DIMS) for c in cols]
            dps = [_dot(dos[c], _rows(v_ref, j, c), NT_DIMS) for c in cols]
            pbs, dss = [], []
            for c in cols:
                p = jnp.exp(scores[c] * scale - lses[c])
                if masked:
                    p = jnp.where(_causal(False), p, 0.0)
                pbs.append(p.astype(BF16))
                dss.append((p * (dps[c] - deltas[c]) * scale).astype(BF16))
            for c in cols:
                dv_ref[_row_block(j), _lanes(c)] += _dot(pbs[c], dos[c], TN_DIMS)
                dk_ref[_row_block(j), _lanes(c)] += _dot(dss[c], qs[c], TN_DIMS)
            return tuple(dqs[c] + _dot(dss[c], kbs[c], NN_DIMS) for c in cols)

        init = tuple(jnp.zeros((bq, HEAD_PAD), F32) for _ in range(ATT_COLS))
        dqs = lax.fori_loop(0, i, lambda j, cr: step(j, cr, False), init)
        for c, dq in enumerate(step(i, dqs, True)):
            dq_ref[:, _lanes(c)] = dq

    width = ATT_COLS * HEAD_PAD
    full = pl.BlockSpec((t, width), lambda h, i: (0, h))
    blk = pl.BlockSpec((bq, width), lambda h, i: (i, h))
    wide = jax.ShapeDtypeStruct((t, N_HEADS * HEAD_PAD), F32)
    extra = cargo.specs() if cargo else []
    outs = pl.pallas_call(
        body, name="mla_bwd", grid=grid,
        in_specs=[blk, full, full, blk, blk, pl.BlockSpec((ATT_COLS, bq, 1), lambda h, i: (h, i, 0))] + extra,
        out_specs=[blk, full, full] + extra,
        out_shape=[wide, wide, wide] + (cargo.out_shape() if cargo else []),
        scratch_shapes=cargo.scratch() if cargo else [],
        compiler_params=_params(("arbitrary", "arbitrary")),
    )(qh, kh, kvb, o, do, lse, *(cargo.srcs if cargo else []))
    return (outs[0], outs[1], outs[2], list(outs[3:])) if cargo else outs


def _head_only(x, lane, u):
    return jnp.where((lane >= u * SB_DIM) & (lane < (u + 1) * SB_DIM), x, jnp.zeros_like(x))


SB_DEAD = -104.0


def _log_sigmoids(z):
    e = jnp.exp(-jnp.abs(z))
    lg = jnp.log(1.0 + e)
    ls_pos = jnp.minimum(z, 0.0) - lg
    return ls_pos, ls_pos - z, e


def _sb_fwd(proj):
    t = proj.shape[0]
    bq = ATT_BLOCK
    nq = t // bq
    scale = 1.0 / math.sqrt(SB_DIM)
    pairs = SB_WIDTH // HEAD_PAD

    def body(q_ref, k_ref, v_ref, o_ref, r_ref, first_ref):
        g, i = pl.program_id(0), pl.program_id(1)
        lane = lax.broadcasted_iota(jnp.int32, (bq, HEAD_PAD), 1)
        upper = _tri('gt')
        chains = [(c, u) for c in range(ATT_COLS) for u in range(2)]
        qms = [_head_only(q_ref[:, _lanes(c)], lane, u) * scale for c, u in chains]

        def step(j, carry, masked):
            ids = range(len(chains))
            zs = [_dot(qms[n], _rows(k_ref, j, chains[n][0]), NT_DIMS) for n in ids]
            pos, neg, parts = [], [], []
            for n in ids:
                ls_pos, ls_neg, _ = _log_sigmoids(zs[n])
                if masked:
                    ls_neg = jnp.where(_causal(True), ls_neg, 0.0)
                pos.append(ls_pos)
                neg.append(ls_neg)
                parts.append(_split_bf16(ls_neg))
            suffix = [_dot(parts[n][0], upper, NN_DIMS) + _dot(parts[n][1], upper, NN_DIMS) for n in ids]
            weights = []
            for n in ids:
                a = jnp.exp(pos[n] + suffix[n] + carry[n][0])
                if masked:
                    a = jnp.where(_causal(True), a, 0.0)
                weights.append(a.astype(BF16))
            return tuple((carry[n][0] + jnp.sum(neg[n], axis=-1, keepdims=True),
                          carry[n][1] + _dot(weights[n], _rows(v_ref, j, chains[n][0]), NN_DIMS)) for n in ids)

        init = tuple((jnp.zeros((bq, 1), F32), jnp.zeros((bq, HEAD_PAD), F32)) for _ in chains)
        carry = step(i, init, True)

        def more(state):
            s, cr = state
            live = cr[0][0]
            for n in range(1, len(chains)):
                live = jnp.maximum(live, cr[n][0])
            return jnp.logical_and(s < i, jnp.max(live) > SB_DEAD)

        walked, carry = lax.while_loop(more, lambda st: (st[0] + 1, step(i - 1 - st[0], st[1], False)),
                                       (jnp.int32(0), carry))
        first_ref[g * nq + i] = i - walked
        for n, (c, u) in enumerate(chains):
            r_ref[2 * c + u] = carry[n][0]
        for c in range(ATT_COLS):
            o_ref[:, _lanes(c)] = jnp.where(lane < SB_DIM, carry[2 * c][1], carry[2 * c + 1][1]).astype(BF16)

    width = ATT_COLS * HEAD_PAD

    def full(c0):
        return pl.BlockSpec((t, width), lambda g, i: (0, c0 // width + g))

    return pl.pallas_call(
        body, name="sb_fwd", grid=(pairs // ATT_COLS, t // bq),
        in_specs=[pl.BlockSpec((bq, width), lambda g, i: (i, P_SBQ // width + g)), full(P_SBK), full(P_SBV)],
        out_specs=[pl.BlockSpec((bq, width), lambda g, i: (i, g)),
                   pl.BlockSpec((2 * ATT_COLS, bq, 1), lambda g, i: (g, i, 0)),
                   pl.BlockSpec(memory_space=pltpu.SMEM)],
        out_shape=[jax.ShapeDtypeStruct((t, SB_WIDTH), BF16), jax.ShapeDtypeStruct((N_HEADS, t, 1), F32),
                   jax.ShapeDtypeStruct((pairs // ATT_COLS * nq,), jnp.int32)],
        compiler_params=_params(("arbitrary", "arbitrary")),
    )(proj, proj, proj)


def _sb_bwd(proj, do, rtot, first):
    t = proj.shape[0]
    bq = ATT_BLOCK
    nq = t // bq
    scale = 1.0 / math.sqrt(SB_DIM)
    pairs = SB_WIDTH // HEAD_PAD

    def body(first_ref, q_ref, k_ref, v_ref, do_ref, r_ref, dq_ref, dk_ref, dv_ref):
        g, i = pl.program_id(0), pl.program_id(1)

        @pl.when(i == 0)
        def _():
            dk_ref[...] = jnp.zeros_like(dk_ref)
            dv_ref[...] = jnp.zeros_like(dv_ref)

        lane = lax.broadcasted_iota(jnp.int32, (bq, HEAD_PAD), 1)
        incl = _tri('le')
        excl = _tri('lt')
        chains = [(c, u) for c in range(ATT_COLS) for u in range(2)]
        qms = [_head_only(q_ref[:, _lanes(c)], lane, u) * scale for c, u in chains]
        doms = [_head_only(do_ref[:, _lanes(c)], lane, u) for c, u in chains]
        rts = [r_ref[2 * c + u] for c, u in chains]

        def step(j, carry, masked):
            ids = range(len(chains))
            kbs = [_rows(k_ref, j, c) for c in range(ATT_COLS)]
            zs = [_dot(qms[n], kbs[chains[n][0]], NT_DIMS) for n in ids]
            das = [_dot(doms[n], _rows(v_ref, j, chains[n][0]), NT_DIMS) for n in ids]
            pos, neg, sigs, parts = [], [], [], []
            for n in ids:
                ls_pos, ls_neg, e = _log_sigmoids(zs[n])
                if masked:
                    ls_neg = jnp.where(_causal(True), ls_neg, 0.0)
                pos.append(ls_pos)
                neg.append(ls_neg)
                sigs.append(jnp.where(zs[n] >= 0.0, 1.0, e) * pl.reciprocal(1.0 + e, approx=True))
                parts.append(_split_bf16(ls_neg))
            prefix = [_dot(parts[n][0], incl, NN_DIMS) + _dot(parts[n][1], incl, NN_DIMS) for n in ids]
            evs, eparts, dvs = [], [], []
            for n in ids:
                a = jnp.exp(pos[n] + (rts[n] - (carry[n][0] + prefix[n])))
                if masked:
                    a = jnp.where(_causal(True), a, 0.0)
                dvs.append(_dot(a.astype(BF16), doms[n], TN_DIMS))
                evs.append(a * das[n])
                eparts.append(evs[n].astype(BF16))
            before = [_dot(eparts[n], excl, NN_DIMS) for n in ids]
            out, dks = [], []
            for n in ids:
                dz = evs[n] - sigs[n] * (evs[n] + (carry[n][1] + before[n]))
                if masked:
                    dz = jnp.where(_causal(True), dz, 0.0)
                dzb = dz.astype(BF16)
                dks.append(_dot(dzb, qms[n], TN_DIMS))
                out.append((carry[n][0] + jnp.sum(neg[n], axis=-1, keepdims=True),
                            carry[n][1] + jnp.sum(evs[n], axis=-1, keepdims=True),
                            carry[n][2] + _dot(dzb, kbs[chains[n][0]], NN_DIMS)))
            for c in range(ATT_COLS):
                dv_ref[_row_block(j), _lanes(c)] += dvs[2 * c] + dvs[2 * c + 1]
                dk_ref[_row_block(j), _lanes(c)] += dks[2 * c] + dks[2 * c + 1]
            return tuple(out)

        init = tuple((jnp.zeros((bq, 1), F32), jnp.zeros((bq, 1), F32), jnp.zeros((bq, HEAD_PAD), F32)) for _ in chains)
        carry = lax.fori_loop(first_ref[g * nq + i], i, lambda j, cr: step(j, cr, False), init)
        carry = step(i, carry, True)
        for c in range(ATT_COLS):
            dq_ref[:, _lanes(c)] = jnp.where(lane < SB_DIM, carry[2 * c][2], carry[2 * c + 1][2]) * scale

    width = ATT_COLS * HEAD_PAD

    def full(c0):
        return pl.BlockSpec((t, width), lambda g, i, first: (0, c0 // width + g))

    blk = pl.BlockSpec((bq, width), lambda g, i, first: (i, g))
    acc = pl.BlockSpec((t, width), lambda g, i, first: (0, g))
    wide = jax.ShapeDtypeStruct((t, SB_WIDTH), F32)
    return pl.pallas_call(
        body, name="sb_bwd",
        grid_spec=pltpu.PrefetchScalarGridSpec(
            num_scalar_prefetch=1, grid=(pairs // ATT_COLS, nq),
            in_specs=[pl.BlockSpec((bq, width), lambda g, i, first: (i, P_SBQ // width + g)), full(P_SBK), full(P_SBV),
                      blk, pl.BlockSpec((2 * ATT_COLS, bq, 1), lambda g, i, first: (g, i, 0))],
            out_specs=[blk, acc, acc]),
        out_shape=[wide, wide, wide],
        compiler_params=_params(("arbitrary", "arbitrary")),
    )(first, proj, proj, proj, do, rtot)


def _cols_to_full(g):
    n, r, c = g.shape
    return jnp.transpose(g, (1, 0, 2)).reshape(r, n * c)


def _full_to_cols(w):
    r, c = w.shape
    return jnp.transpose(w.reshape(r, N_DEV, c // N_DEV), (1, 0, 2))


TRANSPOSED = ('ffn1_w_in', 'ffn2_w_in', 'w_in', 'w_q_up')


def _layout_weight(name, g):
    if name in ('ffn1_w_out', 'ffn2_w_out', 'w_out', 'w_ple_gate', 'ffn1_w_in', 'ffn2_w_in'):
        return g.reshape(g.shape[0] * g.shape[1], g.shape[2])
    if name == 'w_in':
        wt = g.reshape(IN_COLS, D_MODEL)
        z = lambda n: jnp.zeros((n, D_MODEL), BF16)
        return jnp.concatenate([wt[0:640], z(64), wt[640:672], z(32), z(256), wt[2208:4256], wt[672:2208]], axis=0)
    if name == 'w_q_up':
        return jnp.pad(g, ((0, 0), (0, HEAD_PAD - MLA_QK), (0, 0))).reshape(N_HEADS * HEAD_PAD, Q_LORA)
    if name == 'w_branch_mla':
        bm = _cols_to_full(g).reshape(N_HEADS, MLA_NOPE, D_MODEL)
        return jnp.pad(bm, ((0, 0), (HEAD_PAD - MLA_NOPE, 0), (0, 0))).reshape(N_HEADS * HEAD_PAD, D_MODEL)
    return _cols_to_full(g)


def _layout_weights(g):
    return {n: _layout_weight(n, a) for n, a in g.items()}


def _unlayout_grad(name, d):
    if name == 'w_in':
        d = jnp.concatenate([d[0:640], d[704:736], d[P_SBQ:PROJ_W], d[P_GM:P_SBQ]], axis=0)
    if name == 'w_q_up':
        return d.reshape(N_HEADS, HEAD_PAD, Q_LORA)[:, :MLA_QK, :]
    if name in ('ffn1_w_out', 'ffn2_w_out', 'w_out', 'w_ple_gate', 'ffn1_w_in', 'ffn2_w_in', 'w_in'):
        return d.reshape(N_DEV, d.shape[0] // N_DEV, d.shape[1])
    if name == 'w_branch_mla':
        d = d.reshape(N_HEADS, HEAD_PAD, D_MODEL)[:, HEAD_PAD - MLA_NOPE:, :].reshape(SB_WIDTH, D_MODEL)
    return _full_to_cols(d)


def _unlayout_grads(d):
    return {n: _unlayout_grad(n, a) for n, a in d.items()}


def _rope_tables(positions):
    half = MLA_ROPE // 2
    inv_freq = ROPE_BASE ** (-jnp.arange(0, MLA_ROPE, 2, dtype=F32) / MLA_ROPE)
    ang = positions.astype(F32)[:, None] * inv_freq
    cos, sin = jnp.cos(ang), jnp.sin(ang)
    t = positions.shape[0]
    ones = lambda n: jnp.ones((t, n), F32)
    zeros = lambda n: jnp.zeros((t, n), F32)
    cosf = jnp.concatenate([ones(MLA_NOPE), cos, cos, ones(HEAD_PAD - MLA_QK)], axis=1)
    sin_a = jnp.concatenate([zeros(MLA_NOPE), -sin, zeros(half), zeros(HEAD_PAD - MLA_QK)], axis=1)
    sin_b = jnp.concatenate([zeros(MLA_NOPE), zeros(half), sin, zeros(HEAD_PAD - MLA_QK)], axis=1)
    return cosf, sin_a, sin_b


def _local_step(x, p, positions, tgt, norms, plan):
    mm = _matmul
    cosf, sin_a, sin_b = _rope_tables(positions)
    pad_head = lambda g: jnp.pad(g, ((0, 0), (0, HEAD_PAD - MLA_QK)))
    gqh, gkh = pad_head(norms['q_head_norm']), pad_head(norms['k_head_norm'])
    pb = p.astype(BF16)
    w = dict(plan.first_weights())
    dw, dn = {}, {}

    def ride(host, call):
        cargo = plan.cargo(host, dw)
        res, lands = call(cargo), None
        if cargo is not None:
            *res, lands = res
            res = res[0] if len(res) == 1 else tuple(res)
        w.update(plan.landed(host, lands))
        return res

    def ffn_fwd(h, tag):
        n = _rmsnorm_fwd(h, norms[tag + '_norm'], tag + "_norm_fwd")
        ab = ride(tag + "_in_fwd", lambda cargo: mm(
            n, w[tag + '_w_in'], mode='nt', out_dtype=BF16, name=tag + "_in_fwd", cargo=cargo))
        act = _swiglu_fwd(ab, tag + "_swiglu_fwd")
        out = ride(tag + "_out_fwd", lambda cargo: mm(
            act, w[tag + '_w_out'], mode='nn', out_dtype=F32, name=tag + "_out_fwd", res=h, alpha=0.5, cargo=cargo))
        return out, (n, ab, act)

    h1, ffn1_saved = ffn_fwd(x, 'ffn1')
    u = _rmsnorm_fwd(h1, norms['mix_norm'], "mix_norm_fwd")
    proj = mm(u, w['w_in'], mode='nt', out_dtype=BF16, name="proj_fwd")
    cqn, ckvn = _latent_fwd(proj, norms['q_latent_norm'], norms['kv_latent_norm'])
    qraw = mm(cqn, w['w_q_up'], mode='nt', out_dtype=F32, name="q_up_fwd")
    kvraw = mm(ckvn, w['w_kv_up'], mode='nn', out_dtype=F32, name="kv_up_fwd")
    qh, kh, kvb = _headprep_fwd(qraw, kvraw, proj, cosf, sin_a, sin_b, gqh, gkh)
    o_mla, lse = ride("mla_fwd", lambda cargo: _mla_fwd(qh, kh, kvb, cargo))
    o_sb, rtot, sb_first = _sb_fwd(proj)
    bm = mm(o_mla, w['w_branch_mla'], mode='nn', out_dtype=F32, name="branch_mla_fwd")
    bs = mm(o_sb, w['w_branch_sb'], mode='nn', out_dtype=F32, name="branch_sb_fwd")
    merged = _merge_fwd(proj, bm, bs)
    h2 = mm(merged, w['w_out'], mode='nn', out_dtype=F32, name="mix_out_fwd", res=h1)
    h3, ffn2_saved = ffn_fwd(h2, 'ffn2')
    n3 = _rmsnorm_fwd(h3, norms['ple_norm'], "ple_norm_fwd")
    zg = mm(n3, w['w_ple_gate'], mode='nn', out_dtype=F32, name="ple_gate_fwd")
    pp = mm(pb, w['w_ple_proj'], mode='nn', out_dtype=F32, name="ple_proj_fwd")
    dh4, dzg, dpp, loss_lanes = _ple_loss(h3, zg, pp, tgt)

    dw['w_ple_gate'] = mm(n3, dzg, mode='tn', out_dtype=BF16, name="ple_gate_dw")
    dw['w_ple_proj'] = mm(pb, dpp, mode='tn', out_dtype=BF16, name="ple_proj_dw")
    dn3 = mm(dzg, w['w_ple_gate'], mode='nt', out_dtype=F32, name="ple_gate_dx")
    dh3, dhb3, dn['ple_norm'] = _rmsnorm_bwd(dn3, h3, norms['ple_norm'], dh4, "ple_norm_bwd", 0.5)

    def ffn_bwd(h, dh, dhb, saved, tag, out_scale):
        n, ab, act = saved
        dw[tag + '_w_out'] = mm(act, dhb, mode='tn', out_dtype=BF16, name=tag + "_out_dw", tm=1408)
        dact = mm(dhb, w[tag + '_w_out'], mode='nt', out_dtype=BF16, name=tag + "_out_dx", tn=1408)
        dab = _swiglu_bwd(ab, dact, tag + "_swiglu_bwd")
        dw[tag + '_w_in'] = ride(tag + "_in_dw", lambda cargo: mm(
            dab, n, mode='tn', out_dtype=BF16, name=tag + "_in_dw", tm=1408, cargo=cargo))
        dnn = ride(tag + "_in_dx", lambda cargo: mm(
            dab, w[tag + '_w_in'], mode='nn', out_dtype=F32, name=tag + "_in_dx", cargo=cargo))
        dh_prev, dhb_prev, dn[tag + '_norm'] = _rmsnorm_bwd(dnn, h, norms[tag + '_norm'], dh, tag + "_norm_bwd", out_scale)
        return dh_prev, dhb_prev

    dh2, dhb2 = ffn_bwd(h2, dh3, dhb3, ffn2_saved, 'ffn2', 1.0)
    dw['w_out'] = mm(merged, dhb2, mode='tn', out_dtype=BF16, name="mix_out_dw")
    dmerged = mm(dhb2, w['w_out'], mode='nt', out_dtype=F32, name="mix_out_dx")
    dbm, dbs, dgates = _merge_bwd(dmerged, proj, bm, bs)
    dw['w_branch_mla'] = mm(o_mla, dbm, mode='tn', out_dtype=BF16, name="branch_mla_dw")
    dw['w_branch_sb'] = mm(o_sb, dbs, mode='tn', out_dtype=BF16, name="branch_sb_dw")
    do_mla = mm(dbm, w['w_branch_mla'], mode='nt', out_dtype=BF16, name="branch_mla_dx")
    do_sb = mm(dbs, w['w_branch_sb'], mode='nt', out_dtype=BF16, name="branch_sb_dx")
    dqh, dkh, dvp = ride("mla_bwd", lambda cargo: _mla_bwd(qh, kh, kvb, o_mla, do_mla, lse, cargo))
    dsq, dsk, dsv = _sb_bwd(proj, do_sb, rtot, sb_first)
    dqraw, dkvraw, dkr, dgq, dgk = _headprep_bwd(dqh, dkh, dvp, qraw, kvraw, proj, cosf, sin_a, sin_b, gqh, gkh)
    dn['q_head_norm'], dn['k_head_norm'] = dgq[:, :MLA_QK], dgk[:, :MLA_QK]
    dw['w_q_up'] = mm(dqraw, cqn, mode='tn', out_dtype=BF16, name="q_up_dw")
    dw['w_kv_up'] = mm(ckvn, dkvraw, mode='tn', out_dtype=BF16, name="kv_up_dw")
    dcqn = mm(dqraw, w['w_q_up'], mode='nn', out_dtype=F32, name="q_up_dx")
    dckvn = mm(dkvraw, w['w_kv_up'], mode='nt', out_dtype=F32, name="kv_up_dx")
    dlat, dn['q_latent_norm'], dn['kv_latent_norm'] = _latent_bwd(
        dcqn, dckvn, proj, dkr, norms['q_latent_norm'], norms['kv_latent_norm'])
    dproj = jnp.concatenate([dlat, dgates, dsq.astype(BF16), dsk.astype(BF16), dsv.astype(BF16)], axis=1)
    dw['w_in'] = ride("proj_dw", lambda cargo: mm(dproj, u, mode='tn', out_dtype=BF16, name="proj_dw", tm=1536, cargo=cargo))
    du = ride("proj_dx", lambda cargo: mm(dproj, w['w_in'], mode='nn', out_dtype=F32, name="proj_dx", cargo=cargo))
    dh1, dhb1, dn['mix_norm'] = _rmsnorm_bwd(du, h1, norms['mix_norm'], dh2, "mix_norm_bwd", 0.5)
    dx, _ = ffn_bwd(x, dh1, dhb1, ffn1_saved, 'ffn1', 1.0)
    return dx, loss_lanes, dw, dn


MESH = pl.DeviceIdType.MESH
HBM_SPEC = pl.BlockSpec(memory_space=pl.ANY)


def _position():
    return lax.axis_index("x"), lax.axis_index("y"), lax.axis_index("c")


def _index(px, py, pc):
    return 4 * px + 2 * py + pc


def _all_gather(shards):
    n = len(shards)

    def body(*refs):
        ins, outs = refs[:n], refs[n:2 * n]
        send_sems, recv_sems, local_sems = refs[2 * n:]
        x, y, c = _position()
        me, sibling = (x, y, c), (x, y, 1 - c)
        chips = [(1 - x, y), (x, 1 - y), (1 - x, 1 - y)]

        def copy(a, k, block, to, own=False):
            dst = outs[a].at[_index(*block)]
            return pltpu.make_async_remote_copy(
                src_ref=ins[a] if own else dst, dst_ref=dst,
                send_sem=send_sems.at[a, k], recv_sem=recv_sems.at[a, k], device_id=to, device_id_type=MESH)

        mine = [pltpu.make_async_copy(ins[a], outs[a].at[_index(*me)], local_sems.at[a]) for a in range(n)]
        for cp in mine:
            cp.start()
        first = []
        for a in range(n):
            first.append(copy(a, 0, me, sibling, own=True))
            first += [copy(a, 1 + j, me, (*chip, c), own=True) for j, chip in enumerate(chips)]
        for cp in first:
            cp.start()
        passed = []
        for j, chip in enumerate(chips):
            for a in range(n):
                copy(a, 1 + j, (*chip, c), me).wait_recv()
                fwd = copy(a, 4 + j, (*chip, c), sibling)
                fwd.start()
                passed.append(fwd)
        for a in range(n):
            copy(a, 0, sibling, me).wait_recv()
            for j, chip in enumerate(chips):
                copy(a, 4 + j, (*chip, 1 - c), me).wait_recv()
        for cp in first + passed:
            cp.wait_send()
        for cp in mine:
            cp.wait()

    return pl.pallas_call(
        body, name="weights_all_gather",
        in_specs=[HBM_SPEC] * n, out_specs=[HBM_SPEC] * n,
        out_shape=[jax.ShapeDtypeStruct((N_DEV,) + s.shape, s.dtype) for s in shards],
        scratch_shapes=[pltpu.SemaphoreType.DMA((n, 7)), pltpu.SemaphoreType.DMA((n, 7)), pltpu.SemaphoreType.DMA((n,))],
    )(*shards)


def _exchange(parts):
    n = len(parts)
    masks = [(mx, my, mc) for mx in (0, 1) for my in (0, 1) for mc in (0, 1)][1:]

    def body(*refs):
        ins, outs = refs[:n], refs[n:2 * n]
        send_sems, recv_sems, local_sems = refs[2 * n:]
        x, y, c = _position()
        me = _index(x, y, c)

        def peer_of(mask):
            mx, my, mc = mask
            return (x + mx - 2 * x * mx, y + my - 2 * y * my, c + mc - 2 * c * mc)

        def copy(a, k):
            peer = peer_of(masks[k])
            return pltpu.make_async_remote_copy(
                src_ref=ins[a].at[_index(*peer)], dst_ref=outs[a].at[me],
                send_sem=send_sems.at[a, k], recv_sem=recv_sems.at[a, k], device_id=peer, device_id_type=MESH)

        def landed(a, k):
            peer = peer_of(masks[k])
            return pltpu.make_async_remote_copy(
                src_ref=ins[a].at[me], dst_ref=outs[a].at[_index(*peer)],
                send_sem=send_sems.at[a, k], recv_sem=recv_sems.at[a, k], device_id=peer, device_id_type=MESH)

        mine = [pltpu.make_async_copy(ins[a].at[me], outs[a].at[me], local_sems.at[a]) for a in range(n)]
        for cp in mine:
            cp.start()
        sent = [copy(a, k) for k in range(7) for a in range(n)]
        for cp in sent:
            cp.start()
        for k in range(7):
            for a in range(n):
                landed(a, k).wait_recv()
        for cp in sent:
            cp.wait_send()
        for cp in mine:
            cp.wait()

    return pl.pallas_call(
        body, name="grads_exchange",
        in_specs=[HBM_SPEC] * n, out_specs=[HBM_SPEC] * n,
        out_shape=[jax.ShapeDtypeStruct(s.shape, s.dtype) for s in parts],
        scratch_shapes=[pltpu.SemaphoreType.DMA((n, 7)), pltpu.SemaphoreType.DMA((n, 7)), pltpu.SemaphoreType.DMA((n,))],
    )(*parts)


PEER_MASKS = [(mx, my, mc) for mx in (0, 1) for my in (0, 1) for mc in (0, 1)][1:]


def _peer(mask):
    x, y, c = _position()
    mx, my, mc = mask
    return (x + mx - 2 * x * mx, y + my - 2 * y * my, c + mc - 2 * c * mc)


class _Cargo:
    def __init__(self, srcs, scatter):
        self.srcs, self.scatter, self.n = list(srcs), scatter, len(srcs)

    def specs(self):
        return [HBM_SPEC] * self.n

    def out_shape(self):
        return [jax.ShapeDtypeStruct(s.shape if self.scatter else (N_DEV,) + s.shape, s.dtype) for s in self.srcs]

    def scratch(self):
        per_copy = pltpu.SemaphoreType.DMA((self.n, len(PEER_MASKS)))
        return [per_copy, per_copy, pltpu.SemaphoreType.DMA((self.n,))]

    def _mine(self, src_refs, a, to):
        return src_refs[a].at[to] if self.scatter else src_refs[a]

    def start(self, src_refs, land_refs, sems):
        send, recv, local = sems
        me = _index(*_position())
        for a in range(self.n):
            pltpu.make_async_copy(self._mine(src_refs, a, me), land_refs[a].at[me], local.at[a]).start()
        for k, mask in enumerate(PEER_MASKS):
            peer = _peer(mask)
            for a in range(self.n):
                pltpu.make_async_remote_copy(
                    src_ref=self._mine(src_refs, a, _index(*peer)), dst_ref=land_refs[a].at[me],
                    send_sem=send.at[a, k], recv_sem=recv.at[a, k], device_id=peer, device_id_type=MESH).start()

    def wait(self, src_refs, land_refs, sems):
        send, recv, local = sems
        me = _index(*_position())
        for k, mask in enumerate(PEER_MASKS):
            peer = _peer(mask)
            there = _index(*peer)
            for a in range(self.n):
                pltpu.make_async_remote_copy(
                    src_ref=self._mine(src_refs, a, me), dst_ref=land_refs[a].at[there],
                    send_sem=send.at[a, k], recv_sem=recv.at[a, k], device_id=peer, device_id_type=MESH).wait_recv()
                pltpu.make_async_remote_copy(
                    src_ref=self._mine(src_refs, a, there), dst_ref=land_refs[a].at[me],
                    send_sem=send.at[a, k], recv_sem=recv.at[a, k], device_id=peer, device_id_type=MESH).wait_send()
        for a in range(self.n):
            pltpu.make_async_copy(self._mine(src_refs, a, me), land_refs[a].at[me], local.at[a]).wait()


def _with_cargo(cargo, refs, n_in, n_out, steps, counts, compute):
    if cargo is None:
        compute(refs)
        return
    n = cargo.n
    src_refs = refs[n_in:n_in + n]
    land_refs = refs[n_in + n + n_out:n_in + 2 * n + n_out]
    sems = refs[-3:]
    first = functools.reduce(jnp.logical_and, [s == 0 for s in steps])
    last = functools.reduce(jnp.logical_and, [s == c - 1 for s, c in zip(steps, counts)])

    @pl.when(first)
    def _():
        cargo.start(src_refs, land_refs, sems)

    compute(refs[:n_in] + refs[n_in + n:n_in + n + n_out] + refs[n_in + 2 * n + n_out:-3])

    @pl.when(last)
    def _():
        cargo.wait(src_refs, land_refs, sems)


def _adamw(parts, w, m, v, name):
    r, c = w.shape
    tr = next((t for t in (512, 384, 352, 256, 128) if r % t == 0), r) if r > 512 else r
    tc = c if tr < r or r <= 512 else 256
    assert r % tr == 0 and c % tc == 0
    bc1 = 1.0 - ADAM_B1 ** ADAM_STEP
    bc2 = 1.0 - ADAM_B2 ** ADAM_STEP

    def body(p_ref, w_ref, m_ref, v_ref, g_ref, d_ref, nm_ref, nv_ref):
        g = p_ref[0].astype(F32)
        for s in range(1, N_DEV):
            g = g + p_ref[s].astype(F32)
        nm = ADAM_B1 * m_ref[...] + (1.0 - ADAM_B1) * g
        nv = ADAM_B2 * v_ref[...] + (1.0 - ADAM_B2) * (g * g)
        g_ref[...] = g
        nm_ref[...] = nm
        nv_ref[...] = nv
        d_ref[...] = -ADAM_LR * ((nm / bc1) / (jnp.sqrt(nv / bc2) + ADAM_EPS) + ADAM_WD * w_ref[...])

    tile = pl.BlockSpec((tr, tc), lambda i, j: (i, j))
    out = jax.ShapeDtypeStruct((r, c), F32)
    return pl.pallas_call(
        body, name=name, grid=(r // tr, c // tc),
        in_specs=[pl.BlockSpec((N_DEV, tr, tc), lambda i, j: (0, i, j)), tile, tile, tile],
        out_specs=[tile] * 4, out_shape=[out] * 4,
        compiler_params=_params(("parallel", "parallel")),
    )(parts, w, m, v)


GATHER_FIRST = ['ffn1_w_in', 'ffn1_w_out']
RIDES = {
    'ffn1_in_fwd': ('weights', ['w_in']),
    'ffn1_out_fwd': ('weights', ['w_q_up', 'w_kv_up', 'w_branch_mla', 'w_branch_sb', 'w_out']),
    'mla_fwd': ('weights', ['ffn2_w_in', 'ffn2_w_out', 'w_ple_gate', 'w_ple_proj']),
    'mla_bwd': ('grads', ['w_ple_gate', 'w_ple_proj', 'ffn2_w_out', 'ffn2_w_in', 'w_out', 'w_branch_mla', 'w_branch_sb']),
    'proj_dw': ('grads', ['w_q_up', 'w_kv_up']),
    'proj_dx': ('grads', ['w_in']),
    'ffn1_in_dw': ('grads', ['ffn1_w_out']),
    'ffn1_in_dx': ('grads', ['ffn1_w_in']),
}


class _Plan:
    def __init__(self, shards):
        self.shards = shards
        self.received = {}

    def first_weights(self):
        gathered = _all_gather([self.shards[n] for n in GATHER_FIRST])
        return {n: _layout_weight(n, g) for n, g in zip(GATHER_FIRST, gathered)}

    def cargo(self, host, dw):
        if host not in RIDES:
            return None
        kind, names = RIDES[host]
        if kind == 'weights':
            return _Cargo([self.shards[n] for n in names], False)
        return _Cargo([_unlayout_grad(n, dw.pop(n)) for n in names], True)

    def landed(self, host, lands):
        if host not in RIDES:
            return {}
        kind, names = RIDES[host]
        if kind == 'weights':
            return {n: _layout_weight(n, land) for n, land in zip(names, lands)}
        self.received.update(zip(names, lands))
        return {}


def _pack_small(vecs):
    flat = jnp.concatenate([v.reshape(-1) for v in vecs])
    return jnp.pad(flat, (0, SMALL_ROWS * 128 - flat.shape[0])).reshape(SMALL_ROWS, 128)


def _unpack_small(packed, sizes):
    flat = packed.reshape(-1)
    out, at = [], 0
    for n in sizes:
        out.append(flat[at:at + n])
        at += n
    return out


def kernel(x, p, positions, ffn1_norm, ffn1_w_in, ffn1_w_out, mix_norm, w_in, q_latent_norm, w_q_up, kv_latent_norm, w_kv_up, q_head_norm, k_head_norm, w_branch_mla, w_branch_sb, w_out, ffn2_norm, ffn2_w_in, ffn2_w_out, ple_norm, w_ple_gate, w_ple_proj, loss_target, m_ffn1_norm, m_ffn1_w_in, m_ffn1_w_out, m_mix_norm, m_w_in, m_q_latent_norm, m_w_q_up, m_kv_latent_norm, m_w_kv_up, m_q_head_norm, m_k_head_norm, m_w_branch_mla, m_w_branch_sb, m_w_out, m_ffn2_norm, m_ffn2_w_in, m_ffn2_w_out, m_ple_norm, m_w_ple_gate, m_w_ple_proj, v_ffn1_norm, v_ffn1_w_in, v_ffn1_w_out, v_mix_norm, v_w_in, v_q_latent_norm, v_w_q_up, v_kv_latent_norm, v_w_kv_up, v_q_head_norm, v_k_head_norm, v_w_branch_mla, v_w_branch_sb, v_w_out, v_ffn2_norm, v_ffn2_w_in, v_ffn2_w_out, v_ple_norm, v_w_ple_gate, v_w_ple_proj):
    given = dict(locals())
    wts = {n: given[n] for n in WEIGHTS}
    mom = {n: given['m_' + n] for n in WEIGHTS}
    var = {n: given['v_' + n] for n in WEIGHTS}

    def local(a, n):
        return jnp.swapaxes(a[0], 0, 1) if n in TRANSPOSED else a[0]

    plan = _Plan({n: local(wts[n], n).astype(BF16) for n in MATS})
    norms = {n: wts[n] for n in NORMS}
    dx, loss_lanes, dw, dn = _local_step(x[0], p[0, 0], positions[0], loss_target[0], norms, plan)
    assert not dw

    out = {}
    for n in MATS:
        res = _adamw(plan.received[n], local(wts[n], n), local(mom[n], n), local(var[n], n), "adamw_" + n)
        out[n] = [local(r[None], n)[None] for r in res]
    small = _pack_small([dn[n] for n in NORMS] + [0.5 / D_MODEL * jnp.sum(loss_lanes)[None]])
    small_parts = _exchange([jnp.broadcast_to(small[None], (N_DEV, SMALL_ROWS, 128))])[0]
    sizes = [wts[n].shape[1] for n in NORMS]
    pack = lambda d: _pack_small([d[n] for n in NORMS])
    small_res = _adamw(small_parts, pack(wts), pack(mom), pack(var), "adamw_norms")
    loss = small_res[0].reshape(-1)[sum(sizes)]
    for i, res in enumerate(small_res):
        for n, vec in zip(NORMS, _unpack_small(res, sizes)):
            out.setdefault(n, [None] * 4)[i] = vec[None]

    return (loss, dx[None], *[out[n][0] for n in WEIGHTS], *[out[n][1] for n in WEIGHTS],
            *[out[n][2] for n in WEIGHTS], *[out[n][3] for n in WEIGHTS])
```

```python
import functools
import math

import jax
import jax.numpy as jnp
from jax import lax
from jax.experimental import pallas as pl
from jax.experimental.pallas import tpu as pltpu

F32 = jnp.float32
BF16 = jnp.bfloat16

N_DEV = 8
D_MODEL = 1024
D_FF = 2816
PLE_DIM = 256
NORM_EPS = 1e-6
N_HEADS = 8
HEAD_PAD = 128
MLA_NOPE = 64
MLA_ROPE = 32
MLA_QK = 96
Q_LORA = 384
KV_LORA = 256
SB_DIM = 64
SB_WIDTH = 512
ROPE_BASE = 10000.0
IN_COLS = 4256

PROJ_W = 4608
P_CQ, P_CKV, P_KR, P_GM, P_GS, P_SBQ, P_SBK, P_SBV = 0, 384, 640, 1024, 2048, 3072, 3584, 4096

ADAM_LR, ADAM_B1, ADAM_B2, ADAM_EPS, ADAM_WD, ADAM_STEP = 0.001, 0.9, 0.999, 1e-08, 0.01, 10

VMEM_LIMIT = 52 * 1024 * 1024
MATMUL_VMEM = 40 * 1024 * 1024

WEIGHTS = ['ffn1_norm', 'ffn1_w_in', 'ffn1_w_out', 'mix_norm', 'w_in', 'q_latent_norm', 'w_q_up',
           'kv_latent_norm', 'w_kv_up', 'q_head_norm', 'k_head_norm', 'w_branch_mla', 'w_branch_sb',
           'w_out', 'ffn2_norm', 'ffn2_w_in', 'ffn2_w_out', 'ple_norm', 'w_ple_gate', 'w_ple_proj']
NORMS = ['ffn1_norm', 'mix_norm', 'q_latent_norm', 'kv_latent_norm', 'q_head_norm', 'k_head_norm',
         'ffn2_norm', 'ple_norm']
MATS = [n for n in WEIGHTS if n not in NORMS]
SMALL_ROWS = 48

NT_DIMS = (((1,), (1,)), ((), ()))
NN_DIMS = (((1,), (0,)), ((), ()))
TN_DIMS = (((0,), (0,)), ((), ()))


def _params(sem=None, vmem=VMEM_LIMIT):
    return pltpu.CompilerParams(dimension_semantics=sem, vmem_limit_bytes=vmem)


def _pick(n, cap):
    if n <= cap:
        return n
    best = None
    for t in range(128, cap + 1, 128):
        if n % t == 0:
            best = t
    assert best is not None, (n, cap)
    return best


def _dot(a, b, dims):
    return lax.dot_general(a, b, dims, preferred_element_type=F32)


def _matmul(a, b, *, mode, out_dtype, name, tm=None, tn=None, tk=None, res=None, alpha=1.0, cargo=None):
    if mode == 'nn':
        (m, k), (k2, n) = a.shape, b.shape
    elif mode == 'nt':
        (m, k), (n, k2) = a.shape, b.shape
    else:
        (k, m), (k2, n) = a.shape, b.shape
    assert k == k2, (name, a.shape, b.shape)
    has_res = res is not None
    tn = tn or _pick(n, 512)

    def vmem(tm_, tk_):
        io = 2 * 2 * (tm_ * tk_ + tk_ * tn) + 2 * tm_ * tn * (jnp.dtype(out_dtype).itemsize + 4 * has_res)
        return io + (4 * tm_ * tn if tk_ < k else 0)

    tries = [(tm_, tk_) for tk_ in ([tk] if tk else [k, _pick(k, 2048)])
             for tm_ in ([tm] if tm else [_pick(m, 2048), _pick(m, 1024), _pick(m, 512)])]
    tm, tk = next((c for c in tries if vmem(*c) <= MATMUL_VMEM), tries[-1])
    assert m % tm == 0 and n % tn == 0 and k % tk == 0, (name, m, n, k, tm, tn, tk)
    nk = k // tk
    dims = {'nn': NN_DIMS, 'nt': NT_DIMS, 'tn': TN_DIMS}[mode]

    def epilogue(acc, r_ref, o_ref):
        if alpha != 1.0:
            acc = acc * alpha
        if has_res:
            acc = r_ref[...] + acc
        o_ref[...] = acc.astype(out_dtype)

    grid = (m // tm, n // tn, nk)

    def body(*refs):
        steps = [pl.program_id(d) for d in range(3)]

        def compute(own):
            a_ref, b_ref = own[0], own[1]
            r_ref = own[2] if has_res else None
            o_ref = own[2 + has_res]
            if nk == 1:
                epilogue(_dot(a_ref[...], b_ref[...], dims), r_ref, o_ref)
                return
            acc_ref = own[-1]

            @pl.when(steps[2] == 0)
            def _():
                acc_ref[...] = jnp.zeros_like(acc_ref)

            acc_ref[...] += _dot(a_ref[...], b_ref[...], dims)

            @pl.when(steps[2] == nk - 1)
            def _():
                epilogue(acc_ref[...], r_ref, o_ref)

        _with_cargo(cargo, refs, 2 + has_res, 1, steps, grid, compute)

    if mode == 'tn':
        a_spec = pl.BlockSpec((tk, tm), lambda i, j, kk: (kk, i))
    else:
        a_spec = pl.BlockSpec((tm, tk), lambda i, j, kk: (i, kk))
    if mode == 'nt':
        b_spec = pl.BlockSpec((tn, tk), lambda i, j, kk: (j, kk))
    else:
        b_spec = pl.BlockSpec((tk, tn), lambda i, j, kk: (kk, j))
    o_spec = pl.BlockSpec((tm, tn), lambda i, j, kk: (i, j))
    in_specs = [a_spec, b_spec] + ([o_spec] if has_res else [])
    args = (a, b) + ((res,) if has_res else ())
    out_shape = jax.ShapeDtypeStruct((m, n), out_dtype)
    scratch = [pltpu.VMEM((tm, tn), F32)] if nk > 1 else []
    if cargo is None:
        return pl.pallas_call(
            body, name=name, grid=grid, in_specs=in_specs, out_specs=o_spec, out_shape=out_shape,
            scratch_shapes=scratch, compiler_params=_params(("parallel", "parallel", "arbitrary")),
        )(*args)
    outs = pl.pallas_call(
        body, name=name, grid=grid, in_specs=in_specs + cargo.specs(), out_specs=[o_spec] + cargo.specs(),
        out_shape=[out_shape] + cargo.out_shape(), scratch_shapes=scratch + cargo.scratch(),
        compiler_params=_params(("arbitrary", "arbitrary", "arbitrary")),
    )(*args, *cargo.srcs)
    return outs[0], list(outs[1:])


def _row_tile(t, cap=512):
    return min(t, cap)


def _rms(x, width):
    return lax.rsqrt(jnp.sum(x * x, axis=-1, keepdims=True) * (1.0 / width) + NORM_EPS)


def _rmsnorm_fwd(x, g, name):
    t, d = x.shape
    tr = _row_tile(t)

    def body(x_ref, g_ref, o_ref):
        xv = x_ref[...]
        o_ref[...] = ((xv * _rms(xv, d)) * g_ref[...]).astype(BF16)

    return pl.pallas_call(
        body, name=name, grid=(t // tr,),
        in_specs=[pl.BlockSpec((tr, d), lambda i: (i, 0)), pl.BlockSpec((1, d), lambda i: (0, 0))],
        out_specs=pl.BlockSpec((tr, d), lambda i: (i, 0)),
        out_shape=jax.ShapeDtypeStruct((t, d), BF16),
        compiler_params=_params(("parallel",)),
    )(x, g)


def _rmsnorm_bwd(dn, x, g, dh_in, name, out_scale):
    t, d = x.shape
    tr = _row_tile(t, 256)

    def body(dn_ref, x_ref, g_ref, dhin_ref, dh_ref, dhb_ref, dg_ref):
        i = pl.program_id(0)
        xv = x_ref[...]
        dnv = dn_ref[...]
        r = _rms(xv, d)
        y = xv * r
        dy = dnv * g_ref[...]
        dx = r * (dy - y * (jnp.sum(dy * y, axis=-1, keepdims=True) * (1.0 / d)))
        dh = dhin_ref[...] + dx
        dh_ref[...] = dh
        dhb_ref[...] = (dh * out_scale).astype(BF16)
        part = jnp.sum(dnv * y, axis=0, keepdims=True)

        @pl.when(i == 0)
        def _():
            dg_ref[...] = part

        @pl.when(i > 0)
        def _():
            dg_ref[...] += part

    row = pl.BlockSpec((tr, d), lambda i: (i, 0))
    vec = pl.BlockSpec((1, d), lambda i: (0, 0))
    return pl.pallas_call(
        body, name=name, grid=(t // tr,),
        in_specs=[row, row, vec, row], out_specs=[row, row, vec],
        out_shape=[jax.ShapeDtypeStruct((t, d), F32), jax.ShapeDtypeStruct((t, d), BF16),
                   jax.ShapeDtypeStruct((1, d), F32)],
        compiler_params=_params(("arbitrary",)),
    )(dn, x, g, dh_in)


def _sigmoid(x):
    return 1.0 / (1.0 + jnp.exp(-x))


def _swiglu_fwd(ab, name):
    t = ab.shape[0]
    tr = _row_tile(t)

    def body(a_ref, b_ref, o_ref):
        a = a_ref[...].astype(F32)
        o_ref[...] = (a * _sigmoid(a) * b_ref[...].astype(F32)).astype(BF16)

    return pl.pallas_call(
        body, name=name, grid=(t // tr,),
        in_specs=[pl.BlockSpec((tr, D_FF), lambda i: (i, 0)), pl.BlockSpec((tr, D_FF), lambda i: (i, 1))],
        out_specs=pl.BlockSpec((tr, D_FF), lambda i: (i, 0)),
        out_shape=jax.ShapeDtypeStruct((t, D_FF), BF16),
        compiler_params=_params(("parallel",)),
    )(ab, ab)


def _swiglu_bwd(ab, dact, name):
    t = ab.shape[0]
    tr = _row_tile(t, 256)

    def body(ab_ref, d_ref, o_ref):
        a = ab_ref[:, :D_FF].astype(F32)
        b = ab_ref[:, D_FF:].astype(F32)
        dv = d_ref[...].astype(F32)
        s = _sigmoid(a)
        o_ref[:, :D_FF] = (dv * b * s * (1.0 + a * (1.0 - s))).astype(BF16)
        o_ref[:, D_FF:] = (dv * a * s).astype(BF16)

    return pl.pallas_call(
        body, name=name, grid=(t // tr,),
        in_specs=[pl.BlockSpec((tr, 2 * D_FF), lambda i: (i, 0)), pl.BlockSpec((tr, D_FF), lambda i: (i, 0))],
        out_specs=pl.BlockSpec((tr, 2 * D_FF), lambda i: (i, 0)),
        out_shape=jax.ShapeDtypeStruct((t, 2 * D_FF), BF16),
        compiler_params=_params(("parallel",)),
    )(ab, dact)


def _latent_fwd(proj, gq, gkv):
    t = proj.shape[0]
    tr = _row_tile(t)

    def body(p_ref, gq_ref, gkv_ref, cq_ref, ckv_ref):
        cq = p_ref[:, P_CQ:P_CQ + Q_LORA].astype(F32)
        ckv = p_ref[:, P_CKV:P_CKV + KV_LORA].astype(F32)
        cq_ref[...] = ((cq * _rms(cq, Q_LORA)) * gq_ref[...]).astype(BF16)
        ckv_ref[...] = ((ckv * _rms(ckv, KV_LORA)) * gkv_ref[...]).astype(BF16)

    return pl.pallas_call(
        body, name="latent_fwd", grid=(t // tr,),
        in_specs=[pl.BlockSpec((tr, 1024), lambda i: (i, 0)), pl.BlockSpec((1, Q_LORA), lambda i: (0, 0)),
                  pl.BlockSpec((1, KV_LORA), lambda i: (0, 0))],
        out_specs=[pl.BlockSpec((tr, Q_LORA), lambda i: (i, 0)), pl.BlockSpec((tr, KV_LORA), lambda i: (i, 0))],
        out_shape=[jax.ShapeDtypeStruct((t, Q_LORA), BF16), jax.ShapeDtypeStruct((t, KV_LORA), BF16)],
        compiler_params=_params(("parallel",)),
    )(proj, gq, gkv)


def _latent_bwd(dcqn, dckvn, proj, dkr, gq, gkv):
    t = proj.shape[0]
    tr = _row_tile(t, 256)

    def norm_bwd(dn, x, g, width):
        r = _rms(x, width)
        y = x * r
        dy = dn * g
        dx = r * (dy - y * (jnp.sum(dy * y, axis=-1, keepdims=True) * (1.0 / width)))
        return dx, jnp.sum(dn * y, axis=0, keepdims=True)

    def body(dcq_ref, dckv_ref, p_ref, dkr_ref, gq_ref, gkv_ref, o_ref, dgq_ref, dgkv_ref):
        i = pl.program_id(0)
        dcq, pq = norm_bwd(dcq_ref[...], p_ref[:, P_CQ:P_CQ + Q_LORA].astype(F32), gq_ref[...], Q_LORA)
        dckv, pkv = norm_bwd(dckv_ref[...], p_ref[:, P_CKV:P_CKV + KV_LORA].astype(F32), gkv_ref[...], KV_LORA)
        o_ref[:, P_CQ:P_CQ + Q_LORA] = dcq.astype(BF16)
        o_ref[:, P_CKV:P_CKV + KV_LORA] = dckv.astype(BF16)
        o_ref[:, P_KR:P_KR + 128] = dkr_ref[...].astype(BF16)
        o_ref[:, P_KR + 128:1024] = jnp.zeros((tr, 1024 - P_KR - 128), BF16)

        @pl.when(i == 0)
        def _():
            dgq_ref[...] = pq
            dgkv_ref[...] = pkv

        @pl.when(i > 0)
        def _():
            dgq_ref[...] += pq
            dgkv_ref[...] += pkv

    def row(w):
        return pl.BlockSpec((tr, w), lambda i: (i, 0))

    def vec(w):
        return pl.BlockSpec((1, w), lambda i: (0, 0))

    return pl.pallas_call(
        body, name="latent_bwd", grid=(t // tr,),
        in_specs=[row(Q_LORA), row(KV_LORA), row(1024), row(128), vec(Q_LORA), vec(KV_LORA)],
        out_specs=[row(1024), vec(Q_LORA), vec(KV_LORA)],
        out_shape=[jax.ShapeDtypeStruct((t, 1024), BF16), jax.ShapeDtypeStruct((1, Q_LORA), F32),
                   jax.ShapeDtypeStruct((1, KV_LORA), F32)],
        compiler_params=_params(("arbitrary",)),
    )(dcqn, dckvn, proj, dkr, gq, gkv)


def _rope(y, cosf, sin_a, sin_b):
    return y * cosf + pltpu.roll(y, 112, 1) * sin_a + pltpu.roll(y, 16, 1) * sin_b


def _rope_t(d, cosf, sin_a, sin_b):
    return d * cosf + pltpu.roll(d * sin_a, 16, 1) + pltpu.roll(d * sin_b, 112, 1)


def _headprep_fwd(qraw, kvraw, proj, cosf, sin_a, sin_b, gqh, gkh):
    t = qraw.shape[0]
    tr = _row_tile(t, 256)

    def body(q_ref, kv_ref, kr_ref, c_ref, sa_ref, sb_ref, gq_ref, gk_ref, qh_ref, kh_ref, kvb_ref):
        cv, sa, sb = c_ref[...], sa_ref[...], sb_ref[...]
        kr = kr_ref[...].astype(F32)
        lane = lax.broadcasted_iota(jnp.int32, (tr, HEAD_PAD), 1)
        for h in range(N_HEADS):
            cols = slice(h * HEAD_PAD, (h + 1) * HEAD_PAD)
            xq = q_ref[:, cols]
            yq = (xq * _rms(xq, MLA_QK)) * gq_ref[...]
            qh_ref[:, cols] = _rope(yq, cv, sa, sb).astype(BF16)
            kvh = kv_ref[:, cols]
            kvb_ref[:, cols] = jnp.where(lane < MLA_NOPE, 1.0, kvh).astype(BF16)
            xk = jnp.where(lane < MLA_NOPE, kvh, kr)
            yk = (xk * _rms(xk, MLA_QK)) * gk_ref[...]
            kh_ref[:, cols] = _rope(yk, cv, sa, sb).astype(BF16)

    wide = pl.BlockSpec((tr, 1024), lambda i: (i, 0))
    lanes = pl.BlockSpec((tr, HEAD_PAD), lambda i: (i, 0))
    vec = pl.BlockSpec((1, HEAD_PAD), lambda i: (0, 0))
    return pl.pallas_call(
        body, name="headprep_fwd", grid=(t // tr,),
        in_specs=[wide, wide, pl.BlockSpec((tr, HEAD_PAD), lambda i: (i, P_KR // HEAD_PAD)), lanes, lanes, lanes, vec, vec],
        out_specs=[wide, wide, wide],
        out_shape=[jax.ShapeDtypeStruct((t, 1024), BF16)] * 3,
        compiler_params=_params(("parallel",)),
    )(qraw, kvraw, proj, cosf, sin_a, sin_b, gqh, gkh)


def _headprep_bwd(dqh, dkh, dvp, qraw, kvraw, proj, cosf, sin_a, sin_b, gqh, gkh):
    t = qraw.shape[0]
    tr = _row_tile(t, 256)

    def norm_bwd(dn, x, g):
        r = _rms(x, MLA_QK)
        y = x * r
        dy = dn * g
        dx = r * (dy - y * (jnp.sum(dy * y, axis=-1, keepdims=True) * (1.0 / MLA_QK)))
        return dx, jnp.sum(dn * y, axis=0, keepdims=True)

    def body(dq_ref, dk_ref, dv_ref, q_ref, kv_ref, kr_ref, c_ref, sa_ref, sb_ref, gq_ref, gk_ref,
             dqr_ref, dkvr_ref, dkr_ref, dgq_ref, dgk_ref):
        i = pl.program_id(0)
        cv, sa, sb = c_ref[...], sa_ref[...], sb_ref[...]
        kr = kr_ref[...].astype(F32)
        lane = lax.broadcasted_iota(jnp.int32, (tr, HEAD_PAD), 1)
        dkr = jnp.zeros((tr, HEAD_PAD), F32)
        pq = jnp.zeros((1, HEAD_PAD), F32)
        pk = jnp.zeros((1, HEAD_PAD), F32)
        for h in range(N_HEADS):
            cols = slice(h * HEAD_PAD, (h + 1) * HEAD_PAD)
            dxq, pqh = norm_bwd(_rope_t(dq_ref[:, cols], cv, sa, sb), q_ref[:, cols], gq_ref[...])
            dqr_ref[:, cols] = dxq.astype(BF16)
            pq = pq + pqh
            kvh = kv_ref[:, cols]
            xk = jnp.where(lane < MLA_NOPE, kvh, kr)
            dxk, pkh = norm_bwd(_rope_t(dk_ref[:, cols], cv, sa, sb), xk, gk_ref[...])
            pk = pk + pkh
            dkvr_ref[:, cols] = jnp.where(lane < MLA_NOPE, dxk, dv_ref[:, cols]).astype(BF16)
            dkr = dkr + jnp.where(lane < MLA_NOPE, 0.0, dxk)
        dkr_ref[...] = dkr

        @pl.when(i == 0)
        def _():
            dgq_ref[...] = pq
            dgk_ref[...] = pk

        @pl.when(i > 0)
        def _():
            dgq_ref[...] += pq
            dgk_ref[...] += pk

    wide = pl.BlockSpec((tr, 1024), lambda i: (i, 0))
    lanes = pl.BlockSpec((tr, HEAD_PAD), lambda i: (i, 0))
    vec = pl.BlockSpec((1, HEAD_PAD), lambda i: (0, 0))
    return pl.pallas_call(
        body, name="headprep_bwd", grid=(t // tr,),
        in_specs=[wide, wide, wide, wide, wide, pl.BlockSpec((tr, HEAD_PAD), lambda i: (i, P_KR // HEAD_PAD)),
                  lanes, lanes, lanes, vec, vec],
        out_specs=[wide, wide, lanes, vec, vec],
        out_shape=[jax.ShapeDtypeStruct((t, 1024), BF16), jax.ShapeDtypeStruct((t, 1024), BF16),
                   jax.ShapeDtypeStruct((t, HEAD_PAD), F32), jax.ShapeDtypeStruct((1, HEAD_PAD), F32),
                   jax.ShapeDtypeStruct((1, HEAD_PAD), F32)],
        compiler_params=_params(("arbitrary",)),
    )(dqh, dkh, dvp, qraw, kvraw, proj, cosf, sin_a, sin_b, gqh, gkh)


def _merge_fwd(proj, bm, bs):
    t = proj.shape[0]
    tr = _row_tile(t, 256)

    def body(gm_ref, gs_ref, bm_ref, bs_ref, o_ref):
        gm = _sigmoid(gm_ref[...].astype(F32))
        gs = _sigmoid(gs_ref[...].astype(F32))
        o_ref[...] = (gm * bm_ref[...] + gs * bs_ref[...]).astype(BF16)

    row = pl.BlockSpec((tr, 1024), lambda i: (i, 0))
    return pl.pallas_call(
        body, name="merge_fwd", grid=(t // tr,),
        in_specs=[pl.BlockSpec((tr, 1024), lambda i: (i, P_GM // 1024)),
                  pl.BlockSpec((tr, 1024), lambda i: (i, P_GS // 1024)), row, row],
        out_specs=row, out_shape=jax.ShapeDtypeStruct((t, 1024), BF16),
        compiler_params=_params(("parallel",)),
    )(proj, proj, bm, bs)


def _merge_bwd(dmerged, proj, bm, bs):
    t = proj.shape[0]
    tr = _row_tile(t, 256)

    def body(dm_ref, gm_ref, gs_ref, bm_ref, bs_ref, dbm_ref, dbs_ref, dg_ref):
        dm = dm_ref[...]
        gm = _sigmoid(gm_ref[...].astype(F32))
        gs = _sigmoid(gs_ref[...].astype(F32))
        dbm_ref[...] = (dm * gm).astype(BF16)
        dbs_ref[...] = (dm * gs).astype(BF16)
        dg_ref[:, :1024] = (dm * bm_ref[...] * gm * (1.0 - gm)).astype(BF16)
        dg_ref[:, 1024:] = (dm * bs_ref[...] * gs * (1.0 - gs)).astype(BF16)

    row = pl.BlockSpec((tr, 1024), lambda i: (i, 0))
    return pl.pallas_call(
        body, name="merge_bwd", grid=(t // tr,),
        in_specs=[row, pl.BlockSpec((tr, 1024), lambda i: (i, P_GM // 1024)),
                  pl.BlockSpec((tr, 1024), lambda i: (i, P_GS // 1024)), row, row],
        out_specs=[row, row, pl.BlockSpec((tr, 2048), lambda i: (i, 0))],
        out_shape=[jax.ShapeDtypeStruct((t, 1024), BF16), jax.ShapeDtypeStruct((t, 1024), BF16),
                   jax.ShapeDtypeStruct((t, 2048), BF16)],
        compiler_params=_params(("parallel",)),
    )(dmerged, proj, proj, bm, bs)


def _ple_loss(h3, zg, pp, tgt):
    t = h3.shape[0]
    tr = _row_tile(t, 256)

    def body(h_ref, z_ref, p_ref, t_ref, dh_ref, dz_ref, dp_ref, l_ref):
        i = pl.program_id(0)
        pg = _sigmoid(z_ref[...])
        ppv = p_ref[...]
        diff = (h_ref[...] + pg * ppv) - t_ref[...]
        dh = diff * (1.0 / D_MODEL)
        dh_ref[...] = dh
        dp_ref[...] = (dh * pg).astype(BF16)
        dz_ref[...] = (dh * ppv * pg * (1.0 - pg)).astype(BF16)
        sq = jnp.sum(diff * diff, axis=0, keepdims=True)
        part = sq[:, 0:128]
        for c in range(1, D_MODEL // 128):
            part = part + sq[:, c * 128:(c + 1) * 128]

        @pl.when(i == 0)
        def _():
            l_ref[...] = part

        @pl.when(i > 0)
        def _():
            l_ref[...] += part

    row = pl.BlockSpec((tr, 1024), lambda i: (i, 0))
    return pl.pallas_call(
        body, name="ple_loss", grid=(t // tr,),
        in_specs=[row, row, row, row],
        out_specs=[row, row, row, pl.BlockSpec((1, 128), lambda i: (0, 0))],
        out_shape=[jax.ShapeDtypeStruct((t, 1024), F32), jax.ShapeDtypeStruct((t, 1024), BF16),
                   jax.ShapeDtypeStruct((t, 1024), BF16), jax.ShapeDtypeStruct((1, 128), F32)],
        compiler_params=_params(("arbitrary",)),
    )(h3, zg, pp, tgt)


ATT_BLOCK = 256
ATT_COLS = 2
MLA_FWD_COLS = 4
MLA_BWD_COLS = 4


def _split_bf16(x):
    hi = x.astype(BF16)
    return hi, (x - hi.astype(F32)).astype(BF16)


def _tri(kind):
    r = lax.broadcasted_iota(jnp.int32, (ATT_BLOCK, ATT_BLOCK), 0)
    c = lax.broadcasted_iota(jnp.int32, (ATT_BLOCK, ATT_BLOCK), 1)
    cond = {'gt': r > c, 'le': r <= c, 'lt': r < c}[kind]
    return jnp.where(cond, 1.0, 0.0).astype(BF16)


def _causal(strict):
    r = lax.broadcasted_iota(jnp.int32, (ATT_BLOCK, ATT_BLOCK), 0)
    c = lax.broadcasted_iota(jnp.int32, (ATT_BLOCK, ATT_BLOCK), 1)
    return (c < r) if strict else (c <= r)


def _lanes(c):
    return slice(c * HEAD_PAD, (c + 1) * HEAD_PAD)


def _row_block(j):
    return pl.ds(pl.multiple_of(j * ATT_BLOCK, ATT_BLOCK), ATT_BLOCK)


def _rows(ref, j, c):
    return ref[_row_block(j), _lanes(c)]


def _mla_fwd(qh, kh, kvb, cargo=None):
    t = qh.shape[0]
    bq = ATT_BLOCK
    scale = 1.0 / math.sqrt(MLA_QK)
    ncol = MLA_FWD_COLS
    grid = (N_HEADS // ncol, t // bq)

    def body(*refs):
        steps = [pl.program_id(0), pl.program_id(1)]
        _with_cargo(cargo, refs, 3, 2, steps, grid, lambda own: work(steps[1], *own))

    def work(i, q_ref, k_ref, v_ref, o_ref, lse_ref):
        qs = [q_ref[:, _lanes(c)] for c in range(ncol)]

        def step(j, carry, masked):
            cols = range(ncol)
            scores = [_dot(qs[c], _rows(k_ref, j, c), NT_DIMS) for c in cols]
            ms, ps, alphas = [], [], []
            for c in cols:
                s = scores[c] * scale
                if masked:
                    s = jnp.where(_causal(False), s, -1e30)
                m_new = jnp.maximum(carry[c][0], jnp.max(s, axis=-1, keepdims=True))
                ps.append(jnp.exp(s - m_new).astype(BF16))
                alphas.append(jnp.exp(carry[c][0] - m_new))
                ms.append(m_new)
            return tuple((ms[c], alphas[c] * carry[c][1] + _dot(ps[c], _rows(v_ref, j, c), NN_DIMS)) for c in cols)

        init = tuple((jnp.full((bq, 1), -1e30, F32), jnp.zeros((bq, HEAD_PAD), F32)) for _ in range(ncol))
        carry = lax.fori_loop(0, i, lambda j, cr: step(j, cr, False), init)
        for c, (m, acc) in enumerate(step(i, carry, True)):
            l = acc[:, 0:1]
            o_ref[:, _lanes(c)] = (acc / l).astype(BF16)
            lse_ref[c] = m + jnp.log(l)

    width = ncol * HEAD_PAD
    full = pl.BlockSpec((t, width), lambda h, i: (0, h))
    blk = pl.BlockSpec((bq, width), lambda h, i: (i, h))
    extra = cargo.specs() if cargo else []
    outs = pl.pallas_call(
        body, name="mla_fwd", grid=grid,
        in_specs=[blk, full, full] + extra,
        out_specs=[blk, pl.BlockSpec((ncol, bq, 1), lambda h, i: (h, i, 0))] + extra,
        out_shape=[jax.ShapeDtypeStruct((t, N_HEADS * HEAD_PAD), BF16), jax.ShapeDtypeStruct((N_HEADS, t, 1), F32)]
        + (cargo.out_shape() if cargo else []),
        scratch_shapes=cargo.scratch() if cargo else [],
        compiler_params=_params(("arbitrary", "arbitrary")),
    )(qh, kh, kvb, *(cargo.srcs if cargo else []))
    return (outs[0], outs[1], list(outs[2:])) if cargo else outs


def _mla_bwd(qh, kh, kvb, o, do, lse, cargo=None):
    t = qh.shape[0]
    bq = ATT_BLOCK
    scale = 1.0 / math.sqrt(MLA_QK)
    ncol = MLA_BWD_COLS
    width = ncol * HEAD_PAD
    grid = (N_HEADS // ncol, t // bq)

    def body(*refs):
        steps = [pl.program_id(0), pl.program_id(1)]
        _with_cargo(cargo, refs, 6, 3, steps, grid, lambda own: work(steps[0], steps[1], *own))

    def work(h, i, q_ref, k_ref, v_ref, o_ref, do_ref, lse_ref, dq_ref, dk_hbm, dv_hbm, dk_ref, dv_ref, out_sems):

        @pl.when(i == 0)
        def _():
            dk_ref[...] = jnp.zeros_like(dk_ref)
            dv_ref[...] = jnp.zeros_like(dv_ref)

        qs = [q_ref[:, _lanes(c)] for c in range(ncol)]
        dos = [do_ref[:, _lanes(c)] for c in range(ncol)]
        deltas = [jnp.sum(dos[c].astype(F32) * o_ref[:, _lanes(c)].astype(F32), axis=-1, keepdims=True)
                  for c in range(ncol)]
        lses = [lse_ref[c] for c in range(ncol)]

        def step(j, dqs, masked):
            cols = range(ncol)
            kbs = [_rows(k_ref, j, c) for c in cols]
            scores = [_dot(qs[c], kbs[c], NT_DIMS) for c in cols]
            dps = [_dot(dos[c], _rows(v_ref, j, c), NT_DIMS) for c in cols]
            pbs, dss = [], []
            for c in cols:
                p = jnp.exp(scores[c] * scale - lses[c])
                if masked:
                    p = jnp.where(_causal(False), p, 0.0)
                pbs.append(p.astype(BF16))
                dss.append((p * (dps[c] - deltas[c]) * scale).astype(BF16))
            for c in cols:
                dv_ref[_row_block(j), _lanes(c)] += _dot(pbs[c], dos[c], TN_DIMS)
                dk_ref[_row_block(j), _lanes(c)] += _dot(dss[c], qs[c], TN_DIMS)
            return tuple(dqs[c] + _dot(dss[c], kbs[c], NN_DIMS) for c in cols)

        init = tuple(jnp.zeros((bq, HEAD_PAD), F32) for _ in range(ncol))
        dqs = lax.fori_loop(0, i, lambda j, cr: step(j, cr, False), init)
        for c, dq in enumerate(step(i, dqs, True)):
            dq_ref[:, _lanes(c)] = dq

        @pl.when(i == grid[1] - 1)
        def _():
            cols = pl.ds(pl.multiple_of(h * width, width), width)
            out = [pltpu.make_async_copy(dk_ref, dk_hbm.at[:, cols], out_sems.at[0]),
                   pltpu.make_async_copy(dv_ref, dv_hbm.at[:, cols], out_sems.at[1])]
            for cp in out:
                cp.start()
            for cp in out:
                cp.wait()

    full = pl.BlockSpec((t, width), lambda h, i: (0, h))
    blk = pl.BlockSpec((bq, width), lambda h, i: (i, h))
    wide = jax.ShapeDtypeStruct((t, N_HEADS * HEAD_PAD), F32)
    extra = cargo.specs() if cargo else []
    outs = pl.pallas_call(
        body, name="mla_bwd", grid=grid,
        in_specs=[blk, full, full, blk, blk, pl.BlockSpec((ncol, bq, 1), lambda h, i: (h, i, 0))] + extra,
        out_specs=[blk, HBM_SPEC, HBM_SPEC] + extra,
        out_shape=[wide, wide, wide] + (cargo.out_shape() if cargo else []),
        scratch_shapes=[pltpu.VMEM((t, width), F32), pltpu.VMEM((t, width), F32), pltpu.SemaphoreType.DMA((2,))]
        + (cargo.scratch() if cargo else []),
        compiler_params=_params(("arbitrary", "arbitrary")),
    )(qh, kh, kvb, o, do, lse, *(cargo.srcs if cargo else []))
    return (outs[0], outs[1], outs[2], list(outs[3:])) if cargo else outs


def _head_only(x, lane, u):
    return jnp.where((lane >= u * SB_DIM) & (lane < (u + 1) * SB_DIM), x, jnp.zeros_like(x))


SB_DEAD = -104.0


def _log_sigmoids(z):
    e = jnp.exp(-jnp.abs(z))
    lg = jnp.log(1.0 + e)
    ls_pos = jnp.minimum(z, 0.0) - lg
    return ls_pos, ls_pos - z, e


def _sb_fwd(proj):
    t = proj.shape[0]
    bq = ATT_BLOCK
    nq = t // bq
    scale = 1.0 / math.sqrt(SB_DIM)
    pairs = SB_WIDTH // HEAD_PAD

    def body(q_ref, k_ref, v_ref, o_ref, r_ref, first_ref):
        g, i = pl.program_id(0), pl.program_id(1)
        lane = lax.broadcasted_iota(jnp.int32, (bq, HEAD_PAD), 1)
        upper = _tri('gt')
        chains = [(c, u) for c in range(ATT_COLS) for u in range(2)]
        qms = [_head_only(q_ref[:, _lanes(c)], lane, u) * scale for c, u in chains]

        def step(j, carry, masked):
            ids = range(len(chains))
            zs = [_dot(qms[n], _rows(k_ref, j, chains[n][0]), NT_DIMS) for n in ids]
            pos, neg, parts = [], [], []
            for n in ids:
                ls_pos, ls_neg, _ = _log_sigmoids(zs[n])
                if masked:
                    ls_neg = jnp.where(_causal(True), ls_neg, 0.0)
                pos.append(ls_pos)
                neg.append(ls_neg)
                parts.append(_split_bf16(ls_neg))
            suffix = [_dot(parts[n][0], upper, NN_DIMS) + _dot(parts[n][1], upper, NN_DIMS) for n in ids]
            weights = []
            for n in ids:
                a = jnp.exp(pos[n] + suffix[n] + carry[n][0])
                if masked:
                    a = jnp.where(_causal(True), a, 0.0)
                weights.append(a.astype(BF16))
            return tuple((carry[n][0] + jnp.sum(neg[n], axis=-1, keepdims=True),
                          carry[n][1] + _dot(weights[n], _rows(v_ref, j, chains[n][0]), NN_DIMS)) for n in ids)

        init = tuple((jnp.zeros((bq, 1), F32), jnp.zeros((bq, HEAD_PAD), F32)) for _ in chains)
        carry = step(i, init, True)

        def more(state):
            s, cr = state
            live = cr[0][0]
            for n in range(1, len(chains)):
                live = jnp.maximum(live, cr[n][0])
            return jnp.logical_and(s < i, jnp.max(live) > SB_DEAD)

        walked, carry = lax.while_loop(more, lambda st: (st[0] + 1, step(i - 1 - st[0], st[1], False)),
                                       (jnp.int32(0), carry))
        first_ref[g * nq + i] = i - walked
        for n, (c, u) in enumerate(chains):
            r_ref[2 * c + u] = carry[n][0]
        for c in range(ATT_COLS):
            o_ref[:, _lanes(c)] = jnp.where(lane < SB_DIM, carry[2 * c][1], carry[2 * c + 1][1]).astype(BF16)

    width = ATT_COLS * HEAD_PAD

    def full(c0):
        return pl.BlockSpec((t, width), lambda g, i: (0, c0 // width + g))

    return pl.pallas_call(
        body, name="sb_fwd", grid=(pairs // ATT_COLS, t // bq),
        in_specs=[pl.BlockSpec((bq, width), lambda g, i: (i, P_SBQ // width + g)), full(P_SBK), full(P_SBV)],
        out_specs=[pl.BlockSpec((bq, width), lambda g, i: (i, g)),
                   pl.BlockSpec((2 * ATT_COLS, bq, 1), lambda g, i: (g, i, 0)),
                   pl.BlockSpec(memory_space=pltpu.SMEM)],
        out_shape=[jax.ShapeDtypeStruct((t, SB_WIDTH), BF16), jax.ShapeDtypeStruct((N_HEADS, t, 1), F32),
                   jax.ShapeDtypeStruct((pairs // ATT_COLS * nq,), jnp.int32)],
        compiler_params=_params(("arbitrary", "arbitrary")),
    )(proj, proj, proj)


def _sb_bwd(proj, do, rtot, first):
    t = proj.shape[0]
    bq = ATT_BLOCK
    nq = t // bq
    scale = 1.0 / math.sqrt(SB_DIM)
    pairs = SB_WIDTH // HEAD_PAD

    def body(first_ref, q_ref, k_ref, v_ref, do_ref, r_ref, dq_ref, dk_ref, dv_ref):
        g, i = pl.program_id(0), pl.program_id(1)

        @pl.when(i == 0)
        def _():
            dk_ref[...] = jnp.zeros_like(dk_ref)
            dv_ref[...] = jnp.zeros_like(dv_ref)

        lane = lax.broadcasted_iota(jnp.int32, (bq, HEAD_PAD), 1)
        incl = _tri('le')
        excl = _tri('lt')
        chains = [(c, u) for c in range(ATT_COLS) for u in range(2)]
        qms = [_head_only(q_ref[:, _lanes(c)], lane, u) * scale for c, u in chains]
        doms = [_head_only(do_ref[:, _lanes(c)], lane, u) for c, u in chains]
        rts = [r_ref[2 * c + u] for c, u in chains]

        def step(j, carry, masked):
            ids = range(len(chains))
            kbs = [_rows(k_ref, j, c) for c in range(ATT_COLS)]
            zs = [_dot(qms[n], kbs[chains[n][0]], NT_DIMS) for n in ids]
            das = [_dot(doms[n], _rows(v_ref, j, chains[n][0]), NT_DIMS) for n in ids]
            pos, neg, sigs, parts = [], [], [], []
            for n in ids:
                ls_pos, ls_neg, e = _log_sigmoids(zs[n])
                if masked:
                    ls_neg = jnp.where(_causal(True), ls_neg, 0.0)
                pos.append(ls_pos)
                neg.append(ls_neg)
                sigs.append(jnp.where(zs[n] >= 0.0, 1.0, e) * pl.reciprocal(1.0 + e, approx=True))
                parts.append(_split_bf16(ls_neg))
            prefix = [_dot(parts[n][0], incl, NN_DIMS) + _dot(parts[n][1], incl, NN_DIMS) for n in ids]
            evs, eparts, dvs = [], [], []
            for n in ids:
                a = jnp.exp(pos[n] + (rts[n] - (carry[n][0] + prefix[n])))
                if masked:
                    a = jnp.where(_causal(True), a, 0.0)
                dvs.append(_dot(a.astype(BF16), doms[n], TN_DIMS))
                evs.append(a * das[n])
                eparts.append(evs[n].astype(BF16))
            before = [_dot(eparts[n], excl, NN_DIMS) for n in ids]
            out, dks = [], []
            for n in ids:
                dz = evs[n] - sigs[n] * (evs[n] + (carry[n][1] + before[n]))
                if masked:
                    dz = jnp.where(_causal(True), dz, 0.0)
                dzb = dz.astype(BF16)
                dks.append(_dot(dzb, qms[n], TN_DIMS))
                out.append((carry[n][0] + jnp.sum(neg[n], axis=-1, keepdims=True),
                            carry[n][1] + jnp.sum(evs[n], axis=-1, keepdims=True),
                            carry[n][2] + _dot(dzb, kbs[chains[n][0]], NN_DIMS)))
            for c in range(ATT_COLS):
                dv_ref[_row_block(j), _lanes(c)] += dvs[2 * c] + dvs[2 * c + 1]
                dk_ref[_row_block(j), _lanes(c)] += dks[2 * c] + dks[2 * c + 1]
            return tuple(out)

        init = tuple((jnp.zeros((bq, 1), F32), jnp.zeros((bq, 1), F32), jnp.zeros((bq, HEAD_PAD), F32)) for _ in chains)
        carry = lax.fori_loop(first_ref[g * nq + i], i, lambda j, cr: step(j, cr, False), init)
        carry = step(i, carry, True)
        for c in range(ATT_COLS):
            dq_ref[:, _lanes(c)] = jnp.where(lane < SB_DIM, carry[2 * c][2], carry[2 * c + 1][2]) * scale

    width = ATT_COLS * HEAD_PAD

    def full(c0):
        return pl.BlockSpec((t, width), lambda g, i, first: (0, c0 // width + g))

    blk = pl.BlockSpec((bq, width), lambda g, i, first: (i, g))
    acc = pl.BlockSpec((t, width), lambda g, i, first: (0, g))
    wide = jax.ShapeDtypeStruct((t, SB_WIDTH), F32)
    return pl.pallas_call(
        body, name="sb_bwd",
        grid_spec=pltpu.PrefetchScalarGridSpec(
            num_scalar_prefetch=1, grid=(pairs // ATT_COLS, nq),
            in_specs=[pl.BlockSpec((bq, width), lambda g, i, first: (i, P_SBQ // width + g)), full(P_SBK), full(P_SBV),
                      blk, pl.BlockSpec((2 * ATT_COLS, bq, 1), lambda g, i, first: (g, i, 0))],
            out_specs=[blk, acc, acc]),
        out_shape=[wide, wide, wide],
        compiler_params=_params(("arbitrary", "arbitrary")),
    )(first, proj, proj, proj, do, rtot)


def _cols_to_full(g):
    n, r, c = g.shape
    return jnp.transpose(g, (1, 0, 2)).reshape(r, n * c)


def _full_to_cols(w):
    r, c = w.shape
    return jnp.transpose(w.reshape(r, N_DEV, c // N_DEV), (1, 0, 2))


TRANSPOSED = ('ffn1_w_in', 'ffn2_w_in', 'w_in', 'w_q_up')


def _layout_weight(name, g):
    if name in ('ffn1_w_out', 'ffn2_w_out', 'w_out', 'w_ple_gate', 'ffn1_w_in', 'ffn2_w_in'):
        return g.reshape(g.shape[0] * g.shape[1], g.shape[2])
    if name == 'w_in':
        wt = g.reshape(IN_COLS, D_MODEL)
        z = lambda n: jnp.zeros((n, D_MODEL), BF16)
        return jnp.concatenate([wt[0:640], z(64), wt[640:672], z(32), z(256), wt[2208:4256], wt[672:2208]], axis=0)
    if name == 'w_q_up':
        return jnp.pad(g, ((0, 0), (0, HEAD_PAD - MLA_QK), (0, 0))).reshape(N_HEADS * HEAD_PAD, Q_LORA)
    if name == 'w_branch_mla':
        bm = _cols_to_full(g).reshape(N_HEADS, MLA_NOPE, D_MODEL)
        return jnp.pad(bm, ((0, 0), (HEAD_PAD - MLA_NOPE, 0), (0, 0))).reshape(N_HEADS * HEAD_PAD, D_MODEL)
    return _cols_to_full(g)


def _layout_weights(g):
    return {n: _layout_weight(n, a) for n, a in g.items()}


def _unlayout_grad(name, d):
    if name == 'w_in':
        d = jnp.concatenate([d[0:640], d[704:736], d[P_SBQ:PROJ_W], d[P_GM:P_SBQ]], axis=0)
    if name == 'w_q_up':
        return d.reshape(N_HEADS, HEAD_PAD, Q_LORA)[:, :MLA_QK, :]
    if name in ('ffn1_w_out', 'ffn2_w_out', 'w_out', 'w_ple_gate', 'ffn1_w_in', 'ffn2_w_in', 'w_in'):
        return d.reshape(N_DEV, d.shape[0] // N_DEV, d.shape[1])
    if name == 'w_branch_mla':
        d = d.reshape(N_HEADS, HEAD_PAD, D_MODEL)[:, HEAD_PAD - MLA_NOPE:, :].reshape(SB_WIDTH, D_MODEL)
    return _full_to_cols(d)


def _unlayout_grads(d):
    return {n: _unlayout_grad(n, a) for n, a in d.items()}


def _rope_tables(positions):
    half = MLA_ROPE // 2
    inv_freq = ROPE_BASE ** (-jnp.arange(0, MLA_ROPE, 2, dtype=F32) / MLA_ROPE)
    ang = positions.astype(F32)[:, None] * inv_freq
    cos, sin = jnp.cos(ang), jnp.sin(ang)
    t = positions.shape[0]
    ones = lambda n: jnp.ones((t, n), F32)
    zeros = lambda n: jnp.zeros((t, n), F32)
    cosf = jnp.concatenate([ones(MLA_NOPE), cos, cos, ones(HEAD_PAD - MLA_QK)], axis=1)
    sin_a = jnp.concatenate([zeros(MLA_NOPE), -sin, zeros(half), zeros(HEAD_PAD - MLA_QK)], axis=1)
    sin_b = jnp.concatenate([zeros(MLA_NOPE), zeros(half), sin, zeros(HEAD_PAD - MLA_QK)], axis=1)
    return cosf, sin_a, sin_b


def _local_step(x, p, positions, tgt, norms, plan):
    mm = _matmul
    cosf, sin_a, sin_b = _rope_tables(positions)
    pad_head = lambda g: jnp.pad(g, ((0, 0), (0, HEAD_PAD - MLA_QK)))
    gqh, gkh = pad_head(norms['q_head_norm']), pad_head(norms['k_head_norm'])
    pb = p.astype(BF16)
    w = dict(plan.first_weights())
    dw, dn = {}, {}

    def ride(host, call):
        cargo = plan.cargo(host, dw)
        res, lands = call(cargo), None
        if cargo is not None:
            *res, lands = res
            res = res[0] if len(res) == 1 else tuple(res)
        w.update(plan.landed(host, lands))
        return res

    def ffn_fwd(h, tag):
        n = _rmsnorm_fwd(h, norms[tag + '_norm'], tag + "_norm_fwd")
        ab = ride(tag + "_in_fwd", lambda cargo: mm(
            n, w[tag + '_w_in'], mode='nt', out_dtype=BF16, name=tag + "_in_fwd", cargo=cargo))
        act = _swiglu_fwd(ab, tag + "_swiglu_fwd")
        out = ride(tag + "_out_fwd", lambda cargo: mm(
            act, w[tag + '_w_out'], mode='nn', out_dtype=F32, name=tag + "_out_fwd", res=h, alpha=0.5, cargo=cargo))
        return out, (n, ab, act)

    h1, ffn1_saved = ffn_fwd(x, 'ffn1')
    u = _rmsnorm_fwd(h1, norms['mix_norm'], "mix_norm_fwd")
    proj = mm(u, w['w_in'], mode='nt', out_dtype=BF16, name="proj_fwd")
    cqn, ckvn = _latent_fwd(proj, norms['q_latent_norm'], norms['kv_latent_norm'])
    qraw = mm(cqn, w['w_q_up'], mode='nt', out_dtype=F32, name="q_up_fwd")
    kvraw = mm(ckvn, w['w_kv_up'], mode='nn', out_dtype=F32, name="kv_up_fwd")
    qh, kh, kvb = _headprep_fwd(qraw, kvraw, proj, cosf, sin_a, sin_b, gqh, gkh)
    o_mla, lse = ride("mla_fwd", lambda cargo: _mla_fwd(qh, kh, kvb, cargo))
    o_sb, rtot, sb_first = _sb_fwd(proj)
    bm = mm(o_mla, w['w_branch_mla'], mode='nn', out_dtype=F32, name="branch_mla_fwd")
    bs = mm(o_sb, w['w_branch_sb'], mode='nn', out_dtype=F32, name="branch_sb_fwd")
    merged = _merge_fwd(proj, bm, bs)
    h2 = mm(merged, w['w_out'], mode='nn', out_dtype=F32, name="mix_out_fwd", res=h1)
    h3, ffn2_saved = ffn_fwd(h2, 'ffn2')
    n3 = _rmsnorm_fwd(h3, norms['ple_norm'], "ple_norm_fwd")
    zg = mm(n3, w['w_ple_gate'], mode='nn', out_dtype=F32, name="ple_gate_fwd")
    pp = mm(pb, w['w_ple_proj'], mode='nn', out_dtype=F32, name="ple_proj_fwd")
    dh4, dzg, dpp, loss_lanes = _ple_loss(h3, zg, pp, tgt)

    dw['w_ple_gate'] = mm(n3, dzg, mode='tn', out_dtype=BF16, name="ple_gate_dw")
    dw['w_ple_proj'] = mm(pb, dpp, mode='tn', out_dtype=BF16, name="ple_proj_dw")
    dn3 = mm(dzg, w['w_ple_gate'], mode='nt', out_dtype=F32, name="ple_gate_dx")
    dh3, dhb3, dn['ple_norm'] = _rmsnorm_bwd(dn3, h3, norms['ple_norm'], dh4, "ple_norm_bwd", 0.5)

    def ffn_bwd(h, dh, dhb, saved, tag, out_scale):
        n, ab, act = saved
        dw[tag + '_w_out'] = mm(act, dhb, mode='tn', out_dtype=BF16, name=tag + "_out_dw", tm=1408)
        dact = mm(dhb, w[tag + '_w_out'], mode='nt', out_dtype=BF16, name=tag + "_out_dx", tn=1408)
        dab = _swiglu_bwd(ab, dact, tag + "_swiglu_bwd")
        dw[tag + '_w_in'] = ride(tag + "_in_dw", lambda cargo: mm(
            dab, n, mode='tn', out_dtype=BF16, name=tag + "_in_dw", tm=1408, cargo=cargo))
        dnn = ride(tag + "_in_dx", lambda cargo: mm(
            dab, w[tag + '_w_in'], mode='nn', out_dtype=F32, name=tag + "_in_dx", cargo=cargo))
        dh_prev, dhb_prev, dn[tag + '_norm'] = _rmsnorm_bwd(dnn, h, norms[tag + '_norm'], dh, tag + "_norm_bwd", out_scale)
        return dh_prev, dhb_prev

    dh2, dhb2 = ffn_bwd(h2, dh3, dhb3, ffn2_saved, 'ffn2', 1.0)
    dw['w_out'] = mm(merged, dhb2, mode='tn', out_dtype=BF16, name="mix_out_dw")
    dmerged = mm(dhb2, w['w_out'], mode='nt', out_dtype=F32, name="mix_out_dx")
    dbm, dbs, dgates = _merge_bwd(dmerged, proj, bm, bs)
    dw['w_branch_mla'] = mm(o_mla, dbm, mode='tn', out_dtype=BF16, name="branch_mla_dw")
    dw['w_branch_sb'] = mm(o_sb, dbs, mode='tn', out_dtype=BF16, name="branch_sb_dw")
    do_mla = mm(dbm, w['w_branch_mla'], mode='nt', out_dtype=BF16, name="branch_mla_dx")
    do_sb = mm(dbs, w['w_branch_sb'], mode='nt', out_dtype=BF16, name="branch_sb_dx")
    dqh, dkh, dvp = ride("mla_bwd", lambda cargo: _mla_bwd(qh, kh, kvb, o_mla, do_mla, lse, cargo))
    dsq, dsk, dsv = _sb_bwd(proj, do_sb, rtot, sb_first)
    dqraw, dkvraw, dkr, dgq, dgk = _headprep_bwd(dqh, dkh, dvp, qraw, kvraw, proj, cosf, sin_a, sin_b, gqh, gkh)
    dn['q_head_norm'], dn['k_head_norm'] = dgq[:, :MLA_QK], dgk[:, :MLA_QK]
    dw['w_q_up'] = mm(dqraw, cqn, mode='tn', out_dtype=BF16, name="q_up_dw")
    dw['w_kv_up'] = mm(ckvn, dkvraw, mode='tn', out_dtype=BF16, name="kv_up_dw")
    dcqn = mm(dqraw, w['w_q_up'], mode='nn', out_dtype=F32, name="q_up_dx")
    dckvn = mm(dkvraw, w['w_kv_up'], mode='nt', out_dtype=F32, name="kv_up_dx")
    dlat, dn['q_latent_norm'], dn['kv_latent_norm'] = _latent_bwd(
        dcqn, dckvn, proj, dkr, norms['q_latent_norm'], norms['kv_latent_norm'])
    dproj = jnp.concatenate([dlat, dgates, dsq.astype(BF16), dsk.astype(BF16), dsv.astype(BF16)], axis=1)
    dw['w_in'] = ride("proj_dw", lambda cargo: mm(dproj, u, mode='tn', out_dtype=BF16, name="proj_dw", tm=1536, cargo=cargo))
    du = ride("proj_dx", lambda cargo: mm(dproj, w['w_in'], mode='nn', out_dtype=F32, name="proj_dx", cargo=cargo))
    dh1, dhb1, dn['mix_norm'] = _rmsnorm_bwd(du, h1, norms['mix_norm'], dh2, "mix_norm_bwd", 0.5)
    dx, _ = ffn_bwd(x, dh1, dhb1, ffn1_saved, 'ffn1', 1.0)
    return dx, loss_lanes, dw, dn


MESH = pl.DeviceIdType.MESH
HBM_SPEC = pl.BlockSpec(memory_space=pl.ANY)


def _position():
    return lax.axis_index("x"), lax.axis_index("y"), lax.axis_index("c")


def _index(px, py, pc):
    return 4 * px + 2 * py + pc


def _all_gather(shards):
    n = len(shards)

    def body(*refs):
        ins, outs = refs[:n], refs[n:2 * n]
        send_sems, recv_sems, local_sems = refs[2 * n:]
        x, y, c = _position()
        me, sibling = (x, y, c), (x, y, 1 - c)
        chips = [(1 - x, y), (x, 1 - y), (1 - x, 1 - y)]

        def copy(a, k, block, to, own=False):
            dst = outs[a].at[_index(*block)]
            return pltpu.make_async_remote_copy(
                src_ref=ins[a] if own else dst, dst_ref=dst,
                send_sem=send_sems.at[a, k], recv_sem=recv_sems.at[a, k], device_id=to, device_id_type=MESH)

        mine = [pltpu.make_async_copy(ins[a], outs[a].at[_index(*me)], local_sems.at[a]) for a in range(n)]
        for cp in mine:
            cp.start()
        first = []
        for a in range(n):
            first.append(copy(a, 0, me, sibling, own=True))
            first += [copy(a, 1 + j, me, (*chip, c), own=True) for j, chip in enumerate(chips)]
        for cp in first:
            cp.start()
        passed = []
        for j, chip in enumerate(chips):
            for a in range(n):
                copy(a, 1 + j, (*chip, c), me).wait_recv()
                fwd = copy(a, 4 + j, (*chip, c), sibling)
                fwd.start()
                passed.append(fwd)
        for a in range(n):
            copy(a, 0, sibling, me).wait_recv()
            for j, chip in enumerate(chips):
                copy(a, 4 + j, (*chip, 1 - c), me).wait_recv()
        for cp in first + passed:
            cp.wait_send()
        for cp in mine:
            cp.wait()

    return pl.pallas_call(
        body, name="weights_all_gather",
        in_specs=[HBM_SPEC] * n, out_specs=[HBM_SPEC] * n,
        out_shape=[jax.ShapeDtypeStruct((N_DEV,) + s.shape, s.dtype) for s in shards],
        scratch_shapes=[pltpu.SemaphoreType.DMA((n, 7)), pltpu.SemaphoreType.DMA((n, 7)), pltpu.SemaphoreType.DMA((n,))],
    )(*shards)


def _exchange(parts):
    n = len(parts)
    masks = [(mx, my, mc) for mx in (0, 1) for my in (0, 1) for mc in (0, 1)][1:]

    def body(*refs):
        ins, outs = refs[:n], refs[n:2 * n]
        send_sems, recv_sems, local_sems = refs[2 * n:]
        x, y, c = _position()
        me = _index(x, y, c)

        def peer_of(mask):
            mx, my, mc = mask
            return (x + mx - 2 * x * mx, y + my - 2 * y * my, c + mc - 2 * c * mc)

        def copy(a, k):
            peer = peer_of(masks[k])
            return pltpu.make_async_remote_copy(
                src_ref=ins[a].at[_index(*peer)], dst_ref=outs[a].at[me],
                send_sem=send_sems.at[a, k], recv_sem=recv_sems.at[a, k], device_id=peer, device_id_type=MESH)

        def landed(a, k):
            peer = peer_of(masks[k])
            return pltpu.make_async_remote_copy(
                src_ref=ins[a].at[me], dst_ref=outs[a].at[_index(*peer)],
                send_sem=send_sems.at[a, k], recv_sem=recv_sems.at[a, k], device_id=peer, device_id_type=MESH)

        mine = [pltpu.make_async_copy(ins[a].at[me], outs[a].at[me], local_sems.at[a]) for a in range(n)]
        for cp in mine:
            cp.start()
        sent = [copy(a, k) for k in range(7) for a in range(n)]
        for cp in sent:
            cp.start()
        for k in range(7):
            for a in range(n):
                landed(a, k).wait_recv()
        for cp in sent:
            cp.wait_send()
        for cp in mine:
            cp.wait()

    return pl.pallas_call(
        body, name="grads_exchange",
        in_specs=[HBM_SPEC] * n, out_specs=[HBM_SPEC] * n,
        out_shape=[jax.ShapeDtypeStruct(s.shape, s.dtype) for s in parts],
        scratch_shapes=[pltpu.SemaphoreType.DMA((n, 7)), pltpu.SemaphoreType.DMA((n, 7)), pltpu.SemaphoreType.DMA((n,))],
    )(*parts)


PEER_MASKS = [(mx, my, mc) for mx in (0, 1) for my in (0, 1) for mc in (0, 1)][1:]


def _peer(mask):
    x, y, c = _position()
    mx, my, mc = mask
    return (x + mx - 2 * x * mx, y + my - 2 * y * my, c + mc - 2 * c * mc)


class _Cargo:
    def __init__(self, srcs, scatter):
        self.srcs, self.scatter, self.n = list(srcs), scatter, len(srcs)

    def specs(self):
        return [HBM_SPEC] * self.n

    def out_shape(self):
        return [jax.ShapeDtypeStruct(s.shape if self.scatter else (N_DEV,) + s.shape, s.dtype) for s in self.srcs]

    def scratch(self):
        per_copy = pltpu.SemaphoreType.DMA((self.n, len(PEER_MASKS)))
        return [per_copy, per_copy, pltpu.SemaphoreType.DMA((self.n,))]

    def _mine(self, src_refs, a, to):
        return src_refs[a].at[to] if self.scatter else src_refs[a]

    def _shard_copy(self, src_refs, land_refs, sems, a, k, block, to, own=False):
        dst = land_refs[a].at[_index(*block)]
        return pltpu.make_async_remote_copy(
            src_ref=src_refs[a] if own else dst, dst_ref=dst,
            send_sem=sems[0].at[a, k], recv_sem=sems[1].at[a, k], device_id=to, device_id_type=MESH)

    def _first_hops(self, src_refs, land_refs, sems):
        x, y, c = _position()
        chips = [(1 - x, y), (x, 1 - y), (1 - x, 1 - y)]
        hops = []
        for a in range(self.n):
            hops.append(self._shard_copy(src_refs, land_refs, sems, a, 0, (x, y, c), (x, y, 1 - c), own=True))
            hops += [self._shard_copy(src_refs, land_refs, sems, a, 1 + j, (x, y, c), (*chip, c), own=True)
                     for j, chip in enumerate(chips)]
        return hops, chips

    def start(self, src_refs, land_refs, sems):
        send, recv, local = sems
        me = _index(*_position())
        for a in range(self.n):
            pltpu.make_async_copy(self._mine(src_refs, a, me), land_refs[a].at[me], local.at[a]).start()
        if not self.scatter:
            for cp in self._first_hops(src_refs, land_refs, sems)[0]:
                cp.start()
            return
        for k, mask in enumerate(PEER_MASKS):
            peer = _peer(mask)
            for a in range(self.n):
                pltpu.make_async_remote_copy(
                    src_ref=self._mine(src_refs, a, _index(*peer)), dst_ref=land_refs[a].at[me],
                    send_sem=send.at[a, k], recv_sem=recv.at[a, k], device_id=peer, device_id_type=MESH).start()

    def _wait_gathered(self, src_refs, land_refs, sems):
        x, y, c = _position()
        me, sibling = (x, y, c), (x, y, 1 - c)
        first, chips = self._first_hops(src_refs, land_refs, sems)
        passed = []
        for j, chip in enumerate(chips):
            for a in range(self.n):
                self._shard_copy(src_refs, land_refs, sems, a, 1 + j, (*chip, c), me).wait_recv()
                passed.append(self._shard_copy(src_refs, land_refs, sems, a, 4 + j, (*chip, c), sibling))
                passed[-1].start()
        for a in range(self.n):
            self._shard_copy(src_refs, land_refs, sems, a, 0, sibling, me).wait_recv()
            for j, chip in enumerate(chips):
                self._shard_copy(src_refs, land_refs, sems, a, 4 + j, (*chip, 1 - c), me).wait_recv()
        for cp in first + passed:
            cp.wait_send()

    def wait(self, src_refs, land_refs, sems):
        send, recv, local = sems
        me = _index(*_position())
        if not self.scatter:
            self._wait_gathered(src_refs, land_refs, sems)
        for k, mask in enumerate(PEER_MASKS if self.scatter else []):
            peer = _peer(mask)
            there = _index(*peer)
            for a in range(self.n):
                pltpu.make_async_remote_copy(
                    src_ref=self._mine(src_refs, a, me), dst_ref=land_refs[a].at[there],
                    send_sem=send.at[a, k], recv_sem=recv.at[a, k], device_id=peer, device_id_type=MESH).wait_recv()
                pltpu.make_async_remote_copy(
                    src_ref=self._mine(src_refs, a, there), dst_ref=land_refs[a].at[me],
                    send_sem=send.at[a, k], recv_sem=recv.at[a, k], device_id=peer, device_id_type=MESH).wait_send()
        for a in range(self.n):
            pltpu.make_async_copy(self._mine(src_refs, a, me), land_refs[a].at[me], local.at[a]).wait()


def _with_cargo(cargo, refs, n_in, n_out, steps, counts, compute):
    if cargo is None:
        compute(refs)
        return
    n = cargo.n
    src_refs = refs[n_in:n_in + n]
    land_refs = refs[n_in + n + n_out:n_in + 2 * n + n_out]
    sems = refs[-3:]
    first = functools.reduce(jnp.logical_and, [s == 0 for s in steps])
    last = functools.reduce(jnp.logical_and, [s == c - 1 for s, c in zip(steps, counts)])

    @pl.when(first)
    def _():
        cargo.start(src_refs, land_refs, sems)

    compute(refs[:n_in] + refs[n_in + n:n_in + n + n_out] + refs[n_in + 2 * n + n_out:-3])

    @pl.when(last)
    def _():
        cargo.wait(src_refs, land_refs, sems)


def _adamw(parts, w, m, v, name):
    r, c = w.shape
    tr = next((t for t in (512, 384, 352, 256, 128) if r % t == 0), r) if r > 512 else r
    tc = c if tr < r or r <= 512 else 256
    assert r % tr == 0 and c % tc == 0
    bc1 = 1.0 - ADAM_B1 ** ADAM_STEP
    bc2 = 1.0 - ADAM_B2 ** ADAM_STEP

    def body(p_ref, w_ref, m_ref, v_ref, g_ref, d_ref, nm_ref, nv_ref):
        g = p_ref[0].astype(F32)
        for s in range(1, N_DEV):
            g = g + p_ref[s].astype(F32)
        nm = ADAM_B1 * m_ref[...] + (1.0 - ADAM_B1) * g
        nv = ADAM_B2 * v_ref[...] + (1.0 - ADAM_B2) * (g * g)
        g_ref[...] = g
        nm_ref[...] = nm
        nv_ref[...] = nv
        d_ref[...] = -ADAM_LR * ((nm / bc1) / (jnp.sqrt(nv / bc2) + ADAM_EPS) + ADAM_WD * w_ref[...])

    tile = pl.BlockSpec((tr, tc), lambda i, j: (i, j))
    out = jax.ShapeDtypeStruct((r, c), F32)
    return pl.pallas_call(
        body, name=name, grid=(r // tr, c // tc),
        in_specs=[pl.BlockSpec((N_DEV, tr, tc), lambda i, j: (0, i, j)), tile, tile, tile],
        out_specs=[tile] * 4, out_shape=[out] * 4,
        compiler_params=_params(("parallel", "parallel")),
    )(parts, w, m, v)


GATHER_FIRST = ['ffn1_w_in']
RIDES = {
    'ffn1_in_fwd': ('weights', ['ffn1_w_out', 'w_in']),
    'ffn1_out_fwd': ('weights', ['w_q_up', 'w_kv_up', 'w_branch_mla', 'w_branch_sb', 'w_out']),
    'mla_fwd': ('weights', ['ffn2_w_in', 'ffn2_w_out', 'w_ple_gate', 'w_ple_proj']),
    'mla_bwd': ('grads', ['w_ple_gate', 'w_ple_proj', 'ffn2_w_out', 'ffn2_w_in', 'w_out', 'w_branch_mla', 'w_branch_sb']),
    'proj_dw': ('grads', ['w_q_up', 'w_kv_up']),
    'proj_dx': ('grads', ['w_in']),
    'ffn1_in_dw': ('grads', ['ffn1_w_out']),
    'ffn1_in_dx': ('grads', ['ffn1_w_in']),
}


class _Plan:
    def __init__(self, shards):
        self.shards = shards
        self.received = {}

    def first_weights(self):
        gathered = _all_gather([self.shards[n] for n in GATHER_FIRST])
        return {n: _layout_weight(n, g) for n, g in zip(GATHER_FIRST, gathered)}

    def cargo(self, host, dw):
        if host not in RIDES:
            return None
        kind, names = RIDES[host]
        if kind == 'weights':
            return _Cargo([self.shards[n] for n in names], False)
        return _Cargo([_unlayout_grad(n, dw.pop(n)) for n in names], True)

    def landed(self, host, lands):
        if host not in RIDES:
            return {}
        kind, names = RIDES[host]
        if kind == 'weights':
            return {n: _layout_weight(n, land) for n, land in zip(names, lands)}
        self.received.update(zip(names, lands))
        return {}


def _pack_small(vecs):
    flat = jnp.concatenate([v.reshape(-1) for v in vecs])
    return jnp.pad(flat, (0, SMALL_ROWS * 128 - flat.shape[0])).reshape(SMALL_ROWS, 128)


def _unpack_small(packed, sizes):
    flat = packed.reshape(-1)
    out, at = [], 0
    for n in sizes:
        out.append(flat[at:at + n])
        at += n
    return out


def kernel(x, p, positions, ffn1_norm, ffn1_w_in, ffn1_w_out, mix_norm, w_in, q_latent_norm, w_q_up, kv_latent_norm, w_kv_up, q_head_norm, k_head_norm, w_branch_mla, w_branch_sb, w_out, ffn2_norm, ffn2_w_in, ffn2_w_out, ple_norm, w_ple_gate, w_ple_proj, loss_target, m_ffn1_norm, m_ffn1_w_in, m_ffn1_w_out, m_mix_norm, m_w_in, m_q_latent_norm, m_w_q_up, m_kv_latent_norm, m_w_kv_up, m_q_head_norm, m_k_head_norm, m_w_branch_mla, m_w_branch_sb, m_w_out, m_ffn2_norm, m_ffn2_w_in, m_ffn2_w_out, m_ple_norm, m_w_ple_gate, m_w_ple_proj, v_ffn1_norm, v_ffn1_w_in, v_ffn1_w_out, v_mix_norm, v_w_in, v_q_latent_norm, v_w_q_up, v_kv_latent_norm, v_w_kv_up, v_q_head_norm, v_k_head_norm, v_w_branch_mla, v_w_branch_sb, v_w_out, v_ffn2_norm, v_ffn2_w_in, v_ffn2_w_out, v_ple_norm, v_w_ple_gate, v_w_ple_proj):
    given = dict(locals())
    wts = {n: given[n] for n in WEIGHTS}
    mom = {n: given['m_' + n] for n in WEIGHTS}
    var = {n: given['v_' + n] for n in WEIGHTS}

    def local(a, n):
        return jnp.swapaxes(a[0], 0, 1) if n in TRANSPOSED else a[0]

    plan = _Plan({n: local(wts[n], n).astype(BF16) for n in MATS})
    norms = {n: wts[n] for n in NORMS}
    dx, loss_lanes, dw, dn = _local_step(x[0], p[0, 0], positions[0], loss_target[0], norms, plan)
    assert not dw

    out = {}
    for n in MATS:
        res = _adamw(plan.received[n], local(wts[n], n), local(mom[n], n), local(var[n], n), "adamw_" + n)
        out[n] = [local(r[None], n)[None] for r in res]
    small = _pack_small([dn[n] for n in NORMS] + [0.5 / D_MODEL * jnp.sum(loss_lanes)[None]])
    small_parts = _exchange([jnp.broadcast_to(small[None], (N_DEV, SMALL_ROWS, 128))])[0]
    sizes = [wts[n].shape[1] for n in NORMS]
    pack = lambda d: _pack_small([d[n] for n in NORMS])
    small_res = _adamw(small_parts, pack(wts), pack(mom), pack(var), "adamw_norms")
    loss = small_res[0].reshape(-1)[sum(sizes)]
    for i, res in enumerate(small_res):
        for n, vec in zip(NORMS, _unpack_small(res, sizes)):
            out.setdefault(n, [None] * 4)[i] = vec[None]

    return (loss, dx[None], *[out[n][0] for n in WEIGHTS], *[out[n][1] for n in WEIGHTS],
            *[out[n][2] for n in WEIGHTS], *[out[n][3] for n in WEIGHTS])
```

```python
import functools
import math

import jax
import jax.numpy as jnp
from jax import lax
from jax.experimental import pallas as pl
from jax.experimental.pallas import tpu as pltpu

F32 = jnp.float32
BF16 = jnp.bfloat16

N_DEV = 8
D_MODEL = 1024
D_FF = 2816
PLE_DIM = 256
NORM_EPS = 1e-6
N_HEADS = 8
HEAD_PAD = 128
MLA_NOPE = 64
MLA_ROPE = 32
MLA_QK = 96
Q_LORA = 384
KV_LORA = 256
SB_DIM = 64
SB_WIDTH = 512
ROPE_BASE = 10000.0
IN_COLS = 4256

PROJ_W = 4608
P_CQ, P_CKV, P_KR, P_GM, P_GS, P_SBQ, P_SBK, P_SBV = 0, 384, 640, 1024, 2048, 3072, 3584, 4096

ADAM_LR, ADAM_B1, ADAM_B2, ADAM_EPS, ADAM_WD, ADAM_STEP = 0.001, 0.9, 0.999, 1e-08, 0.01, 10

VMEM_LIMIT = 52 * 1024 * 1024
MATMUL_VMEM = 40 * 1024 * 1024

WEIGHTS = ['ffn1_norm', 'ffn1_w_in', 'ffn1_w_out', 'mix_norm', 'w_in', 'q_latent_norm', 'w_q_up',
           'kv_latent_norm', 'w_kv_up', 'q_head_norm', 'k_head_norm', 'w_branch_mla', 'w_branch_sb',
           'w_out', 'ffn2_norm', 'ffn2_w_in', 'ffn2_w_out', 'ple_norm', 'w_ple_gate', 'w_ple_proj']
NORMS = ['ffn1_norm', 'mix_norm', 'q_latent_norm', 'kv_latent_norm', 'q_head_norm', 'k_head_norm',
         'ffn2_norm', 'ple_norm']
MATS = [n for n in WEIGHTS if n not in NORMS]
SMALL_ROWS = 48

NT_DIMS = (((1,), (1,)), ((), ()))
NN_DIMS = (((1,), (0,)), ((), ()))
TN_DIMS = (((0,), (0,)), ((), ()))


def _params(sem=None, vmem=VMEM_LIMIT):
    return pltpu.CompilerParams(dimension_semantics=sem, vmem_limit_bytes=vmem)


def _pick(n, cap):
    if n <= cap:
        return n
    best = None
    for t in range(128, cap + 1, 128):
        if n % t == 0:
            best = t
    assert best is not None, (n, cap)
    return best


def _dot(a, b, dims):
    return lax.dot_general(a, b, dims, preferred_element_type=F32)


def _matmul(a, b, *, mode, out_dtype, name, tm=None, tn=None, tk=None, res=None, alpha=1.0, cargo=None):
    if mode == 'nn':
        (m, k), (k2, n) = a.shape, b.shape
    elif mode == 'nt':
        (m, k), (n, k2) = a.shape, b.shape
    else:
        (k, m), (k2, n) = a.shape, b.shape
    assert k == k2, (name, a.shape, b.shape)
    has_res = res is not None
    tn = tn or _pick(n, 512)

    def vmem(tm_, tk_):
        io = 2 * 2 * (tm_ * tk_ + tk_ * tn) + 2 * tm_ * tn * (jnp.dtype(out_dtype).itemsize + 4 * has_res)
        return io + (4 * tm_ * tn if tk_ < k else 0)

    tries = [(tm_, tk_) for tk_ in ([tk] if tk else [k, _pick(k, 2048)])
             for tm_ in ([tm] if tm else [_pick(m, 2048), _pick(m, 1024), _pick(m, 512)])]
    tm, tk = next((c for c in tries if vmem(*c) <= MATMUL_VMEM), tries[-1])
    assert m % tm == 0 and n % tn == 0 and k % tk == 0, (name, m, n, k, tm, tn, tk)
    nk = k // tk
    dims = {'nn': NN_DIMS, 'nt': NT_DIMS, 'tn': TN_DIMS}[mode]

    def epilogue(acc, r_ref, o_ref):
        if alpha != 1.0:
            acc = acc * alpha
        if has_res:
            acc = r_ref[...] + acc
        o_ref[...] = acc.astype(out_dtype)

    grid = (m // tm, n // tn, nk)

    def body(*refs):
        steps = [pl.program_id(d) for d in range(3)]

        def compute(own):
            a_ref, b_ref = own[0], own[1]
            r_ref = own[2] if has_res else None
            o_ref = own[2 + has_res]
            if nk == 1:
                epilogue(_dot(a_ref[...], b_ref[...], dims), r_ref, o_ref)
                return
            acc_ref = own[-1]

            @pl.when(steps[2] == 0)
            def _():
                acc_ref[...] = jnp.zeros_like(acc_ref)

            acc_ref[...] += _dot(a_ref[...], b_ref[...], dims)

            @pl.when(steps[2] == nk - 1)
            def _():
                epilogue(acc_ref[...], r_ref, o_ref)

        _with_cargo(cargo, refs, 2 + has_res, 1, steps, grid, compute)

    if mode == 'tn':
        a_spec = pl.BlockSpec((tk, tm), lambda i, j, kk: (kk, i))
    else:
        a_spec = pl.BlockSpec((tm, tk), lambda i, j, kk: (i, kk))
    if mode == 'nt':
        b_spec = pl.BlockSpec((tn, tk), lambda i, j, kk: (j, kk))
    else:
        b_spec = pl.BlockSpec((tk, tn), lambda i, j, kk: (kk, j))
    o_spec = pl.BlockSpec((tm, tn), lambda i, j, kk: (i, j))
    in_specs = [a_spec, b_spec] + ([o_spec] if has_res else [])
    args = (a, b) + ((res,) if has_res else ())
    out_shape = jax.ShapeDtypeStruct((m, n), out_dtype)
    scratch = [pltpu.VMEM((tm, tn), F32)] if nk > 1 else []
    if cargo is None:
        return pl.pallas_call(
            body, name=name, grid=grid, in_specs=in_specs, out_specs=o_spec, out_shape=out_shape,
            scratch_shapes=scratch, compiler_params=_params(("parallel", "parallel", "arbitrary")),
        )(*args)
    outs = pl.pallas_call(
        body, name=name, grid=grid, in_specs=in_specs + cargo.specs(), out_specs=[o_spec] + cargo.specs(),
        out_shape=[out_shape] + cargo.out_shape(), scratch_shapes=scratch + cargo.scratch(),
        compiler_params=_params(("arbitrary", "arbitrary", "arbitrary")),
    )(*args, *cargo.srcs)
    return outs[0], list(outs[1:])


def _row_tile(t, cap=512):
    return min(t, cap)


def _rms(x, width):
    return lax.rsqrt(jnp.sum(x * x, axis=-1, keepdims=True) * (1.0 / width) + NORM_EPS)


def _rmsnorm_fwd(x, g, name):
    t, d = x.shape
    tr = _row_tile(t)

    def body(x_ref, g_ref, o_ref):
        xv = x_ref[...]
        o_ref[...] = ((xv * _rms(xv, d)) * g_ref[...]).astype(BF16)

    return pl.pallas_call(
        body, name=name, grid=(t // tr,),
        in_specs=[pl.BlockSpec((tr, d), lambda i: (i, 0)), pl.BlockSpec((1, d), lambda i: (0, 0))],
        out_specs=pl.BlockSpec((tr, d), lambda i: (i, 0)),
        out_shape=jax.ShapeDtypeStruct((t, d), BF16),
        compiler_params=_params(("parallel",)),
    )(x, g)


def _matmul_norm_bwd(a, b, x, g, dh_in, *, mode, name, out_scale, cargo=None):
    m, k = a.shape
    d = x.shape[1]
    tn = _pick(d, 512)

    def vmem(tm_):
        return 2 * 2 * (tm_ * k + k * tn) + tm_ * d * (4 + 2 * (4 + 4) + 2 * (4 + 2))

    tm = next((c for c in (_pick(m, 1024), _pick(m, 512), _pick(m, 256)) if vmem(c) <= MATMUL_VMEM), _pick(m, 256))
    grid = (m // tm, d // tn)
    dims = {'nn': NN_DIMS, 'nt': NT_DIMS}[mode]

    def body(*refs):
        steps = [pl.program_id(0), pl.program_id(1)]

        def compute(own):
            a_ref, b_ref, x_ref, g_ref, dhin_ref, dh_ref, dhb_ref, dg_ref, dn_ref = own
            for jj in range(grid[1]):
                @pl.when(steps[1] == jj)
                def _(jj=jj):
                    dn_ref[:, jj * tn:(jj + 1) * tn] = _dot(a_ref[...], b_ref[...], dims)

            @pl.when(steps[1] == grid[1] - 1)
            def _():
                xv = x_ref[...]
                dnv = dn_ref[...]
                r = _rms(xv, d)
                y = xv * r
                dy = dnv * g_ref[...]
                dh = dhin_ref[...] + r * (dy - y * (jnp.sum(dy * y, axis=-1, keepdims=True) * (1.0 / d)))
                dh_ref[...] = dh
                dhb_ref[...] = (dh * out_scale).astype(BF16)
                part = jnp.sum(dnv * y, axis=0, keepdims=True)

                @pl.when(steps[0] == 0)
                def _():
                    dg_ref[...] = part

                @pl.when(steps[0] > 0)
                def _():
                    dg_ref[...] += part

        _with_cargo(cargo, refs, 5, 3, steps, grid, compute)

    b_spec = pl.BlockSpec((k, tn), lambda i, j: (0, j)) if mode == 'nn' else pl.BlockSpec((tn, k), lambda i, j: (j, 0))
    row = pl.BlockSpec((tm, d), lambda i, j: (i, 0))
    vec = pl.BlockSpec((1, d), lambda i, j: (0, 0))
    extra = cargo.specs() if cargo else []
    outs = pl.pallas_call(
        body, name=name, grid=grid,
        in_specs=[pl.BlockSpec((tm, k), lambda i, j: (i, 0)), b_spec, row, vec, row] + extra,
        out_specs=[row, row, vec] + extra,
        out_shape=[jax.ShapeDtypeStruct((m, d), F32), jax.ShapeDtypeStruct((m, d), BF16),
                   jax.ShapeDtypeStruct((1, d), F32)] + (cargo.out_shape() if cargo else []),
        scratch_shapes=[pltpu.VMEM((tm, d), F32)] + (cargo.scratch() if cargo else []),
        compiler_params=_params(("arbitrary", "arbitrary")),
    )(a, b, x, g, dh_in, *(cargo.srcs if cargo else []))
    return (outs[0], outs[1], outs[2], list(outs[3:])) if cargo else outs


def _sigmoid(x):
    return 1.0 / (1.0 + jnp.exp(-x))


def _swiglu_fwd(ab, name):
    t = ab.shape[0]
    tr = _row_tile(t)

    def body(a_ref, b_ref, o_ref):
        a = a_ref[...].astype(F32)
        o_ref[...] = (a * _sigmoid(a) * b_ref[...].astype(F32)).astype(BF16)

    return pl.pallas_call(
        body, name=name, grid=(t // tr,),
        in_specs=[pl.BlockSpec((tr, D_FF), lambda i: (i, 0)), pl.BlockSpec((tr, D_FF), lambda i: (i, 1))],
        out_specs=pl.BlockSpec((tr, D_FF), lambda i: (i, 0)),
        out_shape=jax.ShapeDtypeStruct((t, D_FF), BF16),
        compiler_params=_params(("parallel",)),
    )(ab, ab)


def _swiglu_bwd(ab, dact, name):
    t = ab.shape[0]
    tr = _row_tile(t, 256)

    def body(ab_ref, d_ref, o_ref):
        a = ab_ref[:, :D_FF].astype(F32)
        b = ab_ref[:, D_FF:].astype(F32)
        dv = d_ref[...].astype(F32)
        s = _sigmoid(a)
        o_ref[:, :D_FF] = (dv * b * s * (1.0 + a * (1.0 - s))).astype(BF16)
        o_ref[:, D_FF:] = (dv * a * s).astype(BF16)

    return pl.pallas_call(
        body, name=name, grid=(t // tr,),
        in_specs=[pl.BlockSpec((tr, 2 * D_FF), lambda i: (i, 0)), pl.BlockSpec((tr, D_FF), lambda i: (i, 0))],
        out_specs=pl.BlockSpec((tr, 2 * D_FF), lambda i: (i, 0)),
        out_shape=jax.ShapeDtypeStruct((t, 2 * D_FF), BF16),
        compiler_params=_params(("parallel",)),
    )(ab, dact)


def _latent_fwd(proj, gq, gkv):
    t = proj.shape[0]
    tr = _row_tile(t)

    def body(p_ref, gq_ref, gkv_ref, cq_ref, ckv_ref):
        cq = p_ref[:, P_CQ:P_CQ + Q_LORA].astype(F32)
        ckv = p_ref[:, P_CKV:P_CKV + KV_LORA].astype(F32)
        cq_ref[...] = ((cq * _rms(cq, Q_LORA)) * gq_ref[...]).astype(BF16)
        ckv_ref[...] = ((ckv * _rms(ckv, KV_LORA)) * gkv_ref[...]).astype(BF16)

    return pl.pallas_call(
        body, name="latent_fwd", grid=(t // tr,),
        in_specs=[pl.BlockSpec((tr, 1024), lambda i: (i, 0)), pl.BlockSpec((1, Q_LORA), lambda i: (0, 0)),
                  pl.BlockSpec((1, KV_LORA), lambda i: (0, 0))],
        out_specs=[pl.BlockSpec((tr, Q_LORA), lambda i: (i, 0)), pl.BlockSpec((tr, KV_LORA), lambda i: (i, 0))],
        out_shape=[jax.ShapeDtypeStruct((t, Q_LORA), BF16), jax.ShapeDtypeStruct((t, KV_LORA), BF16)],
        compiler_params=_params(("parallel",)),
    )(proj, gq, gkv)


def _latent_bwd(dcqn, dckvn, proj, dkr, gq, gkv):
    t = proj.shape[0]
    tr = _row_tile(t, 256)

    def norm_bwd(dn, x, g, width):
        r = _rms(x, width)
        y = x * r
        dy = dn * g
        dx = r * (dy - y * (jnp.sum(dy * y, axis=-1, keepdims=True) * (1.0 / width)))
        return dx, jnp.sum(dn * y, axis=0, keepdims=True)

    def body(dcq_ref, dckv_ref, p_ref, dkr_ref, gq_ref, gkv_ref, o_ref, dgq_ref, dgkv_ref):
        i = pl.program_id(0)
        dcq, pq = norm_bwd(dcq_ref[...], p_ref[:, P_CQ:P_CQ + Q_LORA].astype(F32), gq_ref[...], Q_LORA)
        dckv, pkv = norm_bwd(dckv_ref[...], p_ref[:, P_CKV:P_CKV + KV_LORA].astype(F32), gkv_ref[...], KV_LORA)
        o_ref[:, P_CQ:P_CQ + Q_LORA] = dcq.astype(BF16)
        o_ref[:, P_CKV:P_CKV + KV_LORA] = dckv.astype(BF16)
        o_ref[:, P_KR:P_KR + 128] = dkr_ref[...].astype(BF16)
        o_ref[:, P_KR + 128:1024] = jnp.zeros((tr, 1024 - P_KR - 128), BF16)

        @pl.when(i == 0)
        def _():
            dgq_ref[...] = pq
            dgkv_ref[...] = pkv

        @pl.when(i > 0)
        def _():
            dgq_ref[...] += pq
            dgkv_ref[...] += pkv

    def row(w):
        return pl.BlockSpec((tr, w), lambda i: (i, 0))

    def vec(w):
        return pl.BlockSpec((1, w), lambda i: (0, 0))

    return pl.pallas_call(
        body, name="latent_bwd", grid=(t // tr,),
        in_specs=[row(Q_LORA), row(KV_LORA), row(1024), row(128), vec(Q_LORA), vec(KV_LORA)],
        out_specs=[row(1024), vec(Q_LORA), vec(KV_LORA)],
        out_shape=[jax.ShapeDtypeStruct((t, 1024), BF16), jax.ShapeDtypeStruct((1, Q_LORA), F32),
                   jax.ShapeDtypeStruct((1, KV_LORA), F32)],
        compiler_params=_params(("arbitrary",)),
    )(dcqn, dckvn, proj, dkr, gq, gkv)


def _rope(y, cosf, sin_a, sin_b):
    return y * cosf + pltpu.roll(y, 112, 1) * sin_a + pltpu.roll(y, 16, 1) * sin_b


def _rope_t(d, cosf, sin_a, sin_b):
    return d * cosf + pltpu.roll(d * sin_a, 16, 1) + pltpu.roll(d * sin_b, 112, 1)


def _headprep_fwd(qraw, kvraw, proj, cosf, sin_a, sin_b, gqh, gkh):
    t = qraw.shape[0]
    tr = _row_tile(t, 256)

    def body(q_ref, kv_ref, kr_ref, c_ref, sa_ref, sb_ref, gq_ref, gk_ref, qh_ref, kh_ref, kvb_ref):
        cv, sa, sb = c_ref[...], sa_ref[...], sb_ref[...]
        kr = kr_ref[...].astype(F32)
        lane = lax.broadcasted_iota(jnp.int32, (tr, HEAD_PAD), 1)
        for h in range(N_HEADS):
            cols = slice(h * HEAD_PAD, (h + 1) * HEAD_PAD)
            xq = q_ref[:, cols]
            yq = (xq * _rms(xq, MLA_QK)) * gq_ref[...]
            qh_ref[:, cols] = _rope(yq, cv, sa, sb).astype(BF16)
            kvh = kv_ref[:, cols]
            kvb_ref[:, cols] = jnp.where(lane < MLA_NOPE, 1.0, kvh).astype(BF16)
            xk = jnp.where(lane < MLA_NOPE, kvh, kr)
            yk = (xk * _rms(xk, MLA_QK)) * gk_ref[...]
            kh_ref[:, cols] = _rope(yk, cv, sa, sb).astype(BF16)

    wide = pl.BlockSpec((tr, 1024), lambda i: (i, 0))
    lanes = pl.BlockSpec((tr, HEAD_PAD), lambda i: (i, 0))
    vec = pl.BlockSpec((1, HEAD_PAD), lambda i: (0, 0))
    return pl.pallas_call(
        body, name="headprep_fwd", grid=(t // tr,),
        in_specs=[wide, wide, pl.BlockSpec((tr, HEAD_PAD), lambda i: (i, P_KR // HEAD_PAD)), lanes, lanes, lanes, vec, vec],
        out_specs=[wide, wide, wide],
        out_shape=[jax.ShapeDtypeStruct((t, 1024), BF16)] * 3,
        compiler_params=_params(("parallel",)),
    )(qraw, kvraw, proj, cosf, sin_a, sin_b, gqh, gkh)


def _headprep_bwd(dqh, dkh, dvp, qraw, kvraw, proj, cosf, sin_a, sin_b, gqh, gkh):
    t = qraw.shape[0]
    tr = _row_tile(t, 256)

    def norm_bwd(dn, x, g):
        r = _rms(x, MLA_QK)
        y = x * r
        dy = dn * g
        dx = r * (dy - y * (jnp.sum(dy * y, axis=-1, keepdims=True) * (1.0 / MLA_QK)))
        return dx, jnp.sum(dn * y, axis=0, keepdims=True)

    def body(dq_ref, dk_ref, dv_ref, q_ref, kv_ref, kr_ref, c_ref, sa_ref, sb_ref, gq_ref, gk_ref,
             dqr_ref, dkvr_ref, dkr_ref, dgq_ref, dgk_ref):
        i = pl.program_id(0)
        cv, sa, sb = c_ref[...], sa_ref[...], sb_ref[...]
        kr = kr_ref[...].astype(F32)
        lane = lax.broadcasted_iota(jnp.int32, (tr, HEAD_PAD), 1)
        dkr = jnp.zeros((tr, HEAD_PAD), F32)
        pq = jnp.zeros((1, HEAD_PAD), F32)
        pk = jnp.zeros((1, HEAD_PAD), F32)
        for h in range(N_HEADS):
            cols = slice(h * HEAD_PAD, (h + 1) * HEAD_PAD)
            dxq, pqh = norm_bwd(_rope_t(dq_ref[:, cols], cv, sa, sb), q_ref[:, cols], gq_ref[...])
            dqr_ref[:, cols] = dxq.astype(BF16)
            pq = pq + pqh
            kvh = kv_ref[:, cols]
            xk = jnp.where(lane < MLA_NOPE, kvh, kr)
            dxk, pkh = norm_bwd(_rope_t(dk_ref[:, cols], cv, sa, sb), xk, gk_ref[...])
            pk = pk + pkh
            dkvr_ref[:, cols] = jnp.where(lane < MLA_NOPE, dxk, dv_ref[:, cols]).astype(BF16)
            dkr = dkr + jnp.where(lane < MLA_NOPE, 0.0, dxk)
        dkr_ref[...] = dkr

        @pl.when(i == 0)
        def _():
            dgq_ref[...] = pq
            dgk_ref[...] = pk

        @pl.when(i > 0)
        def _():
            dgq_ref[...] += pq
            dgk_ref[...] += pk

    wide = pl.BlockSpec((tr, 1024), lambda i: (i, 0))
    lanes = pl.BlockSpec((tr, HEAD_PAD), lambda i: (i, 0))
    vec = pl.BlockSpec((1, HEAD_PAD), lambda i: (0, 0))
    return pl.pallas_call(
        body, name="headprep_bwd", grid=(t // tr,),
        in_specs=[wide, wide, wide, wide, wide, pl.BlockSpec((tr, HEAD_PAD), lambda i: (i, P_KR // HEAD_PAD)),
                  lanes, lanes, lanes, vec, vec],
        out_specs=[wide, wide, lanes, vec, vec],
        out_shape=[jax.ShapeDtypeStruct((t, 1024), BF16), jax.ShapeDtypeStruct((t, 1024), BF16),
                   jax.ShapeDtypeStruct((t, HEAD_PAD), F32), jax.ShapeDtypeStruct((1, HEAD_PAD), F32),
                   jax.ShapeDtypeStruct((1, HEAD_PAD), F32)],
        compiler_params=_params(("arbitrary",)),
    )(dqh, dkh, dvp, qraw, kvraw, proj, cosf, sin_a, sin_b, gqh, gkh)


def _merge_fwd(proj, bm, bs):
    t = proj.shape[0]
    tr = _row_tile(t, 256)

    def body(gm_ref, gs_ref, bm_ref, bs_ref, o_ref):
        gm = _sigmoid(gm_ref[...].astype(F32))
        gs = _sigmoid(gs_ref[...].astype(F32))
        o_ref[...] = (gm * bm_ref[...] + gs * bs_ref[...]).astype(BF16)

    row = pl.BlockSpec((tr, 1024), lambda i: (i, 0))
    return pl.pallas_call(
        body, name="merge_fwd", grid=(t // tr,),
        in_specs=[pl.BlockSpec((tr, 1024), lambda i: (i, P_GM // 1024)),
                  pl.BlockSpec((tr, 1024), lambda i: (i, P_GS // 1024)), row, row],
        out_specs=row, out_shape=jax.ShapeDtypeStruct((t, 1024), BF16),
        compiler_params=_params(("parallel",)),
    )(proj, proj, bm, bs)


def _merge_bwd(dmerged, proj, bm, bs):
    t = proj.shape[0]
    tr = _row_tile(t, 256)

    def body(dm_ref, gm_ref, gs_ref, bm_ref, bs_ref, dbm_ref, dbs_ref, dg_ref):
        dm = dm_ref[...]
        gm = _sigmoid(gm_ref[...].astype(F32))
        gs = _sigmoid(gs_ref[...].astype(F32))
        dbm_ref[...] = (dm * gm).astype(BF16)
        dbs_ref[...] = (dm * gs).astype(BF16)
        dg_ref[:, :1024] = (dm * bm_ref[...] * gm * (1.0 - gm)).astype(BF16)
        dg_ref[:, 1024:] = (dm * bs_ref[...] * gs * (1.0 - gs)).astype(BF16)

    row = pl.BlockSpec((tr, 1024), lambda i: (i, 0))
    return pl.pallas_call(
        body, name="merge_bwd", grid=(t // tr,),
        in_specs=[row, pl.BlockSpec((tr, 1024), lambda i: (i, P_GM // 1024)),
                  pl.BlockSpec((tr, 1024), lambda i: (i, P_GS // 1024)), row, row],
        out_specs=[row, row, pl.BlockSpec((tr, 2048), lambda i: (i, 0))],
        out_shape=[jax.ShapeDtypeStruct((t, 1024), BF16), jax.ShapeDtypeStruct((t, 1024), BF16),
                   jax.ShapeDtypeStruct((t, 2048), BF16)],
        compiler_params=_params(("parallel",)),
    )(dmerged, proj, proj, bm, bs)


def _ple_loss(h3, zg, pp, tgt):
    t = h3.shape[0]
    tr = _row_tile(t, 256)

    def body(h_ref, z_ref, p_ref, t_ref, dh_ref, dz_ref, dp_ref, l_ref):
        i = pl.program_id(0)
        pg = _sigmoid(z_ref[...])
        ppv = p_ref[...]
        diff = (h_ref[...] + pg * ppv) - t_ref[...]
        dh = diff * (1.0 / D_MODEL)
        dh_ref[...] = dh
        dp_ref[...] = (dh * pg).astype(BF16)
        dz_ref[...] = (dh * ppv * pg * (1.0 - pg)).astype(BF16)
        sq = jnp.sum(diff * diff, axis=0, keepdims=True)
        part = sq[:, 0:128]
        for c in range(1, D_MODEL // 128):
            part = part + sq[:, c * 128:(c + 1) * 128]

        @pl.when(i == 0)
        def _():
            l_ref[...] = part

        @pl.when(i > 0)
        def _():
            l_ref[...] += part

    row = pl.BlockSpec((tr, 1024), lambda i: (i, 0))
    return pl.pallas_call(
        body, name="ple_loss", grid=(t // tr,),
        in_specs=[row, row, row, row],
        out_specs=[row, row, row, pl.BlockSpec((1, 128), lambda i: (0, 0))],
        out_shape=[jax.ShapeDtypeStruct((t, 1024), F32), jax.ShapeDtypeStruct((t, 1024), BF16),
                   jax.ShapeDtypeStruct((t, 1024), BF16), jax.ShapeDtypeStruct((1, 128), F32)],
        compiler_params=_params(("arbitrary",)),
    )(h3, zg, pp, tgt)


ATT_BLOCK = 256
MLA_FWD_COLS = 4
MLA_BWD_COLS = 4
SB_FWD_COLS = 4
SB_BWD_COLS = 2
SB_BLOCK = 256


def _split_bf16(x):
    hi = x.astype(BF16)
    return hi, (x - hi.astype(F32)).astype(BF16)


def _tri(kind, n):
    r = lax.broadcasted_iota(jnp.int32, (n, n), 0)
    c = lax.broadcasted_iota(jnp.int32, (n, n), 1)
    cond = {'gt': r > c, 'le': r <= c, 'lt': r < c}[kind]
    return jnp.where(cond, 1.0, 0.0).astype(BF16)


def _causal(strict, n=ATT_BLOCK):
    r = lax.broadcasted_iota(jnp.int32, (n, n), 0)
    c = lax.broadcasted_iota(jnp.int32, (n, n), 1)
    return (c < r) if strict else (c <= r)


def _lanes(c):
    return slice(c * HEAD_PAD, (c + 1) * HEAD_PAD)


def _row_block(j, n=ATT_BLOCK):
    return pl.ds(pl.multiple_of(j * n, n), n)


def _rows(ref, j, c, n=ATT_BLOCK):
    return ref[_row_block(j, n), _lanes(c)]


def _mla_fwd(qh, kh, kvb, cargo=None):
    t = qh.shape[0]
    bq = ATT_BLOCK
    scale = 1.0 / math.sqrt(MLA_QK)
    ncol = MLA_FWD_COLS
    grid = (N_HEADS // ncol, t // bq)

    def body(*refs):
        steps = [pl.program_id(0), pl.program_id(1)]
        _with_cargo(cargo, refs, 3, 2, steps, grid, lambda own: work(steps[1], *own))

    def work(i, q_ref, k_ref, v_ref, o_ref, lse_ref):
        qs = [q_ref[:, _lanes(c)] for c in range(ncol)]

        def step(j, carry, masked):
            cols = range(ncol)
            scores = [_dot(qs[c], _rows(k_ref, j, c), NT_DIMS) for c in cols]
            ms, ps, alphas = [], [], []
            for c in cols:
                s = scores[c] * scale
                if masked:
                    s = jnp.where(_causal(False), s, -1e30)
                m_new = jnp.maximum(carry[c][0], jnp.max(s, axis=-1, keepdims=True))
                ps.append(jnp.exp(s - m_new).astype(BF16))
                alphas.append(jnp.exp(carry[c][0] - m_new))
                ms.append(m_new)
            return tuple((ms[c], alphas[c] * carry[c][1] + _dot(ps[c], _rows(v_ref, j, c), NN_DIMS)) for c in cols)

        init = tuple((jnp.full((bq, 1), -1e30, F32), jnp.zeros((bq, HEAD_PAD), F32)) for _ in range(ncol))
        carry = lax.fori_loop(0, i, lambda j, cr: step(j, cr, False), init)
        for c, (m, acc) in enumerate(step(i, carry, True)):
            l = acc[:, 0:1]
            o_ref[:, _lanes(c)] = (acc / l).astype(BF16)
            lse_ref[c] = m + jnp.log(l)

    width = ncol * HEAD_PAD
    full = pl.BlockSpec((t, width), lambda h, i: (0, h))
    blk = pl.BlockSpec((bq, width), lambda h, i: (i, h))
    extra = cargo.specs() if cargo else []
    outs = pl.pallas_call(
        body, name="mla_fwd", grid=grid,
        in_specs=[blk, full, full] + extra,
        out_specs=[blk, pl.BlockSpec((ncol, bq, 1), lambda h, i: (h, i, 0))] + extra,
        out_shape=[jax.ShapeDtypeStruct((t, N_HEADS * HEAD_PAD), BF16), jax.ShapeDtypeStruct((N_HEADS, t, 1), F32)]
        + (cargo.out_shape() if cargo else []),
        scratch_shapes=cargo.scratch() if cargo else [],
        compiler_params=_params(("arbitrary", "arbitrary")),
    )(qh, kh, kvb, *(cargo.srcs if cargo else []))
    return (outs[0], outs[1], list(outs[2:])) if cargo else outs


def _mla_bwd(qh, kh, kvb, o, do, lse, cargo=None):
    t = qh.shape[0]
    bq = ATT_BLOCK
    scale = 1.0 / math.sqrt(MLA_QK)
    ncol = MLA_BWD_COLS
    width = ncol * HEAD_PAD
    grid = (N_HEADS // ncol, t // bq)

    def body(*refs):
        steps = [pl.program_id(0), pl.program_id(1)]
        _with_cargo(cargo, refs, 6, 3, steps, grid, lambda own: work(steps[0], steps[1], *own))

    def work(h, i, q_ref, k_ref, v_ref, o_ref, do_ref, lse_ref, dq_ref, dk_hbm, dv_hbm, dk_ref, dv_ref, out_sems):

        @pl.when(i == 0)
        def _():
            dk_ref[...] = jnp.zeros_like(dk_ref)
            dv_ref[...] = jnp.zeros_like(dv_ref)

        qs = [q_ref[:, _lanes(c)] for c in range(ncol)]
        dos = [do_ref[:, _lanes(c)] for c in range(ncol)]
        deltas = [jnp.sum(dos[c].astype(F32) * o_ref[:, _lanes(c)].astype(F32), axis=-1, keepdims=True)
                  for c in range(ncol)]
        lses = [lse_ref[c] for c in range(ncol)]

        def step(j, dqs, masked):
            cols = range(ncol)
            kbs = [_rows(k_ref, j, c) for c in cols]
            scores = [_dot(qs[c], kbs[c], NT_DIMS) for c in cols]
            dps = [_dot(dos[c], _rows(v_ref, j, c), NT_DIMS) for c in cols]
            pbs, dss = [], []
            for c in cols:
                p = jnp.exp(scores[c] * scale - lses[c])
                if masked:
                    p = jnp.where(_causal(False), p, 0.0)
                pbs.append(p.astype(BF16))
                dss.append((p * (dps[c] - deltas[c]) * scale).astype(BF16))
            for c in cols:
                dv_ref[_row_block(j), _lanes(c)] += _dot(pbs[c], dos[c], TN_DIMS)
                dk_ref[_row_block(j), _lanes(c)] += _dot(dss[c], qs[c], TN_DIMS)
            return tuple(dqs[c] + _dot(dss[c], kbs[c], NN_DIMS) for c in cols)

        init = tuple(jnp.zeros((bq, HEAD_PAD), F32) for _ in range(ncol))
        dqs = lax.fori_loop(0, i, lambda j, cr: step(j, cr, False), init)
        for c, dq in enumerate(step(i, dqs, True)):
            dq_ref[:, _lanes(c)] = dq

        @pl.when(i == grid[1] - 1)
        def _():
            cols = pl.ds(pl.multiple_of(h * width, width), width)
            out = [pltpu.make_async_copy(dk_ref, dk_hbm.at[:, cols], out_sems.at[0]),
                   pltpu.make_async_copy(dv_ref, dv_hbm.at[:, cols], out_sems.at[1])]
            for cp in out:
                cp.start()
            for cp in out:
                cp.wait()

    full = pl.BlockSpec((t, width), lambda h, i: (0, h))
    blk = pl.BlockSpec((bq, width), lambda h, i: (i, h))
    wide = jax.ShapeDtypeStruct((t, N_HEADS * HEAD_PAD), F32)
    extra = cargo.specs() if cargo else []
    outs = pl.pallas_call(
        body, name="mla_bwd", grid=grid,
        in_specs=[blk, full, full, blk, blk, pl.BlockSpec((ncol, bq, 1), lambda h, i: (h, i, 0))] + extra,
        out_specs=[blk, HBM_SPEC, HBM_SPEC] + extra,
        out_shape=[wide, wide, wide] + (cargo.out_shape() if cargo else []),
        scratch_shapes=[pltpu.VMEM((t, width), F32), pltpu.VMEM((t, width), F32), pltpu.SemaphoreType.DMA((2,))]
        + (cargo.scratch() if cargo else []),
        compiler_params=_params(("arbitrary", "arbitrary")),
    )(qh, kh, kvb, o, do, lse, *(cargo.srcs if cargo else []))
    return (outs[0], outs[1], outs[2], list(outs[3:])) if cargo else outs


def _head_only(x, lane, u):
    return jnp.where((lane >= u * SB_DIM) & (lane < (u + 1) * SB_DIM), x, jnp.zeros_like(x))


SB_DEAD = -104.0


def _log_sigmoids(z):
    e = jnp.exp(-jnp.abs(z))
    lg = jnp.log(1.0 + e)
    ls_pos = jnp.minimum(z, 0.0) - lg
    return ls_pos, ls_pos - z, e


def _sb_fwd(proj):
    t = proj.shape[0]
    bq, ncol = SB_BLOCK, SB_FWD_COLS
    nq = t // bq
    scale = 1.0 / math.sqrt(SB_DIM)
    pairs = SB_WIDTH // HEAD_PAD

    def body(q_ref, k_ref, v_ref, o_ref, r_ref, first_ref):
        g, i = pl.program_id(0), pl.program_id(1)
        lane = lax.broadcasted_iota(jnp.int32, (bq, HEAD_PAD), 1)
        upper = _tri('gt', bq)
        chains = [(c, u) for c in range(ncol) for u in range(2)]
        qms = [_head_only(q_ref[:, _lanes(c)], lane, u) * scale for c, u in chains]

        def step(j, carry, masked):
            ids = range(len(chains))
            zs = [_dot(qms[n], _rows(k_ref, j, chains[n][0], bq), NT_DIMS) for n in ids]
            pos, neg, parts = [], [], []
            for n in ids:
                ls_pos, ls_neg, _ = _log_sigmoids(zs[n])
                if masked:
                    ls_neg = jnp.where(_causal(True, bq), ls_neg, 0.0)
                pos.append(ls_pos)
                neg.append(ls_neg)
                parts.append(_split_bf16(ls_neg))
            suffix = [_dot(parts[n][0], upper, NN_DIMS) + _dot(parts[n][1], upper, NN_DIMS) for n in ids]
            weights = []
            for n in ids:
                a = jnp.exp(pos[n] + suffix[n] + carry[n][0])
                if masked:
                    a = jnp.where(_causal(True, bq), a, 0.0)
                weights.append(a.astype(BF16))
            return tuple((carry[n][0] + jnp.sum(neg[n], axis=-1, keepdims=True),
                          carry[n][1] + _dot(weights[n], _rows(v_ref, j, chains[n][0], bq), NN_DIMS)) for n in ids)

        init = tuple((jnp.zeros((bq, 1), F32), jnp.zeros((bq, HEAD_PAD), F32)) for _ in chains)
        carry = step(i, init, True)

        def more(state):
            s, cr = state
            live = cr[0][0]
            for n in range(1, len(chains)):
                live = jnp.maximum(live, cr[n][0])
            return jnp.logical_and(s < i, jnp.max(live) > SB_DEAD)

        walked, carry = lax.while_loop(more, lambda st: (st[0] + 1, step(i - 1 - st[0], st[1], False)),
                                       (jnp.int32(0), carry))
        first_ref[g * nq + i] = i - walked
        for n, (c, u) in enumerate(chains):
            r_ref[2 * c + u] = carry[n][0]
        for c in range(ncol):
            o_ref[:, _lanes(c)] = jnp.where(lane < SB_DIM, carry[2 * c][1], carry[2 * c + 1][1]).astype(BF16)

    width = ncol * HEAD_PAD

    def full(c0):
        return pl.BlockSpec((t, width), lambda g, i: (0, c0 // width + g))

    return pl.pallas_call(
        body, name="sb_fwd", grid=(pairs // ncol, t // bq),
        in_specs=[pl.BlockSpec((bq, width), lambda g, i: (i, P_SBQ // width + g)), full(P_SBK), full(P_SBV)],
        out_specs=[pl.BlockSpec((bq, width), lambda g, i: (i, g)),
                   pl.BlockSpec((2 * ncol, bq, 1), lambda g, i: (g, i, 0)),
                   pl.BlockSpec(memory_space=pltpu.SMEM)],
        out_shape=[jax.ShapeDtypeStruct((t, SB_WIDTH), BF16), jax.ShapeDtypeStruct((N_HEADS, t, 1), F32),
                   jax.ShapeDtypeStruct((pairs // ncol * nq,), jnp.int32)],
        compiler_params=_params(("arbitrary", "arbitrary")),
    )(proj, proj, proj)


def _sb_bwd(proj, do, rtot, first):
    t = proj.shape[0]
    bq, ncol = SB_BLOCK, SB_BWD_COLS
    nq = t // bq
    scale = 1.0 / math.sqrt(SB_DIM)
    pairs = SB_WIDTH // HEAD_PAD

    def body(first_ref, q_ref, k_ref, v_ref, do_ref, r_ref, dq_ref, dk_ref, dv_ref):
        g, i = pl.program_id(0), pl.program_id(1)

        @pl.when(i == 0)
        def _():
            dk_ref[...] = jnp.zeros_like(dk_ref)
            dv_ref[...] = jnp.zeros_like(dv_ref)

        lane = lax.broadcasted_iota(jnp.int32, (bq, HEAD_PAD), 1)
        incl = _tri('le', bq)
        excl = _tri('lt', bq)
        chains = [(c, u) for c in range(ncol) for u in range(2)]
        qms = [_head_only(q_ref[:, _lanes(c)], lane, u) * scale for c, u in chains]
        doms = [_head_only(do_ref[:, _lanes(c)], lane, u) for c, u in chains]
        rts = [r_ref[2 * c + u] for c, u in chains]

        def step(j, carry, masked):
            ids = range(len(chains))
            kbs = [_rows(k_ref, j, c, bq) for c in range(ncol)]
            zs =[_dot(qms[n], kbs[chains[n][0]], NT_DIMS) for n in ids]
            das = [_dot(doms[n], _rows(v_ref, j, chains[n][0], bq), NT_DIMS) for n in ids]
            pos, neg, sigs, parts = [], [], [], []
            for n in ids:
                ls_pos, ls_neg, e = _log_sigmoids(zs[n])
                if masked:
                    ls_neg = jnp.where(_causal(True, bq), ls_neg, 0.0)
                pos.append(ls_pos)
                neg.append(ls_neg)
                sigs.append(jnp.where(zs[n] >= 0.0, 1.0, e) * pl.reciprocal(1.0 + e, approx=True))
                parts.append(_split_bf16(ls_neg))
            prefix = [_dot(parts[n][0], incl, NN_DIMS) + _dot(parts[n][1], incl, NN_DIMS) for n in ids]
            evs, eparts, dvs = [], [], []
            for n in ids:
                a = jnp.exp(pos[n] + (rts[n] - (carry[n][0] + prefix[n])))
                if masked:
                    a = jnp.where(_causal(True, bq), a, 0.0)
                dvs.append(_dot(a.astype(BF16), doms[n], TN_DIMS))
                evs.append(a * das[n])
                eparts.append(evs[n].astype(BF16))
            before = [_dot(eparts[n], excl, NN_DIMS) for n in ids]
            out, dks = [], []
            for n in ids:
                dz = evs[n] - sigs[n] * (evs[n] + (carry[n][1] + before[n]))
                if masked:
                    dz = jnp.where(_causal(True, bq), dz, 0.0)
                dzb = dz.astype(BF16)
                dks.append(_dot(dzb, qms[n], TN_DIMS))
                out.append((carry[n][0] + jnp.sum(neg[n], axis=-1, keepdims=True),
                            carry[n][1] + jnp.sum(evs[n], axis=-1, keepdims=True),
                            carry[n][2] + _dot(dzb, kbs[chains[n][0]], NN_DIMS)))
            for c in range(ncol):
                dv_ref[_row_block(j, bq), _lanes(c)] += dvs[2 * c] + dvs[2 * c + 1]
                dk_ref[_row_block(j, bq), _lanes(c)] += dks[2 * c] + dks[2 * c + 1]
            return tuple(out)

        init = tuple((jnp.zeros((bq, 1), F32), jnp.zeros((bq, 1), F32), jnp.zeros((bq, HEAD_PAD), F32)) for _ in chains)
        start = first_ref[(g * ncol // SB_FWD_COLS) * nq + i]
        carry = lax.fori_loop(start, i, lambda j, cr: step(j, cr, False), init)
        carry = step(i, carry, True)
        for c in range(ncol):
            dq_ref[:, _lanes(c)] = jnp.where(lane < SB_DIM, carry[2 * c][2], carry[2 * c + 1][2]) * scale

    width = ncol * HEAD_PAD

    def full(c0):
        return pl.BlockSpec((t, width), lambda g, i, first: (0, c0 // width + g))

    blk = pl.BlockSpec((bq, width), lambda g, i, first: (i, g))
    acc = pl.BlockSpec((t, width), lambda g, i, first: (0, g))
    wide = jax.ShapeDtypeStruct((t, SB_WIDTH), F32)
    return pl.pallas_call(
        body, name="sb_bwd",
        grid_spec=pltpu.PrefetchScalarGridSpec(
            num_scalar_prefetch=1, grid=(pairs // ncol, nq),
            in_specs=[pl.BlockSpec((bq, width), lambda g, i, first: (i, P_SBQ // width + g)), full(P_SBK), full(P_SBV),
                      blk, pl.BlockSpec((2 * ncol, bq, 1), lambda g, i, first: (g, i, 0))],
            out_specs=[blk, acc, acc]),
        out_shape=[wide, wide, wide],
        compiler_params=_params(("arbitrary", "arbitrary")),
    )(first, proj, proj, proj, do, rtot)


def _cols_to_full(g):
    n, r, c = g.shape
    return jnp.transpose(g, (1, 0, 2)).reshape(r, n * c)


def _full_to_cols(w):
    r, c = w.shape
    return jnp.transpose(w.reshape(r, N_DEV, c // N_DEV), (1, 0, 2))


TRANSPOSED = ('ffn1_w_in', 'ffn2_w_in', 'w_in', 'w_q_up')


def _layout_weight(name, g):
    if name in ('ffn1_w_out', 'ffn2_w_out', 'w_out', 'w_ple_gate', 'ffn1_w_in', 'ffn2_w_in'):
        return g.reshape(g.shape[0] * g.shape[1], g.shape[2])
    if name == 'w_in':
        wt = g.reshape(IN_COLS, D_MODEL)
        z = lambda n: jnp.zeros((n, D_MODEL), BF16)
        return jnp.concatenate([wt[0:640], z(64), wt[640:672], z(32), z(256), wt[2208:4256], wt[672:2208]], axis=0)
    if name == 'w_q_up':
        return jnp.pad(g, ((0, 0), (0, HEAD_PAD - MLA_QK), (0, 0))).reshape(N_HEADS * HEAD_PAD, Q_LORA)
    if name == 'w_branch_mla':
        bm = _cols_to_full(g).reshape(N_HEADS, MLA_NOPE, D_MODEL)
        return jnp.pad(bm, ((0, 0), (HEAD_PAD - MLA_NOPE, 0), (0, 0))).reshape(N_HEADS * HEAD_PAD, D_MODEL)
    return _cols_to_full(g)


def _layout_weights(g):
    return {n: _layout_weight(n, a) for n, a in g.items()}


def _unlayout_grad(name, d):
    if name == 'w_in':
        d = jnp.concatenate([d[0:640], d[704:736], d[P_SBQ:PROJ_W], d[P_GM:P_SBQ]], axis=0)
    if name == 'w_q_up':
        return d.reshape(N_HEADS, HEAD_PAD, Q_LORA)[:, :MLA_QK, :]
    if name in ('ffn1_w_out', 'ffn2_w_out', 'w_out', 'w_ple_gate', 'ffn1_w_in', 'ffn2_w_in', 'w_in'):
        return d.reshape(N_DEV, d.shape[0] // N_DEV, d.shape[1])
    if name == 'w_branch_mla':
        d = d.reshape(N_HEADS, HEAD_PAD, D_MODEL)[:, HEAD_PAD - MLA_NOPE:, :].reshape(SB_WIDTH, D_MODEL)
    return _full_to_cols(d)


def _unlayout_grads(d):
    return {n: _unlayout_grad(n, a) for n, a in d.items()}


def _rope_tables(positions):
    half = MLA_ROPE // 2
    inv_freq = ROPE_BASE ** (-jnp.arange(0, MLA_ROPE, 2, dtype=F32) / MLA_ROPE)
    ang = positions.astype(F32)[:, None] * inv_freq
    cos, sin = jnp.cos(ang), jnp.sin(ang)
    t = positions.shape[0]
    ones = lambda n: jnp.ones((t, n), F32)
    zeros = lambda n: jnp.zeros((t, n), F32)
    cosf = jnp.concatenate([ones(MLA_NOPE), cos, cos, ones(HEAD_PAD - MLA_QK)], axis=1)
    sin_a = jnp.concatenate([zeros(MLA_NOPE), -sin, zeros(half), zeros(HEAD_PAD - MLA_QK)], axis=1)
    sin_b = jnp.concatenate([zeros(MLA_NOPE), zeros(half), sin, zeros(HEAD_PAD - MLA_QK)], axis=1)
    return cosf, sin_a, sin_b


def _local_step(x, p, positions, tgt, norms, plan):
    mm = _matmul
    cosf, sin_a, sin_b = _rope_tables(positions)
    pad_head = lambda g: jnp.pad(g, ((0, 0), (0, HEAD_PAD - MLA_QK)))
    gqh, gkh = pad_head(norms['q_head_norm']), pad_head(norms['k_head_norm'])
    pb = p.astype(BF16)
    w = dict(plan.first_weights())
    dw, dn = {}, {}

    def ride(host, call):
        cargo = plan.cargo(host, dw)
        res, lands = call(cargo), None
        if cargo is not None:
            *res, lands = res
            res = res[0] if len(res) == 1 else tuple(res)
        w.update(plan.landed(host, lands))
        return res

    def ffn_fwd(h, tag):
        n = _rmsnorm_fwd(h, norms[tag + '_norm'], tag + "_norm_fwd")
        ab = ride(tag + "_in_fwd", lambda cargo: mm(
            n, w[tag + '_w_in'], mode='nt', out_dtype=BF16, name=tag + "_in_fwd", cargo=cargo))
        act = _swiglu_fwd(ab, tag + "_swiglu_fwd")
        out = ride(tag + "_out_fwd", lambda cargo: mm(
            act, w[tag + '_w_out'], mode='nn', out_dtype=F32, name=tag + "_out_fwd", res=h, alpha=0.5, cargo=cargo))
        return out, (n, ab, act)

    h1, ffn1_saved = ffn_fwd(x, 'ffn1')
    u = _rmsnorm_fwd(h1, norms['mix_norm'], "mix_norm_fwd")
    proj = mm(u, w['w_in'], mode='nt', out_dtype=BF16, name="proj_fwd")
    cqn, ckvn = _latent_fwd(proj, norms['q_latent_norm'], norms['kv_latent_norm'])
    qraw = mm(cqn, w['w_q_up'], mode='nt', out_dtype=F32, name="q_up_fwd")
    kvraw = mm(ckvn, w['w_kv_up'], mode='nn', out_dtype=F32, name="kv_up_fwd")
    qh, kh, kvb = _headprep_fwd(qraw, kvraw, proj, cosf, sin_a, sin_b, gqh, gkh)
    o_mla, lse = ride("mla_fwd", lambda cargo: _mla_fwd(qh, kh, kvb, cargo))
    o_sb, rtot, sb_first = _sb_fwd(proj)
    bm = mm(o_mla, w['w_branch_mla'], mode='nn', out_dtype=F32, name="branch_mla_fwd")
    bs = mm(o_sb, w['w_branch_sb'], mode='nn', out_dtype=F32, name="branch_sb_fwd")
    merged = _merge_fwd(proj, bm, bs)
    h2 = mm(merged, w['w_out'], mode='nn', out_dtype=F32, name="mix_out_fwd", res=h1)
    h3, ffn2_saved = ffn_fwd(h2, 'ffn2')
    n3 = _rmsnorm_fwd(h3, norms['ple_norm'], "ple_norm_fwd")
    zg = mm(n3, w['w_ple_gate'], mode='nn', out_dtype=F32, name="ple_gate_fwd")
    pp = mm(pb, w['w_ple_proj'], mode='nn', out_dtype=F32, name="ple_proj_fwd")
    dh4, dzg, dpp, loss_lanes = _ple_loss(h3, zg, pp, tgt)

    dw['w_ple_gate'] = mm(n3, dzg, mode='tn', out_dtype=BF16, name="ple_gate_dw")
    dw['w_ple_proj'] = mm(pb, dpp, mode='tn', out_dtype=BF16, name="ple_proj_dw")
    dh3, dhb3, dn['ple_norm'] = _matmul_norm_bwd(
        dzg, w['w_ple_gate'], h3, norms['ple_norm'], dh4, mode='nt', name="ple_gate_dx", out_scale=0.5)

    def ffn_bwd(h, dh, dhb, saved, tag, out_scale):
        n, ab, act = saved
        dw[tag + '_w_out'] = mm(act, dhb, mode='tn', out_dtype=BF16, name=tag + "_out_dw", tm=1408)
        dact = mm(dhb, w[tag + '_w_out'], mode='nt', out_dtype=BF16, name=tag + "_out_dx", tn=1408)
        dab = _swiglu_bwd(ab, dact, tag + "_swiglu_bwd")
        dw[tag + '_w_in'] = ride(tag + "_in_dw", lambda cargo: mm(
            dab, n, mode='tn', out_dtype=BF16, name=tag + "_in_dw", tm=1408, cargo=cargo))
        dh_prev, dhb_prev, dn[tag + '_norm'] = ride(tag + "_in_dx", lambda cargo: _matmul_norm_bwd(
            dab, w[tag + '_w_in'], h, norms[tag + '_norm'], dh, mode='nn', name=tag + "_in_dx", out_scale=out_scale,
            cargo=cargo))
        return dh_prev, dhb_prev

    dh2, dhb2 = ffn_bwd(h2, dh3, dhb3, ffn2_saved, 'ffn2', 1.0)
    dw['w_out'] = mm(merged, dhb2, mode='tn', out_dtype=BF16, name="mix_out_dw")
    dmerged = mm(dhb2, w['w_out'], mode='nt', out_dtype=F32, name="mix_out_dx")
    dbm, dbs, dgates = _merge_bwd(dmerged, proj, bm, bs)
    dw['w_branch_mla'] = mm(o_mla, dbm, mode='tn', out_dtype=BF16, name="branch_mla_dw")
    dw['w_branch_sb'] = mm(o_sb, dbs, mode='tn', out_dtype=BF16, name="branch_sb_dw")
    do_mla = mm(dbm, w['w_branch_mla'], mode='nt', out_dtype=BF16, name="branch_mla_dx")
    do_sb = mm(dbs, w['w_branch_sb'], mode='nt', out_dtype=BF16, name="branch_sb_dx")
    dqh, dkh, dvp = ride("mla_bwd", lambda cargo: _mla_bwd(qh, kh, kvb, o_mla, do_mla, lse, cargo))
    dsq, dsk, dsv = _sb_bwd(proj, do_sb, rtot, sb_first)
    dqraw, dkvraw, dkr, dgq, dgk = _headprep_bwd(dqh, dkh, dvp, qraw, kvraw, proj, cosf, sin_a, sin_b, gqh, gkh)
    dn['q_head_norm'], dn['k_head_norm'] = dgq[:, :MLA_QK], dgk[:, :MLA_QK]
    dw['w_q_up'] = mm(dqraw, cqn, mode='tn', out_dtype=BF16, name="q_up_dw")
    dw['w_kv_up'] = mm(ckvn, dkvraw, mode='tn', out_dtype=BF16, name="kv_up_dw")
    dcqn = mm(dqraw, w['w_q_up'], mode='nn', out_dtype=F32, name="q_up_dx")
    dckvn = mm(dkvraw, w['w_kv_up'], mode='nt', out_dtype=F32, name="kv_up_dx")
    dlat, dn['q_latent_norm'], dn['kv_latent_norm'] = _latent_bwd(
        dcqn, dckvn, proj, dkr, norms['q_latent_norm'], norms['kv_latent_norm'])
    dproj = jnp.concatenate([dlat, dgates, dsq.astype(BF16), dsk.astype(BF16), dsv.astype(BF16)], axis=1)
    dw['w_in'] = ride("proj_dw", lambda cargo: mm(dproj, u, mode='tn', out_dtype=BF16, name="proj_dw", tm=1536, cargo=cargo))
    dh1, dhb1, dn['mix_norm'] = ride("proj_dx", lambda cargo: _matmul_norm_bwd(
        dproj, w['w_in'], h1, norms['mix_norm'], dh2, mode='nn', name="proj_dx", out_scale=0.5, cargo=cargo))
    dx, _ = ffn_bwd(x, dh1, dhb1, ffn1_saved, 'ffn1', 1.0)
    return dx, loss_lanes, dw, dn


MESH = pl.DeviceIdType.MESH
HBM_SPEC = pl.BlockSpec(memory_space=pl.ANY)


def _position():
    return lax.axis_index("x"), lax.axis_index("y"), lax.axis_index("c")


def _index(px, py, pc):
    return 4 * px + 2 * py + pc


def _all_gather(shards):
    n = len(shards)

    def body(*refs):
        ins, outs = refs[:n], refs[n:2 * n]
        send_sems, recv_sems, local_sems = refs[2 * n:]
        x, y, c = _position()
        me, sibling = (x, y, c), (x, y, 1 - c)
        chips = [(1 - x, y), (x, 1 - y), (1 - x, 1 - y)]

        def copy(a, k, block, to, own=False):
            dst = outs[a].at[_index(*block)]
            return pltpu.make_async_remote_copy(
                src_ref=ins[a] if own else dst, dst_ref=dst,
                send_sem=send_sems.at[a, k], recv_sem=recv_sems.at[a, k], device_id=to, device_id_type=MESH)

        mine = [pltpu.make_async_copy(ins[a], outs[a].at[_index(*me)], local_sems.at[a]) for a in range(n)]
        for cp in mine:
            cp.start()
        first = []
        for a in range(n):
            first.append(copy(a, 0, me, sibling, own=True))
            first += [copy(a, 1 + j, me, (*chip, c), own=True) for j, chip in enumerate(chips)]
        for cp in first:
            cp.start()
        passed = []
        for j, chip in enumerate(chips):
            for a in range(n):
                copy(a, 1 + j, (*chip, c), me).wait_recv()
                fwd = copy(a, 4 + j, (*chip, c), sibling)
                fwd.start()
                passed.append(fwd)
        for a in range(n):
            copy(a, 0, sibling, me).wait_recv()
            for j, chip in enumerate(chips):
                copy(a, 4 + j, (*chip, 1 - c), me).wait_recv()
        for cp in first + passed:
            cp.wait_send()
        for cp in mine:
            cp.wait()

    return pl.pallas_call(
        body, name="weights_all_gather",
        in_specs=[HBM_SPEC] * n, out_specs=[HBM_SPEC] * n,
        out_shape=[jax.ShapeDtypeStruct((N_DEV,) + s.shape, s.dtype) for s in shards],
        scratch_shapes=[pltpu.SemaphoreType.DMA((n, 7)), pltpu.SemaphoreType.DMA((n, 7)), pltpu.SemaphoreType.DMA((n,))],
    )(*shards)


def _exchange(parts):
    n = len(parts)
    masks = [(mx, my, mc) for mx in (0, 1) for my in (0, 1) for mc in (0, 1)][1:]

    def body(*refs):
        ins, outs = refs[:n], refs[n:2 * n]
        send_sems, recv_sems, local_sems = refs[2 * n:]
        x, y, c = _position()
        me = _index(x, y, c)

        def peer_of(mask):
            mx, my, mc = mask
            return (x + mx - 2 * x * mx, y + my - 2 * y * my, c + mc - 2 * c * mc)

        def copy(a, k):
            peer = peer_of(masks[k])
            return pltpu.make_async_remote_copy(
                src_ref=ins[a].at[_index(*peer)], dst_ref=outs[a].at[me],
                send_sem=send_sems.at[a, k], recv_sem=recv_sems.at[a, k], device_id=peer, device_id_type=MESH)

        def landed(a, k):
            peer = peer_of(masks[k])
            return pltpu.make_async_remote_copy(
                src_ref=ins[a].at[me], dst_ref=outs[a].at[_index(*peer)],
                send_sem=send_sems.at[a, k], recv_sem=recv_sems.at[a, k], device_id=peer, device_id_type=MESH)

        mine = [pltpu.make_async_copy(ins[a].at[me], outs[a].at[me], local_sems.at[a]) for a in range(n)]
        for cp in mine:
            cp.start()
        sent = [copy(a, k) for k in range(7) for a in range(n)]
        for cp in sent:
            cp.start()
        for k in range(7):
            for a in range(n):
                landed(a, k).wait_recv()
        for cp in sent:
            cp.wait_send()
        for cp in mine:
            cp.wait()

    return pl.pallas_call(
        body, name="grads_exchange",
        in_specs=[HBM_SPEC] * n, out_specs=[HBM_SPEC] * n,
        out_shape=[jax.ShapeDtypeStruct(s.shape, s.dtype) for s in parts],
        scratch_shapes=[pltpu.SemaphoreType.DMA((n, 7)), pltpu.SemaphoreType.DMA((n, 7)), pltpu.SemaphoreType.DMA((n,))],
    )(*parts)


PEER_MASKS = [(mx, my, mc) for mx in (0, 1) for my in (0, 1) for mc in (0, 1)][1:]


def _peer(mask):
    x, y, c = _position()
    mx, my, mc = mask
    return (x + mx - 2 * x * mx, y + my - 2 * y * my, c + mc - 2 * c * mc)


class _Cargo:
    def __init__(self, srcs, scatter):
        self.srcs, self.scatter, self.n = list(srcs), scatter, len(srcs)

    def specs(self):
        return [HBM_SPEC] * self.n

    def out_shape(self):
        return [jax.ShapeDtypeStruct(s.shape if self.scatter else (N_DEV,) + s.shape, s.dtype) for s in self.srcs]

    def scratch(self):
        per_copy = pltpu.SemaphoreType.DMA((self.n, len(PEER_MASKS)))
        return [per_copy, per_copy, pltpu.SemaphoreType.DMA((self.n,))]

    def _mine(self, src_refs, a, to):
        return src_refs[a].at[to] if self.scatter else src_refs[a]

    def _shard_copy(self, src_refs, land_refs, sems, a, k, block, to, own=False):
        dst = land_refs[a].at[_index(*block)]
        return pltpu.make_async_remote_copy(
            src_ref=src_refs[a] if own else dst, dst_ref=dst,
            send_sem=sems[0].at[a, k], recv_sem=sems[1].at[a, k], device_id=to, device_id_type=MESH)

    def _first_hops(self, src_refs, land_refs, sems):
        x, y, c = _position()
        chips = [(1 - x, y), (x, 1 - y), (1 - x, 1 - y)]
        hops = []
        for a in range(self.n):
            hops.append(self._shard_copy(src_refs, land_refs, sems, a, 0, (x, y, c), (x, y, 1 - c), own=True))
            hops += [self._shard_copy(src_refs, land_refs, sems, a, 1 + j, (x, y, c), (*chip, c), own=True)
                     for j, chip in enumerate(chips)]
        return hops, chips

    def start(self, src_refs, land_refs, sems):
        send, recv, local = sems
        me = _index(*_position())
        for a in range(self.n):
            pltpu.make_async_copy(self._mine(src_refs, a, me), land_refs[a].at[me], local.at[a]).start()
        if not self.scatter:
            for cp in self._first_hops(src_refs, land_refs, sems)[0]:
                cp.start()
            return
        for k, mask in enumerate(PEER_MASKS):
            peer = _peer(mask)
            for a in range(self.n):
                pltpu.make_async_remote_copy(
                    src_ref=self._mine(src_refs, a, _index(*peer)), dst_ref=land_refs[a].at[me],
                    send_sem=send.at[a, k], recv_sem=recv.at[a, k], device_id=peer, device_id_type=MESH).start()

    def _wait_gathered(self, src_refs, land_refs, sems):
        x, y, c = _position()
        me, sibling = (x, y, c), (x, y, 1 - c)
        first, chips = self._first_hops(src_refs, land_refs, sems)
        passed = []
        for j, chip in enumerate(chips):
            for a in range(self.n):
                self._shard_copy(src_refs, land_refs, sems, a, 1 + j, (*chip, c), me).wait_recv()
                passed.append(self._shard_copy(src_refs, land_refs, sems, a, 4 + j, (*chip, c), sibling))
                passed[-1].start()
        for a in range(self.n):
            self._shard_copy(src_refs, land_refs, sems, a, 0, sibling, me).wait_recv()
            for j, chip in enumerate(chips):
                self._shard_copy(src_refs, land_refs, sems, a, 4 + j, (*chip, 1 - c), me).wait_recv()
        for cp in first + passed:
            cp.wait_send()

    def wait(self, src_refs, land_refs, sems):
        send, recv, local = sems
        me = _index(*_position())
        if not self.scatter:
            self._wait_gathered(src_refs, land_refs, sems)
        for k, mask in enumerate(PEER_MASKS if self.scatter else []):
            peer = _peer(mask)
            there = _index(*peer)
            for a in range(self.n):
                pltpu.make_async_remote_copy(
                    src_ref=self._mine(src_refs, a, me), dst_ref=land_refs[a].at[there],
                    send_sem=send.at[a, k], recv_sem=recv.at[a, k], device_id=peer, device_id_type=MESH).wait_recv()
                pltpu.make_async_remote_copy(
                    src_ref=self._mine(src_refs, a, there), dst_ref=land_refs[a].at[me],
                    send_sem=send.at[a, k], recv_sem=recv.at[a, k], device_id=peer, device_id_type=MESH).wait_send()
        for a in range(self.n):
            pltpu.make_async_copy(self._mine(src_refs, a, me), land_refs[a].at[me], local.at[a]).wait()


def _with_cargo(cargo, refs, n_in, n_out, steps, counts, compute):
    if cargo is None:
        compute(refs)
        return
    n = cargo.n
    src_refs = refs[n_in:n_in + n]
    land_refs = refs[n_in + n + n_out:n_in + 2 * n + n_out]
    sems = refs[-3:]
    first = functools.reduce(jnp.logical_and, [s == 0 for s in steps])
    last = functools.reduce(jnp.logical_and, [s == c - 1 for s, c in zip(steps, counts)])

    @pl.when(first)
    def _():
        cargo.start(src_refs, land_refs, sems)

    compute(refs[:n_in] + refs[n_in + n:n_in + n + n_out] + refs[n_in + 2 * n + n_out:-3])

    @pl.when(last)
    def _():
        cargo.wait(src_refs, land_refs, sems)


def _adamw(parts, w, m, v, name):
    r, c = w.shape
    tr = next((t for t in (512, 384, 352, 256, 128) if r % t == 0), r) if r > 512 else r
    tc = c if tr < r or r <= 512 else 256
    assert r % tr == 0 and c % tc == 0
    bc1 = 1.0 - ADAM_B1 ** ADAM_STEP
    bc2 = 1.0 - ADAM_B2 ** ADAM_STEP

    def body(p_ref, w_ref, m_ref, v_ref, g_ref, d_ref, nm_ref, nv_ref):
        g = p_ref[0].astype(F32)
        for s in range(1, N_DEV):
            g = g + p_ref[s].astype(F32)
        nm = ADAM_B1 * m_ref[...] + (1.0 - ADAM_B1) * g
        nv = ADAM_B2 * v_ref[...] + (1.0 - ADAM_B2) * (g * g)
        g_ref[...] = g
        nm_ref[...] = nm
        nv_ref[...] = nv
        d_ref[...] = -ADAM_LR * ((nm / bc1) / (jnp.sqrt(nv / bc2) + ADAM_EPS) + ADAM_WD * w_ref[...])

    tile = pl.BlockSpec((tr, tc), lambda i, j: (i, j))
    out = jax.ShapeDtypeStruct((r, c), F32)
    return pl.pallas_call(
        body, name=name, grid=(r // tr, c // tc),
        in_specs=[pl.BlockSpec((N_DEV, tr, tc), lambda i, j: (0, i, j)), tile, tile, tile],
        out_specs=[tile] * 4, out_shape=[out] * 4,
        compiler_params=_params(("parallel", "parallel")),
    )(parts, w, m, v)


GATHER_FIRST = ['ffn1_w_in']
RIDES = {
    'ffn1_in_fwd': ('weights', ['ffn1_w_out', 'w_in']),
    'ffn1_out_fwd': ('weights', ['w_q_up', 'w_kv_up', 'w_branch_mla', 'w_branch_sb', 'w_out']),
    'mla_fwd': ('weights', ['ffn2_w_in', 'ffn2_w_out', 'w_ple_gate', 'w_ple_proj']),
    'mla_bwd': ('grads', ['w_ple_gate', 'w_ple_proj', 'ffn2_w_out', 'ffn2_w_in', 'w_out', 'w_branch_mla', 'w_branch_sb']),
    'proj_dw': ('grads', ['w_q_up', 'w_kv_up']),
    'proj_dx': ('grads', ['w_in']),
    'ffn1_in_dw': ('grads', ['ffn1_w_out']),
    'ffn1_in_dx': ('grads', ['ffn1_w_in']),
}


class _Plan:
    def __init__(self, shards):
        self.shards = shards
        self.received = {}

    def first_weights(self):
        gathered = _all_gather([self.shards[n] for n in GATHER_FIRST])
        return {n: _layout_weight(n, g) for n, g in zip(GATHER_FIRST, gathered)}

    def cargo(self, host, dw):
        if host not in RIDES:
            return None
        kind, names = RIDES[host]
        if kind == 'weights':
            return _Cargo([self.shards[n] for n in names], False)
        return _Cargo([_unlayout_grad(n, dw.pop(n)) for n in names], True)

    def landed(self, host, lands):
        if host not in RIDES:
            return {}
        kind, names = RIDES[host]
        if kind == 'weights':
            return {n: _layout_weight(n, land) for n, land in zip(names, lands)}
        self.received.update(zip(names, lands))
        return {}


def _pack_small(vecs):
    flat = jnp.concatenate([v.reshape(-1) for v in vecs])
    return jnp.pad(flat, (0, SMALL_ROWS * 128 - flat.shape[0])).reshape(SMALL_ROWS, 128)


def _unpack_small(packed, sizes):
    flat = packed.reshape(-1)
    out, at = [], 0
    for n in sizes:
        out.append(flat[at:at + n])
        at += n
    return out


def kernel(x, p, positions, ffn1_norm, ffn1_w_in, ffn1_w_out, mix_norm, w_in, q_latent_norm, w_q_up, kv_latent_norm, w_kv_up, q_head_norm, k_head_norm, w_branch_mla, w_branch_sb, w_out, ffn2_norm, ffn2_w_in, ffn2_w_out, ple_norm, w_ple_gate, w_ple_proj, loss_target, m_ffn1_norm, m_ffn1_w_in, m_ffn1_w_out, m_mix_norm, m_w_in, m_q_latent_norm, m_w_q_up, m_kv_latent_norm, m_w_kv_up, m_q_head_norm, m_k_head_norm, m_w_branch_mla, m_w_branch_sb, m_w_out, m_ffn2_norm, m_ffn2_w_in, m_ffn2_w_out, m_ple_norm, m_w_ple_gate, m_w_ple_proj, v_ffn1_norm, v_ffn1_w_in, v_ffn1_w_out, v_mix_norm, v_w_in, v_q_latent_norm, v_w_q_up, v_kv_latent_norm, v_w_kv_up, v_q_head_norm, v_k_head_norm, v_w_branch_mla, v_w_branch_sb, v_w_out, v_ffn2_norm, v_ffn2_w_in, v_ffn2_w_out, v_ple_norm, v_w_ple_gate, v_w_ple_proj):
    given = dict(locals())
    wts = {n: given[n] for n in WEIGHTS}
    mom = {n: given['m_' + n] for n in WEIGHTS}
    var = {n: given['v_' + n] for n in WEIGHTS}

    def local(a, n):
        return jnp.swapaxes(a[0], 0, 1) if n in TRANSPOSED else a[0]

    plan = _Plan({n: local(wts[n], n).astype(BF16) for n in MATS})
    norms = {n: wts[n] for n in NORMS}
    dx, loss_lanes, dw, dn = _local_step(x[0], p[0, 0], positions[0], loss_target[0], norms, plan)
    assert not dw

    out = {}
    for n in MATS:
        res = _adamw(plan.received[n], local(wts[n], n), local(mom[n], n), local(var[n], n), "adamw_" + n)
        out[n] = [local(r[None], n)[None] for r in res]
    small = _pack_small([dn[n] for n in NORMS] + [0.5 / D_MODEL * jnp.sum(loss_lanes)[None]])
    small_parts = _exchange([jnp.broadcast_to(small[None], (N_DEV, SMALL_ROWS, 128))])[0]
    sizes = [wts[n].shape[1] for n in NORMS]
    pack = lambda d: _pack_small([d[n] for n in NORMS])
    small_res = _adamw(small_parts, pack(wts), pack(mom), pack(var), "adamw_norms")
    loss = small_res[0].reshape(-1)[sum(sizes)]
    for i, res in enumerate(small_res):
        for n, vec in zip(NORMS, _unpack_small(res, sizes)):
            out.setdefault(n, [None] * 4)[i] = vec[None]

    return (loss, dx[None], *[out[n][0] for n in WEIGHTS], *[out[n][1] for n in WEIGHTS],
            *[out[n][2] for n in WEIGHTS], *[out[n][3] for n in WEIGHTS])
```

```python
import functools
import math

import jax
import jax.numpy as jnp
from jax import lax
from jax.experimental import pallas as pl
from jax.experimental.pallas import tpu as pltpu

F32 = jnp.float32
BF16 = jnp.bfloat16

N_DEV = 8
D_MODEL = 1024
D_FF = 2816
PLE_DIM = 256
NORM_EPS = 1e-6
N_HEADS = 8
HEAD_PAD = 128
MLA_NOPE = 64
MLA_ROPE = 32
MLA_QK = 96
Q_LORA = 384
KV_LORA = 256
SB_DIM = 64
SB_WIDTH = 512
ROPE_BASE = 10000.0
IN_COLS = 4256

PROJ_W = 4608
P_CQ, P_CKV, P_KR, P_GM, P_GS, P_SBQ, P_SBK, P_SBV = 0, 384, 640, 1024, 2048, 3072, 3584, 4096

ADAM_LR, ADAM_B1, ADAM_B2, ADAM_EPS, ADAM_WD, ADAM_STEP = 0.001, 0.9, 0.999, 1e-08, 0.01, 10

VMEM_LIMIT = 52 * 1024 * 1024
MATMUL_VMEM = 40 * 1024 * 1024

WEIGHTS = ['ffn1_norm', 'ffn1_w_in', 'ffn1_w_out', 'mix_norm', 'w_in', 'q_latent_norm', 'w_q_up',
           'kv_latent_norm', 'w_kv_up', 'q_head_norm', 'k_head_norm', 'w_branch_mla', 'w_branch_sb',
           'w_out', 'ffn2_norm', 'ffn2_w_in', 'ffn2_w_out', 'ple_norm', 'w_ple_gate', 'w_ple_proj']
NORMS = ['ffn1_norm', 'mix_norm', 'q_latent_norm', 'kv_latent_norm', 'q_head_norm', 'k_head_norm',
         'ffn2_norm', 'ple_norm']
MATS = [n for n in WEIGHTS if n not in NORMS]
SMALL_ROWS = 48

NT_DIMS = (((1,), (1,)), ((), ()))
NN_DIMS = (((1,), (0,)), ((), ()))
TN_DIMS = (((0,), (0,)), ((), ()))


def _params(sem=None, vmem=VMEM_LIMIT):
    return pltpu.CompilerParams(dimension_semantics=sem, vmem_limit_bytes=vmem)


def _pick(n, cap):
    if n <= cap:
        return n
    best = None
    for t in range(128, cap + 1, 128):
        if n % t == 0:
            best = t
    assert best is not None, (n, cap)
    return best


def _dot(a, b, dims):
    return lax.dot_general(a, b, dims, preferred_element_type=F32)


def _matmul(a, b, *, mode, out_dtype, name, tm=None, tn=None, tk=None, res=None, alpha=1.0, cargo=None):
    if mode == 'nn':
        (m, k), (k2, n) = a.shape, b.shape
    elif mode == 'nt':
        (m, k), (n, k2) = a.shape, b.shape
    else:
        (k, m), (k2, n) = a.shape, b.shape
    assert k == k2, (name, a.shape, b.shape)
    has_res = res is not None
    tn = tn or _pick(n, 512)

    def vmem(tm_, tk_):
        io = 2 * 2 * (tm_ * tk_ + tk_ * tn) + 2 * tm_ * tn * (jnp.dtype(out_dtype).itemsize + 4 * has_res)
        return io + (4 * tm_ * tn if tk_ < k else 0)

    tries = [(tm_, tk_) for tk_ in ([tk] if tk else [k, _pick(k, 2048)])
             for tm_ in ([tm] if tm else [_pick(m, 2048), _pick(m, 1024), _pick(m, 512)])]
    tm, tk = next((c for c in tries if vmem(*c) <= MATMUL_VMEM), tries[-1])
    assert m % tm == 0 and n % tn == 0 and k % tk == 0, (name, m, n, k, tm, tn, tk)
    nk = k // tk
    dims = {'nn': NN_DIMS, 'nt': NT_DIMS, 'tn': TN_DIMS}[mode]

    def epilogue(acc, r_ref, o_ref):
        if alpha != 1.0:
            acc = acc * alpha
        if has_res:
            acc = r_ref[...] + acc
        o_ref[...] = acc.astype(out_dtype)

    grid = (m // tm, n // tn, nk)

    def body(*refs):
        steps = [pl.program_id(d) for d in range(3)]

        def compute(own):
            a_ref, b_ref = own[0], own[1]
            r_ref = own[2] if has_res else None
            o_ref = own[2 + has_res]
            if nk == 1:
                epilogue(_dot(a_ref[...], b_ref[...], dims), r_ref, o_ref)
                return
            acc_ref = own[-1]

            @pl.when(steps[2] == 0)
            def _():
                acc_ref[...] = jnp.zeros_like(acc_ref)

            acc_ref[...] += _dot(a_ref[...], b_ref[...], dims)

            @pl.when(steps[2] == nk - 1)
            def _():
                epilogue(acc_ref[...], r_ref, o_ref)

        _with_cargo(cargo, refs, 2 + has_res, 1, steps, grid, compute)

    if mode == 'tn':
        a_spec = pl.BlockSpec((tk, tm), lambda i, j, kk: (kk, i))
    else:
        a_spec = pl.BlockSpec((tm, tk), lambda i, j, kk: (i, kk))
    if mode == 'nt':
        b_spec = pl.BlockSpec((tn, tk), lambda i, j, kk: (j, kk))
    else:
        b_spec = pl.BlockSpec((tk, tn), lambda i, j, kk: (kk, j))
    o_spec = pl.BlockSpec((tm, tn), lambda i, j, kk: (i, j))
    in_specs = [a_spec, b_spec] + ([o_spec] if has_res else [])
    args = (a, b) + ((res,) if has_res else ())
    out_shape = jax.ShapeDtypeStruct((m, n), out_dtype)
    scratch = [pltpu.VMEM((tm, tn), F32)] if nk > 1 else []
    if cargo is None:
        return pl.pallas_call(
            body, name=name, grid=grid, in_specs=in_specs, out_specs=o_spec, out_shape=out_shape,
            scratch_shapes=scratch, compiler_params=_params(("parallel", "parallel", "arbitrary")),
        )(*args)
    outs = pl.pallas_call(
        body, name=name, grid=grid, in_specs=in_specs + cargo.specs(), out_specs=[o_spec] + cargo.specs(),
        out_shape=[out_shape] + cargo.out_shape(), scratch_shapes=scratch + cargo.scratch(),
        compiler_params=_params(("arbitrary", "arbitrary", "arbitrary")),
    )(*args, *cargo.srcs)
    return outs[0], list(outs[1:])


def _row_tile(t, cap=512):
    return min(t, cap)


def _rms(x, width):
    return lax.rsqrt(jnp.sum(x * x, axis=-1, keepdims=True) * (1.0 / width) + NORM_EPS)


def _rmsnorm_fwd(x, g, name):
    t, d = x.shape
    tr = _row_tile(t)

    def body(x_ref, g_ref, o_ref):
        xv = x_ref[...]
        o_ref[...] = ((xv * _rms(xv, d)) * g_ref[...]).astype(BF16)

    return pl.pallas_call(
        body, name=name, grid=(t // tr,),
        in_specs=[pl.BlockSpec((tr, d), lambda i: (i, 0)), pl.BlockSpec((1, d), lambda i: (0, 0))],
        out_specs=pl.BlockSpec((tr, d), lambda i: (i, 0)),
        out_shape=jax.ShapeDtypeStruct((t, d), BF16),
        compiler_params=_params(("parallel",)),
    )(x, g)


def _matmul_norm_bwd(a, b, x, g, dh_in, *, mode, name, out_scale, cargo=None):
    m, k = a.shape
    d = x.shape[1]
    tn = _pick(d, 512)

    def vmem(tm_):
        return 2 * 2 * (tm_ * k + k * tn) + tm_ * d * (4 + 2 * (4 + 4) + 2 * (4 + 2))

    tm = next((c for c in (_pick(m, 1024), _pick(m, 512), _pick(m, 256)) if vmem(c) <= MATMUL_VMEM), _pick(m, 256))
    grid = (m // tm, d // tn)
    dims = {'nn': NN_DIMS, 'nt': NT_DIMS}[mode]

    def body(*refs):
        steps = [pl.program_id(0), pl.program_id(1)]

        def compute(own):
            a_ref, b_ref, x_ref, g_ref, dhin_ref, dh_ref, dhb_ref, dg_ref, dn_ref = own
            for jj in range(grid[1]):
                @pl.when(steps[1] == jj)
                def _(jj=jj):
                    dn_ref[:, jj * tn:(jj + 1) * tn] = _dot(a_ref[...], b_ref[...], dims)

            @pl.when(steps[1] == grid[1] - 1)
            def _():
                xv = x_ref[...]
                dnv = dn_ref[...]
                r = _rms(xv, d)
                y = xv * r
                dy = dnv * g_ref[...]
                dh = dhin_ref[...] + r * (dy - y * (jnp.sum(dy * y, axis=-1, keepdims=True) * (1.0 / d)))
                dh_ref[...] = dh
                dhb_ref[...] = (dh * out_scale).astype(BF16)
                part = jnp.sum(dnv * y, axis=0, keepdims=True)

                @pl.when(steps[0] == 0)
                def _():
                    dg_ref[...] = part

                @pl.when(steps[0] > 0)
                def _():
                    dg_ref[...] += part

        _with_cargo(cargo, refs, 5, 3, steps, grid, compute)

    b_spec = pl.BlockSpec((k, tn), lambda i, j: (0, j)) if mode == 'nn' else pl.BlockSpec((tn, k), lambda i, j: (j, 0))
    row = pl.BlockSpec((tm, d), lambda i, j: (i, 0))
    vec = pl.BlockSpec((1, d), lambda i, j: (0, 0))
    extra = cargo.specs() if cargo else []
    outs = pl.pallas_call(
        body, name=name, grid=grid,
        in_specs=[pl.BlockSpec((tm, k), lambda i, j: (i, 0)), b_spec, row, vec, row] + extra,
        out_specs=[row, row, vec] + extra,
        out_shape=[jax.ShapeDtypeStruct((m, d), F32), jax.ShapeDtypeStruct((m, d), BF16),
                   jax.ShapeDtypeStruct((1, d), F32)] + (cargo.out_shape() if cargo else []),
        scratch_shapes=[pltpu.VMEM((tm, d), F32)] + (cargo.scratch() if cargo else []),
        compiler_params=_params(("arbitrary", "arbitrary")),
    )(a, b, x, g, dh_in, *(cargo.srcs if cargo else []))
    return (outs[0], outs[1], outs[2], list(outs[3:])) if cargo else outs


def _sigmoid(x):
    return 1.0 / (1.0 + jnp.exp(-x))


SWIGLU_CHUNK = 256
SWIGLU_COLS = 1408


def _chunks(width):
    return [slice(lo, min(lo + SWIGLU_CHUNK, width)) for lo in range(0, width, SWIGLU_CHUNK)]


def _swiglu_fwd(n, wt_in, name, cargo=None):
    t = n.shape[0]
    tm = _pick(t, 1024)
    grid = (t // tm, D_FF // SWIGLU_COLS)

    def body(*refs):
        steps = [pl.program_id(0), pl.program_id(1)]

        def compute(own):
            n_ref, wa_ref, wb_ref, a_ref, b_ref, act_ref = own
            nv = n_ref[...]
            for cols in _chunks(SWIGLU_COLS):
                a = _dot(nv, wa_ref[cols, :], NT_DIMS)
                b = _dot(nv, wb_ref[cols, :], NT_DIMS)
                a_ref[:, cols] = a.astype(BF16)
                b_ref[:, cols] = b.astype(BF16)
                act_ref[:, cols] = (a * _sigmoid(a) * b).astype(BF16)

        _with_cargo(cargo, refs, 3, 3, steps, grid, compute)

    half = D_FF // SWIGLU_COLS
    tile = pl.BlockSpec((tm, SWIGLU_COLS), lambda i, j: (i, j))
    out = jax.ShapeDtypeStruct((t, D_FF), BF16)
    extra = cargo.specs() if cargo else []
    outs = pl.pallas_call(
        body, name=name, grid=grid,
        in_specs=[pl.BlockSpec((tm, D_MODEL), lambda i, j: (i, 0)),
                  pl.BlockSpec((SWIGLU_COLS, D_MODEL), lambda i, j: (j, 0)),
                  pl.BlockSpec((SWIGLU_COLS, D_MODEL), lambda i, j: (half + j, 0))] + extra,
        out_specs=[tile, tile, tile] + extra,
        out_shape=[out, out, out] + (cargo.out_shape() if cargo else []),
        scratch_shapes=cargo.scratch() if cargo else [],
        compiler_params=_params(("arbitrary", "arbitrary")),
    )(n, wt_in, wt_in, *(cargo.srcs if cargo else []))
    return (outs[0], outs[1], outs[2], list(outs[3:])) if cargo else outs


def _swiglu_bwd(dh, w_out, a, b, name):
    t = a.shape[0]
    tr = _row_tile(t, 512)

    def body(d_ref, w_ref, a_ref, b_ref, o_ref):
        dhv = d_ref[...]
        for cols in _chunks(D_FF):
            dv = _dot(dhv, w_ref[cols, :], NT_DIMS)
            av = a_ref[:, cols].astype(F32)
            s = _sigmoid(av)
            o_ref[:, cols] = (dv * b_ref[:, cols].astype(F32) * s * (1.0 + av * (1.0 - s))).astype(BF16)
            o_ref[:, slice(D_FF + cols.start, D_FF + cols.stop)] = (dv * av * s).astype(BF16)

    row = pl.BlockSpec((tr, D_FF), lambda i: (i, 0))
    return pl.pallas_call(
        body, name=name, grid=(t // tr,),
        in_specs=[pl.BlockSpec((tr, D_MODEL), lambda i: (i, 0)), pl.BlockSpec((D_FF, D_MODEL), lambda i: (0, 0)), row, row],
        out_specs=pl.BlockSpec((tr, 2 * D_FF), lambda i: (i, 0)),
        out_shape=jax.ShapeDtypeStruct((t, 2 * D_FF), BF16),
        compiler_params=_params(("parallel",)),
    )(dh, w_out, a, b)


def _latent_fwd(proj, gq, gkv):
    t = proj.shape[0]
    tr = _row_tile(t)

    def body(p_ref, gq_ref, gkv_ref, cq_ref, ckv_ref):
        cq = p_ref[:, P_CQ:P_CQ + Q_LORA].astype(F32)
        ckv = p_ref[:, P_CKV:P_CKV + KV_LORA].astype(F32)
        cq_ref[...] = ((cq * _rms(cq, Q_LORA)) * gq_ref[...]).astype(BF16)
        ckv_ref[...] = ((ckv * _rms(ckv, KV_LORA)) * gkv_ref[...]).astype(BF16)

    return pl.pallas_call(
        body, name="latent_fwd", grid=(t // tr,),
        in_specs=[pl.BlockSpec((tr, 1024), lambda i: (i, 0)), pl.BlockSpec((1, Q_LORA), lambda i: (0, 0)),
                  pl.BlockSpec((1, KV_LORA), lambda i: (0, 0))],
        out_specs=[pl.BlockSpec((tr, Q_LORA), lambda i: (i, 0)), pl.BlockSpec((tr, KV_LORA), lambda i: (i, 0))],
        out_shape=[jax.ShapeDtypeStruct((t, Q_LORA), BF16), jax.ShapeDtypeStruct((t, KV_LORA), BF16)],
        compiler_params=_params(("parallel",)),
    )(proj, gq, gkv)


def _latent_bwd(dcqn, dckvn, proj, dkr, gq, gkv):
    t = proj.shape[0]
    tr = _row_tile(t, 256)

    def norm_bwd(dn, x, g, width):
        r = _rms(x, width)
        y = x * r
        dy = dn * g
        dx = r * (dy - y * (jnp.sum(dy * y, axis=-1, keepdims=True) * (1.0 / width)))
        return dx, jnp.sum(dn * y, axis=0, keepdims=True)

    def body(dcq_ref, dckv_ref, p_ref, dkr_ref, gq_ref, gkv_ref, o_ref, dgq_ref, dgkv_ref):
        i = pl.program_id(0)
        dcq, pq = norm_bwd(dcq_ref[...], p_ref[:, P_CQ:P_CQ + Q_LORA].astype(F32), gq_ref[...], Q_LORA)
        dckv, pkv = norm_bwd(dckv_ref[...], p_ref[:, P_CKV:P_CKV + KV_LORA].astype(F32), gkv_ref[...], KV_LORA)
        o_ref[:, P_CQ:P_CQ + Q_LORA] = dcq.astype(BF16)
        o_ref[:, P_CKV:P_CKV + KV_LORA] = dckv.astype(BF16)
        o_ref[:, P_KR:P_KR + 128] = dkr_ref[...].astype(BF16)
        o_ref[:, P_KR + 128:1024] = jnp.zeros((tr, 1024 - P_KR - 128), BF16)

        @pl.when(i == 0)
        def _():
            dgq_ref[...] = pq
            dgkv_ref[...] = pkv

        @pl.when(i > 0)
        def _():
            dgq_ref[...] += pq
            dgkv_ref[...] += pkv

    def row(w):
        return pl.BlockSpec((tr, w), lambda i: (i, 0))

    def vec(w):
        return pl.BlockSpec((1, w), lambda i: (0, 0))

    return pl.pallas_call(
        body, name="latent_bwd", grid=(t // tr,),
        in_specs=[row(Q_LORA), row(KV_LORA), row(1024), row(128), vec(Q_LORA), vec(KV_LORA)],
        out_specs=[row(1024), vec(Q_LORA), vec(KV_LORA)],
        out_shape=[jax.ShapeDtypeStruct((t, 1024), BF16), jax.ShapeDtypeStruct((1, Q_LORA), F32),
                   jax.ShapeDtypeStruct((1, KV_LORA), F32)],
        compiler_params=_params(("arbitrary",)),
    )(dcqn, dckvn, proj, dkr, gq, gkv)


def _rope(y, cosf, sin_a, sin_b):
    return y * cosf + pltpu.roll(y, 112, 1) * sin_a + pltpu.roll(y, 16, 1) * sin_b


def _rope_t(d, cosf, sin_a, sin_b):
    return d * cosf + pltpu.roll(d * sin_a, 16, 1) + pltpu.roll(d * sin_b, 112, 1)


def _headprep_fwd(qraw, kvraw, proj, cosf, sin_a, sin_b, gqh, gkh):
    t = qraw.shape[0]
    tr = _row_tile(t, 256)

    def body(q_ref, kv_ref, kr_ref, c_ref, sa_ref, sb_ref, gq_ref, gk_ref, qh_ref, kh_ref, kvb_ref):
        cv, sa, sb = c_ref[...], sa_ref[...], sb_ref[...]
        kr = kr_ref[...].astype(F32)
        lane = lax.broadcasted_iota(jnp.int32, (tr, HEAD_PAD), 1)
        for h in range(N_HEADS):
            cols = slice(h * HEAD_PAD, (h + 1) * HEAD_PAD)
            xq = q_ref[:, cols]
            yq = (xq * _rms(xq, MLA_QK)) * gq_ref[...]
            qh_ref[:, cols] = _rope(yq, cv, sa, sb).astype(BF16)
            kvh = kv_ref[:, cols]
            kvb_ref[:, cols] = jnp.where(lane < MLA_NOPE, 1.0, kvh).astype(BF16)
            xk = jnp.where(lane < MLA_NOPE, kvh, kr)
            yk = (xk * _rms(xk, MLA_QK)) * gk_ref[...]
            kh_ref[:, cols] = _rope(yk, cv, sa, sb).astype(BF16)

    wide = pl.BlockSpec((tr, 1024), lambda i: (i, 0))
    lanes = pl.BlockSpec((tr, HEAD_PAD), lambda i: (i, 0))
    vec = pl.BlockSpec((1, HEAD_PAD), lambda i: (0, 0))
    return pl.pallas_call(
        body, name="headprep_fwd", grid=(t // tr,),
        in_specs=[wide, wide, pl.BlockSpec((tr, HEAD_PAD), lambda i: (i, P_KR // HEAD_PAD)), lanes, lanes, lanes, vec, vec],
        out_specs=[wide, wide, wide],
        out_shape=[jax.ShapeDtypeStruct((t, 1024), BF16)] * 3,
        compiler_params=_params(("parallel",)),
    )(qraw, kvraw, proj, cosf, sin_a, sin_b, gqh, gkh)


def _headprep_bwd(dqh, dkh, dvp, qraw, kvraw, proj, cosf, sin_a, sin_b, gqh, gkh):
    t = qraw.shape[0]
    tr = _row_tile(t, 256)

    def norm_bwd(dn, x, g):
        r = _rms(x, MLA_QK)
        y = x * r
        dy = dn * g
        dx = r * (dy - y * (jnp.sum(dy * y, axis=-1, keepdims=True) * (1.0 / MLA_QK)))
        return dx, jnp.sum(dn * y, axis=0, keepdims=True)

    def body(dq_ref, dk_ref, dv_ref, q_ref, kv_ref, kr_ref, c_ref, sa_ref, sb_ref, gq_ref, gk_ref,
             dqr_ref, dkvr_ref, dkr_ref, dgq_ref, dgk_ref):
        i = pl.program_id(0)
        cv, sa, sb = c_ref[...], sa_ref[...], sb_ref[...]
        kr = kr_ref[...].astype(F32)
        lane = lax.broadcasted_iota(jnp.int32, (tr, HEAD_PAD), 1)
        dkr = jnp.zeros((tr, HEAD_PAD), F32)
        pq = jnp.zeros((1, HEAD_PAD), F32)
        pk = jnp.zeros((1, HEAD_PAD), F32)
        for h in range(N_HEADS):
            cols = slice(h * HEAD_PAD, (h + 1) * HEAD_PAD)
            dxq, pqh = norm_bwd(_rope_t(dq_ref[:, cols], cv, sa, sb), q_ref[:, cols], gq_ref[...])
            dqr_ref[:, cols] = dxq.astype(BF16)
            pq = pq + pqh
            kvh = kv_ref[:, cols]
            xk = jnp.where(lane < MLA_NOPE, kvh, kr)
            dxk, pkh = norm_bwd(_rope_t(dk_ref[:, cols], cv, sa, sb), xk, gk_ref[...])
            pk = pk + pkh
            dkvr_ref[:, cols] = jnp.where(lane < MLA_NOPE, dxk, dv_ref[:, cols]).astype(BF16)
            dkr = dkr + jnp.where(lane < MLA_NOPE, 0.0, dxk)
        dkr_ref[...] = dkr

        @pl.when(i == 0)
        def _():
            dgq_ref[...] = pq
            dgk_ref[...] = pk

        @pl.when(i > 0)
        def _():
            dgq_ref[...] += pq
            dgk_ref[...] += pk

    wide = pl.BlockSpec((tr, 1024), lambda i: (i, 0))
    lanes = pl.BlockSpec((tr, HEAD_PAD), lambda i: (i, 0))
    vec = pl.BlockSpec((1, HEAD_PAD), lambda i: (0, 0))
    return pl.pallas_call(
        body, name="headprep_bwd", grid=(t // tr,),
        in_specs=[wide, wide, wide, wide, wide, pl.BlockSpec((tr, HEAD_PAD), lambda i: (i, P_KR // HEAD_PAD)),
                  lanes, lanes, lanes, vec, vec],
        out_specs=[wide, wide, lanes, vec, vec],
        out_shape=[jax.ShapeDtypeStruct((t, 1024), BF16), jax.ShapeDtypeStruct((t, 1024), BF16),
                   jax.ShapeDtypeStruct((t, HEAD_PAD), F32), jax.ShapeDtypeStruct((1, HEAD_PAD), F32),
                   jax.ShapeDtypeStruct((1, HEAD_PAD), F32)],
        compiler_params=_params(("arbitrary",)),
    )(dqh, dkh, dvp, qraw, kvraw, proj, cosf, sin_a, sin_b, gqh, gkh)


def _merge_fwd(proj, bm, bs):
    t = proj.shape[0]
    tr = _row_tile(t, 256)

    def body(gm_ref, gs_ref, bm_ref, bs_ref, o_ref):
        gm = _sigmoid(gm_ref[...].astype(F32))
        gs = _sigmoid(gs_ref[...].astype(F32))
        o_ref[...] = (gm * bm_ref[...] + gs * bs_ref[...]).astype(BF16)

    row = pl.BlockSpec((tr, 1024), lambda i: (i, 0))
    return pl.pallas_call(
        body, name="merge_fwd", grid=(t // tr,),
        in_specs=[pl.BlockSpec((tr, 1024), lambda i: (i, P_GM // 1024)),
                  pl.BlockSpec((tr, 1024), lambda i: (i, P_GS // 1024)), row, row],
        out_specs=row, out_shape=jax.ShapeDtypeStruct((t, 1024), BF16),
        compiler_params=_params(("parallel",)),
    )(proj, proj, bm, bs)


def _merge_bwd(dmerged, proj, bm, bs):
    t = proj.shape[0]
    tr = _row_tile(t, 256)

    def body(dm_ref, gm_ref, gs_ref, bm_ref, bs_ref, dbm_ref, dbs_ref, dg_ref):
        dm = dm_ref[...]
        gm = _sigmoid(gm_ref[...].astype(F32))
        gs = _sigmoid(gs_ref[...].astype(F32))
        dbm_ref[...] = (dm * gm).astype(BF16)
        dbs_ref[...] = (dm * gs).astype(BF16)
        dg_ref[:, :1024] = (dm * bm_ref[...] * gm * (1.0 - gm)).astype(BF16)
        dg_ref[:, 1024:] = (dm * bs_ref[...] * gs * (1.0 - gs)).astype(BF16)

    row = pl.BlockSpec((tr, 1024), lambda i: (i, 0))
    return pl.pallas_call(
        body, name="merge_bwd", grid=(t // tr,),
        in_specs=[row, pl.BlockSpec((tr, 1024), lambda i: (i, P_GM // 1024)),
                  pl.BlockSpec((tr, 1024), lambda i: (i, P_GS // 1024)), row, row],
        out_specs=[row, row, pl.BlockSpec((tr, 2048), lambda i: (i, 0))],
        out_shape=[jax.ShapeDtypeStruct((t, 1024), BF16), jax.ShapeDtypeStruct((t, 1024), BF16),
                   jax.ShapeDtypeStruct((t, 2048), BF16)],
        compiler_params=_params(("parallel",)),
    )(dmerged, proj, proj, bm, bs)


def _ple_loss(h3, zg, pp, tgt):
    t = h3.shape[0]
    tr = _row_tile(t, 256)

    def body(h_ref, z_ref, p_ref, t_ref, dh_ref, dz_ref, dp_ref, l_ref):
        i = pl.program_id(0)
        pg = _sigmoid(z_ref[...])
        ppv = p_ref[...]
        diff = (h_ref[...] + pg * ppv) - t_ref[...]
        dh = diff * (1.0 / D_MODEL)
        dh_ref[...] = dh
        dp_ref[...] = (dh * pg).astype(BF16)
        dz_ref[...] = (dh * ppv * pg * (1.0 - pg)).astype(BF16)
        sq = jnp.sum(diff * diff, axis=0, keepdims=True)
        part = sq[:, 0:128]
        for c in range(1, D_MODEL // 128):
            part = part + sq[:, c * 128:(c + 1) * 128]

        @pl.when(i == 0)
        def _():
            l_ref[...] = part

        @pl.when(i > 0)
        def _():
            l_ref[...] += part

    row = pl.BlockSpec((tr, 1024), lambda i: (i, 0))
    return pl.pallas_call(
        body, name="ple_loss", grid=(t // tr,),
        in_specs=[row, row, row, row],
        out_specs=[row, row, row, pl.BlockSpec((1, 128), lambda i: (0, 0))],
        out_shape=[jax.ShapeDtypeStruct((t, 1024), F32), jax.ShapeDtypeStruct((t, 1024), BF16),
                   jax.ShapeDtypeStruct((t, 1024), BF16), jax.ShapeDtypeStruct((1, 128), F32)],
        compiler_params=_params(("arbitrary",)),
    )(h3, zg, pp, tgt)


ATT_BLOCK = 256
MLA_FWD_COLS = 4
MLA_BWD_COLS = 4
SB_FWD_COLS = 4
SB_BWD_COLS = 2
SB_BLOCK = 256


def _split_bf16(x):
    hi = x.astype(BF16)
    return hi, (x - hi.astype(F32)).astype(BF16)


def _tri(kind, n):
    r = lax.broadcasted_iota(jnp.int32, (n, n), 0)
    c = lax.broadcasted_iota(jnp.int32, (n, n), 1)
    cond = {'gt': r > c, 'le': r <= c, 'lt': r < c}[kind]
    return jnp.where(cond, 1.0, 0.0).astype(BF16)


def _causal(strict, n=ATT_BLOCK):
    r = lax.broadcasted_iota(jnp.int32, (n, n), 0)
    c = lax.broadcasted_iota(jnp.int32, (n, n), 1)
    return (c < r) if strict else (c <= r)


def _lanes(c):
    return slice(c * HEAD_PAD, (c + 1) * HEAD_PAD)


def _row_block(j, n=ATT_BLOCK):
    return pl.ds(pl.multiple_of(j * n, n), n)


def _rows(ref, j, c, n=ATT_BLOCK):
    return ref[_row_block(j, n), _lanes(c)]


def _mla_fwd(qh, kh, kvb, cargo=None):
    t = qh.shape[0]
    bq = ATT_BLOCK
    scale = 1.0 / math.sqrt(MLA_QK)
    ncol = MLA_FWD_COLS
    grid = (N_HEADS // ncol, t // bq)

    def body(*refs):
        steps = [pl.program_id(0), pl.program_id(1)]
        _with_cargo(cargo, refs, 3, 2, steps, grid, lambda own: work(steps[1], *own))

    def work(i, q_ref, k_ref, v_ref, o_ref, lse_ref):
        qs = [q_ref[:, _lanes(c)] for c in range(ncol)]

        def step(j, carry, masked):
            cols = range(ncol)
            scores = [_dot(qs[c], _rows(k_ref, j, c), NT_DIMS) for c in cols]
            ms, ps, alphas = [], [], []
            for c in cols:
                s = scores[c] * scale
                if masked:
                    s = jnp.where(_causal(False), s, -1e30)
                m_new = jnp.maximum(carry[c][0], jnp.max(s, axis=-1, keepdims=True))
                ps.append(jnp.exp(s - m_new).astype(BF16))
                alphas.append(jnp.exp(carry[c][0] - m_new))
                ms.append(m_new)
            return tuple((ms[c], alphas[c] * carry[c][1] + _dot(ps[c], _rows(v_ref, j, c), NN_DIMS)) for c in cols)

        init = tuple((jnp.full((bq, 1), -1e30, F32), jnp.zeros((bq, HEAD_PAD), F32)) for _ in range(ncol))
        carry = lax.fori_loop(0, i, lambda j, cr: step(j, cr, False), init)
        for c, (m, acc) in enumerate(step(i, carry, True)):
            l = acc[:, 0:1]
            o_ref[:, _lanes(c)] = (acc / l).astype(BF16)
            lse_ref[c] = m + jnp.log(l)

    width = ncol * HEAD_PAD
    full = pl.BlockSpec((t, width), lambda h, i: (0, h))
    blk = pl.BlockSpec((bq, width), lambda h, i: (i, h))
    extra = cargo.specs() if cargo else []
    outs = pl.pallas_call(
        body, name="mla_fwd", grid=grid,
        in_specs=[blk, full, full] + extra,
        out_specs=[blk, pl.BlockSpec((ncol, bq, 1), lambda h, i: (h, i, 0))] + extra,
        out_shape=[jax.ShapeDtypeStruct((t, N_HEADS * HEAD_PAD), BF16), jax.ShapeDtypeStruct((N_HEADS, t, 1), F32)]
        + (cargo.out_shape() if cargo else []),
        scratch_shapes=cargo.scratch() if cargo else [],
        compiler_params=_params(("arbitrary", "arbitrary")),
    )(qh, kh, kvb, *(cargo.srcs if cargo else []))
    return (outs[0], outs[1], list(outs[2:])) if cargo else outs


def _mla_bwd(qh, kh, kvb, o, do, lse, cargo=None):
    t = qh.shape[0]
    bq = ATT_BLOCK
    scale = 1.0 / math.sqrt(MLA_QK)
    ncol = MLA_BWD_COLS
    width = ncol * HEAD_PAD
    grid = (N_HEADS // ncol, t // bq)

    def body(*refs):
        steps = [pl.program_id(0), pl.program_id(1)]
        _with_cargo(cargo, refs, 6, 3, steps, grid, lambda own: work(steps[0], steps[1], *own))

    def work(h, i, q_ref, k_ref, v_ref, o_ref, do_ref, lse_ref, dq_ref, dk_hbm, dv_hbm, dk_ref, dv_ref, out_sems):

        @pl.when(i == 0)
        def _():
            dk_ref[...] = jnp.zeros_like(dk_ref)
            dv_ref[...] = jnp.zeros_like(dv_ref)

        qs = [q_ref[:, _lanes(c)] for c in range(ncol)]
        dos = [do_ref[:, _lanes(c)] for c in range(ncol)]
        deltas = [jnp.sum(dos[c].astype(F32) * o_ref[:, _lanes(c)].astype(F32), axis=-1, keepdims=True)
                  for c in range(ncol)]
        lses = [lse_ref[c] for c in range(ncol)]

        def step(j, dqs, masked):
            cols = range(ncol)
            kbs = [_rows(k_ref, j, c) for c in cols]
            scores = [_dot(qs[c], kbs[c], NT_DIMS) for c in cols]
            dps = [_dot(dos[c], _rows(v_ref, j, c), NT_DIMS) for c in cols]
            pbs, dss = [], []
            for c in cols:
                p = jnp.exp(scores[c] * scale - lses[c])
                if masked:
                    p = jnp.where(_causal(False), p, 0.0)
                pbs.append(p.astype(BF16))
                dss.append((p * (dps[c] - deltas[c]) * scale).astype(BF16))
            for c in cols:
                dv_ref[_row_block(j), _lanes(c)] += _dot(pbs[c], dos[c], TN_DIMS)
                dk_ref[_row_block(j), _lanes(c)] += _dot(dss[c], qs[c], TN_DIMS)
            return tuple(dqs[c] + _dot(dss[c], kbs[c], NN_DIMS) for c in cols)

        init = tuple(jnp.zeros((bq, HEAD_PAD), F32) for _ in range(ncol))
        dqs = lax.fori_loop(0, i, lambda j, cr: step(j, cr, False), init)
        for c, dq in enumerate(step(i, dqs, True)):
            dq_ref[:, _lanes(c)] = dq

        @pl.when(i == grid[1] - 1)
        def _():
            cols = pl.ds(pl.multiple_of(h * width, width), width)
            out = [pltpu.make_async_copy(dk_ref, dk_hbm.at[:, cols], out_sems.at[0]),
                   pltpu.make_async_copy(dv_ref, dv_hbm.at[:, cols], out_sems.at[1])]
            for cp in out:
                cp.start()
            for cp in out:
                cp.wait()

    full = pl.BlockSpec((t, width), lambda h, i: (0, h))
    blk = pl.BlockSpec((bq, width), lambda h, i: (i, h))
    wide = jax.ShapeDtypeStruct((t, N_HEADS * HEAD_PAD), F32)
    extra = cargo.specs() if cargo else []
    outs = pl.pallas_call(
        body, name="mla_bwd", grid=grid,
        in_specs=[blk, full, full, blk, blk, pl.BlockSpec((ncol, bq, 1), lambda h, i: (h, i, 0))] + extra,
        out_specs=[blk, HBM_SPEC, HBM_SPEC] + extra,
        out_shape=[wide, wide, wide] + (cargo.out_shape() if cargo else []),
        scratch_shapes=[pltpu.VMEM((t, width), F32), pltpu.VMEM((t, width), F32), pltpu.SemaphoreType.DMA((2,))]
        + (cargo.scratch() if cargo else []),
        compiler_params=_params(("arbitrary", "arbitrary")),
    )(qh, kh, kvb, o, do, lse, *(cargo.srcs if cargo else []))
    return (outs[0], outs[1], outs[2], list(outs[3:])) if cargo else outs


def _head_only(x, lane, u):
    return jnp.where((lane >= u * SB_DIM) & (lane < (u + 1) * SB_DIM), x, jnp.zeros_like(x))


SB_DEAD = -104.0


def _log_sigmoids(z):
    e = jnp.exp(-jnp.abs(z))
    lg = jnp.log(1.0 + e)
    ls_pos = jnp.minimum(z, 0.0) - lg
    return ls_pos, ls_pos - z, e


def _sb_fwd(proj):
    t = proj.shape[0]
    bq, ncol = SB_BLOCK, SB_FWD_COLS
    nq = t // bq
    scale = 1.0 / math.sqrt(SB_DIM)
    pairs = SB_WIDTH // HEAD_PAD

    def body(q_ref, k_ref, v_ref, o_ref, r_ref, first_ref):
        g, i = pl.program_id(0), pl.program_id(1)
        lane = lax.broadcasted_iota(jnp.int32, (bq, HEAD_PAD), 1)
        upper = _tri('gt', bq)
        chains = [(c, u) for c in range(ncol) for u in range(2)]
        qms = [_head_only(q_ref[:, _lanes(c)], lane, u) * scale for c, u in chains]

        def step(j, carry, masked):
            ids = range(len(chains))
            zs = [_dot(qms[n], _rows(k_ref, j, chains[n][0], bq), NT_DIMS) for n in ids]
            pos, neg, parts = [], [], []
            for n in ids:
                ls_pos, ls_neg, _ = _log_sigmoids(zs[n])
                if masked:
                    ls_neg = jnp.where(_causal(True, bq), ls_neg, 0.0)
                pos.append(ls_pos)
                neg.append(ls_neg)
                parts.append(_split_bf16(ls_neg))
            suffix = [_dot(parts[n][0], upper, NN_DIMS) + _dot(parts[n][1], upper, NN_DIMS) for n in ids]
            weights = []
            for n in ids:
                a = jnp.exp(pos[n] + suffix[n] + carry[n][0])
                if masked:
                    a = jnp.where(_causal(True, bq), a, 0.0)
                weights.append(a.astype(BF16))
            return tuple((carry[n][0] + jnp.sum(neg[n], axis=-1, keepdims=True),
                          carry[n][1] + _dot(weights[n], _rows(v_ref, j, chains[n][0], bq), NN_DIMS)) for n in ids)

        init = tuple((jnp.zeros((bq, 1), F32), jnp.zeros((bq, HEAD_PAD), F32)) for _ in chains)
        carry = step(i, init, True)

        def more(state):
            s, cr = state
            live = cr[0][0]
            for n in range(1, len(chains)):
                live = jnp.maximum(live, cr[n][0])
            return jnp.logical_and(s < i, jnp.max(live) > SB_DEAD)

        walked, carry = lax.while_loop(more, lambda st: (st[0] + 1, step(i - 1 - st[0], st[1], False)),
                                       (jnp.int32(0), carry))
        first_ref[g * nq + i] = i - walked
        for n, (c, u) in enumerate(chains):
            r_ref[2 * c + u] = carry[n][0]
        for c in range(ncol):
            o_ref[:, _lanes(c)] = jnp.where(lane < SB_DIM, carry[2 * c][1], carry[2 * c + 1][1]).astype(BF16)

    width = ncol * HEAD_PAD

    def full(c0):
        return pl.BlockSpec((t, width), lambda g, i: (0, c0 // width + g))

    return pl.pallas_call(
        body, name="sb_fwd", grid=(pairs // ncol, t // bq),
        in_specs=[pl.BlockSpec((bq, width), lambda g, i: (i, P_SBQ // width + g)), full(P_SBK), full(P_SBV)],
        out_specs=[pl.BlockSpec((bq, width), lambda g, i: (i, g)),
                   pl.BlockSpec((2 * ncol, bq, 1), lambda g, i: (g, i, 0)),
                   pl.BlockSpec(memory_space=pltpu.SMEM)],
        out_shape=[jax.ShapeDtypeStruct((t, SB_WIDTH), BF16), jax.ShapeDtypeStruct((N_HEADS, t, 1), F32),
                   jax.ShapeDtypeStruct((pairs // ncol * nq,), jnp.int32)],
        compiler_params=_params(("arbitrary", "arbitrary")),
    )(proj, proj, proj)


def _sb_bwd(proj, do, rtot, first):
    t = proj.shape[0]
    bq, ncol = SB_BLOCK, SB_BWD_COLS
    nq = t // bq
    scale = 1.0 / math.sqrt(SB_DIM)
    pairs = SB_WIDTH // HEAD_PAD

    def body(first_ref, q_ref, k_ref, v_ref, do_ref, r_ref, dq_ref, dk_ref, dv_ref):
        g, i = pl.program_id(0), pl.program_id(1)

        @pl.when(i == 0)
        def _():
            dk_ref[...] = jnp.zeros_like(dk_ref)
            dv_ref[...] = jnp.zeros_like(dv_ref)

        lane = lax.broadcasted_iota(jnp.int32, (bq, HEAD_PAD), 1)
        incl = _tri('le', bq)
        excl = _tri('lt', bq)
        chains = [(c, u) for c in range(ncol) for u in range(2)]
        qms = [_head_only(q_ref[:, _lanes(c)], lane, u) * scale for c, u in chains]
        doms = [_head_only(do_ref[:, _lanes(c)], lane, u) for c, u in chains]
        rts = [r_ref[2 * c + u] for c, u in chains]

        def step(j, carry, masked):
            ids = range(len(chains))
            kbs = [_rows(k_ref, j, c, bq) for c in range(ncol)]
            zs =[_dot(qms[n], kbs[chains[n][0]], NT_DIMS) for n in ids]
            das = [_dot(doms[n], _rows(v_ref, j, chains[n][0], bq), NT_DIMS) for n in ids]
            pos, neg, sigs, parts = [], [], [], []
            for n in ids:
                ls_pos, ls_neg, e = _log_sigmoids(zs[n])
                if masked:
                    ls_neg = jnp.where(_causal(True, bq), ls_neg, 0.0)
                pos.append(ls_pos)
                neg.append(ls_neg)
                sigs.append(jnp.where(zs[n] >= 0.0, 1.0, e) * pl.reciprocal(1.0 + e, approx=True))
                parts.append(_split_bf16(ls_neg))
            prefix = [_dot(parts[n][0], incl, NN_DIMS) + _dot(parts[n][1], incl, NN_DIMS) for n in ids]
            evs, eparts, dvs = [], [], []
            for n in ids:
                a = jnp.exp(pos[n] + (rts[n] - (carry[n][0] + prefix[n])))
                if masked:
                    a = jnp.where(_causal(True, bq), a, 0.0)
                dvs.append(_dot(a.astype(BF16), doms[n], TN_DIMS))
                evs.append(a * das[n])
                eparts.append(evs[n].astype(BF16))
            before = [_dot(eparts[n], excl, NN_DIMS) for n in ids]
            out, dks = [], []
            for n in ids:
                dz = evs[n] - sigs[n] * (evs[n] + (carry[n][1] + before[n]))
                if masked:
                    dz = jnp.where(_causal(True, bq), dz, 0.0)
                dzb = dz.astype(BF16)
                dks.append(_dot(dzb, qms[n], TN_DIMS))
                out.append((carry[n][0] + jnp.sum(neg[n], axis=-1, keepdims=True),
                            carry[n][1] + jnp.sum(evs[n], axis=-1, keepdims=True),
                            carry[n][2] + _dot(dzb, kbs[chains[n][0]], NN_DIMS)))
            for c in range(ncol):
                dv_ref[_row_block(j, bq), _lanes(c)] += dvs[2 * c] + dvs[2 * c + 1]
                dk_ref[_row_block(j, bq), _lanes(c)] += dks[2 * c] + dks[2 * c + 1]
            return tuple(out)

        init = tuple((jnp.zeros((bq, 1), F32), jnp.zeros((bq, 1), F32), jnp.zeros((bq, HEAD_PAD), F32)) for _ in chains)
        start = first_ref[(g * ncol // SB_FWD_COLS) * nq + i]
        carry = lax.fori_loop(start, i, lambda j, cr: step(j, cr, False), init)
        carry = step(i, carry, True)
        for c in range(ncol):
            dq_ref[:, _lanes(c)] = jnp.where(lane < SB_DIM, carry[2 * c][2], carry[2 * c + 1][2]) * scale

    width = ncol * HEAD_PAD

    def full(c0):
        return pl.BlockSpec((t, width), lambda g, i, first: (0, c0 // width + g))

    blk = pl.BlockSpec((bq, width), lambda g, i, first: (i, g))
    acc = pl.BlockSpec((t, width), lambda g, i, first: (0, g))
    wide = jax.ShapeDtypeStruct((t, SB_WIDTH), F32)
    return pl.pallas_call(
        body, name="sb_bwd",
        grid_spec=pltpu.PrefetchScalarGridSpec(
            num_scalar_prefetch=1, grid=(pairs // ncol, nq),
            in_specs=[pl.BlockSpec((bq, width), lambda g, i, first: (i, P_SBQ // width + g)), full(P_SBK), full(P_SBV),
                      blk, pl.BlockSpec((2 * ncol, bq, 1), lambda g, i, first: (g, i, 0))],
            out_specs=[blk, acc, acc]),
        out_shape=[wide, wide, wide],
        compiler_params=_params(("arbitrary", "arbitrary")),
    )(first, proj, proj, proj, do, rtot)


def _cols_to_full(g):
    n, r, c = g.shape
    return jnp.transpose(g, (1, 0, 2)).reshape(r, n * c)


def _full_to_cols(w):
    r, c = w.shape
    return jnp.transpose(w.reshape(r, N_DEV, c // N_DEV), (1, 0, 2))


TRANSPOSED = ('ffn1_w_in', 'ffn2_w_in', 'w_in', 'w_q_up')


def _layout_weight(name, g):
    if name in ('ffn1_w_out', 'ffn2_w_out', 'w_out', 'w_ple_gate', 'ffn1_w_in', 'ffn2_w_in'):
        return g.reshape(g.shape[0] * g.shape[1], g.shape[2])
    if name == 'w_in':
        wt = g.reshape(IN_COLS, D_MODEL)
        z = lambda n: jnp.zeros((n, D_MODEL), BF16)
        return jnp.concatenate([wt[0:640], z(64), wt[640:672], z(32), z(256), wt[2208:4256], wt[672:2208]], axis=0)
    if name == 'w_q_up':
        return jnp.pad(g, ((0, 0), (0, HEAD_PAD - MLA_QK), (0, 0))).reshape(N_HEADS * HEAD_PAD, Q_LORA)
    if name == 'w_branch_mla':
        bm = _cols_to_full(g).reshape(N_HEADS, MLA_NOPE, D_MODEL)
        return jnp.pad(bm, ((0, 0), (HEAD_PAD - MLA_NOPE, 0), (0, 0))).reshape(N_HEADS * HEAD_PAD, D_MODEL)
    return _cols_to_full(g)


def _layout_weights(g):
    return {n: _layout_weight(n, a) for n, a in g.items()}


def _unlayout_grad(name, d):
    if name == 'w_in':
        d = jnp.concatenate([d[0:640], d[704:736], d[P_SBQ:PROJ_W], d[P_GM:P_SBQ]], axis=0)
    if name == 'w_q_up':
        return d.reshape(N_HEADS, HEAD_PAD, Q_LORA)[:, :MLA_QK, :]
    if name in ('ffn1_w_out', 'ffn2_w_out', 'w_out', 'w_ple_gate', 'ffn1_w_in', 'ffn2_w_in', 'w_in'):
        return d.reshape(N_DEV, d.shape[0] // N_DEV, d.shape[1])
    if name == 'w_branch_mla':
        d = d.reshape(N_HEADS, HEAD_PAD, D_MODEL)[:, HEAD_PAD - MLA_NOPE:, :].reshape(SB_WIDTH, D_MODEL)
    return _full_to_cols(d)


def _unlayout_grads(d):
    return {n: _unlayout_grad(n, a) for n, a in d.items()}


def _rope_tables(positions):
    half = MLA_ROPE // 2
    inv_freq = ROPE_BASE ** (-jnp.arange(0, MLA_ROPE, 2, dtype=F32) / MLA_ROPE)
    ang = positions.astype(F32)[:, None] * inv_freq
    cos, sin = jnp.cos(ang), jnp.sin(ang)
    t = positions.shape[0]
    ones = lambda n: jnp.ones((t, n), F32)
    zeros = lambda n: jnp.zeros((t, n), F32)
    cosf = jnp.concatenate([ones(MLA_NOPE), cos, cos, ones(HEAD_PAD - MLA_QK)], axis=1)
    sin_a = jnp.concatenate([zeros(MLA_NOPE), -sin, zeros(half), zeros(HEAD_PAD - MLA_QK)], axis=1)
    sin_b = jnp.concatenate([zeros(MLA_NOPE), zeros(half), sin, zeros(HEAD_PAD - MLA_QK)], axis=1)
    return cosf, sin_a, sin_b


def _local_step(x, p, positions, tgt, norms, plan):
    mm = _matmul
    cosf, sin_a, sin_b = _rope_tables(positions)
    pad_head = lambda g: jnp.pad(g, ((0, 0), (0, HEAD_PAD - MLA_QK)))
    gqh, gkh = pad_head(norms['q_head_norm']), pad_head(norms['k_head_norm'])
    pb = p.astype(BF16)
    w = dict(plan.first_weights())
    dw, dn = {}, {}

    def ride(host, call):
        cargo = plan.cargo(host, dw)
        res, lands = call(cargo), None
        if cargo is not None:
            *res, lands = res
            res = res[0] if len(res) == 1 else tuple(res)
        w.update(plan.landed(host, lands))
        return res

    def ffn_fwd(h, tag):
        n = _rmsnorm_fwd(h, norms[tag + '_norm'], tag + "_norm_fwd")
        a, b, act = ride(tag + "_in_fwd", lambda cargo: _swiglu_fwd(n, w[tag + '_w_in'], tag + "_in_fwd", cargo))
        out = ride(tag + "_out_fwd", lambda cargo: mm(
            act, w[tag + '_w_out'], mode='nn', out_dtype=F32, name=tag + "_out_fwd", res=h, alpha=0.5, cargo=cargo))
        return out, (n, a, b, act)

    h1, ffn1_saved = ffn_fwd(x, 'ffn1')
    u = _rmsnorm_fwd(h1, norms['mix_norm'], "mix_norm_fwd")
    proj = mm(u, w['w_in'], mode='nt', out_dtype=BF16, name="proj_fwd")
    cqn, ckvn = _latent_fwd(proj, norms['q_latent_norm'], norms['kv_latent_norm'])
    qraw = mm(cqn, w['w_q_up'], mode='nt', out_dtype=F32, name="q_up_fwd")
    kvraw = mm(ckvn, w['w_kv_up'], mode='nn', out_dtype=F32, name="kv_up_fwd")
    qh, kh, kvb = _headprep_fwd(qraw, kvraw, proj, cosf, sin_a, sin_b, gqh, gkh)
    o_mla, lse = ride("mla_fwd", lambda cargo: _mla_fwd(qh, kh, kvb, cargo))
    o_sb, rtot, sb_first = _sb_fwd(proj)
    bm = mm(o_mla, w['w_branch_mla'], mode='nn', out_dtype=F32, name="branch_mla_fwd")
    bs = mm(o_sb, w['w_branch_sb'], mode='nn', out_dtype=F32, name="branch_sb_fwd")
    merged = _merge_fwd(proj, bm, bs)
    h2 = mm(merged, w['w_out'], mode='nn', out_dtype=F32, name="mix_out_fwd", res=h1)
    h3, ffn2_saved = ffn_fwd(h2, 'ffn2')
    n3 = _rmsnorm_fwd(h3, norms['ple_norm'], "ple_norm_fwd")
    zg = mm(n3, w['w_ple_gate'], mode='nn', out_dtype=F32, name="ple_gate_fwd")
    pp = mm(pb, w['w_ple_proj'], mode='nn', out_dtype=F32, name="ple_proj_fwd")
    dh4, dzg, dpp, loss_lanes = _ple_loss(h3, zg, pp, tgt)

    dw['w_ple_gate'] = mm(n3, dzg, mode='tn', out_dtype=BF16, name="ple_gate_dw")
    dw['w_ple_proj'] = mm(pb, dpp, mode='tn', out_dtype=BF16, name="ple_proj_dw")
    dh3, dhb3, dn['ple_norm'] = _matmul_norm_bwd(
        dzg, w['w_ple_gate'], h3, norms['ple_norm'], dh4, mode='nt', name="ple_gate_dx", out_scale=0.5)

    def ffn_bwd(h, dh, dhb, saved, tag, out_scale):
        n, a, b, act = saved
        dw[tag + '_w_out'] = mm(act, dhb, mode='tn', out_dtype=BF16, name=tag + "_out_dw", tm=1408)
        dab = _swiglu_bwd(dhb, w[tag + '_w_out'], a, b, tag + "_out_dx")
        dw[tag + '_w_in'] = ride(tag + "_in_dw", lambda cargo: mm(
            dab, n, mode='tn', out_dtype=BF16, name=tag + "_in_dw", tm=1408, cargo=cargo))
        dh_prev, dhb_prev, dn[tag + '_norm'] = ride(tag + "_in_dx", lambda cargo: _matmul_norm_bwd(
            dab, w[tag + '_w_in'], h, norms[tag + '_norm'], dh, mode='nn', name=tag + "_in_dx", out_scale=out_scale,
            cargo=cargo))
        return dh_prev, dhb_prev

    dh2, dhb2 = ffn_bwd(h2, dh3, dhb3, ffn2_saved, 'ffn2', 1.0)
    dw['w_out'] = mm(merged, dhb2, mode='tn', out_dtype=BF16, name="mix_out_dw")
    dmerged = mm(dhb2, w['w_out'], mode='nt', out_dtype=F32, name="mix_out_dx")
    dbm, dbs, dgates = _merge_bwd(dmerged, proj, bm, bs)
    dw['w_branch_mla'] = mm(o_mla, dbm, mode='tn', out_dtype=BF16, name="branch_mla_dw")
    dw['w_branch_sb'] = mm(o_sb, dbs, mode='tn', out_dtype=BF16, name="branch_sb_dw")
    do_mla = mm(dbm, w['w_branch_mla'], mode='nt', out_dtype=BF16, name="branch_mla_dx")
    do_sb = mm(dbs, w['w_branch_sb'], mode='nt', out_dtype=BF16, name="branch_sb_dx")
    dqh, dkh, dvp = ride("mla_bwd", lambda cargo: _mla_bwd(qh, kh, kvb, o_mla, do_mla, lse, cargo))
    dsq, dsk, dsv = _sb_bwd(proj, do_sb, rtot, sb_first)
    dqraw, dkvraw, dkr, dgq, dgk = _headprep_bwd(dqh, dkh, dvp, qraw, kvraw, proj, cosf, sin_a, sin_b, gqh, gkh)
    dn['q_head_norm'], dn['k_head_norm'] = dgq[:, :MLA_QK], dgk[:, :MLA_QK]
    dw['w_q_up'] = mm(dqraw, cqn, mode='tn', out_dtype=BF16, name="q_up_dw")
    dw['w_kv_up'] = mm(ckvn, dkvraw, mode='tn', out_dtype=BF16, name="kv_up_dw")
    dcqn = mm(dqraw, w['w_q_up'], mode='nn', out_dtype=F32, name="q_up_dx")
    dckvn = mm(dkvraw, w['w_kv_up'], mode='nt', out_dtype=F32, name="kv_up_dx")
    dlat, dn['q_latent_norm'], dn['kv_latent_norm'] = _latent_bwd(
        dcqn, dckvn, proj, dkr, norms['q_latent_norm'], norms['kv_latent_norm'])
    dproj = jnp.concatenate([dlat, dgates, dsq.astype(BF16), dsk.astype(BF16), dsv.astype(BF16)], axis=1)
    dw['w_in'] = ride("proj_dw", lambda cargo: mm(dproj, u, mode='tn', out_dtype=BF16, name="proj_dw", tm=1536, cargo=cargo))
    dh1, dhb1, dn['mix_norm'] = ride("proj_dx", lambda cargo: _matmul_norm_bwd(
        dproj, w['w_in'], h1, norms['mix_norm'], dh2, mode='nn', name="proj_dx", out_scale=0.5, cargo=cargo))
    dx, _ = ffn_bwd(x, dh1, dhb1, ffn1_saved, 'ffn1', 1.0)
    return dx, loss_lanes, dw, dn


MESH = pl.DeviceIdType.MESH
HBM_SPEC = pl.BlockSpec(memory_space=pl.ANY)


def _position():
    return lax.axis_index("x"), lax.axis_index("y"), lax.axis_index("c")


def _index(px, py, pc):
    return 4 * px + 2 * py + pc


def _all_gather(shards):
    n = len(shards)

    def body(*refs):
        ins, outs = refs[:n], refs[n:2 * n]
        send_sems, recv_sems, local_sems = refs[2 * n:]
        x, y, c = _position()
        me, sibling = (x, y, c), (x, y, 1 - c)
        chips = [(1 - x, y), (x, 1 - y), (1 - x, 1 - y)]

        def copy(a, k, block, to, own=False):
            dst = outs[a].at[_index(*block)]
            return pltpu.make_async_remote_copy(
                src_ref=ins[a] if own else dst, dst_ref=dst,
                send_sem=send_sems.at[a, k], recv_sem=recv_sems.at[a, k], device_id=to, device_id_type=MESH)

        mine = [pltpu.make_async_copy(ins[a], outs[a].at[_index(*me)], local_sems.at[a]) for a in range(n)]
        for cp in mine:
            cp.start()
        first = []
        for a in range(n):
            first.append(copy(a, 0, me, sibling, own=True))
            first += [copy(a, 1 + j, me, (*chip, c), own=True) for j, chip in enumerate(chips)]
        for cp in first:
            cp.start()
        passed = []
        for j, chip in enumerate(chips):
            for a in range(n):
                copy(a, 1 + j, (*chip, c), me).wait_recv()
                fwd = copy(a, 4 + j, (*chip, c), sibling)
                fwd.start()
                passed.append(fwd)
        for a in range(n):
            copy(a, 0, sibling, me).wait_recv()
            for j, chip in enumerate(chips):
                copy(a, 4 + j, (*chip, 1 - c), me).wait_recv()
        for cp in first + passed:
            cp.wait_send()
        for cp in mine:
            cp.wait()

    return pl.pallas_call(
        body, name="weights_all_gather",
        in_specs=[HBM_SPEC] * n, out_specs=[HBM_SPEC] * n,
        out_shape=[jax.ShapeDtypeStruct((N_DEV,) + s.shape, s.dtype) for s in shards],
        scratch_shapes=[pltpu.SemaphoreType.DMA((n, 7)), pltpu.SemaphoreType.DMA((n, 7)), pltpu.SemaphoreType.DMA((n,))],
    )(*shards)


def _exchange(parts):
    n = len(parts)
    masks = [(mx, my, mc) for mx in (0, 1) for my in (0, 1) for mc in (0, 1)][1:]

    def body(*refs):
        ins, outs = refs[:n], refs[n:2 * n]
        send_sems, recv_sems, local_sems = refs[2 * n:]
        x, y, c = _position()
        me = _index(x, y, c)

        def peer_of(mask):
            mx, my, mc = mask
            return (x + mx - 2 * x * mx, y + my - 2 * y * my, c + mc - 2 * c * mc)

        def copy(a, k):
            peer = peer_of(masks[k])
            return pltpu.make_async_remote_copy(
                src_ref=ins[a].at[_index(*peer)], dst_ref=outs[a].at[me],
                send_sem=send_sems.at[a, k], recv_sem=recv_sems.at[a, k], device_id=peer, device_id_type=MESH)

        def landed(a, k):
            peer = peer_of(masks[k])
            return pltpu.make_async_remote_copy(
                src_ref=ins[a].at[me], dst_ref=outs[a].at[_index(*peer)],
                send_sem=send_sems.at[a, k], recv_sem=recv_sems.at[a, k], device_id=peer, device_id_type=MESH)

        mine = [pltpu.make_async_copy(ins[a].at[me], outs[a].at[me], local_sems.at[a]) for a in range(n)]
        for cp in mine:
            cp.start()
        sent = [copy(a, k) for k in range(7) for a in range(n)]
        for cp in sent:
            cp.start()
        for k in range(7):
            for a in range(n):
                landed(a, k).wait_recv()
        for cp in sent:
            cp.wait_send()
        for cp in mine:
            cp.wait()

    return pl.pallas_call(
        body, name="grads_exchange",
        in_specs=[HBM_SPEC] * n, out_specs=[HBM_SPEC] * n,
        out_shape=[jax.ShapeDtypeStruct(s.shape, s.dtype) for s in parts],
        scratch_shapes=[pltpu.SemaphoreType.DMA((n, 7)), pltpu.SemaphoreType.DMA((n, 7)), pltpu.SemaphoreType.DMA((n,))],
    )(*parts)


PEER_MASKS = [(mx, my, mc) for mx in (0, 1) for my in (0, 1) for mc in (0, 1)][1:]


def _peer(mask):
    x, y, c = _position()
    mx, my, mc = mask
    return (x + mx - 2 * x * mx, y + my - 2 * y * my, c + mc - 2 * c * mc)


class _Cargo:
    def __init__(self, srcs, scatter):
        self.srcs, self.scatter, self.n = list(srcs), scatter, len(srcs)

    def specs(self):
        return [HBM_SPEC] * self.n

    def out_shape(self):
        return [jax.ShapeDtypeStruct(s.shape if self.scatter else (N_DEV,) + s.shape, s.dtype) for s in self.srcs]

    def scratch(self):
        per_copy = pltpu.SemaphoreType.DMA((self.n, len(PEER_MASKS)))
        return [per_copy, per_copy, pltpu.SemaphoreType.DMA((self.n,))]

    def _mine(self, src_refs, a, to):
        return src_refs[a].at[to] if self.scatter else src_refs[a]

    def _shard_copy(self, src_refs, land_refs, sems, a, k, block, to, own=False):
        dst = land_refs[a].at[_index(*block)]
        return pltpu.make_async_remote_copy(
            src_ref=src_refs[a] if own else dst, dst_ref=dst,
            send_sem=sems[0].at[a, k], recv_sem=sems[1].at[a, k], device_id=to, device_id_type=MESH)

    def _first_hops(self, src_refs, land_refs, sems):
        x, y, c = _position()
        chips = [(1 - x, y), (x, 1 - y), (1 - x, 1 - y)]
        hops = []
        for a in range(self.n):
            hops.append(self._shard_copy(src_refs, land_refs, sems, a, 0, (x, y, c), (x, y, 1 - c), own=True))
            hops += [self._shard_copy(src_refs, land_refs, sems, a, 1 + j, (x, y, c), (*chip, c), own=True)
                     for j, chip in enumerate(chips)]
        return hops, chips

    def start(self, src_refs, land_refs, sems):
        send, recv, local = sems
        me = _index(*_position())
        for a in range(self.n):
            pltpu.make_async_copy(self._mine(src_refs, a, me), land_refs[a].at[me], local.at[a]).start()
        if not self.scatter:
            for cp in self._first_hops(src_refs, land_refs, sems)[0]:
                cp.start()
            return
        for k, mask in enumerate(PEER_MASKS):
            peer = _peer(mask)
            for a in range(self.n):
                pltpu.make_async_remote_copy(
                    src_ref=self._mine(src_refs, a, _index(*peer)), dst_ref=land_refs[a].at[me],
                    send_sem=send.at[a, k], recv_sem=recv.at[a, k], device_id=peer, device_id_type=MESH).start()

    def _wait_gathered(self, src_refs, land_refs, sems):
        x, y, c = _position()
        me, sibling = (x, y, c), (x, y, 1 - c)
        first, chips = self._first_hops(src_refs, land_refs, sems)
        passed = []
        for j, chip in enumerate(chips):
            for a in range(self.n):
                self._shard_copy(src_refs, land_refs, sems, a, 1 + j, (*chip, c), me).wait_recv()
                passed.append(self._shard_copy(src_refs, land_refs, sems, a, 4 + j, (*chip, c), sibling))
                passed[-1].start()
        for a in range(self.n):
            self._shard_copy(src_refs, land_refs, sems, a, 0, sibling, me).wait_recv()
            for j, chip in enumerate(chips):
                self._shard_copy(src_refs, land_refs, sems, a, 4 + j, (*chip, 1 - c), me).wait_recv()
        for cp in first + passed:
            cp.wait_send()

    def wait(self, src_refs, land_refs, sems):
        send, recv, local = sems
        me = _index(*_position())
        if not self.scatter:
            self._wait_gathered(src_refs, land_refs, sems)
        for k, mask in enumerate(PEER_MASKS if self.scatter else []):
            peer = _peer(mask)
            there = _index(*peer)
            for a in range(self.n):
                pltpu.make_async_remote_copy(
                    src_ref=self._mine(src_refs, a, me), dst_ref=land_refs[a].at[there],
                    send_sem=send.at[a, k], recv_sem=recv.at[a, k], device_id=peer, device_id_type=MESH).wait_recv()
                pltpu.make_async_remote_copy(
                    src_ref=self._mine(src_refs, a, there), dst_ref=land_refs[a].at[me],
                    send_sem=send.at[a, k], recv_sem=recv.at[a, k], device_id=peer, device_id_type=MESH).wait_send()
        for a in range(self.n):
            pltpu.make_async_copy(self._mine(src_refs, a, me), land_refs[a].at[me], local.at[a]).wait()


def _with_cargo(cargo, refs, n_in, n_out, steps, counts, compute):
    if cargo is None:
        compute(refs)
        return
    n = cargo.n
    src_refs = refs[n_in:n_in + n]
    land_refs = refs[n_in + n + n_out:n_in + 2 * n + n_out]
    sems = refs[-3:]
    first = functools.reduce(jnp.logical_and, [s == 0 for s in steps])
    last = functools.reduce(jnp.logical_and, [s == c - 1 for s, c in zip(steps, counts)])

    @pl.when(first)
    def _():
        cargo.start(src_refs, land_refs, sems)

    compute(refs[:n_in] + refs[n_in + n:n_in + n + n_out] + refs[n_in + 2 * n + n_out:-3])

    @pl.when(last)
    def _():
        cargo.wait(src_refs, land_refs, sems)


def _adamw(parts, w, m, v, name):
    r, c = w.shape
    tr = next((t for t in (512, 384, 352, 256, 128) if r % t == 0), r) if r > 512 else r
    tc = c if tr < r or r <= 512 else 256
    assert r % tr == 0 and c % tc == 0
    bc1 = 1.0 - ADAM_B1 ** ADAM_STEP
    bc2 = 1.0 - ADAM_B2 ** ADAM_STEP

    def body(p_ref, w_ref, m_ref, v_ref, g_ref, d_ref, nm_ref, nv_ref):
        g = p_ref[0].astype(F32)
        for s in range(1, N_DEV):
            g = g + p_ref[s].astype(F32)
        nm = ADAM_B1 * m_ref[...] + (1.0 - ADAM_B1) * g
        nv = ADAM_B2 * v_ref[...] + (1.0 - ADAM_B2) * (g * g)
        g_ref[...] = g
        nm_ref[...] = nm
        nv_ref[...] = nv
        d_ref[...] = -ADAM_LR * ((nm / bc1) / (jnp.sqrt(nv / bc2) + ADAM_EPS) + ADAM_WD * w_ref[...])

    tile = pl.BlockSpec((tr, tc), lambda i, j: (i, j))
    out = jax.ShapeDtypeStruct((r, c), F32)
    return pl.pallas_call(
        body, name=name, grid=(r // tr, c // tc),
        in_specs=[pl.BlockSpec((N_DEV, tr, tc), lambda i, j: (0, i, j)), tile, tile, tile],
        out_specs=[tile] * 4, out_shape=[out] * 4,
        compiler_params=_params(("parallel", "parallel")),
    )(parts, w, m, v)


GATHER_FIRST = ['ffn1_w_in']
RIDES = {
    'ffn1_in_fwd': ('weights', ['ffn1_w_out', 'w_in']),
    'ffn1_out_fwd': ('weights', ['w_q_up', 'w_kv_up', 'w_branch_mla', 'w_branch_sb', 'w_out']),
    'mla_fwd': ('weights', ['ffn2_w_in', 'ffn2_w_out', 'w_ple_gate', 'w_ple_proj']),
    'mla_bwd': ('grads', ['w_ple_gate', 'w_ple_proj', 'ffn2_w_out', 'ffn2_w_in', 'w_out', 'w_branch_mla', 'w_branch_sb']),
    'proj_dw': ('grads', ['w_q_up', 'w_kv_up']),
    'proj_dx': ('grads', ['w_in']),
    'ffn1_in_dw': ('grads', ['ffn1_w_out']),
    'ffn1_in_dx': ('grads', ['ffn1_w_in']),
}


class _Plan:
    def __init__(self, shards):
        self.shards = shards
        self.received = {}

    def first_weights(self):
        gathered = _all_gather([self.shards[n] for n in GATHER_FIRST])
        return {n: _layout_weight(n, g) for n, g in zip(GATHER_FIRST, gathered)}

    def cargo(self, host, dw):
        if host not in RIDES:
            return None
        kind, names = RIDES[host]
        if kind == 'weights':
            return _Cargo([self.shards[n] for n in names], False)
        return _Cargo([_unlayout_grad(n, dw.pop(n)) for n in names], True)

    def landed(self, host, lands):
        if host not in RIDES:
            return {}
        kind, names = RIDES[host]
        if kind == 'weights':
            return {n: _layout_weight(n, land) for n, land in zip(names, lands)}
        self.received.update(zip(names, lands))
        return {}


def _pack_small(vecs):
    flat = jnp.concatenate([v.reshape(-1) for v in vecs])
    return jnp.pad(flat, (0, SMALL_ROWS * 128 - flat.shape[0])).reshape(SMALL_ROWS, 128)


def _unpack_small(packed, sizes):
    flat = packed.reshape(-1)
    out, at = [], 0
    for n in sizes:
        out.append(flat[at:at + n])
        at += n
    return out


def kernel(x, p, positions, ffn1_norm, ffn1_w_in, ffn1_w_out, mix_norm, w_in, q_latent_norm, w_q_up, kv_latent_norm, w_kv_up, q_head_norm, k_head_norm, w_branch_mla, w_branch_sb, w_out, ffn2_norm, ffn2_w_in, ffn2_w_out, ple_norm, w_ple_gate, w_ple_proj, loss_target, m_ffn1_norm, m_ffn1_w_in, m_ffn1_w_out, m_mix_norm, m_w_in, m_q_latent_norm, m_w_q_up, m_kv_latent_norm, m_w_kv_up, m_q_head_norm, m_k_head_norm, m_w_branch_mla, m_w_branch_sb, m_w_out, m_ffn2_norm, m_ffn2_w_in, m_ffn2_w_out, m_ple_norm, m_w_ple_gate, m_w_ple_proj, v_ffn1_norm, v_ffn1_w_in, v_ffn1_w_out, v_mix_norm, v_w_in, v_q_latent_norm, v_w_q_up, v_kv_latent_norm, v_w_kv_up, v_q_head_norm, v_k_head_norm, v_w_branch_mla, v_w_branch_sb, v_w_out, v_ffn2_norm, v_ffn2_w_in, v_ffn2_w_out, v_ple_norm, v_w_ple_gate, v_w_ple_proj):
    given = dict(locals())
    wts = {n: given[n] for n in WEIGHTS}
    mom = {n: given['m_' + n] for n in WEIGHTS}
    var = {n: given['v_' + n] for n in WEIGHTS}

    def local(a, n):
        return jnp.swapaxes(a[0], 0, 1) if n in TRANSPOSED else a[0]

    plan = _Plan({n: local(wts[n], n).astype(BF16) for n in MATS})
    norms = {n: wts[n] for n in NORMS}
    dx, loss_lanes, dw, dn = _local_step(x[0], p[0, 0], positions[0], loss_target[0], norms, plan)
    assert not dw

    out = {}
    for n in MATS:
        res = _adamw(plan.received[n], local(wts[n], n), local(mom[n], n), local(var[n], n), "adamw_" + n)
        out[n] = [local(r[None], n)[None] for r in res]
    small = _pack_small([dn[n] for n in NORMS] + [0.5 / D_MODEL * jnp.sum(loss_lanes)[None]])
    small_parts = _exchange([jnp.broadcast_to(small[None], (N_DEV, SMALL_ROWS, 128))])[0]
    sizes = [wts[n].shape[1] for n in NORMS]
    pack = lambda d: _pack_small([d[n] for n in NORMS])
    small_res = _adamw(small_parts, pack(wts), pack(mom), pack(var), "adamw_norms")
    loss = small_res[0].reshape(-1)[sum(sizes)]
    for i, res in enumerate(small_res):
        for n, vec in zip(NORMS, _unpack_small(res, sizes)):
            out.setdefault(n, [None] * 4)[i] = vec[None]

    return (loss, dx[None], *[out[n][0] for n in WEIGHTS], *[out[n][1] for n in WEIGHTS],
            *[out[n][2] for n in WEIGHTS], *[out[n][3] for n in WEIGHTS])
```

```python
import functools
import math

import jax
import jax.numpy as jnp
from jax import lax
from jax.experimental import pallas as pl
from jax.experimental.pallas import tpu as pltpu

F32 = jnp.float32
BF16 = jnp.bfloat16

N_DEV = 8
D_MODEL = 1024
D_FF = 2816
PLE_DIM = 256
NORM_EPS = 1e-6
N_HEADS = 8
HEAD_PAD = 128
MLA_NOPE = 64
MLA_ROPE = 32
MLA_QK = 96
Q_LORA = 384
KV_LORA = 256
SB_DIM = 64
SB_WIDTH = 512
ROPE_BASE = 10000.0
IN_COLS = 4256

PROJ_W = 4608
P_CQ, P_CKV, P_KR, P_GM, P_GS, P_SBQ, P_SBK, P_SBV = 0, 384, 640, 1024, 2048, 3072, 3584, 4096

ADAM_LR, ADAM_B1, ADAM_B2, ADAM_EPS, ADAM_WD, ADAM_STEP = 0.001, 0.9, 0.999, 1e-08, 0.01, 10

VMEM_LIMIT = 52 * 1024 * 1024
MATMUL_VMEM = 40 * 1024 * 1024

WEIGHTS = ['ffn1_norm', 'ffn1_w_in', 'ffn1_w_out', 'mix_norm', 'w_in', 'q_latent_norm', 'w_q_up',
           'kv_latent_norm', 'w_kv_up', 'q_head_norm', 'k_head_norm', 'w_branch_mla', 'w_branch_sb',
           'w_out', 'ffn2_norm', 'ffn2_w_in', 'ffn2_w_out', 'ple_norm', 'w_ple_gate', 'w_ple_proj']
NORMS = ['ffn1_norm', 'mix_norm', 'q_latent_norm', 'kv_latent_norm', 'q_head_norm', 'k_head_norm',
         'ffn2_norm', 'ple_norm']
MATS = [n for n in WEIGHTS if n not in NORMS]
SMALL_ROWS = 48

NT_DIMS = (((1,), (1,)), ((), ()))
NN_DIMS = (((1,), (0,)), ((), ()))
TN_DIMS = (((0,), (0,)), ((), ()))


def _params(sem=None, vmem=VMEM_LIMIT):
    return pltpu.CompilerParams(dimension_semantics=sem, vmem_limit_bytes=vmem)


def _pick(n, cap):
    if n <= cap:
        return n
    best = None
    for t in range(128, cap + 1, 128):
        if n % t == 0:
            best = t
    assert best is not None, (n, cap)
    return best


def _dot(a, b, dims):
    return lax.dot_general(a, b, dims, preferred_element_type=F32)


def _matmul(a, b, *, mode, out_dtype, name, tm=None, tn=None, tk=None, res=None, alpha=1.0, cargo=None):
    if mode == 'nn':
        (m, k), (k2, n) = a.shape, b.shape
    elif mode == 'nt':
        (m, k), (n, k2) = a.shape, b.shape
    else:
        (k, m), (k2, n) = a.shape, b.shape
    assert k == k2, (name, a.shape, b.shape)
    has_res = res is not None
    tn = tn or _pick(n, 512)

    def vmem(tm_, tk_):
        io = 2 * 2 * (tm_ * tk_ + tk_ * tn) + 2 * tm_ * tn * (jnp.dtype(out_dtype).itemsize + 4 * has_res)
        return io + (4 * tm_ * tn if tk_ < k else 0)

    tries = [(tm_, tk_) for tk_ in ([tk] if tk else [k, _pick(k, 2048)])
             for tm_ in ([tm] if tm else [_pick(m, 2048), _pick(m, 1024), _pick(m, 512)])]
    tm, tk = next((c for c in tries if vmem(*c) <= MATMUL_VMEM), tries[-1])
    assert m % tm == 0 and n % tn == 0 and k % tk == 0, (name, m, n, k, tm, tn, tk)
    nk = k // tk
    dims = {'nn': NN_DIMS, 'nt': NT_DIMS, 'tn': TN_DIMS}[mode]

    def epilogue(acc, r_ref, o_ref):
        if alpha != 1.0:
            acc = acc * alpha
        if has_res:
            acc = r_ref[...] + acc
        o_ref[...] = acc.astype(out_dtype)

    grid = (m // tm, n // tn, nk)

    def body(*refs):
        steps = [pl.program_id(d) for d in range(3)]

        def compute(own):
            a_ref, b_ref = own[0], own[1]
            r_ref = own[2] if has_res else None
            o_ref = own[2 + has_res]
            if nk == 1:
                epilogue(_dot(a_ref[...], b_ref[...], dims), r_ref, o_ref)
                return
            acc_ref = own[-1]

            @pl.when(steps[2] == 0)
            def _():
                acc_ref[...] = jnp.zeros_like(acc_ref)

            acc_ref[...] += _dot(a_ref[...], b_ref[...], dims)

            @pl.when(steps[2] == nk - 1)
            def _():
                epilogue(acc_ref[...], r_ref, o_ref)

        _with_cargo(cargo, refs, 2 + has_res, 1, steps, grid, compute)

    if mode == 'tn':
        a_spec = pl.BlockSpec((tk, tm), lambda i, j, kk: (kk, i))
    else:
        a_spec = pl.BlockSpec((tm, tk), lambda i, j, kk: (i, kk))
    if mode == 'nt':
        b_spec = pl.BlockSpec((tn, tk), lambda i, j, kk: (j, kk))
    else:
        b_spec = pl.BlockSpec((tk, tn), lambda i, j, kk: (kk, j))
    o_spec = pl.BlockSpec((tm, tn), lambda i, j, kk: (i, j))
    in_specs = [a_spec, b_spec] + ([o_spec] if has_res else [])
    args = (a, b) + ((res,) if has_res else ())
    out_shape = jax.ShapeDtypeStruct((m, n), out_dtype)
    scratch = [pltpu.VMEM((tm, tn), F32)] if nk > 1 else []
    if cargo is None:
        return pl.pallas_call(
            body, name=name, grid=grid, in_specs=in_specs, out_specs=o_spec, out_shape=out_shape,
            scratch_shapes=scratch, compiler_params=_params(("parallel", "parallel", "arbitrary")),
        )(*args)
    outs = pl.pallas_call(
        body, name=name, grid=grid, in_specs=in_specs + cargo.specs(), out_specs=[o_spec] + cargo.specs(),
        out_shape=[out_shape] + cargo.out_shape(), scratch_shapes=scratch + cargo.scratch(),
        compiler_params=_params(("arbitrary", "arbitrary", "arbitrary")),
    )(*args, *cargo.srcs)
    return outs[0], list(outs[1:])


def _row_tile(t, cap=512):
    return min(t, cap)


def _rms(x, width):
    return lax.rsqrt(jnp.sum(x * x, axis=-1, keepdims=True) * (1.0 / width) + NORM_EPS)


def _rmsnorm_fwd(x, g, name):
    t, d = x.shape
    tr = _row_tile(t)

    def body(x_ref, g_ref, o_ref):
        xv = x_ref[...]
        o_ref[...] = ((xv * _rms(xv, d)) * g_ref[...]).astype(BF16)

    return pl.pallas_call(
        body, name=name, grid=(t // tr,),
        in_specs=[pl.BlockSpec((tr, d), lambda i: (i, 0)), pl.BlockSpec((1, d), lambda i: (0, 0))],
        out_specs=pl.BlockSpec((tr, d), lambda i: (i, 0)),
        out_shape=jax.ShapeDtypeStruct((t, d), BF16),
        compiler_params=_params(("parallel",)),
    )(x, g)


def _matmul_norm_bwd(a, b, x, g, dh_in, *, mode, name, out_scale, cargo=None):
    m, k = a.shape
    d = x.shape[1]
    tn = _pick(d, 512)

    def vmem(tm_):
        return 2 * 2 * (tm_ * k + k * tn) + tm_ * d * (4 + 2 * (4 + 4) + 2 * (4 + 2))

    tm = next((c for c in (_pick(m, 1024), _pick(m, 512), _pick(m, 256)) if vmem(c) <= MATMUL_VMEM), _pick(m, 256))
    grid = (m // tm, d // tn)
    dims = {'nn': NN_DIMS, 'nt': NT_DIMS}[mode]

    def body(*refs):
        steps = [pl.program_id(0), pl.program_id(1)]

        def compute(own):
            a_ref, b_ref, x_ref, g_ref, dhin_ref, dh_ref, dhb_ref, dg_ref, dn_ref = own
            for jj in range(grid[1]):
                @pl.when(steps[1] == jj)
                def _(jj=jj):
                    dn_ref[:, jj * tn:(jj + 1) * tn] = _dot(a_ref[...], b_ref[...], dims)

            @pl.when(steps[1] == grid[1] - 1)
            def _():
                xv = x_ref[...]
                dnv = dn_ref[...]
                r = _rms(xv, d)
                y = xv * r
                dy = dnv * g_ref[...]
                dh = dhin_ref[...] + r * (dy - y * (jnp.sum(dy * y, axis=-1, keepdims=True) * (1.0 / d)))
                dh_ref[...] = dh
                dhb_ref[...] = (dh * out_scale).astype(BF16)
                part = jnp.sum(dnv * y, axis=0, keepdims=True)

                @pl.when(steps[0] == 0)
                def _():
                    dg_ref[...] = part

                @pl.when(steps[0] > 0)
                def _():
                    dg_ref[...] += part

        _with_cargo(cargo, refs, 5, 3, steps, grid, compute)

    b_spec = pl.BlockSpec((k, tn), lambda i, j: (0, j)) if mode == 'nn' else pl.BlockSpec((tn, k), lambda i, j: (j, 0))
    row = pl.BlockSpec((tm, d), lambda i, j: (i, 0))
    vec = pl.BlockSpec((1, d), lambda i, j: (0, 0))
    extra = cargo.specs() if cargo else []
    outs = pl.pallas_call(
        body, name=name, grid=grid,
        in_specs=[pl.BlockSpec((tm, k), lambda i, j: (i, 0)), b_spec, row, vec, row] + extra,
        out_specs=[row, row, vec] + extra,
        out_shape=[jax.ShapeDtypeStruct((m, d), F32), jax.ShapeDtypeStruct((m, d), BF16),
                   jax.ShapeDtypeStruct((1, d), F32)] + (cargo.out_shape() if cargo else []),
        scratch_shapes=[pltpu.VMEM((tm, d), F32)] + (cargo.scratch() if cargo else []),
        compiler_params=_params(("arbitrary", "arbitrary")),
    )(a, b, x, g, dh_in, *(cargo.srcs if cargo else []))
    return (outs[0], outs[1], outs[2], list(outs[3:])) if cargo else outs


def _sigmoid(x):
    return 1.0 / (1.0 + jnp.exp(-x))


SWIGLU_CHUNK = 256
SWIGLU_COLS = 1408


def _chunks(width):
    return [slice(lo, min(lo + SWIGLU_CHUNK, width)) for lo in range(0, width, SWIGLU_CHUNK)]


def _swiglu_fwd(n, wt_in, name, cargo=None):
    t = n.shape[0]
    tm = _pick(t, 1024)
    grid = (t // tm, D_FF // SWIGLU_COLS)

    def body(*refs):
        steps = [pl.program_id(0), pl.program_id(1)]

        def compute(own):
            n_ref, wa_ref, wb_ref, a_ref, b_ref, act_ref = own
            nv = n_ref[...]
            for cols in _chunks(SWIGLU_COLS):
                a = _dot(nv, wa_ref[cols, :], NT_DIMS)
                b = _dot(nv, wb_ref[cols, :], NT_DIMS)
                a_ref[:, cols] = a.astype(BF16)
                b_ref[:, cols] = b.astype(BF16)
                act_ref[:, cols] = (a * _sigmoid(a) * b).astype(BF16)

        _with_cargo(cargo, refs, 3, 3, steps, grid, compute)

    half = D_FF // SWIGLU_COLS
    tile = pl.BlockSpec((tm, SWIGLU_COLS), lambda i, j: (i, j))
    out = jax.ShapeDtypeStruct((t, D_FF), BF16)
    extra = cargo.specs() if cargo else []
    outs = pl.pallas_call(
        body, name=name, grid=grid,
        in_specs=[pl.BlockSpec((tm, D_MODEL), lambda i, j: (i, 0)),
                  pl.BlockSpec((SWIGLU_COLS, D_MODEL), lambda i, j: (j, 0)),
                  pl.BlockSpec((SWIGLU_COLS, D_MODEL), lambda i, j: (half + j, 0))] + extra,
        out_specs=[tile, tile, tile] + extra,
        out_shape=[out, out, out] + (cargo.out_shape() if cargo else []),
        scratch_shapes=cargo.scratch() if cargo else [],
        compiler_params=_params(("arbitrary", "arbitrary")),
    )(n, wt_in, wt_in, *(cargo.srcs if cargo else []))
    return (outs[0], outs[1], outs[2], list(outs[3:])) if cargo else outs


def _swiglu_bwd(dh, w_out, a, b, name):
    t = a.shape[0]
    tr = _row_tile(t, 512)

    def body(d_ref, w_ref, a_ref, b_ref, o_ref):
        dhv = d_ref[...]
        for cols in _chunks(D_FF):
            dv = _dot(dhv, w_ref[cols, :], NT_DIMS)
            av = a_ref[:, cols].astype(F32)
            s = _sigmoid(av)
            o_ref[:, cols] = (dv * b_ref[:, cols].astype(F32) * s * (1.0 + av * (1.0 - s))).astype(BF16)
            o_ref[:, slice(D_FF + cols.start, D_FF + cols.stop)] = (dv * av * s).astype(BF16)

    row = pl.BlockSpec((tr, D_FF), lambda i: (i, 0))
    return pl.pallas_call(
        body, name=name, grid=(t // tr,),
        in_specs=[pl.BlockSpec((tr, D_MODEL), lambda i: (i, 0)), pl.BlockSpec((D_FF, D_MODEL), lambda i: (0, 0)), row, row],
        out_specs=pl.BlockSpec((tr, 2 * D_FF), lambda i: (i, 0)),
        out_shape=jax.ShapeDtypeStruct((t, 2 * D_FF), BF16),
        compiler_params=_params(("parallel",)),
    )(dh, w_out, a, b)


def _latent_fwd(proj, gq, gkv):
    t = proj.shape[0]
    tr = _row_tile(t)

    def body(p_ref, gq_ref, gkv_ref, cq_ref, ckv_ref):
        cq = p_ref[:, P_CQ:P_CQ + Q_LORA].astype(F32)
        ckv = p_ref[:, P_CKV:P_CKV + KV_LORA].astype(F32)
        cq_ref[...] = ((cq * _rms(cq, Q_LORA)) * gq_ref[...]).astype(BF16)
        ckv_ref[...] = ((ckv * _rms(ckv, KV_LORA)) * gkv_ref[...]).astype(BF16)

    return pl.pallas_call(
        body, name="latent_fwd", grid=(t // tr,),
        in_specs=[pl.BlockSpec((tr, 1024), lambda i: (i, 0)), pl.BlockSpec((1, Q_LORA), lambda i: (0, 0)),
                  pl.BlockSpec((1, KV_LORA), lambda i: (0, 0))],
        out_specs=[pl.BlockSpec((tr, Q_LORA), lambda i: (i, 0)), pl.BlockSpec((tr, KV_LORA), lambda i: (i, 0))],
        out_shape=[jax.ShapeDtypeStruct((t, Q_LORA), BF16), jax.ShapeDtypeStruct((t, KV_LORA), BF16)],
        compiler_params=_params(("parallel",)),
    )(proj, gq, gkv)


def _latent_bwd(dcqn, dckvn, proj, dkr, gq, gkv):
    t = proj.shape[0]
    tr = _row_tile(t, 256)

    def norm_bwd(dn, x, g, width):
        r = _rms(x, width)
        y = x * r
        dy = dn * g
        dx = r * (dy - y * (jnp.sum(dy * y, axis=-1, keepdims=True) * (1.0 / width)))
        return dx, jnp.sum(dn * y, axis=0, keepdims=True)

    def body(dcq_ref, dckv_ref, p_ref, dkr_ref, gq_ref, gkv_ref, o_ref, dgq_ref, dgkv_ref):
        i = pl.program_id(0)
        dcq, pq = norm_bwd(dcq_ref[...], p_ref[:, P_CQ:P_CQ + Q_LORA].astype(F32), gq_ref[...], Q_LORA)
        dckv, pkv = norm_bwd(dckv_ref[...], p_ref[:, P_CKV:P_CKV + KV_LORA].astype(F32), gkv_ref[...], KV_LORA)
        o_ref[:, P_CQ:P_CQ + Q_LORA] = dcq.astype(BF16)
        o_ref[:, P_CKV:P_CKV + KV_LORA] = dckv.astype(BF16)
        o_ref[:, P_KR:P_KR + 128] = dkr_ref[...].astype(BF16)
        o_ref[:, P_KR + 128:1024] = jnp.zeros((tr, 1024 - P_KR - 128), BF16)

        @pl.when(i == 0)
        def _():
            dgq_ref[...] = pq
            dgkv_ref[...] = pkv

        @pl.when(i > 0)
        def _():
            dgq_ref[...] += pq
            dgkv_ref[...] += pkv

    def row(w):
        return pl.BlockSpec((tr, w), lambda i: (i, 0))

    def vec(w):
        return pl.BlockSpec((1, w), lambda i: (0, 0))

    return pl.pallas_call(
        body, name="latent_bwd", grid=(t // tr,),
        in_specs=[row(Q_LORA), row(KV_LORA), row(1024), row(128), vec(Q_LORA), vec(KV_LORA)],
        out_specs=[row(1024), vec(Q_LORA), vec(KV_LORA)],
        out_shape=[jax.ShapeDtypeStruct((t, 1024), BF16), jax.ShapeDtypeStruct((1, Q_LORA), F32),
                   jax.ShapeDtypeStruct((1, KV_LORA), F32)],
        compiler_params=_params(("arbitrary",)),
    )(dcqn, dckvn, proj, dkr, gq, gkv)


def _rope(y, cosf, sin_a, sin_b):
    return y * cosf + pltpu.roll(y, 112, 1) * sin_a + pltpu.roll(y, 16, 1) * sin_b


def _rope_t(d, cosf, sin_a, sin_b):
    return d * cosf + pltpu.roll(d * sin_a, 16, 1) + pltpu.roll(d * sin_b, 112, 1)


def _headprep_fwd(qraw, kvraw, proj, cosf, sin_a, sin_b, gqh, gkh):
    t = qraw.shape[0]
    tr = _row_tile(t, 256)

    def body(q_ref, kv_ref, kr_ref, c_ref, sa_ref, sb_ref, gq_ref, gk_ref, qh_ref, kh_ref, kvb_ref):
        cv, sa, sb = c_ref[...], sa_ref[...], sb_ref[...]
        kr = kr_ref[...].astype(F32)
        lane = lax.broadcasted_iota(jnp.int32, (tr, HEAD_PAD), 1)
        for h in range(N_HEADS):
            cols = slice(h * HEAD_PAD, (h + 1) * HEAD_PAD)
            xq = q_ref[:, cols]
            yq = (xq * _rms(xq, MLA_QK)) * gq_ref[...]
            qh_ref[:, cols] = (_rope(yq, cv, sa, sb) * MLA_Q_SCALE).astype(BF16)
            kvh = kv_ref[:, cols]
            kvb_ref[:, cols] = jnp.where(lane < MLA_NOPE, 1.0, kvh).astype(BF16)
            xk = jnp.where(lane < MLA_NOPE, kvh, kr)
            yk = (xk * _rms(xk, MLA_QK)) * gk_ref[...]
            kh_ref[:, cols] = _rope(yk, cv, sa, sb).astype(BF16)

    wide = pl.BlockSpec((tr, 1024), lambda i: (i, 0))
    lanes = pl.BlockSpec((tr, HEAD_PAD), lambda i: (i, 0))
    vec = pl.BlockSpec((1, HEAD_PAD), lambda i: (0, 0))
    return pl.pallas_call(
        body, name="headprep_fwd", grid=(t // tr,),
        in_specs=[wide, wide, pl.BlockSpec((tr, HEAD_PAD), lambda i: (i, P_KR // HEAD_PAD)), lanes, lanes, lanes, vec, vec],
        out_specs=[wide, wide, wide],
        out_shape=[jax.ShapeDtypeStruct((t, 1024), BF16)] * 3,
        compiler_params=_params(("parallel",)),
    )(qraw, kvraw, proj, cosf, sin_a, sin_b, gqh, gkh)


def _headprep_bwd(dqh, dkh, dvp, qraw, kvraw, proj, cosf, sin_a, sin_b, gqh, gkh):
    t = qraw.shape[0]
    tr = _row_tile(t, 256)

    def norm_bwd(dn, x, g):
        r = _rms(x, MLA_QK)
        y = x * r
        dy = dn * g
        dx = r * (dy - y * (jnp.sum(dy * y, axis=-1, keepdims=True) * (1.0 / MLA_QK)))
        return dx, jnp.sum(dn * y, axis=0, keepdims=True)

    def body(dq_ref, dk_ref, dv_ref, q_ref, kv_ref, kr_ref, c_ref, sa_ref, sb_ref, gq_ref, gk_ref,
             dqr_ref, dkvr_ref, dkr_ref, dgq_ref, dgk_ref):
        i = pl.program_id(0)
        cv, sa, sb = c_ref[...], sa_ref[...], sb_ref[...]
        kr = kr_ref[...].astype(F32)
        lane = lax.broadcasted_iota(jnp.int32, (tr, HEAD_PAD), 1)
        dkr = jnp.zeros((tr, HEAD_PAD), F32)
        pq = jnp.zeros((1, HEAD_PAD), F32)
        pk = jnp.zeros((1, HEAD_PAD), F32)
        for h in range(N_HEADS):
            cols = slice(h * HEAD_PAD, (h + 1) * HEAD_PAD)
            dxq, pqh = norm_bwd(_rope_t(dq_ref[:, cols], cv, sa, sb), q_ref[:, cols], gq_ref[...])
            dqr_ref[:, cols] = dxq.astype(BF16)
            pq = pq + pqh
            kvh = kv_ref[:, cols]
            xk = jnp.where(lane < MLA_NOPE, kvh, kr)
            dxk, pkh = norm_bwd(_rope_t(dk_ref[:, cols], cv, sa, sb), xk, gk_ref[...])
            pk = pk + pkh
            dkvr_ref[:, cols] = jnp.where(lane < MLA_NOPE, dxk, dv_ref[:, cols]).astype(BF16)
            dkr = dkr + jnp.where(lane < MLA_NOPE, 0.0, dxk)
        dkr_ref[...] = dkr

        @pl.when(i == 0)
        def _():
            dgq_ref[...] = pq
            dgk_ref[...] = pk

        @pl.when(i > 0)
        def _():
            dgq_ref[...] += pq
            dgk_ref[...] += pk

    wide = pl.BlockSpec((tr, 1024), lambda i: (i, 0))
    lanes = pl.BlockSpec((tr, HEAD_PAD), lambda i: (i, 0))
    vec = pl.BlockSpec((1, HEAD_PAD), lambda i: (0, 0))
    return pl.pallas_call(
        body, name="headprep_bwd", grid=(t // tr,),
        in_specs=[wide, wide, wide, wide, wide, pl.BlockSpec((tr, HEAD_PAD), lambda i: (i, P_KR // HEAD_PAD)),
                  lanes, lanes, lanes, vec, vec],
        out_specs=[wide, wide, lanes, vec, vec],
        out_shape=[jax.ShapeDtypeStruct((t, 1024), BF16), jax.ShapeDtypeStruct((t, 1024), BF16),
                   jax.ShapeDtypeStruct((t, HEAD_PAD), F32), jax.ShapeDtypeStruct((1, HEAD_PAD), F32),
                   jax.ShapeDtypeStruct((1, HEAD_PAD), F32)],
        compiler_params=_params(("arbitrary",)),
    )(dqh, dkh, dvp, qraw, kvraw, proj, cosf, sin_a, sin_b, gqh, gkh)


def _merge_fwd(proj, bm, bs):
    t = proj.shape[0]
    tr = _row_tile(t, 256)

    def body(gm_ref, gs_ref, bm_ref, bs_ref, o_ref):
        gm = _sigmoid(gm_ref[...].astype(F32))
        gs = _sigmoid(gs_ref[...].astype(F32))
        o_ref[...] = (gm * bm_ref[...] + gs * bs_ref[...]).astype(BF16)

    row = pl.BlockSpec((tr, 1024), lambda i: (i, 0))
    return pl.pallas_call(
        body, name="merge_fwd", grid=(t // tr,),
        in_specs=[pl.BlockSpec((tr, 1024), lambda i: (i, P_GM // 1024)),
                  pl.BlockSpec((tr, 1024), lambda i: (i, P_GS // 1024)), row, row],
        out_specs=row, out_shape=jax.ShapeDtypeStruct((t, 1024), BF16),
        compiler_params=_params(("parallel",)),
    )(proj, proj, bm, bs)


def _merge_bwd(dh, w_out, proj, bm, bs):
    t = proj.shape[0]
    tr = _row_tile(t, 512)

    def body(d_ref, w_ref, gm_ref, gs_ref, bm_ref, bs_ref, dbm_ref, dbs_ref, dg_ref):
        dhv = d_ref[...]
        for cols in _chunks(D_MODEL):
            dm = _dot(dhv, w_ref[cols, :], NT_DIMS)
            gm = _sigmoid(gm_ref[:, cols].astype(F32))
            gs = _sigmoid(gs_ref[:, cols].astype(F32))
            dbm_ref[:, cols] = (dm * gm).astype(BF16)
            dbs_ref[:, cols] = (dm * gs).astype(BF16)
            dg_ref[:, cols] = (dm * bm_ref[:, cols] * gm * (1.0 - gm)).astype(BF16)
            dg_ref[:, slice(D_MODEL + cols.start, D_MODEL + cols.stop)] = (dm * bs_ref[:, cols] * gs * (1.0 - gs)).astype(BF16)

    row = pl.BlockSpec((tr, 1024), lambda i: (i, 0))
    return pl.pallas_call(
        body, name="mix_out_dx", grid=(t // tr,),
        in_specs=[row, pl.BlockSpec((D_MODEL, D_MODEL), lambda i: (0, 0)),
                  pl.BlockSpec((tr, 1024), lambda i: (i, P_GM // 1024)),
                  pl.BlockSpec((tr, 1024), lambda i: (i, P_GS // 1024)), row, row],
        out_specs=[row, row, pl.BlockSpec((tr, 2048), lambda i: (i, 0))],
        out_shape=[jax.ShapeDtypeStruct((t, 1024), BF16), jax.ShapeDtypeStruct((t, 1024), BF16),
                   jax.ShapeDtypeStruct((t, 2048), BF16)],
        compiler_params=_params(("parallel",)),
    )(dh, w_out, proj, proj, bm, bs)


def _ple_loss(n3, w_gate, pb, w_proj, h3, tgt):
    t = h3.shape[0]
    tr = _row_tile(t, 512)

    def body(n_ref, wg_ref, p_ref, wp_ref, h_ref, t_ref, dh_ref, dz_ref, dp_ref, l_ref):
        i = pl.program_id(0)
        nv, pv = n_ref[...], p_ref[...]
        part = jnp.zeros((1, 128), F32)
        for cols in _chunks(D_MODEL):
            pg = _sigmoid(_dot(nv, wg_ref[:, cols], NN_DIMS))
            ppv = _dot(pv, wp_ref[:, cols], NN_DIMS)
            diff = (h_ref[:, cols] + pg * ppv) - t_ref[:, cols]
            dh = diff * (1.0 / D_MODEL)
            dh_ref[:, cols] = dh
            dp_ref[:, cols] = (dh * pg).astype(BF16)
            dz_ref[:, cols] = (dh * ppv * pg * (1.0 - pg)).astype(BF16)
            sq = jnp.sum(diff * diff, axis=0, keepdims=True)
            for c in range(sq.shape[1] // 128):
                part = part + sq[:, c * 128:(c + 1) * 128]

        @pl.when(i == 0)
        def _():
            l_ref[...] = part

        @pl.when(i > 0)
        def _():
            l_ref[...] += part

    row = pl.BlockSpec((tr, 1024), lambda i: (i, 0))
    return pl.pallas_call(
        body, name="ple_loss", grid=(t // tr,),
        in_specs=[row, pl.BlockSpec((D_MODEL, D_MODEL), lambda i: (0, 0)), pl.BlockSpec((tr, PLE_DIM), lambda i: (i, 0)),
                  pl.BlockSpec((PLE_DIM, D_MODEL), lambda i: (0, 0)), row, row],
        out_specs=[row, row, row, pl.BlockSpec((1, 128), lambda i: (0, 0))],
        out_shape=[jax.ShapeDtypeStruct((t, 1024), F32), jax.ShapeDtypeStruct((t, 1024), BF16),
                   jax.ShapeDtypeStruct((t, 1024), BF16), jax.ShapeDtypeStruct((1, 128), F32)],
        compiler_params=_params(("arbitrary",)),
    )(n3, w_gate, pb, w_proj, h3, tgt)


ATT_BLOCK = 256
MLA_Q_SCALE = math.log2(math.e) / math.sqrt(MLA_QK)
MLA_FWD_COLS = 4
MLA_BWD_COLS = 4
SB_FWD_COLS = 4
SB_BWD_COLS = 2
SB_BLOCK = 256


def _split_bf16(x):
    hi = x.astype(BF16)
    return hi, (x - hi.astype(F32)).astype(BF16)


def _tri(kind, n):
    r = lax.broadcasted_iota(jnp.int32, (n, n), 0)
    c = lax.broadcasted_iota(jnp.int32, (n, n), 1)
    cond = {'gt': r > c, 'le': r <= c, 'lt': r < c}[kind]
    return jnp.where(cond, 1.0, 0.0).astype(BF16)


def _causal(strict, n=ATT_BLOCK):
    r = lax.broadcasted_iota(jnp.int32, (n, n), 0)
    c = lax.broadcasted_iota(jnp.int32, (n, n), 1)
    return (c < r) if strict else (c <= r)


def _lanes(c):
    return slice(c * HEAD_PAD, (c + 1) * HEAD_PAD)


def _row_block(j, n=ATT_BLOCK):
    return pl.ds(pl.multiple_of(j * n, n), n)


def _rows(ref, j, c, n=ATT_BLOCK):
    return ref[_row_block(j, n), _lanes(c)]


def _mla_fwd(qh, kh, kvb, cargo=None):
    t = qh.shape[0]
    bq = ATT_BLOCK
    ncol = MLA_FWD_COLS
    grid = (N_HEADS // ncol, t // bq)

    def body(*refs):
        steps = [pl.program_id(0), pl.program_id(1)]
        _with_cargo(cargo, refs, 3, 2, steps, grid, lambda own: work(steps[1], *own))

    def work(i, q_ref, k_ref, v_ref, o_ref, lse_ref):
        qs = [q_ref[:, _lanes(c)] for c in range(ncol)]

        def step(j, carry, masked):
            cols = range(ncol)
            scores = [_dot(qs[c], _rows(k_ref, j, c), NT_DIMS) for c in cols]
            ms, ps, alphas = [], [], []
            for c in cols:
                s = scores[c]
                if masked:
                    s = jnp.where(_causal(False), s, -1e30)
                m_new = jnp.maximum(carry[c][0], jnp.max(s, axis=-1, keepdims=True))
                ps.append(jnp.exp2(s - m_new).astype(BF16))
                alphas.append(jnp.exp2(carry[c][0] - m_new))
                ms.append(m_new)
            return tuple((ms[c], alphas[c] * carry[c][1] + _dot(ps[c], _rows(v_ref, j, c), NN_DIMS)) for c in cols)

        init = tuple((jnp.full((bq, 1), -1e30, F32), jnp.zeros((bq, HEAD_PAD), F32)) for _ in range(ncol))
        carry = lax.fori_loop(0, i, lambda j, cr: step(j, cr, False), init)
        for c, (m, acc) in enumerate(step(i, carry, True)):
            l = acc[:, 0:1]
            o_ref[:, _lanes(c)] = (acc / l).astype(BF16)
            lse_ref[c] = m + jnp.log2(l)

    width = ncol * HEAD_PAD
    full = pl.BlockSpec((t, width), lambda h, i: (0, h))
    blk = pl.BlockSpec((bq, width), lambda h, i: (i, h))
    extra = cargo.specs() if cargo else []
    outs = pl.pallas_call(
        body, name="mla_fwd", grid=grid,
        in_specs=[blk, full, full] + extra,
        out_specs=[blk, pl.BlockSpec((ncol, bq, 1), lambda h, i: (h, i, 0))] + extra,
        out_shape=[jax.ShapeDtypeStruct((t, N_HEADS * HEAD_PAD), BF16), jax.ShapeDtypeStruct((N_HEADS, t, 1), F32)]
        + (cargo.out_shape() if cargo else []),
        scratch_shapes=cargo.scratch() if cargo else [],
        compiler_params=_params(("arbitrary", "arbitrary")),
    )(qh, kh, kvb, *(cargo.srcs if cargo else []))
    return (outs[0], outs[1], list(outs[2:])) if cargo else outs


def _mla_bwd(qh, kh, kvb, o, do, lse, cargo=None):
    t = qh.shape[0]
    bq = ATT_BLOCK
    ncol = MLA_BWD_COLS
    width = ncol * HEAD_PAD
    grid = (N_HEADS // ncol, t // bq)

    def body(*refs):
        steps = [pl.program_id(0), pl.program_id(1)]
        _with_cargo(cargo, refs, 6, 3, steps, grid, lambda own: work(steps[0], steps[1], *own))

    def work(h, i, q_ref, k_ref, v_ref, o_ref, do_ref, lse_ref, dq_ref, dk_hbm, dv_hbm, dk_ref, dv_ref, out_sems):

        @pl.when(i == 0)
        def _():
            dk_ref[...] = jnp.zeros_like(dk_ref)
            dv_ref[...] = jnp.zeros_like(dv_ref)

        qs = [q_ref[:, _lanes(c)] for c in range(ncol)]
        dos = [do_ref[:, _lanes(c)] for c in range(ncol)]
        deltas = [jnp.sum(dos[c].astype(F32) * o_ref[:, _lanes(c)].astype(F32), axis=-1, keepdims=True)
                  for c in range(ncol)]
        lses = [lse_ref[c] for c in range(ncol)]

        def step(j, dqs, masked):
            cols = range(ncol)
            kbs = [_rows(k_ref, j, c) for c in cols]
            scores = [_dot(qs[c], kbs[c], NT_DIMS) for c in cols]
            dps = [_dot(dos[c], _rows(v_ref, j, c), NT_DIMS) for c in cols]
            pbs, dss = [], []
            for c in cols:
                p = jnp.exp2(scores[c] - lses[c])
                if masked:
                    p = jnp.where(_causal(False), p, 0.0)
                pbs.append(p.astype(BF16))
                dss.append((p * (dps[c] - deltas[c])).astype(BF16))
            for c in cols:
                dv_ref[_row_block(j), _lanes(c)] += _dot(pbs[c], dos[c], TN_DIMS)
                dk_ref[_row_block(j), _lanes(c)] += _dot(dss[c], qs[c], TN_DIMS)
            return tuple(dqs[c] + _dot(dss[c], kbs[c], NN_DIMS) for c in cols)

        init = tuple(jnp.zeros((bq, HEAD_PAD), F32) for _ in range(ncol))
        dqs = lax.fori_loop(0, i, lambda j, cr: step(j, cr, False), init)
        for c, dq in enumerate(step(i, dqs, True)):
            dq_ref[:, _lanes(c)] = dq * (1.0 / math.sqrt(MLA_QK))

        @pl.when(i == grid[1] - 1)
        def _():
            dk_ref[...] = dk_ref[...] * math.log(2.0)
            cols = pl.ds(pl.multiple_of(h * width, width), width)
            out = [pltpu.make_async_copy(dk_ref, dk_hbm.at[:, cols], out_sems.at[0]),
                   pltpu.make_async_copy(dv_ref, dv_hbm.at[:, cols], out_sems.at[1])]
            for cp in out:
                cp.start()
            for cp in out:
                cp.wait()

    full = pl.BlockSpec((t, width), lambda h, i: (0, h))
    blk = pl.BlockSpec((bq, width), lambda h, i: (i, h))
    wide = jax.ShapeDtypeStruct((t, N_HEADS * HEAD_PAD), F32)
    extra = cargo.specs() if cargo else []
    outs = pl.pallas_call(
        body, name="mla_bwd", grid=grid,
        in_specs=[blk, full, full, blk, blk, pl.BlockSpec((ncol, bq, 1), lambda h, i: (h, i, 0))] + extra,
        out_specs=[blk, HBM_SPEC, HBM_SPEC] + extra,
        out_shape=[wide, wide, wide] + (cargo.out_shape() if cargo else []),
        scratch_shapes=[pltpu.VMEM((t, width), F32), pltpu.VMEM((t, width), F32), pltpu.SemaphoreType.DMA((2,))]
        + (cargo.scratch() if cargo else []),
        compiler_params=_params(("arbitrary", "arbitrary")),
    )(qh, kh, kvb, o, do, lse, *(cargo.srcs if cargo else []))
    return (outs[0], outs[1], outs[2], list(outs[3:])) if cargo else outs


def _head_only(x, lane, u):
    return jnp.where((lane >= u * SB_DIM) & (lane < (u + 1) * SB_DIM), x, jnp.zeros_like(x))


SB_DEAD = -104.0


def _log_sigmoids(z):
    e = jnp.exp(-jnp.abs(z))
    lg = jnp.log(1.0 + e)
    ls_pos = jnp.minimum(z, 0.0) - lg
    return ls_pos, ls_pos - z, e


def _sb_fwd(proj):
    t = proj.shape[0]
    bq, ncol = SB_BLOCK, SB_FWD_COLS
    nq = t // bq
    scale = 1.0 / math.sqrt(SB_DIM)
    pairs = SB_WIDTH // HEAD_PAD

    def body(q_ref, k_ref, v_ref, o_ref, r_ref, first_ref):
        g, i = pl.program_id(0), pl.program_id(1)
        lane = lax.broadcasted_iota(jnp.int32, (bq, HEAD_PAD), 1)
        upper = _tri('gt', bq)
        chains = [(c, u) for c in range(ncol) for u in range(2)]
        qms = [_head_only(q_ref[:, _lanes(c)], lane, u) * scale for c, u in chains]

        def step(j, carry, masked):
            ids = range(len(chains))
            zs = [_dot(qms[n], _rows(k_ref, j, chains[n][0], bq), NT_DIMS) for n in ids]
            pos, neg, parts = [], [], []
            for n in ids:
                ls_pos, ls_neg, _ = _log_sigmoids(zs[n])
                if masked:
                    ls_neg = jnp.where(_causal(True, bq), ls_neg, 0.0)
                pos.append(ls_pos)
                neg.append(ls_neg)
                parts.append(_split_bf16(ls_neg))
            suffix = [_dot(parts[n][0], upper, NN_DIMS) + _dot(parts[n][1], upper, NN_DIMS) for n in ids]
            weights = []
            for n in ids:
                a = jnp.exp(pos[n] + suffix[n] + carry[n][0])
                if masked:
                    a = jnp.where(_causal(True, bq), a, 0.0)
                weights.append(a.astype(BF16))
            return tuple((carry[n][0] + jnp.sum(neg[n], axis=-1, keepdims=True),
                          carry[n][1] + _dot(weights[n], _rows(v_ref, j, chains[n][0], bq), NN_DIMS)) for n in ids)

        init = tuple((jnp.zeros((bq, 1), F32), jnp.zeros((bq, HEAD_PAD), F32)) for _ in chains)
        carry = step(i, init, True)

        def more(state):
            s, cr = state
            live = cr[0][0]
            for n in range(1, len(chains)):
                live = jnp.maximum(live, cr[n][0])
            return jnp.logical_and(s < i, jnp.max(live) > SB_DEAD)

        walked, carry = lax.while_loop(more, lambda st: (st[0] + 1, step(i - 1 - st[0], st[1], False)),
                                       (jnp.int32(0), carry))
        first_ref[g * nq + i] = i - walked
        for n, (c, u) in enumerate(chains):
            r_ref[2 * c + u] = carry[n][0]
        for c in range(ncol):
            o_ref[:, _lanes(c)] = jnp.where(lane < SB_DIM, carry[2 * c][1], carry[2 * c + 1][1]).astype(BF16)

    width = ncol * HEAD_PAD

    def full(c0):
        return pl.BlockSpec((t, width), lambda g, i: (0, c0 // width + g))

    return pl.pallas_call(
        body, name="sb_fwd", grid=(pairs // ncol, t // bq),
        in_specs=[pl.BlockSpec((bq, width), lambda g, i: (i, P_SBQ // width + g)), full(P_SBK), full(P_SBV)],
        out_specs=[pl.BlockSpec((bq, width), lambda g, i: (i, g)),
                   pl.BlockSpec((2 * ncol, bq, 1), lambda g, i: (g, i, 0)),
                   pl.BlockSpec(memory_space=pltpu.SMEM)],
        out_shape=[jax.ShapeDtypeStruct((t, SB_WIDTH), BF16), jax.ShapeDtypeStruct((N_HEADS, t, 1), F32),
                   jax.ShapeDtypeStruct((pairs // ncol * nq,), jnp.int32)],
        compiler_params=_params(("arbitrary", "arbitrary")),
    )(proj, proj, proj)


def _sb_bwd(proj, do, rtot, first):
    t = proj.shape[0]
    bq, ncol = SB_BLOCK, SB_BWD_COLS
    nq = t // bq
    scale = 1.0 / math.sqrt(SB_DIM)
    pairs = SB_WIDTH // HEAD_PAD

    def body(first_ref, q_ref, k_ref, v_ref, do_ref, r_ref, dq_ref, dk_ref, dv_ref):
        g, i = pl.program_id(0), pl.program_id(1)

        @pl.when(i == 0)
        def _():
            dk_ref[...] = jnp.zeros_like(dk_ref)
            dv_ref[...] = jnp.zeros_like(dv_ref)

        lane = lax.broadcasted_iota(jnp.int32, (bq, HEAD_PAD), 1)
        incl = _tri('le', bq)
        excl = _tri('lt', bq)
        chains = [(c, u) for c in range(ncol) for u in range(2)]
        qms = [_head_only(q_ref[:, _lanes(c)], lane, u) * scale for c, u in chains]
        doms = [_head_only(do_ref[:, _lanes(c)], lane, u) for c, u in chains]
        rts = [r_ref[2 * c + u] for c, u in chains]

        def step(j, carry, masked):
            ids = range(len(chains))
            kbs = [_rows(k_ref, j, c, bq) for c in range(ncol)]
            zs =[_dot(qms[n], kbs[chains[n][0]], NT_DIMS) for n in ids]
            das = [_dot(doms[n], _rows(v_ref, j, chains[n][0], bq), NT_DIMS) for n in ids]
            pos, neg, sigs, parts = [], [], [], []
            for n in ids:
                ls_pos, ls_neg, e = _log_sigmoids(zs[n])
                if masked:
                    ls_neg = jnp.where(_causal(True, bq), ls_neg, 0.0)
                pos.append(ls_pos)
                neg.append(ls_neg)
                sigs.append(jnp.where(zs[n] >= 0.0, 1.0, e) * pl.reciprocal(1.0 + e, approx=True))
                parts.append(_split_bf16(ls_neg))
            prefix = [_dot(parts[n][0], incl, NN_DIMS) + _dot(parts[n][1], incl, NN_DIMS) for n in ids]
            evs, eparts, dvs = [], [], []
            for n in ids:
                a = jnp.exp(pos[n] + (rts[n] - (carry[n][0] + prefix[n])))
                if masked:
                    a = jnp.where(_causal(True, bq), a, 0.0)
                dvs.append(_dot(a.astype(BF16), doms[n], TN_DIMS))
                evs.append(a * das[n])
                eparts.append(evs[n].astype(BF16))
            before = [_dot(eparts[n], excl, NN_DIMS) for n in ids]
            out, dks = [], []
            for n in ids:
                dz = evs[n] - sigs[n] * (evs[n] + (carry[n][1] + before[n]))
                if masked:
                    dz = jnp.where(_causal(True, bq), dz, 0.0)
                dzb = dz.astype(BF16)
                dks.append(_dot(dzb, qms[n], TN_DIMS))
                out.append((carry[n][0] + jnp.sum(neg[n], axis=-1, keepdims=True),
                            carry[n][1] + jnp.sum(evs[n], axis=-1, keepdims=True),
                            carry[n][2] + _dot(dzb, kbs[chains[n][0]], NN_DIMS)))
            for c in range(ncol):
                dv_ref[_row_block(j, bq), _lanes(c)] += dvs[2 * c] + dvs[2 * c + 1]
                dk_ref[_row_block(j, bq), _lanes(c)] += dks[2 * c] + dks[2 * c + 1]
            return tuple(out)

        init = tuple((jnp.zeros((bq, 1), F32), jnp.zeros((bq, 1), F32), jnp.zeros((bq, HEAD_PAD), F32)) for _ in chains)
        start = first_ref[(g * ncol // SB_FWD_COLS) * nq + i]
        carry = lax.fori_loop(start, i, lambda j, cr: step(j, cr, False), init)
        carry = step(i, carry, True)
        for c in range(ncol):
            dq_ref[:, _lanes(c)] = jnp.where(lane < SB_DIM, carry[2 * c][2], carry[2 * c + 1][2]) * scale

    width = ncol * HEAD_PAD

    def full(c0):
        return pl.BlockSpec((t, width), lambda g, i, first: (0, c0 // width + g))

    blk = pl.BlockSpec((bq, width), lambda g, i, first: (i, g))
    acc = pl.BlockSpec((t, width), lambda g, i, first: (0, g))
    wide = jax.ShapeDtypeStruct((t, SB_WIDTH), F32)
    return pl.pallas_call(
        body, name="sb_bwd",
        grid_spec=pltpu.PrefetchScalarGridSpec(
            num_scalar_prefetch=1, grid=(pairs // ncol, nq),
            in_specs=[pl.BlockSpec((bq, width), lambda g, i, first: (i, P_SBQ // width + g)), full(P_SBK), full(P_SBV),
                      blk, pl.BlockSpec((2 * ncol, bq, 1), lambda g, i, first: (g, i, 0))],
            out_specs=[blk, acc, acc]),
        out_shape=[wide, wide, wide],
        compiler_params=_params(("arbitrary", "arbitrary")),
    )(first, proj, proj, proj, do, rtot)


def _cols_to_full(g):
    n, r, c = g.shape
    return jnp.transpose(g, (1, 0, 2)).reshape(r, n * c)


def _full_to_cols(w):
    r, c = w.shape
    return jnp.transpose(w.reshape(r, N_DEV, c // N_DEV), (1, 0, 2))


TRANSPOSED = ('ffn1_w_in', 'ffn2_w_in', 'w_in', 'w_q_up')


def _layout_weight(name, g):
    if name in ('ffn1_w_out', 'ffn2_w_out', 'w_out', 'w_ple_gate', 'ffn1_w_in', 'ffn2_w_in'):
        return g.reshape(g.shape[0] * g.shape[1], g.shape[2])
    if name == 'w_in':
        wt = g.reshape(IN_COLS, D_MODEL)
        z = lambda n: jnp.zeros((n, D_MODEL), BF16)
        return jnp.concatenate([wt[0:640], z(64), wt[640:672], z(32), z(256), wt[2208:4256], wt[672:2208]], axis=0)
    if name == 'w_q_up':
        return jnp.pad(g, ((0, 0), (0, HEAD_PAD - MLA_QK), (0, 0))).reshape(N_HEADS * HEAD_PAD, Q_LORA)
    if name == 'w_branch_mla':
        bm = _cols_to_full(g).reshape(N_HEADS, MLA_NOPE, D_MODEL)
        return jnp.pad(bm, ((0, 0), (HEAD_PAD - MLA_NOPE, 0), (0, 0))).reshape(N_HEADS * HEAD_PAD, D_MODEL)
    return _cols_to_full(g)


def _layout_weights(g):
    return {n: _layout_weight(n, a) for n, a in g.items()}


def _unlayout_grad(name, d):
    if name == 'w_in':
        d = jnp.concatenate([d[0:640], d[704:736], d[P_SBQ:PROJ_W], d[P_GM:P_SBQ]], axis=0)
    if name == 'w_q_up':
        return d.reshape(N_HEADS, HEAD_PAD, Q_LORA)[:, :MLA_QK, :]
    if name in ('ffn1_w_out', 'ffn2_w_out', 'w_out', 'w_ple_gate', 'ffn1_w_in', 'ffn2_w_in', 'w_in'):
        return d.reshape(N_DEV, d.shape[0] // N_DEV, d.shape[1])
    if name == 'w_branch_mla':
        d = d.reshape(N_HEADS, HEAD_PAD, D_MODEL)[:, HEAD_PAD - MLA_NOPE:, :].reshape(SB_WIDTH, D_MODEL)
    return _full_to_cols(d)


def _unlayout_grads(d):
    return {n: _unlayout_grad(n, a) for n, a in d.items()}


def _rope_tables(positions):
    half = MLA_ROPE // 2
    inv_freq = ROPE_BASE ** (-jnp.arange(0, MLA_ROPE, 2, dtype=F32) / MLA_ROPE)
    ang = positions.astype(F32)[:, None] * inv_freq
    cos, sin = jnp.cos(ang), jnp.sin(ang)
    t = positions.shape[0]
    ones = lambda n: jnp.ones((t, n), F32)
    zeros = lambda n: jnp.zeros((t, n), F32)
    cosf = jnp.concatenate([ones(MLA_NOPE), cos, cos, ones(HEAD_PAD - MLA_QK)], axis=1)
    sin_a = jnp.concatenate([zeros(MLA_NOPE), -sin, zeros(half), zeros(HEAD_PAD - MLA_QK)], axis=1)
    sin_b = jnp.concatenate([zeros(MLA_NOPE), zeros(half), sin, zeros(HEAD_PAD - MLA_QK)], axis=1)
    return cosf, sin_a, sin_b


def _local_step(x, p, positions, tgt, norms, plan):
    mm = _matmul
    cosf, sin_a, sin_b = _rope_tables(positions)
    pad_head = lambda g: jnp.pad(g, ((0, 0), (0, HEAD_PAD - MLA_QK)))
    gqh, gkh = pad_head(norms['q_head_norm']), pad_head(norms['k_head_norm'])
    pb = p.astype(BF16)
    w = dict(plan.first_weights())
    dw, dn = {}, {}

    def ride(host, call):
        cargo = plan.cargo(host, dw)
        res, lands = call(cargo), None
        if cargo is not None:
            *res, lands = res
            res = res[0] if len(res) == 1 else tuple(res)
        w.update(plan.landed(host, lands))
        return res

    def ffn_fwd(h, tag):
        n = _rmsnorm_fwd(h, norms[tag + '_norm'], tag + "_norm_fwd")
        a, b, act = ride(tag + "_in_fwd", lambda cargo: _swiglu_fwd(n, w[tag + '_w_in'], tag + "_in_fwd", cargo))
        out = ride(tag + "_out_fwd", lambda cargo: mm(
            act, w[tag + '_w_out'], mode='nn', out_dtype=F32, name=tag + "_out_fwd", res=h, alpha=0.5, cargo=cargo))
        return out, (n, a, b, act)

    h1, ffn1_saved = ffn_fwd(x, 'ffn1')
    u = _rmsnorm_fwd(h1, norms['mix_norm'], "mix_norm_fwd")
    proj = mm(u, w['w_in'], mode='nt', out_dtype=BF16, name="proj_fwd")
    cqn, ckvn = _latent_fwd(proj, norms['q_latent_norm'], norms['kv_latent_norm'])
    qraw = mm(cqn, w['w_q_up'], mode='nt', out_dtype=F32, name="q_up_fwd")
    kvraw = mm(ckvn, w['w_kv_up'], mode='nn', out_dtype=F32, name="kv_up_fwd")
    qh, kh, kvb = _headprep_fwd(qraw, kvraw, proj, cosf, sin_a, sin_b, gqh, gkh)
    o_mla, lse = ride("mla_fwd", lambda cargo: _mla_fwd(qh, kh, kvb, cargo))
    o_sb, rtot, sb_first = _sb_fwd(proj)
    bm = mm(o_mla, w['w_branch_mla'], mode='nn', out_dtype=F32, name="branch_mla_fwd")
    bs = mm(o_sb, w['w_branch_sb'], mode='nn', out_dtype=F32, name="branch_sb_fwd")
    merged = _merge_fwd(proj, bm, bs)
    h2 = mm(merged, w['w_out'], mode='nn', out_dtype=F32, name="mix_out_fwd", res=h1)
    h3, ffn2_saved = ffn_fwd(h2, 'ffn2')
    n3 = _rmsnorm_fwd(h3, norms['ple_norm'], "ple_norm_fwd")
    dh4, dzg, dpp, loss_lanes = _ple_loss(n3, w['w_ple_gate'], pb, w['w_ple_proj'], h3, tgt)

    dw['w_ple_gate'] = mm(n3, dzg, mode='tn', out_dtype=BF16, name="ple_gate_dw")
    dw['w_ple_proj'] = mm(pb, dpp, mode='tn', out_dtype=BF16, name="ple_proj_dw")
    dh3, dhb3, dn['ple_norm'] = _matmul_norm_bwd(
        dzg, w['w_ple_gate'], h3, norms['ple_norm'], dh4, mode='nt', name="ple_gate_dx", out_scale=0.5)

    def ffn_bwd(h, dh, dhb, saved, tag, out_scale):
        n, a, b, act = saved
        dw[tag + '_w_out'] = mm(act, dhb, mode='tn', out_dtype=BF16, name=tag + "_out_dw", tm=1408)
        dab = _swiglu_bwd(dhb, w[tag + '_w_out'], a, b, tag + "_out_dx")
        dw[tag + '_w_in'] = ride(tag + "_in_dw", lambda cargo: mm(
            dab, n, mode='tn', out_dtype=BF16, name=tag + "_in_dw", tm=1408, cargo=cargo))
        dh_prev, dhb_prev, dn[tag + '_norm'] = ride(tag + "_in_dx", lambda cargo: _matmul_norm_bwd(
            dab, w[tag + '_w_in'], h, norms[tag + '_norm'], dh, mode='nn', name=tag + "_in_dx", out_scale=out_scale,
            cargo=cargo))
        return dh_prev, dhb_prev

    dh2, dhb2 = ffn_bwd(h2, dh3, dhb3, ffn2_saved, 'ffn2', 1.0)
    dw['w_out'] = mm(merged, dhb2, mode='tn', out_dtype=BF16, name="mix_out_dw")
    dbm, dbs, dgates = _merge_bwd(dhb2, w['w_out'], proj, bm, bs)
    dw['w_branch_mla'] = mm(o_mla, dbm, mode='tn', out_dtype=BF16, name="branch_mla_dw")
    dw['w_branch_sb'] = mm(o_sb, dbs, mode='tn', out_dtype=BF16, name="branch_sb_dw")
    do_mla = mm(dbm, w['w_branch_mla'], mode='nt', out_dtype=BF16, name="branch_mla_dx")
    do_sb = mm(dbs, w['w_branch_sb'], mode='nt', out_dtype=BF16, name="branch_sb_dx")
    dqh, dkh, dvp = ride("mla_bwd", lambda cargo: _mla_bwd(qh, kh, kvb, o_mla, do_mla, lse, cargo))
    dsq, dsk, dsv = _sb_bwd(proj, do_sb, rtot, sb_first)
    dqraw, dkvraw, dkr, dgq, dgk = _headprep_bwd(dqh, dkh, dvp, qraw, kvraw, proj, cosf, sin_a, sin_b, gqh, gkh)
    dn['q_head_norm'], dn['k_head_norm'] = dgq[:, :MLA_QK], dgk[:, :MLA_QK]
    dw['w_q_up'] = mm(dqraw, cqn, mode='tn', out_dtype=BF16, name="q_up_dw")
    dw['w_kv_up'] = mm(ckvn, dkvraw, mode='tn', out_dtype=BF16, name="kv_up_dw")
    dcqn = mm(dqraw, w['w_q_up'], mode='nn', out_dtype=F32, name="q_up_dx")
    dckvn = mm(dkvraw, w['w_kv_up'], mode='nt', out_dtype=F32, name="kv_up_dx")
    dlat, dn['q_latent_norm'], dn['kv_latent_norm'] = _latent_bwd(
        dcqn, dckvn, proj, dkr, norms['q_latent_norm'], norms['kv_latent_norm'])
    dproj = jnp.concatenate([dlat, dgates, dsq.astype(BF16), dsk.astype(BF16), dsv.astype(BF16)], axis=1)
    dw['w_in'] = ride("proj_dw", lambda cargo: mm(dproj, u, mode='tn', out_dtype=BF16, name="proj_dw", tm=1536, cargo=cargo))
    dh1, dhb1, dn['mix_norm'] = ride("proj_dx", lambda cargo: _matmul_norm_bwd(
        dproj, w['w_in'], h1, norms['mix_norm'], dh2, mode='nn', name="proj_dx", out_scale=0.5, cargo=cargo))
    dx, _ = ffn_bwd(x, dh1, dhb1, ffn1_saved, 'ffn1', 1.0)
    return dx, loss_lanes, dw, dn


MESH = pl.DeviceIdType.MESH
HBM_SPEC = pl.BlockSpec(memory_space=pl.ANY)


def _position():
    return lax.axis_index("x"), lax.axis_index("y"), lax.axis_index("c")


def _index(px, py, pc):
    return 4 * px + 2 * py + pc


def _all_gather(shards):
    n = len(shards)

    def body(*refs):
        ins, outs = refs[:n], refs[n:2 * n]
        send_sems, recv_sems, local_sems = refs[2 * n:]
        x, y, c = _position()
        me, sibling = (x, y, c), (x, y, 1 - c)
        chips = [(1 - x, y), (x, 1 - y), (1 - x, 1 - y)]

        def copy(a, k, block, to, own=False):
            dst = outs[a].at[_index(*block)]
            return pltpu.make_async_remote_copy(
                src_ref=ins[a] if own else dst, dst_ref=dst,
                send_sem=send_sems.at[a, k], recv_sem=recv_sems.at[a, k], device_id=to, device_id_type=MESH)

        mine = [pltpu.make_async_copy(ins[a], outs[a].at[_index(*me)], local_sems.at[a]) for a in range(n)]
        for cp in mine:
            cp.start()
        first = []
        for a in range(n):
            first.append(copy(a, 0, me, sibling, own=True))
            first += [copy(a, 1 + j, me, (*chip, c), own=True) for j, chip in enumerate(chips)]
        for cp in first:
            cp.start()
        passed = []
        for j, chip in enumerate(chips):
            for a in range(n):
                copy(a, 1 + j, (*chip, c), me).wait_recv()
                fwd = copy(a, 4 + j, (*chip, c), sibling)
                fwd.start()
                passed.append(fwd)
        for a in range(n):
            copy(a, 0, sibling, me).wait_recv()
            for j, chip in enumerate(chips):
                copy(a, 4 + j, (*chip, 1 - c), me).wait_recv()
        for cp in first + passed:
            cp.wait_send()
        for cp in mine:
            cp.wait()

    return pl.pallas_call(
        body, name="weights_all_gather",
        in_specs=[HBM_SPEC] * n, out_specs=[HBM_SPEC] * n,
        out_shape=[jax.ShapeDtypeStruct((N_DEV,) + s.shape, s.dtype) for s in shards],
        scratch_shapes=[pltpu.SemaphoreType.DMA((n, 7)), pltpu.SemaphoreType.DMA((n, 7)), pltpu.SemaphoreType.DMA((n,))],
    )(*shards)


def _exchange(parts):
    n = len(parts)
    masks = [(mx, my, mc) for mx in (0, 1) for my in (0, 1) for mc in (0, 1)][1:]

    def body(*refs):
        ins, outs = refs[:n], refs[n:2 * n]
        send_sems, recv_sems, local_sems = refs[2 * n:]
        x, y, c = _position()
        me = _index(x, y, c)

        def peer_of(mask):
            mx, my, mc = mask
            return (x + mx - 2 * x * mx, y + my - 2 * y * my, c + mc - 2 * c * mc)

        def copy(a, k):
            peer = peer_of(masks[k])
            return pltpu.make_async_remote_copy(
                src_ref=ins[a].at[_index(*peer)], dst_ref=outs[a].at[me],
                send_sem=send_sems.at[a, k], recv_sem=recv_sems.at[a, k], device_id=peer, device_id_type=MESH)

        def landed(a, k):
            peer = peer_of(masks[k])
            return pltpu.make_async_remote_copy(
                src_ref=ins[a].at[me], dst_ref=outs[a].at[_index(*peer)],
                send_sem=send_sems.at[a, k], recv_sem=recv_sems.at[a, k], device_id=peer, device_id_type=MESH)

        mine = [pltpu.make_async_copy(ins[a].at[me], outs[a].at[me], local_sems.at[a]) for a in range(n)]
        for cp in mine:
            cp.start()
        sent = [copy(a, k) for k in range(7) for a in range(n)]
        for cp in sent:
            cp.start()
        for k in range(7):
            for a in range(n):
                landed(a, k).wait_recv()
        for cp in sent:
            cp.wait_send()
        for cp in mine:
            cp.wait()

    return pl.pallas_call(
        body, name="grads_exchange",
        in_specs=[HBM_SPEC] * n, out_specs=[HBM_SPEC] * n,
        out_shape=[jax.ShapeDtypeStruct(s.shape, s.dtype) for s in parts],
        scratch_shapes=[pltpu.SemaphoreType.DMA((n, 7)), pltpu.SemaphoreType.DMA((n, 7)), pltpu.SemaphoreType.DMA((n,))],
    )(*parts)


PEER_MASKS = [(mx, my, mc) for mx in (0, 1) for my in (0, 1) for mc in (0, 1)][1:]


def _peer(mask):
    x, y, c = _position()
    mx, my, mc = mask
    return (x + mx - 2 * x * mx, y + my - 2 * y * my, c + mc - 2 * c * mc)


class _Cargo:
    def __init__(self, srcs, scatter):
        self.srcs, self.scatter, self.n = list(srcs), scatter, len(srcs)

    def specs(self):
        return [HBM_SPEC] * self.n

    def out_shape(self):
        return [jax.ShapeDtypeStruct(s.shape if self.scatter else (N_DEV,) + s.shape, s.dtype) for s in self.srcs]

    def scratch(self):
        per_copy = pltpu.SemaphoreType.DMA((self.n, len(PEER_MASKS)))
        return [per_copy, per_copy, pltpu.SemaphoreType.DMA((self.n,))]

    def _mine(self, src_refs, a, to):
        return src_refs[a].at[to] if self.scatter else src_refs[a]

    def _shard_copy(self, src_refs, land_refs, sems, a, k, block, to, own=False):
        dst = land_refs[a].at[_index(*block)]
        return pltpu.make_async_remote_copy(
            src_ref=src_refs[a] if own else dst, dst_ref=dst,
            send_sem=sems[0].at[a, k], recv_sem=sems[1].at[a, k], device_id=to, device_id_type=MESH)

    def _first_hops(self, src_refs, land_refs, sems):
        x, y, c = _position()
        chips = [(1 - x, y), (x, 1 - y), (1 - x, 1 - y)]
        hops = []
        for a in range(self.n):
            hops.append(self._shard_copy(src_refs, land_refs, sems, a, 0, (x, y, c), (x, y, 1 - c), own=True))
            hops += [self._shard_copy(src_refs, land_refs, sems, a, 1 + j, (x, y, c), (*chip, c), own=True)
                     for j, chip in enumerate(chips)]
        return hops, chips

    def start(self, src_refs, land_refs, sems):
        send, recv, local = sems
        me = _index(*_position())
        for a in range(self.n):
            pltpu.make_async_copy(self._mine(src_refs, a, me), land_refs[a].at[me], local.at[a]).start()
        if not self.scatter:
            for cp in self._first_hops(src_refs, land_refs, sems)[0]:
                cp.start()
            return
        for k, mask in enumerate(PEER_MASKS):
            peer = _peer(mask)
            for a in range(self.n):
                pltpu.make_async_remote_copy(
                    src_ref=self._mine(src_refs, a, _index(*peer)), dst_ref=land_refs[a].at[me],
                    send_sem=send.at[a, k], recv_sem=recv.at[a, k], device_id=peer, device_id_type=MESH).start()

    def _wait_gathered(self, src_refs, land_refs, sems):
        x, y, c = _position()
        me, sibling = (x, y, c), (x, y, 1 - c)
        first, chips = self._first_hops(src_refs, land_refs, sems)
        passed = []
        for j, chip in enumerate(chips):
            for a in range(self.n):
                self._shard_copy(src_refs, land_refs, sems, a, 1 + j, (*chip, c), me).wait_recv()
                passed.append(self._shard_copy(src_refs, land_refs, sems, a, 4 + j, (*chip, c), sibling))
                passed[-1].start()
        for a in range(self.n):
            self._shard_copy(src_refs, land_refs, sems, a, 0, sibling, me).wait_recv()
            for j, chip in enumerate(chips):
                self._shard_copy(src_refs, land_refs, sems, a, 4 + j, (*chip, 1 - c), me).wait_recv()
        for cp in first + passed:
            cp.wait_send()

    def wait(self, src_refs, land_refs, sems):
        send, recv, local = sems
        me = _index(*_position())
        if not self.scatter:
            self._wait_gathered(src_refs, land_refs, sems)
        for k, mask in enumerate(PEER_MASKS if self.scatter else []):
            peer = _peer(mask)
            there = _index(*peer)
            for a in range(self.n):
                pltpu.make_async_remote_copy(
                    src_ref=self._mine(src_refs, a, me), dst_ref=land_refs[a].at[there],
                    send_sem=send.at[a, k], recv_sem=recv.at[a, k], device_id=peer, device_id_type=MESH).wait_recv()
                pltpu.make_async_remote_copy(
                    src_ref=self._mine(src_refs, a, there), dst_ref=land_refs[a].at[me],
                    send_sem=send.at[a, k], recv_sem=recv.at[a, k], device_id=peer, device_id_type=MESH).wait_send()
        for a in range(self.n):
            pltpu.make_async_copy(self._mine(src_refs, a, me), land_refs[a].at[me], local.at[a]).wait()


def _with_cargo(cargo, refs, n_in, n_out, steps, counts, compute):
    if cargo is None:
        compute(refs)
        return
    n = cargo.n
    src_refs = refs[n_in:n_in + n]
    land_refs = refs[n_in + n + n_out:n_in + 2 * n + n_out]
    sems = refs[-3:]
    first = functools.reduce(jnp.logical_and, [s == 0 for s in steps])
    last = functools.reduce(jnp.logical_and, [s == c - 1 for s, c in zip(steps, counts)])

    @pl.when(first)
    def _():
        cargo.start(src_refs, land_refs, sems)

    compute(refs[:n_in] + refs[n_in + n:n_in + n + n_out] + refs[n_in + 2 * n + n_out:-3])

    @pl.when(last)
    def _():
        cargo.wait(src_refs, land_refs, sems)


def _adamw(parts, w, m, v, name):
    r, c = w.shape
    tr = next((t for t in (512, 384, 352, 256, 128) if r % t == 0), r) if r > 512 else r
    tc = c if tr < r or r <= 512 else 256
    assert r % tr == 0 and c % tc == 0
    bc1 = 1.0 - ADAM_B1 ** ADAM_STEP
    bc2 = 1.0 - ADAM_B2 ** ADAM_STEP

    def body(p_ref, w_ref, m_ref, v_ref, g_ref, d_ref, nm_ref, nv_ref):
        g = p_ref[0].astype(F32)
        for s in range(1, N_DEV):
            g = g + p_ref[s].astype(F32)
        nm = ADAM_B1 * m_ref[...] + (1.0 - ADAM_B1) * g
        nv = ADAM_B2 * v_ref[...] + (1.0 - ADAM_B2) * (g * g)
        g_ref[...] = g
        nm_ref[...] = nm
        nv_ref[...] = nv
        d_ref[...] = -ADAM_LR * ((nm / bc1) / (jnp.sqrt(nv / bc2) + ADAM_EPS) + ADAM_WD * w_ref[...])

    tile = pl.BlockSpec((tr, tc), lambda i, j: (i, j))
    out = jax.ShapeDtypeStruct((r, c), F32)
    return pl.pallas_call(
        body, name=name, grid=(r // tr, c // tc),
        in_specs=[pl.BlockSpec((N_DEV, tr, tc), lambda i, j: (0, i, j)), tile, tile, tile],
        out_specs=[tile] * 4, out_shape=[out] * 4,
        compiler_params=_params(("parallel", "parallel")),
    )(parts, w, m, v)


GATHER_FIRST = ['ffn1_w_in']
RIDES = {
    'ffn1_in_fwd': ('weights', ['ffn1_w_out', 'w_in']),
    'ffn1_out_fwd': ('weights', ['w_q_up', 'w_kv_up', 'w_branch_mla', 'w_branch_sb', 'w_out']),
    'mla_fwd': ('weights', ['ffn2_w_in', 'ffn2_w_out', 'w_ple_gate', 'w_ple_proj']),
    'mla_bwd': ('grads', ['w_ple_gate', 'w_ple_proj', 'ffn2_w_out', 'ffn2_w_in', 'w_out', 'w_branch_mla', 'w_branch_sb']),
    'proj_dw': ('grads', ['w_q_up', 'w_kv_up']),
    'proj_dx': ('grads', ['w_in']),
    'ffn1_in_dw': ('grads', ['ffn1_w_out']),
    'ffn1_in_dx': ('grads', ['ffn1_w_in']),
}


class _Plan:
    def __init__(self, shards):
        self.shards = shards
        self.received = {}

    def first_weights(self):
        gathered = _all_gather([self.shards[n] for n in GATHER_FIRST])
        return {n: _layout_weight(n, g) for n, g in zip(GATHER_FIRST, gathered)}

    def cargo(self, host, dw):
        if host not in RIDES:
            return None
        kind, names = RIDES[host]
        if kind == 'weights':
            return _Cargo([self.shards[n] for n in names], False)
        return _Cargo([_unlayout_grad(n, dw.pop(n)) for n in names], True)

    def landed(self, host, lands):
        if host not in RIDES:
            return {}
        kind, names = RIDES[host]
        if kind == 'weights':
            return {n: _layout_weight(n, land) for n, land in zip(names, lands)}
        self.received.update(zip(names, lands))
        return {}


def _pack_small(vecs):
    flat = jnp.concatenate([v.reshape(-1) for v in vecs])
    return jnp.pad(flat, (0, SMALL_ROWS * 128 - flat.shape[0])).reshape(SMALL_ROWS, 128)


def _unpack_small(packed, sizes):
    flat = packed.reshape(-1)
    out, at = [], 0
    for n in sizes:
        out.append(flat[at:at + n])
        at += n
    return out


def kernel(x, p, positions, ffn1_norm, ffn1_w_in, ffn1_w_out, mix_norm, w_in, q_latent_norm, w_q_up, kv_latent_norm, w_kv_up, q_head_norm, k_head_norm, w_branch_mla, w_branch_sb, w_out, ffn2_norm, ffn2_w_in, ffn2_w_out, ple_norm, w_ple_gate, w_ple_proj, loss_target, m_ffn1_norm, m_ffn1_w_in, m_ffn1_w_out, m_mix_norm, m_w_in, m_q_latent_norm, m_w_q_up, m_kv_latent_norm, m_w_kv_up, m_q_head_norm, m_k_head_norm, m_w_branch_mla, m_w_branch_sb, m_w_out, m_ffn2_norm, m_ffn2_w_in, m_ffn2_w_out, m_ple_norm, m_w_ple_gate, m_w_ple_proj, v_ffn1_norm, v_ffn1_w_in, v_ffn1_w_out, v_mix_norm, v_w_in, v_q_latent_norm, v_w_q_up, v_kv_latent_norm, v_w_kv_up, v_q_head_norm, v_k_head_norm, v_w_branch_mla, v_w_branch_sb, v_w_out, v_ffn2_norm, v_ffn2_w_in, v_ffn2_w_out, v_ple_norm, v_w_ple_gate, v_w_ple_proj):
    given = dict(locals())
    wts = {n: given[n] for n in WEIGHTS}
    mom = {n: given['m_' + n] for n in WEIGHTS}
    var = {n: given['v_' + n] for n in WEIGHTS}

    def local(a, n):
        return jnp.swapaxes(a[0], 0, 1) if n in TRANSPOSED else a[0]

    plan = _Plan({n: local(wts[n], n).astype(BF16) for n in MATS})
    norms = {n: wts[n] for n in NORMS}
    dx, loss_lanes, dw, dn = _local_step(x[0], p[0, 0], positions[0], loss_target[0], norms, plan)
    assert not dw

    out = {}
    for n in MATS:
        res = _adamw(plan.received[n], local(wts[n], n), local(mom[n], n), local(var[n], n), "adamw_" + n)
        out[n] = [local(r[None], n)[None] for r in res]
    small = _pack_small([dn[n] for n in NORMS] + [0.5 / D_MODEL * jnp.sum(loss_lanes)[None]])
    small_parts = _exchange([jnp.broadcast_to(small[None], (N_DEV, SMALL_ROWS, 128))])[0]
    sizes = [wts[n].shape[1] for n in NORMS]
    pack = lambda d: _pack_small([d[n] for n in NORMS])
    small_res = _adamw(small_parts, pack(wts), pack(mom), pack(var), "adamw_norms")
    loss = small_res[0].reshape(-1)[sum(sizes)]
    for i, res in enumerate(small_res):
        for n, vec in zip(NORMS, _unpack_small(res, sizes)):
            out.setdefault(n, [None] * 4)[i] = vec[None]

    return (loss, dx[None], *[out[n][0] for n in WEIGHTS], *[out[n][1] for n in WEIGHTS],
            *[out[n][2] for n in WEIGHTS], *[out[n][3] for n in WEIGHTS])
```

```python
import functools
import math

import jax
import jax.numpy as jnp
from jax import lax
from jax.experimental import pallas as pl
from jax.experimental.pallas import tpu as pltpu

F32 = jnp.float32
BF16 = jnp.bfloat16

N_DEV = 8
D_MODEL = 1024
D_FF = 2816
PLE_DIM = 256
NORM_EPS = 1e-6
N_HEADS = 8
HEAD_PAD = 128
MLA_NOPE = 64
MLA_ROPE = 32
MLA_QK = 96
Q_LORA = 384
KV_LORA = 256
SB_DIM = 64
SB_WIDTH = 512
ROPE_BASE = 10000.0
IN_COLS = 4256

PROJ_W = 4608
P_CQ, P_CKV, P_KR, P_GM, P_GS, P_SBQ, P_SBK, P_SBV = 0, 384, 640, 1024, 2048, 3072, 3584, 4096

ADAM_LR, ADAM_B1, ADAM_B2, ADAM_EPS, ADAM_WD, ADAM_STEP = 0.001, 0.9, 0.999, 1e-08, 0.01, 10

VMEM_LIMIT = 52 * 1024 * 1024
MATMUL_VMEM = 40 * 1024 * 1024

WEIGHTS = ['ffn1_norm', 'ffn1_w_in', 'ffn1_w_out', 'mix_norm', 'w_in', 'q_latent_norm', 'w_q_up',
           'kv_latent_norm', 'w_kv_up', 'q_head_norm', 'k_head_norm', 'w_branch_mla', 'w_branch_sb',
           'w_out', 'ffn2_norm', 'ffn2_w_in', 'ffn2_w_out', 'ple_norm', 'w_ple_gate', 'w_ple_proj']
NORMS = ['ffn1_norm', 'mix_norm', 'q_latent_norm', 'kv_latent_norm', 'q_head_norm', 'k_head_norm',
         'ffn2_norm', 'ple_norm']
MATS = [n for n in WEIGHTS if n not in NORMS]
SMALL_ROWS = 48

NT_DIMS = (((1,), (1,)), ((), ()))
NN_DIMS = (((1,), (0,)), ((), ()))
TN_DIMS = (((0,), (0,)), ((), ()))


def _params(sem=None, vmem=VMEM_LIMIT):
    return pltpu.CompilerParams(dimension_semantics=sem, vmem_limit_bytes=vmem)


def _pick(n, cap):
    if n <= cap:
        return n
    best = None
    for t in range(128, cap + 1, 128):
        if n % t == 0:
            best = t
    assert best is not None, (n, cap)
    return best


def _dot(a, b, dims):
    return lax.dot_general(a, b, dims, preferred_element_type=F32)


def _matmul(a, b, *, mode, out_dtype, name, tm=None, tn=None, tk=None, res=None, alpha=1.0, cargo=None):
    if mode == 'nn':
        (m, k), (k2, n) = a.shape, b.shape
    elif mode == 'nt':
        (m, k), (n, k2) = a.shape, b.shape
    else:
        (k, m), (k2, n) = a.shape, b.shape
    assert k == k2, (name, a.shape, b.shape)
    has_res = res is not None
    tn = tn or _pick(n, 512)

    def vmem(tm_, tk_):
        io = 2 * 2 * (tm_ * tk_ + tk_ * tn) + 2 * tm_ * tn * (jnp.dtype(out_dtype).itemsize + 4 * has_res)
        return io + (4 * tm_ * tn if tk_ < k else 0)

    tries = [(tm_, tk_) for tk_ in ([tk] if tk else [k, _pick(k, 2048)])
             for tm_ in ([tm] if tm else [_pick(m, 2048), _pick(m, 1024), _pick(m, 512)])]
    tm, tk = next((c for c in tries if vmem(*c) <= MATMUL_VMEM), tries[-1])
    assert m % tm == 0 and n % tn == 0 and k % tk == 0, (name, m, n, k, tm, tn, tk)
    nk = k // tk
    dims = {'nn': NN_DIMS, 'nt': NT_DIMS, 'tn': TN_DIMS}[mode]

    def epilogue(acc, r_ref, o_ref):
        if alpha != 1.0:
            acc = acc * alpha
        if has_res:
            acc = r_ref[...] + acc
        o_ref[...] = acc.astype(out_dtype)

    grid = (m // tm, n // tn, nk)

    def body(*refs):
        steps = [pl.program_id(d) for d in range(3)]

        def compute(own):
            a_ref, b_ref = own[0], own[1]
            r_ref = own[2] if has_res else None
            o_ref = own[2 + has_res]
            if nk == 1:
                epilogue(_dot(a_ref[...], b_ref[...], dims), r_ref, o_ref)
                return
            acc_ref = own[-1]

            @pl.when(steps[2] == 0)
            def _():
                acc_ref[...] = jnp.zeros_like(acc_ref)

            acc_ref[...] += _dot(a_ref[...], b_ref[...], dims)

            @pl.when(steps[2] == nk - 1)
            def _():
                epilogue(acc_ref[...], r_ref, o_ref)

        _with_cargo(cargo, refs, 2 + has_res, 1, steps, grid, compute)

    if mode == 'tn':
        a_spec = pl.BlockSpec((tk, tm), lambda i, j, kk: (kk, i))
    else:
        a_spec = pl.BlockSpec((tm, tk), lambda i, j, kk: (i, kk))
    if mode == 'nt':
        b_spec = pl.BlockSpec((tn, tk), lambda i, j, kk: (j, kk))
    else:
        b_spec = pl.BlockSpec((tk, tn), lambda i, j, kk: (kk, j))
    o_spec = pl.BlockSpec((tm, tn), lambda i, j, kk: (i, j))
    in_specs = [a_spec, b_spec] + ([o_spec] if has_res else [])
    args = (a, b) + ((res,) if has_res else ())
    out_shape = jax.ShapeDtypeStruct((m, n), out_dtype)
    scratch = [pltpu.VMEM((tm, tn), F32)] if nk > 1 else []
    if cargo is None:
        return pl.pallas_call(
            body, name=name, grid=grid, in_specs=in_specs, out_specs=o_spec, out_shape=out_shape,
            scratch_shapes=scratch, compiler_params=_params(("parallel", "parallel", "arbitrary")),
        )(*args)
    outs = pl.pallas_call(
        body, name=name, grid=grid, in_specs=in_specs + cargo.specs(), out_specs=[o_spec] + cargo.specs(),
        out_shape=[out_shape] + cargo.out_shape(), scratch_shapes=scratch + cargo.scratch(),
        compiler_params=_params(("arbitrary", "arbitrary", "arbitrary")),
    )(*args, *cargo.srcs)
    return outs[0], list(outs[1:])


def _row_tile(t, cap=512):
    return min(t, cap)


def _rms(x, width):
    return lax.rsqrt(jnp.sum(x * x, axis=-1, keepdims=True) * (1.0 / width) + NORM_EPS)


def _rmsnorm_fwd(x, g, name):
    t, d = x.shape
    tr = _row_tile(t)

    def body(x_ref, g_ref, o_ref):
        xv = x_ref[...]
        o_ref[...] = ((xv * _rms(xv, d)) * g_ref[...]).astype(BF16)

    return pl.pallas_call(
        body, name=name, grid=(t // tr,),
        in_specs=[pl.BlockSpec((tr, d), lambda i: (i, 0)), pl.BlockSpec((1, d), lambda i: (0, 0))],
        out_specs=pl.BlockSpec((tr, d), lambda i: (i, 0)),
        out_shape=jax.ShapeDtypeStruct((t, d), BF16),
        compiler_params=_params(("parallel",)),
    )(x, g)


def _matmul_norm_bwd(a, b, x, g, dh_in, *, mode, name, out_scale, cargo=None):
    m, k = a.shape
    d = x.shape[1]
    tn = _pick(d, 512)

    def vmem(tm_):
        return 2 * 2 * (tm_ * k + k * tn) + tm_ * d * (4 + 2 * (4 + 4) + 2 * (4 + 2))

    tm = next((c for c in (_pick(m, 1024), _pick(m, 512), _pick(m, 256)) if vmem(c) <= MATMUL_VMEM), _pick(m, 256))
    grid = (m // tm, d // tn)
    dims = {'nn': NN_DIMS, 'nt': NT_DIMS}[mode]

    def body(*refs):
        steps = [pl.program_id(0), pl.program_id(1)]

        def compute(own):
            a_ref, b_ref, x_ref, g_ref, dhin_ref, dh_ref, dhb_ref, dg_ref, dn_ref = own
            for jj in range(grid[1]):
                @pl.when(steps[1] == jj)
                def _(jj=jj):
                    dn_ref[:, jj * tn:(jj + 1) * tn] = _dot(a_ref[...], b_ref[...], dims)

            @pl.when(steps[1] == grid[1] - 1)
            def _():
                xv = x_ref[...]
                dnv = dn_ref[...]
                r = _rms(xv, d)
                y = xv * r
                dy = dnv * g_ref[...]
                dh = dhin_ref[...] + r * (dy - y * (jnp.sum(dy * y, axis=-1, keepdims=True) * (1.0 / d)))
                dh_ref[...] = dh
                dhb_ref[...] = (dh * out_scale).astype(BF16)
                part = jnp.sum(dnv * y, axis=0, keepdims=True)

                @pl.when(steps[0] == 0)
                def _():
                    dg_ref[...] = part

                @pl.when(steps[0] > 0)
                def _():
                    dg_ref[...] += part

        _with_cargo(cargo, refs, 5, 3, steps, grid, compute)

    b_spec = pl.BlockSpec((k, tn), lambda i, j: (0, j)) if mode == 'nn' else pl.BlockSpec((tn, k), lambda i, j: (j, 0))
    row = pl.BlockSpec((tm, d), lambda i, j: (i, 0))
    vec = pl.BlockSpec((1, d), lambda i, j: (0, 0))
    extra = cargo.specs() if cargo else []
    outs = pl.pallas_call(
        body, name=name, grid=grid,
        in_specs=[pl.BlockSpec((tm, k), lambda i, j: (i, 0)), b_spec, row, vec, row] + extra,
        out_specs=[row, row, vec] + extra,
        out_shape=[jax.ShapeDtypeStruct((m, d), F32), jax.ShapeDtypeStruct((m, d), BF16),
                   jax.ShapeDtypeStruct((1, d), F32)] + (cargo.out_shape() if cargo else []),
        scratch_shapes=[pltpu.VMEM((tm, d), F32)] + (cargo.scratch() if cargo else []),
        compiler_params=_params(("arbitrary", "arbitrary")),
    )(a, b, x, g, dh_in, *(cargo.srcs if cargo else []))
    return (outs[0], outs[1], outs[2], list(outs[3:])) if cargo else outs


def _sigmoid(x):
    return 1.0 / (1.0 + jnp.exp(-x))


SWIGLU_CHUNK = 256
SWIGLU_COLS = 1408


def _chunks(width):
    return [slice(lo, min(lo + SWIGLU_CHUNK, width)) for lo in range(0, width, SWIGLU_CHUNK)]


def _swiglu_fwd(h, g, wt_in, name, cargo=None):
    t = h.shape[0]
    tm = _pick(t, 1024)
    grid = (t // tm, D_FF // SWIGLU_COLS)

    def body(*refs):
        steps = [pl.program_id(0), pl.program_id(1)]

        def compute(own):
            h_ref, g_ref, wa_ref, wb_ref, n_ref, a_ref, b_ref, act_ref, n_sc = own

            @pl.when(steps[1] == 0)
            def _():
                xv = h_ref[...]
                n_sc[...] = ((xv * _rms(xv, D_MODEL)) * g_ref[...]).astype(BF16)
                n_ref[...] = n_sc[...]

            nv = n_sc[...]
            for cols in _chunks(SWIGLU_COLS):
                a = _dot(nv, wa_ref[cols, :], NT_DIMS)
                b = _dot(nv, wb_ref[cols, :], NT_DIMS)
                a_ref[:, cols] = a.astype(BF16)
                b_ref[:, cols] = b.astype(BF16)
                act_ref[:, cols] = (a * _sigmoid(a) * b).astype(BF16)

        _with_cargo(cargo, refs, 4, 4, steps, grid, compute)

    half = D_FF // SWIGLU_COLS
    row = pl.BlockSpec((tm, D_MODEL), lambda i, j: (i, 0))
    tile = pl.BlockSpec((tm, SWIGLU_COLS), lambda i, j: (i, j))
    out = jax.ShapeDtypeStruct((t, D_FF), BF16)
    extra = cargo.specs() if cargo else []
    outs = pl.pallas_call(
        body, name=name, grid=grid,
        in_specs=[row, pl.BlockSpec((1, D_MODEL), lambda i, j: (0, 0)),
                  pl.BlockSpec((SWIGLU_COLS, D_MODEL), lambda i, j: (j, 0)),
                  pl.BlockSpec((SWIGLU_COLS, D_MODEL), lambda i, j: (half + j, 0))] + extra,
        out_specs=[row, tile, tile, tile] + extra,
        out_shape=[jax.ShapeDtypeStruct((t, D_MODEL), BF16), out, out, out] + (cargo.out_shape() if cargo else []),
        scratch_shapes=[pltpu.VMEM((tm, D_MODEL), BF16)] + (cargo.scratch() if cargo else []),
        compiler_params=_params(("arbitrary", "arbitrary")),
    )(h, g, wt_in, wt_in, *(cargo.srcs if cargo else []))
    return (outs[0], outs[1], outs[2], outs[3], list(outs[4:])) if cargo else outs


def _swiglu_bwd(dh, w_out, a, b, name):
    t = a.shape[0]
    tr = _row_tile(t, 512)

    def body(d_ref, w_ref, a_ref, b_ref, o_ref):
        dhv = d_ref[...]
        for cols in _chunks(D_FF):
            dv = _dot(dhv, w_ref[cols, :], NT_DIMS)
            av = a_ref[:, cols].astype(F32)
            s = _sigmoid(av)
            o_ref[:, cols] = (dv * b_ref[:, cols].astype(F32) * s * (1.0 + av * (1.0 - s))).astype(BF16)
            o_ref[:, slice(D_FF + cols.start, D_FF + cols.stop)] = (dv * av * s).astype(BF16)

    row = pl.BlockSpec((tr, D_FF), lambda i: (i, 0))
    return pl.pallas_call(
        body, name=name, grid=(t // tr,),
        in_specs=[pl.BlockSpec((tr, D_MODEL), lambda i: (i, 0)), pl.BlockSpec((D_FF, D_MODEL), lambda i: (0, 0)), row, row],
        out_specs=pl.BlockSpec((tr, 2 * D_FF), lambda i: (i, 0)),
        out_shape=jax.ShapeDtypeStruct((t, 2 * D_FF), BF16),
        compiler_params=_params(("parallel",)),
    )(dh, w_out, a, b)


def _latent_fwd(proj, gq, gkv):
    t = proj.shape[0]
    tr = _row_tile(t)

    def body(p_ref, gq_ref, gkv_ref, cq_ref, ckv_ref):
        cq = p_ref[:, P_CQ:P_CQ + Q_LORA].astype(F32)
        ckv = p_ref[:, P_CKV:P_CKV + KV_LORA].astype(F32)
        cq_ref[...] = ((cq * _rms(cq, Q_LORA)) * gq_ref[...]).astype(BF16)
        ckv_ref[...] = ((ckv * _rms(ckv, KV_LORA)) * gkv_ref[...]).astype(BF16)

    return pl.pallas_call(
        body, name="latent_fwd", grid=(t // tr,),
        in_specs=[pl.BlockSpec((tr, 1024), lambda i: (i, 0)), pl.BlockSpec((1, Q_LORA), lambda i: (0, 0)),
                  pl.BlockSpec((1, KV_LORA), lambda i: (0, 0))],
        out_specs=[pl.BlockSpec((tr, Q_LORA), lambda i: (i, 0)), pl.BlockSpec((tr, KV_LORA), lambda i: (i, 0))],
        out_shape=[jax.ShapeDtypeStruct((t, Q_LORA), BF16), jax.ShapeDtypeStruct((t, KV_LORA), BF16)],
        compiler_params=_params(("parallel",)),
    )(proj, gq, gkv)


def _latent_bwd(dcqn, dckvn, proj, dkr, gq, gkv):
    t = proj.shape[0]
    tr = _row_tile(t, 256)

    def norm_bwd(dn, x, g, width):
        r = _rms(x, width)
        y = x * r
        dy = dn * g
        dx = r * (dy - y * (jnp.sum(dy * y, axis=-1, keepdims=True) * (1.0 / width)))
        return dx, jnp.sum(dn * y, axis=0, keepdims=True)

    def body(dcq_ref, dckv_ref, p_ref, dkr_ref, gq_ref, gkv_ref, o_ref, dgq_ref, dgkv_ref):
        i = pl.program_id(0)
        dcq, pq = norm_bwd(dcq_ref[...], p_ref[:, P_CQ:P_CQ + Q_LORA].astype(F32), gq_ref[...], Q_LORA)
        dckv, pkv = norm_bwd(dckv_ref[...], p_ref[:, P_CKV:P_CKV + KV_LORA].astype(F32), gkv_ref[...], KV_LORA)
        o_ref[:, P_CQ:P_CQ + Q_LORA] = dcq.astype(BF16)
        o_ref[:, P_CKV:P_CKV + KV_LORA] = dckv.astype(BF16)
        o_ref[:, P_KR:P_KR + 128] = dkr_ref[...].astype(BF16)
        o_ref[:, P_KR + 128:1024] = jnp.zeros((tr, 1024 - P_KR - 128), BF16)

        @pl.when(i == 0)
        def _():
            dgq_ref[...] = pq
            dgkv_ref[...] = pkv

        @pl.when(i > 0)
        def _():
            dgq_ref[...] += pq
            dgkv_ref[...] += pkv

    def row(w):
        return pl.BlockSpec((tr, w), lambda i: (i, 0))

    def vec(w):
        return pl.BlockSpec((1, w), lambda i: (0, 0))

    return pl.pallas_call(
        body, name="latent_bwd", grid=(t // tr,),
        in_specs=[row(Q_LORA), row(KV_LORA), row(1024), row(128), vec(Q_LORA), vec(KV_LORA)],
        out_specs=[row(1024), vec(Q_LORA), vec(KV_LORA)],
        out_shape=[jax.ShapeDtypeStruct((t, 1024), BF16), jax.ShapeDtypeStruct((1, Q_LORA), F32),
                   jax.ShapeDtypeStruct((1, KV_LORA), F32)],
        compiler_params=_params(("arbitrary",)),
    )(dcqn, dckvn, proj, dkr, gq, gkv)


def _rope(y, cosf, sin_a, sin_b):
    return y * cosf + pltpu.roll(y, 112, 1) * sin_a + pltpu.roll(y, 16, 1) * sin_b


def _rope_t(d, cosf, sin_a, sin_b):
    return d * cosf + pltpu.roll(d * sin_a, 16, 1) + pltpu.roll(d * sin_b, 112, 1)


def _headprep_fwd(qraw, kvraw, proj, cosf, sin_a, sin_b, gqh, gkh):
    t = qraw.shape[0]
    tr = _row_tile(t, 256)

    def body(q_ref, kv_ref, kr_ref, c_ref, sa_ref, sb_ref, gq_ref, gk_ref, qh_ref, kh_ref, kvb_ref):
        cv, sa, sb = c_ref[...], sa_ref[...], sb_ref[...]
        kr = kr_ref[...].astype(F32)
        lane = lax.broadcasted_iota(jnp.int32, (tr, HEAD_PAD), 1)
        for h in range(N_HEADS):
            cols = slice(h * HEAD_PAD, (h + 1) * HEAD_PAD)
            xq = q_ref[:, cols]
            yq = (xq * _rms(xq, MLA_QK)) * gq_ref[...]
            qh_ref[:, cols] = (_rope(yq, cv, sa, sb) * MLA_Q_SCALE).astype(BF16)
            kvh = kv_ref[:, cols]
            kvb_ref[:, cols] = jnp.where(lane < MLA_NOPE, 1.0, kvh).astype(BF16)
            xk = jnp.where(lane < MLA_NOPE, kvh, kr)
            yk = (xk * _rms(xk, MLA_QK)) * gk_ref[...]
            kh_ref[:, cols] = _rope(yk, cv, sa, sb).astype(BF16)

    wide = pl.BlockSpec((tr, 1024), lambda i: (i, 0))
    lanes = pl.BlockSpec((tr, HEAD_PAD), lambda i: (i, 0))
    vec = pl.BlockSpec((1, HEAD_PAD), lambda i: (0, 0))
    return pl.pallas_call(
        body, name="headprep_fwd", grid=(t // tr,),
        in_specs=[wide, wide, pl.BlockSpec((tr, HEAD_PAD), lambda i: (i, P_KR // HEAD_PAD)), lanes, lanes, lanes, vec, vec],
        out_specs=[wide, wide, wide],
        out_shape=[jax.ShapeDtypeStruct((t, 1024), BF16)] * 3,
        compiler_params=_params(("parallel",)),
    )(qraw, kvraw, proj, cosf, sin_a, sin_b, gqh, gkh)


def _headprep_bwd(dqh, dkh, dvp, qraw, kvraw, proj, cosf, sin_a, sin_b, gqh, gkh):
    t = qraw.shape[0]
    tr = _row_tile(t, 256)

    def norm_bwd(dn, x, g):
        r = _rms(x, MLA_QK)
        y = x * r
        dy = dn * g
        dx = r * (dy - y * (jnp.sum(dy * y, axis=-1, keepdims=True) * (1.0 / MLA_QK)))
        return dx, jnp.sum(dn * y, axis=0, keepdims=True)

    def body(dq_ref, dk_ref, dv_ref, q_ref, kv_ref, kr_ref, c_ref, sa_ref, sb_ref, gq_ref, gk_ref,
             dqr_ref, dkvr_ref, dkr_ref, dgq_ref, dgk_ref):
        i = pl.program_id(0)
        cv, sa, sb = c_ref[...], sa_ref[...], sb_ref[...]
        kr = kr_ref[...].astype(F32)
        lane = lax.broadcasted_iota(jnp.int32, (tr, HEAD_PAD), 1)
        dkr = jnp.zeros((tr, HEAD_PAD), F32)
        pq = jnp.zeros((1, HEAD_PAD), F32)
        pk = jnp.zeros((1, HEAD_PAD), F32)
        for h in range(N_HEADS):
            cols = slice(h * HEAD_PAD, (h + 1) * HEAD_PAD)
            dxq, pqh = norm_bwd(_rope_t(dq_ref[:, cols], cv, sa, sb), q_ref[:, cols], gq_ref[...])
            dqr_ref[:, cols] = dxq.astype(BF16)
            pq = pq + pqh
            kvh = kv_ref[:, cols]
            xk = jnp.where(lane < MLA_NOPE, kvh, kr)
            dxk, pkh = norm_bwd(_rope_t(dk_ref[:, cols], cv, sa, sb), xk, gk_ref[...])
            pk = pk + pkh
            dkvr_ref[:, cols] = jnp.where(lane < MLA_NOPE, dxk, dv_ref[:, cols]).astype(BF16)
            dkr = dkr + jnp.where(lane < MLA_NOPE, 0.0, dxk)
        dkr_ref[...] = dkr

        @pl.when(i == 0)
        def _():
            dgq_ref[...] = pq
            dgk_ref[...] = pk

        @pl.when(i > 0)
        def _():
            dgq_ref[...] += pq
            dgk_ref[...] += pk

    wide = pl.BlockSpec((tr, 1024), lambda i: (i, 0))
    lanes = pl.BlockSpec((tr, HEAD_PAD), lambda i: (i, 0))
    vec = pl.BlockSpec((1, HEAD_PAD), lambda i: (0, 0))
    return pl.pallas_call(
        body, name="headprep_bwd", grid=(t // tr,),
        in_specs=[wide, wide, wide, wide, wide, pl.BlockSpec((tr, HEAD_PAD), lambda i: (i, P_KR // HEAD_PAD)),
                  lanes, lanes, lanes, vec, vec],
        out_specs=[wide, wide, lanes, vec, vec],
        out_shape=[jax.ShapeDtypeStruct((t, 1024), BF16), jax.ShapeDtypeStruct((t, 1024), BF16),
                   jax.ShapeDtypeStruct((t, HEAD_PAD), F32), jax.ShapeDtypeStruct((1, HEAD_PAD), F32),
                   jax.ShapeDtypeStruct((1, HEAD_PAD), F32)],
        compiler_params=_params(("arbitrary",)),
    )(dqh, dkh, dvp, qraw, kvraw, proj, cosf, sin_a, sin_b, gqh, gkh)


def _merge_fwd(o_mla, w_mla, o_sb, w_sb, proj):
    t = proj.shape[0]
    tr = _row_tile(t, 512)

    def body(om_ref, wm_ref, os_ref, ws_ref, gm_ref, gs_ref, o_ref, bm_ref, bs_ref):
        omv, osv = om_ref[...], os_ref[...]
        for cols in _chunks(D_MODEL):
            bm = _dot(omv, wm_ref[:, cols], NN_DIMS)
            bs = _dot(osv, ws_ref[:, cols], NN_DIMS)
            bm_ref[:, cols] = bm
            bs_ref[:, cols] = bs
            gm = _sigmoid(gm_ref[:, cols].astype(F32))
            gs = _sigmoid(gs_ref[:, cols].astype(F32))
            o_ref[:, cols] = (gm * bm + gs * bs).astype(BF16)

    row = pl.BlockSpec((tr, 1024), lambda i: (i, 0))
    f32 = jax.ShapeDtypeStruct((t, 1024), F32)
    return pl.pallas_call(
        body, name="merge_fwd", grid=(t // tr,),
        in_specs=[row, pl.BlockSpec(w_mla.shape, lambda i: (0, 0)),
                  pl.BlockSpec((tr, SB_WIDTH), lambda i: (i, 0)), pl.BlockSpec(w_sb.shape, lambda i: (0, 0)),
                  pl.BlockSpec((tr, 1024), lambda i: (i, P_GM // 1024)), pl.BlockSpec((tr, 1024), lambda i: (i, P_GS // 1024))],
        out_specs=[row, row, row], out_shape=[jax.ShapeDtypeStruct((t, 1024), BF16), f32, f32],
        compiler_params=_params(("parallel",)),
    )(o_mla, w_mla, o_sb, w_sb, proj, proj)


def _merge_bwd(dh, w_out, proj, bm, bs):
    t = proj.shape[0]
    tr = _row_tile(t, 512)

    def body(d_ref, w_ref, gm_ref, gs_ref, bm_ref, bs_ref, dbm_ref, dbs_ref, dg_ref):
        dhv = d_ref[...]
        for cols in _chunks(D_MODEL):
            dm = _dot(dhv, w_ref[cols, :], NT_DIMS)
            gm = _sigmoid(gm_ref[:, cols].astype(F32))
            gs = _sigmoid(gs_ref[:, cols].astype(F32))
            dbm_ref[:, cols] = (dm * gm).astype(BF16)
            dbs_ref[:, cols] = (dm * gs).astype(BF16)
            dg_ref[:, cols] = (dm * bm_ref[:, cols] * gm * (1.0 - gm)).astype(BF16)
            dg_ref[:, slice(D_MODEL + cols.start, D_MODEL + cols.stop)] = (dm * bs_ref[:, cols] * gs * (1.0 - gs)).astype(BF16)

    row = pl.BlockSpec((tr, 1024), lambda i: (i, 0))
    return pl.pallas_call(
        body, name="mix_out_dx", grid=(t // tr,),
        in_specs=[row, pl.BlockSpec((D_MODEL, D_MODEL), lambda i: (0, 0)),
                  pl.BlockSpec((tr, 1024), lambda i: (i, P_GM // 1024)),
                  pl.BlockSpec((tr, 1024), lambda i: (i, P_GS // 1024)), row, row],
        out_specs=[row, row, pl.BlockSpec((tr, 2048), lambda i: (i, 0))],
        out_shape=[jax.ShapeDtypeStruct((t, 1024), BF16), jax.ShapeDtypeStruct((t, 1024), BF16),
                   jax.ShapeDtypeStruct((t, 2048), BF16)],
        compiler_params=_params(("parallel",)),
    )(dh, w_out, proj, proj, bm, bs)


def _ple_loss(h3, g, w_gate, pb, w_proj, tgt):
    t = h3.shape[0]
    tr = _row_tile(t, 512)

    def body(h_ref, g_ref, wg_ref, p_ref, wp_ref, t_ref, n_ref, dh_ref, dz_ref, dp_ref, l_ref):
        i = pl.program_id(0)
        xv = h_ref[...]
        nv = ((xv * _rms(xv, D_MODEL)) * g_ref[...]).astype(BF16)
        n_ref[...] = nv
        pv = p_ref[...]
        part = jnp.zeros((1, 128), F32)
        for cols in _chunks(D_MODEL):
            pg = _sigmoid(_dot(nv, wg_ref[:, cols], NN_DIMS))
            ppv = _dot(pv, wp_ref[:, cols], NN_DIMS)
            diff = (h_ref[:, cols] + pg * ppv) - t_ref[:, cols]
            dh = diff * (1.0 / D_MODEL)
            dh_ref[:, cols] = dh
            dp_ref[:, cols] = (dh * pg).astype(BF16)
            dz_ref[:, cols] = (dh * ppv * pg * (1.0 - pg)).astype(BF16)
            sq = jnp.sum(diff * diff, axis=0, keepdims=True)
            for c in range(sq.shape[1] // 128):
                part = part + sq[:, c * 128:(c + 1) * 128]

        @pl.when(i == 0)
        def _():
            l_ref[...] = part

        @pl.when(i > 0)
        def _():
            l_ref[...] += part

    row = pl.BlockSpec((tr, 1024), lambda i: (i, 0))
    return pl.pallas_call(
        body, name="ple_loss", grid=(t // tr,),
        in_specs=[row, pl.BlockSpec((1, D_MODEL), lambda i: (0, 0)), pl.BlockSpec((D_MODEL, D_MODEL), lambda i: (0, 0)),
                  pl.BlockSpec((tr, PLE_DIM), lambda i: (i, 0)), pl.BlockSpec((PLE_DIM, D_MODEL), lambda i: (0, 0)), row],
        out_specs=[row, row, row, row, pl.BlockSpec((1, 128), lambda i: (0, 0))],
        out_shape=[jax.ShapeDtypeStruct((t, 1024), BF16), jax.ShapeDtypeStruct((t, 1024), F32),
                   jax.ShapeDtypeStruct((t, 1024), BF16), jax.ShapeDtypeStruct((t, 1024), BF16),
                   jax.ShapeDtypeStruct((1, 128), F32)],
        compiler_params=_params(("arbitrary",)),
    )(h3, g, w_gate, pb, w_proj, tgt)


ATT_BLOCK = 256
MLA_Q_SCALE = math.log2(math.e) / math.sqrt(MLA_QK)
MLA_FWD_COLS = 4
MLA_BWD_COLS = 4
SB_FWD_COLS = 4
SB_BWD_COLS = 2
SB_BLOCK = 256


def _split_bf16(x):
    hi = x.astype(BF16)
    return hi, (x - hi.astype(F32)).astype(BF16)


def _tri(kind, n):
    r = lax.broadcasted_iota(jnp.int32, (n, n), 0)
    c = lax.broadcasted_iota(jnp.int32, (n, n), 1)
    cond = {'gt': r > c, 'le': r <= c, 'lt': r < c}[kind]
    return jnp.where(cond, 1.0, 0.0).astype(BF16)


def _causal(strict, n=ATT_BLOCK):
    r = lax.broadcasted_iota(jnp.int32, (n, n), 0)
    c = lax.broadcasted_iota(jnp.int32, (n, n), 1)
    return (c < r) if strict else (c <= r)


def _lanes(c):
    return slice(c * HEAD_PAD, (c + 1) * HEAD_PAD)


def _row_block(j, n=ATT_BLOCK):
    return pl.ds(pl.multiple_of(j * n, n), n)


def _rows(ref, j, c, n=ATT_BLOCK):
    return ref[_row_block(j, n), _lanes(c)]


def _mla_fwd(qh, kh, kvb, cargo=None):
    t = qh.shape[0]
    bq = ATT_BLOCK
    ncol = MLA_FWD_COLS
    grid = (N_HEADS // ncol, t // bq)

    def body(*refs):
        steps = [pl.program_id(0), pl.program_id(1)]
        _with_cargo(cargo, refs, 3, 2, steps, grid, lambda own: work(steps[1], *own))

    def work(i, q_ref, k_ref, v_ref, o_ref, lse_ref):
        qs = [q_ref[:, _lanes(c)] for c in range(ncol)]

        def step(j, carry, masked):
            cols = range(ncol)
            scores = [_dot(qs[c], _rows(k_ref, j, c), NT_DIMS) for c in cols]
            ms, ps, alphas = [], [], []
            for c in cols:
                s = scores[c]
                if masked:
                    s = jnp.where(_causal(False), s, -1e30)
                m_new = jnp.maximum(carry[c][0], jnp.max(s, axis=-1, keepdims=True))
                ps.append(jnp.exp2(s - m_new).astype(BF16))
                alphas.append(jnp.exp2(carry[c][0] - m_new))
                ms.append(m_new)
            return tuple((ms[c], alphas[c] * carry[c][1] + _dot(ps[c], _rows(v_ref, j, c), NN_DIMS)) for c in cols)

        init = tuple((jnp.full((bq, 1), -1e30, F32), jnp.zeros((bq, HEAD_PAD), F32)) for _ in range(ncol))
        carry = lax.fori_loop(0, i, lambda j, cr: step(j, cr, False), init)
        for c, (m, acc) in enumerate(step(i, carry, True)):
            l = acc[:, 0:1]
            o_ref[:, _lanes(c)] = (acc / l).astype(BF16)
            lse_ref[c] = m + jnp.log2(l)

    width = ncol * HEAD_PAD
    full = pl.BlockSpec((t, width), lambda h, i: (0, h))
    blk = pl.BlockSpec((bq, width), lambda h, i: (i, h))
    extra = cargo.specs() if cargo else []
    outs = pl.pallas_call(
        body, name="mla_fwd", grid=grid,
        in_specs=[blk, full, full] + extra,
        out_specs=[blk, pl.BlockSpec((ncol, bq, 1), lambda h, i: (h, i, 0))] + extra,
        out_shape=[jax.ShapeDtypeStruct((t, N_HEADS * HEAD_PAD), BF16), jax.ShapeDtypeStruct((N_HEADS, t, 1), F32)]
        + (cargo.out_shape() if cargo else []),
        scratch_shapes=cargo.scratch() if cargo else [],
        compiler_params=_params(("arbitrary", "arbitrary")),
    )(qh, kh, kvb, *(cargo.srcs if cargo else []))
    return (outs[0], outs[1], list(outs[2:])) if cargo else outs


def _mla_bwd(qh, kh, kvb, o, do, lse, cargo=None):
    t = qh.shape[0]
    bq = ATT_BLOCK
    ncol = MLA_BWD_COLS
    width = ncol * HEAD_PAD
    grid = (N_HEADS // ncol, t // bq)

    def body(*refs):
        steps = [pl.program_id(0), pl.program_id(1)]
        _with_cargo(cargo, refs, 6, 3, steps, grid, lambda own: work(steps[0], steps[1], *own))

    def work(h, i, q_ref, k_ref, v_ref, o_ref, do_ref, lse_ref, dq_ref, dk_hbm, dv_hbm, dk_ref, dv_ref, out_sems):

        @pl.when(i == 0)
        def _():
            dk_ref[...] = jnp.zeros_like(dk_ref)
            dv_ref[...] = jnp.zeros_like(dv_ref)

        qs = [q_ref[:, _lanes(c)] for c in range(ncol)]
        dos = [do_ref[:, _lanes(c)] for c in range(ncol)]
        deltas = [jnp.sum(dos[c].astype(F32) * o_ref[:, _lanes(c)].astype(F32), axis=-1, keepdims=True)
                  for c in range(ncol)]
        lses = [lse_ref[c] for c in range(ncol)]

        def step(j, dqs, masked):
            cols = range(ncol)
            kbs = [_rows(k_ref, j, c) for c in cols]
            scores = [_dot(qs[c], kbs[c], NT_DIMS) for c in cols]
            dps = [_dot(dos[c], _rows(v_ref, j, c), NT_DIMS) for c in cols]
            pbs, dss = [], []
            for c in cols:
                p = jnp.exp2(scores[c] - lses[c])
                if masked:
                    p = jnp.where(_causal(False), p, 0.0)
                pbs.append(p.astype(BF16))
                dss.append((p * (dps[c] - deltas[c])).astype(BF16))
            for c in cols:
                dv_ref[_row_block(j), _lanes(c)] += _dot(pbs[c], dos[c], TN_DIMS)
                dk_ref[_row_block(j), _lanes(c)] += _dot(dss[c], qs[c], TN_DIMS)
            return tuple(dqs[c] + _dot(dss[c], kbs[c], NN_DIMS) for c in cols)

        init = tuple(jnp.zeros((bq, HEAD_PAD), F32) for _ in range(ncol))
        dqs = lax.fori_loop(0, i, lambda j, cr: step(j, cr, False), init)
        for c, dq in enumerate(step(i, dqs, True)):
            dq_ref[:, _lanes(c)] = dq * (1.0 / math.sqrt(MLA_QK))

        @pl.when(i == grid[1] - 1)
        def _():
            dk_ref[...] = dk_ref[...] * math.log(2.0)
            cols = pl.ds(pl.multiple_of(h * width, width), width)
            out = [pltpu.make_async_copy(dk_ref, dk_hbm.at[:, cols], out_sems.at[0]),
                   pltpu.make_async_copy(dv_ref, dv_hbm.at[:, cols], out_sems.at[1])]
            for cp in out:
                cp.start()
            for cp in out:
                cp.wait()

    full = pl.BlockSpec((t, width), lambda h, i: (0, h))
    blk = pl.BlockSpec((bq, width), lambda h, i: (i, h))
    wide = jax.ShapeDtypeStruct((t, N_HEADS * HEAD_PAD), F32)
    extra = cargo.specs() if cargo else []
    outs = pl.pallas_call(
        body, name="mla_bwd", grid=grid,
        in_specs=[blk, full, full, blk, blk, pl.BlockSpec((ncol, bq, 1), lambda h, i: (h, i, 0))] + extra,
        out_specs=[blk, HBM_SPEC, HBM_SPEC] + extra,
        out_shape=[wide, wide, wide] + (cargo.out_shape() if cargo else []),
        scratch_shapes=[pltpu.VMEM((t, width), F32), pltpu.VMEM((t, width), F32), pltpu.SemaphoreType.DMA((2,))]
        + (cargo.scratch() if cargo else []),
        compiler_params=_params(("arbitrary", "arbitrary")),
    )(qh, kh, kvb, o, do, lse, *(cargo.srcs if cargo else []))
    return (outs[0], outs[1], outs[2], list(outs[3:])) if cargo else outs


def _head_only(x, lane, u):
    return jnp.where((lane >= u * SB_DIM) & (lane < (u + 1) * SB_DIM), x, jnp.zeros_like(x))


SB_DEAD = -104.0


def _log_sigmoids(z):
    e = jnp.exp(-jnp.abs(z))
    lg = jnp.log(1.0 + e)
    ls_pos = jnp.minimum(z, 0.0) - lg
    return ls_pos, ls_pos - z, e


def _sb_fwd(proj):
    t = proj.shape[0]
    bq, ncol = SB_BLOCK, SB_FWD_COLS
    nq = t // bq
    scale = 1.0 / math.sqrt(SB_DIM)
    pairs = SB_WIDTH // HEAD_PAD

    def body(q_ref, k_ref, v_ref, o_ref, r_ref, first_ref):
        g, i = pl.program_id(0), pl.program_id(1)
        lane = lax.broadcasted_iota(jnp.int32, (bq, HEAD_PAD), 1)
        upper = _tri('gt', bq)
        chains = [(c, u) for c in range(ncol) for u in range(2)]
        qms = [_head_only(q_ref[:, _lanes(c)], lane, u) * scale for c, u in chains]

        def step(j, carry, masked):
            ids = range(len(chains))
            zs = [_dot(qms[n], _rows(k_ref, j, chains[n][0], bq), NT_DIMS) for n in ids]
            pos, neg, parts = [], [], []
            for n in ids:
                ls_pos, ls_neg, _ = _log_sigmoids(zs[n])
                if masked:
                    ls_neg = jnp.where(_causal(True, bq), ls_neg, 0.0)
                pos.append(ls_pos)
                neg.append(ls_neg)
                parts.append(_split_bf16(ls_neg))
            suffix = [_dot(parts[n][0], upper, NN_DIMS) + _dot(parts[n][1], upper, NN_DIMS) for n in ids]
            weights = []
            for n in ids:
                a = jnp.exp(pos[n] + suffix[n] + carry[n][0])
                if masked:
                    a = jnp.where(_causal(True, bq), a, 0.0)
                weights.append(a.astype(BF16))
            return tuple((carry[n][0] + jnp.sum(neg[n], axis=-1, keepdims=True),
                          carry[n][1] + _dot(weights[n], _rows(v_ref, j, chains[n][0], bq), NN_DIMS)) for n in ids)

        init = tuple((jnp.zeros((bq, 1), F32), jnp.zeros((bq, HEAD_PAD), F32)) for _ in chains)
        carry = step(i, init, True)

        def more(state):
            s, cr = state
            live = cr[0][0]
            for n in range(1, len(chains)):
                live = jnp.maximum(live, cr[n][0])
            return jnp.logical_and(s < i, jnp.max(live) > SB_DEAD)

        walked, carry = lax.while_loop(more, lambda st: (st[0] + 1, step(i - 1 - st[0], st[1], False)),
                                       (jnp.int32(0), carry))
        first_ref[g * nq + i] = i - walked
        for n, (c, u) in enumerate(chains):
            r_ref[2 * c + u] = carry[n][0]
        for c in range(ncol):
            o_ref[:, _lanes(c)] = jnp.where(lane < SB_DIM, carry[2 * c][1], carry[2 * c + 1][1]).astype(BF16)

    width = ncol * HEAD_PAD

    def full(c0):
        return pl.BlockSpec((t, width), lambda g, i: (0, c0 // width + g))

    return pl.pallas_call(
        body, name="sb_fwd", grid=(pairs // ncol, t // bq),
        in_specs=[pl.BlockSpec((bq, width), lambda g, i: (i, P_SBQ // width + g)), full(P_SBK), full(P_SBV)],
        out_specs=[pl.BlockSpec((bq, width), lambda g, i: (i, g)),
                   pl.BlockSpec((2 * ncol, bq, 1), lambda g, i: (g, i, 0)),
                   pl.BlockSpec(memory_space=pltpu.SMEM)],
        out_shape=[jax.ShapeDtypeStruct((t, SB_WIDTH), BF16), jax.ShapeDtypeStruct((N_HEADS, t, 1), F32),
                   jax.ShapeDtypeStruct((pairs // ncol * nq,), jnp.int32)],
        compiler_params=_params(("arbitrary", "arbitrary")),
    )(proj, proj, proj)


def _sb_bwd(proj, do, rtot, first):
    t = proj.shape[0]
    bq, ncol = SB_BLOCK, SB_BWD_COLS
    nq = t // bq
    scale = 1.0 / math.sqrt(SB_DIM)
    pairs = SB_WIDTH // HEAD_PAD

    def body(first_ref, q_ref, k_ref, v_ref, do_ref, r_ref, dq_ref, dk_ref, dv_ref):
        g, i = pl.program_id(0), pl.program_id(1)

        @pl.when(i == 0)
        def _():
            dk_ref[...] = jnp.zeros_like(dk_ref)
            dv_ref[...] = jnp.zeros_like(dv_ref)

        lane = lax.broadcasted_iota(jnp.int32, (bq, HEAD_PAD), 1)
        incl = _tri('le', bq)
        excl = _tri('lt', bq)
        chains = [(c, u) for c in range(ncol) for u in range(2)]
        qms = [_head_only(q_ref[:, _lanes(c)], lane, u) * scale for c, u in chains]
        doms = [_head_only(do_ref[:, _lanes(c)], lane, u) for c, u in chains]
        rts = [r_ref[2 * c + u] for c, u in chains]

        def step(j, carry, masked):
            ids = range(len(chains))
            kbs = [_rows(k_ref, j, c, bq) for c in range(ncol)]
            zs =[_dot(qms[n], kbs[chains[n][0]], NT_DIMS) for n in ids]
            das = [_dot(doms[n], _rows(v_ref, j, chains[n][0], bq), NT_DIMS) for n in ids]
            pos, neg, sigs, parts = [], [], [], []
            for n in ids:
                ls_pos, ls_neg, e = _log_sigmoids(zs[n])
                if masked:
                    ls_neg = jnp.where(_causal(True, bq), ls_neg, 0.0)
                pos.append(ls_pos)
                neg.append(ls_neg)
                sigs.append(jnp.where(zs[n] >= 0.0, 1.0, e) * pl.reciprocal(1.0 + e, approx=True))
                parts.append(_split_bf16(ls_neg))
            prefix = [_dot(parts[n][0], incl, NN_DIMS) + _dot(parts[n][1], incl, NN_DIMS) for n in ids]
            evs, eparts, dvs = [], [], []
            for n in ids:
                a = jnp.exp(pos[n] + (rts[n] - (carry[n][0] + prefix[n])))
                if masked:
                    a = jnp.where(_causal(True, bq), a, 0.0)
                dvs.append(_dot(a.astype(BF16), doms[n], TN_DIMS))
                evs.append(a * das[n])
                eparts.append(evs[n].astype(BF16))
            before = [_dot(eparts[n], excl, NN_DIMS) for n in ids]
            out, dks = [], []
            for n in ids:
                dz = evs[n] - sigs[n] * (evs[n] + (carry[n][1] + before[n]))
                if masked:
                    dz = jnp.where(_causal(True, bq), dz, 0.0)
                dzb = dz.astype(BF16)
                dks.append(_dot(dzb, qms[n], TN_DIMS))
                out.append((carry[n][0] + jnp.sum(neg[n], axis=-1, keepdims=True),
                            carry[n][1] + jnp.sum(evs[n], axis=-1, keepdims=True),
                            carry[n][2] + _dot(dzb, kbs[chains[n][0]], NN_DIMS)))
            for c in range(ncol):
                dv_ref[_row_block(j, bq), _lanes(c)] += dvs[2 * c] + dvs[2 * c + 1]
                dk_ref[_row_block(j, bq), _lanes(c)] += dks[2 * c] + dks[2 * c + 1]
            return tuple(out)

        init = tuple((jnp.zeros((bq, 1), F32), jnp.zeros((bq, 1), F32), jnp.zeros((bq, HEAD_PAD), F32)) for _ in chains)
        start = first_ref[(g * ncol // SB_FWD_COLS) * nq + i]
        carry = lax.fori_loop(start, i, lambda j, cr: step(j, cr, False), init)
        carry = step(i, carry, True)
        for c in range(ncol):
            dq_ref[:, _lanes(c)] = jnp.where(lane < SB_DIM, carry[2 * c][2], carry[2 * c + 1][2]) * scale

    width = ncol * HEAD_PAD

    def full(c0):
        return pl.BlockSpec((t, width), lambda g, i, first: (0, c0 // width + g))

    blk = pl.BlockSpec((bq, width), lambda g, i, first: (i, g))
    acc = pl.BlockSpec((t, width), lambda g, i, first: (0, g))
    wide = jax.ShapeDtypeStruct((t, SB_WIDTH), F32)
    return pl.pallas_call(
        body, name="sb_bwd",
        grid_spec=pltpu.PrefetchScalarGridSpec(
            num_scalar_prefetch=1, grid=(pairs // ncol, nq),
            in_specs=[pl.BlockSpec((bq, width), lambda g, i, first: (i, P_SBQ // width + g)), full(P_SBK), full(P_SBV),
                      blk, pl.BlockSpec((2 * ncol, bq, 1), lambda g, i, first: (g, i, 0))],
            out_specs=[blk, acc, acc]),
        out_shape=[wide, wide, wide],
        compiler_params=_params(("arbitrary", "arbitrary")),
    )(first, proj, proj, proj, do, rtot)


def _cols_to_full(g):
    n, r, c = g.shape
    return jnp.transpose(g, (1, 0, 2)).reshape(r, n * c)


def _full_to_cols(w):
    r, c = w.shape
    return jnp.transpose(w.reshape(r, N_DEV, c // N_DEV), (1, 0, 2))


TRANSPOSED = ('ffn1_w_in', 'ffn2_w_in', 'w_in', 'w_q_up')


def _layout_weight(name, g):
    if name in ('ffn1_w_out', 'ffn2_w_out', 'w_out', 'w_ple_gate', 'ffn1_w_in', 'ffn2_w_in'):
        return g.reshape(g.shape[0] * g.shape[1], g.shape[2])
    if name == 'w_in':
        wt = g.reshape(IN_COLS, D_MODEL)
        z = lambda n: jnp.zeros((n, D_MODEL), BF16)
        return jnp.concatenate([wt[0:640], z(64), wt[640:672], z(32), z(256), wt[2208:4256], wt[672:2208]], axis=0)
    if name == 'w_q_up':
        return jnp.pad(g, ((0, 0), (0, HEAD_PAD - MLA_QK), (0, 0))).reshape(N_HEADS * HEAD_PAD, Q_LORA)
    if name == 'w_branch_mla':
        bm = _cols_to_full(g).reshape(N_HEADS, MLA_NOPE, D_MODEL)
        return jnp.pad(bm, ((0, 0), (HEAD_PAD - MLA_NOPE, 0), (0, 0))).reshape(N_HEADS * HEAD_PAD, D_MODEL)
    return _cols_to_full(g)


def _layout_weights(g):
    return {n: _layout_weight(n, a) for n, a in g.items()}


def _unlayout_grad(name, d):
    if name == 'w_in':
        d = jnp.concatenate([d[0:640], d[704:736], d[P_SBQ:PROJ_W], d[P_GM:P_SBQ]], axis=0)
    if name == 'w_q_up':
        return d.reshape(N_HEADS, HEAD_PAD, Q_LORA)[:, :MLA_QK, :]
    if name in ('ffn1_w_out', 'ffn2_w_out', 'w_out', 'w_ple_gate', 'ffn1_w_in', 'ffn2_w_in', 'w_in'):
        return d.reshape(N_DEV, d.shape[0] // N_DEV, d.shape[1])
    if name == 'w_branch_mla':
        d = d.reshape(N_HEADS, HEAD_PAD, D_MODEL)[:, HEAD_PAD - MLA_NOPE:, :].reshape(SB_WIDTH, D_MODEL)
    return _full_to_cols(d)


def _unlayout_grads(d):
    return {n: _unlayout_grad(n, a) for n, a in d.items()}


def _rope_tables(positions):
    half = MLA_ROPE // 2
    inv_freq = ROPE_BASE ** (-jnp.arange(0, MLA_ROPE, 2, dtype=F32) / MLA_ROPE)
    ang = positions.astype(F32)[:, None] * inv_freq
    cos, sin = jnp.cos(ang), jnp.sin(ang)
    t = positions.shape[0]
    ones = lambda n: jnp.ones((t, n), F32)
    zeros = lambda n: jnp.zeros((t, n), F32)
    cosf = jnp.concatenate([ones(MLA_NOPE), cos, cos, ones(HEAD_PAD - MLA_QK)], axis=1)
    sin_a = jnp.concatenate([zeros(MLA_NOPE), -sin, zeros(half), zeros(HEAD_PAD - MLA_QK)], axis=1)
    sin_b = jnp.concatenate([zeros(MLA_NOPE), zeros(half), sin, zeros(HEAD_PAD - MLA_QK)], axis=1)
    return cosf, sin_a, sin_b


def _local_step(x, p, positions, tgt, norms, plan):
    mm = _matmul
    cosf, sin_a, sin_b = _rope_tables(positions)
    pad_head = lambda g: jnp.pad(g, ((0, 0), (0, HEAD_PAD - MLA_QK)))
    gqh, gkh = pad_head(norms['q_head_norm']), pad_head(norms['k_head_norm'])
    pb = p.astype(BF16)
    w = dict(plan.first_weights())
    dw, dn = {}, {}

    def ride(host, call):
        cargo = plan.cargo(host, dw)
        res, lands = call(cargo), None
        if cargo is not None:
            *res, lands = res
            res = res[0] if len(res) == 1 else tuple(res)
        w.update(plan.landed(host, lands))
        return res

    def ffn_fwd(h, tag):
        n, a, b, act = ride(tag + "_in_fwd", lambda cargo: _swiglu_fwd(
            h, norms[tag + '_norm'], w[tag + '_w_in'], tag + "_in_fwd", cargo))
        out = ride(tag + "_out_fwd", lambda cargo: mm(
            act, w[tag + '_w_out'], mode='nn', out_dtype=F32, name=tag + "_out_fwd", res=h, alpha=0.5, cargo=cargo))
        return out, (n, a, b, act)

    h1, ffn1_saved = ffn_fwd(x, 'ffn1')
    u = _rmsnorm_fwd(h1, norms['mix_norm'], "mix_norm_fwd")
    proj = mm(u, w['w_in'], mode='nt', out_dtype=BF16, name="proj_fwd")
    cqn, ckvn = _latent_fwd(proj, norms['q_latent_norm'], norms['kv_latent_norm'])
    qraw = mm(cqn, w['w_q_up'], mode='nt', out_dtype=F32, name="q_up_fwd")
    kvraw = mm(ckvn, w['w_kv_up'], mode='nn', out_dtype=F32, name="kv_up_fwd")
    qh, kh, kvb = _headprep_fwd(qraw, kvraw, proj, cosf, sin_a, sin_b, gqh, gkh)
    o_mla, lse = ride("mla_fwd", lambda cargo: _mla_fwd(qh, kh, kvb, cargo))
    o_sb, rtot, sb_first = _sb_fwd(proj)
    merged, bm, bs = _merge_fwd(o_mla, w['w_branch_mla'], o_sb, w['w_branch_sb'], proj)
    h2 = mm(merged, w['w_out'], mode='nn', out_dtype=F32, name="mix_out_fwd", res=h1)
    h3, ffn2_saved = ffn_fwd(h2, 'ffn2')
    n3, dh4, dzg, dpp, loss_lanes = _ple_loss(h3, norms['ple_norm'], w['w_ple_gate'], pb, w['w_ple_proj'], tgt)

    dw['w_ple_gate'] = mm(n3, dzg, mode='tn', out_dtype=BF16, name="ple_gate_dw")
    dw['w_ple_proj'] = mm(pb, dpp, mode='tn', out_dtype=BF16, name="ple_proj_dw")
    dh3, dhb3, dn['ple_norm'] = _matmul_norm_bwd(
        dzg, w['w_ple_gate'], h3, norms['ple_norm'], dh4, mode='nt', name="ple_gate_dx", out_scale=0.5)

    def ffn_bwd(h, dh, dhb, saved, tag, out_scale):
        n, a, b, act = saved
        dw[tag + '_w_out'] = mm(act, dhb, mode='tn', out_dtype=BF16, name=tag + "_out_dw", tm=1408)
        dab = _swiglu_bwd(dhb, w[tag + '_w_out'], a, b, tag + "_out_dx")
        dw[tag + '_w_in'] = ride(tag + "_in_dw", lambda cargo: mm(
            dab, n, mode='tn', out_dtype=BF16, name=tag + "_in_dw", tm=1408, cargo=cargo))
        dh_prev, dhb_prev, dn[tag + '_norm'] = ride(tag + "_in_dx", lambda cargo: _matmul_norm_bwd(
            dab, w[tag + '_w_in'], h, norms[tag + '_norm'], dh, mode='nn', name=tag + "_in_dx", out_scale=out_scale,
            cargo=cargo))
        return dh_prev, dhb_prev

    dh2, dhb2 = ffn_bwd(h2, dh3, dhb3, ffn2_saved, 'ffn2', 1.0)
    dw['w_out'] = mm(merged, dhb2, mode='tn', out_dtype=BF16, name="mix_out_dw")
    dbm, dbs, dgates = _merge_bwd(dhb2, w['w_out'], proj, bm, bs)
    dw['w_branch_mla'] = mm(o_mla, dbm, mode='tn', out_dtype=BF16, name="branch_mla_dw")
    dw['w_branch_sb'] = mm(o_sb, dbs, mode='tn', out_dtype=BF16, name="branch_sb_dw")
    do_mla = mm(dbm, w['w_branch_mla'], mode='nt', out_dtype=BF16, name="branch_mla_dx")
    do_sb = mm(dbs, w['w_branch_sb'], mode='nt', out_dtype=BF16, name="branch_sb_dx")
    dqh, dkh, dvp = ride("mla_bwd", lambda cargo: _mla_bwd(qh, kh, kvb, o_mla, do_mla, lse, cargo))
    dsq, dsk, dsv = _sb_bwd(proj, do_sb, rtot, sb_first)
    dqraw, dkvraw, dkr, dgq, dgk = _headprep_bwd(dqh, dkh, dvp, qraw, kvraw, proj, cosf, sin_a, sin_b, gqh, gkh)
    dn['q_head_norm'], dn['k_head_norm'] = dgq[:, :MLA_QK], dgk[:, :MLA_QK]
    dw['w_q_up'] = mm(dqraw, cqn, mode='tn', out_dtype=BF16, name="q_up_dw")
    dw['w_kv_up'] = mm(ckvn, dkvraw, mode='tn', out_dtype=BF16, name="kv_up_dw")
    dcqn = mm(dqraw, w['w_q_up'], mode='nn', out_dtype=F32, name="q_up_dx")
    dckvn = mm(dkvraw, w['w_kv_up'], mode='nt', out_dtype=F32, name="kv_up_dx")
    dlat, dn['q_latent_norm'], dn['kv_latent_norm'] = _latent_bwd(
        dcqn, dckvn, proj, dkr, norms['q_latent_norm'], norms['kv_latent_norm'])
    dproj = jnp.concatenate([dlat, dgates, dsq.astype(BF16), dsk.astype(BF16), dsv.astype(BF16)], axis=1)
    dw['w_in'] = ride("proj_dw", lambda cargo: mm(dproj, u, mode='tn', out_dtype=BF16, name="proj_dw", tm=1536, cargo=cargo))
    dh1, dhb1, dn['mix_norm'] = ride("proj_dx", lambda cargo: _matmul_norm_bwd(
        dproj, w['w_in'], h1, norms['mix_norm'], dh2, mode='nn', name="proj_dx", out_scale=0.5, cargo=cargo))
    dx, _ = ffn_bwd(x, dh1, dhb1, ffn1_saved, 'ffn1', 1.0)
    return dx, loss_lanes, dw, dn


MESH = pl.DeviceIdType.MESH
HBM_SPEC = pl.BlockSpec(memory_space=pl.ANY)


def _position():
    return lax.axis_index("x"), lax.axis_index("y"), lax.axis_index("c")


def _index(px, py, pc):
    return 4 * px + 2 * py + pc


def _all_gather(shards):
    n = len(shards)

    def body(*refs):
        ins, outs = refs[:n], refs[n:2 * n]
        send_sems, recv_sems, local_sems = refs[2 * n:]
        x, y, c = _position()
        me, sibling = (x, y, c), (x, y, 1 - c)
        chips = [(1 - x, y), (x, 1 - y), (1 - x, 1 - y)]

        def copy(a, k, block, to, own=False):
            dst = outs[a].at[_index(*block)]
            return pltpu.make_async_remote_copy(
                src_ref=ins[a] if own else dst, dst_ref=dst,
                send_sem=send_sems.at[a, k], recv_sem=recv_sems.at[a, k], device_id=to, device_id_type=MESH)

        mine = [pltpu.make_async_copy(ins[a], outs[a].at[_index(*me)], local_sems.at[a]) for a in range(n)]
        for cp in mine:
            cp.start()
        first = []
        for a in range(n):
            first.append(copy(a, 0, me, sibling, own=True))
            first += [copy(a, 1 + j, me, (*chip, c), own=True) for j, chip in enumerate(chips)]
        for cp in first:
            cp.start()
        passed = []
        for j, chip in enumerate(chips):
            for a in range(n):
                copy(a, 1 + j, (*chip, c), me).wait_recv()
                fwd = copy(a, 4 + j, (*chip, c), sibling)
                fwd.start()
                passed.append(fwd)
        for a in range(n):
            copy(a, 0, sibling, me).wait_recv()
            for j, chip in enumerate(chips):
                copy(a, 4 + j, (*chip, 1 - c), me).wait_recv()
        for cp in first + passed:
            cp.wait_send()
        for cp in mine:
            cp.wait()

    return pl.pallas_call(
        body, name="weights_all_gather",
        in_specs=[HBM_SPEC] * n, out_specs=[HBM_SPEC] * n,
        out_shape=[jax.ShapeDtypeStruct((N_DEV,) + s.shape, s.dtype) for s in shards],
        scratch_shapes=[pltpu.SemaphoreType.DMA((n, 7)), pltpu.SemaphoreType.DMA((n, 7)), pltpu.SemaphoreType.DMA((n,))],
    )(*shards)


def _exchange(parts):
    n = len(parts)
    masks = [(mx, my, mc) for mx in (0, 1) for my in (0, 1) for mc in (0, 1)][1:]

    def body(*refs):
        ins, outs = refs[:n], refs[n:2 * n]
        send_sems, recv_sems, local_sems = refs[2 * n:]
        x, y, c = _position()
        me = _index(x, y, c)

        def peer_of(mask):
            mx, my, mc = mask
            return (x + mx - 2 * x * mx, y + my - 2 * y * my, c + mc - 2 * c * mc)

        def copy(a, k):
            peer = peer_of(masks[k])
            return pltpu.make_async_remote_copy(
                src_ref=ins[a].at[_index(*peer)], dst_ref=outs[a].at[me],
                send_sem=send_sems.at[a, k], recv_sem=recv_sems.at[a, k], device_id=peer, device_id_type=MESH)

        def landed(a, k):
            peer = peer_of(masks[k])
            return pltpu.make_async_remote_copy(
                src_ref=ins[a].at[me], dst_ref=outs[a].at[_index(*peer)],
                send_sem=send_sems.at[a, k], recv_sem=recv_sems.at[a, k], device_id=peer, device_id_type=MESH)

        mine = [pltpu.make_async_copy(ins[a].at[me], outs[a].at[me], local_sems.at[a]) for a in range(n)]
        for cp in mine:
            cp.start()
        sent = [copy(a, k) for k in range(7) for a in range(n)]
        for cp in sent:
            cp.start()
        for k in range(7):
            for a in range(n):
                landed(a, k).wait_recv()
        for cp in sent:
            cp.wait_send()
        for cp in mine:
            cp.wait()

    return pl.pallas_call(
        body, name="grads_exchange",
        in_specs=[HBM_SPEC] * n, out_specs=[HBM_SPEC] * n,
        out_shape=[jax.ShapeDtypeStruct(s.shape, s.dtype) for s in parts],
        scratch_shapes=[pltpu.SemaphoreType.DMA((n, 7)), pltpu.SemaphoreType.DMA((n, 7)), pltpu.SemaphoreType.DMA((n,))],
    )(*parts)


PEER_MASKS = [(mx, my, mc) for mx in (0, 1) for my in (0, 1) for mc in (0, 1)][1:]


def _peer(mask):
    x, y, c = _position()
    mx, my, mc = mask
    return (x + mx - 2 * x * mx, y + my - 2 * y * my, c + mc - 2 * c * mc)


class _Cargo:
    def __init__(self, srcs, scatter):
        self.srcs, self.scatter, self.n = list(srcs), scatter, len(srcs)

    def specs(self):
        return [HBM_SPEC] * self.n

    def out_shape(self):
        return [jax.ShapeDtypeStruct(s.shape if self.scatter else (N_DEV,) + s.shape, s.dtype) for s in self.srcs]

    def scratch(self):
        per_copy = pltpu.SemaphoreType.DMA((self.n, len(PEER_MASKS)))
        return [per_copy, per_copy, pltpu.SemaphoreType.DMA((self.n,))]

    def _mine(self, src_refs, a, to):
        return src_refs[a].at[to] if self.scatter else src_refs[a]

    def _shard_copy(self, src_refs, land_refs, sems, a, k, block, to, own=False):
        dst = land_refs[a].at[_index(*block)]
        return pltpu.make_async_remote_copy(
            src_ref=src_refs[a] if own else dst, dst_ref=dst,
            send_sem=sems[0].at[a, k], recv_sem=sems[1].at[a, k], device_id=to, device_id_type=MESH)

    def _first_hops(self, src_refs, land_refs, sems):
        x, y, c = _position()
        chips = [(1 - x, y), (x, 1 - y), (1 - x, 1 - y)]
        hops = []
        for a in range(self.n):
            hops.append(self._shard_copy(src_refs, land_refs, sems, a, 0, (x, y, c), (x, y, 1 - c), own=True))
            hops += [self._shard_copy(src_refs, land_refs, sems, a, 1 + j, (x, y, c), (*chip, c), own=True)
                     for j, chip in enumerate(chips)]
        return hops, chips

    def start(self, src_refs, land_refs, sems):
        send, recv, local = sems
        me = _index(*_position())
        for a in range(self.n):
            pltpu.make_async_copy(self._mine(src_refs, a, me), land_refs[a].at[me], local.at[a]).start()
        if not self.scatter:
            for cp in self._first_hops(src_refs, land_refs, sems)[0]:
                cp.start()
            return
        for k, mask in enumerate(PEER_MASKS):
            peer = _peer(mask)
            for a in range(self.n):
                pltpu.make_async_remote_copy(
                    src_ref=self._mine(src_refs, a, _index(*peer)), dst_ref=land_refs[a].at[me],
                    send_sem=send.at[a, k], recv_sem=recv.at[a, k], device_id=peer, device_id_type=MESH).start()

    def _wait_gathered(self, src_refs, land_refs, sems):
        x, y, c = _position()
        me, sibling = (x, y, c), (x, y, 1 - c)
        first, chips = self._first_hops(src_refs, land_refs, sems)
        passed = []
        for j, chip in enumerate(chips):
            for a in range(self.n):
                self._shard_copy(src_refs, land_refs, sems, a, 1 + j, (*chip, c), me).wait_recv()
                passed.append(self._shard_copy(src_refs, land_refs, sems, a, 4 + j, (*chip, c), sibling))
                passed[-1].start()
        for a in range(self.n):
            self._shard_copy(src_refs, land_refs, sems, a, 0, sibling, me).wait_recv()
            for j, chip in enumerate(chips):
                self._shard_copy(src_refs, land_refs, sems, a, 4 + j, (*chip, 1 - c), me).wait_recv()
        for cp in first + passed:
            cp.wait_send()

    def wait(self, src_refs, land_refs, sems):
        send, recv, local = sems
        me = _index(*_position())
        if not self.scatter:
            self._wait_gathered(src_refs, land_refs, sems)
        for k, mask in enumerate(PEER_MASKS if self.scatter else []):
            peer = _peer(mask)
            there = _index(*peer)
            for a in range(self.n):
                pltpu.make_async_remote_copy(
                    src_ref=self._mine(src_refs, a, me), dst_ref=land_refs[a].at[there],
                    send_sem=send.at[a, k], recv_sem=recv.at[a, k], device_id=peer, device_id_type=MESH).wait_recv()
                pltpu.make_async_remote_copy(
                    src_ref=self._mine(src_refs, a, there), dst_ref=land_refs[a].at[me],
                    send_sem=send.at[a, k], recv_sem=recv.at[a, k], device_id=peer, device_id_type=MESH).wait_send()
        for a in range(self.n):
            pltpu.make_async_copy(self._mine(src_refs, a, me), land_refs[a].at[me], local.at[a]).wait()


def _with_cargo(cargo, refs, n_in, n_out, steps, counts, compute):
    if cargo is None:
        compute(refs)
        return
    n = cargo.n
    src_refs = refs[n_in:n_in + n]
    land_refs = refs[n_in + n + n_out:n_in + 2 * n + n_out]
    sems = refs[-3:]
    first = functools.reduce(jnp.logical_and, [s == 0 for s in steps])
    last = functools.reduce(jnp.logical_and, [s == c - 1 for s, c in zip(steps, counts)])

    @pl.when(first)
    def _():
        cargo.start(src_refs, land_refs, sems)

    compute(refs[:n_in] + refs[n_in + n:n_in + n + n_out] + refs[n_in + 2 * n + n_out:-3])

    @pl.when(last)
    def _():
        cargo.wait(src_refs, land_refs, sems)


def _adamw(parts, w, m, v, name):
    r, c = w.shape
    tr = next((t for t in (512, 384, 352, 256, 128) if r % t == 0), r) if r > 512 else r
    tc = c if tr < r or r <= 512 else 256
    assert r % tr == 0 and c % tc == 0
    bc1 = 1.0 - ADAM_B1 ** ADAM_STEP
    bc2 = 1.0 - ADAM_B2 ** ADAM_STEP

    def body(p_ref, w_ref, m_ref, v_ref, g_ref, d_ref, nm_ref, nv_ref):
        g = p_ref[0].astype(F32)
        for s in range(1, N_DEV):
            g = g + p_ref[s].astype(F32)
        nm = ADAM_B1 * m_ref[...] + (1.0 - ADAM_B1) * g
        nv = ADAM_B2 * v_ref[...] + (1.0 - ADAM_B2) * (g * g)
        g_ref[...] = g
        nm_ref[...] = nm
        nv_ref[...] = nv
        d_ref[...] = -ADAM_LR * ((nm / bc1) / (jnp.sqrt(nv / bc2) + ADAM_EPS) + ADAM_WD * w_ref[...])

    tile = pl.BlockSpec((tr, tc), lambda i, j: (i, j))
    out = jax.ShapeDtypeStruct((r, c), F32)
    return pl.pallas_call(
        body, name=name, grid=(r // tr, c // tc),
        in_specs=[pl.BlockSpec((N_DEV, tr, tc), lambda i, j: (0, i, j)), tile, tile, tile],
        out_specs=[tile] * 4, out_shape=[out] * 4,
        compiler_params=_params(("parallel", "parallel")),
    )(parts, w, m, v)


GATHER_FIRST = ['ffn1_w_in']
RIDES = {
    'ffn1_in_fwd': ('weights', ['ffn1_w_out', 'w_in']),
    'ffn1_out_fwd': ('weights', ['w_q_up', 'w_kv_up', 'w_branch_mla', 'w_branch_sb', 'w_out']),
    'mla_fwd': ('weights', ['ffn2_w_in', 'ffn2_w_out', 'w_ple_gate', 'w_ple_proj']),
    'mla_bwd': ('grads', ['w_ple_gate', 'w_ple_proj', 'ffn2_w_out', 'ffn2_w_in', 'w_out', 'w_branch_mla', 'w_branch_sb']),
    'proj_dw': ('grads', ['w_q_up', 'w_kv_up']),
    'proj_dx': ('grads', ['w_in']),
    'ffn1_in_dw': ('grads', ['ffn1_w_out']),
    'ffn1_in_dx': ('grads', ['ffn1_w_in']),
}


class _Plan:
    def __init__(self, shards):
        self.shards = shards
        self.received = {}

    def first_weights(self):
        gathered = _all_gather([self.shards[n] for n in GATHER_FIRST])
        return {n: _layout_weight(n, g) for n, g in zip(GATHER_FIRST, gathered)}

    def cargo(self, host, dw):
        if host not in RIDES:
            return None
        kind, names = RIDES[host]
        if kind == 'weights':
            return _Cargo([self.shards[n] for n in names], False)
        return _Cargo([_unlayout_grad(n, dw.pop(n)) for n in names], True)

    def landed(self, host, lands):
        if host not in RIDES:
            return {}
        kind, names = RIDES[host]
        if kind == 'weights':
            return {n: _layout_weight(n, land) for n, land in zip(names, lands)}
        self.received.update(zip(names, lands))
        return {}


def _pack_small(vecs):
    flat = jnp.concatenate([v.reshape(-1) for v in vecs])
    return jnp.pad(flat, (0, SMALL_ROWS * 128 - flat.shape[0])).reshape(SMALL_ROWS, 128)


def _unpack_small(packed, sizes):
    flat = packed.reshape(-1)
    out, at = [], 0
    for n in sizes:
        out.append(flat[at:at + n])
        at += n
    return out


def kernel(x, p, positions, ffn1_norm, ffn1_w_in, ffn1_w_out, mix_norm, w_in, q_latent_norm, w_q_up, kv_latent_norm, w_kv_up, q_head_norm, k_head_norm, w_branch_mla, w_branch_sb, w_out, ffn2_norm, ffn2_w_in, ffn2_w_out, ple_norm, w_ple_gate, w_ple_proj, loss_target, m_ffn1_norm, m_ffn1_w_in, m_ffn1_w_out, m_mix_norm, m_w_in, m_q_latent_norm, m_w_q_up, m_kv_latent_norm, m_w_kv_up, m_q_head_norm, m_k_head_norm, m_w_branch_mla, m_w_branch_sb, m_w_out, m_ffn2_norm, m_ffn2_w_in, m_ffn2_w_out, m_ple_norm, m_w_ple_gate, m_w_ple_proj, v_ffn1_norm, v_ffn1_w_in, v_ffn1_w_out, v_mix_norm, v_w_in, v_q_latent_norm, v_w_q_up, v_kv_latent_norm, v_w_kv_up, v_q_head_norm, v_k_head_norm, v_w_branch_mla, v_w_branch_sb, v_w_out, v_ffn2_norm, v_ffn2_w_in, v_ffn2_w_out, v_ple_norm, v_w_ple_gate, v_w_ple_proj):
    given = dict(locals())
    wts = {n: given[n] for n in WEIGHTS}
    mom = {n: given['m_' + n] for n in WEIGHTS}
    var = {n: given['v_' + n] for n in WEIGHTS}

    def local(a, n):
        return jnp.swapaxes(a[0], 0, 1) if n in TRANSPOSED else a[0]

    plan = _Plan({n: local(wts[n], n).astype(BF16) for n in MATS})
    norms = {n: wts[n] for n in NORMS}
    dx, loss_lanes, dw, dn = _local_step(x[0], p[0, 0], positions[0], loss_target[0], norms, plan)
    assert not dw

    out = {}
    for n in MATS:
        res = _adamw(plan.received[n], local(wts[n], n), local(mom[n], n), local(var[n], n), "adamw_" + n)
        out[n] = [local(r[None], n)[None] for r in res]
    small = _pack_small([dn[n] for n in NORMS] + [0.5 / D_MODEL * jnp.sum(loss_lanes)[None]])
    small_parts = _exchange([jnp.broadcast_to(small[None], (N_DEV, SMALL_ROWS, 128))])[0]
    sizes = [wts[n].shape[1] for n in NORMS]
    pack = lambda d: _pack_small([d[n] for n in NORMS])
    small_res = _adamw(small_parts, pack(wts), pack(mom), pack(var), "adamw_norms")
    loss = small_res[0].reshape(-1)[sum(sizes)]
    for i, res in enumerate(small_res):
        for n, vec in zip(NORMS, _unpack_small(res, sizes)):
            out.setdefault(n, [None] * 4)[i] = vec[None]

    return (loss, dx[None], *[out[n][0] for n in WEIGHTS], *[out[n][1] for n in WEIGHTS],
            *[out[n][2] for n in WEIGHTS], *[out[n][3] for n in WEIGHTS])
```

```python
import functools
import math

import jax
import jax.numpy as jnp
from jax import lax
from jax.experimental import pallas as pl
from jax.experimental.pallas import tpu as pltpu

F32 = jnp.float32
BF16 = jnp.bfloat16

N_DEV = 8
D_MODEL = 1024
D_FF = 2816
PLE_DIM = 256
NORM_EPS = 1e-6
N_HEADS = 8
HEAD_PAD = 128
MLA_NOPE = 64
MLA_ROPE = 32
MLA_QK = 96
Q_LORA = 384
KV_LORA = 256
SB_DIM = 64
SB_WIDTH = 512
ROPE_BASE = 10000.0
IN_COLS = 4256

PROJ_W = 4608
P_CQ, P_CKV, P_KR, P_GM, P_GS, P_SBQ, P_SBK, P_SBV = 0, 384, 640, 1024, 2048, 3072, 3584, 4096

ADAM_LR, ADAM_B1, ADAM_B2, ADAM_EPS, ADAM_WD, ADAM_STEP = 0.001, 0.9, 0.999, 1e-08, 0.01, 10

VMEM_LIMIT = 52 * 1024 * 1024
MATMUL_VMEM = 40 * 1024 * 1024

WEIGHTS = ['ffn1_norm', 'ffn1_w_in', 'ffn1_w_out', 'mix_norm', 'w_in', 'q_latent_norm', 'w_q_up',
           'kv_latent_norm', 'w_kv_up', 'q_head_norm', 'k_head_norm', 'w_branch_mla', 'w_branch_sb',
           'w_out', 'ffn2_norm', 'ffn2_w_in', 'ffn2_w_out', 'ple_norm', 'w_ple_gate', 'w_ple_proj']
NORMS = ['ffn1_norm', 'mix_norm', 'q_latent_norm', 'kv_latent_norm', 'q_head_norm', 'k_head_norm',
         'ffn2_norm', 'ple_norm']
MATS = [n for n in WEIGHTS if n not in NORMS]
SMALL_ROWS = 48

NT_DIMS = (((1,), (1,)), ((), ()))
NN_DIMS = (((1,), (0,)), ((), ()))
TN_DIMS = (((0,), (0,)), ((), ()))


def _params(sem=None, vmem=VMEM_LIMIT):
    return pltpu.CompilerParams(dimension_semantics=sem, vmem_limit_bytes=vmem)


def _pick(n, cap):
    if n <= cap:
        return n
    best = None
    for t in range(128, cap + 1, 128):
        if n % t == 0:
            best = t
    assert best is not None, (n, cap)
    return best


def _dot(a, b, dims):
    return lax.dot_general(a, b, dims, preferred_element_type=F32)


def _matmul(a, b, *, mode, out_dtype, name, tm=None, tn=None, tk=None, res=None, alpha=1.0, cargo=None):
    if mode == 'nn':
        (m, k), (k2, n) = a.shape, b.shape
    elif mode == 'nt':
        (m, k), (n, k2) = a.shape, b.shape
    else:
        (k, m), (k2, n) = a.shape, b.shape
    assert k == k2, (name, a.shape, b.shape)
    has_res = res is not None
    tn = tn or _pick(n, 512)

    def vmem(tm_, tk_):
        io = 2 * 2 * (tm_ * tk_ + tk_ * tn) + 2 * tm_ * tn * (jnp.dtype(out_dtype).itemsize + 4 * has_res)
        return io + (4 * tm_ * tn if tk_ < k else 0)

    tries = [(tm_, tk_) for tk_ in ([tk] if tk else [k, _pick(k, 2048)])
             for tm_ in ([tm] if tm else [_pick(m, 2048), _pick(m, 1024), _pick(m, 512)])]
    tm, tk = next((c for c in tries if vmem(*c) <= MATMUL_VMEM), tries[-1])
    assert m % tm == 0 and n % tn == 0 and k % tk == 0, (name, m, n, k, tm, tn, tk)
    nk = k // tk
    dims = {'nn': NN_DIMS, 'nt': NT_DIMS, 'tn': TN_DIMS}[mode]

    def epilogue(acc, r_ref, o_ref):
        if alpha != 1.0:
            acc = acc * alpha
        if has_res:
            acc = r_ref[...] + acc
        o_ref[...] = acc.astype(out_dtype)

    grid = (m // tm, n // tn, nk)

    def body(*refs):
        steps = [pl.program_id(d) for d in range(3)]

        def compute(own):
            a_ref, b_ref = own[0], own[1]
            r_ref = own[2] if has_res else None
            o_ref = own[2 + has_res]
            if nk == 1:
                epilogue(_dot(a_ref[...], b_ref[...], dims), r_ref, o_ref)
                return
            acc_ref = own[-1]

            @pl.when(steps[2] == 0)
            def _():
                acc_ref[...] = jnp.zeros_like(acc_ref)

            acc_ref[...] += _dot(a_ref[...], b_ref[...], dims)

            @pl.when(steps[2] == nk - 1)
            def _():
                epilogue(acc_ref[...], r_ref, o_ref)

        _with_cargo(cargo, refs, 2 + has_res, 1, steps, grid, compute)

    if mode == 'tn':
        a_spec = pl.BlockSpec((tk, tm), lambda i, j, kk: (kk, i))
    else:
        a_spec = pl.BlockSpec((tm, tk), lambda i, j, kk: (i, kk))
    if mode == 'nt':
        b_spec = pl.BlockSpec((tn, tk), lambda i, j, kk: (j, kk))
    else:
        b_spec = pl.BlockSpec((tk, tn), lambda i, j, kk: (kk, j))
    o_spec = pl.BlockSpec((tm, tn), lambda i, j, kk: (i, j))
    in_specs = [a_spec, b_spec] + ([o_spec] if has_res else [])
    args = (a, b) + ((res,) if has_res else ())
    out_shape = jax.ShapeDtypeStruct((m, n), out_dtype)
    scratch = [pltpu.VMEM((tm, tn), F32)] if nk > 1 else []
    if cargo is None:
        return pl.pallas_call(
            body, name=name, grid=grid, in_specs=in_specs, out_specs=o_spec, out_shape=out_shape,
            scratch_shapes=scratch, compiler_params=_params(("parallel", "parallel", "arbitrary")),
        )(*args)
    outs = pl.pallas_call(
        body, name=name, grid=grid, in_specs=in_specs + cargo.specs(), out_specs=[o_spec] + cargo.specs(),
        out_shape=[out_shape] + cargo.out_shape(), scratch_shapes=scratch + cargo.scratch(),
        compiler_params=_params(("arbitrary", "arbitrary", "arbitrary")),
    )(*args, *cargo.srcs)
    return outs[0], list(outs[1:])


def _row_tile(t, cap=512):
    return min(t, cap)


def _rms(x, width):
    return lax.rsqrt(jnp.sum(x * x, axis=-1, keepdims=True) * (1.0 / width) + NORM_EPS)


def _rmsnorm_fwd(x, g, name):
    t, d = x.shape
    tr = _row_tile(t)

    def body(x_ref, g_ref, o_ref):
        xv = x_ref[...]
        o_ref[...] = ((xv * _rms(xv, d)) * g_ref[...]).astype(BF16)

    return pl.pallas_call(
        body, name=name, grid=(t // tr,),
        in_specs=[pl.BlockSpec((tr, d), lambda i: (i, 0)), pl.BlockSpec((1, d), lambda i: (0, 0))],
        out_specs=pl.BlockSpec((tr, d), lambda i: (i, 0)),
        out_shape=jax.ShapeDtypeStruct((t, d), BF16),
        compiler_params=_params(("parallel",)),
    )(x, g)


def _matmul_norm_bwd(a, b, x, g, dh_in, *, mode, name, out_scale, cargo=None):
    m, k = a.shape
    d = x.shape[1]
    tn = _pick(d, 512)

    def vmem(tm_):
        return 2 * 2 * (tm_ * k + k * tn) + tm_ * d * (4 + 2 * (4 + 4) + 2 * (4 + 2))

    tm = next((c for c in (_pick(m, 1024), _pick(m, 512), _pick(m, 256)) if vmem(c) <= MATMUL_VMEM), _pick(m, 256))
    grid = (m // tm, d // tn)
    dims = {'nn': NN_DIMS, 'nt': NT_DIMS}[mode]

    def body(*refs):
        steps = [pl.program_id(0), pl.program_id(1)]

        def compute(own):
            a_ref, b_ref, x_ref, g_ref, dhin_ref, dh_ref, dhb_ref, dg_ref, dn_ref = own
            for jj in range(grid[1]):
                @pl.when(steps[1] == jj)
                def _(jj=jj):
                    dn_ref[:, jj * tn:(jj + 1) * tn] = _dot(a_ref[...], b_ref[...], dims)

            @pl.when(steps[1] == grid[1] - 1)
            def _():
                xv = x_ref[...]
                dnv = dn_ref[...]
                r = _rms(xv, d)
                y = xv * r
                dy = dnv * g_ref[...]
                dh = dhin_ref[...] + r * (dy - y * (jnp.sum(dy * y, axis=-1, keepdims=True) * (1.0 / d)))
                dh_ref[...] = dh
                dhb_ref[...] = (dh * out_scale).astype(BF16)
                part = jnp.sum(dnv * y, axis=0, keepdims=True)

                @pl.when(steps[0] == 0)
                def _():
                    dg_ref[...] = part

                @pl.when(steps[0] > 0)
                def _():
                    dg_ref[...] += part

        _with_cargo(cargo, refs, 5, 3, steps, grid, compute)

    b_spec = pl.BlockSpec((k, tn), lambda i, j: (0, j)) if mode == 'nn' else pl.BlockSpec((tn, k), lambda i, j: (j, 0))
    row = pl.BlockSpec((tm, d), lambda i, j: (i, 0))
    vec = pl.BlockSpec((1, d), lambda i, j: (0, 0))
    extra = cargo.specs() if cargo else []
    outs = pl.pallas_call(
        body, name=name, grid=grid,
        in_specs=[pl.BlockSpec((tm, k), lambda i, j: (i, 0)), b_spec, row, vec, row] + extra,
        out_specs=[row, row, vec] + extra,
        out_shape=[jax.ShapeDtypeStruct((m, d), F32), jax.ShapeDtypeStruct((m, d), BF16),
                   jax.ShapeDtypeStruct((1, d), F32)] + (cargo.out_shape() if cargo else []),
        scratch_shapes=[pltpu.VMEM((tm, d), F32)] + (cargo.scratch() if cargo else []),
        compiler_params=_params(("arbitrary", "arbitrary")),
    )(a, b, x, g, dh_in, *(cargo.srcs if cargo else []))
    return (outs[0], outs[1], outs[2], list(outs[3:])) if cargo else outs


def _sigmoid(x):
    return 1.0 / (1.0 + jnp.exp(-x))


SWIGLU_CHUNK = 256
SWIGLU_COLS = 1408


def _chunks(width):
    return [slice(lo, min(lo + SWIGLU_CHUNK, width)) for lo in range(0, width, SWIGLU_CHUNK)]


def _swiglu_fwd(h, g, wt_in, name, cargo=None):
    t = h.shape[0]
    tm = _pick(t, 1024)
    grid = (t // tm, D_FF // SWIGLU_COLS)

    def body(*refs):
        steps = [pl.program_id(0), pl.program_id(1)]

        def compute(own):
            h_ref, g_ref, wa_ref, wb_ref, n_ref, a_ref, b_ref, act_ref, n_sc = own

            @pl.when(steps[1] == 0)
            def _():
                xv = h_ref[...]
                n_sc[...] = ((xv * _rms(xv, D_MODEL)) * g_ref[...]).astype(BF16)
                n_ref[...] = n_sc[...]

            nv = n_sc[...]
            for cols in _chunks(SWIGLU_COLS):
                a = _dot(nv, wa_ref[cols, :], NT_DIMS)
                b = _dot(nv, wb_ref[cols, :], NT_DIMS)
                a_ref[:, cols] = a.astype(BF16)
                b_ref[:, cols] = b.astype(BF16)
                act_ref[:, cols] = (a * _sigmoid(a) * b).astype(BF16)

        _with_cargo(cargo, refs, 4, 4, steps, grid, compute)

    half = D_FF // SWIGLU_COLS
    row = pl.BlockSpec((tm, D_MODEL), lambda i, j: (i, 0))
    tile = pl.BlockSpec((tm, SWIGLU_COLS), lambda i, j: (i, j))
    out = jax.ShapeDtypeStruct((t, D_FF), BF16)
    extra = cargo.specs() if cargo else []
    outs = pl.pallas_call(
        body, name=name, grid=grid,
        in_specs=[row, pl.BlockSpec((1, D_MODEL), lambda i, j: (0, 0)),
                  pl.BlockSpec((SWIGLU_COLS, D_MODEL), lambda i, j: (j, 0)),
                  pl.BlockSpec((SWIGLU_COLS, D_MODEL), lambda i, j: (half + j, 0))] + extra,
        out_specs=[row, tile, tile, tile] + extra,
        out_shape=[jax.ShapeDtypeStruct((t, D_MODEL), BF16), out, out, out] + (cargo.out_shape() if cargo else []),
        scratch_shapes=[pltpu.VMEM((tm, D_MODEL), BF16)] + (cargo.scratch() if cargo else []),
        compiler_params=_params(("arbitrary", "arbitrary")),
    )(h, g, wt_in, wt_in, *(cargo.srcs if cargo else []))
    return (outs[0], outs[1], outs[2], outs[3], list(outs[4:])) if cargo else outs


def _swiglu_bwd(dh, w_out, a, b, name):
    t = a.shape[0]
    tr = _row_tile(t, 512)

    def body(d_ref, w_ref, a_ref, b_ref, o_ref):
        dhv = d_ref[...]
        for cols in _chunks(D_FF):
            dv = _dot(dhv, w_ref[cols, :], NT_DIMS)
            av = a_ref[:, cols].astype(F32)
            s = _sigmoid(av)
            o_ref[:, cols] = (dv * b_ref[:, cols].astype(F32) * s * (1.0 + av * (1.0 - s))).astype(BF16)
            o_ref[:, slice(D_FF + cols.start, D_FF + cols.stop)] = (dv * av * s).astype(BF16)

    row = pl.BlockSpec((tr, D_FF), lambda i: (i, 0))
    return pl.pallas_call(
        body, name=name, grid=(t // tr,),
        in_specs=[pl.BlockSpec((tr, D_MODEL), lambda i: (i, 0)), pl.BlockSpec((D_FF, D_MODEL), lambda i: (0, 0)), row, row],
        out_specs=pl.BlockSpec((tr, 2 * D_FF), lambda i: (i, 0)),
        out_shape=jax.ShapeDtypeStruct((t, 2 * D_FF), BF16),
        compiler_params=_params(("parallel",)),
    )(dh, w_out, a, b)


def _latent_fwd(proj, gq, gkv):
    t = proj.shape[0]
    tr = _row_tile(t)

    def body(p_ref, gq_ref, gkv_ref, cq_ref, ckv_ref):
        cq = p_ref[:, P_CQ:P_CQ + Q_LORA].astype(F32)
        ckv = p_ref[:, P_CKV:P_CKV + KV_LORA].astype(F32)
        cq_ref[...] = ((cq * _rms(cq, Q_LORA)) * gq_ref[...]).astype(BF16)
        ckv_ref[...] = ((ckv * _rms(ckv, KV_LORA)) * gkv_ref[...]).astype(BF16)

    return pl.pallas_call(
        body, name="latent_fwd", grid=(t // tr,),
        in_specs=[pl.BlockSpec((tr, 1024), lambda i: (i, 0)), pl.BlockSpec((1, Q_LORA), lambda i: (0, 0)),
                  pl.BlockSpec((1, KV_LORA), lambda i: (0, 0))],
        out_specs=[pl.BlockSpec((tr, Q_LORA), lambda i: (i, 0)), pl.BlockSpec((tr, KV_LORA), lambda i: (i, 0))],
        out_shape=[jax.ShapeDtypeStruct((t, Q_LORA), BF16), jax.ShapeDtypeStruct((t, KV_LORA), BF16)],
        compiler_params=_params(("parallel",)),
    )(proj, gq, gkv)


def _latent_bwd(dcqn, dckvn, proj, dkr, gq, gkv):
    t = proj.shape[0]
    tr = _row_tile(t, 256)

    def norm_bwd(dn, x, g, width):
        r = _rms(x, width)
        y = x * r
        dy = dn * g
        dx = r * (dy - y * (jnp.sum(dy * y, axis=-1, keepdims=True) * (1.0 / width)))
        return dx, jnp.sum(dn * y, axis=0, keepdims=True)

    def body(dcq_ref, dckv_ref, p_ref, dkr_ref, gq_ref, gkv_ref, o_ref, dgq_ref, dgkv_ref):
        i = pl.program_id(0)
        dcq, pq = norm_bwd(dcq_ref[...], p_ref[:, P_CQ:P_CQ + Q_LORA].astype(F32), gq_ref[...], Q_LORA)
        dckv, pkv = norm_bwd(dckv_ref[...], p_ref[:, P_CKV:P_CKV + KV_LORA].astype(F32), gkv_ref[...], KV_LORA)
        o_ref[:, P_CQ:P_CQ + Q_LORA] = dcq.astype(BF16)
        o_ref[:, P_CKV:P_CKV + KV_LORA] = dckv.astype(BF16)
        o_ref[:, P_KR:P_KR + 128] = dkr_ref[...].astype(BF16)
        o_ref[:, P_KR + 128:1024] = jnp.zeros((tr, 1024 - P_KR - 128), BF16)

        @pl.when(i == 0)
        def _():
            dgq_ref[...] = pq
            dgkv_ref[...] = pkv

        @pl.when(i > 0)
        def _():
            dgq_ref[...] += pq
            dgkv_ref[...] += pkv

    def row(w):
        return pl.BlockSpec((tr, w), lambda i: (i, 0))

    def vec(w):
        return pl.BlockSpec((1, w), lambda i: (0, 0))

    return pl.pallas_call(
        body, name="latent_bwd", grid=(t // tr,),
        in_specs=[row(Q_LORA), row(KV_LORA), row(1024), row(128), vec(Q_LORA), vec(KV_LORA)],
        out_specs=[row(1024), vec(Q_LORA), vec(KV_LORA)],
        out_shape=[jax.ShapeDtypeStruct((t, 1024), BF16), jax.ShapeDtypeStruct((1, Q_LORA), F32),
                   jax.ShapeDtypeStruct((1, KV_LORA), F32)],
        compiler_params=_params(("arbitrary",)),
    )(dcqn, dckvn, proj, dkr, gq, gkv)


def _rope(y, cosf, sin_a, sin_b):
    return y * cosf + pltpu.roll(y, 112, 1) * sin_a + pltpu.roll(y, 16, 1) * sin_b


def _rope_t(d, cosf, sin_a, sin_b):
    return d * cosf + pltpu.roll(d * sin_a, 16, 1) + pltpu.roll(d * sin_b, 112, 1)


def _headprep_fwd(qraw, kvraw, proj, cosf, sin_a, sin_b, gqh, gkh):
    t = qraw.shape[0]
    tr = _row_tile(t, 256)

    def body(q_ref, kv_ref, kr_ref, c_ref, sa_ref, sb_ref, gq_ref, gk_ref, qh_ref, kh_ref, kvb_ref):
        cv, sa, sb = c_ref[...], sa_ref[...], sb_ref[...]
        kr = kr_ref[...].astype(F32)
        lane = lax.broadcasted_iota(jnp.int32, (tr, HEAD_PAD), 1)
        for h in range(N_HEADS):
            cols = slice(h * HEAD_PAD, (h + 1) * HEAD_PAD)
            xq = q_ref[:, cols]
            yq = (xq * _rms(xq, MLA_QK)) * gq_ref[...]
            qh_ref[:, cols] = (_rope(yq, cv, sa, sb) * MLA_Q_SCALE).astype(BF16)
            kvh = kv_ref[:, cols]
            kvb_ref[:, cols] = jnp.where(lane < MLA_NOPE, 1.0, kvh).astype(BF16)
            xk = jnp.where(lane < MLA_NOPE, kvh, kr)
            yk = (xk * _rms(xk, MLA_QK)) * gk_ref[...]
            kh_ref[:, cols] = _rope(yk, cv, sa, sb).astype(BF16)

    wide = pl.BlockSpec((tr, 1024), lambda i: (i, 0))
    lanes = pl.BlockSpec((tr, HEAD_PAD), lambda i: (i, 0))
    vec = pl.BlockSpec((1, HEAD_PAD), lambda i: (0, 0))
    return pl.pallas_call(
        body, name="headprep_fwd", grid=(t // tr,),
        in_specs=[wide, wide, pl.BlockSpec((tr, HEAD_PAD), lambda i: (i, P_KR // HEAD_PAD)), lanes, lanes, lanes, vec, vec],
        out_specs=[wide, wide, wide],
        out_shape=[jax.ShapeDtypeStruct((t, 1024), BF16)] * 3,
        compiler_params=_params(("parallel",)),
    )(qraw, kvraw, proj, cosf, sin_a, sin_b, gqh, gkh)


def _headprep_bwd(dqh, dkh, dvp, qraw, kvraw, proj, cosf, sin_a, sin_b, gqh, gkh):
    t = qraw.shape[0]
    tr = _row_tile(t, 256)

    def norm_bwd(dn, x, g):
        r = _rms(x, MLA_QK)
        y = x * r
        dy = dn * g
        dx = r * (dy - y * (jnp.sum(dy * y, axis=-1, keepdims=True) * (1.0 / MLA_QK)))
        return dx, jnp.sum(dn * y, axis=0, keepdims=True)

    def body(dq_ref, dk_ref, dv_ref, q_ref, kv_ref, kr_ref, c_ref, sa_ref, sb_ref, gq_ref, gk_ref,
             dqr_ref, dkvr_ref, dkr_ref, dgq_ref, dgk_ref):
        i = pl.program_id(0)
        cv, sa, sb = c_ref[...], sa_ref[...], sb_ref[...]
        kr = kr_ref[...].astype(F32)
        lane = lax.broadcasted_iota(jnp.int32, (tr, HEAD_PAD), 1)
        dkr = jnp.zeros((tr, HEAD_PAD), F32)
        pq = jnp.zeros((1, HEAD_PAD), F32)
        pk = jnp.zeros((1, HEAD_PAD), F32)
        for h in range(N_HEADS):
            cols = slice(h * HEAD_PAD, (h + 1) * HEAD_PAD)
            dxq, pqh = norm_bwd(_rope_t(dq_ref[:, cols], cv, sa, sb), q_ref[:, cols], gq_ref[...])
            dqr_ref[:, cols] = dxq.astype(BF16)
            pq = pq + pqh
            kvh = kv_ref[:, cols]
            xk = jnp.where(lane < MLA_NOPE, kvh, kr)
            dxk, pkh = norm_bwd(_rope_t(dk_ref[:, cols], cv, sa, sb), xk, gk_ref[...])
            pk = pk + pkh
            dkvr_ref[:, cols] = jnp.where(lane < MLA_NOPE, dxk, dv_ref[:, cols]).astype(BF16)
            dkr = dkr + jnp.where(lane < MLA_NOPE, 0.0, dxk)
        dkr_ref[...] = dkr

        @pl.when(i == 0)
        def _():
            dgq_ref[...] = pq
            dgk_ref[...] = pk

        @pl.when(i > 0)
        def _():
            dgq_ref[...] += pq
            dgk_ref[...] += pk

    wide = pl.BlockSpec((tr, 1024), lambda i: (i, 0))
    lanes = pl.BlockSpec((tr, HEAD_PAD), lambda i: (i, 0))
    vec = pl.BlockSpec((1, HEAD_PAD), lambda i: (0, 0))
    return pl.pallas_call(
        body, name="headprep_bwd", grid=(t // tr,),
        in_specs=[wide, wide, wide, wide, wide, pl.BlockSpec((tr, HEAD_PAD), lambda i: (i, P_KR // HEAD_PAD)),
                  lanes, lanes, lanes, vec, vec],
        out_specs=[wide, wide, lanes, vec, vec],
        out_shape=[jax.ShapeDtypeStruct((t, 1024), BF16), jax.ShapeDtypeStruct((t, 1024), BF16),
                   jax.ShapeDtypeStruct((t, HEAD_PAD), F32), jax.ShapeDtypeStruct((1, HEAD_PAD), F32),
                   jax.ShapeDtypeStruct((1, HEAD_PAD), F32)],
        compiler_params=_params(("arbitrary",)),
    )(dqh, dkh, dvp, qraw, kvraw, proj, cosf, sin_a, sin_b, gqh, gkh)


def _merge_fwd(o_mla, w_mla, o_sb, w_sb, proj):
    t = proj.shape[0]
    tr = _row_tile(t, 512)

    def body(om_ref, wm_ref, os_ref, ws_ref, gm_ref, gs_ref, o_ref, bm_ref, bs_ref):
        omv, osv = om_ref[...], os_ref[...]
        for cols in _chunks(D_MODEL):
            bm = _dot(omv, wm_ref[:, cols], NN_DIMS)
            bs = _dot(osv, ws_ref[:, cols], NN_DIMS)
            bm_ref[:, cols] = bm
            bs_ref[:, cols] = bs
            gm = _sigmoid(gm_ref[:, cols].astype(F32))
            gs = _sigmoid(gs_ref[:, cols].astype(F32))
            o_ref[:, cols] = (gm * bm + gs * bs).astype(BF16)

    row = pl.BlockSpec((tr, 1024), lambda i: (i, 0))
    f32 = jax.ShapeDtypeStruct((t, 1024), F32)
    return pl.pallas_call(
        body, name="merge_fwd", grid=(t // tr,),
        in_specs=[row, pl.BlockSpec(w_mla.shape, lambda i: (0, 0)),
                  pl.BlockSpec((tr, SB_WIDTH), lambda i: (i, 0)), pl.BlockSpec(w_sb.shape, lambda i: (0, 0)),
                  pl.BlockSpec((tr, 1024), lambda i: (i, P_GM // 1024)), pl.BlockSpec((tr, 1024), lambda i: (i, P_GS // 1024))],
        out_specs=[row, row, row], out_shape=[jax.ShapeDtypeStruct((t, 1024), BF16), f32, f32],
        compiler_params=_params(("parallel",)),
    )(o_mla, w_mla, o_sb, w_sb, proj, proj)


def _merge_bwd(dh, w_out, proj, bm, bs):
    t = proj.shape[0]
    tr = _row_tile(t, 512)

    def body(d_ref, w_ref, gm_ref, gs_ref, bm_ref, bs_ref, dbm_ref, dbs_ref, dg_ref):
        dhv = d_ref[...]
        for cols in _chunks(D_MODEL):
            dm = _dot(dhv, w_ref[cols, :], NT_DIMS)
            gm = _sigmoid(gm_ref[:, cols].astype(F32))
            gs = _sigmoid(gs_ref[:, cols].astype(F32))
            dbm_ref[:, cols] = (dm * gm).astype(BF16)
            dbs_ref[:, cols] = (dm * gs).astype(BF16)
            dg_ref[:, cols] = (dm * bm_ref[:, cols] * gm * (1.0 - gm)).astype(BF16)
            dg_ref[:, slice(D_MODEL + cols.start, D_MODEL + cols.stop)] = (dm * bs_ref[:, cols] * gs * (1.0 - gs)).astype(BF16)

    row = pl.BlockSpec((tr, 1024), lambda i: (i, 0))
    return pl.pallas_call(
        body, name="mix_out_dx", grid=(t // tr,),
        in_specs=[row, pl.BlockSpec((D_MODEL, D_MODEL), lambda i: (0, 0)),
                  pl.BlockSpec((tr, 1024), lambda i: (i, P_GM // 1024)),
                  pl.BlockSpec((tr, 1024), lambda i: (i, P_GS // 1024)), row, row],
        out_specs=[row, row, pl.BlockSpec((tr, 2048), lambda i: (i, 0))],
        out_shape=[jax.ShapeDtypeStruct((t, 1024), BF16), jax.ShapeDtypeStruct((t, 1024), BF16),
                   jax.ShapeDtypeStruct((t, 2048), BF16)],
        compiler_params=_params(("parallel",)),
    )(dh, w_out, proj, proj, bm, bs)


def _ple_loss(h3, g, w_gate, pb, w_proj, tgt):
    t = h3.shape[0]
    tr = _row_tile(t, 512)

    def body(h_ref, g_ref, wg_ref, p_ref, wp_ref, t_ref, n_ref, dh_ref, dz_ref, dp_ref, l_ref):
        i = pl.program_id(0)
        xv = h_ref[...]
        nv = ((xv * _rms(xv, D_MODEL)) * g_ref[...]).astype(BF16)
        n_ref[...] = nv
        pv = p_ref[...]
        part = jnp.zeros((1, 128), F32)
        for cols in _chunks(D_MODEL):
            pg = _sigmoid(_dot(nv, wg_ref[:, cols], NN_DIMS))
            ppv = _dot(pv, wp_ref[:, cols], NN_DIMS)
            diff = (h_ref[:, cols] + pg * ppv) - t_ref[:, cols]
            dh = diff * (1.0 / D_MODEL)
            dh_ref[:, cols] = dh
            dp_ref[:, cols] = (dh * pg).astype(BF16)
            dz_ref[:, cols] = (dh * ppv * pg * (1.0 - pg)).astype(BF16)
            sq = jnp.sum(diff * diff, axis=0, keepdims=True)
            for c in range(sq.shape[1] // 128):
                part = part + sq[:, c * 128:(c + 1) * 128]

        @pl.when(i == 0)
        def _():
            l_ref[...] = part

        @pl.when(i > 0)
        def _():
            l_ref[...] += part

    row = pl.BlockSpec((tr, 1024), lambda i: (i, 0))
    return pl.pallas_call(
        body, name="ple_loss", grid=(t // tr,),
        in_specs=[row, pl.BlockSpec((1, D_MODEL), lambda i: (0, 0)), pl.BlockSpec((D_MODEL, D_MODEL), lambda i: (0, 0)),
                  pl.BlockSpec((tr, PLE_DIM), lambda i: (i, 0)), pl.BlockSpec((PLE_DIM, D_MODEL), lambda i: (0, 0)), row],
        out_specs=[row, row, row, row, pl.BlockSpec((1, 128), lambda i: (0, 0))],
        out_shape=[jax.ShapeDtypeStruct((t, 1024), BF16), jax.ShapeDtypeStruct((t, 1024), F32),
                   jax.ShapeDtypeStruct((t, 1024), BF16), jax.ShapeDtypeStruct((t, 1024), BF16),
                   jax.ShapeDtypeStruct((1, 128), F32)],
        compiler_params=_params(("arbitrary",)),
    )(h3, g, w_gate, pb, w_proj, tgt)


ATT_BLOCK = 256
MLA_Q_SCALE = math.log2(math.e) / math.sqrt(MLA_QK)
MLA_Q_BLOCK = 512
MLA_FWD_COLS = 4
MLA_BWD_COLS = 4
SB_FWD_COLS = 4
SB_BWD_COLS = 2
SB_BLOCK = 256


def _split_bf16(x):
    hi = x.astype(BF16)
    return hi, (x - hi.astype(F32)).astype(BF16)


def _tri(kind, n):
    r = lax.broadcasted_iota(jnp.int32, (n, n), 0)
    c = lax.broadcasted_iota(jnp.int32, (n, n), 1)
    cond = {'gt': r > c, 'le': r <= c, 'lt': r < c}[kind]
    return jnp.where(cond, 1.0, 0.0).astype(BF16)


def _causal(strict, n=ATT_BLOCK):
    r = lax.broadcasted_iota(jnp.int32, (n, n), 0)
    c = lax.broadcasted_iota(jnp.int32, (n, n), 1)
    return (c < r) if strict else (c <= r)


def _below_diagonal(rows, offset):
    r = lax.broadcasted_iota(jnp.int32, (rows, ATT_BLOCK), 0)
    c = lax.broadcasted_iota(jnp.int32, (rows, ATT_BLOCK), 1)
    return c + offset <= r


def _lanes(c):
    return slice(c * HEAD_PAD, (c + 1) * HEAD_PAD)


def _row_block(j, n=ATT_BLOCK):
    return pl.ds(pl.multiple_of(j * n, n), n)


def _rows(ref, j, c, n=ATT_BLOCK):
    return ref[_row_block(j, n), _lanes(c)]


def _mla_fwd(qh, kh, kvb, cargo=None):
    t = qh.shape[0]
    bq = min(MLA_Q_BLOCK, t)
    per_q = bq // ATT_BLOCK
    ncol = MLA_FWD_COLS
    grid = (N_HEADS // ncol, t // bq)

    def body(*refs):
        steps = [pl.program_id(0), pl.program_id(1)]
        _with_cargo(cargo, refs, 3, 2, steps, grid, lambda own: work(steps[1], *own))

    def work(i, q_ref, k_ref, v_ref, o_ref, lse_ref):
        qs = [q_ref[:, _lanes(c)] for c in range(ncol)]

        def step(j, carry, masked):
            cols = range(ncol)
            scores = [_dot(qs[c], _rows(k_ref, j, c), NT_DIMS) for c in cols]
            ms, ps, alphas = [], [], []
            for c in cols:
                s = scores[c]
                if masked is not None:
                    s = jnp.where(_below_diagonal(bq, masked), s, -1e30)
                m_new = jnp.maximum(carry[c][0], jnp.max(s, axis=-1, keepdims=True))
                ps.append(jnp.exp2(s - m_new).astype(BF16))
                alphas.append(jnp.exp2(carry[c][0] - m_new))
                ms.append(m_new)
            return tuple((ms[c], alphas[c] * carry[c][1] + _dot(ps[c], _rows(v_ref, j, c), NN_DIMS)) for c in cols)

        init = tuple((jnp.full((bq, 1), -1e30, F32), jnp.zeros((bq, HEAD_PAD), F32)) for _ in range(ncol))
        carry = lax.fori_loop(0, i * per_q, lambda j, cr: step(j, cr, None), init)
        for d in range(per_q):
            carry = step(i * per_q + d, carry, d * ATT_BLOCK)
        for c, (m, acc) in enumerate(carry):
            l = acc[:, 0:1]
            o_ref[:, _lanes(c)] = (acc / l).astype(BF16)
            lse_ref[c] = m + jnp.log2(l)

    width = ncol * HEAD_PAD
    full = pl.BlockSpec((t, width), lambda h, i: (0, h))
    blk = pl.BlockSpec((bq, width), lambda h, i: (i, h))
    extra = cargo.specs() if cargo else []
    outs = pl.pallas_call(
        body, name="mla_fwd", grid=grid,
        in_specs=[blk, full, full] + extra,
        out_specs=[blk, pl.BlockSpec((ncol, bq, 1), lambda h, i: (h, i, 0))] + extra,
        out_shape=[jax.ShapeDtypeStruct((t, N_HEADS * HEAD_PAD), BF16), jax.ShapeDtypeStruct((N_HEADS, t, 1), F32)]
        + (cargo.out_shape() if cargo else []),
        scratch_shapes=cargo.scratch() if cargo else [],
        compiler_params=_params(("arbitrary", "arbitrary")),
    )(qh, kh, kvb, *(cargo.srcs if cargo else []))
    return (outs[0], outs[1], list(outs[2:])) if cargo else outs


def _mla_bwd(qh, kh, kvb, o, do, lse, cargo=None):
    t = qh.shape[0]
    bq = min(MLA_Q_BLOCK, t)
    per_q = bq // ATT_BLOCK
    ncol = MLA_BWD_COLS
    width = ncol * HEAD_PAD
    grid = (N_HEADS // ncol, t // bq)

    def body(*refs):
        steps = [pl.program_id(0), pl.program_id(1)]
        _with_cargo(cargo, refs, 6, 3, steps, grid, lambda own: work(steps[0], steps[1], *own))

    def work(h, i, q_ref, k_ref, v_ref, o_ref, do_ref, lse_ref, dq_ref, dk_hbm, dv_hbm, dk_ref, dv_ref, out_sems):

        @pl.when(i == 0)
        def _():
            dk_ref[...] = jnp.zeros_like(dk_ref)
            dv_ref[...] = jnp.zeros_like(dv_ref)

        qs = [q_ref[:, _lanes(c)] for c in range(ncol)]
        dos = [do_ref[:, _lanes(c)] for c in range(ncol)]
        deltas = [jnp.sum(dos[c].astype(F32) * o_ref[:, _lanes(c)].astype(F32), axis=-1, keepdims=True)
                  for c in range(ncol)]
        lses = [lse_ref[c] for c in range(ncol)]

        def step(j, dqs, masked):
            cols = range(ncol)
            kbs = [_rows(k_ref, j, c) for c in cols]
            scores = [_dot(qs[c], kbs[c], NT_DIMS) for c in cols]
            dps = [_dot(dos[c], _rows(v_ref, j, c), NT_DIMS) for c in cols]
            pbs, dss = [], []
            for c in cols:
                p = jnp.exp2(scores[c] - lses[c])
                if masked is not None:
                    p = jnp.where(_below_diagonal(bq, masked), p, 0.0)
                pbs.append(p.astype(BF16))
                dss.append((p * (dps[c] - deltas[c])).astype(BF16))
            for c in cols:
                dv_ref[_row_block(j), _lanes(c)] += _dot(pbs[c], dos[c], TN_DIMS)
                dk_ref[_row_block(j), _lanes(c)] += _dot(dss[c], qs[c], TN_DIMS)
            return tuple(dqs[c] + _dot(dss[c], kbs[c], NN_DIMS) for c in cols)

        init = tuple(jnp.zeros((bq, HEAD_PAD), F32) for _ in range(ncol))
        dqs = lax.fori_loop(0, i * per_q, lambda j, cr: step(j, cr, None), init)
        for d in range(per_q):
            dqs = step(i * per_q + d, dqs, d * ATT_BLOCK)
        for c, dq in enumerate(dqs):
            dq_ref[:, _lanes(c)] = dq * (1.0 / math.sqrt(MLA_QK))

        @pl.when(i == grid[1] - 1)
        def _():
            dk_ref[...] = dk_ref[...] * math.log(2.0)
            cols = pl.ds(pl.multiple_of(h * width, width), width)
            out = [pltpu.make_async_copy(dk_ref, dk_hbm.at[:, cols], out_sems.at[0]),
                   pltpu.make_async_copy(dv_ref, dv_hbm.at[:, cols], out_sems.at[1])]
            for cp in out:
                cp.start()
            for cp in out:
                cp.wait()

    full = pl.BlockSpec((t, width), lambda h, i: (0, h))
    blk = pl.BlockSpec((bq, width), lambda h, i: (i, h))
    wide = jax.ShapeDtypeStruct((t, N_HEADS * HEAD_PAD), F32)
    extra = cargo.specs() if cargo else []
    outs = pl.pallas_call(
        body, name="mla_bwd", grid=grid,
        in_specs=[blk, full, full, blk, blk, pl.BlockSpec((ncol, bq, 1), lambda h, i: (h, i, 0))] + extra,
        out_specs=[blk, HBM_SPEC, HBM_SPEC] + extra,
        out_shape=[wide, wide, wide] + (cargo.out_shape() if cargo else []),
        scratch_shapes=[pltpu.VMEM((t, width), F32), pltpu.VMEM((t, width), F32), pltpu.SemaphoreType.DMA((2,))]
        + (cargo.scratch() if cargo else []),
        compiler_params=_params(("arbitrary", "arbitrary")),
    )(qh, kh, kvb, o, do, lse, *(cargo.srcs if cargo else []))
    return (outs[0], outs[1], outs[2], list(outs[3:])) if cargo else outs


def _head_only(x, lane, u):
    return jnp.where((lane >= u * SB_DIM) & (lane < (u + 1) * SB_DIM), x, jnp.zeros_like(x))


SB_DEAD = -104.0


def _log_sigmoids(z):
    e = jnp.exp(-jnp.abs(z))
    lg = jnp.log(1.0 + e)
    ls_pos = jnp.minimum(z, 0.0) - lg
    return ls_pos, ls_pos - z, e


def _sb_fwd(proj):
    t = proj.shape[0]
    bq, ncol = SB_BLOCK, SB_FWD_COLS
    nq = t // bq
    scale = 1.0 / math.sqrt(SB_DIM)
    pairs = SB_WIDTH // HEAD_PAD

    def body(q_ref, k_ref, v_ref, o_ref, r_ref, first_ref):
        g, i = pl.program_id(0), pl.program_id(1)
        lane = lax.broadcasted_iota(jnp.int32, (bq, HEAD_PAD), 1)
        upper = _tri('gt', bq)
        chains = [(c, u) for c in range(ncol) for u in range(2)]
        qms = [_head_only(q_ref[:, _lanes(c)], lane, u) * scale for c, u in chains]

        def step(j, carry, masked):
            ids = range(len(chains))
            zs = [_dot(qms[n], _rows(k_ref, j, chains[n][0], bq), NT_DIMS) for n in ids]
            pos, neg, parts = [], [], []
            for n in ids:
                ls_pos, ls_neg, _ = _log_sigmoids(zs[n])
                if masked:
                    ls_neg = jnp.where(_causal(True, bq), ls_neg, 0.0)
                pos.append(ls_pos)
                neg.append(ls_neg)
                parts.append(_split_bf16(ls_neg))
            suffix = [_dot(parts[n][0], upper, NN_DIMS) + _dot(parts[n][1], upper, NN_DIMS) for n in ids]
            weights = []
            for n in ids:
                a = jnp.exp(pos[n] + suffix[n] + carry[n][0])
                if masked:
                    a = jnp.where(_causal(True, bq), a, 0.0)
                weights.append(a.astype(BF16))
            return tuple((carry[n][0] + jnp.sum(neg[n], axis=-1, keepdims=True),
                          carry[n][1] + _dot(weights[n], _rows(v_ref, j, chains[n][0], bq), NN_DIMS)) for n in ids)

        init = tuple((jnp.zeros((bq, 1), F32), jnp.zeros((bq, HEAD_PAD), F32)) for _ in chains)
        carry = step(i, init, True)

        def more(state):
            s, cr = state
            live = cr[0][0]
            for n in range(1, len(chains)):
                live = jnp.maximum(live, cr[n][0])
            return jnp.logical_and(s < i, jnp.max(live) > SB_DEAD)

        walked, carry = lax.while_loop(more, lambda st: (st[0] + 1, step(i - 1 - st[0], st[1], False)),
                                       (jnp.int32(0), carry))
        first_ref[g * nq + i] = i - walked
        for n, (c, u) in enumerate(chains):
            r_ref[2 * c + u] = carry[n][0]
        for c in range(ncol):
            o_ref[:, _lanes(c)] = jnp.where(lane < SB_DIM, carry[2 * c][1], carry[2 * c + 1][1]).astype(BF16)

    width = ncol * HEAD_PAD

    def full(c0):
        return pl.BlockSpec((t, width), lambda g, i: (0, c0 // width + g))

    return pl.pallas_call(
        body, name="sb_fwd", grid=(pairs // ncol, t // bq),
        in_specs=[pl.BlockSpec((bq, width), lambda g, i: (i, P_SBQ // width + g)), full(P_SBK), full(P_SBV)],
        out_specs=[pl.BlockSpec((bq, width), lambda g, i: (i, g)),
                   pl.BlockSpec((2 * ncol, bq, 1), lambda g, i: (g, i, 0)),
                   pl.BlockSpec(memory_space=pltpu.SMEM)],
        out_shape=[jax.ShapeDtypeStruct((t, SB_WIDTH), BF16), jax.ShapeDtypeStruct((N_HEADS, t, 1), F32),
                   jax.ShapeDtypeStruct((pairs // ncol * nq,), jnp.int32)],
        compiler_params=_params(("arbitrary", "arbitrary")),
    )(proj, proj, proj)


def _sb_bwd(proj, do, rtot, first):
    t = proj.shape[0]
    bq, ncol = SB_BLOCK, SB_BWD_COLS
    nq = t // bq
    scale = 1.0 / math.sqrt(SB_DIM)
    pairs = SB_WIDTH // HEAD_PAD

    def body(first_ref, q_ref, k_ref, v_ref, do_ref, r_ref, dq_ref, dk_ref, dv_ref):
        g, i = pl.program_id(0), pl.program_id(1)

        @pl.when(i == 0)
        def _():
            dk_ref[...] = jnp.zeros_like(dk_ref)
            dv_ref[...] = jnp.zeros_like(dv_ref)

        lane = lax.broadcasted_iota(jnp.int32, (bq, HEAD_PAD), 1)
        incl = _tri('le', bq)
        excl = _tri('lt', bq)
        chains = [(c, u) for c in range(ncol) for u in range(2)]
        qms = [_head_only(q_ref[:, _lanes(c)], lane, u) * scale for c, u in chains]
        doms = [_head_only(do_ref[:, _lanes(c)], lane, u) for c, u in chains]
        rts = [r_ref[2 * c + u] for c, u in chains]

        def step(j, carry, masked):
            ids = range(len(chains))
            kbs = [_rows(k_ref, j, c, bq) for c in range(ncol)]
            zs =[_dot(qms[n], kbs[chains[n][0]], NT_DIMS) for n in ids]
            das = [_dot(doms[n], _rows(v_ref, j, chains[n][0], bq), NT_DIMS) for n in ids]
            pos, neg, sigs, parts = [], [], [], []
            for n in ids:
                ls_pos, ls_neg, e = _log_sigmoids(zs[n])
                if masked:
                    ls_neg = jnp.where(_causal(True, bq), ls_neg, 0.0)
                pos.append(ls_pos)
                neg.append(ls_neg)
                sigs.append(jnp.where(zs[n] >= 0.0, 1.0, e) * pl.reciprocal(1.0 + e, approx=True))
                parts.append(_split_bf16(ls_neg))
            prefix = [_dot(parts[n][0], incl, NN_DIMS) + _dot(parts[n][1], incl, NN_DIMS) for n in ids]
            evs, eparts, dvs = [], [], []
            for n in ids:
                a = jnp.exp(pos[n] + (rts[n] - (carry[n][0] + prefix[n])))
                if masked:
                    a = jnp.where(_causal(True, bq), a, 0.0)
                dvs.append(_dot(a.astype(BF16), doms[n], TN_DIMS))
                evs.append(a * das[n])
                eparts.append(evs[n].astype(BF16))
            before = [_dot(eparts[n], excl, NN_DIMS) for n in ids]
            out, dks = [], []
            for n in ids:
                dz = evs[n] - sigs[n] * (evs[n] + (carry[n][1] + before[n]))
                if masked:
                    dz = jnp.where(_causal(True, bq), dz, 0.0)
                dzb = dz.astype(BF16)
                dks.append(_dot(dzb, qms[n], TN_DIMS))
                out.append((carry[n][0] + jnp.sum(neg[n], axis=-1, keepdims=True),
                            carry[n][1] + jnp.sum(evs[n], axis=-1, keepdims=True),
                            carry[n][2] + _dot(dzb, kbs[chains[n][0]], NN_DIMS)))
            for c in range(ncol):
                dv_ref[_row_block(j, bq), _lanes(c)] += dvs[2 * c] + dvs[2 * c + 1]
                dk_ref[_row_block(j, bq), _lanes(c)] += dks[2 * c] + dks[2 * c + 1]
            return tuple(out)

        init = tuple((jnp.zeros((bq, 1), F32), jnp.zeros((bq, 1), F32), jnp.zeros((bq, HEAD_PAD), F32)) for _ in chains)
        start = first_ref[(g * ncol // SB_FWD_COLS) * nq + i]
        carry = lax.fori_loop(start, i, lambda j, cr: step(j, cr, False), init)
        carry = step(i, carry, True)
        for c in range(ncol):
            dq_ref[:, _lanes(c)] = jnp.where(lane < SB_DIM, carry[2 * c][2], carry[2 * c + 1][2]) * scale

    width = ncol * HEAD_PAD

    def full(c0):
        return pl.BlockSpec((t, width), lambda g, i, first: (0, c0 // width + g))

    blk = pl.BlockSpec((bq, width), lambda g, i, first: (i, g))
    acc = pl.BlockSpec((t, width), lambda g, i, first: (0, g))
    wide = jax.ShapeDtypeStruct((t, SB_WIDTH), F32)
    return pl.pallas_call(
        body, name="sb_bwd",
        grid_spec=pltpu.PrefetchScalarGridSpec(
            num_scalar_prefetch=1, grid=(pairs // ncol, nq),
            in_specs=[pl.BlockSpec((bq, width), lambda g, i, first: (i, P_SBQ // width + g)), full(P_SBK), full(P_SBV),
                      blk, pl.BlockSpec((2 * ncol, bq, 1), lambda g, i, first: (g, i, 0))],
            out_specs=[blk, acc, acc]),
        out_shape=[wide, wide, wide],
        compiler_params=_params(("arbitrary", "arbitrary")),
    )(first, proj, proj, proj, do, rtot)


def _cols_to_full(g):
    n, r, c = g.shape
    return jnp.transpose(g, (1, 0, 2)).reshape(r, n * c)


def _full_to_cols(w):
    r, c = w.shape
    return jnp.transpose(w.reshape(r, N_DEV, c // N_DEV), (1, 0, 2))


TRANSPOSED = ('ffn1_w_in', 'ffn2_w_in', 'w_in', 'w_q_up')


def _layout_weight(name, g):
    if name in ('ffn1_w_out', 'ffn2_w_out', 'w_out', 'w_ple_gate', 'ffn1_w_in', 'ffn2_w_in'):
        return g.reshape(g.shape[0] * g.shape[1], g.shape[2])
    if name == 'w_in':
        wt = g.reshape(IN_COLS, D_MODEL)
        z = lambda n: jnp.zeros((n, D_MODEL), BF16)
        return jnp.concatenate([wt[0:640], z(64), wt[640:672], z(32), z(256), wt[2208:4256], wt[672:2208]], axis=0)
    if name == 'w_q_up':
        return jnp.pad(g, ((0, 0), (0, HEAD_PAD - MLA_QK), (0, 0))).reshape(N_HEADS * HEAD_PAD, Q_LORA)
    if name == 'w_branch_mla':
        bm = _cols_to_full(g).reshape(N_HEADS, MLA_NOPE, D_MODEL)
        return jnp.pad(bm, ((0, 0), (HEAD_PAD - MLA_NOPE, 0), (0, 0))).reshape(N_HEADS * HEAD_PAD, D_MODEL)
    return _cols_to_full(g)


def _layout_weights(g):
    return {n: _layout_weight(n, a) for n, a in g.items()}


def _unlayout_grad(name, d):
    if name == 'w_in':
        d = jnp.concatenate([d[0:640], d[704:736], d[P_SBQ:PROJ_W], d[P_GM:P_SBQ]], axis=0)
    if name == 'w_q_up':
        return d.reshape(N_HEADS, HEAD_PAD, Q_LORA)[:, :MLA_QK, :]
    if name in ('ffn1_w_out', 'ffn2_w_out', 'w_out', 'w_ple_gate', 'ffn1_w_in', 'ffn2_w_in', 'w_in'):
        return d.reshape(N_DEV, d.shape[0] // N_DEV, d.shape[1])
    if name == 'w_branch_mla':
        d = d.reshape(N_HEADS, HEAD_PAD, D_MODEL)[:, HEAD_PAD - MLA_NOPE:, :].reshape(SB_WIDTH, D_MODEL)
    return _full_to_cols(d)


def _unlayout_grads(d):
    return {n: _unlayout_grad(n, a) for n, a in d.items()}


def _rope_tables(positions):
    half = MLA_ROPE // 2
    inv_freq = ROPE_BASE ** (-jnp.arange(0, MLA_ROPE, 2, dtype=F32) / MLA_ROPE)
    ang = positions.astype(F32)[:, None] * inv_freq
    cos, sin = jnp.cos(ang), jnp.sin(ang)
    t = positions.shape[0]
    ones = lambda n: jnp.ones((t, n), F32)
    zeros = lambda n: jnp.zeros((t, n), F32)
    cosf = jnp.concatenate([ones(MLA_NOPE), cos, cos, ones(HEAD_PAD - MLA_QK)], axis=1)
    sin_a = jnp.concatenate([zeros(MLA_NOPE), -sin, zeros(half), zeros(HEAD_PAD - MLA_QK)], axis=1)
    sin_b = jnp.concatenate([zeros(MLA_NOPE), zeros(half), sin, zeros(HEAD_PAD - MLA_QK)], axis=1)
    return cosf, sin_a, sin_b


def _local_step(x, p, positions, tgt, norms, plan):
    mm = _matmul
    cosf, sin_a, sin_b = _rope_tables(positions)
    pad_head = lambda g: jnp.pad(g, ((0, 0), (0, HEAD_PAD - MLA_QK)))
    gqh, gkh = pad_head(norms['q_head_norm']), pad_head(norms['k_head_norm'])
    pb = p.astype(BF16)
    w = dict(plan.first_weights())
    dw, dn = {}, {}

    def ride(host, call):
        cargo = plan.cargo(host, dw)
        res, lands = call(cargo), None
        if cargo is not None:
            *res, lands = res
            res = res[0] if len(res) == 1 else tuple(res)
        w.update(plan.landed(host, lands))
        return res

    def ffn_fwd(h, tag):
        n, a, b, act = ride(tag + "_in_fwd", lambda cargo: _swiglu_fwd(
            h, norms[tag + '_norm'], w[tag + '_w_in'], tag + "_in_fwd", cargo))
        out = ride(tag + "_out_fwd", lambda cargo: mm(
            act, w[tag + '_w_out'], mode='nn', out_dtype=F32, name=tag + "_out_fwd", res=h, alpha=0.5, cargo=cargo))
        return out, (n, a, b, act)

    h1, ffn1_saved = ffn_fwd(x, 'ffn1')
    u = _rmsnorm_fwd(h1, norms['mix_norm'], "mix_norm_fwd")
    proj = mm(u, w['w_in'], mode='nt', out_dtype=BF16, name="proj_fwd")
    cqn, ckvn = _latent_fwd(proj, norms['q_latent_norm'], norms['kv_latent_norm'])
    qraw = mm(cqn, w['w_q_up'], mode='nt', out_dtype=F32, name="q_up_fwd")
    kvraw = mm(ckvn, w['w_kv_up'], mode='nn', out_dtype=F32, name="kv_up_fwd")
    qh, kh, kvb = _headprep_fwd(qraw, kvraw, proj, cosf, sin_a, sin_b, gqh, gkh)
    o_mla, lse = ride("mla_fwd", lambda cargo: _mla_fwd(qh, kh, kvb, cargo))
    o_sb, rtot, sb_first = _sb_fwd(proj)
    merged, bm, bs = _merge_fwd(o_mla, w['w_branch_mla'], o_sb, w['w_branch_sb'], proj)
    h2 = mm(merged, w['w_out'], mode='nn', out_dtype=F32, name="mix_out_fwd", res=h1)
    h3, ffn2_saved = ffn_fwd(h2, 'ffn2')
    n3, dh4, dzg, dpp, loss_lanes = _ple_loss(h3, norms['ple_norm'], w['w_ple_gate'], pb, w['w_ple_proj'], tgt)

    dw['w_ple_gate'] = mm(n3, dzg, mode='tn', out_dtype=BF16, name="ple_gate_dw")
    dw['w_ple_proj'] = mm(pb, dpp, mode='tn', out_dtype=BF16, name="ple_proj_dw")
    dh3, dhb3, dn['ple_norm'] = _matmul_norm_bwd(
        dzg, w['w_ple_gate'], h3, norms['ple_norm'], dh4, mode='nt', name="ple_gate_dx", out_scale=0.5)

    def ffn_bwd(h, dh, dhb, saved, tag, out_scale):
        n, a, b, act = saved
        dw[tag + '_w_out'] = mm(act, dhb, mode='tn', out_dtype=BF16, name=tag + "_out_dw", tm=1408)
        dab = _swiglu_bwd(dhb, w[tag + '_w_out'], a, b, tag + "_out_dx")
        dw[tag + '_w_in'] = ride(tag + "_in_dw", lambda cargo: mm(
            dab, n, mode='tn', out_dtype=BF16, name=tag + "_in_dw", tm=1408, cargo=cargo))
        dh_prev, dhb_prev, dn[tag + '_norm'] = ride(tag + "_in_dx", lambda cargo: _matmul_norm_bwd(
            dab, w[tag + '_w_in'], h, norms[tag + '_norm'], dh, mode='nn', name=tag + "_in_dx", out_scale=out_scale,
            cargo=cargo))
        return dh_prev, dhb_prev

    dh2, dhb2 = ffn_bwd(h2, dh3, dhb3, ffn2_saved, 'ffn2', 1.0)
    dw['w_out'] = mm(merged, dhb2, mode='tn', out_dtype=BF16, name="mix_out_dw")
    dbm, dbs, dgates = _merge_bwd(dhb2, w['w_out'], proj, bm, bs)
    dw['w_branch_mla'] = mm(o_mla, dbm, mode='tn', out_dtype=BF16, name="branch_mla_dw")
    dw['w_branch_sb'] = mm(o_sb, dbs, mode='tn', out_dtype=BF16, name="branch_sb_dw")
    do_mla = mm(dbm, w['w_branch_mla'], mode='nt', out_dtype=BF16, name="branch_mla_dx")
    do_sb = mm(dbs, w['w_branch_sb'], mode='nt', out_dtype=BF16, name="branch_sb_dx")
    dqh, dkh, dvp = ride("mla_bwd", lambda cargo: _mla_bwd(qh, kh, kvb, o_mla, do_mla, lse, cargo))
    dsq, dsk, dsv = _sb_bwd(proj, do_sb, rtot, sb_first)
    dqraw, dkvraw, dkr, dgq, dgk = _headprep_bwd(dqh, dkh, dvp, qraw, kvraw, proj, cosf, sin_a, sin_b, gqh, gkh)
    dn['q_head_norm'], dn['k_head_norm'] = dgq[:, :MLA_QK], dgk[:, :MLA_QK]
    dw['w_q_up'] = mm(dqraw, cqn, mode='tn', out_dtype=BF16, name="q_up_dw")
    dw['w_kv_up'] = mm(ckvn, dkvraw, mode='tn', out_dtype=BF16, name="kv_up_dw")
    dcqn = mm(dqraw, w['w_q_up'], mode='nn', out_dtype=F32, name="q_up_dx")
    dckvn = mm(dkvraw, w['w_kv_up'], mode='nt', out_dtype=F32, name="kv_up_dx")
    dlat, dn['q_latent_norm'], dn['kv_latent_norm'] = _latent_bwd(
        dcqn, dckvn, proj, dkr, norms['q_latent_norm'], norms['kv_latent_norm'])
    dproj = jnp.concatenate([dlat, dgates, dsq.astype(BF16), dsk.astype(BF16), dsv.astype(BF16)], axis=1)
    dw['w_in'] = ride("proj_dw", lambda cargo: mm(dproj, u, mode='tn', out_dtype=BF16, name="proj_dw", tm=1536, cargo=cargo))
    dh1, dhb1, dn['mix_norm'] = ride("proj_dx", lambda cargo: _matmul_norm_bwd(
        dproj, w['w_in'], h1, norms['mix_norm'], dh2, mode='nn', name="proj_dx", out_scale=0.5, cargo=cargo))
    dx, _ = ffn_bwd(x, dh1, dhb1, ffn1_saved, 'ffn1', 1.0)
    return dx, loss_lanes, dw, dn


MESH = pl.DeviceIdType.MESH
HBM_SPEC = pl.BlockSpec(memory_space=pl.ANY)


def _position():
    return lax.axis_index("x"), lax.axis_index("y"), lax.axis_index("c")


def _index(px, py, pc):
    return 4 * px + 2 * py + pc


def _all_gather(shards):
    n = len(shards)

    def body(*refs):
        ins, outs = refs[:n], refs[n:2 * n]
        send_sems, recv_sems, local_sems = refs[2 * n:]
        x, y, c = _position()
        me, sibling = (x, y, c), (x, y, 1 - c)
        chips = [(1 - x, y), (x, 1 - y), (1 - x, 1 - y)]

        def copy(a, k, block, to, own=False):
            dst = outs[a].at[_index(*block)]
            return pltpu.make_async_remote_copy(
                src_ref=ins[a] if own else dst, dst_ref=dst,
                send_sem=send_sems.at[a, k], recv_sem=recv_sems.at[a, k], device_id=to, device_id_type=MESH)

        mine = [pltpu.make_async_copy(ins[a], outs[a].at[_index(*me)], local_sems.at[a]) for a in range(n)]
        for cp in mine:
            cp.start()
        first = []
        for a in range(n):
            first.append(copy(a, 0, me, sibling, own=True))
            first += [copy(a, 1 + j, me, (*chip, c), own=True) for j, chip in enumerate(chips)]
        for cp in first:
            cp.start()
        passed = []
        for j, chip in enumerate(chips):
            for a in range(n):
                copy(a, 1 + j, (*chip, c), me).wait_recv()
                fwd = copy(a, 4 + j, (*chip, c), sibling)
                fwd.start()
                passed.append(fwd)
        for a in range(n):
            copy(a, 0, sibling, me).wait_recv()
            for j, chip in enumerate(chips):
                copy(a, 4 + j, (*chip, 1 - c), me).wait_recv()
        for cp in first + passed:
            cp.wait_send()
        for cp in mine:
            cp.wait()

    return pl.pallas_call(
        body, name="weights_all_gather",
        in_specs=[HBM_SPEC] * n, out_specs=[HBM_SPEC] * n,
        out_shape=[jax.ShapeDtypeStruct((N_DEV,) + s.shape, s.dtype) for s in shards],
        scratch_shapes=[pltpu.SemaphoreType.DMA((n, 7)), pltpu.SemaphoreType.DMA((n, 7)), pltpu.SemaphoreType.DMA((n,))],
    )(*shards)


def _exchange(parts):
    n = len(parts)
    masks = [(mx, my, mc) for mx in (0, 1) for my in (0, 1) for mc in (0, 1)][1:]

    def body(*refs):
        ins, outs = refs[:n], refs[n:2 * n]
        send_sems, recv_sems, local_sems = refs[2 * n:]
        x, y, c = _position()
        me = _index(x, y, c)

        def peer_of(mask):
            mx, my, mc = mask
            return (x + mx - 2 * x * mx, y + my - 2 * y * my, c + mc - 2 * c * mc)

        def copy(a, k):
            peer = peer_of(masks[k])
            return pltpu.make_async_remote_copy(
                src_ref=ins[a].at[_index(*peer)], dst_ref=outs[a].at[me],
                send_sem=send_sems.at[a, k], recv_sem=recv_sems.at[a, k], device_id=peer, device_id_type=MESH)

        def landed(a, k):
            peer = peer_of(masks[k])
            return pltpu.make_async_remote_copy(
                src_ref=ins[a].at[me], dst_ref=outs[a].at[_index(*peer)],
                send_sem=send_sems.at[a, k], recv_sem=recv_sems.at[a, k], device_id=peer, device_id_type=MESH)

        mine = [pltpu.make_async_copy(ins[a].at[me], outs[a].at[me], local_sems.at[a]) for a in range(n)]
        for cp in mine:
            cp.start()
        sent = [copy(a, k) for k in range(7) for a in range(n)]
        for cp in sent:
            cp.start()
        for k in range(7):
            for a in range(n):
                landed(a, k).wait_recv()
        for cp in sent:
            cp.wait_send()
        for cp in mine:
            cp.wait()

    return pl.pallas_call(
        body, name="grads_exchange",
        in_specs=[HBM_SPEC] * n, out_specs=[HBM_SPEC] * n,
        out_shape=[jax.ShapeDtypeStruct(s.shape, s.dtype) for s in parts],
        scratch_shapes=[pltpu.SemaphoreType.DMA((n, 7)), pltpu.SemaphoreType.DMA((n, 7)), pltpu.SemaphoreType.DMA((n,))],
    )(*parts)


PEER_MASKS = [(mx, my, mc) for mx in (0, 1) for my in (0, 1) for mc in (0, 1)][1:]


def _peer(mask):
    x, y, c = _position()
    mx, my, mc = mask
    return (x + mx - 2 * x * mx, y + my - 2 * y * my, c + mc - 2 * c * mc)


class _Cargo:
    def __init__(self, srcs, scatter):
        self.srcs, self.scatter, self.n = list(srcs), scatter, len(srcs)

    def specs(self):
        return [HBM_SPEC] * self.n

    def out_shape(self):
        return [jax.ShapeDtypeStruct(s.shape if self.scatter else (N_DEV,) + s.shape, s.dtype) for s in self.srcs]

    def scratch(self):
        per_copy = pltpu.SemaphoreType.DMA((self.n, len(PEER_MASKS)))
        return [per_copy, per_copy, pltpu.SemaphoreType.DMA((self.n,))]

    def _mine(self, src_refs, a, to):
        return src_refs[a].at[to] if self.scatter else src_refs[a]

    def _shard_copy(self, src_refs, land_refs, sems, a, k, block, to, own=False):
        dst = land_refs[a].at[_index(*block)]
        return pltpu.make_async_remote_copy(
            src_ref=src_refs[a] if own else dst, dst_ref=dst,
            send_sem=sems[0].at[a, k], recv_sem=sems[1].at[a, k], device_id=to, device_id_type=MESH)

    def _first_hops(self, src_refs, land_refs, sems):
        x, y, c = _position()
        chips = [(1 - x, y), (x, 1 - y), (1 - x, 1 - y)]
        hops = []
        for a in range(self.n):
            hops.append(self._shard_copy(src_refs, land_refs, sems, a, 0, (x, y, c), (x, y, 1 - c), own=True))
            hops += [self._shard_copy(src_refs, land_refs, sems, a, 1 + j, (x, y, c), (*chip, c), own=True)
                     for j, chip in enumerate(chips)]
        return hops, chips

    def start(self, src_refs, land_refs, sems):
        send, recv, local = sems
        me = _index(*_position())
        for a in range(self.n):
            pltpu.make_async_copy(self._mine(src_refs, a, me), land_refs[a].at[me], local.at[a]).start()
        if not self.scatter:
            for cp in self._first_hops(src_refs, land_refs, sems)[0]:
                cp.start()
            return
        for k, mask in enumerate(PEER_MASKS):
            peer = _peer(mask)
            for a in range(self.n):
                pltpu.make_async_remote_copy(
                    src_ref=self._mine(src_refs, a, _index(*peer)), dst_ref=land_refs[a].at[me],
                    send_sem=send.at[a, k], recv_sem=recv.at[a, k], device_id=peer, device_id_type=MESH).start()

    def _wait_gathered(self, src_refs, land_refs, sems):
        x, y, c = _position()
        me, sibling = (x, y, c), (x, y, 1 - c)
        first, chips = self._first_hops(src_refs, land_refs, sems)
        passed = []
        for j, chip in enumerate(chips):
            for a in range(self.n):
                self._shard_copy(src_refs, land_refs, sems, a, 1 + j, (*chip, c), me).wait_recv()
                passed.append(self._shard_copy(src_refs, land_refs, sems, a, 4 + j, (*chip, c), sibling))
                passed[-1].start()
        for a in range(self.n):
            self._shard_copy(src_refs, land_refs, sems, a, 0, sibling, me).wait_recv()
            for j, chip in enumerate(chips):
                self._shard_copy(src_refs, land_refs, sems, a, 4 + j, (*chip, 1 - c), me).wait_recv()
        for cp in first + passed:
            cp.wait_send()

    def wait(self, src_refs, land_refs, sems):
        send, recv, local = sems
        me = _index(*_position())
        if not self.scatter:
            self._wait_gathered(src_refs, land_refs, sems)
        for k, mask in enumerate(PEER_MASKS if self.scatter else []):
            peer = _peer(mask)
            there = _index(*peer)
            for a in range(self.n):
                pltpu.make_async_remote_copy(
                    src_ref=self._mine(src_refs, a, me), dst_ref=land_refs[a].at[there],
                    send_sem=send.at[a, k], recv_sem=recv.at[a, k], device_id=peer, device_id_type=MESH).wait_recv()
                pltpu.make_async_remote_copy(
                    src_ref=self._mine(src_refs, a, there), dst_ref=land_refs[a].at[me],
                    send_sem=send.at[a, k], recv_sem=recv.at[a, k], device_id=peer, device_id_type=MESH).wait_send()
        for a in range(self.n):
            pltpu.make_async_copy(self._mine(src_refs, a, me), land_refs[a].at[me], local.at[a]).wait()


def _with_cargo(cargo, refs, n_in, n_out, steps, counts, compute):
    if cargo is None:
        compute(refs)
        return
    n = cargo.n
    src_refs = refs[n_in:n_in + n]
    land_refs = refs[n_in + n + n_out:n_in + 2 * n + n_out]
    sems = refs[-3:]
    first = functools.reduce(jnp.logical_and, [s == 0 for s in steps])
    last = functools.reduce(jnp.logical_and, [s == c - 1 for s, c in zip(steps, counts)])

    @pl.when(first)
    def _():
        cargo.start(src_refs, land_refs, sems)

    compute(refs[:n_in] + refs[n_in + n:n_in + n + n_out] + refs[n_in + 2 * n + n_out:-3])

    @pl.when(last)
    def _():
        cargo.wait(src_refs, land_refs, sems)


def _adamw(parts, w, m, v, name):
    r, c = w.shape
    tr = next((t for t in (512, 384, 352, 256, 128) if r % t == 0), r) if r > 512 else r
    tc = c if tr < r or r <= 512 else 256
    assert r % tr == 0 and c % tc == 0
    bc1 = 1.0 - ADAM_B1 ** ADAM_STEP
    bc2 = 1.0 - ADAM_B2 ** ADAM_STEP

    def body(p_ref, w_ref, m_ref, v_ref, g_ref, d_ref, nm_ref, nv_ref):
        g = p_ref[0].astype(F32)
        for s in range(1, N_DEV):
            g = g + p_ref[s].astype(F32)
        nm = ADAM_B1 * m_ref[...] + (1.0 - ADAM_B1) * g
        nv = ADAM_B2 * v_ref[...] + (1.0 - ADAM_B2) * (g * g)
        g_ref[...] = g
        nm_ref[...] = nm
        nv_ref[...] = nv
        d_ref[...] = -ADAM_LR * ((nm / bc1) / (jnp.sqrt(nv / bc2) + ADAM_EPS) + ADAM_WD * w_ref[...])

    tile = pl.BlockSpec((tr, tc), lambda i, j: (i, j))
    out = jax.ShapeDtypeStruct((r, c), F32)
    return pl.pallas_call(
        body, name=name, grid=(r // tr, c // tc),
        in_specs=[pl.BlockSpec((N_DEV, tr, tc), lambda i, j: (0, i, j)), tile, tile, tile],
        out_specs=[tile] * 4, out_shape=[out] * 4,
        compiler_params=_params(("parallel", "parallel")),
    )(parts, w, m, v)


GATHER_FIRST = ['ffn1_w_in']
RIDES = {
    'ffn1_in_fwd': ('weights', ['ffn1_w_out', 'w_in']),
    'ffn1_out_fwd': ('weights', ['w_q_up', 'w_kv_up', 'w_branch_mla', 'w_branch_sb', 'w_out']),
    'mla_fwd': ('weights', ['ffn2_w_in', 'ffn2_w_out', 'w_ple_gate', 'w_ple_proj']),
    'mla_bwd': ('grads', ['w_ple_gate', 'w_ple_proj', 'ffn2_w_out', 'ffn2_w_in', 'w_out', 'w_branch_mla', 'w_branch_sb']),
    'proj_dw': ('grads', ['w_q_up', 'w_kv_up']),
    'proj_dx': ('grads', ['w_in']),
    'ffn1_in_dw': ('grads', ['ffn1_w_out']),
    'ffn1_in_dx': ('grads', ['ffn1_w_in']),
}


class _Plan:
    def __init__(self, shards):
        self.shards = shards
        self.received = {}

    def first_weights(self):
        gathered = _all_gather([self.shards[n] for n in GATHER_FIRST])
        return {n: _layout_weight(n, g) for n, g in zip(GATHER_FIRST, gathered)}

    def cargo(self, host, dw):
        if host not in RIDES:
            return None
        kind, names = RIDES[host]
        if kind == 'weights':
            return _Cargo([self.shards[n] for n in names], False)
        return _Cargo([_unlayout_grad(n, dw.pop(n)) for n in names], True)

    def landed(self, host, lands):
        if host not in RIDES:
            return {}
        kind, names = RIDES[host]
        if kind == 'weights':
            return {n: _layout_weight(n, land) for n, land in zip(names, lands)}
        self.received.update(zip(names, lands))
        return {}


def _pack_small(vecs):
    flat = jnp.concatenate([v.reshape(-1) for v in vecs])
    return jnp.pad(flat, (0, SMALL_ROWS * 128 - flat.shape[0])).reshape(SMALL_ROWS, 128)


def _unpack_small(packed, sizes):
    flat = packed.reshape(-1)
    out, at = [], 0
    for n in sizes:
        out.append(flat[at:at + n])
        at += n
    return out


def kernel(x, p, positions, ffn1_norm, ffn1_w_in, ffn1_w_out, mix_norm, w_in, q_latent_norm, w_q_up, kv_latent_norm, w_kv_up, q_head_norm, k_head_norm, w_branch_mla, w_branch_sb, w_out, ffn2_norm, ffn2_w_in, ffn2_w_out, ple_norm, w_ple_gate, w_ple_proj, loss_target, m_ffn1_norm, m_ffn1_w_in, m_ffn1_w_out, m_mix_norm, m_w_in, m_q_latent_norm, m_w_q_up, m_kv_latent_norm, m_w_kv_up, m_q_head_norm, m_k_head_norm, m_w_branch_mla, m_w_branch_sb, m_w_out, m_ffn2_norm, m_ffn2_w_in, m_ffn2_w_out, m_ple_norm, m_w_ple_gate, m_w_ple_proj, v_ffn1_norm, v_ffn1_w_in, v_ffn1_w_out, v_mix_norm, v_w_in, v_q_latent_norm, v_w_q_up, v_kv_latent_norm, v_w_kv_up, v_q_head_norm, v_k_head_norm, v_w_branch_mla, v_w_branch_sb, v_w_out, v_ffn2_norm, v_ffn2_w_in, v_ffn2_w_out, v_ple_norm, v_w_ple_gate, v_w_ple_proj):
    given = dict(locals())
    wts = {n: given[n] for n in WEIGHTS}
    mom = {n: given['m_' + n] for n in WEIGHTS}
    var = {n: given['v_' + n] for n in WEIGHTS}

    def local(a, n):
        return jnp.swapaxes(a[0], 0, 1) if n in TRANSPOSED else a[0]

    plan = _Plan({n: local(wts[n], n).astype(BF16) for n in MATS})
    norms = {n: wts[n] for n in NORMS}
    dx, loss_lanes, dw, dn = _local_step(x[0], p[0, 0], positions[0], loss_target[0], norms, plan)
    assert not dw

    out = {}
    for n in MATS:
        res = _adamw(plan.received[n], local(wts[n], n), local(mom[n], n), local(var[n], n), "adamw_" + n)
        out[n] = [local(r[None], n)[None] for r in res]
    small = _pack_small([dn[n] for n in NORMS] + [0.5 / D_MODEL * jnp.sum(loss_lanes)[None]])
    small_parts = _exchange([jnp.broadcast_to(small[None], (N_DEV, SMALL_ROWS, 128))])[0]
    sizes = [wts[n].shape[1] for n in NORMS]
    pack = lambda d: _pack_small([d[n] for n in NORMS])
    small_res = _adamw(small_parts, pack(wts), pack(mom), pack(var), "adamw_norms")
    loss = small_res[0].reshape(-1)[sum(sizes)]
    for i, res in enumerate(small_res):
        for n, vec in zip(NORMS, _unpack_small(res, sizes)):
            out.setdefault(n, [None] * 4)[i] = vec[None]

    return (loss, dx[None], *[out[n][0] for n in WEIGHTS], *[out[n][1] for n in WEIGHTS],
            *[out[n][2] for n in WEIGHTS], *[out[n][3] for n in WEIGHTS])
```

```python
import functools
import math

import jax
import jax.numpy as jnp
from jax import lax
from jax.experimental import pallas as pl
from jax.experimental.pallas import tpu as pltpu

F32 = jnp.float32
BF16 = jnp.bfloat16

N_DEV = 8
D_MODEL = 1024
D_FF = 2816
PLE_DIM = 256
NORM_EPS = 1e-6
N_HEADS = 8
HEAD_PAD = 128
MLA_NOPE = 64
MLA_ROPE = 32
MLA_QK = 96
Q_LORA = 384
KV_LORA = 256
SB_DIM = 64
SB_WIDTH = 512
ROPE_BASE = 10000.0
IN_COLS = 4256

PROJ_W = 4608
P_CQ, P_CKV, P_KR, P_GM, P_GS, P_SBQ, P_SBK, P_SBV = 0, 384, 640, 1024, 2048, 3072, 3584, 4096

ADAM_LR, ADAM_B1, ADAM_B2, ADAM_EPS, ADAM_WD, ADAM_STEP = 0.001, 0.9, 0.999, 1e-08, 0.01, 10

VMEM_LIMIT = 52 * 1024 * 1024
MATMUL_VMEM = 40 * 1024 * 1024

WEIGHTS = ['ffn1_norm', 'ffn1_w_in', 'ffn1_w_out', 'mix_norm', 'w_in', 'q_latent_norm', 'w_q_up',
           'kv_latent_norm', 'w_kv_up', 'q_head_norm', 'k_head_norm', 'w_branch_mla', 'w_branch_sb',
           'w_out', 'ffn2_norm', 'ffn2_w_in', 'ffn2_w_out', 'ple_norm', 'w_ple_gate', 'w_ple_proj']
NORMS = ['ffn1_norm', 'mix_norm', 'q_latent_norm', 'kv_latent_norm', 'q_head_norm', 'k_head_norm',
         'ffn2_norm', 'ple_norm']
MATS = [n for n in WEIGHTS if n not in NORMS]
SMALL_ROWS = 48

NT_DIMS = (((1,), (1,)), ((), ()))
NN_DIMS = (((1,), (0,)), ((), ()))
TN_DIMS = (((0,), (0,)), ((), ()))


def _params(sem=None, vmem=VMEM_LIMIT):
    return pltpu.CompilerParams(dimension_semantics=sem, vmem_limit_bytes=vmem)


def _pick(n, cap):
    if n <= cap:
        return n
    best = None
    for t in range(128, cap + 1, 128):
        if n % t == 0:
            best = t
    assert best is not None, (n, cap)
    return best


def _dot(a, b, dims):
    return lax.dot_general(a, b, dims, preferred_element_type=F32)


def _matmul(a, b, *, mode, out_dtype, name, tm=None, tn=None, tk=None, res=None, alpha=1.0, cargo=None):
    if mode == 'nn':
        (m, k), (k2, n) = a.shape, b.shape
    elif mode == 'nt':
        (m, k), (n, k2) = a.shape, b.shape
    else:
        (k, m), (k2, n) = a.shape, b.shape
    assert k == k2, (name, a.shape, b.shape)
    has_res = res is not None
    tn = tn or _pick(n, 512)

    def vmem(tm_, tk_):
        io = 2 * 2 * (tm_ * tk_ + tk_ * tn) + 2 * tm_ * tn * (jnp.dtype(out_dtype).itemsize + 4 * has_res)
        return io + (4 * tm_ * tn if tk_ < k else 0)

    tries = [(tm_, tk_) for tk_ in ([tk] if tk else [k, _pick(k, 2048)])
             for tm_ in ([tm] if tm else [_pick(m, 2048), _pick(m, 1024), _pick(m, 512)])]
    tm, tk = next((c for c in tries if vmem(*c) <= MATMUL_VMEM), tries[-1])
    assert m % tm == 0 and n % tn == 0 and k % tk == 0, (name, m, n, k, tm, tn, tk)
    nk = k // tk
    dims = {'nn': NN_DIMS, 'nt': NT_DIMS, 'tn': TN_DIMS}[mode]

    def epilogue(acc, r_ref, o_ref):
        if alpha != 1.0:
            acc = acc * alpha
        if has_res:
            acc = r_ref[...] + acc
        o_ref[...] = acc.astype(out_dtype)

    grid = (m // tm, n // tn, nk)

    def body(*refs):
        steps = [pl.program_id(d) for d in range(3)]

        def compute(own):
            a_ref, b_ref = own[0], own[1]
            r_ref = own[2] if has_res else None
            o_ref = own[2 + has_res]
            if nk == 1:
                epilogue(_dot(a_ref[...], b_ref[...], dims), r_ref, o_ref)
                return
            acc_ref = own[-1]

            @pl.when(steps[2] == 0)
            def _():
                acc_ref[...] = jnp.zeros_like(acc_ref)

            acc_ref[...] += _dot(a_ref[...], b_ref[...], dims)

            @pl.when(steps[2] == nk - 1)
            def _():
                epilogue(acc_ref[...], r_ref, o_ref)

        _with_cargo(cargo, refs, 2 + has_res, 1, steps, grid, compute)

    if mode == 'tn':
        a_spec = pl.BlockSpec((tk, tm), lambda i, j, kk: (kk, i))
    else:
        a_spec = pl.BlockSpec((tm, tk), lambda i, j, kk: (i, kk))
    if mode == 'nt':
        b_spec = pl.BlockSpec((tn, tk), lambda i, j, kk: (j, kk))
    else:
        b_spec = pl.BlockSpec((tk, tn), lambda i, j, kk: (kk, j))
    o_spec = pl.BlockSpec((tm, tn), lambda i, j, kk: (i, j))
    in_specs = [a_spec, b_spec] + ([o_spec] if has_res else [])
    args = (a, b) + ((res,) if has_res else ())
    out_shape = jax.ShapeDtypeStruct((m, n), out_dtype)
    scratch = [pltpu.VMEM((tm, tn), F32)] if nk > 1 else []
    if cargo is None:
        return pl.pallas_call(
            body, name=name, grid=grid, in_specs=in_specs, out_specs=o_spec, out_shape=out_shape,
            scratch_shapes=scratch, compiler_params=_params(("parallel", "parallel", "arbitrary")),
        )(*args)
    outs = pl.pallas_call(
        body, name=name, grid=grid, in_specs=in_specs + cargo.specs(), out_specs=[o_spec] + cargo.specs(),
        out_shape=[out_shape] + cargo.out_shape(), scratch_shapes=scratch + cargo.scratch(),
        compiler_params=_params(("arbitrary", "arbitrary", "arbitrary")),
    )(*args, *cargo.srcs)
    return outs[0], list(outs[1:])


def _row_tile(t, cap=512):
    return min(t, cap)


def _rms(x, width):
    return lax.rsqrt(jnp.sum(x * x, axis=-1, keepdims=True) * (1.0 / width) + NORM_EPS)


def _rmsnorm_fwd(x, g, name):
    t, d = x.shape
    tr = _row_tile(t)

    def body(x_ref, g_ref, o_ref):
        xv = x_ref[...]
        o_ref[...] = ((xv * _rms(xv, d)) * g_ref[...]).astype(BF16)

    return pl.pallas_call(
        body, name=name, grid=(t // tr,),
        in_specs=[pl.BlockSpec((tr, d), lambda i: (i, 0)), pl.BlockSpec((1, d), lambda i: (0, 0))],
        out_specs=pl.BlockSpec((tr, d), lambda i: (i, 0)),
        out_shape=jax.ShapeDtypeStruct((t, d), BF16),
        compiler_params=_params(("parallel",)),
    )(x, g)


def _matmul_norm_bwd(a, b, x, g, dh_in, *, mode, name, out_scale, cargo=None):
    m, k = a.shape
    d = x.shape[1]
    tn = _pick(d, 512)

    def vmem(tm_):
        return 2 * 2 * (tm_ * k + k * tn) + tm_ * d * (4 + 2 * (4 + 4) + 2 * (4 + 2))

    tm = next((c for c in (_pick(m, 1024), _pick(m, 512), _pick(m, 256)) if vmem(c) <= MATMUL_VMEM), _pick(m, 256))
    grid = (m // tm, d // tn)
    dims = {'nn': NN_DIMS, 'nt': NT_DIMS}[mode]

    def body(*refs):
        steps = [pl.program_id(0), pl.program_id(1)]

        def compute(own):
            a_ref, b_ref, x_ref, g_ref, dhin_ref, dh_ref, dhb_ref, dg_ref, dn_ref = own
            for jj in range(grid[1]):
                @pl.when(steps[1] == jj)
                def _(jj=jj):
                    dn_ref[:, jj * tn:(jj + 1) * tn] = _dot(a_ref[...], b_ref[...], dims)

            @pl.when(steps[1] == grid[1] - 1)
            def _():
                xv = x_ref[...]
                dnv = dn_ref[...]
                r = _rms(xv, d)
                y = xv * r
                dy = dnv * g_ref[...]
                dh = dhin_ref[...] + r * (dy - y * (jnp.sum(dy * y, axis=-1, keepdims=True) * (1.0 / d)))
                dh_ref[...] = dh
                dhb_ref[...] = (dh * out_scale).astype(BF16)
                part = jnp.sum(dnv * y, axis=0, keepdims=True)

                @pl.when(steps[0] == 0)
                def _():
                    dg_ref[...] = part

                @pl.when(steps[0] > 0)
                def _():
                    dg_ref[...] += part

        _with_cargo(cargo, refs, 5, 3, steps, grid, compute)

    b_spec = pl.BlockSpec((k, tn), lambda i, j: (0, j)) if mode == 'nn' else pl.BlockSpec((tn, k), lambda i, j: (j, 0))
    row = pl.BlockSpec((tm, d), lambda i, j: (i, 0))
    vec = pl.BlockSpec((1, d), lambda i, j: (0, 0))
    extra = cargo.specs() if cargo else []
    outs = pl.pallas_call(
        body, name=name, grid=grid,
        in_specs=[pl.BlockSpec((tm, k), lambda i, j: (i, 0)), b_spec, row, vec, row] + extra,
        out_specs=[row, row, vec] + extra,
        out_shape=[jax.ShapeDtypeStruct((m, d), F32), jax.ShapeDtypeStruct((m, d), BF16),
                   jax.ShapeDtypeStruct((1, d), F32)] + (cargo.out_shape() if cargo else []),
        scratch_shapes=[pltpu.VMEM((tm, d), F32)] + (cargo.scratch() if cargo else []),
        compiler_params=_params(("arbitrary", "arbitrary")),
    )(a, b, x, g, dh_in, *(cargo.srcs if cargo else []))
    return (outs[0], outs[1], outs[2], list(outs[3:])) if cargo else outs


def _sigmoid(x):
    return 1.0 / (1.0 + jnp.exp(-x))


SWIGLU_CHUNK = 256
SWIGLU_COLS = 1408


def _chunks(width):
    return [slice(lo, min(lo + SWIGLU_CHUNK, width)) for lo in range(0, width, SWIGLU_CHUNK)]


def _swiglu_fwd(h, g, wt_in, name, cargo=None):
    t = h.shape[0]
    tm = _pick(t, 1024)
    grid = (t // tm, D_FF // SWIGLU_COLS)

    def body(*refs):
        steps = [pl.program_id(0), pl.program_id(1)]

        def compute(own):
            h_ref, g_ref, wa_ref, wb_ref, n_ref, a_ref, b_ref, act_ref, n_sc = own

            @pl.when(steps[1] == 0)
            def _():
                xv = h_ref[...]
                n_sc[...] = ((xv * _rms(xv, D_MODEL)) * g_ref[...]).astype(BF16)
                n_ref[...] = n_sc[...]

            nv = n_sc[...]
            for cols in _chunks(SWIGLU_COLS):
                a = _dot(nv, wa_ref[cols, :], NT_DIMS)
                b = _dot(nv, wb_ref[cols, :], NT_DIMS)
                a_ref[:, cols] = a.astype(BF16)
                b_ref[:, cols] = b.astype(BF16)
                act_ref[:, cols] = (a * _sigmoid(a) * b).astype(BF16)

        _with_cargo(cargo, refs, 4, 4, steps, grid, compute)

    half = D_FF // SWIGLU_COLS
    row = pl.BlockSpec((tm, D_MODEL), lambda i, j: (i, 0))
    tile = pl.BlockSpec((tm, SWIGLU_COLS), lambda i, j: (i, j))
    out = jax.ShapeDtypeStruct((t, D_FF), BF16)
    extra = cargo.specs() if cargo else []
    outs = pl.pallas_call(
        body, name=name, grid=grid,
        in_specs=[row, pl.BlockSpec((1, D_MODEL), lambda i, j: (0, 0)),
                  pl.BlockSpec((SWIGLU_COLS, D_MODEL), lambda i, j: (j, 0)),
                  pl.BlockSpec((SWIGLU_COLS, D_MODEL), lambda i, j: (half + j, 0))] + extra,
        out_specs=[row, tile, tile, tile] + extra,
        out_shape=[jax.ShapeDtypeStruct((t, D_MODEL), BF16), out, out, out] + (cargo.out_shape() if cargo else []),
        scratch_shapes=[pltpu.VMEM((tm, D_MODEL), BF16)] + (cargo.scratch() if cargo else []),
        compiler_params=_params(("arbitrary", "arbitrary")),
    )(h, g, wt_in, wt_in, *(cargo.srcs if cargo else []))
    return (outs[0], outs[1], outs[2], outs[3], list(outs[4:])) if cargo else outs


def _swiglu_bwd(dh, w_out, a, b, name):
    t = a.shape[0]
    tr = _row_tile(t, 512)

    def body(d_ref, w_ref, a_ref, b_ref, o_ref):
        dhv = d_ref[...]
        for cols in _chunks(D_FF):
            dv = _dot(dhv, w_ref[cols, :], NT_DIMS)
            av = a_ref[:, cols].astype(F32)
            s = _sigmoid(av)
            o_ref[:, cols] = (dv * b_ref[:, cols].astype(F32) * s * (1.0 + av * (1.0 - s))).astype(BF16)
            o_ref[:, slice(D_FF + cols.start, D_FF + cols.stop)] = (dv * av * s).astype(BF16)

    row = pl.BlockSpec((tr, D_FF), lambda i: (i, 0))
    return pl.pallas_call(
        body, name=name, grid=(t // tr,),
        in_specs=[pl.BlockSpec((tr, D_MODEL), lambda i: (i, 0)), pl.BlockSpec((D_FF, D_MODEL), lambda i: (0, 0)), row, row],
        out_specs=pl.BlockSpec((tr, 2 * D_FF), lambda i: (i, 0)),
        out_shape=jax.ShapeDtypeStruct((t, 2 * D_FF), BF16),
        compiler_params=_params(("parallel",)),
    )(dh, w_out, a, b)


def _latent_fwd(proj, gq, gkv):
    t = proj.shape[0]
    tr = _row_tile(t)

    def body(p_ref, gq_ref, gkv_ref, cq_ref, ckv_ref):
        cq = p_ref[:, P_CQ:P_CQ + Q_LORA].astype(F32)
        ckv = p_ref[:, P_CKV:P_CKV + KV_LORA].astype(F32)
        cq_ref[...] = ((cq * _rms(cq, Q_LORA)) * gq_ref[...]).astype(BF16)
        ckv_ref[...] = ((ckv * _rms(ckv, KV_LORA)) * gkv_ref[...]).astype(BF16)

    return pl.pallas_call(
        body, name="latent_fwd", grid=(t // tr,),
        in_specs=[pl.BlockSpec((tr, 1024), lambda i: (i, 0)), pl.BlockSpec((1, Q_LORA), lambda i: (0, 0)),
                  pl.BlockSpec((1, KV_LORA), lambda i: (0, 0))],
        out_specs=[pl.BlockSpec((tr, Q_LORA), lambda i: (i, 0)), pl.BlockSpec((tr, KV_LORA), lambda i: (i, 0))],
        out_shape=[jax.ShapeDtypeStruct((t, Q_LORA), BF16), jax.ShapeDtypeStruct((t, KV_LORA), BF16)],
        compiler_params=_params(("parallel",)),
    )(proj, gq, gkv)


def _latent_bwd(dcqn, dckvn, proj, dkr, gq, gkv):
    t = proj.shape[0]
    tr = _row_tile(t, 256)

    def norm_bwd(dn, x, g, width):
        r = _rms(x, width)
        y = x * r
        dy = dn * g
        dx = r * (dy - y * (jnp.sum(dy * y, axis=-1, keepdims=True) * (1.0 / width)))
        return dx, jnp.sum(dn * y, axis=0, keepdims=True)

    def body(dcq_ref, dckv_ref, p_ref, dkr_ref, gq_ref, gkv_ref, o_ref, dgq_ref, dgkv_ref):
        i = pl.program_id(0)
        dcq, pq = norm_bwd(dcq_ref[...], p_ref[:, P_CQ:P_CQ + Q_LORA].astype(F32), gq_ref[...], Q_LORA)
        dckv, pkv = norm_bwd(dckv_ref[...], p_ref[:, P_CKV:P_CKV + KV_LORA].astype(F32), gkv_ref[...], KV_LORA)
        o_ref[:, P_CQ:P_CQ + Q_LORA] = dcq.astype(BF16)
        o_ref[:, P_CKV:P_CKV + KV_LORA] = dckv.astype(BF16)
        o_ref[:, P_KR:P_KR + 128] = dkr_ref[...].astype(BF16)
        o_ref[:, P_KR + 128:1024] = jnp.zeros((tr, 1024 - P_KR - 128), BF16)

        @pl.when(i == 0)
        def _():
            dgq_ref[...] = pq
            dgkv_ref[...] = pkv

        @pl.when(i > 0)
        def _():
            dgq_ref[...] += pq
            dgkv_ref[...] += pkv

    def row(w):
        return pl.BlockSpec((tr, w), lambda i: (i, 0))

    def vec(w):
        return pl.BlockSpec((1, w), lambda i: (0, 0))

    return pl.pallas_call(
        body, name="latent_bwd", grid=(t // tr,),
        in_specs=[row(Q_LORA), row(KV_LORA), row(1024), row(128), vec(Q_LORA), vec(KV_LORA)],
        out_specs=[row(1024), vec(Q_LORA), vec(KV_LORA)],
        out_shape=[jax.ShapeDtypeStruct((t, 1024), BF16), jax.ShapeDtypeStruct((1, Q_LORA), F32),
                   jax.ShapeDtypeStruct((1, KV_LORA), F32)],
        compiler_params=_params(("arbitrary",)),
    )(dcqn, dckvn, proj, dkr, gq, gkv)


def _rope(y, cosf, sin_a, sin_b):
    return y * cosf + pltpu.roll(y, 112, 1) * sin_a + pltpu.roll(y, 16, 1) * sin_b


def _rope_t(d, cosf, sin_a, sin_b):
    return d * cosf + pltpu.roll(d * sin_a, 16, 1) + pltpu.roll(d * sin_b, 112, 1)


def _headprep_fwd(qraw, kvraw, proj, cosf, sin_a, sin_b, gqh, gkh):
    t = qraw.shape[0]
    tr = _row_tile(t, 256)

    def body(q_ref, kv_ref, kr_ref, c_ref, sa_ref, sb_ref, gq_ref, gk_ref, qh_ref, kh_ref, kvb_ref):
        cv, sa, sb = c_ref[...], sa_ref[...], sb_ref[...]
        kr = kr_ref[...].astype(F32)
        lane = lax.broadcasted_iota(jnp.int32, (tr, HEAD_PAD), 1)
        heads = [slice(h * HEAD_PAD, (h + 1) * HEAD_PAD) for h in range(N_HEADS)]
        xqs = [q_ref[:, cols] for cols in heads]
        kvs = [kv_ref[:, cols] for cols in heads]
        xks = [jnp.where(lane < MLA_NOPE, kvh, kr) for kvh in kvs]
        rqs = [_rms(x, MLA_QK) for x in xqs]
        rks = [_rms(x, MLA_QK) for x in xks]
        gq = gq_ref[...] * MLA_Q_SCALE
        yqs = [(x * r) * gq for x, r in zip(xqs, rqs)]
        yks = [(x * r) * gk_ref[...] for x, r in zip(xks, rks)]
        for cols, yq, yk, kvh in zip(heads, yqs, yks, kvs):
            qh_ref[:, cols] = _rope(yq, cv, sa, sb).astype(BF16)
            kh_ref[:, cols] = _rope(yk, cv, sa, sb).astype(BF16)
            kvb_ref[:, cols] = jnp.where(lane < MLA_NOPE, 1.0, kvh).astype(BF16)

    wide = pl.BlockSpec((tr, 1024), lambda i: (i, 0))
    lanes = pl.BlockSpec((tr, HEAD_PAD), lambda i: (i, 0))
    vec = pl.BlockSpec((1, HEAD_PAD), lambda i: (0, 0))
    return pl.pallas_call(
        body, name="headprep_fwd", grid=(t // tr,),
        in_specs=[wide, wide, pl.BlockSpec((tr, HEAD_PAD), lambda i: (i, P_KR // HEAD_PAD)), lanes, lanes, lanes, vec, vec],
        out_specs=[wide, wide, wide],
        out_shape=[jax.ShapeDtypeStruct((t, 1024), BF16)] * 3,
        compiler_params=_params(("parallel",)),
    )(qraw, kvraw, proj, cosf, sin_a, sin_b, gqh, gkh)


def _headprep_bwd(dqh, dkh, dvp, qraw, kvraw, proj, cosf, sin_a, sin_b, gqh, gkh):
    t = qraw.shape[0]
    tr = _row_tile(t, 256)

    def body(dq_ref, dk_ref, dv_ref, q_ref, kv_ref, kr_ref, c_ref, sa_ref, sb_ref, gq_ref, gk_ref,
             dqr_ref, dkvr_ref, dkr_ref, dgq_ref, dgk_ref):
        i = pl.program_id(0)
        cv, sa, sb = c_ref[...], sa_ref[...], sb_ref[...]
        kr = kr_ref[...].astype(F32)
        lane = lax.broadcasted_iota(jnp.int32, (tr, HEAD_PAD), 1)
        heads = [slice(h * HEAD_PAD, (h + 1) * HEAD_PAD) for h in range(N_HEADS)]
        xs = [q_ref[:, cols] for cols in heads] + [jnp.where(lane < MLA_NOPE, kv_ref[:, cols], kr) for cols in heads]
        gs = [gq_ref[...]] * N_HEADS + [gk_ref[...]] * N_HEADS
        dns = [_rope_t(ref[:, cols], cv, sa, sb) for ref in (dq_ref, dk_ref) for cols in heads]
        rs = [_rms(x, MLA_QK) for x in xs]
        ys = [x * r for x, r in zip(xs, rs)]
        dys = [dn * g for dn, g in zip(dns, gs)]
        means = [jnp.sum(dy * y, axis=-1, keepdims=True) * (1.0 / MLA_QK) for dy, y in zip(dys, ys)]
        dxs = [r * (dy - y * m) for r, dy, y, m in zip(rs, dys, ys, means)]
        parts = [jnp.sum(dn * y, axis=0, keepdims=True) for dn, y in zip(dns, ys)]
        dkr = jnp.zeros((tr, HEAD_PAD), F32)
        pq = jnp.zeros((1, HEAD_PAD), F32)
        pk = jnp.zeros((1, HEAD_PAD), F32)
        for h, cols in enumerate(heads):
            dqr_ref[:, cols] = dxs[h].astype(BF16)
            dxk = dxs[N_HEADS + h]
            dkvr_ref[:, cols] = jnp.where(lane < MLA_NOPE, dxk, dv_ref[:, cols]).astype(BF16)
            dkr = dkr + jnp.where(lane < MLA_NOPE, 0.0, dxk)
            pq = pq + parts[h]
            pk = pk + parts[N_HEADS + h]
        dkr_ref[...] = dkr

        @pl.when(i == 0)
        def _():
            dgq_ref[...] = pq
            dgk_ref[...] = pk

        @pl.when(i > 0)
        def _():
            dgq_ref[...] += pq
            dgk_ref[...] += pk

    wide = pl.BlockSpec((tr, 1024), lambda i: (i, 0))
    lanes = pl.BlockSpec((tr, HEAD_PAD), lambda i: (i, 0))
    vec = pl.BlockSpec((1, HEAD_PAD), lambda i: (0, 0))
    return pl.pallas_call(
        body, name="headprep_bwd", grid=(t // tr,),
        in_specs=[wide, wide, wide, wide, wide, pl.BlockSpec((tr, HEAD_PAD), lambda i: (i, P_KR // HEAD_PAD)),
                  lanes, lanes, lanes, vec, vec],
        out_specs=[wide, wide, lanes, vec, vec],
        out_shape=[jax.ShapeDtypeStruct((t, 1024), BF16), jax.ShapeDtypeStruct((t, 1024), BF16),
                   jax.ShapeDtypeStruct((t, HEAD_PAD), F32), jax.ShapeDtypeStruct((1, HEAD_PAD), F32),
                   jax.ShapeDtypeStruct((1, HEAD_PAD), F32)],
        compiler_params=_params(("arbitrary",)),
    )(dqh, dkh, dvp, qraw, kvraw, proj, cosf, sin_a, sin_b, gqh, gkh)


def _merge_fwd(o_mla, w_mla, o_sb, w_sb, proj):
    t = proj.shape[0]
    tr = _row_tile(t, 512)

    def body(om_ref, wm_ref, os_ref, ws_ref, gm_ref, gs_ref, o_ref, bm_ref, bs_ref):
        omv, osv = om_ref[...], os_ref[...]
        for cols in _chunks(D_MODEL):
            bm = _dot(omv, wm_ref[:, cols], NN_DIMS)
            bs = _dot(osv, ws_ref[:, cols], NN_DIMS)
            bm_ref[:, cols] = bm
            bs_ref[:, cols] = bs
            gm = _sigmoid(gm_ref[:, cols].astype(F32))
            gs = _sigmoid(gs_ref[:, cols].astype(F32))
            o_ref[:, cols] = (gm * bm + gs * bs).astype(BF16)

    row = pl.BlockSpec((tr, 1024), lambda i: (i, 0))
    f32 = jax.ShapeDtypeStruct((t, 1024), F32)
    return pl.pallas_call(
        body, name="merge_fwd", grid=(t // tr,),
        in_specs=[row, pl.BlockSpec(w_mla.shape, lambda i: (0, 0)),
                  pl.BlockSpec((tr, SB_WIDTH), lambda i: (i, 0)), pl.BlockSpec(w_sb.shape, lambda i: (0, 0)),
                  pl.BlockSpec((tr, 1024), lambda i: (i, P_GM // 1024)), pl.BlockSpec((tr, 1024), lambda i: (i, P_GS // 1024))],
        out_specs=[row, row, row], out_shape=[jax.ShapeDtypeStruct((t, 1024), BF16), f32, f32],
        compiler_params=_params(("parallel",)),
    )(o_mla, w_mla, o_sb, w_sb, proj, proj)


def _merge_bwd(dh, w_out, proj, bm, bs):
    t = proj.shape[0]
    tr = _row_tile(t, 512)

    def body(d_ref, w_ref, gm_ref, gs_ref, bm_ref, bs_ref, dbm_ref, dbs_ref, dg_ref):
        dhv = d_ref[...]
        for cols in _chunks(D_MODEL):
            dm = _dot(dhv, w_ref[cols, :], NT_DIMS)
            gm = _sigmoid(gm_ref[:, cols].astype(F32))
            gs = _sigmoid(gs_ref[:, cols].astype(F32))
            dbm_ref[:, cols] = (dm * gm).astype(BF16)
            dbs_ref[:, cols] = (dm * gs).astype(BF16)
            dg_ref[:, cols] = (dm * bm_ref[:, cols] * gm * (1.0 - gm)).astype(BF16)
            dg_ref[:, slice(D_MODEL + cols.start, D_MODEL + cols.stop)] = (dm * bs_ref[:, cols] * gs * (1.0 - gs)).astype(BF16)

    row = pl.BlockSpec((tr, 1024), lambda i: (i, 0))
    return pl.pallas_call(
        body, name="mix_out_dx", grid=(t // tr,),
        in_specs=[row, pl.BlockSpec((D_MODEL, D_MODEL), lambda i: (0, 0)),
                  pl.BlockSpec((tr, 1024), lambda i: (i, P_GM // 1024)),
                  pl.BlockSpec((tr, 1024), lambda i: (i, P_GS // 1024)), row, row],
        out_specs=[row, row, pl.BlockSpec((tr, 2048), lambda i: (i, 0))],
        out_shape=[jax.ShapeDtypeStruct((t, 1024), BF16), jax.ShapeDtypeStruct((t, 1024), BF16),
                   jax.ShapeDtypeStruct((t, 2048), BF16)],
        compiler_params=_params(("parallel",)),
    )(dh, w_out, proj, proj, bm, bs)


def _ple_loss(h3, g, w_gate, pb, w_proj, tgt):
    t = h3.shape[0]
    tr = _row_tile(t, 512)

    def body(h_ref, g_ref, wg_ref, p_ref, wp_ref, t_ref, n_ref, dh_ref, dz_ref, dp_ref, l_ref):
        i = pl.program_id(0)
        xv = h_ref[...]
        nv = ((xv * _rms(xv, D_MODEL)) * g_ref[...]).astype(BF16)
        n_ref[...] = nv
        pv = p_ref[...]
        part = jnp.zeros((1, 128), F32)
        for cols in _chunks(D_MODEL):
            pg = _sigmoid(_dot(nv, wg_ref[:, cols], NN_DIMS))
            ppv = _dot(pv, wp_ref[:, cols], NN_DIMS)
            diff = (h_ref[:, cols] + pg * ppv) - t_ref[:, cols]
            dh = diff * (1.0 / D_MODEL)
            dh_ref[:, cols] = dh
            dp_ref[:, cols] = (dh * pg).astype(BF16)
            dz_ref[:, cols] = (dh * ppv * pg * (1.0 - pg)).astype(BF16)
            sq = jnp.sum(diff * diff, axis=0, keepdims=True)
            for c in range(sq.shape[1] // 128):
                part = part + sq[:, c * 128:(c + 1) * 128]

        @pl.when(i == 0)
        def _():
            l_ref[...] = part

        @pl.when(i > 0)
        def _():
            l_ref[...] += part

    row = pl.BlockSpec((tr, 1024), lambda i: (i, 0))
    return pl.pallas_call(
        body, name="ple_loss", grid=(t // tr,),
        in_specs=[row, pl.BlockSpec((1, D_MODEL), lambda i: (0, 0)), pl.BlockSpec((D_MODEL, D_MODEL), lambda i: (0, 0)),
                  pl.BlockSpec((tr, PLE_DIM), lambda i: (i, 0)), pl.BlockSpec((PLE_DIM, D_MODEL), lambda i: (0, 0)), row],
        out_specs=[row, row, row, row, pl.BlockSpec((1, 128), lambda i: (0, 0))],
        out_shape=[jax.ShapeDtypeStruct((t, 1024), BF16), jax.ShapeDtypeStruct((t, 1024), F32),
                   jax.ShapeDtypeStruct((t, 1024), BF16), jax.ShapeDtypeStruct((t, 1024), BF16),
                   jax.ShapeDtypeStruct((1, 128), F32)],
        compiler_params=_params(("arbitrary",)),
    )(h3, g, w_gate, pb, w_proj, tgt)


ATT_BLOCK = 256
MLA_Q_SCALE = math.log2(math.e) / math.sqrt(MLA_QK)
MLA_Q_BLOCK = 512
MLA_FWD_COLS = 4
MLA_BWD_COLS = 4
SB_FWD_COLS = 4
SB_BWD_COLS = 2
SB_BLOCK = 256


def _split_bf16(x):
    hi = x.astype(BF16)
    return hi, (x - hi.astype(F32)).astype(BF16)


def _tri(kind, n):
    r = lax.broadcasted_iota(jnp.int32, (n, n), 0)
    c = lax.broadcasted_iota(jnp.int32, (n, n), 1)
    cond = {'gt': r > c, 'le': r <= c, 'lt': r < c}[kind]
    return jnp.where(cond, 1.0, 0.0).astype(BF16)


def _causal(strict, n=ATT_BLOCK):
    r = lax.broadcasted_iota(jnp.int32, (n, n), 0)
    c = lax.broadcasted_iota(jnp.int32, (n, n), 1)
    return (c < r) if strict else (c <= r)


def _below_diagonal(rows, offset):
    r = lax.broadcasted_iota(jnp.int32, (rows, ATT_BLOCK), 0)
    c = lax.broadcasted_iota(jnp.int32, (rows, ATT_BLOCK), 1)
    return c + offset <= r


def _lanes(c):
    return slice(c * HEAD_PAD, (c + 1) * HEAD_PAD)


def _row_block(j, n=ATT_BLOCK):
    return pl.ds(pl.multiple_of(j * n, n), n)


def _rows(ref, j, c, n=ATT_BLOCK):
    return ref[_row_block(j, n), _lanes(c)]


def _mla_fwd(qh, kh, kvb, cargo=None):
    t = qh.shape[0]
    bq = min(MLA_Q_BLOCK, t)
    per_q = bq // ATT_BLOCK
    ncol = MLA_FWD_COLS
    grid = (N_HEADS // ncol, t // bq)

    def body(*refs):
        steps = [pl.program_id(0), pl.program_id(1)]
        _with_cargo(cargo, refs, 3, 2, steps, grid, lambda own: work(steps[1], *own))

    def work(i, q_ref, k_ref, v_ref, o_ref, lse_ref):
        qs = [q_ref[:, _lanes(c)] for c in range(ncol)]

        def step(j, carry, masked):
            cols = range(ncol)
            scores = [_dot(qs[c], _rows(k_ref, j, c), NT_DIMS) for c in cols]
            ms, ps, alphas = [], [], []
            for c in cols:
                s = scores[c]
                if masked is not None:
                    s = jnp.where(_below_diagonal(bq, masked), s, -1e30)
                m_new = jnp.maximum(carry[c][0], jnp.max(s, axis=-1, keepdims=True))
                ps.append(jnp.exp2(s - m_new).astype(BF16))
                alphas.append(jnp.exp2(carry[c][0] - m_new))
                ms.append(m_new)
            return tuple((ms[c], alphas[c] * carry[c][1] + _dot(ps[c], _rows(v_ref, j, c), NN_DIMS)) for c in cols)

        init = tuple((jnp.full((bq, 1), -1e30, F32), jnp.zeros((bq, HEAD_PAD), F32)) for _ in range(ncol))
        carry = lax.fori_loop(0, i * per_q, lambda j, cr: step(j, cr, None), init)
        for d in range(per_q):
            carry = step(i * per_q + d, carry, d * ATT_BLOCK)
        for c, (m, acc) in enumerate(carry):
            l = acc[:, 0:1]
            o_ref[:, _lanes(c)] = (acc / l).astype(BF16)
            lse_ref[c] = m + jnp.log2(l)

    width = ncol * HEAD_PAD
    full = pl.BlockSpec((t, width), lambda h, i: (0, h))
    blk = pl.BlockSpec((bq, width), lambda h, i: (i, h))
    extra = cargo.specs() if cargo else []
    outs = pl.pallas_call(
        body, name="mla_fwd", grid=grid,
        in_specs=[blk, full, full] + extra,
        out_specs=[blk, pl.BlockSpec((ncol, bq, 1), lambda h, i: (h, i, 0))] + extra,
        out_shape=[jax.ShapeDtypeStruct((t, N_HEADS * HEAD_PAD), BF16), jax.ShapeDtypeStruct((N_HEADS, t, 1), F32)]
        + (cargo.out_shape() if cargo else []),
        scratch_shapes=cargo.scratch() if cargo else [],
        compiler_params=_params(("arbitrary", "arbitrary")),
    )(qh, kh, kvb, *(cargo.srcs if cargo else []))
    return (outs[0], outs[1], list(outs[2:])) if cargo else outs


def _mla_bwd(qh, kh, kvb, o, do, lse, cargo=None):
    t = qh.shape[0]
    bq = min(MLA_Q_BLOCK, t)
    per_q = bq // ATT_BLOCK
    ncol = MLA_BWD_COLS
    width = ncol * HEAD_PAD
    grid = (N_HEADS // ncol, t // bq)

    def body(*refs):
        steps = [pl.program_id(0), pl.program_id(1)]
        _with_cargo(cargo, refs, 6, 3, steps, grid, lambda own: work(steps[0], steps[1], *own))

    def work(h, i, q_ref, k_ref, v_ref, o_ref, do_ref, lse_ref, dq_ref, dk_hbm, dv_hbm, dk_ref, dv_ref, out_sems):

        @pl.when(i == 0)
        def _():
            dk_ref[...] = jnp.zeros_like(dk_ref)
            dv_ref[...] = jnp.zeros_like(dv_ref)

        qs = [q_ref[:, _lanes(c)] for c in range(ncol)]
        dos = [do_ref[:, _lanes(c)] for c in range(ncol)]
        deltas = [jnp.sum(dos[c].astype(F32) * o_ref[:, _lanes(c)].astype(F32), axis=-1, keepdims=True)
                  for c in range(ncol)]
        lses = [lse_ref[c] for c in range(ncol)]

        def step(j, dqs, masked):
            cols = range(ncol)
            kbs = [_rows(k_ref, j, c) for c in cols]
            scores = [_dot(qs[c], kbs[c], NT_DIMS) for c in cols]
            dps = [_dot(dos[c], _rows(v_ref, j, c), NT_DIMS) for c in cols]
            pbs, dss = [], []
            for c in cols:
                p = jnp.exp2(scores[c] - lses[c])
                if masked is not None:
                    p = jnp.where(_below_diagonal(bq, masked), p, 0.0)
                pbs.append(p.astype(BF16))
                dss.append((p * (dps[c] - deltas[c])).astype(BF16))
            for c in cols:
                dv_ref[_row_block(j), _lanes(c)] += _dot(pbs[c], dos[c], TN_DIMS)
                dk_ref[_row_block(j), _lanes(c)] += _dot(dss[c], qs[c], TN_DIMS)
            return tuple(dqs[c] + _dot(dss[c], kbs[c], NN_DIMS) for c in cols)

        init = tuple(jnp.zeros((bq, HEAD_PAD), F32) for _ in range(ncol))
        dqs = lax.fori_loop(0, i * per_q, lambda j, cr: step(j, cr, None), init)
        for d in range(per_q):
            dqs = step(i * per_q + d, dqs, d * ATT_BLOCK)
        for c, dq in enumerate(dqs):
            dq_ref[:, _lanes(c)] = dq * (1.0 / math.sqrt(MLA_QK))

        @pl.when(i == grid[1] - 1)
        def _():
            dk_ref[...] = dk_ref[...] * math.log(2.0)
            cols = pl.ds(pl.multiple_of(h * width, width), width)
            out = [pltpu.make_async_copy(dk_ref, dk_hbm.at[:, cols], out_sems.at[0]),
                   pltpu.make_async_copy(dv_ref, dv_hbm.at[:, cols], out_sems.at[1])]
            for cp in out:
                cp.start()
            for cp in out:
                cp.wait()

    full = pl.BlockSpec((t, width), lambda h, i: (0, h))
    blk = pl.BlockSpec((bq, width), lambda h, i: (i, h))
    wide = jax.ShapeDtypeStruct((t, N_HEADS * HEAD_PAD), F32)
    extra = cargo.specs() if cargo else []
    outs = pl.pallas_call(
        body, name="mla_bwd", grid=grid,
        in_specs=[blk, full, full, blk, blk, pl.BlockSpec((ncol, bq, 1), lambda h, i: (h, i, 0))] + extra,
        out_specs=[blk, HBM_SPEC, HBM_SPEC] + extra,
        out_shape=[wide, wide, wide] + (cargo.out_shape() if cargo else []),
        scratch_shapes=[pltpu.VMEM((t, width), F32), pltpu.VMEM((t, width), F32), pltpu.SemaphoreType.DMA((2,))]
        + (cargo.scratch() if cargo else []),
        compiler_params=_params(("arbitrary", "arbitrary")),
    )(qh, kh, kvb, o, do, lse, *(cargo.srcs if cargo else []))
    return (outs[0], outs[1], outs[2], list(outs[3:])) if cargo else outs


def _head_only(x, lane, u):
    return jnp.where((lane >= u * SB_DIM) & (lane < (u + 1) * SB_DIM), x, jnp.zeros_like(x))


SB_DEAD = -104.0


def _log_sigmoids(z):
    e = jnp.exp(-jnp.abs(z))
    lg = jnp.log(1.0 + e)
    ls_pos = jnp.minimum(z, 0.0) - lg
    return ls_pos, ls_pos - z, e


def _sb_fwd(proj):
    t = proj.shape[0]
    bq, ncol = SB_BLOCK, SB_FWD_COLS
    nq = t // bq
    scale = 1.0 / math.sqrt(SB_DIM)
    pairs = SB_WIDTH // HEAD_PAD

    def body(q_ref, k_ref, v_ref, o_ref, r_ref, first_ref):
        g, i = pl.program_id(0), pl.program_id(1)
        lane = lax.broadcasted_iota(jnp.int32, (bq, HEAD_PAD), 1)
        upper = _tri('gt', bq)
        chains = [(c, u) for c in range(ncol) for u in range(2)]
        qms = [_head_only(q_ref[:, _lanes(c)], lane, u) * scale for c, u in chains]

        def step(j, carry, masked):
            ids = range(len(chains))
            zs = [_dot(qms[n], _rows(k_ref, j, chains[n][0], bq), NT_DIMS) for n in ids]
            pos, neg, parts = [], [], []
            for n in ids:
                ls_pos, ls_neg, _ = _log_sigmoids(zs[n])
                if masked:
                    ls_neg = jnp.where(_causal(True, bq), ls_neg, 0.0)
                pos.append(ls_pos)
                neg.append(ls_neg)
                parts.append(_split_bf16(ls_neg))
            suffix = [_dot(parts[n][0], upper, NN_DIMS) + _dot(parts[n][1], upper, NN_DIMS) for n in ids]
            weights = []
            for n in ids:
                a = jnp.exp(pos[n] + suffix[n] + carry[n][0])
                if masked:
                    a = jnp.where(_causal(True, bq), a, 0.0)
                weights.append(a.astype(BF16))
            return tuple((carry[n][0] + jnp.sum(neg[n], axis=-1, keepdims=True),
                          carry[n][1] + _dot(weights[n], _rows(v_ref, j, chains[n][0], bq), NN_DIMS)) for n in ids)

        init = tuple((jnp.zeros((bq, 1), F32), jnp.zeros((bq, HEAD_PAD), F32)) for _ in chains)
        carry = step(i, init, True)

        def more(state):
            s, cr = state
            live = cr[0][0]
            for n in range(1, len(chains)):
                live = jnp.maximum(live, cr[n][0])
            return jnp.logical_and(s < i, jnp.max(live) > SB_DEAD)

        walked, carry = lax.while_loop(more, lambda st: (st[0] + 1, step(i - 1 - st[0], st[1], False)),
                                       (jnp.int32(0), carry))
        first_ref[g * nq + i] = i - walked
        for n, (c, u) in enumerate(chains):
            r_ref[2 * c + u] = carry[n][0]
        for c in range(ncol):
            o_ref[:, _lanes(c)] = jnp.where(lane < SB_DIM, carry[2 * c][1], carry[2 * c + 1][1]).astype(BF16)

    width = ncol * HEAD_PAD

    def full(c0):
        return pl.BlockSpec((t, width), lambda g, i: (0, c0 // width + g))

    return pl.pallas_call(
        body, name="sb_fwd", grid=(pairs // ncol, t // bq),
        in_specs=[pl.BlockSpec((bq, width), lambda g, i: (i, P_SBQ // width + g)), full(P_SBK), full(P_SBV)],
        out_specs=[pl.BlockSpec((bq, width), lambda g, i: (i, g)),
                   pl.BlockSpec((2 * ncol, bq, 1), lambda g, i: (g, i, 0)),
                   pl.BlockSpec(memory_space=pltpu.SMEM)],
        out_shape=[jax.ShapeDtypeStruct((t, SB_WIDTH), BF16), jax.ShapeDtypeStruct((N_HEADS, t, 1), F32),
                   jax.ShapeDtypeStruct((pairs // ncol * nq,), jnp.int32)],
        compiler_params=_params(("arbitrary", "arbitrary")),
    )(proj, proj, proj)


def _sb_bwd(proj, do, rtot, first):
    t = proj.shape[0]
    bq, ncol = SB_BLOCK, SB_BWD_COLS
    nq = t // bq
    scale = 1.0 / math.sqrt(SB_DIM)
    pairs = SB_WIDTH // HEAD_PAD

    def body(first_ref, q_ref, k_ref, v_ref, do_ref, r_ref, dq_ref, dk_ref, dv_ref):
        g, i = pl.program_id(0), pl.program_id(1)

        @pl.when(i == 0)
        def _():
            dk_ref[...] = jnp.zeros_like(dk_ref)
            dv_ref[...] = jnp.zeros_like(dv_ref)

        lane = lax.broadcasted_iota(jnp.int32, (bq, HEAD_PAD), 1)
        incl = _tri('le', bq)
        excl = _tri('lt', bq)
        chains = [(c, u) for c in range(ncol) for u in range(2)]
        qms = [_head_only(q_ref[:, _lanes(c)], lane, u) * scale for c, u in chains]
        doms = [_head_only(do_ref[:, _lanes(c)], lane, u) for c, u in chains]
        rts = [r_ref[2 * c + u] for c, u in chains]

        def step(j, carry, masked):
            ids = range(len(chains))
            kbs = [_rows(k_ref, j, c, bq) for c in range(ncol)]
            zs =[_dot(qms[n], kbs[chains[n][0]], NT_DIMS) for n in ids]
            das = [_dot(doms[n], _rows(v_ref, j, chains[n][0], bq), NT_DIMS) for n in ids]
            pos, neg, sigs, parts = [], [], [], []
            for n in ids:
                ls_pos, ls_neg, e = _log_sigmoids(zs[n])
                if masked:
                    ls_neg = jnp.where(_causal(True, bq), ls_neg, 0.0)
                pos.append(ls_pos)
                neg.append(ls_neg)
                sigs.append(jnp.where(zs[n] >= 0.0, 1.0, e) * pl.reciprocal(1.0 + e, approx=True))
                parts.append(_split_bf16(ls_neg))
            prefix = [_dot(parts[n][0], incl, NN_DIMS) + _dot(parts[n][1], incl, NN_DIMS) for n in ids]
            evs, eparts, dvs = [], [], []
            for n in ids:
                a = jnp.exp(pos[n] + (rts[n] - (carry[n][0] + prefix[n])))
                if masked:
                    a = jnp.where(_causal(True, bq), a, 0.0)
                dvs.append(_dot(a.astype(BF16), doms[n], TN_DIMS))
                evs.append(a * das[n])
                eparts.append(evs[n].astype(BF16))
            before = [_dot(eparts[n], excl, NN_DIMS) for n in ids]
            out, dks = [], []
            for n in ids:
                dz = evs[n] - sigs[n] * (evs[n] + (carry[n][1] + before[n]))
                if masked:
                    dz = jnp.where(_causal(True, bq), dz, 0.0)
                dzb = dz.astype(BF16)
                dks.append(_dot(dzb, qms[n], TN_DIMS))
                out.append((carry[n][0] + jnp.sum(neg[n], axis=-1, keepdims=True),
                            carry[n][1] + jnp.sum(evs[n], axis=-1, keepdims=True),
                            carry[n][2] + _dot(dzb, kbs[chains[n][0]], NN_DIMS)))
            for c in range(ncol):
                dv_ref[_row_block(j, bq), _lanes(c)] += dvs[2 * c] + dvs[2 * c + 1]
                dk_ref[_row_block(j, bq), _lanes(c)] += dks[2 * c] + dks[2 * c + 1]
            return tuple(out)

        init = tuple((jnp.zeros((bq, 1), F32), jnp.zeros((bq, 1), F32), jnp.zeros((bq, HEAD_PAD), F32)) for _ in chains)
        start = first_ref[(g * ncol // SB_FWD_COLS) * nq + i]
        carry = lax.fori_loop(start, i, lambda j, cr: step(j, cr, False), init)
        carry = step(i, carry, True)
        for c in range(ncol):
            dq_ref[:, _lanes(c)] = jnp.where(lane < SB_DIM, carry[2 * c][2], carry[2 * c + 1][2]) * scale

    width = ncol * HEAD_PAD

    def full(c0):
        return pl.BlockSpec((t, width), lambda g, i, first: (0, c0 // width + g))

    blk = pl.BlockSpec((bq, width), lambda g, i, first: (i, g))
    acc = pl.BlockSpec((t, width), lambda g, i, first: (0, g))
    wide = jax.ShapeDtypeStruct((t, SB_WIDTH), F32)
    return pl.pallas_call(
        body, name="sb_bwd",
        grid_spec=pltpu.PrefetchScalarGridSpec(
            num_scalar_prefetch=1, grid=(pairs // ncol, nq),
            in_specs=[pl.BlockSpec((bq, width), lambda g, i, first: (i, P_SBQ // width + g)), full(P_SBK), full(P_SBV),
                      blk, pl.BlockSpec((2 * ncol, bq, 1), lambda g, i, first: (g, i, 0))],
            out_specs=[blk, acc, acc]),
        out_shape=[wide, wide, wide],
        compiler_params=_params(("arbitrary", "arbitrary")),
    )(first, proj, proj, proj, do, rtot)


def _cols_to_full(g):
    n, r, c = g.shape
    return jnp.transpose(g, (1, 0, 2)).reshape(r, n * c)


def _full_to_cols(w):
    r, c = w.shape
    return jnp.transpose(w.reshape(r, N_DEV, c // N_DEV), (1, 0, 2))


TRANSPOSED = ('ffn1_w_in', 'ffn2_w_in', 'w_in', 'w_q_up')


def _layout_weight(name, g):
    if name in ('ffn1_w_out', 'ffn2_w_out', 'w_out', 'w_ple_gate', 'ffn1_w_in', 'ffn2_w_in'):
        return g.reshape(g.shape[0] * g.shape[1], g.shape[2])
    if name == 'w_in':
        wt = g.reshape(IN_COLS, D_MODEL)
        z = lambda n: jnp.zeros((n, D_MODEL), BF16)
        return jnp.concatenate([wt[0:640], z(64), wt[640:672], z(32), z(256), wt[2208:4256], wt[672:2208]], axis=0)
    if name == 'w_q_up':
        return jnp.pad(g, ((0, 0), (0, HEAD_PAD - MLA_QK), (0, 0))).reshape(N_HEADS * HEAD_PAD, Q_LORA)
    if name == 'w_branch_mla':
        bm = _cols_to_full(g).reshape(N_HEADS, MLA_NOPE, D_MODEL)
        return jnp.pad(bm, ((0, 0), (HEAD_PAD - MLA_NOPE, 0), (0, 0))).reshape(N_HEADS * HEAD_PAD, D_MODEL)
    return _cols_to_full(g)


def _layout_weights(g):
    return {n: _layout_weight(n, a) for n, a in g.items()}


def _unlayout_grad(name, d):
    if name == 'w_in':
        d = jnp.concatenate([d[0:640], d[704:736], d[P_SBQ:PROJ_W], d[P_GM:P_SBQ]], axis=0)
    if name == 'w_q_up':
        return d.reshape(N_HEADS, HEAD_PAD, Q_LORA)[:, :MLA_QK, :]
    if name in ('ffn1_w_out', 'ffn2_w_out', 'w_out', 'w_ple_gate', 'ffn1_w_in', 'ffn2_w_in', 'w_in'):
        return d.reshape(N_DEV, d.shape[0] // N_DEV, d.shape[1])
    if name == 'w_branch_mla':
        d = d.reshape(N_HEADS, HEAD_PAD, D_MODEL)[:, HEAD_PAD - MLA_NOPE:, :].reshape(SB_WIDTH, D_MODEL)
    return _full_to_cols(d)


def _unlayout_grads(d):
    return {n: _unlayout_grad(n, a) for n, a in d.items()}


def _rope_tables(positions):
    half = MLA_ROPE // 2
    inv_freq = ROPE_BASE ** (-jnp.arange(0, MLA_ROPE, 2, dtype=F32) / MLA_ROPE)
    ang = positions.astype(F32)[:, None] * inv_freq
    cos, sin = jnp.cos(ang), jnp.sin(ang)
    t = positions.shape[0]
    ones = lambda n: jnp.ones((t, n), F32)
    zeros = lambda n: jnp.zeros((t, n), F32)
    cosf = jnp.concatenate([ones(MLA_NOPE), cos, cos, ones(HEAD_PAD - MLA_QK)], axis=1)
    sin_a = jnp.concatenate([zeros(MLA_NOPE), -sin, zeros(half), zeros(HEAD_PAD - MLA_QK)], axis=1)
    sin_b = jnp.concatenate([zeros(MLA_NOPE), zeros(half), sin, zeros(HEAD_PAD - MLA_QK)], axis=1)
    return cosf, sin_a, sin_b


def _local_step(x, p, positions, tgt, norms, plan):
    mm = _matmul
    cosf, sin_a, sin_b = _rope_tables(positions)
    pad_head = lambda g: jnp.pad(g, ((0, 0), (0, HEAD_PAD - MLA_QK)))
    gqh, gkh = pad_head(norms['q_head_norm']), pad_head(norms['k_head_norm'])
    pb = p.astype(BF16)
    w = dict(plan.first_weights())
    dw, dn = {}, {}

    def ride(host, call):
        cargo = plan.cargo(host, dw)
        res, lands = call(cargo), None
        if cargo is not None:
            *res, lands = res
            res = res[0] if len(res) == 1 else tuple(res)
        w.update(plan.landed(host, lands))
        return res

    def ffn_fwd(h, tag):
        n, a, b, act = ride(tag + "_in_fwd", lambda cargo: _swiglu_fwd(
            h, norms[tag + '_norm'], w[tag + '_w_in'], tag + "_in_fwd", cargo))
        out = ride(tag + "_out_fwd", lambda cargo: mm(
            act, w[tag + '_w_out'], mode='nn', out_dtype=F32, name=tag + "_out_fwd", res=h, alpha=0.5, cargo=cargo))
        return out, (n, a, b, act)

    h1, ffn1_saved = ffn_fwd(x, 'ffn1')
    u = _rmsnorm_fwd(h1, norms['mix_norm'], "mix_norm_fwd")
    proj = mm(u, w['w_in'], mode='nt', out_dtype=BF16, name="proj_fwd")
    cqn, ckvn = _latent_fwd(proj, norms['q_latent_norm'], norms['kv_latent_norm'])
    qraw = mm(cqn, w['w_q_up'], mode='nt', out_dtype=F32, name="q_up_fwd")
    kvraw = mm(ckvn, w['w_kv_up'], mode='nn', out_dtype=F32, name="kv_up_fwd")
    qh, kh, kvb = _headprep_fwd(qraw, kvraw, proj, cosf, sin_a, sin_b, gqh, gkh)
    o_mla, lse = ride("mla_fwd", lambda cargo: _mla_fwd(qh, kh, kvb, cargo))
    o_sb, rtot, sb_first = _sb_fwd(proj)
    merged, bm, bs = _merge_fwd(o_mla, w['w_branch_mla'], o_sb, w['w_branch_sb'], proj)
    h2 = mm(merged, w['w_out'], mode='nn', out_dtype=F32, name="mix_out_fwd", res=h1)
    h3, ffn2_saved = ffn_fwd(h2, 'ffn2')
    n3, dh4, dzg, dpp, loss_lanes = _ple_loss(h3, norms['ple_norm'], w['w_ple_gate'], pb, w['w_ple_proj'], tgt)

    dw['w_ple_gate'] = mm(n3, dzg, mode='tn', out_dtype=BF16, name="ple_gate_dw")
    dw['w_ple_proj'] = mm(pb, dpp, mode='tn', out_dtype=BF16, name="ple_proj_dw")
    dh3, dhb3, dn['ple_norm'] = _matmul_norm_bwd(
        dzg, w['w_ple_gate'], h3, norms['ple_norm'], dh4, mode='nt', name="ple_gate_dx", out_scale=0.5)

    def ffn_bwd(h, dh, dhb, saved, tag, out_scale):
        n, a, b, act = saved
        dw[tag + '_w_out'] = mm(act, dhb, mode='tn', out_dtype=BF16, name=tag + "_out_dw", tm=1408)
        dab = _swiglu_bwd(dhb, w[tag + '_w_out'], a, b, tag + "_out_dx")
        dw[tag + '_w_in'] = ride(tag + "_in_dw", lambda cargo: mm(
            dab, n, mode='tn', out_dtype=BF16, name=tag + "_in_dw", tm=1408, cargo=cargo))
        dh_prev, dhb_prev, dn[tag + '_norm'] = ride(tag + "_in_dx", lambda cargo: _matmul_norm_bwd(
            dab, w[tag + '_w_in'], h, norms[tag + '_norm'], dh, mode='nn', name=tag + "_in_dx", out_scale=out_scale,
            cargo=cargo))
        return dh_prev, dhb_prev

    dh2, dhb2 = ffn_bwd(h2, dh3, dhb3, ffn2_saved, 'ffn2', 1.0)
    dw['w_out'] = mm(merged, dhb2, mode='tn', out_dtype=BF16, name="mix_out_dw")
    dbm, dbs, dgates = _merge_bwd(dhb2, w['w_out'], proj, bm, bs)
    dw['w_branch_mla'] = mm(o_mla, dbm, mode='tn', out_dtype=BF16, name="branch_mla_dw")
    dw['w_branch_sb'] = mm(o_sb, dbs, mode='tn', out_dtype=BF16, name="branch_sb_dw")
    do_mla = mm(dbm, w['w_branch_mla'], mode='nt', out_dtype=BF16, name="branch_mla_dx")
    do_sb = mm(dbs, w['w_branch_sb'], mode='nt', out_dtype=BF16, name="branch_sb_dx")
    dqh, dkh, dvp = ride("mla_bwd", lambda cargo: _mla_bwd(qh, kh, kvb, o_mla, do_mla, lse, cargo))
    dsq, dsk, dsv = _sb_bwd(proj, do_sb, rtot, sb_first)
    dqraw, dkvraw, dkr, dgq, dgk = _headprep_bwd(dqh, dkh, dvp, qraw, kvraw, proj, cosf, sin_a, sin_b, gqh, gkh)
    dn['q_head_norm'], dn['k_head_norm'] = dgq[:, :MLA_QK], dgk[:, :MLA_QK]
    dw['w_q_up'] = mm(dqraw, cqn, mode='tn', out_dtype=BF16, name="q_up_dw")
    dw['w_kv_up'] = mm(ckvn, dkvraw, mode='tn', out_dtype=BF16, name="kv_up_dw")
    dcqn = mm(dqraw, w['w_q_up'], mode='nn', out_dtype=F32, name="q_up_dx")
    dckvn = mm(dkvraw, w['w_kv_up'], mode='nt', out_dtype=F32, name="kv_up_dx")
    dlat, dn['q_latent_norm'], dn['kv_latent_norm'] = _latent_bwd(
        dcqn, dckvn, proj, dkr, norms['q_latent_norm'], norms['kv_latent_norm'])
    dproj = jnp.concatenate([dlat, dgates, dsq.astype(BF16), dsk.astype(BF16), dsv.astype(BF16)], axis=1)
    dw['w_in'] = ride("proj_dw", lambda cargo: mm(dproj, u, mode='tn', out_dtype=BF16, name="proj_dw", tm=1536, cargo=cargo))
    dh1, dhb1, dn['mix_norm'] = ride("proj_dx", lambda cargo: _matmul_norm_bwd(
        dproj, w['w_in'], h1, norms['mix_norm'], dh2, mode='nn', name="proj_dx", out_scale=0.5, cargo=cargo))
    dx, _ = ffn_bwd(x, dh1, dhb1, ffn1_saved, 'ffn1', 1.0)
    return dx, loss_lanes, dw, dn


MESH = pl.DeviceIdType.MESH
HBM_SPEC = pl.BlockSpec(memory_space=pl.ANY)


def _position():
    return lax.axis_index("x"), lax.axis_index("y"), lax.axis_index("c")


def _index(px, py, pc):
    return 4 * px + 2 * py + pc


def _all_gather(shards):
    n = len(shards)

    def body(*refs):
        ins, outs = refs[:n], refs[n:2 * n]
        send_sems, recv_sems, local_sems = refs[2 * n:]
        x, y, c = _position()
        me, sibling = (x, y, c), (x, y, 1 - c)
        chips = [(1 - x, y), (x, 1 - y), (1 - x, 1 - y)]

        def copy(a, k, block, to, own=False):
            dst = outs[a].at[_index(*block)]
            return pltpu.make_async_remote_copy(
                src_ref=ins[a] if own else dst, dst_ref=dst,
                send_sem=send_sems.at[a, k], recv_sem=recv_sems.at[a, k], device_id=to, device_id_type=MESH)

        mine = [pltpu.make_async_copy(ins[a], outs[a].at[_index(*me)], local_sems.at[a]) for a in range(n)]
        for cp in mine:
            cp.start()
        first = []
        for a in range(n):
            first.append(copy(a, 0, me, sibling, own=True))
            first += [copy(a, 1 + j, me, (*chip, c), own=True) for j, chip in enumerate(chips)]
        for cp in first:
            cp.start()
        passed = []
        for j, chip in enumerate(chips):
            for a in range(n):
                copy(a, 1 + j, (*chip, c), me).wait_recv()
                fwd = copy(a, 4 + j, (*chip, c), sibling)
                fwd.start()
                passed.append(fwd)
        for a in range(n):
            copy(a, 0, sibling, me).wait_recv()
            for j, chip in enumerate(chips):
                copy(a, 4 + j, (*chip, 1 - c), me).wait_recv()
        for cp in first + passed:
            cp.wait_send()
        for cp in mine:
            cp.wait()

    return pl.pallas_call(
        body, name="weights_all_gather",
        in_specs=[HBM_SPEC] * n, out_specs=[HBM_SPEC] * n,
        out_shape=[jax.ShapeDtypeStruct((N_DEV,) + s.shape, s.dtype) for s in shards],
        scratch_shapes=[pltpu.SemaphoreType.DMA((n, 7)), pltpu.SemaphoreType.DMA((n, 7)), pltpu.SemaphoreType.DMA((n,))],
    )(*shards)


def _exchange(parts):
    n = len(parts)
    masks = [(mx, my, mc) for mx in (0, 1) for my in (0, 1) for mc in (0, 1)][1:]

    def body(*refs):
        ins, outs = refs[:n], refs[n:2 * n]
        send_sems, recv_sems, local_sems = refs[2 * n:]
        x, y, c = _position()
        me = _index(x, y, c)

        def peer_of(mask):
            mx, my, mc = mask
            return (x + mx - 2 * x * mx, y + my - 2 * y * my, c + mc - 2 * c * mc)

        def copy(a, k):
            peer = peer_of(masks[k])
            return pltpu.make_async_remote_copy(
                src_ref=ins[a].at[_index(*peer)], dst_ref=outs[a].at[me],
                send_sem=send_sems.at[a, k], recv_sem=recv_sems.at[a, k], device_id=peer, device_id_type=MESH)

        def landed(a, k):
            peer = peer_of(masks[k])
            return pltpu.make_async_remote_copy(
                src_ref=ins[a].at[me], dst_ref=outs[a].at[_index(*peer)],
                send_sem=send_sems.at[a, k], recv_sem=recv_sems.at[a, k], device_id=peer, device_id_type=MESH)

        mine = [pltpu.make_async_copy(ins[a].at[me], outs[a].at[me], local_sems.at[a]) for a in range(n)]
        for cp in mine:
            cp.start()
        sent = [copy(a, k) for k in range(7) for a in range(n)]
        for cp in sent:
            cp.start()
        for k in range(7):
            for a in range(n):
                landed(a, k).wait_recv()
        for cp in sent:
            cp.wait_send()
        for cp in mine:
            cp.wait()

    return pl.pallas_call(
        body, name="grads_exchange",
        in_specs=[HBM_SPEC] * n, out_specs=[HBM_SPEC] * n,
        out_shape=[jax.ShapeDtypeStruct(s.shape, s.dtype) for s in parts],
        scratch_shapes=[pltpu.SemaphoreType.DMA((n, 7)), pltpu.SemaphoreType.DMA((n, 7)), pltpu.SemaphoreType.DMA((n,))],
    )(*parts)


PEER_MASKS = [(mx, my, mc) for mx in (0, 1) for my in (0, 1) for mc in (0, 1)][1:]


def _peer(mask):
    x, y, c = _position()
    mx, my, mc = mask
    return (x + mx - 2 * x * mx, y + my - 2 * y * my, c + mc - 2 * c * mc)


class _Cargo:
    def __init__(self, srcs, scatter):
        self.srcs, self.scatter, self.n = list(srcs), scatter, len(srcs)

    def specs(self):
        return [HBM_SPEC] * self.n

    def out_shape(self):
        return [jax.ShapeDtypeStruct(s.shape if self.scatter else (N_DEV,) + s.shape, s.dtype) for s in self.srcs]

    def scratch(self):
        per_copy = pltpu.SemaphoreType.DMA((self.n, len(PEER_MASKS)))
        return [per_copy, per_copy, pltpu.SemaphoreType.DMA((self.n,))]

    def _mine(self, src_refs, a, to):
        return src_refs[a].at[to] if self.scatter else src_refs[a]

    def _shard_copy(self, src_refs, land_refs, sems, a, k, block, to, own=False):
        dst = land_refs[a].at[_index(*block)]
        return pltpu.make_async_remote_copy(
            src_ref=src_refs[a] if own else dst, dst_ref=dst,
            send_sem=sems[0].at[a, k], recv_sem=sems[1].at[a, k], device_id=to, device_id_type=MESH)

    def _first_hops(self, src_refs, land_refs, sems):
        x, y, c = _position()
        chips = [(1 - x, y), (x, 1 - y), (1 - x, 1 - y)]
        hops = []
        for a in range(self.n):
            hops.append(self._shard_copy(src_refs, land_refs, sems, a, 0, (x, y, c), (x, y, 1 - c), own=True))
            hops += [self._shard_copy(src_refs, land_refs, sems, a, 1 + j, (x, y, c), (*chip, c), own=True)
                     for j, chip in enumerate(chips)]
        return hops, chips

    def start(self, src_refs, land_refs, sems):
        send, recv, local = sems
        me = _index(*_position())
        for a in range(self.n):
            pltpu.make_async_copy(self._mine(src_refs, a, me), land_refs[a].at[me], local.at[a]).start()
        if not self.scatter:
            for cp in self._first_hops(src_refs, land_refs, sems)[0]:
                cp.start()
            return
        for k, mask in enumerate(PEER_MASKS):
            peer = _peer(mask)
            for a in range(self.n):
                pltpu.make_async_remote_copy(
                    src_ref=self._mine(src_refs, a, _index(*peer)), dst_ref=land_refs[a].at[me],
                    send_sem=send.at[a, k], recv_sem=recv.at[a, k], device_id=peer, device_id_type=MESH).start()

    def _wait_gathered(self, src_refs, land_refs, sems):
        x, y, c = _position()
        me, sibling = (x, y, c), (x, y, 1 - c)
        first, chips = self._first_hops(src_refs, land_refs, sems)
        passed = []
        for j, chip in enumerate(chips):
            for a in range(self.n):
                self._shard_copy(src_refs, land_refs, sems, a, 1 + j, (*chip, c), me).wait_recv()
                passed.append(self._shard_copy(src_refs, land_refs, sems, a, 4 + j, (*chip, c), sibling))
                passed[-1].start()
        for a in range(self.n):
            self._shard_copy(src_refs, land_refs, sems, a, 0, sibling, me).wait_recv()
            for j, chip in enumerate(chips):
                self._shard_copy(src_refs, land_refs, sems, a, 4 + j, (*chip, 1 - c), me).wait_recv()
        for cp in first + passed:
            cp.wait_send()

    def wait(self, src_refs, land_refs, sems):
        send, recv, local = sems
        me = _index(*_position())
        if not self.scatter:
            self._wait_gathered(src_refs, land_refs, sems)
        for k, mask in enumerate(PEER_MASKS if self.scatter else []):
            peer = _peer(mask)
            there = _index(*peer)
            for a in range(self.n):
                pltpu.make_async_remote_copy(
                    src_ref=self._mine(src_refs, a, me), dst_ref=land_refs[a].at[there],
                    send_sem=send.at[a, k], recv_sem=recv.at[a, k], device_id=peer, device_id_type=MESH).wait_recv()
                pltpu.make_async_remote_copy(
                    src_ref=self._mine(src_refs, a, there), dst_ref=land_refs[a].at[me],
                    send_sem=send.at[a, k], recv_sem=recv.at[a, k], device_id=peer, device_id_type=MESH).wait_send()
        for a in range(self.n):
            pltpu.make_async_copy(self._mine(src_refs, a, me), land_refs[a].at[me], local.at[a]).wait()


def _with_cargo(cargo, refs, n_in, n_out, steps, counts, compute):
    if cargo is None:
        compute(refs)
        return
    n = cargo.n
    src_refs = refs[n_in:n_in + n]
    land_refs = refs[n_in + n + n_out:n_in + 2 * n + n_out]
    sems = refs[-3:]
    first = functools.reduce(jnp.logical_and, [s == 0 for s in steps])
    last = functools.reduce(jnp.logical_and, [s == c - 1 for s, c in zip(steps, counts)])

    @pl.when(first)
    def _():
        cargo.start(src_refs, land_refs, sems)

    compute(refs[:n_in] + refs[n_in + n:n_in + n + n_out] + refs[n_in + 2 * n + n_out:-3])

    @pl.when(last)
    def _():
        cargo.wait(src_refs, land_refs, sems)


def _adamw(parts, w, m, v, name):
    r, c = w.shape
    tr = next((t for t in (512, 384, 352, 256, 128) if r % t == 0), r) if r > 512 else r
    tc = c if tr < r or r <= 512 else 256
    assert r % tr == 0 and c % tc == 0
    bc1 = 1.0 - ADAM_B1 ** ADAM_STEP
    bc2 = 1.0 - ADAM_B2 ** ADAM_STEP

    def body(p_ref, w_ref, m_ref, v_ref, g_ref, d_ref, nm_ref, nv_ref):
        g = p_ref[0].astype(F32)
        for s in range(1, N_DEV):
            g = g + p_ref[s].astype(F32)
        nm = ADAM_B1 * m_ref[...] + (1.0 - ADAM_B1) * g
        nv = ADAM_B2 * v_ref[...] + (1.0 - ADAM_B2) * (g * g)
        g_ref[...] = g
        nm_ref[...] = nm
        nv_ref[...] = nv
        d_ref[...] = -ADAM_LR * ((nm / bc1) / (jnp.sqrt(nv / bc2) + ADAM_EPS) + ADAM_WD * w_ref[...])

    tile = pl.BlockSpec((tr, tc), lambda i, j: (i, j))
    out = jax.ShapeDtypeStruct((r, c), F32)
    return pl.pallas_call(
        body, name=name, grid=(r // tr, c // tc),
        in_specs=[pl.BlockSpec((N_DEV, tr, tc), lambda i, j: (0, i, j)), tile, tile, tile],
        out_specs=[tile] * 4, out_shape=[out] * 4,
        compiler_params=_params(("parallel", "parallel")),
    )(parts, w, m, v)


GATHER_FIRST = ['ffn1_w_in']
RIDES = {
    'ffn1_in_fwd': ('weights', ['ffn1_w_out', 'w_in']),
    'ffn1_out_fwd': ('weights', ['w_q_up', 'w_kv_up', 'w_branch_mla', 'w_branch_sb', 'w_out']),
    'mla_fwd': ('weights', ['ffn2_w_in', 'ffn2_w_out', 'w_ple_gate', 'w_ple_proj']),
    'mla_bwd': ('grads', ['w_ple_gate', 'w_ple_proj', 'ffn2_w_out', 'ffn2_w_in', 'w_out', 'w_branch_mla', 'w_branch_sb']),
    'proj_dw': ('grads', ['w_q_up', 'w_kv_up']),
    'proj_dx': ('grads', ['w_in']),
    'ffn1_in_dw': ('grads', ['ffn1_w_out']),
    'ffn1_in_dx': ('grads', ['ffn1_w_in']),
}


class _Plan:
    def __init__(self, shards):
        self.shards = shards
        self.received = {}

    def first_weights(self):
        gathered = _all_gather([self.shards[n] for n in GATHER_FIRST])
        return {n: _layout_weight(n, g) for n, g in zip(GATHER_FIRST, gathered)}

    def cargo(self, host, dw):
        if host not in RIDES:
            return None
        kind, names = RIDES[host]
        if kind == 'weights':
            return _Cargo([self.shards[n] for n in names], False)
        return _Cargo([_unlayout_grad(n, dw.pop(n)) for n in names], True)

    def landed(self, host, lands):
        if host not in RIDES:
            return {}
        kind, names = RIDES[host]
        if kind == 'weights':
            return {n: _layout_weight(n, land) for n, land in zip(names, lands)}
        self.received.update(zip(names, lands))
        return {}


def _pack_small(vecs):
    flat = jnp.concatenate([v.reshape(-1) for v in vecs])
    return jnp.pad(flat, (0, SMALL_ROWS * 128 - flat.shape[0])).reshape(SMALL_ROWS, 128)


def _unpack_small(packed, sizes):
    flat = packed.reshape(-1)
    out, at = [], 0
    for n in sizes:
        out.append(flat[at:at + n])
        at += n
    return out


def kernel(x, p, positions, ffn1_norm, ffn1_w_in, ffn1_w_out, mix_norm, w_in, q_latent_norm, w_q_up, kv_latent_norm, w_kv_up, q_head_norm, k_head_norm, w_branch_mla, w_branch_sb, w_out, ffn2_norm, ffn2_w_in, ffn2_w_out, ple_norm, w_ple_gate, w_ple_proj, loss_target, m_ffn1_norm, m_ffn1_w_in, m_ffn1_w_out, m_mix_norm, m_w_in, m_q_latent_norm, m_w_q_up, m_kv_latent_norm, m_w_kv_up, m_q_head_norm, m_k_head_norm, m_w_branch_mla, m_w_branch_sb, m_w_out, m_ffn2_norm, m_ffn2_w_in, m_ffn2_w_out, m_ple_norm, m_w_ple_gate, m_w_ple_proj, v_ffn1_norm, v_ffn1_w_in, v_ffn1_w_out, v_mix_norm, v_w_in, v_q_latent_norm, v_w_q_up, v_kv_latent_norm, v_w_kv_up, v_q_head_norm, v_k_head_norm, v_w_branch_mla, v_w_branch_sb, v_w_out, v_ffn2_norm, v_ffn2_w_in, v_ffn2_w_out, v_ple_norm, v_w_ple_gate, v_w_ple_proj):
    given = dict(locals())
    wts = {n: given[n] for n in WEIGHTS}
    mom = {n: given['m_' + n] for n in WEIGHTS}
    var = {n: given['v_' + n] for n in WEIGHTS}

    def local(a, n):
        return jnp.swapaxes(a[0], 0, 1) if n in TRANSPOSED else a[0]

    plan = _Plan({n: local(wts[n], n).astype(BF16) for n in MATS})
    norms = {n: wts[n] for n in NORMS}
    dx, loss_lanes, dw, dn = _local_step(x[0], p[0, 0], positions[0], loss_target[0], norms, plan)
    assert not dw

    out = {}
    for n in MATS:
        res = _adamw(plan.received[n], local(wts[n], n), local(mom[n], n), local(var[n], n), "adamw_" + n)
        out[n] = [local(r[None], n)[None] for r in res]
    small = _pack_small([dn[n] for n in NORMS] + [0.5 / D_MODEL * jnp.sum(loss_lanes)[None]])
    small_parts = _exchange([jnp.broadcast_to(small[None], (N_DEV, SMALL_ROWS, 128))])[0]
    sizes = [wts[n].shape[1] for n in NORMS]
    pack = lambda d: _pack_small([d[n] for n in NORMS])
    small_res = _adamw(small_parts, pack(wts), pack(mom), pack(var), "adamw_norms")
    loss = small_res[0].reshape(-1)[sum(sizes)]
    for i, res in enumerate(small_res):
        for n, vec in zip(NORMS, _unpack_small(res, sizes)):
            out.setdefault(n, [None] * 4)[i] = vec[None]

    return (loss, dx[None], *[out[n][0] for n in WEIGHTS], *[out[n][1] for n in WEIGHTS],
            *[out[n][2] for n in WEIGHTS], *[out[n][3] for n in WEIGHTS])
```

```python
import functools
import math

import jax
import jax.numpy as jnp
from jax import lax
from jax.experimental import pallas as pl
from jax.experimental.pallas import tpu as pltpu

F32 = jnp.float32
BF16 = jnp.bfloat16

N_DEV = 8
D_MODEL = 1024
D_FF = 2816
PLE_DIM = 256
NORM_EPS = 1e-6
N_HEADS = 8
HEAD_PAD = 128
MLA_NOPE = 64
MLA_ROPE = 32
MLA_QK = 96
Q_LORA = 384
KV_LORA = 256
SB_DIM = 64
SB_WIDTH = 512
ROPE_BASE = 10000.0
IN_COLS = 4256

PROJ_W = 4608
P_CQ, P_CKV, P_KR, P_GM, P_GS, P_SBQ, P_SBK, P_SBV = 0, 384, 640, 1024, 2048, 3072, 3584, 4096

ADAM_LR, ADAM_B1, ADAM_B2, ADAM_EPS, ADAM_WD, ADAM_STEP = 0.001, 0.9, 0.999, 1e-08, 0.01, 10

VMEM_LIMIT = 52 * 1024 * 1024
MATMUL_VMEM = 40 * 1024 * 1024

WEIGHTS = ['ffn1_norm', 'ffn1_w_in', 'ffn1_w_out', 'mix_norm', 'w_in', 'q_latent_norm', 'w_q_up',
           'kv_latent_norm', 'w_kv_up', 'q_head_norm', 'k_head_norm', 'w_branch_mla', 'w_branch_sb',
           'w_out', 'ffn2_norm', 'ffn2_w_in', 'ffn2_w_out', 'ple_norm', 'w_ple_gate', 'w_ple_proj']
NORMS = ['ffn1_norm', 'mix_norm', 'q_latent_norm', 'kv_latent_norm', 'q_head_norm', 'k_head_norm',
         'ffn2_norm', 'ple_norm']
MATS = [n for n in WEIGHTS if n not in NORMS]
SMALL_ROWS = 48

NT_DIMS = (((1,), (1,)), ((), ()))
NN_DIMS = (((1,), (0,)), ((), ()))
TN_DIMS = (((0,), (0,)), ((), ()))


def _params(sem=None, vmem=VMEM_LIMIT):
    return pltpu.CompilerParams(dimension_semantics=sem, vmem_limit_bytes=vmem)


def _pick(n, cap):
    if n <= cap:
        return n
    best = None
    for t in range(128, cap + 1, 128):
        if n % t == 0:
            best = t
    assert best is not None, (n, cap)
    return best


def _dot(a, b, dims):
    return lax.dot_general(a, b, dims, preferred_element_type=F32)


def _matmul(a, b, *, mode, out_dtype, name, tm=None, tn=None, tk=None, res=None, alpha=1.0, cargo=None):
    if mode == 'nn':
        (m, k), (k2, n) = a.shape, b.shape
    elif mode == 'nt':
        (m, k), (n, k2) = a.shape, b.shape
    else:
        (k, m), (k2, n) = a.shape, b.shape
    assert k == k2, (name, a.shape, b.shape)
    has_res = res is not None
    tn = tn or _pick(n, 512)

    def vmem(tm_, tk_):
        io = 2 * 2 * (tm_ * tk_ + tk_ * tn) + 2 * tm_ * tn * (jnp.dtype(out_dtype).itemsize + 4 * has_res)
        return io + (4 * tm_ * tn if tk_ < k else 0)

    tries = [(tm_, tk_) for tk_ in ([tk] if tk else [k, _pick(k, 2048)])
             for tm_ in ([tm] if tm else [_pick(m, 2048), _pick(m, 1024), _pick(m, 512)])]
    tm, tk = next((c for c in tries if vmem(*c) <= MATMUL_VMEM), tries[-1])
    assert m % tm == 0 and n % tn == 0 and k % tk == 0, (name, m, n, k, tm, tn, tk)
    nk = k // tk
    dims = {'nn': NN_DIMS, 'nt': NT_DIMS, 'tn': TN_DIMS}[mode]

    def epilogue(acc, r_ref, o_ref):
        if alpha != 1.0:
            acc = acc * alpha
        if has_res:
            acc = r_ref[...] + acc
        o_ref[...] = acc.astype(out_dtype)

    grid = (m // tm, n // tn, nk)

    def body(*refs):
        steps = [pl.program_id(d) for d in range(3)]

        def compute(own):
            a_ref, b_ref = own[0], own[1]
            r_ref = own[2] if has_res else None
            o_ref = own[2 + has_res]
            if nk == 1:
                epilogue(_dot(a_ref[...], b_ref[...], dims), r_ref, o_ref)
                return
            acc_ref = own[-1]

            @pl.when(steps[2] == 0)
            def _():
                acc_ref[...] = jnp.zeros_like(acc_ref)

            acc_ref[...] += _dot(a_ref[...], b_ref[...], dims)

            @pl.when(steps[2] == nk - 1)
            def _():
                epilogue(acc_ref[...], r_ref, o_ref)

        _with_cargo(cargo, refs, 2 + has_res, 1, steps, grid, compute)

    if mode == 'tn':
        a_spec = pl.BlockSpec((tk, tm), lambda i, j, kk: (kk, i))
    else:
        a_spec = pl.BlockSpec((tm, tk), lambda i, j, kk: (i, kk))
    if mode == 'nt':
        b_spec = pl.BlockSpec((tn, tk), lambda i, j, kk: (j, kk))
    else:
        b_spec = pl.BlockSpec((tk, tn), lambda i, j, kk: (kk, j))
    o_spec = pl.BlockSpec((tm, tn), lambda i, j, kk: (i, j))
    in_specs = [a_spec, b_spec] + ([o_spec] if has_res else [])
    args = (a, b) + ((res,) if has_res else ())
    out_shape = jax.ShapeDtypeStruct((m, n), out_dtype)
    scratch = [pltpu.VMEM((tm, tn), F32)] if nk > 1 else []
    if cargo is None:
        return pl.pallas_call(
            body, name=name, grid=grid, in_specs=in_specs, out_specs=o_spec, out_shape=out_shape,
            scratch_shapes=scratch, compiler_params=_params(("parallel", "parallel", "arbitrary")),
        )(*args)
    outs = pl.pallas_call(
        body, name=name, grid=grid, in_specs=in_specs + cargo.specs(), out_specs=[o_spec] + cargo.specs(),
        out_shape=[out_shape] + cargo.out_shape(), scratch_shapes=scratch + cargo.scratch(),
        compiler_params=_params(("arbitrary", "arbitrary", "arbitrary")),
    )(*args, *cargo.srcs)
    return outs[0], list(outs[1:])


def _row_tile(t, cap=512):
    return min(t, cap)


def _rms(x, width):
    return lax.rsqrt(jnp.sum(x * x, axis=-1, keepdims=True) * (1.0 / width) + NORM_EPS)


def _rmsnorm_fwd(x, g, name):
    t, d = x.shape
    tr = _row_tile(t)

    def body(x_ref, g_ref, o_ref):
        xv = x_ref[...]
        o_ref[...] = ((xv * _rms(xv, d)) * g_ref[...]).astype(BF16)

    return pl.pallas_call(
        body, name=name, grid=(t // tr,),
        in_specs=[pl.BlockSpec((tr, d), lambda i: (i, 0)), pl.BlockSpec((1, d), lambda i: (0, 0))],
        out_specs=pl.BlockSpec((tr, d), lambda i: (i, 0)),
        out_shape=jax.ShapeDtypeStruct((t, d), BF16),
        compiler_params=_params(("parallel",)),
    )(x, g)


def _matmul_norm_bwd(a, b, x, g, dh_in, *, mode, name, out_scale, cargo=None):
    m, k = a.shape
    d = x.shape[1]
    tn = _pick(d, 512)

    def vmem(tm_):
        return 2 * 2 * (tm_ * k + k * tn) + tm_ * d * (4 + 2 * (4 + 4) + 2 * (4 + 2))

    tm = next((c for c in (_pick(m, 1024), _pick(m, 512), _pick(m, 256)) if vmem(c) <= MATMUL_VMEM), _pick(m, 256))
    grid = (m // tm, d // tn)
    dims = {'nn': NN_DIMS, 'nt': NT_DIMS}[mode]

    def body(*refs):
        steps = [pl.program_id(0), pl.program_id(1)]

        def compute(own):
            a_ref, b_ref, x_ref, g_ref, dhin_ref, dh_ref, dhb_ref, dg_ref, dn_ref = own
            for jj in range(grid[1]):
                @pl.when(steps[1] == jj)
                def _(jj=jj):
                    dn_ref[:, jj * tn:(jj + 1) * tn] = _dot(a_ref[...], b_ref[...], dims)

            @pl.when(steps[1] == grid[1] - 1)
            def _():
                xv = x_ref[...]
                dnv = dn_ref[...]
                r = _rms(xv, d)
                y = xv * r
                dy = dnv * g_ref[...]
                dh = dhin_ref[...] + r * (dy - y * (jnp.sum(dy * y, axis=-1, keepdims=True) * (1.0 / d)))
                dh_ref[...] = dh
                dhb_ref[...] = (dh * out_scale).astype(BF16)
                part = jnp.sum(dnv * y, axis=0, keepdims=True)

                @pl.when(steps[0] == 0)
                def _():
                    dg_ref[...] = part

                @pl.when(steps[0] > 0)
                def _():
                    dg_ref[...] += part

        _with_cargo(cargo, refs, 5, 3, steps, grid, compute)

    b_spec = pl.BlockSpec((k, tn), lambda i, j: (0, j)) if mode == 'nn' else pl.BlockSpec((tn, k), lambda i, j: (j, 0))
    row = pl.BlockSpec((tm, d), lambda i, j: (i, 0))
    vec = pl.BlockSpec((1, d), lambda i, j: (0, 0))
    extra = cargo.specs() if cargo else []
    outs = pl.pallas_call(
        body, name=name, grid=grid,
        in_specs=[pl.BlockSpec((tm, k), lambda i, j: (i, 0)), b_spec, row, vec, row] + extra,
        out_specs=[row, row, vec] + extra,
        out_shape=[jax.ShapeDtypeStruct((m, d), F32), jax.ShapeDtypeStruct((m, d), BF16),
                   jax.ShapeDtypeStruct((1, d), F32)] + (cargo.out_shape() if cargo else []),
        scratch_shapes=[pltpu.VMEM((tm, d), F32)] + (cargo.scratch() if cargo else []),
        compiler_params=_params(("arbitrary", "arbitrary")),
    )(a, b, x, g, dh_in, *(cargo.srcs if cargo else []))
    return (outs[0], outs[1], outs[2], list(outs[3:])) if cargo else outs


def _sigmoid(x):
    return 1.0 / (1.0 + jnp.exp(-x))


SWIGLU_CHUNK = 256
SWIGLU_COLS = 1408


def _chunks(width):
    return [slice(lo, min(lo + SWIGLU_CHUNK, width)) for lo in range(0, width, SWIGLU_CHUNK)]


def _swiglu_fwd(h, g, wt_in, name, cargo=None):
    t = h.shape[0]
    tm = _pick(t, 1024)
    grid = (t // tm, D_FF // SWIGLU_COLS)

    def body(*refs):
        steps = [pl.program_id(0), pl.program_id(1)]

        def compute(own):
            h_ref, g_ref, wa_ref, wb_ref, n_ref, a_ref, b_ref, act_ref, n_sc = own

            @pl.when(steps[1] == 0)
            def _():
                xv = h_ref[...]
                n_sc[...] = ((xv * _rms(xv, D_MODEL)) * g_ref[...]).astype(BF16)
                n_ref[...] = n_sc[...]

            nv = n_sc[...]
            for cols in _chunks(SWIGLU_COLS):
                a = _dot(nv, wa_ref[cols, :], NT_DIMS)
                b = _dot(nv, wb_ref[cols, :], NT_DIMS)
                a_ref[:, cols] = a.astype(BF16)
                b_ref[:, cols] = b.astype(BF16)
                act_ref[:, cols] = (a * _sigmoid(a) * b).astype(BF16)

        _with_cargo(cargo, refs, 4, 4, steps, grid, compute)

    half = D_FF // SWIGLU_COLS
    row = pl.BlockSpec((tm, D_MODEL), lambda i, j: (i, 0))
    tile = pl.BlockSpec((tm, SWIGLU_COLS), lambda i, j: (i, j))
    out = jax.ShapeDtypeStruct((t, D_FF), BF16)
    extra = cargo.specs() if cargo else []
    outs = pl.pallas_call(
        body, name=name, grid=grid,
        in_specs=[row, pl.BlockSpec((1, D_MODEL), lambda i, j: (0, 0)),
                  pl.BlockSpec((SWIGLU_COLS, D_MODEL), lambda i, j: (j, 0)),
                  pl.BlockSpec((SWIGLU_COLS, D_MODEL), lambda i, j: (half + j, 0))] + extra,
        out_specs=[row, tile, tile, tile] + extra,
        out_shape=[jax.ShapeDtypeStruct((t, D_MODEL), BF16), out, out, out] + (cargo.out_shape() if cargo else []),
        scratch_shapes=[pltpu.VMEM((tm, D_MODEL), BF16)] + (cargo.scratch() if cargo else []),
        compiler_params=_params(("arbitrary", "arbitrary")),
    )(h, g, wt_in, wt_in, *(cargo.srcs if cargo else []))
    return (outs[0], outs[1], outs[2], outs[3], list(outs[4:])) if cargo else outs


def _swiglu_bwd(dh, w_out, a, b, name):
    t = a.shape[0]
    tr = _row_tile(t, 512)

    def body(d_ref, w_ref, a_ref, b_ref, o_ref):
        dhv = d_ref[...]
        for cols in _chunks(D_FF):
            dv = _dot(dhv, w_ref[cols, :], NT_DIMS)
            av = a_ref[:, cols].astype(F32)
            s = _sigmoid(av)
            o_ref[:, cols] = (dv * b_ref[:, cols].astype(F32) * s * (1.0 + av * (1.0 - s))).astype(BF16)
            o_ref[:, slice(D_FF + cols.start, D_FF + cols.stop)] = (dv * av * s).astype(BF16)

    row = pl.BlockSpec((tr, D_FF), lambda i: (i, 0))
    return pl.pallas_call(
        body, name=name, grid=(t // tr,),
        in_specs=[pl.BlockSpec((tr, D_MODEL), lambda i: (i, 0)), pl.BlockSpec((D_FF, D_MODEL), lambda i: (0, 0)), row, row],
        out_specs=pl.BlockSpec((tr, 2 * D_FF), lambda i: (i, 0)),
        out_shape=jax.ShapeDtypeStruct((t, 2 * D_FF), BF16),
        compiler_params=_params(("parallel",)),
    )(dh, w_out, a, b)


def _latent_fwd(proj, gq, gkv):
    t = proj.shape[0]
    tr = _row_tile(t)

    def body(p_ref, gq_ref, gkv_ref, cq_ref, ckv_ref):
        cq = p_ref[:, P_CQ:P_CQ + Q_LORA].astype(F32)
        ckv = p_ref[:, P_CKV:P_CKV + KV_LORA].astype(F32)
        cq_ref[...] = ((cq * _rms(cq, Q_LORA)) * gq_ref[...]).astype(BF16)
        ckv_ref[...] = ((ckv * _rms(ckv, KV_LORA)) * gkv_ref[...]).astype(BF16)

    return pl.pallas_call(
        body, name="latent_fwd", grid=(t // tr,),
        in_specs=[pl.BlockSpec((tr, 1024), lambda i: (i, 0)), pl.BlockSpec((1, Q_LORA), lambda i: (0, 0)),
                  pl.BlockSpec((1, KV_LORA), lambda i: (0, 0))],
        out_specs=[pl.BlockSpec((tr, Q_LORA), lambda i: (i, 0)), pl.BlockSpec((tr, KV_LORA), lambda i: (i, 0))],
        out_shape=[jax.ShapeDtypeStruct((t, Q_LORA), BF16), jax.ShapeDtypeStruct((t, KV_LORA), BF16)],
        compiler_params=_params(("parallel",)),
    )(proj, gq, gkv)


def _latent_bwd(dcqn, dckvn, proj, dkr, gq, gkv):
    t = proj.shape[0]
    tr = _row_tile(t, 256)

    def norm_bwd(dn, x, g, width):
        r = _rms(x, width)
        y = x * r
        dy = dn * g
        dx = r * (dy - y * (jnp.sum(dy * y, axis=-1, keepdims=True) * (1.0 / width)))
        return dx, jnp.sum(dn * y, axis=0, keepdims=True)

    def body(dcq_ref, dckv_ref, p_ref, dkr_ref, gq_ref, gkv_ref, o_ref, dgq_ref, dgkv_ref):
        i = pl.program_id(0)
        dcq, pq = norm_bwd(dcq_ref[...], p_ref[:, P_CQ:P_CQ + Q_LORA].astype(F32), gq_ref[...], Q_LORA)
        dckv, pkv = norm_bwd(dckv_ref[...], p_ref[:, P_CKV:P_CKV + KV_LORA].astype(F32), gkv_ref[...], KV_LORA)
        o_ref[:, P_CQ:P_CQ + Q_LORA] = dcq.astype(BF16)
        o_ref[:, P_CKV:P_CKV + KV_LORA] = dckv.astype(BF16)
        o_ref[:, P_KR:P_KR + 128] = dkr_ref[...].astype(BF16)
        o_ref[:, P_KR + 128:1024] = jnp.zeros((tr, 1024 - P_KR - 128), BF16)

        @pl.when(i == 0)
        def _():
            dgq_ref[...] = pq
            dgkv_ref[...] = pkv

        @pl.when(i > 0)
        def _():
            dgq_ref[...] += pq
            dgkv_ref[...] += pkv

    def row(w):
        return pl.BlockSpec((tr, w), lambda i: (i, 0))

    def vec(w):
        return pl.BlockSpec((1, w), lambda i: (0, 0))

    return pl.pallas_call(
        body, name="latent_bwd", grid=(t // tr,),
        in_specs=[row(Q_LORA), row(KV_LORA), row(1024), row(128), vec(Q_LORA), vec(KV_LORA)],
        out_specs=[row(1024), vec(Q_LORA), vec(KV_LORA)],
        out_shape=[jax.ShapeDtypeStruct((t, 1024), BF16), jax.ShapeDtypeStruct((1, Q_LORA), F32),
                   jax.ShapeDtypeStruct((1, KV_LORA), F32)],
        compiler_params=_params(("arbitrary",)),
    )(dcqn, dckvn, proj, dkr, gq, gkv)


def _rope(y, cosf, sin_a, sin_b):
    return y * cosf + pltpu.roll(y, 112, 1) * sin_a + pltpu.roll(y, 16, 1) * sin_b


def _rope_t(d, cosf, sin_a, sin_b):
    return d * cosf + pltpu.roll(d * sin_a, 16, 1) + pltpu.roll(d * sin_b, 112, 1)


def _headprep_fwd(qraw, kvraw, proj, cosf, sin_a, sin_b, gqh, gkh):
    t = qraw.shape[0]
    tr = _row_tile(t, 256)

    def body(q_ref, kv_ref, kr_ref, c_ref, sa_ref, sb_ref, gq_ref, gk_ref, qh_ref, kh_ref, kvb_ref):
        cv, sa, sb = c_ref[...], sa_ref[...], sb_ref[...]
        kr = kr_ref[...].astype(F32)
        lane = lax.broadcasted_iota(jnp.int32, (tr, HEAD_PAD), 1)
        heads = [slice(h * HEAD_PAD, (h + 1) * HEAD_PAD) for h in range(N_HEADS)]
        xqs = [q_ref[:, cols] for cols in heads]
        kvs = [kv_ref[:, cols] for cols in heads]
        xks = [jnp.where(lane < MLA_NOPE, kvh, kr) for kvh in kvs]
        rqs = [_rms(x, MLA_QK) for x in xqs]
        rks = [_rms(x, MLA_QK) for x in xks]
        gq = gq_ref[...] * MLA_Q_SCALE
        yqs = [(x * r) * gq for x, r in zip(xqs, rqs)]
        yks = [(x * r) * gk_ref[...] for x, r in zip(xks, rks)]
        for cols, yq, yk, kvh in zip(heads, yqs, yks, kvs):
            qh_ref[:, cols] = _rope(yq, cv, sa, sb).astype(BF16)
            kh_ref[:, cols] = _rope(yk, cv, sa, sb).astype(BF16)
            kvb_ref[:, cols] = jnp.where(lane < MLA_NOPE, 1.0, kvh).astype(BF16)

    wide = pl.BlockSpec((tr, 1024), lambda i: (i, 0))
    lanes = pl.BlockSpec((tr, HEAD_PAD), lambda i: (i, 0))
    vec = pl.BlockSpec((1, HEAD_PAD), lambda i: (0, 0))
    return pl.pallas_call(
        body, name="headprep_fwd", grid=(t // tr,),
        in_specs=[wide, wide, pl.BlockSpec((tr, HEAD_PAD), lambda i: (i, P_KR // HEAD_PAD)), lanes, lanes, lanes, vec, vec],
        out_specs=[wide, wide, wide],
        out_shape=[jax.ShapeDtypeStruct((t, 1024), BF16)] * 3,
        compiler_params=_params(("parallel",)),
    )(qraw, kvraw, proj, cosf, sin_a, sin_b, gqh, gkh)


def _headprep_bwd(dqh, dkh, dvp, qraw, kvraw, proj, cosf, sin_a, sin_b, gqh, gkh):
    t = qraw.shape[0]
    tr = _row_tile(t, 256)

    def body(dq_ref, dk_ref, dv_ref, q_ref, kv_ref, kr_ref, c_ref, sa_ref, sb_ref, gq_ref, gk_ref,
             dqr_ref, dkvr_ref, dkr_ref, dgq_ref, dgk_ref):
        i = pl.program_id(0)
        cv, sa, sb = c_ref[...], sa_ref[...], sb_ref[...]
        kr = kr_ref[...].astype(F32)
        lane = lax.broadcasted_iota(jnp.int32, (tr, HEAD_PAD), 1)
        heads = [slice(h * HEAD_PAD, (h + 1) * HEAD_PAD) for h in range(N_HEADS)]
        xs = [q_ref[:, cols] for cols in heads] + [jnp.where(lane < MLA_NOPE, kv_ref[:, cols], kr) for cols in heads]
        gs = [gq_ref[...]] * N_HEADS + [gk_ref[...]] * N_HEADS
        dns = [_rope_t(ref[:, cols], cv, sa, sb) for ref in (dq_ref, dk_ref) for cols in heads]
        rs = [_rms(x, MLA_QK) for x in xs]
        ys = [x * r for x, r in zip(xs, rs)]
        dys = [dn * g for dn, g in zip(dns, gs)]
        means = [jnp.sum(dy * y, axis=-1, keepdims=True) * (1.0 / MLA_QK) for dy, y in zip(dys, ys)]
        dxs = [r * (dy - y * m) for r, dy, y, m in zip(rs, dys, ys, means)]
        parts = [jnp.sum(dn * y, axis=0, keepdims=True) for dn, y in zip(dns, ys)]
        dkr = jnp.zeros((tr, HEAD_PAD), F32)
        pq = jnp.zeros((1, HEAD_PAD), F32)
        pk = jnp.zeros((1, HEAD_PAD), F32)
        for h, cols in enumerate(heads):
            dqr_ref[:, cols] = dxs[h].astype(BF16)
            dxk = dxs[N_HEADS + h]
            dkvr_ref[:, cols] = jnp.where(lane < MLA_NOPE, dxk, dv_ref[:, cols]).astype(BF16)
            dkr = dkr + jnp.where(lane < MLA_NOPE, 0.0, dxk)
            pq = pq + parts[h]
            pk = pk + parts[N_HEADS + h]
        dkr_ref[...] = dkr

        @pl.when(i == 0)
        def _():
            dgq_ref[...] = pq
            dgk_ref[...] = pk

        @pl.when(i > 0)
        def _():
            dgq_ref[...] += pq
            dgk_ref[...] += pk

    wide = pl.BlockSpec((tr, 1024), lambda i: (i, 0))
    lanes = pl.BlockSpec((tr, HEAD_PAD), lambda i: (i, 0))
    vec = pl.BlockSpec((1, HEAD_PAD), lambda i: (0, 0))
    return pl.pallas_call(
        body, name="headprep_bwd", grid=(t // tr,),
        in_specs=[wide, wide, wide, wide, wide, pl.BlockSpec((tr, HEAD_PAD), lambda i: (i, P_KR // HEAD_PAD)),
                  lanes, lanes, lanes, vec, vec],
        out_specs=[wide, wide, lanes, vec, vec],
        out_shape=[jax.ShapeDtypeStruct((t, 1024), BF16), jax.ShapeDtypeStruct((t, 1024), BF16),
                   jax.ShapeDtypeStruct((t, HEAD_PAD), F32), jax.ShapeDtypeStruct((1, HEAD_PAD), F32),
                   jax.ShapeDtypeStruct((1, HEAD_PAD), F32)],
        compiler_params=_params(("arbitrary",)),
    )(dqh, dkh, dvp, qraw, kvraw, proj, cosf, sin_a, sin_b, gqh, gkh)


def _merge_fwd(o_mla, w_mla, o_sb, w_sb, proj):
    t = proj.shape[0]
    tr = _row_tile(t, 512)

    def body(om_ref, wm_ref, os_ref, ws_ref, gm_ref, gs_ref, o_ref, bm_ref, bs_ref):
        omv, osv = om_ref[...], os_ref[...]
        for cols in _chunks(D_MODEL):
            bm = _dot(omv, wm_ref[:, cols], NN_DIMS)
            bs = _dot(osv, ws_ref[:, cols], NN_DIMS)
            bm_ref[:, cols] = bm
            bs_ref[:, cols] = bs
            gm = _sigmoid(gm_ref[:, cols].astype(F32))
            gs = _sigmoid(gs_ref[:, cols].astype(F32))
            o_ref[:, cols] = (gm * bm + gs * bs).astype(BF16)

    row = pl.BlockSpec((tr, 1024), lambda i: (i, 0))
    f32 = jax.ShapeDtypeStruct((t, 1024), F32)
    return pl.pallas_call(
        body, name="merge_fwd", grid=(t // tr,),
        in_specs=[row, pl.BlockSpec(w_mla.shape, lambda i: (0, 0)),
                  pl.BlockSpec((tr, SB_WIDTH), lambda i: (i, 0)), pl.BlockSpec(w_sb.shape, lambda i: (0, 0)),
                  pl.BlockSpec((tr, 1024), lambda i: (i, P_GM // 1024)), pl.BlockSpec((tr, 1024), lambda i: (i, P_GS // 1024))],
        out_specs=[row, row, row], out_shape=[jax.ShapeDtypeStruct((t, 1024), BF16), f32, f32],
        compiler_params=_params(("parallel",)),
    )(o_mla, w_mla, o_sb, w_sb, proj, proj)


def _merge_bwd(dh, w_out, proj, bm, bs):
    t = proj.shape[0]
    tr = _row_tile(t, 512)

    def body(d_ref, w_ref, gm_ref, gs_ref, bm_ref, bs_ref, dbm_ref, dbs_ref, dg_ref):
        dhv = d_ref[...]
        for cols in _chunks(D_MODEL):
            dm = _dot(dhv, w_ref[cols, :], NT_DIMS)
            gm = _sigmoid(gm_ref[:, cols].astype(F32))
            gs = _sigmoid(gs_ref[:, cols].astype(F32))
            dbm_ref[:, cols] = (dm * gm).astype(BF16)
            dbs_ref[:, cols] = (dm * gs).astype(BF16)
            dg_ref[:, cols] = (dm * bm_ref[:, cols] * gm * (1.0 - gm)).astype(BF16)
            dg_ref[:, slice(D_MODEL + cols.start, D_MODEL + cols.stop)] = (dm * bs_ref[:, cols] * gs * (1.0 - gs)).astype(BF16)

    row = pl.BlockSpec((tr, 1024), lambda i: (i, 0))
    return pl.pallas_call(
        body, name="mix_out_dx", grid=(t // tr,),
        in_specs=[row, pl.BlockSpec((D_MODEL, D_MODEL), lambda i: (0, 0)),
                  pl.BlockSpec((tr, 1024), lambda i: (i, P_GM // 1024)),
                  pl.BlockSpec((tr, 1024), lambda i: (i, P_GS // 1024)), row, row],
        out_specs=[row, row, pl.BlockSpec((tr, 2048), lambda i: (i, 0))],
        out_shape=[jax.ShapeDtypeStruct((t, 1024), BF16), jax.ShapeDtypeStruct((t, 1024), BF16),
                   jax.ShapeDtypeStruct((t, 2048), BF16)],
        compiler_params=_params(("parallel",)),
    )(dh, w_out, proj, proj, bm, bs)


def _ple_loss(h3, g, w_gate, pb, w_proj, tgt):
    t = h3.shape[0]
    tr = _row_tile(t, 512)

    def body(h_ref, g_ref, wg_ref, p_ref, wp_ref, t_ref, n_ref, dh_ref, dz_ref, dp_ref, l_ref):
        i = pl.program_id(0)
        xv = h_ref[...]
        nv = ((xv * _rms(xv, D_MODEL)) * g_ref[...]).astype(BF16)
        n_ref[...] = nv
        pv = p_ref[...]
        part = jnp.zeros((1, 128), F32)
        for cols in _chunks(D_MODEL):
            pg = _sigmoid(_dot(nv, wg_ref[:, cols], NN_DIMS))
            ppv = _dot(pv, wp_ref[:, cols], NN_DIMS)
            diff = (h_ref[:, cols] + pg * ppv) - t_ref[:, cols]
            dh = diff * (1.0 / D_MODEL)
            dh_ref[:, cols] = dh
            dp_ref[:, cols] = (dh * pg).astype(BF16)
            dz_ref[:, cols] = (dh * ppv * pg * (1.0 - pg)).astype(BF16)
            sq = jnp.sum(diff * diff, axis=0, keepdims=True)
            for c in range(sq.shape[1] // 128):
                part = part + sq[:, c * 128:(c + 1) * 128]

        @pl.when(i == 0)
        def _():
            l_ref[...] = part

        @pl.when(i > 0)
        def _():
            l_ref[...] += part

    row = pl.BlockSpec((tr, 1024), lambda i: (i, 0))
    return pl.pallas_call(
        body, name="ple_loss", grid=(t // tr,),
        in_specs=[row, pl.BlockSpec((1, D_MODEL), lambda i: (0, 0)), pl.BlockSpec((D_MODEL, D_MODEL), lambda i: (0, 0)),
                  pl.BlockSpec((tr, PLE_DIM), lambda i: (i, 0)), pl.BlockSpec((PLE_DIM, D_MODEL), lambda i: (0, 0)), row],
        out_specs=[row, row, row, row, pl.BlockSpec((1, 128), lambda i: (0, 0))],
        out_shape=[jax.ShapeDtypeStruct((t, 1024), BF16), jax.ShapeDtypeStruct((t, 1024), F32),
                   jax.ShapeDtypeStruct((t, 1024), BF16), jax.ShapeDtypeStruct((t, 1024), BF16),
                   jax.ShapeDtypeStruct((1, 128), F32)],
        compiler_params=_params(("arbitrary",)),
    )(h3, g, w_gate, pb, w_proj, tgt)


ATT_BLOCK = 256
MLA_Q_SCALE = math.log2(math.e) / math.sqrt(MLA_QK)
MLA_Q_BLOCK = 512
MLA_FWD_COLS = 4
MLA_BWD_COLS = 4
SB_FWD_COLS = 4
SB_BWD_COLS = 2
SB_BLOCK = 256


def _split_bf16(x):
    hi = x.astype(BF16)
    return hi, (x - hi.astype(F32)).astype(BF16)


def _tri(kind, n):
    r = lax.broadcasted_iota(jnp.int32, (n, n), 0)
    c = lax.broadcasted_iota(jnp.int32, (n, n), 1)
    cond = {'gt': r > c, 'le': r <= c, 'lt': r < c}[kind]
    return jnp.where(cond, 1.0, 0.0).astype(BF16)


def _causal(strict, n=ATT_BLOCK):
    r = lax.broadcasted_iota(jnp.int32, (n, n), 0)
    c = lax.broadcasted_iota(jnp.int32, (n, n), 1)
    return (c < r) if strict else (c <= r)


def _below_diagonal(rows):
    r = lax.broadcasted_iota(jnp.int32, (rows, ATT_BLOCK), 0)
    c = lax.broadcasted_iota(jnp.int32, (rows, ATT_BLOCK), 1)
    return c <= r


def _lanes(c):
    return slice(c * HEAD_PAD, (c + 1) * HEAD_PAD)


def _row_block(j, n=ATT_BLOCK):
    return pl.ds(pl.multiple_of(j * n, n), n)


def _rows(ref, j, c, n=ATT_BLOCK):
    return ref[_row_block(j, n), _lanes(c)]


def _mla_fwd(qh, kh, kvb, cargo=None):
    t = qh.shape[0]
    bq = min(MLA_Q_BLOCK, t)
    per_q = bq // ATT_BLOCK
    ncol = MLA_FWD_COLS
    grid = (N_HEADS // ncol, t // bq)

    def body(*refs):
        steps = [pl.program_id(0), pl.program_id(1)]
        _with_cargo(cargo, refs, 3, 2, steps, grid, lambda own: work(steps[1], *own))

    def work(i, q_ref, k_ref, v_ref, o_ref, lse_ref):
        qs = [q_ref[:, _lanes(c)] for c in range(ncol)]

        def step(j, carry, top):
            cols = range(ncol)
            first = top or 0
            scores = [_dot(qs[c][first:], _rows(k_ref, j, c), NT_DIMS) for c in cols]
            ms, ps, alphas = [], [], []
            for c in cols:
                s = scores[c]
                if top is not None:
                    s = jnp.where(_below_diagonal(bq - first), s, -1e30)
                m_old = carry[c][0][first:]
                m_new = jnp.maximum(m_old, jnp.max(s, axis=-1, keepdims=True))
                ps.append(jnp.exp2(s - m_new).astype(BF16))
                alphas.append(jnp.exp2(m_old - m_new))
                ms.append(m_new)
            accs = [alphas[c] * carry[c][1][first:] + _dot(ps[c], _rows(v_ref, j, c), NN_DIMS) for c in cols]
            if first:
                ms = [jnp.concatenate([carry[c][0][:first], ms[c]], axis=0) for c in cols]
                accs = [jnp.concatenate([carry[c][1][:first], accs[c]], axis=0) for c in cols]
            return tuple(zip(ms, accs))

        init = tuple((jnp.full((bq, 1), -1e30, F32), jnp.zeros((bq, HEAD_PAD), F32)) for _ in range(ncol))
        carry = lax.fori_loop(0, i * per_q, lambda j, cr: step(j, cr, None), init)
        for d in range(per_q):
            carry = step(i * per_q + d, carry, d * ATT_BLOCK)
        for c, (m, acc) in enumerate(carry):
            l = acc[:, 0:1]
            o_ref[:, _lanes(c)] = (acc / l).astype(BF16)
            lse_ref[c] = m + jnp.log2(l)

    width = ncol * HEAD_PAD
    full = pl.BlockSpec((t, width), lambda h, i: (0, h))
    blk = pl.BlockSpec((bq, width), lambda h, i: (i, h))
    extra = cargo.specs() if cargo else []
    outs = pl.pallas_call(
        body, name="mla_fwd", grid=grid,
        in_specs=[blk, full, full] + extra,
        out_specs=[blk, pl.BlockSpec((ncol, bq, 1), lambda h, i: (h, i, 0))] + extra,
        out_shape=[jax.ShapeDtypeStruct((t, N_HEADS * HEAD_PAD), BF16), jax.ShapeDtypeStruct((N_HEADS, t, 1), F32)]
        + (cargo.out_shape() if cargo else []),
        scratch_shapes=cargo.scratch() if cargo else [],
        compiler_params=_params(("arbitrary", "arbitrary")),
    )(qh, kh, kvb, *(cargo.srcs if cargo else []))
    return (outs[0], outs[1], list(outs[2:])) if cargo else outs


def _mla_bwd(qh, kh, kvb, o, do, lse, cargo=None):
    t = qh.shape[0]
    bq = min(MLA_Q_BLOCK, t)
    per_q = bq // ATT_BLOCK
    ncol = MLA_BWD_COLS
    width = ncol * HEAD_PAD
    grid = (N_HEADS // ncol, t // bq)

    def body(*refs):
        steps = [pl.program_id(0), pl.program_id(1)]
        _with_cargo(cargo, refs, 6, 3, steps, grid, lambda own: work(steps[0], steps[1], *own))

    def work(h, i, q_ref, k_ref, v_ref, o_ref, do_ref, lse_ref, dq_ref, dk_hbm, dv_hbm, dk_ref, dv_ref, out_sems):

        @pl.when(i == 0)
        def _():
            dk_ref[...] = jnp.zeros_like(dk_ref)
            dv_ref[...] = jnp.zeros_like(dv_ref)

        qs = [q_ref[:, _lanes(c)] for c in range(ncol)]
        dos = [do_ref[:, _lanes(c)] for c in range(ncol)]
        deltas = [jnp.sum(dos[c].astype(F32) * o_ref[:, _lanes(c)].astype(F32), axis=-1, keepdims=True)
                  for c in range(ncol)]
        lses = [lse_ref[c] for c in range(ncol)]

        def step(j, dqs, top):
            cols = range(ncol)
            first = top or 0
            kbs = [_rows(k_ref, j, c) for c in cols]
            scores = [_dot(qs[c][first:], kbs[c], NT_DIMS) for c in cols]
            dps = [_dot(dos[c][first:], _rows(v_ref, j, c), NT_DIMS) for c in cols]
            pbs, dss = [], []
            for c in cols:
                p = jnp.exp2(scores[c] - lses[c][first:])
                if top is not None:
                    p = jnp.where(_below_diagonal(bq - first), p, 0.0)
                pbs.append(p.astype(BF16))
                dss.append((p * (dps[c] - deltas[c][first:])).astype(BF16))
            for c in cols:
                dv_ref[_row_block(j), _lanes(c)] += _dot(pbs[c], dos[c][first:], TN_DIMS)
                dk_ref[_row_block(j), _lanes(c)] += _dot(dss[c], qs[c][first:], TN_DIMS)
            new = [dqs[c][first:] + _dot(dss[c], kbs[c], NN_DIMS) for c in cols]
            if first:
                new = [jnp.concatenate([dqs[c][:first], new[c]], axis=0) for c in cols]
            return tuple(new)

        init = tuple(jnp.zeros((bq, HEAD_PAD), F32) for _ in range(ncol))
        dqs = lax.fori_loop(0, i * per_q, lambda j, cr: step(j, cr, None), init)
        for d in range(per_q):
            dqs = step(i * per_q + d, dqs, d * ATT_BLOCK)
        for c, dq in enumerate(dqs):
            dq_ref[:, _lanes(c)] = dq * (1.0 / math.sqrt(MLA_QK))

        @pl.when(i == grid[1] - 1)
        def _():
            dk_ref[...] = dk_ref[...] * math.log(2.0)
            cols = pl.ds(pl.multiple_of(h * width, width), width)
            out = [pltpu.make_async_copy(dk_ref, dk_hbm.at[:, cols], out_sems.at[0]),
                   pltpu.make_async_copy(dv_ref, dv_hbm.at[:, cols], out_sems.at[1])]
            for cp in out:
                cp.start()
            for cp in out:
                cp.wait()

    full = pl.BlockSpec((t, width), lambda h, i: (0, h))
    blk = pl.BlockSpec((bq, width), lambda h, i: (i, h))
    wide = jax.ShapeDtypeStruct((t, N_HEADS * HEAD_PAD), F32)
    extra = cargo.specs() if cargo else []
    outs = pl.pallas_call(
        body, name="mla_bwd", grid=grid,
        in_specs=[blk, full, full, blk, blk, pl.BlockSpec((ncol, bq, 1), lambda h, i: (h, i, 0))] + extra,
        out_specs=[blk, HBM_SPEC, HBM_SPEC] + extra,
        out_shape=[wide, wide, wide] + (cargo.out_shape() if cargo else []),
        scratch_shapes=[pltpu.VMEM((t, width), F32), pltpu.VMEM((t, width), F32), pltpu.SemaphoreType.DMA((2,))]
        + (cargo.scratch() if cargo else []),
        compiler_params=_params(("arbitrary", "arbitrary")),
    )(qh, kh, kvb, o, do, lse, *(cargo.srcs if cargo else []))
    return (outs[0], outs[1], outs[2], list(outs[3:])) if cargo else outs


def _head_only(x, lane, u):
    return jnp.where((lane >= u * SB_DIM) & (lane < (u + 1) * SB_DIM), x, jnp.zeros_like(x))


SB_DEAD = -104.0


def _log_sigmoids(z):
    e = jnp.exp(-jnp.abs(z))
    lg = jnp.log(1.0 + e)
    ls_pos = jnp.minimum(z, 0.0) - lg
    return ls_pos, ls_pos - z, e


def _sb_fwd(proj):
    t = proj.shape[0]
    bq, ncol = SB_BLOCK, SB_FWD_COLS
    nq = t // bq
    scale = 1.0 / math.sqrt(SB_DIM)
    pairs = SB_WIDTH // HEAD_PAD

    def body(q_ref, k_ref, v_ref, o_ref, r_ref, first_ref):
        g, i = pl.program_id(0), pl.program_id(1)
        lane = lax.broadcasted_iota(jnp.int32, (bq, HEAD_PAD), 1)
        upper = _tri('gt', bq)
        chains = [(c, u) for c in range(ncol) for u in range(2)]
        qms = [_head_only(q_ref[:, _lanes(c)], lane, u) * scale for c, u in chains]

        def step(j, carry, masked):
            ids = range(len(chains))
            zs = [_dot(qms[n], _rows(k_ref, j, chains[n][0], bq), NT_DIMS) for n in ids]
            pos, neg, parts = [], [], []
            for n in ids:
                ls_pos, ls_neg, _ = _log_sigmoids(zs[n])
                if masked:
                    ls_neg = jnp.where(_causal(True, bq), ls_neg, 0.0)
                pos.append(ls_pos)
                neg.append(ls_neg)
                parts.append(_split_bf16(ls_neg))
            suffix = [_dot(parts[n][0], upper, NN_DIMS) + _dot(parts[n][1], upper, NN_DIMS) for n in ids]
            weights = []
            for n in ids:
                a = jnp.exp(pos[n] + suffix[n] + carry[n][0])
                if masked:
                    a = jnp.where(_causal(True, bq), a, 0.0)
                weights.append(a.astype(BF16))
            return tuple((carry[n][0] + jnp.sum(neg[n], axis=-1, keepdims=True),
                          carry[n][1] + _dot(weights[n], _rows(v_ref, j, chains[n][0], bq), NN_DIMS)) for n in ids)

        init = tuple((jnp.zeros((bq, 1), F32), jnp.zeros((bq, HEAD_PAD), F32)) for _ in chains)
        carry = step(i, init, True)

        def more(state):
            s, cr = state
            live = cr[0][0]
            for n in range(1, len(chains)):
                live = jnp.maximum(live, cr[n][0])
            return jnp.logical_and(s < i, jnp.max(live) > SB_DEAD)

        walked, carry = lax.while_loop(more, lambda st: (st[0] + 1, step(i - 1 - st[0], st[1], False)),
                                       (jnp.int32(0), carry))
        first_ref[g * nq + i] = i - walked
        for n, (c, u) in enumerate(chains):
            r_ref[2 * c + u] = carry[n][0]
        for c in range(ncol):
            o_ref[:, _lanes(c)] = jnp.where(lane < SB_DIM, carry[2 * c][1], carry[2 * c + 1][1]).astype(BF16)

    width = ncol * HEAD_PAD

    def full(c0):
        return pl.BlockSpec((t, width), lambda g, i: (0, c0 // width + g))

    return pl.pallas_call(
        body, name="sb_fwd", grid=(pairs // ncol, t // bq),
        in_specs=[pl.BlockSpec((bq, width), lambda g, i: (i, P_SBQ // width + g)), full(P_SBK), full(P_SBV)],
        out_specs=[pl.BlockSpec((bq, width), lambda g, i: (i, g)),
                   pl.BlockSpec((2 * ncol, bq, 1), lambda g, i: (g, i, 0)),
                   pl.BlockSpec(memory_space=pltpu.SMEM)],
        out_shape=[jax.ShapeDtypeStruct((t, SB_WIDTH), BF16), jax.ShapeDtypeStruct((N_HEADS, t, 1), F32),
                   jax.ShapeDtypeStruct((pairs // ncol * nq,), jnp.int32)],
        compiler_params=_params(("arbitrary", "arbitrary")),
    )(proj, proj, proj)


def _sb_bwd(proj, do, rtot, first):
    t = proj.shape[0]
    bq, ncol = SB_BLOCK, SB_BWD_COLS
    nq = t // bq
    scale = 1.0 / math.sqrt(SB_DIM)
    pairs = SB_WIDTH // HEAD_PAD

    def body(first_ref, q_ref, k_ref, v_ref, do_ref, r_ref, dq_ref, dk_ref, dv_ref):
        g, i = pl.program_id(0), pl.program_id(1)

        @pl.when(i == 0)
        def _():
            dk_ref[...] = jnp.zeros_like(dk_ref)
            dv_ref[...] = jnp.zeros_like(dv_ref)

        lane = lax.broadcasted_iota(jnp.int32, (bq, HEAD_PAD), 1)
        incl = _tri('le', bq)
        excl = _tri('lt', bq)
        chains = [(c, u) for c in range(ncol) for u in range(2)]
        qms = [_head_only(q_ref[:, _lanes(c)], lane, u) * scale for c, u in chains]
        doms = [_head_only(do_ref[:, _lanes(c)], lane, u) for c, u in chains]
        rts = [r_ref[2 * c + u] for c, u in chains]

        def step(j, carry, masked):
            ids = range(len(chains))
            kbs = [_rows(k_ref, j, c, bq) for c in range(ncol)]
            zs =[_dot(qms[n], kbs[chains[n][0]], NT_DIMS) for n in ids]
            das = [_dot(doms[n], _rows(v_ref, j, chains[n][0], bq), NT_DIMS) for n in ids]
            pos, neg, sigs, parts = [], [], [], []
            for n in ids:
                ls_pos, ls_neg, e = _log_sigmoids(zs[n])
                if masked:
                    ls_neg = jnp.where(_causal(True, bq), ls_neg, 0.0)
                pos.append(ls_pos)
                neg.append(ls_neg)
                sigs.append(jnp.where(zs[n] >= 0.0, 1.0, e) * pl.reciprocal(1.0 + e, approx=True))
                parts.append(_split_bf16(ls_neg))
            prefix = [_dot(parts[n][0], incl, NN_DIMS) + _dot(parts[n][1], incl, NN_DIMS) for n in ids]
            evs, eparts, dvs = [], [], []
            for n in ids:
                a = jnp.exp(pos[n] + (rts[n] - (carry[n][0] + prefix[n])))
                if masked:
                    a = jnp.where(_causal(True, bq), a, 0.0)
                dvs.append(_dot(a.astype(BF16), doms[n], TN_DIMS))
                evs.append(a * das[n])
                eparts.append(evs[n].astype(BF16))
            before = [_dot(eparts[n], excl, NN_DIMS) for n in ids]
            out, dks = [], []
            for n in ids:
                dz = evs[n] - sigs[n] * (evs[n] + (carry[n][1] + before[n]))
                if masked:
                    dz = jnp.where(_causal(True, bq), dz, 0.0)
                dzb = dz.astype(BF16)
                dks.append(_dot(dzb, qms[n], TN_DIMS))
                out.append((carry[n][0] + jnp.sum(neg[n], axis=-1, keepdims=True),
                            carry[n][1] + jnp.sum(evs[n], axis=-1, keepdims=True),
                            carry[n][2] + _dot(dzb, kbs[chains[n][0]], NN_DIMS)))
            for c in range(ncol):
                dv_ref[_row_block(j, bq), _lanes(c)] += dvs[2 * c] + dvs[2 * c + 1]
                dk_ref[_row_block(j, bq), _lanes(c)] += dks[2 * c] + dks[2 * c + 1]
            return tuple(out)

        init = tuple((jnp.zeros((bq, 1), F32), jnp.zeros((bq, 1), F32), jnp.zeros((bq, HEAD_PAD), F32)) for _ in chains)
        start = first_ref[(g * ncol // SB_FWD_COLS) * nq + i]
        carry = lax.fori_loop(start, i, lambda j, cr: step(j, cr, False), init)
        carry = step(i, carry, True)
        for c in range(ncol):
            dq_ref[:, _lanes(c)] = jnp.where(lane < SB_DIM, carry[2 * c][2], carry[2 * c + 1][2]) * scale

    width = ncol * HEAD_PAD

    def full(c0):
        return pl.BlockSpec((t, width), lambda g, i, first: (0, c0 // width + g))

    blk = pl.BlockSpec((bq, width), lambda g, i, first: (i, g))
    acc = pl.BlockSpec((t, width), lambda g, i, first: (0, g))
    wide = jax.ShapeDtypeStruct((t, SB_WIDTH), F32)
    return pl.pallas_call(
        body, name="sb_bwd",
        grid_spec=pltpu.PrefetchScalarGridSpec(
            num_scalar_prefetch=1, grid=(pairs // ncol, nq),
            in_specs=[pl.BlockSpec((bq, width), lambda g, i, first: (i, P_SBQ // width + g)), full(P_SBK), full(P_SBV),
                      blk, pl.BlockSpec((2 * ncol, bq, 1), lambda g, i, first: (g, i, 0))],
            out_specs=[blk, acc, acc]),
        out_shape=[wide, wide, wide],
        compiler_params=_params(("arbitrary", "arbitrary")),
    )(first, proj, proj, proj, do, rtot)


def _cols_to_full(g):
    n, r, c = g.shape
    return jnp.transpose(g, (1, 0, 2)).reshape(r, n * c)


def _full_to_cols(w):
    r, c = w.shape
    return jnp.transpose(w.reshape(r, N_DEV, c // N_DEV), (1, 0, 2))


TRANSPOSED = ('ffn1_w_in', 'ffn2_w_in', 'w_in', 'w_q_up')


def _layout_weight(name, g):
    if name in ('ffn1_w_out', 'ffn2_w_out', 'w_out', 'w_ple_gate', 'ffn1_w_in', 'ffn2_w_in'):
        return g.reshape(g.shape[0] * g.shape[1], g.shape[2])
    if name == 'w_in':
        wt = g.reshape(IN_COLS, D_MODEL)
        z = lambda n: jnp.zeros((n, D_MODEL), BF16)
        return jnp.concatenate([wt[0:640], z(64), wt[640:672], z(32), z(256), wt[2208:4256], wt[672:2208]], axis=0)
    if name == 'w_q_up':
        return jnp.pad(g, ((0, 0), (0, HEAD_PAD - MLA_QK), (0, 0))).reshape(N_HEADS * HEAD_PAD, Q_LORA)
    if name == 'w_branch_mla':
        bm = _cols_to_full(g).reshape(N_HEADS, MLA_NOPE, D_MODEL)
        return jnp.pad(bm, ((0, 0), (HEAD_PAD - MLA_NOPE, 0), (0, 0))).reshape(N_HEADS * HEAD_PAD, D_MODEL)
    return _cols_to_full(g)


def _unlayout_grad(name, d):
    if name == 'w_in':
        d = jnp.concatenate([d[0:640], d[704:736], d[P_SBQ:PROJ_W], d[P_GM:P_SBQ]], axis=0)
    if name == 'w_q_up':
        return d.reshape(N_HEADS, HEAD_PAD, Q_LORA)[:, :MLA_QK, :]
    if name in ('ffn1_w_out', 'ffn2_w_out', 'w_out', 'w_ple_gate', 'ffn1_w_in', 'ffn2_w_in', 'w_in'):
        return d.reshape(N_DEV, d.shape[0] // N_DEV, d.shape[1])
    if name == 'w_branch_mla':
        d = d.reshape(N_HEADS, HEAD_PAD, D_MODEL)[:, HEAD_PAD - MLA_NOPE:, :].reshape(SB_WIDTH, D_MODEL)
    return _full_to_cols(d)


def _rope_tables(positions):
    half = MLA_ROPE // 2
    inv_freq = ROPE_BASE ** (-jnp.arange(0, MLA_ROPE, 2, dtype=F32) / MLA_ROPE)
    ang = positions.astype(F32)[:, None] * inv_freq
    cos, sin = jnp.cos(ang), jnp.sin(ang)
    t = positions.shape[0]
    ones = lambda n: jnp.ones((t, n), F32)
    zeros = lambda n: jnp.zeros((t, n), F32)
    cosf = jnp.concatenate([ones(MLA_NOPE), cos, cos, ones(HEAD_PAD - MLA_QK)], axis=1)
    sin_a = jnp.concatenate([zeros(MLA_NOPE), -sin, zeros(half), zeros(HEAD_PAD - MLA_QK)], axis=1)
    sin_b = jnp.concatenate([zeros(MLA_NOPE), zeros(half), sin, zeros(HEAD_PAD - MLA_QK)], axis=1)
    return cosf, sin_a, sin_b


def _local_step(x, p, positions, tgt, norms, plan):
    mm = _matmul
    cosf, sin_a, sin_b = _rope_tables(positions)
    pad_head = lambda g: jnp.pad(g, ((0, 0), (0, HEAD_PAD - MLA_QK)))
    gqh, gkh = pad_head(norms['q_head_norm']), pad_head(norms['k_head_norm'])
    pb = p.astype(BF16)
    w = dict(plan.first_weights())
    dw, dn = {}, {}

    def ride(host, call):
        cargo = plan.cargo(host, dw)
        res, lands = call(cargo), None
        if cargo is not None:
            *res, lands = res
            res = res[0] if len(res) == 1 else tuple(res)
        w.update(plan.landed(host, lands))
        return res

    def ffn_fwd(h, tag):
        n, a, b, act = ride(tag + "_in_fwd", lambda cargo: _swiglu_fwd(
            h, norms[tag + '_norm'], w[tag + '_w_in'], tag + "_in_fwd", cargo))
        out = ride(tag + "_out_fwd", lambda cargo: mm(
            act, w[tag + '_w_out'], mode='nn', out_dtype=F32, name=tag + "_out_fwd", res=h, alpha=0.5, cargo=cargo))
        return out, (n, a, b, act)

    h1, ffn1_saved = ffn_fwd(x, 'ffn1')
    u = _rmsnorm_fwd(h1, norms['mix_norm'], "mix_norm_fwd")
    proj = mm(u, w['w_in'], mode='nt', out_dtype=BF16, name="proj_fwd")
    cqn, ckvn = _latent_fwd(proj, norms['q_latent_norm'], norms['kv_latent_norm'])
    qraw = mm(cqn, w['w_q_up'], mode='nt', out_dtype=F32, name="q_up_fwd")
    kvraw = mm(ckvn, w['w_kv_up'], mode='nn', out_dtype=F32, name="kv_up_fwd")
    qh, kh, kvb = _headprep_fwd(qraw, kvraw, proj, cosf, sin_a, sin_b, gqh, gkh)
    o_mla, lse = ride("mla_fwd", lambda cargo: _mla_fwd(qh, kh, kvb, cargo))
    o_sb, rtot, sb_first = _sb_fwd(proj)
    merged, bm, bs = _merge_fwd(o_mla, w['w_branch_mla'], o_sb, w['w_branch_sb'], proj)
    h2 = mm(merged, w['w_out'], mode='nn', out_dtype=F32, name="mix_out_fwd", res=h1)
    h3, ffn2_saved = ffn_fwd(h2, 'ffn2')
    n3, dh4, dzg, dpp, loss_lanes = _ple_loss(h3, norms['ple_norm'], w['w_ple_gate'], pb, w['w_ple_proj'], tgt)

    dw['w_ple_gate'] = mm(n3, dzg, mode='tn', out_dtype=BF16, name="ple_gate_dw")
    dw['w_ple_proj'] = mm(pb, dpp, mode='tn', out_dtype=BF16, name="ple_proj_dw")
    dh3, dhb3, dn['ple_norm'] = _matmul_norm_bwd(
        dzg, w['w_ple_gate'], h3, norms['ple_norm'], dh4, mode='nt', name="ple_gate_dx", out_scale=0.5)

    def ffn_bwd(h, dh, dhb, saved, tag, out_scale):
        n, a, b, act = saved
        dw[tag + '_w_out'] = mm(act, dhb, mode='tn', out_dtype=BF16, name=tag + "_out_dw", tm=1408)
        dab = _swiglu_bwd(dhb, w[tag + '_w_out'], a, b, tag + "_out_dx")
        dw[tag + '_w_in'] = ride(tag + "_in_dw", lambda cargo: mm(
            dab, n, mode='tn', out_dtype=BF16, name=tag + "_in_dw", tm=1408, cargo=cargo))
        dh_prev, dhb_prev, dn[tag + '_norm'] = ride(tag + "_in_dx", lambda cargo: _matmul_norm_bwd(
            dab, w[tag + '_w_in'], h, norms[tag + '_norm'], dh, mode='nn', name=tag + "_in_dx", out_scale=out_scale,
            cargo=cargo))
        return dh_prev, dhb_prev

    dh2, dhb2 = ffn_bwd(h2, dh3, dhb3, ffn2_saved, 'ffn2', 1.0)
    dw['w_out'] = mm(merged, dhb2, mode='tn', out_dtype=BF16, name="mix_out_dw")
    dbm, dbs, dgates = _merge_bwd(dhb2, w['w_out'], proj, bm, bs)
    dw['w_branch_mla'] = mm(o_mla, dbm, mode='tn', out_dtype=BF16, name="branch_mla_dw")
    dw['w_branch_sb'] = mm(o_sb, dbs, mode='tn', out_dtype=BF16, name="branch_sb_dw")
    do_mla = mm(dbm, w['w_branch_mla'], mode='nt', out_dtype=BF16, name="branch_mla_dx")
    do_sb = mm(dbs, w['w_branch_sb'], mode='nt', out_dtype=BF16, name="branch_sb_dx")
    dqh, dkh, dvp = ride("mla_bwd", lambda cargo: _mla_bwd(qh, kh, kvb, o_mla, do_mla, lse, cargo))
    dsq, dsk, dsv = _sb_bwd(proj, do_sb, rtot, sb_first)
    dqraw, dkvraw, dkr, dgq, dgk = _headprep_bwd(dqh, dkh, dvp, qraw, kvraw, proj, cosf, sin_a, sin_b, gqh, gkh)
    dn['q_head_norm'], dn['k_head_norm'] = dgq[:, :MLA_QK], dgk[:, :MLA_QK]
    dw['w_q_up'] = mm(dqraw, cqn, mode='tn', out_dtype=BF16, name="q_up_dw")
    dw['w_kv_up'] = mm(ckvn, dkvraw, mode='tn', out_dtype=BF16, name="kv_up_dw")
    dcqn = mm(dqraw, w['w_q_up'], mode='nn', out_dtype=F32, name="q_up_dx")
    dckvn = mm(dkvraw, w['w_kv_up'], mode='nt', out_dtype=F32, name="kv_up_dx")
    dlat, dn['q_latent_norm'], dn['kv_latent_norm'] = _latent_bwd(
        dcqn, dckvn, proj, dkr, norms['q_latent_norm'], norms['kv_latent_norm'])
    dproj = jnp.concatenate([dlat, dgates, dsq.astype(BF16), dsk.astype(BF16), dsv.astype(BF16)], axis=1)
    dw['w_in'] = ride("proj_dw", lambda cargo: mm(dproj, u, mode='tn', out_dtype=BF16, name="proj_dw", tm=1536, cargo=cargo))
    dh1, dhb1, dn['mix_norm'] = ride("proj_dx", lambda cargo: _matmul_norm_bwd(
        dproj, w['w_in'], h1, norms['mix_norm'], dh2, mode='nn', name="proj_dx", out_scale=0.5, cargo=cargo))
    dx, _ = ffn_bwd(x, dh1, dhb1, ffn1_saved, 'ffn1', 1.0)
    return dx, loss_lanes, dw, dn


MESH = pl.DeviceIdType.MESH
HBM_SPEC = pl.BlockSpec(memory_space=pl.ANY)


def _position():
    return lax.axis_index("x"), lax.axis_index("y"), lax.axis_index("c")


def _index(px, py, pc):
    return 4 * px + 2 * py + pc


def _all_gather(shards):
    n = len(shards)

    def body(*refs):
        ins, outs = refs[:n], refs[n:2 * n]
        send_sems, recv_sems, local_sems = refs[2 * n:]
        x, y, c = _position()
        me, sibling = (x, y, c), (x, y, 1 - c)
        chips = [(1 - x, y), (x, 1 - y), (1 - x, 1 - y)]

        def copy(a, k, block, to, own=False):
            dst = outs[a].at[_index(*block)]
            return pltpu.make_async_remote_copy(
                src_ref=ins[a] if own else dst, dst_ref=dst,
                send_sem=send_sems.at[a, k], recv_sem=recv_sems.at[a, k], device_id=to, device_id_type=MESH)

        mine = [pltpu.make_async_copy(ins[a], outs[a].at[_index(*me)], local_sems.at[a]) for a in range(n)]
        for cp in mine:
            cp.start()
        first = []
        for a in range(n):
            first.append(copy(a, 0, me, sibling, own=True))
            first += [copy(a, 1 + j, me, (*chip, c), own=True) for j, chip in enumerate(chips)]
        for cp in first:
            cp.start()
        passed = []
        for j, chip in enumerate(chips):
            for a in range(n):
                copy(a, 1 + j, (*chip, c), me).wait_recv()
                fwd = copy(a, 4 + j, (*chip, c), sibling)
                fwd.start()
                passed.append(fwd)
        for a in range(n):
            copy(a, 0, sibling, me).wait_recv()
            for j, chip in enumerate(chips):
                copy(a, 4 + j, (*chip, 1 - c), me).wait_recv()
        for cp in first + passed:
            cp.wait_send()
        for cp in mine:
            cp.wait()

    return pl.pallas_call(
        body, name="weights_all_gather",
        in_specs=[HBM_SPEC] * n, out_specs=[HBM_SPEC] * n,
        out_shape=[jax.ShapeDtypeStruct((N_DEV,) + s.shape, s.dtype) for s in shards],
        scratch_shapes=[pltpu.SemaphoreType.DMA((n, 7)), pltpu.SemaphoreType.DMA((n, 7)), pltpu.SemaphoreType.DMA((n,))],
    )(*shards)


def _exchange(parts):
    n = len(parts)
    masks = [(mx, my, mc) for mx in (0, 1) for my in (0, 1) for mc in (0, 1)][1:]

    def body(*refs):
        ins, outs = refs[:n], refs[n:2 * n]
        send_sems, recv_sems, local_sems = refs[2 * n:]
        x, y, c = _position()
        me = _index(x, y, c)

        def peer_of(mask):
            mx, my, mc = mask
            return (x + mx - 2 * x * mx, y + my - 2 * y * my, c + mc - 2 * c * mc)

        def copy(a, k):
            peer = peer_of(masks[k])
            return pltpu.make_async_remote_copy(
                src_ref=ins[a].at[_index(*peer)], dst_ref=outs[a].at[me],
                send_sem=send_sems.at[a, k], recv_sem=recv_sems.at[a, k], device_id=peer, device_id_type=MESH)

        def landed(a, k):
            peer = peer_of(masks[k])
            return pltpu.make_async_remote_copy(
                src_ref=ins[a].at[me], dst_ref=outs[a].at[_index(*peer)],
                send_sem=send_sems.at[a, k], recv_sem=recv_sems.at[a, k], device_id=peer, device_id_type=MESH)

        mine = [pltpu.make_async_copy(ins[a].at[me], outs[a].at[me], local_sems.at[a]) for a in range(n)]
        for cp in mine:
            cp.start()
        sent = [copy(a, k) for k in range(7) for a in range(n)]
        for cp in sent:
            cp.start()
        for k in range(7):
            for a in range(n):
                landed(a, k).wait_recv()
        for cp in sent:
            cp.wait_send()
        for cp in mine:
            cp.wait()

    return pl.pallas_call(
        body, name="grads_exchange",
        in_specs=[HBM_SPEC] * n, out_specs=[HBM_SPEC] * n,
        out_shape=[jax.ShapeDtypeStruct(s.shape, s.dtype) for s in parts],
        scratch_shapes=[pltpu.SemaphoreType.DMA((n, 7)), pltpu.SemaphoreType.DMA((n, 7)), pltpu.SemaphoreType.DMA((n,))],
    )(*parts)


PEER_MASKS = [(mx, my, mc) for mx in (0, 1) for my in (0, 1) for mc in (0, 1)][1:]


def _peer(mask):
    x, y, c = _position()
    mx, my, mc = mask
    return (x + mx - 2 * x * mx, y + my - 2 * y * my, c + mc - 2 * c * mc)


class _Cargo:
    def __init__(self, srcs, scatter):
        self.srcs, self.scatter, self.n = list(srcs), scatter, len(srcs)

    def specs(self):
        return [HBM_SPEC] * self.n

    def out_shape(self):
        return [jax.ShapeDtypeStruct(s.shape if self.scatter else (N_DEV,) + s.shape, s.dtype) for s in self.srcs]

    def scratch(self):
        per_copy = pltpu.SemaphoreType.DMA((self.n, len(PEER_MASKS)))
        return [per_copy, per_copy, pltpu.SemaphoreType.DMA((self.n,))]

    def _mine(self, src_refs, a, to):
        return src_refs[a].at[to] if self.scatter else src_refs[a]

    def _shard_copy(self, src_refs, land_refs, sems, a, k, block, to, own=False):
        dst = land_refs[a].at[_index(*block)]
        return pltpu.make_async_remote_copy(
            src_ref=src_refs[a] if own else dst, dst_ref=dst,
            send_sem=sems[0].at[a, k], recv_sem=sems[1].at[a, k], device_id=to, device_id_type=MESH)

    def _first_hops(self, src_refs, land_refs, sems):
        x, y, c = _position()
        chips = [(1 - x, y), (x, 1 - y), (1 - x, 1 - y)]
        hops = []
        for a in range(self.n):
            hops.append(self._shard_copy(src_refs, land_refs, sems, a, 0, (x, y, c), (x, y, 1 - c), own=True))
            hops += [self._shard_copy(src_refs, land_refs, sems, a, 1 + j, (x, y, c), (*chip, c), own=True)
                     for j, chip in enumerate(chips)]
        return hops, chips

    def start(self, src_refs, land_refs, sems):
        send, recv, local = sems
        me = _index(*_position())
        for a in range(self.n):
            pltpu.make_async_copy(self._mine(src_refs, a, me), land_refs[a].at[me], local.at[a]).start()
        if not self.scatter:
            for cp in self._first_hops(src_refs, land_refs, sems)[0]:
                cp.start()
            return
        for k, mask in enumerate(PEER_MASKS):
            peer = _peer(mask)
            for a in range(self.n):
                pltpu.make_async_remote_copy(
                    src_ref=self._mine(src_refs, a, _index(*peer)), dst_ref=land_refs[a].at[me],
                    send_sem=send.at[a, k], recv_sem=recv.at[a, k], device_id=peer, device_id_type=MESH).start()

    def _wait_gathered(self, src_refs, land_refs, sems):
        x, y, c = _position()
        me, sibling = (x, y, c), (x, y, 1 - c)
        first, chips = self._first_hops(src_refs, land_refs, sems)
        passed = []
        for j, chip in enumerate(chips):
            for a in range(self.n):
                self._shard_copy(src_refs, land_refs, sems, a, 1 + j, (*chip, c), me).wait_recv()
                passed.append(self._shard_copy(src_refs, land_refs, sems, a, 4 + j, (*chip, c), sibling))
                passed[-1].start()
        for a in range(self.n):
            self._shard_copy(src_refs, land_refs, sems, a, 0, sibling, me).wait_recv()
            for j, chip in enumerate(chips):
                self._shard_copy(src_refs, land_refs, sems, a, 4 + j, (*chip, 1 - c), me).wait_recv()
        for cp in first + passed:
            cp.wait_send()

    def wait(self, src_refs, land_refs, sems):
        send, recv, local = sems
        me = _index(*_position())
        if not self.scatter:
            self._wait_gathered(src_refs, land_refs, sems)
        for k, mask in enumerate(PEER_MASKS if self.scatter else []):
            peer = _peer(mask)
            there = _index(*peer)
            for a in range(self.n):
                pltpu.make_async_remote_copy(
                    src_ref=self._mine(src_refs, a, me), dst_ref=land_refs[a].at[there],
                    send_sem=send.at[a, k], recv_sem=recv.at[a, k], device_id=peer, device_id_type=MESH).wait_recv()
                pltpu.make_async_remote_copy(
                    src_ref=self._mine(src_refs, a, there), dst_ref=land_refs[a].at[me],
                    send_sem=send.at[a, k], recv_sem=recv.at[a, k], device_id=peer, device_id_type=MESH).wait_send()
        for a in range(self.n):
            pltpu.make_async_copy(self._mine(src_refs, a, me), land_refs[a].at[me], local.at[a]).wait()


def _with_cargo(cargo, refs, n_in, n_out, steps, counts, compute):
    if cargo is None:
        compute(refs)
        return
    n = cargo.n
    src_refs = refs[n_in:n_in + n]
    land_refs = refs[n_in + n + n_out:n_in + 2 * n + n_out]
    sems = refs[-3:]
    first = functools.reduce(jnp.logical_and, [s == 0 for s in steps])
    last = functools.reduce(jnp.logical_and, [s == c - 1 for s, c in zip(steps, counts)])

    @pl.when(first)
    def _():
        cargo.start(src_refs, land_refs, sems)

    compute(refs[:n_in] + refs[n_in + n:n_in + n + n_out] + refs[n_in + 2 * n + n_out:-3])

    @pl.when(last)
    def _():
        cargo.wait(src_refs, land_refs, sems)


def _adamw(parts, w, m, v, name):
    r, c = w.shape
    tr = next((t for t in (512, 384, 352, 256, 128) if r % t == 0), r) if r > 512 else r
    tc = c if tr < r or r <= 512 else 256
    assert r % tr == 0 and c % tc == 0
    bc1 = 1.0 - ADAM_B1 ** ADAM_STEP
    bc2 = 1.0 - ADAM_B2 ** ADAM_STEP

    def body(p_ref, w_ref, m_ref, v_ref, g_ref, d_ref, nm_ref, nv_ref):
        g = p_ref[0].astype(F32)
        for s in range(1, N_DEV):
            g = g + p_ref[s].astype(F32)
        nm = ADAM_B1 * m_ref[...] + (1.0 - ADAM_B1) * g
        nv = ADAM_B2 * v_ref[...] + (1.0 - ADAM_B2) * (g * g)
        g_ref[...] = g
        nm_ref[...] = nm
        nv_ref[...] = nv
        d_ref[...] = -ADAM_LR * ((nm / bc1) / (jnp.sqrt(nv / bc2) + ADAM_EPS) + ADAM_WD * w_ref[...])

    tile = pl.BlockSpec((tr, tc), lambda i, j: (i, j))
    out = jax.ShapeDtypeStruct((r, c), F32)
    return pl.pallas_call(
        body, name=name, grid=(r // tr, c // tc),
        in_specs=[pl.BlockSpec((N_DEV, tr, tc), lambda i, j: (0, i, j)), tile, tile, tile],
        out_specs=[tile] * 4, out_shape=[out] * 4,
        compiler_params=_params(("parallel", "parallel")),
    )(parts, w, m, v)


GATHER_FIRST = ['ffn1_w_in']
RIDES = {
    'ffn1_in_fwd': ('weights', ['ffn1_w_out', 'w_in']),
    'ffn1_out_fwd': ('weights', ['w_q_up', 'w_kv_up', 'w_branch_mla', 'w_branch_sb', 'w_out']),
    'mla_fwd': ('weights', ['ffn2_w_in', 'ffn2_w_out', 'w_ple_gate', 'w_ple_proj']),
    'mla_bwd': ('grads', ['w_ple_gate', 'w_ple_proj', 'ffn2_w_out', 'ffn2_w_in', 'w_out', 'w_branch_mla', 'w_branch_sb']),
    'proj_dw': ('grads', ['w_q_up', 'w_kv_up']),
    'proj_dx': ('grads', ['w_in']),
    'ffn1_in_dw': ('grads', ['ffn1_w_out']),
    'ffn1_in_dx': ('grads', ['ffn1_w_in']),
}


class _Plan:
    def __init__(self, shards):
        self.shards = shards
        self.received = {}

    def first_weights(self):
        gathered = _all_gather([self.shards[n] for n in GATHER_FIRST])
        return {n: _layout_weight(n, g) for n, g in zip(GATHER_FIRST, gathered)}

    def cargo(self, host, dw):
        if host not in RIDES:
            return None
        kind, names = RIDES[host]
        if kind == 'weights':
            return _Cargo([self.shards[n] for n in names], False)
        return _Cargo([_unlayout_grad(n, dw.pop(n)) for n in names], True)

    def landed(self, host, lands):
        if host not in RIDES:
            return {}
        kind, names = RIDES[host]
        if kind == 'weights':
            return {n: _layout_weight(n, land) for n, land in zip(names, lands)}
        self.received.update(zip(names, lands))
        return {}


def _pack_small(vecs):
    flat = jnp.concatenate([v.reshape(-1) for v in vecs])
    return jnp.pad(flat, (0, SMALL_ROWS * 128 - flat.shape[0])).reshape(SMALL_ROWS, 128)


def _unpack_small(packed, sizes):
    flat = packed.reshape(-1)
    out, at = [], 0
    for n in sizes:
        out.append(flat[at:at + n])
        at += n
    return out


def kernel(x, p, positions, ffn1_norm, ffn1_w_in, ffn1_w_out, mix_norm, w_in, q_latent_norm, w_q_up, kv_latent_norm, w_kv_up, q_head_norm, k_head_norm, w_branch_mla, w_branch_sb, w_out, ffn2_norm, ffn2_w_in, ffn2_w_out, ple_norm, w_ple_gate, w_ple_proj, loss_target, m_ffn1_norm, m_ffn1_w_in, m_ffn1_w_out, m_mix_norm, m_w_in, m_q_latent_norm, m_w_q_up, m_kv_latent_norm, m_w_kv_up, m_q_head_norm, m_k_head_norm, m_w_branch_mla, m_w_branch_sb, m_w_out, m_ffn2_norm, m_ffn2_w_in, m_ffn2_w_out, m_ple_norm, m_w_ple_gate, m_w_ple_proj, v_ffn1_norm, v_ffn1_w_in, v_ffn1_w_out, v_mix_norm, v_w_in, v_q_latent_norm, v_w_q_up, v_kv_latent_norm, v_w_kv_up, v_q_head_norm, v_k_head_norm, v_w_branch_mla, v_w_branch_sb, v_w_out, v_ffn2_norm, v_ffn2_w_in, v_ffn2_w_out, v_ple_norm, v_w_ple_gate, v_w_ple_proj):
    given = dict(locals())
    wts = {n: given[n] for n in WEIGHTS}
    mom = {n: given['m_' + n] for n in WEIGHTS}
    var = {n: given['v_' + n] for n in WEIGHTS}

    def local(a, n):
        return jnp.swapaxes(a[0], 0, 1) if n in TRANSPOSED else a[0]

    plan = _Plan({n: local(wts[n], n).astype(BF16) for n in MATS})
    norms = {n: wts[n] for n in NORMS}
    dx, loss_lanes, dw, dn = _local_step(x[0], p[0, 0], positions[0], loss_target[0], norms, plan)
    assert not dw

    out = {}
    for n in MATS:
        res = _adamw(plan.received[n], local(wts[n], n), local(mom[n], n), local(var[n], n), "adamw_" + n)
        out[n] = [local(r[None], n)[None] for r in res]
    small = _pack_small([dn[n] for n in NORMS] + [0.5 / D_MODEL * jnp.sum(loss_lanes)[None]])
    small_parts = _exchange([jnp.broadcast_to(small[None], (N_DEV, SMALL_ROWS, 128))])[0]
    sizes = [wts[n].shape[1] for n in NORMS]
    pack = lambda d: _pack_small([d[n] for n in NORMS])
    small_res = _adamw(small_parts, pack(wts), pack(mom), pack(var), "adamw_norms")
    loss = small_res[0].reshape(-1)[sum(sizes)]
    for i, res in enumerate(small_res):
        for n, vec in zip(NORMS, _unpack_small(res, sizes)):
            out.setdefault(n, [None] * 4)[i] = vec[None]

    return (loss, dx[None], *[out[n][0] for n in WEIGHTS], *[out[n][1] for n in WEIGHTS],
            *[out[n][2] for n in WEIGHTS], *[out[n][3] for n in WEIGHTS])
```

```python
import functools
import math

import jax
import jax.numpy as jnp
from jax import lax
from jax.experimental import pallas as pl
from jax.experimental.pallas import tpu as pltpu

F32 = jnp.float32
BF16 = jnp.bfloat16

N_DEV = 8
D_MODEL = 1024
D_FF = 2816
PLE_DIM = 256
NORM_EPS = 1e-6
N_HEADS = 8
HEAD_PAD = 128
MLA_NOPE = 64
MLA_ROPE = 32
MLA_QK = 96
Q_LORA = 384
KV_LORA = 256
SB_DIM = 64
SB_WIDTH = 512
ROPE_BASE = 10000.0
IN_COLS = 4256

PROJ_W = 4608
P_CQ, P_CKV, P_KR, P_GM, P_GS, P_SBQ, P_SBK, P_SBV = 0, 384, 640, 1024, 2048, 3072, 3584, 4096

ADAM_LR, ADAM_B1, ADAM_B2, ADAM_EPS, ADAM_WD, ADAM_STEP = 0.001, 0.9, 0.999, 1e-08, 0.01, 10

VMEM_LIMIT = 52 * 1024 * 1024
MATMUL_VMEM = 40 * 1024 * 1024

WEIGHTS = ['ffn1_norm', 'ffn1_w_in', 'ffn1_w_out', 'mix_norm', 'w_in', 'q_latent_norm', 'w_q_up',
           'kv_latent_norm', 'w_kv_up', 'q_head_norm', 'k_head_norm', 'w_branch_mla', 'w_branch_sb',
           'w_out', 'ffn2_norm', 'ffn2_w_in', 'ffn2_w_out', 'ple_norm', 'w_ple_gate', 'w_ple_proj']
NORMS = ['ffn1_norm', 'mix_norm', 'q_latent_norm', 'kv_latent_norm', 'q_head_norm', 'k_head_norm',
         'ffn2_norm', 'ple_norm']
MATS = [n for n in WEIGHTS if n not in NORMS]
SMALL_ROWS = 48

NT_DIMS = (((1,), (1,)), ((), ()))
NN_DIMS = (((1,), (0,)), ((), ()))
TN_DIMS = (((0,), (0,)), ((), ()))


def _params(sem=None, vmem=VMEM_LIMIT):
    return pltpu.CompilerParams(dimension_semantics=sem, vmem_limit_bytes=vmem)


def _pick(n, cap):
    if n <= cap:
        return n
    best = None
    for t in range(128, cap + 1, 128):
        if n % t == 0:
            best = t
    assert best is not None, (n, cap)
    return best


def _dot(a, b, dims):
    return lax.dot_general(a, b, dims, preferred_element_type=F32)


def _matmul(a, b, *, mode, out_dtype, name, tm=None, tn=None, tk=None, res=None, alpha=1.0, cargo=None):
    if mode == 'nn':
        (m, k), (k2, n) = a.shape, b.shape
    elif mode == 'nt':
        (m, k), (n, k2) = a.shape, b.shape
    else:
        (k, m), (k2, n) = a.shape, b.shape
    assert k == k2, (name, a.shape, b.shape)
    has_res = res is not None
    tn = tn or _pick(n, 512)

    def vmem(tm_, tk_):
        io = 2 * 2 * (tm_ * tk_ + tk_ * tn) + 2 * tm_ * tn * (jnp.dtype(out_dtype).itemsize + 4 * has_res)
        return io + (4 * tm_ * tn if tk_ < k else 0)

    tries = [(tm_, tk_) for tk_ in ([tk] if tk else [k, _pick(k, 2048)])
             for tm_ in ([tm] if tm else [_pick(m, 2048), _pick(m, 1024), _pick(m, 512)])]
    tm, tk = next((c for c in tries if vmem(*c) <= MATMUL_VMEM), tries[-1])
    assert m % tm == 0 and n % tn == 0 and k % tk == 0, (name, m, n, k, tm, tn, tk)
    nk = k // tk
    dims = {'nn': NN_DIMS, 'nt': NT_DIMS, 'tn': TN_DIMS}[mode]

    def epilogue(acc, r_ref, o_ref):
        if alpha != 1.0:
            acc = acc * alpha
        if has_res:
            acc = r_ref[...] + acc
        o_ref[...] = acc.astype(out_dtype)

    grid = (m // tm, n // tn, nk)

    def body(*refs):
        steps = [pl.program_id(d) for d in range(3)]

        def compute(own):
            a_ref, b_ref = own[0], own[1]
            r_ref = own[2] if has_res else None
            o_ref = own[2 + has_res]
            if nk == 1:
                epilogue(_dot(a_ref[...], b_ref[...], dims), r_ref, o_ref)
                return
            acc_ref = own[-1]

            @pl.when(steps[2] == 0)
            def _():
                acc_ref[...] = jnp.zeros_like(acc_ref)

            acc_ref[...] += _dot(a_ref[...], b_ref[...], dims)

            @pl.when(steps[2] == nk - 1)
            def _():
                epilogue(acc_ref[...], r_ref, o_ref)

        _with_cargo(cargo, refs, 2 + has_res, 1, steps, grid, compute)

    if mode == 'tn':
        a_spec = pl.BlockSpec((tk, tm), lambda i, j, kk: (kk, i))
    else:
        a_spec = pl.BlockSpec((tm, tk), lambda i, j, kk: (i, kk))
    if mode == 'nt':
        b_spec = pl.BlockSpec((tn, tk), lambda i, j, kk: (j, kk))
    else:
        b_spec = pl.BlockSpec((tk, tn), lambda i, j, kk: (kk, j))
    o_spec = pl.BlockSpec((tm, tn), lambda i, j, kk: (i, j))
    in_specs = [a_spec, b_spec] + ([o_spec] if has_res else [])
    args = (a, b) + ((res,) if has_res else ())
    out_shape = jax.ShapeDtypeStruct((m, n), out_dtype)
    scratch = [pltpu.VMEM((tm, tn), F32)] if nk > 1 else []
    if cargo is None:
        return pl.pallas_call(
            body, name=name, grid=grid, in_specs=in_specs, out_specs=o_spec, out_shape=out_shape,
            scratch_shapes=scratch, compiler_params=_params(("parallel", "parallel", "arbitrary")),
        )(*args)
    outs = pl.pallas_call(
        body, name=name, grid=grid, in_specs=in_specs + cargo.specs(), out_specs=[o_spec] + cargo.specs(),
        out_shape=[out_shape] + cargo.out_shape(), scratch_shapes=scratch + cargo.scratch(),
        compiler_params=_params(("arbitrary", "arbitrary", "arbitrary")),
    )(*args, *cargo.srcs)
    return outs[0], list(outs[1:])


def _row_tile(t, cap=512):
    return min(t, cap)


def _rms(x, width):
    return lax.rsqrt(jnp.sum(x * x, axis=-1, keepdims=True) * (1.0 / width) + NORM_EPS)


def _rmsnorm_fwd(x, g, name):
    t, d = x.shape
    tr = _row_tile(t)

    def body(x_ref, g_ref, o_ref):
        xv = x_ref[...]
        o_ref[...] = ((xv * _rms(xv, d)) * g_ref[...]).astype(BF16)

    return pl.pallas_call(
        body, name=name, grid=(t // tr,),
        in_specs=[pl.BlockSpec((tr, d), lambda i: (i, 0)), pl.BlockSpec((1, d), lambda i: (0, 0))],
        out_specs=pl.BlockSpec((tr, d), lambda i: (i, 0)),
        out_shape=jax.ShapeDtypeStruct((t, d), BF16),
        compiler_params=_params(("parallel",)),
    )(x, g)


def _matmul_norm_bwd(a, b, x, g, dh_in, *, mode, name, out_scale, cargo=None):
    m, k = a.shape
    d = x.shape[1]
    tn = _pick(d, 512)

    def vmem(tm_):
        return 2 * 2 * (tm_ * k + k * tn) + tm_ * d * (4 + 2 * (4 + 4) + 2 * (4 + 2))

    tm = next((c for c in (_pick(m, 1024), _pick(m, 512), _pick(m, 256)) if vmem(c) <= MATMUL_VMEM), _pick(m, 256))
    grid = (m // tm, d // tn)
    dims = {'nn': NN_DIMS, 'nt': NT_DIMS}[mode]

    def body(*refs):
        steps = [pl.program_id(0), pl.program_id(1)]

        def compute(own):
            a_ref, b_ref, x_ref, g_ref, dhin_ref, dh_ref, dhb_ref, dg_ref, dn_ref = own
            for jj in range(grid[1]):
                @pl.when(steps[1] == jj)
                def _(jj=jj):
                    dn_ref[:, jj * tn:(jj + 1) * tn] = _dot(a_ref[...], b_ref[...], dims)

            @pl.when(steps[1] == grid[1] - 1)
            def _():
                xv = x_ref[...]
                dnv = dn_ref[...]
                r = _rms(xv, d)
                y = xv * r
                dy = dnv * g_ref[...]
                dh = dhin_ref[...] + r * (dy - y * (jnp.sum(dy * y, axis=-1, keepdims=True) * (1.0 / d)))
                dh_ref[...] = dh
                dhb_ref[...] = (dh * out_scale).astype(BF16)
                part = jnp.sum(dnv * y, axis=0, keepdims=True)

                @pl.when(steps[0] == 0)
                def _():
                    dg_ref[...] = part

                @pl.when(steps[0] > 0)
                def _():
                    dg_ref[...] += part

        _with_cargo(cargo, refs, 5, 3, steps, grid, compute)

    b_spec = pl.BlockSpec((k, tn), lambda i, j: (0, j)) if mode == 'nn' else pl.BlockSpec((tn, k), lambda i, j: (j, 0))
    row = pl.BlockSpec((tm, d), lambda i, j: (i, 0))
    vec = pl.BlockSpec((1, d), lambda i, j: (0, 0))
    extra = cargo.specs() if cargo else []
    outs = pl.pallas_call(
        body, name=name, grid=grid,
        in_specs=[pl.BlockSpec((tm, k), lambda i, j: (i, 0)), b_spec, row, vec, row] + extra,
        out_specs=[row, row, vec] + extra,
        out_shape=[jax.ShapeDtypeStruct((m, d), F32), jax.ShapeDtypeStruct((m, d), BF16),
                   jax.ShapeDtypeStruct((1, d), F32)] + (cargo.out_shape() if cargo else []),
        scratch_shapes=[pltpu.VMEM((tm, d), F32)] + (cargo.scratch() if cargo else []),
        compiler_params=_params(("arbitrary", "arbitrary")),
    )(a, b, x, g, dh_in, *(cargo.srcs if cargo else []))
    return (outs[0], outs[1], outs[2], list(outs[3:])) if cargo else outs


def _sigmoid(x):
    return 1.0 / (1.0 + jnp.exp(-x))


SWIGLU_CHUNK = 256
SWIGLU_COLS = 1408


def _chunks(width):
    return [slice(lo, min(lo + SWIGLU_CHUNK, width)) for lo in range(0, width, SWIGLU_CHUNK)]


def _swiglu_fwd(h, g, wt_in, name, cargo=None):
    t = h.shape[0]
    tm = _pick(t, 1024)
    grid = (t // tm, D_FF // SWIGLU_COLS)

    def body(*refs):
        steps = [pl.program_id(0), pl.program_id(1)]

        def compute(own):
            h_ref, g_ref, wa_ref, wb_ref, n_ref, a_ref, b_ref, act_ref, n_sc = own

            @pl.when(steps[1] == 0)
            def _():
                xv = h_ref[...]
                n_sc[...] = ((xv * _rms(xv, D_MODEL)) * g_ref[...]).astype(BF16)
                n_ref[...] = n_sc[...]

            nv = n_sc[...]
            for cols in _chunks(SWIGLU_COLS):
                a = _dot(nv, wa_ref[cols, :], NT_DIMS)
                b = _dot(nv, wb_ref[cols, :], NT_DIMS)
                a_ref[:, cols] = a.astype(BF16)
                b_ref[:, cols] = b.astype(BF16)
                act_ref[:, cols] = (a * _sigmoid(a) * b).astype(BF16)

        _with_cargo(cargo, refs, 4, 4, steps, grid, compute)

    half = D_FF // SWIGLU_COLS
    row = pl.BlockSpec((tm, D_MODEL), lambda i, j: (i, 0))
    tile = pl.BlockSpec((tm, SWIGLU_COLS), lambda i, j: (i, j))
    out = jax.ShapeDtypeStruct((t, D_FF), BF16)
    extra = cargo.specs() if cargo else []
    outs = pl.pallas_call(
        body, name=name, grid=grid,
        in_specs=[row, pl.BlockSpec((1, D_MODEL), lambda i, j: (0, 0)),
                  pl.BlockSpec((SWIGLU_COLS, D_MODEL), lambda i, j: (j, 0)),
                  pl.BlockSpec((SWIGLU_COLS, D_MODEL), lambda i, j: (half + j, 0))] + extra,
        out_specs=[row, tile, tile, tile] + extra,
        out_shape=[jax.ShapeDtypeStruct((t, D_MODEL), BF16), out, out, out] + (cargo.out_shape() if cargo else []),
        scratch_shapes=[pltpu.VMEM((tm, D_MODEL), BF16)] + (cargo.scratch() if cargo else []),
        compiler_params=_params(("arbitrary", "arbitrary")),
    )(h, g, wt_in, wt_in, *(cargo.srcs if cargo else []))
    return (outs[0], outs[1], outs[2], outs[3], list(outs[4:])) if cargo else outs


def _swiglu_bwd(dh, w_out, a, b, name):
    t = a.shape[0]
    tr = _row_tile(t, 512)

    def body(d_ref, w_ref, a_ref, b_ref, o_ref):
        dhv = d_ref[...]
        for cols in _chunks(D_FF):
            dv = _dot(dhv, w_ref[cols, :], NT_DIMS)
            av = a_ref[:, cols].astype(F32)
            s = _sigmoid(av)
            o_ref[:, cols] = (dv * b_ref[:, cols].astype(F32) * s * (1.0 + av * (1.0 - s))).astype(BF16)
            o_ref[:, slice(D_FF + cols.start, D_FF + cols.stop)] = (dv * av * s).astype(BF16)

    row = pl.BlockSpec((tr, D_FF), lambda i: (i, 0))
    return pl.pallas_call(
        body, name=name, grid=(t // tr,),
        in_specs=[pl.BlockSpec((tr, D_MODEL), lambda i: (i, 0)), pl.BlockSpec((D_FF, D_MODEL), lambda i: (0, 0)), row, row],
        out_specs=pl.BlockSpec((tr, 2 * D_FF), lambda i: (i, 0)),
        out_shape=jax.ShapeDtypeStruct((t, 2 * D_FF), BF16),
        compiler_params=_params(("parallel",)),
    )(dh, w_out, a, b)


def _latent_fwd(proj, gq, gkv):
    t = proj.shape[0]
    tr = _row_tile(t)

    def body(p_ref, gq_ref, gkv_ref, cq_ref, ckv_ref):
        cq = p_ref[:, P_CQ:P_CQ + Q_LORA].astype(F32)
        ckv = p_ref[:, P_CKV:P_CKV + KV_LORA].astype(F32)
        cq_ref[...] = ((cq * _rms(cq, Q_LORA)) * gq_ref[...]).astype(BF16)
        ckv_ref[...] = ((ckv * _rms(ckv, KV_LORA)) * gkv_ref[...]).astype(BF16)

    return pl.pallas_call(
        body, name="latent_fwd", grid=(t // tr,),
        in_specs=[pl.BlockSpec((tr, 1024), lambda i: (i, 0)), pl.BlockSpec((1, Q_LORA), lambda i: (0, 0)),
                  pl.BlockSpec((1, KV_LORA), lambda i: (0, 0))],
        out_specs=[pl.BlockSpec((tr, Q_LORA), lambda i: (i, 0)), pl.BlockSpec((tr, KV_LORA), lambda i: (i, 0))],
        out_shape=[jax.ShapeDtypeStruct((t, Q_LORA), BF16), jax.ShapeDtypeStruct((t, KV_LORA), BF16)],
        compiler_params=_params(("parallel",)),
    )(proj, gq, gkv)


def _latent_bwd(dcqn, dckvn, proj, dkr, gq, gkv):
    t = proj.shape[0]
    tr = _row_tile(t, 256)

    def norm_bwd(dn, x, g, width):
        r = _rms(x, width)
        y = x * r
        dy = dn * g
        dx = r * (dy - y * (jnp.sum(dy * y, axis=-1, keepdims=True) * (1.0 / width)))
        return dx, jnp.sum(dn * y, axis=0, keepdims=True)

    def body(dcq_ref, dckv_ref, p_ref, dkr_ref, gq_ref, gkv_ref, o_ref, dgq_ref, dgkv_ref):
        i = pl.program_id(0)
        dcq, pq = norm_bwd(dcq_ref[...], p_ref[:, P_CQ:P_CQ + Q_LORA].astype(F32), gq_ref[...], Q_LORA)
        dckv, pkv = norm_bwd(dckv_ref[...], p_ref[:, P_CKV:P_CKV + KV_LORA].astype(F32), gkv_ref[...], KV_LORA)
        o_ref[:, P_CQ:P_CQ + Q_LORA] = dcq.astype(BF16)
        o_ref[:, P_CKV:P_CKV + KV_LORA] = dckv.astype(BF16)
        o_ref[:, P_KR:P_KR + 128] = dkr_ref[...].astype(BF16)
        o_ref[:, P_KR + 128:1024] = jnp.zeros((tr, 1024 - P_KR - 128), BF16)

        @pl.when(i == 0)
        def _():
            dgq_ref[...] = pq
            dgkv_ref[...] = pkv

        @pl.when(i > 0)
        def _():
            dgq_ref[...] += pq
            dgkv_ref[...] += pkv

    def row(w):
        return pl.BlockSpec((tr, w), lambda i: (i, 0))

    def vec(w):
        return pl.BlockSpec((1, w), lambda i: (0, 0))

    return pl.pallas_call(
        body, name="latent_bwd", grid=(t // tr,),
        in_specs=[row(Q_LORA), row(KV_LORA), row(1024), row(128), vec(Q_LORA), vec(KV_LORA)],
        out_specs=[row(1024), vec(Q_LORA), vec(KV_LORA)],
        out_shape=[jax.ShapeDtypeStruct((t, 1024), BF16), jax.ShapeDtypeStruct((1, Q_LORA), F32),
                   jax.ShapeDtypeStruct((1, KV_LORA), F32)],
        compiler_params=_params(("arbitrary",)),
    )(dcqn, dckvn, proj, dkr, gq, gkv)


def _rope(y, cosf, sin_a, sin_b):
    return y * cosf + pltpu.roll(y, 112, 1) * sin_a + pltpu.roll(y, 16, 1) * sin_b


def _rope_t(d, cosf, sin_a, sin_b):
    return d * cosf + pltpu.roll(d * sin_a, 16, 1) + pltpu.roll(d * sin_b, 112, 1)


def _headprep_fwd(qraw, kvraw, proj, cosf, sin_a, sin_b, gqh, gkh):
    t = qraw.shape[0]
    tr = _row_tile(t, 256)

    def body(q_ref, kv_ref, kr_ref, c_ref, sa_ref, sb_ref, gq_ref, gk_ref, qh_ref, kh_ref, kvb_ref):
        cv, sa, sb = c_ref[...], sa_ref[...], sb_ref[...]
        kr = kr_ref[...].astype(F32)
        lane = lax.broadcasted_iota(jnp.int32, (tr, HEAD_PAD), 1)
        heads = [slice(h * HEAD_PAD, (h + 1) * HEAD_PAD) for h in range(N_HEADS)]
        xqs = [q_ref[:, cols] for cols in heads]
        kvs = [kv_ref[:, cols] for cols in heads]
        xks = [jnp.where(lane < MLA_NOPE, kvh, kr) for kvh in kvs]
        rqs = [_rms(x, MLA_QK) for x in xqs]
        rks = [_rms(x, MLA_QK) for x in xks]
        gq = gq_ref[...] * MLA_Q_SCALE
        yqs = [(x * r) * gq for x, r in zip(xqs, rqs)]
        yks = [(x * r) * gk_ref[...] for x, r in zip(xks, rks)]
        for cols, yq, yk, kvh in zip(heads, yqs, yks, kvs):
            qh_ref[:, cols] = _rope(yq, cv, sa, sb).astype(BF16)
            kh_ref[:, cols] = _rope(yk, cv, sa, sb).astype(BF16)
            kvb_ref[:, cols] = jnp.where(lane < MLA_NOPE, 1.0, kvh).astype(BF16)

    wide = pl.BlockSpec((tr, 1024), lambda i: (i, 0))
    lanes = pl.BlockSpec((tr, HEAD_PAD), lambda i: (i, 0))
    vec = pl.BlockSpec((1, HEAD_PAD), lambda i: (0, 0))
    return pl.pallas_call(
        body, name="headprep_fwd", grid=(t // tr,),
        in_specs=[wide, wide, pl.BlockSpec((tr, HEAD_PAD), lambda i: (i, P_KR // HEAD_PAD)), lanes, lanes, lanes, vec, vec],
        out_specs=[wide, wide, wide],
        out_shape=[jax.ShapeDtypeStruct((t, 1024), BF16)] * 3,
        compiler_params=_params(("parallel",)),
    )(qraw, kvraw, proj, cosf, sin_a, sin_b, gqh, gkh)


def _headprep_bwd(dqh, dkh, dvp, qraw, kvraw, proj, cosf, sin_a, sin_b, gqh, gkh):
    t = qraw.shape[0]
    tr = _row_tile(t, 256)

    def body(dq_ref, dk_ref, dv_ref, q_ref, kv_ref, kr_ref, c_ref, sa_ref, sb_ref, gq_ref, gk_ref,
             dqr_ref, dkvr_ref, dkr_ref, dgq_ref, dgk_ref):
        i = pl.program_id(0)
        cv, sa, sb = c_ref[...], sa_ref[...], sb_ref[...]
        kr = kr_ref[...].astype(F32)
        lane = lax.broadcasted_iota(jnp.int32, (tr, HEAD_PAD), 1)
        heads = [slice(h * HEAD_PAD, (h + 1) * HEAD_PAD) for h in range(N_HEADS)]
        xs = [q_ref[:, cols] for cols in heads] + [jnp.where(lane < MLA_NOPE, kv_ref[:, cols], kr) for cols in heads]
        gs = [gq_ref[...]] * N_HEADS + [gk_ref[...]] * N_HEADS
        dns = [_rope_t(ref[:, cols], cv, sa, sb) for ref in (dq_ref, dk_ref) for cols in heads]
        rs = [_rms(x, MLA_QK) for x in xs]
        ys = [x * r for x, r in zip(xs, rs)]
        dys = [dn * g for dn, g in zip(dns, gs)]
        means = [jnp.sum(dy * y, axis=-1, keepdims=True) * (1.0 / MLA_QK) for dy, y in zip(dys, ys)]
        dxs = [r * (dy - y * m) for r, dy, y, m in zip(rs, dys, ys, means)]
        parts = [jnp.sum(dn * y, axis=0, keepdims=True) for dn, y in zip(dns, ys)]
        dkr = jnp.zeros((tr, HEAD_PAD), F32)
        pq = jnp.zeros((1, HEAD_PAD), F32)
        pk = jnp.zeros((1, HEAD_PAD), F32)
        for h, cols in enumerate(heads):
            dqr_ref[:, cols] = dxs[h].astype(BF16)
            dxk = dxs[N_HEADS + h]
            dkvr_ref[:, cols] = jnp.where(lane < MLA_NOPE, dxk, dv_ref[:, cols]).astype(BF16)
            dkr = dkr + jnp.where(lane < MLA_NOPE, 0.0, dxk)
            pq = pq + parts[h]
            pk = pk + parts[N_HEADS + h]
        dkr_ref[...] = dkr

        @pl.when(i == 0)
        def _():
            dgq_ref[...] = pq
            dgk_ref[...] = pk

        @pl.when(i > 0)
        def _():
            dgq_ref[...] += pq
            dgk_ref[...] += pk

    wide = pl.BlockSpec((tr, 1024), lambda i: (i, 0))
    lanes = pl.BlockSpec((tr, HEAD_PAD), lambda i: (i, 0))
    vec = pl.BlockSpec((1, HEAD_PAD), lambda i: (0, 0))
    return pl.pallas_call(
        body, name="headprep_bwd", grid=(t // tr,),
        in_specs=[wide, wide, wide, wide, wide, pl.BlockSpec((tr, HEAD_PAD), lambda i: (i, P_KR // HEAD_PAD)),
                  lanes, lanes, lanes, vec, vec],
        out_specs=[wide, wide, lanes, vec, vec],
        out_shape=[jax.ShapeDtypeStruct((t, 1024), BF16), jax.ShapeDtypeStruct((t, 1024), BF16),
                   jax.ShapeDtypeStruct((t, HEAD_PAD), F32), jax.ShapeDtypeStruct((1, HEAD_PAD), F32),
                   jax.ShapeDtypeStruct((1, HEAD_PAD), F32)],
        compiler_params=_params(("arbitrary",)),
    )(dqh, dkh, dvp, qraw, kvraw, proj, cosf, sin_a, sin_b, gqh, gkh)


def _merge_fwd(o_mla, w_mla, o_sb, w_sb, proj):
    t = proj.shape[0]
    tr = _row_tile(t, 512)

    def body(om_ref, wm_ref, os_ref, ws_ref, gm_ref, gs_ref, o_ref, bm_ref, bs_ref):
        omv, osv = om_ref[...], os_ref[...]
        for cols in _chunks(D_MODEL):
            bm = _dot(omv, wm_ref[:, cols], NN_DIMS)
            bs = _dot(osv, ws_ref[:, cols], NN_DIMS)
            bm_ref[:, cols] = bm
            bs_ref[:, cols] = bs
            gm = _sigmoid(gm_ref[:, cols].astype(F32))
            gs = _sigmoid(gs_ref[:, cols].astype(F32))
            o_ref[:, cols] = (gm * bm + gs * bs).astype(BF16)

    row = pl.BlockSpec((tr, 1024), lambda i: (i, 0))
    f32 = jax.ShapeDtypeStruct((t, 1024), F32)
    return pl.pallas_call(
        body, name="merge_fwd", grid=(t // tr,),
        in_specs=[row, pl.BlockSpec(w_mla.shape, lambda i: (0, 0)),
                  pl.BlockSpec((tr, SB_WIDTH), lambda i: (i, 0)), pl.BlockSpec(w_sb.shape, lambda i: (0, 0)),
                  pl.BlockSpec((tr, 1024), lambda i: (i, P_GM // 1024)), pl.BlockSpec((tr, 1024), lambda i: (i, P_GS // 1024))],
        out_specs=[row, row, row], out_shape=[jax.ShapeDtypeStruct((t, 1024), BF16), f32, f32],
        compiler_params=_params(("parallel",)),
    )(o_mla, w_mla, o_sb, w_sb, proj, proj)


def _merge_bwd(dh, w_out, proj, bm, bs):
    t = proj.shape[0]
    tr = _row_tile(t, 512)

    def body(d_ref, w_ref, gm_ref, gs_ref, bm_ref, bs_ref, dbm_ref, dbs_ref, dg_ref):
        dhv = d_ref[...]
        for cols in _chunks(D_MODEL):
            dm = _dot(dhv, w_ref[cols, :], NT_DIMS)
            gm = _sigmoid(gm_ref[:, cols].astype(F32))
            gs = _sigmoid(gs_ref[:, cols].astype(F32))
            dbm_ref[:, cols] = (dm * gm).astype(BF16)
            dbs_ref[:, cols] = (dm * gs).astype(BF16)
            dg_ref[:, cols] = (dm * bm_ref[:, cols] * gm * (1.0 - gm)).astype(BF16)
            dg_ref[:, slice(D_MODEL + cols.start, D_MODEL + cols.stop)] = (dm * bs_ref[:, cols] * gs * (1.0 - gs)).astype(BF16)

    row = pl.BlockSpec((tr, 1024), lambda i: (i, 0))
    return pl.pallas_call(
        body, name="mix_out_dx", grid=(t // tr,),
        in_specs=[row, pl.BlockSpec((D_MODEL, D_MODEL), lambda i: (0, 0)),
                  pl.BlockSpec((tr, 1024), lambda i: (i, P_GM // 1024)),
                  pl.BlockSpec((tr, 1024), lambda i: (i, P_GS // 1024)), row, row],
        out_specs=[row, row, pl.BlockSpec((tr, 2048), lambda i: (i, 0))],
        out_shape=[jax.ShapeDtypeStruct((t, 1024), BF16), jax.ShapeDtypeStruct((t, 1024), BF16),
                   jax.ShapeDtypeStruct((t, 2048), BF16)],
        compiler_params=_params(("parallel",)),
    )(dh, w_out, proj, proj, bm, bs)


def _ple_loss(h3, g, w_gate, pb, w_proj, tgt):
    t = h3.shape[0]
    tr = _row_tile(t, 512)

    def body(h_ref, g_ref, wg_ref, p_ref, wp_ref, t_ref, n_ref, dh_ref, dz_ref, dp_ref, l_ref):
        i = pl.program_id(0)
        xv = h_ref[...]
        nv = ((xv * _rms(xv, D_MODEL)) * g_ref[...]).astype(BF16)
        n_ref[...] = nv
        pv = p_ref[...]
        part = jnp.zeros((1, 128), F32)
        for cols in _chunks(D_MODEL):
            pg = _sigmoid(_dot(nv, wg_ref[:, cols], NN_DIMS))
            ppv = _dot(pv, wp_ref[:, cols], NN_DIMS)
            diff = (h_ref[:, cols] + pg * ppv) - t_ref[:, cols]
            dh = diff * (1.0 / D_MODEL)
            dh_ref[:, cols] = dh
            dp_ref[:, cols] = (dh * pg).astype(BF16)
            dz_ref[:, cols] = (dh * ppv * pg * (1.0 - pg)).astype(BF16)
            sq = jnp.sum(diff * diff, axis=0, keepdims=True)
            for c in range(sq.shape[1] // 128):
                part = part + sq[:, c * 128:(c + 1) * 128]

        @pl.when(i == 0)
        def _():
            l_ref[...] = part

        @pl.when(i > 0)
        def _():
            l_ref[...] += part

    row = pl.BlockSpec((tr, 1024), lambda i: (i, 0))
    return pl.pallas_call(
        body, name="ple_loss", grid=(t // tr,),
        in_specs=[row, pl.BlockSpec((1, D_MODEL), lambda i: (0, 0)), pl.BlockSpec((D_MODEL, D_MODEL), lambda i: (0, 0)),
                  pl.BlockSpec((tr, PLE_DIM), lambda i: (i, 0)), pl.BlockSpec((PLE_DIM, D_MODEL), lambda i: (0, 0)), row],
        out_specs=[row, row, row, row, pl.BlockSpec((1, 128), lambda i: (0, 0))],
        out_shape=[jax.ShapeDtypeStruct((t, 1024), BF16), jax.ShapeDtypeStruct((t, 1024), F32),
                   jax.ShapeDtypeStruct((t, 1024), BF16), jax.ShapeDtypeStruct((t, 1024), BF16),
                   jax.ShapeDtypeStruct((1, 128), F32)],
        compiler_params=_params(("arbitrary",)),
    )(h3, g, w_gate, pb, w_proj, tgt)


ATT_BLOCK = 256
MLA_Q_SCALE = math.log2(math.e) / math.sqrt(MLA_QK)
MLA_Q_BLOCK = 512
MLA_FWD_COLS = 8
MLA_BWD_COLS = 4
SB_FWD_COLS = 4
SB_BWD_COLS = 2
SB_BLOCK = 256


def _split_bf16(x):
    hi = x.astype(BF16)
    return hi, (x - hi.astype(F32)).astype(BF16)


def _tri(kind, n):
    r = lax.broadcasted_iota(jnp.int32, (n, n), 0)
    c = lax.broadcasted_iota(jnp.int32, (n, n), 1)
    cond = {'gt': r > c, 'le': r <= c, 'lt': r < c}[kind]
    return jnp.where(cond, 1.0, 0.0).astype(BF16)


def _causal(strict, n=ATT_BLOCK):
    r = lax.broadcasted_iota(jnp.int32, (n, n), 0)
    c = lax.broadcasted_iota(jnp.int32, (n, n), 1)
    return (c < r) if strict else (c <= r)


def _below_diagonal(rows):
    r = lax.broadcasted_iota(jnp.int32, (rows, ATT_BLOCK), 0)
    c = lax.broadcasted_iota(jnp.int32, (rows, ATT_BLOCK), 1)
    return c <= r


def _lanes(c):
    return slice(c * HEAD_PAD, (c + 1) * HEAD_PAD)


def _row_block(j, n=ATT_BLOCK):
    return pl.ds(pl.multiple_of(j * n, n), n)


def _rows(ref, j, c, n=ATT_BLOCK):
    return ref[_row_block(j, n), _lanes(c)]


def _mla_fwd(qh, kh, kvb, cargo=None):
    t = qh.shape[0]
    bq = min(MLA_Q_BLOCK, t)
    per_q = bq // ATT_BLOCK
    ncol = MLA_FWD_COLS
    grid = (N_HEADS // ncol, t // bq)

    def body(*refs):
        steps = [pl.program_id(0), pl.program_id(1)]
        _with_cargo(cargo, refs, 3, 2, steps, grid, lambda own: work(steps[1], *own))

    def work(i, q_ref, k_ref, v_ref, o_ref, lse_ref):
        qs = [q_ref[:, _lanes(c)] for c in range(ncol)]

        def step(j, carry, top):
            cols = range(ncol)
            first = top or 0
            scores = [_dot(qs[c][first:], _rows(k_ref, j, c), NT_DIMS) for c in cols]
            ms, ps, alphas = [], [], []
            for c in cols:
                s = scores[c]
                if top is not None:
                    s = jnp.where(_below_diagonal(bq - first), s, -1e30)
                m_old = carry[c][0][first:]
                m_new = jnp.maximum(m_old, jnp.max(s, axis=-1, keepdims=True))
                ps.append(jnp.exp2(s - m_new).astype(BF16))
                alphas.append(jnp.exp2(m_old - m_new))
                ms.append(m_new)
            accs = [alphas[c] * carry[c][1][first:] + _dot(ps[c], _rows(v_ref, j, c), NN_DIMS) for c in cols]
            if first:
                ms = [jnp.concatenate([carry[c][0][:first], ms[c]], axis=0) for c in cols]
                accs = [jnp.concatenate([carry[c][1][:first], accs[c]], axis=0) for c in cols]
            return tuple(zip(ms, accs))

        init = tuple((jnp.full((bq, 1), -1e30, F32), jnp.zeros((bq, HEAD_PAD), F32)) for _ in range(ncol))
        carry = lax.fori_loop(0, i * per_q, lambda j, cr: step(j, cr, None), init)
        for d in range(per_q):
            carry = step(i * per_q + d, carry, d * ATT_BLOCK)
        for c, (m, acc) in enumerate(carry):
            l = acc[:, 0:1]
            o_ref[:, _lanes(c)] = (acc / l).astype(BF16)
            lse_ref[c] = m + jnp.log2(l)

    width = ncol * HEAD_PAD
    full = pl.BlockSpec((t, width), lambda h, i: (0, h))
    blk = pl.BlockSpec((bq, width), lambda h, i: (i, h))
    extra = cargo.specs() if cargo else []
    outs = pl.pallas_call(
        body, name="mla_fwd", grid=grid,
        in_specs=[blk, full, full] + extra,
        out_specs=[blk, pl.BlockSpec((ncol, bq, 1), lambda h, i: (h, i, 0))] + extra,
        out_shape=[jax.ShapeDtypeStruct((t, N_HEADS * HEAD_PAD), BF16), jax.ShapeDtypeStruct((N_HEADS, t, 1), F32)]
        + (cargo.out_shape() if cargo else []),
        scratch_shapes=cargo.scratch() if cargo else [],
        compiler_params=_params(("arbitrary", "arbitrary")),
    )(qh, kh, kvb, *(cargo.srcs if cargo else []))
    return (outs[0], outs[1], list(outs[2:])) if cargo else outs


def _mla_bwd(qh, kh, kvb, o, do, lse, cargo=None):
    t = qh.shape[0]
    bq = min(MLA_Q_BLOCK, t)
    per_q = bq // ATT_BLOCK
    ncol = MLA_BWD_COLS
    width = ncol * HEAD_PAD
    grid = (N_HEADS // ncol, t // bq)

    def body(*refs):
        steps = [pl.program_id(0), pl.program_id(1)]
        _with_cargo(cargo, refs, 6, 3, steps, grid, lambda own: work(steps[0], steps[1], *own))

    def work(h, i, q_ref, k_ref, v_ref, o_ref, do_ref, lse_ref, dq_ref, dk_hbm, dv_hbm, dk_ref, dv_ref, out_sems):

        @pl.when(i == 0)
        def _():
            dk_ref[...] = jnp.zeros_like(dk_ref)
            dv_ref[...] = jnp.zeros_like(dv_ref)

        qs = [q_ref[:, _lanes(c)] for c in range(ncol)]
        dos = [do_ref[:, _lanes(c)] for c in range(ncol)]
        deltas = [jnp.sum(dos[c].astype(F32) * o_ref[:, _lanes(c)].astype(F32), axis=-1, keepdims=True)
                  for c in range(ncol)]
        lses = [lse_ref[c] for c in range(ncol)]

        def step(j, dqs, top):
            cols = range(ncol)
            first = top or 0
            kbs = [_rows(k_ref, j, c) for c in cols]
            scores = [_dot(qs[c][first:], kbs[c], NT_DIMS) for c in cols]
            dps = [_dot(dos[c][first:], _rows(v_ref, j, c), NT_DIMS) for c in cols]
            pbs, dss = [], []
            for c in cols:
                p = jnp.exp2(scores[c] - lses[c][first:])
                if top is not None:
                    p = jnp.where(_below_diagonal(bq - first), p, 0.0)
                pbs.append(p.astype(BF16))
                dss.append((p * (dps[c] - deltas[c][first:])).astype(BF16))
            for c in cols:
                dv_ref[_row_block(j), _lanes(c)] += _dot(pbs[c], dos[c][first:], TN_DIMS)
                dk_ref[_row_block(j), _lanes(c)] += _dot(dss[c], qs[c][first:], TN_DIMS)
            new = [dqs[c][first:] + _dot(dss[c], kbs[c], NN_DIMS) for c in cols]
            if first:
                new = [jnp.concatenate([dqs[c][:first], new[c]], axis=0) for c in cols]
            return tuple(new)

        init = tuple(jnp.zeros((bq, HEAD_PAD), F32) for _ in range(ncol))
        dqs = lax.fori_loop(0, i * per_q, lambda j, cr: step(j, cr, None), init)
        for d in range(per_q):
            dqs = step(i * per_q + d, dqs, d * ATT_BLOCK)
        for c, dq in enumerate(dqs):
            dq_ref[:, _lanes(c)] = dq * (1.0 / math.sqrt(MLA_QK))

        @pl.when(i == grid[1] - 1)
        def _():
            dk_ref[...] = dk_ref[...] * math.log(2.0)
            cols = pl.ds(pl.multiple_of(h * width, width), width)
            out = [pltpu.make_async_copy(dk_ref, dk_hbm.at[:, cols], out_sems.at[0]),
                   pltpu.make_async_copy(dv_ref, dv_hbm.at[:, cols], out_sems.at[1])]
            for cp in out:
                cp.start()
            for cp in out:
                cp.wait()

    full = pl.BlockSpec((t, width), lambda h, i: (0, h))
    blk = pl.BlockSpec((bq, width), lambda h, i: (i, h))
    wide = jax.ShapeDtypeStruct((t, N_HEADS * HEAD_PAD), F32)
    extra = cargo.specs() if cargo else []
    outs = pl.pallas_call(
        body, name="mla_bwd", grid=grid,
        in_specs=[blk, full, full, blk, blk, pl.BlockSpec((ncol, bq, 1), lambda h, i: (h, i, 0))] + extra,
        out_specs=[blk, HBM_SPEC, HBM_SPEC] + extra,
        out_shape=[wide, wide, wide] + (cargo.out_shape() if cargo else []),
        scratch_shapes=[pltpu.VMEM((t, width), F32), pltpu.VMEM((t, width), F32), pltpu.SemaphoreType.DMA((2,))]
        + (cargo.scratch() if cargo else []),
        compiler_params=_params(("arbitrary", "arbitrary")),
    )(qh, kh, kvb, o, do, lse, *(cargo.srcs if cargo else []))
    return (outs[0], outs[1], outs[2], list(outs[3:])) if cargo else outs


def _head_only(x, lane, u):
    return jnp.where((lane >= u * SB_DIM) & (lane < (u + 1) * SB_DIM), x, jnp.zeros_like(x))


SB_DEAD = -104.0


def _log_sigmoids(z):
    e = jnp.exp(-jnp.abs(z))
    lg = jnp.log(1.0 + e)
    ls_pos = jnp.minimum(z, 0.0) - lg
    return ls_pos, ls_pos - z, e


def _sb_fwd(proj):
    t = proj.shape[0]
    bq, ncol = SB_BLOCK, SB_FWD_COLS
    nq = t // bq
    scale = 1.0 / math.sqrt(SB_DIM)
    pairs = SB_WIDTH // HEAD_PAD

    def body(q_ref, k_ref, v_ref, o_ref, r_ref, first_ref):
        g, i = pl.program_id(0), pl.program_id(1)
        lane = lax.broadcasted_iota(jnp.int32, (bq, HEAD_PAD), 1)
        upper = _tri('gt', bq)
        chains = [(c, u) for c in range(ncol) for u in range(2)]
        qms = [_head_only(q_ref[:, _lanes(c)], lane, u) * scale for c, u in chains]

        def step(j, carry, masked):
            ids = range(len(chains))
            zs = [_dot(qms[n], _rows(k_ref, j, chains[n][0], bq), NT_DIMS) for n in ids]
            pos, neg, parts = [], [], []
            for n in ids:
                ls_pos, ls_neg, _ = _log_sigmoids(zs[n])
                if masked:
                    ls_neg = jnp.where(_causal(True, bq), ls_neg, 0.0)
                pos.append(ls_pos)
                neg.append(ls_neg)
                parts.append(_split_bf16(ls_neg))
            suffix = [_dot(parts[n][0], upper, NN_DIMS) + _dot(parts[n][1], upper, NN_DIMS) for n in ids]
            weights = []
            for n in ids:
                a = jnp.exp(pos[n] + suffix[n] + carry[n][0])
                if masked:
                    a = jnp.where(_causal(True, bq), a, 0.0)
                weights.append(a.astype(BF16))
            return tuple((carry[n][0] + jnp.sum(neg[n], axis=-1, keepdims=True),
                          carry[n][1] + _dot(weights[n], _rows(v_ref, j, chains[n][0], bq), NN_DIMS)) for n in ids)

        init = tuple((jnp.zeros((bq, 1), F32), jnp.zeros((bq, HEAD_PAD), F32)) for _ in chains)
        carry = step(i, init, True)

        def more(state):
            s, cr = state
            live = cr[0][0]
            for n in range(1, len(chains)):
                live = jnp.maximum(live, cr[n][0])
            return jnp.logical_and(s < i, jnp.max(live) > SB_DEAD)

        walked, carry = lax.while_loop(more, lambda st: (st[0] + 1, step(i - 1 - st[0], st[1], False)),
                                       (jnp.int32(0), carry))
        first_ref[g * nq + i] = i - walked
        for n, (c, u) in enumerate(chains):
            r_ref[2 * c + u] = carry[n][0]
        for c in range(ncol):
            o_ref[:, _lanes(c)] = jnp.where(lane < SB_DIM, carry[2 * c][1], carry[2 * c + 1][1]).astype(BF16)

    width = ncol * HEAD_PAD

    def full(c0):
        return pl.BlockSpec((t, width), lambda g, i: (0, c0 // width + g))

    return pl.pallas_call(
        body, name="sb_fwd", grid=(pairs // ncol, t // bq),
        in_specs=[pl.BlockSpec((bq, width), lambda g, i: (i, P_SBQ // width + g)), full(P_SBK), full(P_SBV)],
        out_specs=[pl.BlockSpec((bq, width), lambda g, i: (i, g)),
                   pl.BlockSpec((2 * ncol, bq, 1), lambda g, i: (g, i, 0)),
                   pl.BlockSpec(memory_space=pltpu.SMEM)],
        out_shape=[jax.ShapeDtypeStruct((t, SB_WIDTH), BF16), jax.ShapeDtypeStruct((N_HEADS, t, 1), F32),
                   jax.ShapeDtypeStruct((pairs // ncol * nq,), jnp.int32)],
        compiler_params=_params(("arbitrary", "arbitrary")),
    )(proj, proj, proj)


def _sb_bwd(proj, do, rtot, first):
    t = proj.shape[0]
    bq, ncol = SB_BLOCK, SB_BWD_COLS
    nq = t // bq
    scale = 1.0 / math.sqrt(SB_DIM)
    pairs = SB_WIDTH // HEAD_PAD

    def body(first_ref, q_ref, k_ref, v_ref, do_ref, r_ref, dq_ref, dk_ref, dv_ref):
        g, i = pl.program_id(0), pl.program_id(1)

        @pl.when(i == 0)
        def _():
            dk_ref[...] = jnp.zeros_like(dk_ref)
            dv_ref[...] = jnp.zeros_like(dv_ref)

        lane = lax.broadcasted_iota(jnp.int32, (bq, HEAD_PAD), 1)
        incl = _tri('le', bq)
        excl = _tri('lt', bq)
        chains = [(c, u) for c in range(ncol) for u in range(2)]
        qms = [_head_only(q_ref[:, _lanes(c)], lane, u) * scale for c, u in chains]
        doms = [_head_only(do_ref[:, _lanes(c)], lane, u) for c, u in chains]
        rts = [r_ref[2 * c + u] for c, u in chains]

        def step(j, carry, masked):
            ids = range(len(chains))
            kbs = [_rows(k_ref, j, c, bq) for c in range(ncol)]
            zs =[_dot(qms[n], kbs[chains[n][0]], NT_DIMS) for n in ids]
            das = [_dot(doms[n], _rows(v_ref, j, chains[n][0], bq), NT_DIMS) for n in ids]
            pos, neg, sigs, parts = [], [], [], []
            for n in ids:
                ls_pos, ls_neg, e = _log_sigmoids(zs[n])
                if masked:
                    ls_neg = jnp.where(_causal(True, bq), ls_neg, 0.0)
                pos.append(ls_pos)
                neg.append(ls_neg)
                sigs.append(jnp.where(zs[n] >= 0.0, 1.0, e) * pl.reciprocal(1.0 + e, approx=True))
                parts.append(_split_bf16(ls_neg))
            prefix = [_dot(parts[n][0], incl, NN_DIMS) + _dot(parts[n][1], incl, NN_DIMS) for n in ids]
            evs, eparts, dvs = [], [], []
            for n in ids:
                a = jnp.exp(pos[n] + (rts[n] - (carry[n][0] + prefix[n])))
                if masked:
                    a = jnp.where(_causal(True, bq), a, 0.0)
                dvs.append(_dot(a.astype(BF16), doms[n], TN_DIMS))
                evs.append(a * das[n])
                eparts.append(evs[n].astype(BF16))
            before = [_dot(eparts[n], excl, NN_DIMS) for n in ids]
            out, dks = [], []
            for n in ids:
                dz = evs[n] - sigs[n] * (evs[n] + (carry[n][1] + before[n]))
                if masked:
                    dz = jnp.where(_causal(True, bq), dz, 0.0)
                dzb = dz.astype(BF16)
                dks.append(_dot(dzb, qms[n], TN_DIMS))
                out.append((carry[n][0] + jnp.sum(neg[n], axis=-1, keepdims=True),
                            carry[n][1] + jnp.sum(evs[n], axis=-1, keepdims=True),
                            carry[n][2] + _dot(dzb, kbs[chains[n][0]], NN_DIMS)))
            for c in range(ncol):
                dv_ref[_row_block(j, bq), _lanes(c)] += dvs[2 * c] + dvs[2 * c + 1]
                dk_ref[_row_block(j, bq), _lanes(c)] += dks[2 * c] + dks[2 * c + 1]
            return tuple(out)

        init = tuple((jnp.zeros((bq, 1), F32), jnp.zeros((bq, 1), F32), jnp.zeros((bq, HEAD_PAD), F32)) for _ in chains)
        start = first_ref[(g * ncol // SB_FWD_COLS) * nq + i]
        carry = lax.fori_loop(start, i, lambda j, cr: step(j, cr, False), init)
        carry = step(i, carry, True)
        for c in range(ncol):
            dq_ref[:, _lanes(c)] = jnp.where(lane < SB_DIM, carry[2 * c][2], carry[2 * c + 1][2]) * scale

    width = ncol * HEAD_PAD

    def full(c0):
        return pl.BlockSpec((t, width), lambda g, i, first: (0, c0 // width + g))

    blk = pl.BlockSpec((bq, width), lambda g, i, first: (i, g))
    acc = pl.BlockSpec((t, width), lambda g, i, first: (0, g))
    wide = jax.ShapeDtypeStruct((t, SB_WIDTH), F32)
    return pl.pallas_call(
        body, name="sb_bwd",
        grid_spec=pltpu.PrefetchScalarGridSpec(
            num_scalar_prefetch=1, grid=(pairs // ncol, nq),
            in_specs=[pl.BlockSpec((bq, width), lambda g, i, first: (i, P_SBQ // width + g)), full(P_SBK), full(P_SBV),
                      blk, pl.BlockSpec((2 * ncol, bq, 1), lambda g, i, first: (g, i, 0))],
            out_specs=[blk, acc, acc]),
        out_shape=[wide, wide, wide],
        compiler_params=_params(("arbitrary", "arbitrary")),
    )(first, proj, proj, proj, do, rtot)


def _cols_to_full(g):
    n, r, c = g.shape
    return jnp.transpose(g, (1, 0, 2)).reshape(r, n * c)


def _full_to_cols(w):
    r, c = w.shape
    return jnp.transpose(w.reshape(r, N_DEV, c // N_DEV), (1, 0, 2))


TRANSPOSED = ('ffn1_w_in', 'ffn2_w_in', 'w_in', 'w_q_up')


def _layout_weight(name, g):
    if name in ('ffn1_w_out', 'ffn2_w_out', 'w_out', 'w_ple_gate', 'ffn1_w_in', 'ffn2_w_in'):
        return g.reshape(g.shape[0] * g.shape[1], g.shape[2])
    if name == 'w_in':
        wt = g.reshape(IN_COLS, D_MODEL)
        z = lambda n: jnp.zeros((n, D_MODEL), BF16)
        return jnp.concatenate([wt[0:640], z(64), wt[640:672], z(32), z(256), wt[2208:4256], wt[672:2208]], axis=0)
    if name == 'w_q_up':
        return jnp.pad(g, ((0, 0), (0, HEAD_PAD - MLA_QK), (0, 0))).reshape(N_HEADS * HEAD_PAD, Q_LORA)
    if name == 'w_branch_mla':
        bm = _cols_to_full(g).reshape(N_HEADS, MLA_NOPE, D_MODEL)
        return jnp.pad(bm, ((0, 0), (HEAD_PAD - MLA_NOPE, 0), (0, 0))).reshape(N_HEADS * HEAD_PAD, D_MODEL)
    return _cols_to_full(g)


def _unlayout_grad(name, d):
    if name == 'w_in':
        d = jnp.concatenate([d[0:640], d[704:736], d[P_SBQ:PROJ_W], d[P_GM:P_SBQ]], axis=0)
    if name == 'w_q_up':
        return d.reshape(N_HEADS, HEAD_PAD, Q_LORA)[:, :MLA_QK, :]
    if name in ('ffn1_w_out', 'ffn2_w_out', 'w_out', 'w_ple_gate', 'ffn1_w_in', 'ffn2_w_in', 'w_in'):
        return d.reshape(N_DEV, d.shape[0] // N_DEV, d.shape[1])
    if name == 'w_branch_mla':
        d = d.reshape(N_HEADS, HEAD_PAD, D_MODEL)[:, HEAD_PAD - MLA_NOPE:, :].reshape(SB_WIDTH, D_MODEL)
    return _full_to_cols(d)


def _rope_tables(positions):
    half = MLA_ROPE // 2
    inv_freq = ROPE_BASE ** (-jnp.arange(0, MLA_ROPE, 2, dtype=F32) / MLA_ROPE)
    ang = positions.astype(F32)[:, None] * inv_freq
    cos, sin = jnp.cos(ang), jnp.sin(ang)
    t = positions.shape[0]
    ones = lambda n: jnp.ones((t, n), F32)
    zeros = lambda n: jnp.zeros((t, n), F32)
    cosf = jnp.concatenate([ones(MLA_NOPE), cos, cos, ones(HEAD_PAD - MLA_QK)], axis=1)
    sin_a = jnp.concatenate([zeros(MLA_NOPE), -sin, zeros(half), zeros(HEAD_PAD - MLA_QK)], axis=1)
    sin_b = jnp.concatenate([zeros(MLA_NOPE), zeros(half), sin, zeros(HEAD_PAD - MLA_QK)], axis=1)
    return cosf, sin_a, sin_b


def _local_step(x, p, positions, tgt, norms, plan):
    mm = _matmul
    cosf, sin_a, sin_b = _rope_tables(positions)
    pad_head = lambda g: jnp.pad(g, ((0, 0), (0, HEAD_PAD - MLA_QK)))
    gqh, gkh = pad_head(norms['q_head_norm']), pad_head(norms['k_head_norm'])
    pb = p.astype(BF16)
    w = dict(plan.first_weights())
    dw, dn = {}, {}

    def ride(host, call):
        cargo = plan.cargo(host, dw)
        res, lands = call(cargo), None
        if cargo is not None:
            *res, lands = res
            res = res[0] if len(res) == 1 else tuple(res)
        w.update(plan.landed(host, lands))
        return res

    def ffn_fwd(h, tag):
        n, a, b, act = ride(tag + "_in_fwd", lambda cargo: _swiglu_fwd(
            h, norms[tag + '_norm'], w[tag + '_w_in'], tag + "_in_fwd", cargo))
        out = ride(tag + "_out_fwd", lambda cargo: mm(
            act, w[tag + '_w_out'], mode='nn', out_dtype=F32, name=tag + "_out_fwd", res=h, alpha=0.5, cargo=cargo))
        return out, (n, a, b, act)

    h1, ffn1_saved = ffn_fwd(x, 'ffn1')
    u = _rmsnorm_fwd(h1, norms['mix_norm'], "mix_norm_fwd")
    proj = mm(u, w['w_in'], mode='nt', out_dtype=BF16, name="proj_fwd")
    cqn, ckvn = _latent_fwd(proj, norms['q_latent_norm'], norms['kv_latent_norm'])
    qraw = mm(cqn, w['w_q_up'], mode='nt', out_dtype=F32, name="q_up_fwd")
    kvraw = mm(ckvn, w['w_kv_up'], mode='nn', out_dtype=F32, name="kv_up_fwd")
    qh, kh, kvb = _headprep_fwd(qraw, kvraw, proj, cosf, sin_a, sin_b, gqh, gkh)
    o_mla, lse = ride("mla_fwd", lambda cargo: _mla_fwd(qh, kh, kvb, cargo))
    o_sb, rtot, sb_first = _sb_fwd(proj)
    merged, bm, bs = _merge_fwd(o_mla, w['w_branch_mla'], o_sb, w['w_branch_sb'], proj)
    h2 = mm(merged, w['w_out'], mode='nn', out_dtype=F32, name="mix_out_fwd", res=h1)
    h3, ffn2_saved = ffn_fwd(h2, 'ffn2')
    n3, dh4, dzg, dpp, loss_lanes = _ple_loss(h3, norms['ple_norm'], w['w_ple_gate'], pb, w['w_ple_proj'], tgt)

    dw['w_ple_gate'] = mm(n3, dzg, mode='tn', out_dtype=BF16, name="ple_gate_dw")
    dw['w_ple_proj'] = mm(pb, dpp, mode='tn', out_dtype=BF16, name="ple_proj_dw")
    dh3, dhb3, dn['ple_norm'] = _matmul_norm_bwd(
        dzg, w['w_ple_gate'], h3, norms['ple_norm'], dh4, mode='nt', name="ple_gate_dx", out_scale=0.5)

    def ffn_bwd(h, dh, dhb, saved, tag, out_scale):
        n, a, b, act = saved
        dw[tag + '_w_out'] = mm(act, dhb, mode='tn', out_dtype=BF16, name=tag + "_out_dw", tm=1408)
        dab = _swiglu_bwd(dhb, w[tag + '_w_out'], a, b, tag + "_out_dx")
        dw[tag + '_w_in'] = ride(tag + "_in_dw", lambda cargo: mm(
            dab, n, mode='tn', out_dtype=BF16, name=tag + "_in_dw", tm=1408, cargo=cargo))
        dh_prev, dhb_prev, dn[tag + '_norm'] = ride(tag + "_in_dx", lambda cargo: _matmul_norm_bwd(
            dab, w[tag + '_w_in'], h, norms[tag + '_norm'], dh, mode='nn', name=tag + "_in_dx", out_scale=out_scale,
            cargo=cargo))
        return dh_prev, dhb_prev

    dh2, dhb2 = ffn_bwd(h2, dh3, dhb3, ffn2_saved, 'ffn2', 1.0)
    dw['w_out'] = mm(merged, dhb2, mode='tn', out_dtype=BF16, name="mix_out_dw")
    dbm, dbs, dgates = _merge_bwd(dhb2, w['w_out'], proj, bm, bs)
    dw['w_branch_mla'] = mm(o_mla, dbm, mode='tn', out_dtype=BF16, name="branch_mla_dw")
    dw['w_branch_sb'] = mm(o_sb, dbs, mode='tn', out_dtype=BF16, name="branch_sb_dw")
    do_mla = mm(dbm, w['w_branch_mla'], mode='nt', out_dtype=BF16, name="branch_mla_dx")
    do_sb = mm(dbs, w['w_branch_sb'], mode='nt', out_dtype=BF16, name="branch_sb_dx")
    dqh, dkh, dvp = ride("mla_bwd", lambda cargo: _mla_bwd(qh, kh, kvb, o_mla, do_mla, lse, cargo))
    dsq, dsk, dsv = _sb_bwd(proj, do_sb, rtot, sb_first)
    dqraw, dkvraw, dkr, dgq, dgk = _headprep_bwd(dqh, dkh, dvp, qraw, kvraw, proj, cosf, sin_a, sin_b, gqh, gkh)
    dn['q_head_norm'], dn['k_head_norm'] = dgq[:, :MLA_QK], dgk[:, :MLA_QK]
    dw['w_q_up'] = mm(dqraw, cqn, mode='tn', out_dtype=BF16, name="q_up_dw")
    dw['w_kv_up'] = mm(ckvn, dkvraw, mode='tn', out_dtype=BF16, name="kv_up_dw")
    dcqn = mm(dqraw, w['w_q_up'], mode='nn', out_dtype=F32, name="q_up_dx")
    dckvn = mm(dkvraw, w['w_kv_up'], mode='nt', out_dtype=F32, name="kv_up_dx")
    dlat, dn['q_latent_norm'], dn['kv_latent_norm'] = _latent_bwd(
        dcqn, dckvn, proj, dkr, norms['q_latent_norm'], norms['kv_latent_norm'])
    dproj = jnp.concatenate([dlat, dgates, dsq.astype(BF16), dsk.astype(BF16), dsv.astype(BF16)], axis=1)
    dw['w_in'] = ride("proj_dw", lambda cargo: mm(dproj, u, mode='tn', out_dtype=BF16, name="proj_dw", tm=1536, cargo=cargo))
    dh1, dhb1, dn['mix_norm'] = ride("proj_dx", lambda cargo: _matmul_norm_bwd(
        dproj, w['w_in'], h1, norms['mix_norm'], dh2, mode='nn', name="proj_dx", out_scale=0.5, cargo=cargo))
    dx, _ = ffn_bwd(x, dh1, dhb1, ffn1_saved, 'ffn1', 1.0)
    return dx, loss_lanes, dw, dn


MESH = pl.DeviceIdType.MESH
HBM_SPEC = pl.BlockSpec(memory_space=pl.ANY)


def _position():
    return lax.axis_index("x"), lax.axis_index("y"), lax.axis_index("c")


def _index(px, py, pc):
    return 4 * px + 2 * py + pc


def _all_gather(shards):
    n = len(shards)

    def body(*refs):
        ins, outs = refs[:n], refs[n:2 * n]
        send_sems, recv_sems, local_sems = refs[2 * n:]
        x, y, c = _position()
        me, sibling = (x, y, c), (x, y, 1 - c)
        chips = [(1 - x, y), (x, 1 - y), (1 - x, 1 - y)]

        def copy(a, k, block, to, own=False):
            dst = outs[a].at[_index(*block)]
            return pltpu.make_async_remote_copy(
                src_ref=ins[a] if own else dst, dst_ref=dst,
                send_sem=send_sems.at[a, k], recv_sem=recv_sems.at[a, k], device_id=to, device_id_type=MESH)

        mine = [pltpu.make_async_copy(ins[a], outs[a].at[_index(*me)], local_sems.at[a]) for a in range(n)]
        for cp in mine:
            cp.start()
        first = []
        for a in range(n):
            first.append(copy(a, 0, me, sibling, own=True))
            first += [copy(a, 1 + j, me, (*chip, c), own=True) for j, chip in enumerate(chips)]
        for cp in first:
            cp.start()
        passed = []
        for j, chip in enumerate(chips):
            for a in range(n):
                copy(a, 1 + j, (*chip, c), me).wait_recv()
                fwd = copy(a, 4 + j, (*chip, c), sibling)
                fwd.start()
                passed.append(fwd)
        for a in range(n):
            copy(a, 0, sibling, me).wait_recv()
            for j, chip in enumerate(chips):
                copy(a, 4 + j, (*chip, 1 - c), me).wait_recv()
        for cp in first + passed:
            cp.wait_send()
        for cp in mine:
            cp.wait()

    return pl.pallas_call(
        body, name="weights_all_gather",
        in_specs=[HBM_SPEC] * n, out_specs=[HBM_SPEC] * n,
        out_shape=[jax.ShapeDtypeStruct((N_DEV,) + s.shape, s.dtype) for s in shards],
        scratch_shapes=[pltpu.SemaphoreType.DMA((n, 7)), pltpu.SemaphoreType.DMA((n, 7)), pltpu.SemaphoreType.DMA((n,))],
    )(*shards)


def _exchange(parts):
    n = len(parts)
    masks = [(mx, my, mc) for mx in (0, 1) for my in (0, 1) for mc in (0, 1)][1:]

    def body(*refs):
        ins, outs = refs[:n], refs[n:2 * n]
        send_sems, recv_sems, local_sems = refs[2 * n:]
        x, y, c = _position()
        me = _index(x, y, c)

        def peer_of(mask):
            mx, my, mc = mask
            return (x + mx - 2 * x * mx, y + my - 2 * y * my, c + mc - 2 * c * mc)

        def copy(a, k):
            peer = peer_of(masks[k])
            return pltpu.make_async_remote_copy(
                src_ref=ins[a].at[_index(*peer)], dst_ref=outs[a].at[me],
                send_sem=send_sems.at[a, k], recv_sem=recv_sems.at[a, k], device_id=peer, device_id_type=MESH)

        def landed(a, k):
            peer = peer_of(masks[k])
            return pltpu.make_async_remote_copy(
                src_ref=ins[a].at[me], dst_ref=outs[a].at[_index(*peer)],
                send_sem=send_sems.at[a, k], recv_sem=recv_sems.at[a, k], device_id=peer, device_id_type=MESH)

        mine = [pltpu.make_async_copy(ins[a].at[me], outs[a].at[me], local_sems.at[a]) for a in range(n)]
        for cp in mine:
            cp.start()
        sent = [copy(a, k) for k in range(7) for a in range(n)]
        for cp in sent:
            cp.start()
        for k in range(7):
            for a in range(n):
                landed(a, k).wait_recv()
        for cp in sent:
            cp.wait_send()
        for cp in mine:
            cp.wait()

    return pl.pallas_call(
        body, name="grads_exchange",
        in_specs=[HBM_SPEC] * n, out_specs=[HBM_SPEC] * n,
        out_shape=[jax.ShapeDtypeStruct(s.shape, s.dtype) for s in parts],
        scratch_shapes=[pltpu.SemaphoreType.DMA((n, 7)), pltpu.SemaphoreType.DMA((n, 7)), pltpu.SemaphoreType.DMA((n,))],
    )(*parts)


PEER_MASKS = [(mx, my, mc) for mx in (0, 1) for my in (0, 1) for mc in (0, 1)][1:]


def _peer(mask):
    x, y, c = _position()
    mx, my, mc = mask
    return (x + mx - 2 * x * mx, y + my - 2 * y * my, c + mc - 2 * c * mc)


class _Cargo:
    def __init__(self, srcs, scatter):
        self.srcs, self.scatter, self.n = list(srcs), scatter, len(srcs)

    def specs(self):
        return [HBM_SPEC] * self.n

    def out_shape(self):
        return [jax.ShapeDtypeStruct(s.shape if self.scatter else (N_DEV,) + s.shape, s.dtype) for s in self.srcs]

    def scratch(self):
        per_copy = pltpu.SemaphoreType.DMA((self.n, len(PEER_MASKS)))
        return [per_copy, per_copy, pltpu.SemaphoreType.DMA((self.n,))]

    def _mine(self, src_refs, a, to):
        return src_refs[a].at[to] if self.scatter else src_refs[a]

    def _shard_copy(self, src_refs, land_refs, sems, a, k, block, to, own=False):
        dst = land_refs[a].at[_index(*block)]
        return pltpu.make_async_remote_copy(
            src_ref=src_refs[a] if own else dst, dst_ref=dst,
            send_sem=sems[0].at[a, k], recv_sem=sems[1].at[a, k], device_id=to, device_id_type=MESH)

    def _first_hops(self, src_refs, land_refs, sems):
        x, y, c = _position()
        chips = [(1 - x, y), (x, 1 - y), (1 - x, 1 - y)]
        hops = []
        for a in range(self.n):
            hops.append(self._shard_copy(src_refs, land_refs, sems, a, 0, (x, y, c), (x, y, 1 - c), own=True))
            hops += [self._shard_copy(src_refs, land_refs, sems, a, 1 + j, (x, y, c), (*chip, c), own=True)
                     for j, chip in enumerate(chips)]
        return hops, chips

    def start(self, src_refs, land_refs, sems):
        send, recv, local = sems
        me = _index(*_position())
        for a in range(self.n):
            pltpu.make_async_copy(self._mine(src_refs, a, me), land_refs[a].at[me], local.at[a]).start()
        if not self.scatter:
            for cp in self._first_hops(src_refs, land_refs, sems)[0]:
                cp.start()
            return
        for k, mask in enumerate(PEER_MASKS):
            peer = _peer(mask)
            for a in range(self.n):
                pltpu.make_async_remote_copy(
                    src_ref=self._mine(src_refs, a, _index(*peer)), dst_ref=land_refs[a].at[me],
                    send_sem=send.at[a, k], recv_sem=recv.at[a, k], device_id=peer, device_id_type=MESH).start()

    def _wait_gathered(self, src_refs, land_refs, sems):
        x, y, c = _position()
        me, sibling = (x, y, c), (x, y, 1 - c)
        first, chips = self._first_hops(src_refs, land_refs, sems)
        passed = []
        for j, chip in enumerate(chips):
            for a in range(self.n):
                self._shard_copy(src_refs, land_refs, sems, a, 1 + j, (*chip, c), me).wait_recv()
                passed.append(self._shard_copy(src_refs, land_refs, sems, a, 4 + j, (*chip, c), sibling))
                passed[-1].start()
        for a in range(self.n):
            self._shard_copy(src_refs, land_refs, sems, a, 0, sibling, me).wait_recv()
            for j, chip in enumerate(chips):
                self._shard_copy(src_refs, land_refs, sems, a, 4 + j, (*chip, 1 - c), me).wait_recv()
        for cp in first + passed:
            cp.wait_send()

    def wait(self, src_refs, land_refs, sems):
        send, recv, local = sems
        me = _index(*_position())
        if not self.scatter:
            self._wait_gathered(src_refs, land_refs, sems)
        for k, mask in enumerate(PEER_MASKS if self.scatter else []):
            peer = _peer(mask)
            there = _index(*peer)
            for a in range(self.n):
                pltpu.make_async_remote_copy(
                    src_ref=self._mine(src_refs, a, me), dst_ref=land_refs[a].at[there],
                    send_sem=send.at[a, k], recv_sem=recv.at[a, k], device_id=peer, device_id_type=MESH).wait_recv()
                pltpu.make_async_remote_copy(
                    src_ref=self._mine(src_refs, a, there), dst_ref=land_refs[a].at[me],
                    send_sem=send.at[a, k], recv_sem=recv.at[a, k], device_id=peer, device_id_type=MESH).wait_send()
        for a in range(self.n):
            pltpu.make_async_copy(self._mine(src_refs, a, me), land_refs[a].at[me], local.at[a]).wait()


def _with_cargo(cargo, refs, n_in, n_out, steps, counts, compute):
    if cargo is None:
        compute(refs)
        return
    n = cargo.n
    src_refs = refs[n_in:n_in + n]
    land_refs = refs[n_in + n + n_out:n_in + 2 * n + n_out]
    sems = refs[-3:]
    first = functools.reduce(jnp.logical_and, [s == 0 for s in steps])
    last = functools.reduce(jnp.logical_and, [s == c - 1 for s, c in zip(steps, counts)])

    @pl.when(first)
    def _():
        cargo.start(src_refs, land_refs, sems)

    compute(refs[:n_in] + refs[n_in + n:n_in + n + n_out] + refs[n_in + 2 * n + n_out:-3])

    @pl.when(last)
    def _():
        cargo.wait(src_refs, land_refs, sems)


def _adamw(parts, w, m, v, name):
    r, c = w.shape
    tr = next((t for t in (512, 384, 352, 256, 128) if r % t == 0), r) if r > 512 else r
    tc = c if tr < r or r <= 512 else 256
    assert r % tr == 0 and c % tc == 0
    bc1 = 1.0 - ADAM_B1 ** ADAM_STEP
    bc2 = 1.0 - ADAM_B2 ** ADAM_STEP

    def body(p_ref, w_ref, m_ref, v_ref, g_ref, d_ref, nm_ref, nv_ref):
        g = p_ref[0].astype(F32)
        for s in range(1, N_DEV):
            g = g + p_ref[s].astype(F32)
        nm = ADAM_B1 * m_ref[...] + (1.0 - ADAM_B1) * g
        nv = ADAM_B2 * v_ref[...] + (1.0 - ADAM_B2) * (g * g)
        g_ref[...] = g
        nm_ref[...] = nm
        nv_ref[...] = nv
        d_ref[...] = -ADAM_LR * ((nm / bc1) / (jnp.sqrt(nv / bc2) + ADAM_EPS) + ADAM_WD * w_ref[...])

    tile = pl.BlockSpec((tr, tc), lambda i, j: (i, j))
    out = jax.ShapeDtypeStruct((r, c), F32)
    return pl.pallas_call(
        body, name=name, grid=(r // tr, c // tc),
        in_specs=[pl.BlockSpec((N_DEV, tr, tc), lambda i, j: (0, i, j)), tile, tile, tile],
        out_specs=[tile] * 4, out_shape=[out] * 4,
        compiler_params=_params(("parallel", "parallel")),
    )(parts, w, m, v)


GATHER_FIRST = ['ffn1_w_in']
RIDES = {
    'ffn1_in_fwd': ('weights', ['ffn1_w_out', 'w_in']),
    'ffn1_out_fwd': ('weights', ['w_q_up', 'w_kv_up', 'w_branch_mla', 'w_branch_sb', 'w_out']),
    'mla_fwd': ('weights', ['ffn2_w_in', 'ffn2_w_out', 'w_ple_gate', 'w_ple_proj']),
    'mla_bwd': ('grads', ['w_ple_gate', 'w_ple_proj', 'ffn2_w_out', 'ffn2_w_in', 'w_out', 'w_branch_mla', 'w_branch_sb']),
    'proj_dw': ('grads', ['w_q_up', 'w_kv_up']),
    'proj_dx': ('grads', ['w_in']),
    'ffn1_in_dw': ('grads', ['ffn1_w_out']),
    'ffn1_in_dx': ('grads', ['ffn1_w_in']),
}


class _Plan:
    def __init__(self, shards):
        self.shards = shards
        self.received = {}

    def first_weights(self):
        gathered = _all_gather([self.shards[n] for n in GATHER_FIRST])
        return {n: _layout_weight(n, g) for n, g in zip(GATHER_FIRST, gathered)}

    def cargo(self, host, dw):
        if host not in RIDES:
            return None
        kind, names = RIDES[host]
        if kind == 'weights':
            return _Cargo([self.shards[n] for n in names], False)
        return _Cargo([_unlayout_grad(n, dw.pop(n)) for n in names], True)

    def landed(self, host, lands):
        if host not in RIDES:
            return {}
        kind, names = RIDES[host]
        if kind == 'weights':
            return {n: _layout_weight(n, land) for n, land in zip(names, lands)}
        self.received.update(zip(names, lands))
        return {}


def _pack_small(vecs):
    flat = jnp.concatenate([v.reshape(-1) for v in vecs])
    return jnp.pad(flat, (0, SMALL_ROWS * 128 - flat.shape[0])).reshape(SMALL_ROWS, 128)


def _unpack_small(packed, sizes):
    flat = packed.reshape(-1)
    out, at = [], 0
    for n in sizes:
        out.append(flat[at:at + n])
        at += n
    return out


def kernel(x, p, positions, ffn1_norm, ffn1_w_in, ffn1_w_out, mix_norm, w_in, q_latent_norm, w_q_up, kv_latent_norm, w_kv_up, q_head_norm, k_head_norm, w_branch_mla, w_branch_sb, w_out, ffn2_norm, ffn2_w_in, ffn2_w_out, ple_norm, w_ple_gate, w_ple_proj, loss_target, m_ffn1_norm, m_ffn1_w_in, m_ffn1_w_out, m_mix_norm, m_w_in, m_q_latent_norm, m_w_q_up, m_kv_latent_norm, m_w_kv_up, m_q_head_norm, m_k_head_norm, m_w_branch_mla, m_w_branch_sb, m_w_out, m_ffn2_norm, m_ffn2_w_in, m_ffn2_w_out, m_ple_norm, m_w_ple_gate, m_w_ple_proj, v_ffn1_norm, v_ffn1_w_in, v_ffn1_w_out, v_mix_norm, v_w_in, v_q_latent_norm, v_w_q_up, v_kv_latent_norm, v_w_kv_up, v_q_head_norm, v_k_head_norm, v_w_branch_mla, v_w_branch_sb, v_w_out, v_ffn2_norm, v_ffn2_w_in, v_ffn2_w_out, v_ple_norm, v_w_ple_gate, v_w_ple_proj):
    given = dict(locals())
    wts = {n: given[n] for n in WEIGHTS}
    mom = {n: given['m_' + n] for n in WEIGHTS}
    var = {n: given['v_' + n] for n in WEIGHTS}

    def local(a, n):
        return jnp.swapaxes(a[0], 0, 1) if n in TRANSPOSED else a[0]

    plan = _Plan({n: local(wts[n], n).astype(BF16) for n in MATS})
    norms = {n: wts[n] for n in NORMS}
    dx, loss_lanes, dw, dn = _local_step(x[0], p[0, 0], positions[0], loss_target[0], norms, plan)
    assert not dw

    out = {}
    for n in MATS:
        res = _adamw(plan.received[n], local(wts[n], n), local(mom[n], n), local(var[n], n), "adamw_" + n)
        out[n] = [local(r[None], n)[None] for r in res]
    small = _pack_small([dn[n] for n in NORMS] + [0.5 / D_MODEL * jnp.sum(loss_lanes)[None]])
    small_parts = _exchange([jnp.broadcast_to(small[None], (N_DEV, SMALL_ROWS, 128))])[0]
    sizes = [wts[n].shape[1] for n in NORMS]
    pack = lambda d: _pack_small([d[n] for n in NORMS])
    small_res = _adamw(small_parts, pack(wts), pack(mom), pack(var), "adamw_norms")
    loss = small_res[0].reshape(-1)[sum(sizes)]
    for i, res in enumerate(small_res):
        for n, vec in zip(NORMS, _unpack_small(res, sizes)):
            out.setdefault(n, [None] * 4)[i] = vec[None]

    return (loss, dx[None], *[out[n][0] for n in WEIGHTS], *[out[n][1] for n in WEIGHTS],
            *[out[n][2] for n in WEIGHTS], *[out[n][3] for n in WEIGHTS])
```

```python
import functools
import math

import jax
import jax.numpy as jnp
from jax import lax
from jax.experimental import pallas as pl
from jax.experimental.pallas import tpu as pltpu

F32 = jnp.float32
BF16 = jnp.bfloat16

N_DEV = 8
D_MODEL = 1024
D_FF = 2816
PLE_DIM = 256
NORM_EPS = 1e-6
N_HEADS = 8
HEAD_PAD = 128
MLA_NOPE = 64
MLA_ROPE = 32
MLA_QK = 96
Q_LORA = 384
KV_LORA = 256
SB_DIM = 64
SB_WIDTH = 512
ROPE_BASE = 10000.0
IN_COLS = 4256

PROJ_W = 4608
P_CQ, P_CKV, P_KR, P_GM, P_GS, P_SBQ, P_SBK, P_SBV = 0, 384, 640, 1024, 2048, 3072, 3584, 4096

ADAM_LR, ADAM_B1, ADAM_B2, ADAM_EPS, ADAM_WD, ADAM_STEP = 0.001, 0.9, 0.999, 1e-08, 0.01, 10

VMEM_LIMIT = 52 * 1024 * 1024
MATMUL_VMEM = 40 * 1024 * 1024

WEIGHTS = ['ffn1_norm', 'ffn1_w_in', 'ffn1_w_out', 'mix_norm', 'w_in', 'q_latent_norm', 'w_q_up',
           'kv_latent_norm', 'w_kv_up', 'q_head_norm', 'k_head_norm', 'w_branch_mla', 'w_branch_sb',
           'w_out', 'ffn2_norm', 'ffn2_w_in', 'ffn2_w_out', 'ple_norm', 'w_ple_gate', 'w_ple_proj']
NORMS = ['ffn1_norm', 'mix_norm', 'q_latent_norm', 'kv_latent_norm', 'q_head_norm', 'k_head_norm',
         'ffn2_norm', 'ple_norm']
MATS = [n for n in WEIGHTS if n not in NORMS]
SMALL_ROWS = 48

NT_DIMS = (((1,), (1,)), ((), ()))
NN_DIMS = (((1,), (0,)), ((), ()))
TN_DIMS = (((0,), (0,)), ((), ()))


def _params(sem=None, vmem=VMEM_LIMIT):
    return pltpu.CompilerParams(dimension_semantics=sem, vmem_limit_bytes=vmem)


def _pick(n, cap):
    if n <= cap:
        return n
    best = None
    for t in range(128, cap + 1, 128):
        if n % t == 0:
            best = t
    assert best is not None, (n, cap)
    return best


def _dot(a, b, dims):
    return lax.dot_general(a, b, dims, preferred_element_type=F32)


def _matmul(a, b, *, mode, out_dtype, name, tm=None, tn=None, tk=None, res=None, alpha=1.0, cargo=None):
    if mode == 'nn':
        (m, k), (k2, n) = a.shape, b.shape
    elif mode == 'nt':
        (m, k), (n, k2) = a.shape, b.shape
    else:
        (k, m), (k2, n) = a.shape, b.shape
    assert k == k2, (name, a.shape, b.shape)
    has_res = res is not None
    tn = tn or _pick(n, 512)

    def vmem(tm_, tk_):
        io = 2 * 2 * (tm_ * tk_ + tk_ * tn) + 2 * tm_ * tn * (jnp.dtype(out_dtype).itemsize + 4 * has_res)
        return io + (4 * tm_ * tn if tk_ < k else 0)

    tries = [(tm_, tk_) for tk_ in ([tk] if tk else [k, _pick(k, 2048)])
             for tm_ in ([tm] if tm else [_pick(m, 2048), _pick(m, 1024), _pick(m, 512)])]
    tm, tk = next((c for c in tries if vmem(*c) <= MATMUL_VMEM), tries[-1])
    assert m % tm == 0 and n % tn == 0 and k % tk == 0, (name, m, n, k, tm, tn, tk)
    nk = k // tk
    dims = {'nn': NN_DIMS, 'nt': NT_DIMS, 'tn': TN_DIMS}[mode]

    def epilogue(acc, r_ref, o_ref):
        if alpha != 1.0:
            acc = acc * alpha
        if has_res:
            acc = r_ref[...] + acc
        o_ref[...] = acc.astype(out_dtype)

    grid = (m // tm, n // tn, nk)

    def body(*refs):
        steps = [pl.program_id(d) for d in range(3)]

        def compute(own):
            a_ref, b_ref = own[0], own[1]
            r_ref = own[2] if has_res else None
            o_ref = own[2 + has_res]
            if nk == 1:
                epilogue(_dot(a_ref[...], b_ref[...], dims), r_ref, o_ref)
                return
            acc_ref = own[-1]

            @pl.when(steps[2] == 0)
            def _():
                acc_ref[...] = jnp.zeros_like(acc_ref)

            acc_ref[...] += _dot(a_ref[...], b_ref[...], dims)

            @pl.when(steps[2] == nk - 1)
            def _():
                epilogue(acc_ref[...], r_ref, o_ref)

        _with_cargo(cargo, refs, 2 + has_res, 1, steps, grid, compute)

    if mode == 'tn':
        a_spec = pl.BlockSpec((tk, tm), lambda i, j, kk: (kk, i))
    else:
        a_spec = pl.BlockSpec((tm, tk), lambda i, j, kk: (i, kk))
    if mode == 'nt':
        b_spec = pl.BlockSpec((tn, tk), lambda i, j, kk: (j, kk))
    else:
        b_spec = pl.BlockSpec((tk, tn), lambda i, j, kk: (kk, j))
    o_spec = pl.BlockSpec((tm, tn), lambda i, j, kk: (i, j))
    in_specs = [a_spec, b_spec] + ([o_spec] if has_res else [])
    args = (a, b) + ((res,) if has_res else ())
    out_shape = jax.ShapeDtypeStruct((m, n), out_dtype)
    scratch = [pltpu.VMEM((tm, tn), F32)] if nk > 1 else []
    if cargo is None:
        return pl.pallas_call(
            body, name=name, grid=grid, in_specs=in_specs, out_specs=o_spec, out_shape=out_shape,
            scratch_shapes=scratch, compiler_params=_params(("parallel", "parallel", "arbitrary")),
        )(*args)
    outs = pl.pallas_call(
        body, name=name, grid=grid, in_specs=in_specs + cargo.specs(), out_specs=[o_spec] + cargo.specs(),
        out_shape=[out_shape] + cargo.out_shape(), scratch_shapes=scratch + cargo.scratch(),
        compiler_params=_params(("arbitrary", "arbitrary", "arbitrary")),
    )(*args, *cargo.srcs)
    return outs[0], list(outs[1:])


def _row_tile(t, cap=512):
    return min(t, cap)


def _rms(x, width):
    return lax.rsqrt(jnp.sum(x * x, axis=-1, keepdims=True) * (1.0 / width) + NORM_EPS)


def _rmsnorm_fwd(x, g, name):
    t, d = x.shape
    tr = _row_tile(t)

    def body(x_ref, g_ref, o_ref):
        xv = x_ref[...]
        o_ref[...] = ((xv * _rms(xv, d)) * g_ref[...]).astype(BF16)

    return pl.pallas_call(
        body, name=name, grid=(t // tr,),
        in_specs=[pl.BlockSpec((tr, d), lambda i: (i, 0)), pl.BlockSpec((1, d), lambda i: (0, 0))],
        out_specs=pl.BlockSpec((tr, d), lambda i: (i, 0)),
        out_shape=jax.ShapeDtypeStruct((t, d), BF16),
        compiler_params=_params(("parallel",)),
    )(x, g)


def _matmul_norm_bwd(a, b, x, g, dh_in, *, mode, name, out_scale, cargo=None):
    m, k = a.shape
    d = x.shape[1]
    tn = _pick(d, 512)

    def vmem(tm_):
        return 2 * 2 * (tm_ * k + k * tn) + tm_ * d * (4 + 2 * (4 + 4) + 2 * (4 + 2))

    tm = next((c for c in (_pick(m, 1024), _pick(m, 512), _pick(m, 256)) if vmem(c) <= MATMUL_VMEM), _pick(m, 256))
    grid = (m // tm, d // tn)
    dims = {'nn': NN_DIMS, 'nt': NT_DIMS}[mode]

    def body(*refs):
        steps = [pl.program_id(0), pl.program_id(1)]

        def compute(own):
            a_ref, b_ref, x_ref, g_ref, dhin_ref, dh_ref, dhb_ref, dg_ref, dn_ref = own
            for jj in range(grid[1]):
                @pl.when(steps[1] == jj)
                def _(jj=jj):
                    dn_ref[:, jj * tn:(jj + 1) * tn] = _dot(a_ref[...], b_ref[...], dims)

            @pl.when(steps[1] == grid[1] - 1)
            def _():
                xv = x_ref[...]
                dnv = dn_ref[...]
                r = _rms(xv, d)
                y = xv * r
                dy = dnv * g_ref[...]
                dh = dhin_ref[...] + r * (dy - y * (jnp.sum(dy * y, axis=-1, keepdims=True) * (1.0 / d)))
                dh_ref[...] = dh
                dhb_ref[...] = (dh * out_scale).astype(BF16)
                part = jnp.sum(dnv * y, axis=0, keepdims=True)

                @pl.when(steps[0] == 0)
                def _():
                    dg_ref[...] = part

                @pl.when(steps[0] > 0)
                def _():
                    dg_ref[...] += part

        _with_cargo(cargo, refs, 5, 3, steps, grid, compute)

    b_spec = pl.BlockSpec((k, tn), lambda i, j: (0, j)) if mode == 'nn' else pl.BlockSpec((tn, k), lambda i, j: (j, 0))
    row = pl.BlockSpec((tm, d), lambda i, j: (i, 0))
    vec = pl.BlockSpec((1, d), lambda i, j: (0, 0))
    extra = cargo.specs() if cargo else []
    outs = pl.pallas_call(
        body, name=name, grid=grid,
        in_specs=[pl.BlockSpec((tm, k), lambda i, j: (i, 0)), b_spec, row, vec, row] + extra,
        out_specs=[row, row, vec] + extra,
        out_shape=[jax.ShapeDtypeStruct((m, d), F32), jax.ShapeDtypeStruct((m, d), BF16),
                   jax.ShapeDtypeStruct((1, d), F32)] + (cargo.out_shape() if cargo else []),
        scratch_shapes=[pltpu.VMEM((tm, d), F32)] + (cargo.scratch() if cargo else []),
        compiler_params=_params(("arbitrary", "arbitrary")),
    )(a, b, x, g, dh_in, *(cargo.srcs if cargo else []))
    return (outs[0], outs[1], outs[2], list(outs[3:])) if cargo else outs


def _sigmoid(x):
    return 1.0 / (1.0 + jnp.exp(-x))


SWIGLU_CHUNK = 256
SWIGLU_COLS = 1408


def _chunks(width):
    return [slice(lo, min(lo + SWIGLU_CHUNK, width)) for lo in range(0, width, SWIGLU_CHUNK)]


def _swiglu_fwd(h, g, wt_in, name, cargo=None):
    t = h.shape[0]
    tm = _pick(t, 1024)
    grid = (t // tm, D_FF // SWIGLU_COLS)

    def body(*refs):
        steps = [pl.program_id(0), pl.program_id(1)]

        def compute(own):
            h_ref, g_ref, wa_ref, wb_ref, n_ref, a_ref, b_ref, act_ref, n_sc = own

            @pl.when(steps[1] == 0)
            def _():
                xv = h_ref[...]
                n_sc[...] = ((xv * _rms(xv, D_MODEL)) * g_ref[...]).astype(BF16)
                n_ref[...] = n_sc[...]

            nv = n_sc[...]
            for cols in _chunks(SWIGLU_COLS):
                a = _dot(nv, wa_ref[cols, :], NT_DIMS)
                b = _dot(nv, wb_ref[cols, :], NT_DIMS)
                a_ref[:, cols] = a.astype(BF16)
                b_ref[:, cols] = b.astype(BF16)
                act_ref[:, cols] = (a * _sigmoid(a) * b).astype(BF16)

        _with_cargo(cargo, refs, 4, 4, steps, grid, compute)

    half = D_FF // SWIGLU_COLS
    row = pl.BlockSpec((tm, D_MODEL), lambda i, j: (i, 0))
    tile = pl.BlockSpec((tm, SWIGLU_COLS), lambda i, j: (i, j))
    out = jax.ShapeDtypeStruct((t, D_FF), BF16)
    extra = cargo.specs() if cargo else []
    outs = pl.pallas_call(
        body, name=name, grid=grid,
        in_specs=[row, pl.BlockSpec((1, D_MODEL), lambda i, j: (0, 0)),
                  pl.BlockSpec((SWIGLU_COLS, D_MODEL), lambda i, j: (j, 0)),
                  pl.BlockSpec((SWIGLU_COLS, D_MODEL), lambda i, j: (half + j, 0))] + extra,
        out_specs=[row, tile, tile, tile] + extra,
        out_shape=[jax.ShapeDtypeStruct((t, D_MODEL), BF16), out, out, out] + (cargo.out_shape() if cargo else []),
        scratch_shapes=[pltpu.VMEM((tm, D_MODEL), BF16)] + (cargo.scratch() if cargo else []),
        compiler_params=_params(("arbitrary", "arbitrary")),
    )(h, g, wt_in, wt_in, *(cargo.srcs if cargo else []))
    return (outs[0], outs[1], outs[2], outs[3], list(outs[4:])) if cargo else outs


def _swiglu_bwd(dh, w_out, a, b, name):
    t = a.shape[0]
    tr = _row_tile(t, 512)

    def body(d_ref, w_ref, a_ref, b_ref, o_ref):
        dhv = d_ref[...]
        for cols in _chunks(D_FF):
            dv = _dot(dhv, w_ref[cols, :], NT_DIMS)
            av = a_ref[:, cols].astype(F32)
            s = _sigmoid(av)
            o_ref[:, cols] = (dv * b_ref[:, cols].astype(F32) * s * (1.0 + av * (1.0 - s))).astype(BF16)
            o_ref[:, slice(D_FF + cols.start, D_FF + cols.stop)] = (dv * av * s).astype(BF16)

    row = pl.BlockSpec((tr, D_FF), lambda i: (i, 0))
    return pl.pallas_call(
        body, name=name, grid=(t // tr,),
        in_specs=[pl.BlockSpec((tr, D_MODEL), lambda i: (i, 0)), pl.BlockSpec((D_FF, D_MODEL), lambda i: (0, 0)), row, row],
        out_specs=pl.BlockSpec((tr, 2 * D_FF), lambda i: (i, 0)),
        out_shape=jax.ShapeDtypeStruct((t, 2 * D_FF), BF16),
        compiler_params=_params(("parallel",)),
    )(dh, w_out, a, b)


def _latent_fwd(proj, gq, gkv):
    t = proj.shape[0]
    tr = _row_tile(t)

    def body(p_ref, gq_ref, gkv_ref, cq_ref, ckv_ref):
        cq = p_ref[:, P_CQ:P_CQ + Q_LORA].astype(F32)
        ckv = p_ref[:, P_CKV:P_CKV + KV_LORA].astype(F32)
        cq_ref[...] = ((cq * _rms(cq, Q_LORA)) * gq_ref[...]).astype(BF16)
        ckv_ref[...] = ((ckv * _rms(ckv, KV_LORA)) * gkv_ref[...]).astype(BF16)

    return pl.pallas_call(
        body, name="latent_fwd", grid=(t // tr,),
        in_specs=[pl.BlockSpec((tr, 1024), lambda i: (i, 0)), pl.BlockSpec((1, Q_LORA), lambda i: (0, 0)),
                  pl.BlockSpec((1, KV_LORA), lambda i: (0, 0))],
        out_specs=[pl.BlockSpec((tr, Q_LORA), lambda i: (i, 0)), pl.BlockSpec((tr, KV_LORA), lambda i: (i, 0))],
        out_shape=[jax.ShapeDtypeStruct((t, Q_LORA), BF16), jax.ShapeDtypeStruct((t, KV_LORA), BF16)],
        compiler_params=_params(("parallel",)),
    )(proj, gq, gkv)


def _latent_bwd(dcqn, dckvn, proj, dkr, gq, gkv, dgates, dsq, dsk, dsv):
    t = proj.shape[0]
    tr = _row_tile(t, 256)

    def norm_bwd(dn, x, g, width):
        r = _rms(x, width)
        y = x * r
        dy = dn * g
        dx = r * (dy - y * (jnp.sum(dy * y, axis=-1, keepdims=True) * (1.0 / width)))
        return dx, jnp.sum(dn * y, axis=0, keepdims=True)

    def body(dcq_ref, dckv_ref, p_ref, dkr_ref, gq_ref, gkv_ref, dg_ref, dsq_ref, dsk_ref, dsv_ref,
             o_ref, dgq_ref, dgkv_ref):
        i = pl.program_id(0)
        o_ref[:, P_GM:P_SBQ] = dg_ref[...]
        for at, ref in ((P_SBQ, dsq_ref), (P_SBK, dsk_ref), (P_SBV, dsv_ref)):
            o_ref[:, at:at + SB_WIDTH] = ref[...].astype(BF16)
        dcq, pq = norm_bwd(dcq_ref[...], p_ref[:, P_CQ:P_CQ + Q_LORA].astype(F32), gq_ref[...], Q_LORA)
        dckv, pkv = norm_bwd(dckv_ref[...], p_ref[:, P_CKV:P_CKV + KV_LORA].astype(F32), gkv_ref[...], KV_LORA)
        o_ref[:, P_CQ:P_CQ + Q_LORA] = dcq.astype(BF16)
        o_ref[:, P_CKV:P_CKV + KV_LORA] = dckv.astype(BF16)
        o_ref[:, P_KR:P_KR + 128] = dkr_ref[...].astype(BF16)
        o_ref[:, P_KR + 128:1024] = jnp.zeros((tr, 1024 - P_KR - 128), BF16)

        @pl.when(i == 0)
        def _():
            dgq_ref[...] = pq
            dgkv_ref[...] = pkv

        @pl.when(i > 0)
        def _():
            dgq_ref[...] += pq
            dgkv_ref[...] += pkv

    def row(w):
        return pl.BlockSpec((tr, w), lambda i: (i, 0))

    def vec(w):
        return pl.BlockSpec((1, w), lambda i: (0, 0))

    return pl.pallas_call(
        body, name="latent_bwd", grid=(t // tr,),
        in_specs=[row(Q_LORA), row(KV_LORA), row(1024), row(128), vec(Q_LORA), vec(KV_LORA),
                  row(2 * D_MODEL), row(SB_WIDTH), row(SB_WIDTH), row(SB_WIDTH)],
        out_specs=[row(PROJ_W), vec(Q_LORA), vec(KV_LORA)],
        out_shape=[jax.ShapeDtypeStruct((t, PROJ_W), BF16), jax.ShapeDtypeStruct((1, Q_LORA), F32),
                   jax.ShapeDtypeStruct((1, KV_LORA), F32)],
        compiler_params=_params(("arbitrary",)),
    )(dcqn, dckvn, proj, dkr, gq, gkv, dgates, dsq, dsk, dsv)


def _rope(y, cosf, sin_a, sin_b):
    return y * cosf + pltpu.roll(y, 112, 1) * sin_a + pltpu.roll(y, 16, 1) * sin_b


def _rope_t(d, cosf, sin_a, sin_b):
    return d * cosf + pltpu.roll(d * sin_a, 16, 1) + pltpu.roll(d * sin_b, 112, 1)


def _headprep_fwd(qraw, kvraw, proj, cosf, sin_a, sin_b, gqh, gkh):
    t = qraw.shape[0]
    tr = _row_tile(t, 256)

    def body(q_ref, kv_ref, kr_ref, c_ref, sa_ref, sb_ref, gq_ref, gk_ref, qh_ref, kh_ref, kvb_ref):
        cv, sa, sb = c_ref[...], sa_ref[...], sb_ref[...]
        kr = kr_ref[...].astype(F32)
        lane = lax.broadcasted_iota(jnp.int32, (tr, HEAD_PAD), 1)
        heads = [slice(h * HEAD_PAD, (h + 1) * HEAD_PAD) for h in range(N_HEADS)]
        xqs = [q_ref[:, cols] for cols in heads]
        kvs = [kv_ref[:, cols] for cols in heads]
        xks = [jnp.where(lane < MLA_NOPE, kvh, kr) for kvh in kvs]
        rqs = [_rms(x, MLA_QK) for x in xqs]
        rks = [_rms(x, MLA_QK) for x in xks]
        gq = gq_ref[...] * MLA_Q_SCALE
        yqs = [(x * r) * gq for x, r in zip(xqs, rqs)]
        yks = [(x * r) * gk_ref[...] for x, r in zip(xks, rks)]
        for cols, yq, yk, kvh in zip(heads, yqs, yks, kvs):
            qh_ref[:, cols] = _rope(yq, cv, sa, sb).astype(BF16)
            kh_ref[:, cols] = _rope(yk, cv, sa, sb).astype(BF16)
            kvb_ref[:, cols] = jnp.where(lane < MLA_NOPE, 1.0, kvh).astype(BF16)

    wide = pl.BlockSpec((tr, 1024), lambda i: (i, 0))
    lanes = pl.BlockSpec((tr, HEAD_PAD), lambda i: (i, 0))
    vec = pl.BlockSpec((1, HEAD_PAD), lambda i: (0, 0))
    return pl.pallas_call(
        body, name="headprep_fwd", grid=(t // tr,),
        in_specs=[wide, wide, pl.BlockSpec((tr, HEAD_PAD), lambda i: (i, P_KR // HEAD_PAD)), lanes, lanes, lanes, vec, vec],
        out_specs=[wide, wide, wide],
        out_shape=[jax.ShapeDtypeStruct((t, 1024), BF16)] * 3,
        compiler_params=_params(("parallel",)),
    )(qraw, kvraw, proj, cosf, sin_a, sin_b, gqh, gkh)


def _headprep_bwd(dqh, dkh, dvp, qraw, kvraw, proj, cosf, sin_a, sin_b, gqh, gkh):
    t = qraw.shape[0]
    tr = _row_tile(t, 256)

    def body(dq_ref, dk_ref, dv_ref, q_ref, kv_ref, kr_ref, c_ref, sa_ref, sb_ref, gq_ref, gk_ref,
             dqr_ref, dkvr_ref, dkr_ref, dgq_ref, dgk_ref):
        i = pl.program_id(0)
        cv, sa, sb = c_ref[...], sa_ref[...], sb_ref[...]
        kr = kr_ref[...].astype(F32)
        lane = lax.broadcasted_iota(jnp.int32, (tr, HEAD_PAD), 1)
        heads = [slice(h * HEAD_PAD, (h + 1) * HEAD_PAD) for h in range(N_HEADS)]
        xs = [q_ref[:, cols] for cols in heads] + [jnp.where(lane < MLA_NOPE, kv_ref[:, cols], kr) for cols in heads]
        gs = [gq_ref[...]] * N_HEADS + [gk_ref[...]] * N_HEADS
        dns = [_rope_t(ref[:, cols], cv, sa, sb) for ref in (dq_ref, dk_ref) for cols in heads]
        rs = [_rms(x, MLA_QK) for x in xs]
        ys = [x * r for x, r in zip(xs, rs)]
        dys = [dn * g for dn, g in zip(dns, gs)]
        means = [jnp.sum(dy * y, axis=-1, keepdims=True) * (1.0 / MLA_QK) for dy, y in zip(dys, ys)]
        dxs = [r * (dy - y * m) for r, dy, y, m in zip(rs, dys, ys, means)]
        parts = [jnp.sum(dn * y, axis=0, keepdims=True) for dn, y in zip(dns, ys)]
        dkr = jnp.zeros((tr, HEAD_PAD), F32)
        pq = jnp.zeros((1, HEAD_PAD), F32)
        pk = jnp.zeros((1, HEAD_PAD), F32)
        for h, cols in enumerate(heads):
            dqr_ref[:, cols] = dxs[h].astype(BF16)
            dxk = dxs[N_HEADS + h]
            dkvr_ref[:, cols] = jnp.where(lane < MLA_NOPE, dxk, dv_ref[:, cols]).astype(BF16)
            dkr = dkr + jnp.where(lane < MLA_NOPE, 0.0, dxk)
            pq = pq + parts[h]
            pk = pk + parts[N_HEADS + h]
        dkr_ref[...] = dkr

        @pl.when(i == 0)
        def _():
            dgq_ref[...] = pq
            dgk_ref[...] = pk

        @pl.when(i > 0)
        def _():
            dgq_ref[...] += pq
            dgk_ref[...] += pk

    wide = pl.BlockSpec((tr, 1024), lambda i: (i, 0))
    lanes = pl.BlockSpec((tr, HEAD_PAD), lambda i: (i, 0))
    vec = pl.BlockSpec((1, HEAD_PAD), lambda i: (0, 0))
    return pl.pallas_call(
        body, name="headprep_bwd", grid=(t // tr,),
        in_specs=[wide, wide, wide, wide, wide, pl.BlockSpec((tr, HEAD_PAD), lambda i: (i, P_KR // HEAD_PAD)),
                  lanes, lanes, lanes, vec, vec],
        out_specs=[wide, wide, lanes, vec, vec],
        out_shape=[jax.ShapeDtypeStruct((t, 1024), BF16), jax.ShapeDtypeStruct((t, 1024), BF16),
                   jax.ShapeDtypeStruct((t, HEAD_PAD), F32), jax.ShapeDtypeStruct((1, HEAD_PAD), F32),
                   jax.ShapeDtypeStruct((1, HEAD_PAD), F32)],
        compiler_params=_params(("arbitrary",)),
    )(dqh, dkh, dvp, qraw, kvraw, proj, cosf, sin_a, sin_b, gqh, gkh)


def _merge_fwd(o_mla, w_mla, o_sb, w_sb, proj):
    t = proj.shape[0]
    tr = _row_tile(t, 512)

    def body(om_ref, wm_ref, os_ref, ws_ref, gm_ref, gs_ref, o_ref, bm_ref, bs_ref):
        omv, osv = om_ref[...], os_ref[...]
        for cols in _chunks(D_MODEL):
            bm = _dot(omv, wm_ref[:, cols], NN_DIMS)
            bs = _dot(osv, ws_ref[:, cols], NN_DIMS)
            bm_ref[:, cols] = bm
            bs_ref[:, cols] = bs
            gm = _sigmoid(gm_ref[:, cols].astype(F32))
            gs = _sigmoid(gs_ref[:, cols].astype(F32))
            o_ref[:, cols] = (gm * bm + gs * bs).astype(BF16)

    row = pl.BlockSpec((tr, 1024), lambda i: (i, 0))
    f32 = jax.ShapeDtypeStruct((t, 1024), F32)
    return pl.pallas_call(
        body, name="merge_fwd", grid=(t // tr,),
        in_specs=[row, pl.BlockSpec(w_mla.shape, lambda i: (0, 0)),
                  pl.BlockSpec((tr, SB_WIDTH), lambda i: (i, 0)), pl.BlockSpec(w_sb.shape, lambda i: (0, 0)),
                  pl.BlockSpec((tr, 1024), lambda i: (i, P_GM // 1024)), pl.BlockSpec((tr, 1024), lambda i: (i, P_GS // 1024))],
        out_specs=[row, row, row], out_shape=[jax.ShapeDtypeStruct((t, 1024), BF16), f32, f32],
        compiler_params=_params(("parallel",)),
    )(o_mla, w_mla, o_sb, w_sb, proj, proj)


def _merge_bwd(dh, w_out, proj, bm, bs):
    t = proj.shape[0]
    tr = _row_tile(t, 512)

    def body(d_ref, w_ref, gm_ref, gs_ref, bm_ref, bs_ref, dbm_ref, dbs_ref, dg_ref):
        dhv = d_ref[...]
        for cols in _chunks(D_MODEL):
            dm = _dot(dhv, w_ref[cols, :], NT_DIMS)
            gm = _sigmoid(gm_ref[:, cols].astype(F32))
            gs = _sigmoid(gs_ref[:, cols].astype(F32))
            dbm_ref[:, cols] = (dm * gm).astype(BF16)
            dbs_ref[:, cols] = (dm * gs).astype(BF16)
            dg_ref[:, cols] = (dm * bm_ref[:, cols] * gm * (1.0 - gm)).astype(BF16)
            dg_ref[:, slice(D_MODEL + cols.start, D_MODEL + cols.stop)] = (dm * bs_ref[:, cols] * gs * (1.0 - gs)).astype(BF16)

    row = pl.BlockSpec((tr, 1024), lambda i: (i, 0))
    return pl.pallas_call(
        body, name="mix_out_dx", grid=(t // tr,),
        in_specs=[row, pl.BlockSpec((D_MODEL, D_MODEL), lambda i: (0, 0)),
                  pl.BlockSpec((tr, 1024), lambda i: (i, P_GM // 1024)),
                  pl.BlockSpec((tr, 1024), lambda i: (i, P_GS // 1024)), row, row],
        out_specs=[row, row, pl.BlockSpec((tr, 2048), lambda i: (i, 0))],
        out_shape=[jax.ShapeDtypeStruct((t, 1024), BF16), jax.ShapeDtypeStruct((t, 1024), BF16),
                   jax.ShapeDtypeStruct((t, 2048), BF16)],
        compiler_params=_params(("parallel",)),
    )(dh, w_out, proj, proj, bm, bs)


def _ple_loss(h3, g, w_gate, pb, w_proj, tgt):
    t = h3.shape[0]
    tr = _row_tile(t, 512)

    def body(h_ref, g_ref, wg_ref, p_ref, wp_ref, t_ref, n_ref, dh_ref, dz_ref, dp_ref, l_ref):
        i = pl.program_id(0)
        xv = h_ref[...]
        nv = ((xv * _rms(xv, D_MODEL)) * g_ref[...]).astype(BF16)
        n_ref[...] = nv
        pv = p_ref[...]
        part = jnp.zeros((1, 128), F32)
        for cols in _chunks(D_MODEL):
            pg = _sigmoid(_dot(nv, wg_ref[:, cols], NN_DIMS))
            ppv = _dot(pv, wp_ref[:, cols], NN_DIMS)
            diff = (h_ref[:, cols] + pg * ppv) - t_ref[:, cols]
            dh = diff * (1.0 / D_MODEL)
            dh_ref[:, cols] = dh
            dp_ref[:, cols] = (dh * pg).astype(BF16)
            dz_ref[:, cols] = (dh * ppv * pg * (1.0 - pg)).astype(BF16)
            sq = jnp.sum(diff * diff, axis=0, keepdims=True)
            for c in range(sq.shape[1] // 128):
                part = part + sq[:, c * 128:(c + 1) * 128]

        @pl.when(i == 0)
        def _():
            l_ref[...] = part

        @pl.when(i > 0)
        def _():
            l_ref[...] += part

    row = pl.BlockSpec((tr, 1024), lambda i: (i, 0))
    return pl.pallas_call(
        body, name="ple_loss", grid=(t // tr,),
        in_specs=[row, pl.BlockSpec((1, D_MODEL), lambda i: (0, 0)), pl.BlockSpec((D_MODEL, D_MODEL), lambda i: (0, 0)),
                  pl.BlockSpec((tr, PLE_DIM), lambda i: (i, 0)), pl.BlockSpec((PLE_DIM, D_MODEL), lambda i: (0, 0)), row],
        out_specs=[row, row, row, row, pl.BlockSpec((1, 128), lambda i: (0, 0))],
        out_shape=[jax.ShapeDtypeStruct((t, 1024), BF16), jax.ShapeDtypeStruct((t, 1024), F32),
                   jax.ShapeDtypeStruct((t, 1024), BF16), jax.ShapeDtypeStruct((t, 1024), BF16),
                   jax.ShapeDtypeStruct((1, 128), F32)],
        compiler_params=_params(("arbitrary",)),
    )(h3, g, w_gate, pb, w_proj, tgt)


ATT_BLOCK = 256
MLA_Q_SCALE = math.log2(math.e) / math.sqrt(MLA_QK)
MLA_Q_BLOCK = 512
MLA_FWD_COLS = 8
MLA_BWD_COLS = 4
SB_FWD_COLS = 4
SB_BWD_COLS = 2
SB_BLOCK = 256


def _split_bf16(x):
    hi = x.astype(BF16)
    return hi, (x - hi.astype(F32)).astype(BF16)


def _tri(kind, n):
    r = lax.broadcasted_iota(jnp.int32, (n, n), 0)
    c = lax.broadcasted_iota(jnp.int32, (n, n), 1)
    cond = {'gt': r > c, 'le': r <= c, 'lt': r < c}[kind]
    return jnp.where(cond, 1.0, 0.0).astype(BF16)


def _causal(strict, n=ATT_BLOCK):
    r = lax.broadcasted_iota(jnp.int32, (n, n), 0)
    c = lax.broadcasted_iota(jnp.int32, (n, n), 1)
    return (c < r) if strict else (c <= r)


def _below_diagonal(rows):
    r = lax.broadcasted_iota(jnp.int32, (rows, ATT_BLOCK), 0)
    c = lax.broadcasted_iota(jnp.int32, (rows, ATT_BLOCK), 1)
    return c <= r


def _lanes(c):
    return slice(c * HEAD_PAD, (c + 1) * HEAD_PAD)


def _row_block(j, n=ATT_BLOCK):
    return pl.ds(pl.multiple_of(j * n, n), n)


def _rows(ref, j, c, n=ATT_BLOCK):
    return ref[_row_block(j, n), _lanes(c)]


def _mla_fwd(qh, kh, kvb, cargo=None):
    t = qh.shape[0]
    bq = min(MLA_Q_BLOCK, t)
    per_q = bq // ATT_BLOCK
    ncol = MLA_FWD_COLS
    grid = (N_HEADS // ncol, t // bq)

    def body(*refs):
        steps = [pl.program_id(0), pl.program_id(1)]
        _with_cargo(cargo, refs, 3, 2, steps, grid, lambda own: work(steps[1], *own))

    def work(i, q_ref, k_ref, v_ref, o_ref, lse_ref):
        qs = [q_ref[:, _lanes(c)] for c in range(ncol)]

        def step(j, carry, top):
            cols = range(ncol)
            first = top or 0
            scores = [_dot(qs[c][first:], _rows(k_ref, j, c), NT_DIMS) for c in cols]
            ms, ps, alphas = [], [], []
            for c in cols:
                s = scores[c]
                if top is not None:
                    s = jnp.where(_below_diagonal(bq - first), s, -1e30)
                m_old = carry[c][0][first:]
                m_new = jnp.maximum(m_old, jnp.max(s, axis=-1, keepdims=True))
                ps.append(jnp.exp2(s - m_new).astype(BF16))
                alphas.append(jnp.exp2(m_old - m_new))
                ms.append(m_new)
            accs = [alphas[c] * carry[c][1][first:] + _dot(ps[c], _rows(v_ref, j, c), NN_DIMS) for c in cols]
            if first:
                ms = [jnp.concatenate([carry[c][0][:first], ms[c]], axis=0) for c in cols]
                accs = [jnp.concatenate([carry[c][1][:first], accs[c]], axis=0) for c in cols]
            return tuple(zip(ms, accs))

        init = tuple((jnp.full((bq, 1), -1e30, F32), jnp.zeros((bq, HEAD_PAD), F32)) for _ in range(ncol))
        carry = lax.fori_loop(0, i * per_q, lambda j, cr: step(j, cr, None), init)
        for d in range(per_q):
            carry = step(i * per_q + d, carry, d * ATT_BLOCK)
        for c, (m, acc) in enumerate(carry):
            l = acc[:, 0:1]
            o_ref[:, _lanes(c)] = (acc / l).astype(BF16)
            lse_ref[c] = m + jnp.log2(l)

    width = ncol * HEAD_PAD
    full = pl.BlockSpec((t, width), lambda h, i: (0, h))
    blk = pl.BlockSpec((bq, width), lambda h, i: (i, h))
    extra = cargo.specs() if cargo else []
    outs = pl.pallas_call(
        body, name="mla_fwd", grid=grid,
        in_specs=[blk, full, full] + extra,
        out_specs=[blk, pl.BlockSpec((ncol, bq, 1), lambda h, i: (h, i, 0))] + extra,
        out_shape=[jax.ShapeDtypeStruct((t, N_HEADS * HEAD_PAD), BF16), jax.ShapeDtypeStruct((N_HEADS, t, 1), F32)]
        + (cargo.out_shape() if cargo else []),
        scratch_shapes=cargo.scratch() if cargo else [],
        compiler_params=_params(("arbitrary", "arbitrary")),
    )(qh, kh, kvb, *(cargo.srcs if cargo else []))
    return (outs[0], outs[1], list(outs[2:])) if cargo else outs


def _mla_bwd(qh, kh, kvb, o, do, lse, cargo=None):
    t = qh.shape[0]
    bq = min(MLA_Q_BLOCK, t)
    per_q = bq // ATT_BLOCK
    ncol = MLA_BWD_COLS
    width = ncol * HEAD_PAD
    grid = (N_HEADS // ncol, t // bq)

    def body(*refs):
        steps = [pl.program_id(0), pl.program_id(1)]
        _with_cargo(cargo, refs, 6, 3, steps, grid, lambda own: work(steps[0], steps[1], *own))

    def work(h, i, q_ref, k_ref, v_ref, o_ref, do_ref, lse_ref, dq_ref, dk_hbm, dv_hbm, dk_ref, dv_ref, out_sems):

        @pl.when(i == 0)
        def _():
            dk_ref[...] = jnp.zeros_like(dk_ref)
            dv_ref[...] = jnp.zeros_like(dv_ref)

        qs = [q_ref[:, _lanes(c)] for c in range(ncol)]
        dos = [do_ref[:, _lanes(c)] for c in range(ncol)]
        deltas = [jnp.sum(dos[c].astype(F32) * o_ref[:, _lanes(c)].astype(F32), axis=-1, keepdims=True)
                  for c in range(ncol)]
        lses = [lse_ref[c] for c in range(ncol)]

        def step(j, dqs, top):
            cols = range(ncol)
            first = top or 0
            kbs = [_rows(k_ref, j, c) for c in cols]
            scores = [_dot(qs[c][first:], kbs[c], NT_DIMS) for c in cols]
            dps = [_dot(dos[c][first:], _rows(v_ref, j, c), NT_DIMS) for c in cols]
            pbs, dss = [], []
            for c in cols:
                p = jnp.exp2(scores[c] - lses[c][first:])
                if top is not None:
                    p = jnp.where(_below_diagonal(bq - first), p, 0.0)
                pbs.append(p.astype(BF16))
                dss.append((p * (dps[c] - deltas[c][first:])).astype(BF16))
            for c in cols:
                dv_ref[_row_block(j), _lanes(c)] += _dot(pbs[c], dos[c][first:], TN_DIMS)
                dk_ref[_row_block(j), _lanes(c)] += _dot(dss[c], qs[c][first:], TN_DIMS)
            new = [dqs[c][first:] + _dot(dss[c], kbs[c], NN_DIMS) for c in cols]
            if first:
                new = [jnp.concatenate([dqs[c][:first], new[c]], axis=0) for c in cols]
            return tuple(new)

        init = tuple(jnp.zeros((bq, HEAD_PAD), F32) for _ in range(ncol))
        dqs = lax.fori_loop(0, i * per_q, lambda j, cr: step(j, cr, None), init)
        for d in range(per_q):
            dqs = step(i * per_q + d, dqs, d * ATT_BLOCK)
        for c, dq in enumerate(dqs):
            dq_ref[:, _lanes(c)] = dq * (1.0 / math.sqrt(MLA_QK))

        @pl.when(i == grid[1] - 1)
        def _():
            dk_ref[...] = dk_ref[...] * math.log(2.0)
            cols = pl.ds(pl.multiple_of(h * width, width), width)
            out = [pltpu.make_async_copy(dk_ref, dk_hbm.at[:, cols], out_sems.at[0]),
                   pltpu.make_async_copy(dv_ref, dv_hbm.at[:, cols], out_sems.at[1])]
            for cp in out:
                cp.start()
            for cp in out:
                cp.wait()

    full = pl.BlockSpec((t, width), lambda h, i: (0, h))
    blk = pl.BlockSpec((bq, width), lambda h, i: (i, h))
    wide = jax.ShapeDtypeStruct((t, N_HEADS * HEAD_PAD), F32)
    extra = cargo.specs() if cargo else []
    outs = pl.pallas_call(
        body, name="mla_bwd", grid=grid,
        in_specs=[blk, full, full, blk, blk, pl.BlockSpec((ncol, bq, 1), lambda h, i: (h, i, 0))] + extra,
        out_specs=[blk, HBM_SPEC, HBM_SPEC] + extra,
        out_shape=[wide, wide, wide] + (cargo.out_shape() if cargo else []),
        scratch_shapes=[pltpu.VMEM((t, width), F32), pltpu.VMEM((t, width), F32), pltpu.SemaphoreType.DMA((2,))]
        + (cargo.scratch() if cargo else []),
        compiler_params=_params(("arbitrary", "arbitrary")),
    )(qh, kh, kvb, o, do, lse, *(cargo.srcs if cargo else []))
    return (outs[0], outs[1], outs[2], list(outs[3:])) if cargo else outs


def _head_only(x, lane, u):
    return jnp.where((lane >= u * SB_DIM) & (lane < (u + 1) * SB_DIM), x, jnp.zeros_like(x))


SB_DEAD = -104.0


def _log_sigmoids(z):
    e = jnp.exp(-jnp.abs(z))
    lg = jnp.log(1.0 + e)
    ls_pos = jnp.minimum(z, 0.0) - lg
    return ls_pos, ls_pos - z, e


def _sb_fwd(proj):
    t = proj.shape[0]
    bq, ncol = SB_BLOCK, SB_FWD_COLS
    nq = t // bq
    scale = 1.0 / math.sqrt(SB_DIM)
    pairs = SB_WIDTH // HEAD_PAD

    def body(q_ref, k_ref, v_ref, o_ref, r_ref, first_ref):
        g, i = pl.program_id(0), pl.program_id(1)
        lane = lax.broadcasted_iota(jnp.int32, (bq, HEAD_PAD), 1)
        upper = _tri('gt', bq)
        chains = [(c, u) for c in range(ncol) for u in range(2)]
        qms = [_head_only(q_ref[:, _lanes(c)], lane, u) * scale for c, u in chains]

        def step(j, carry, masked):
            ids = range(len(chains))
            zs = [_dot(qms[n], _rows(k_ref, j, chains[n][0], bq), NT_DIMS) for n in ids]
            pos, neg, parts = [], [], []
            for n in ids:
                ls_pos, ls_neg, _ = _log_sigmoids(zs[n])
                if masked:
                    ls_neg = jnp.where(_causal(True, bq), ls_neg, 0.0)
                pos.append(ls_pos)
                neg.append(ls_neg)
                parts.append(_split_bf16(ls_neg))
            suffix = [_dot(parts[n][0], upper, NN_DIMS) + _dot(parts[n][1], upper, NN_DIMS) for n in ids]
            weights = []
            for n in ids:
                a = jnp.exp(pos[n] + suffix[n] + carry[n][0])
                if masked:
                    a = jnp.where(_causal(True, bq), a, 0.0)
                weights.append(a.astype(BF16))
            return tuple((carry[n][0] + jnp.sum(neg[n], axis=-1, keepdims=True),
                          carry[n][1] + _dot(weights[n], _rows(v_ref, j, chains[n][0], bq), NN_DIMS)) for n in ids)

        init = tuple((jnp.zeros((bq, 1), F32), jnp.zeros((bq, HEAD_PAD), F32)) for _ in chains)
        carry = step(i, init, True)

        def more(state):
            s, cr = state
            live = cr[0][0]
            for n in range(1, len(chains)):
                live = jnp.maximum(live, cr[n][0])
            return jnp.logical_and(s < i, jnp.max(live) > SB_DEAD)

        walked, carry = lax.while_loop(more, lambda st: (st[0] + 1, step(i - 1 - st[0], st[1], False)),
                                       (jnp.int32(0), carry))
        first_ref[g * nq + i] = i - walked
        for n, (c, u) in enumerate(chains):
            r_ref[2 * c + u] = carry[n][0]
        for c in range(ncol):
            o_ref[:, _lanes(c)] = jnp.where(lane < SB_DIM, carry[2 * c][1], carry[2 * c + 1][1]).astype(BF16)

    width = ncol * HEAD_PAD

    def full(c0):
        return pl.BlockSpec((t, width), lambda g, i: (0, c0 // width + g))

    return pl.pallas_call(
        body, name="sb_fwd", grid=(pairs // ncol, t // bq),
        in_specs=[pl.BlockSpec((bq, width), lambda g, i: (i, P_SBQ // width + g)), full(P_SBK), full(P_SBV)],
        out_specs=[pl.BlockSpec((bq, width), lambda g, i: (i, g)),
                   pl.BlockSpec((2 * ncol, bq, 1), lambda g, i: (g, i, 0)),
                   pl.BlockSpec(memory_space=pltpu.SMEM)],
        out_shape=[jax.ShapeDtypeStruct((t, SB_WIDTH), BF16), jax.ShapeDtypeStruct((N_HEADS, t, 1), F32),
                   jax.ShapeDtypeStruct((pairs // ncol * nq,), jnp.int32)],
        compiler_params=_params(("arbitrary", "arbitrary")),
    )(proj, proj, proj)


def _sb_bwd(proj, do, rtot, first):
    t = proj.shape[0]
    bq, ncol = SB_BLOCK, SB_BWD_COLS
    nq = t // bq
    scale = 1.0 / math.sqrt(SB_DIM)
    pairs = SB_WIDTH // HEAD_PAD

    def body(first_ref, q_ref, k_ref, v_ref, do_ref, r_ref, dq_ref, dk_ref, dv_ref):
        g, i = pl.program_id(0), pl.program_id(1)

        @pl.when(i == 0)
        def _():
            dk_ref[...] = jnp.zeros_like(dk_ref)
            dv_ref[...] = jnp.zeros_like(dv_ref)

        lane = lax.broadcasted_iota(jnp.int32, (bq, HEAD_PAD), 1)
        incl = _tri('le', bq)
        excl = _tri('lt', bq)
        chains = [(c, u) for c in range(ncol) for u in range(2)]
        qms = [_head_only(q_ref[:, _lanes(c)], lane, u) * scale for c, u in chains]
        doms = [_head_only(do_ref[:, _lanes(c)], lane, u) for c, u in chains]
        rts = [r_ref[2 * c + u] for c, u in chains]

        def step(j, carry, masked):
            ids = range(len(chains))
            kbs = [_rows(k_ref, j, c, bq) for c in range(ncol)]
            zs =[_dot(qms[n], kbs[chains[n][0]], NT_DIMS) for n in ids]
            das = [_dot(doms[n], _rows(v_ref, j, chains[n][0], bq), NT_DIMS) for n in ids]
            pos, neg, sigs, parts = [], [], [], []
            for n in ids:
                ls_pos, ls_neg, e = _log_sigmoids(zs[n])
                if masked:
                    ls_neg = jnp.where(_causal(True, bq), ls_neg, 0.0)
                pos.append(ls_pos)
                neg.append(ls_neg)
                sigs.append(jnp.where(zs[n] >= 0.0, 1.0, e) * pl.reciprocal(1.0 + e, approx=True))
                parts.append(_split_bf16(ls_neg))
            prefix = [_dot(parts[n][0], incl, NN_DIMS) + _dot(parts[n][1], incl, NN_DIMS) for n in ids]
            evs, eparts, dvs = [], [], []
            for n in ids:
                a = jnp.exp(pos[n] + (rts[n] - (carry[n][0] + prefix[n])))
                if masked:
                    a = jnp.where(_causal(True, bq), a, 0.0)
                dvs.append(_dot(a.astype(BF16), doms[n], TN_DIMS))
                evs.append(a * das[n])
                eparts.append(evs[n].astype(BF16))
            before = [_dot(eparts[n], excl, NN_DIMS) for n in ids]
            out, dks = [], []
            for n in ids:
                dz = evs[n] - sigs[n] * (evs[n] + (carry[n][1] + before[n]))
                if masked:
                    dz = jnp.where(_causal(True, bq), dz, 0.0)
                dzb = dz.astype(BF16)
                dks.append(_dot(dzb, qms[n], TN_DIMS))
                out.append((carry[n][0] + jnp.sum(neg[n], axis=-1, keepdims=True),
                            carry[n][1] + jnp.sum(evs[n], axis=-1, keepdims=True),
                            carry[n][2] + _dot(dzb, kbs[chains[n][0]], NN_DIMS)))
            for c in range(ncol):
                dv_ref[_row_block(j, bq), _lanes(c)] += dvs[2 * c] + dvs[2 * c + 1]
                dk_ref[_row_block(j, bq), _lanes(c)] += dks[2 * c] + dks[2 * c + 1]
            return tuple(out)

        init = tuple((jnp.zeros((bq, 1), F32), jnp.zeros((bq, 1), F32), jnp.zeros((bq, HEAD_PAD), F32)) for _ in chains)
        start = first_ref[(g * ncol // SB_FWD_COLS) * nq + i]
        carry = lax.fori_loop(start, i, lambda j, cr: step(j, cr, False), init)
        carry = step(i, carry, True)
        for c in range(ncol):
            dq_ref[:, _lanes(c)] = jnp.where(lane < SB_DIM, carry[2 * c][2], carry[2 * c + 1][2]) * scale

    width = ncol * HEAD_PAD

    def full(c0):
        return pl.BlockSpec((t, width), lambda g, i, first: (0, c0 // width + g))

    blk = pl.BlockSpec((bq, width), lambda g, i, first: (i, g))
    acc = pl.BlockSpec((t, width), lambda g, i, first: (0, g))
    wide = jax.ShapeDtypeStruct((t, SB_WIDTH), F32)
    return pl.pallas_call(
        body, name="sb_bwd",
        grid_spec=pltpu.PrefetchScalarGridSpec(
            num_scalar_prefetch=1, grid=(pairs // ncol, nq),
            in_specs=[pl.BlockSpec((bq, width), lambda g, i, first: (i, P_SBQ // width + g)), full(P_SBK), full(P_SBV),
                      blk, pl.BlockSpec((2 * ncol, bq, 1), lambda g, i, first: (g, i, 0))],
            out_specs=[blk, acc, acc]),
        out_shape=[wide, wide, wide],
        compiler_params=_params(("arbitrary", "arbitrary")),
    )(first, proj, proj, proj, do, rtot)


def _cols_to_full(g):
    n, r, c = g.shape
    return jnp.transpose(g, (1, 0, 2)).reshape(r, n * c)


def _full_to_cols(w):
    r, c = w.shape
    return jnp.transpose(w.reshape(r, N_DEV, c // N_DEV), (1, 0, 2))


TRANSPOSED = ('ffn1_w_in', 'ffn2_w_in', 'w_in', 'w_q_up')


def _layout_weight(name, g):
    if name in ('ffn1_w_out', 'ffn2_w_out', 'w_out', 'w_ple_gate', 'ffn1_w_in', 'ffn2_w_in'):
        return g.reshape(g.shape[0] * g.shape[1], g.shape[2])
    if name == 'w_in':
        wt = g.reshape(IN_COLS, D_MODEL)
        z = lambda n: jnp.zeros((n, D_MODEL), BF16)
        return jnp.concatenate([wt[0:640], z(64), wt[640:672], z(32), z(256), wt[2208:4256], wt[672:2208]], axis=0)
    if name == 'w_q_up':
        return jnp.pad(g, ((0, 0), (0, HEAD_PAD - MLA_QK), (0, 0))).reshape(N_HEADS * HEAD_PAD, Q_LORA)
    if name == 'w_branch_mla':
        bm = _cols_to_full(g).reshape(N_HEADS, MLA_NOPE, D_MODEL)
        return jnp.pad(bm, ((0, 0), (HEAD_PAD - MLA_NOPE, 0), (0, 0))).reshape(N_HEADS * HEAD_PAD, D_MODEL)
    return _cols_to_full(g)


def _unlayout_grad(name, d):
    if name == 'w_in':
        d = jnp.concatenate([d[0:640], d[704:736], d[P_SBQ:PROJ_W], d[P_GM:P_SBQ]], axis=0)
    if name == 'w_q_up':
        return d.reshape(N_HEADS, HEAD_PAD, Q_LORA)[:, :MLA_QK, :]
    if name in ('ffn1_w_out', 'ffn2_w_out', 'w_out', 'w_ple_gate', 'ffn1_w_in', 'ffn2_w_in', 'w_in'):
        return d.reshape(N_DEV, d.shape[0] // N_DEV, d.shape[1])
    if name == 'w_branch_mla':
        d = d.reshape(N_HEADS, HEAD_PAD, D_MODEL)[:, HEAD_PAD - MLA_NOPE:, :].reshape(SB_WIDTH, D_MODEL)
    return _full_to_cols(d)


def _rope_tables(positions):
    half = MLA_ROPE // 2
    inv_freq = ROPE_BASE ** (-jnp.arange(0, MLA_ROPE, 2, dtype=F32) / MLA_ROPE)
    ang = positions.astype(F32)[:, None] * inv_freq
    cos, sin = jnp.cos(ang), jnp.sin(ang)
    t = positions.shape[0]
    ones = lambda n: jnp.ones((t, n), F32)
    zeros = lambda n: jnp.zeros((t, n), F32)
    cosf = jnp.concatenate([ones(MLA_NOPE), cos, cos, ones(HEAD_PAD - MLA_QK)], axis=1)
    sin_a = jnp.concatenate([zeros(MLA_NOPE), -sin, zeros(half), zeros(HEAD_PAD - MLA_QK)], axis=1)
    sin_b = jnp.concatenate([zeros(MLA_NOPE), zeros(half), sin, zeros(HEAD_PAD - MLA_QK)], axis=1)
    return cosf, sin_a, sin_b


def _local_step(x, p, positions, tgt, norms, plan):
    mm = _matmul
    cosf, sin_a, sin_b = _rope_tables(positions)
    pad_head = lambda g: jnp.pad(g, ((0, 0), (0, HEAD_PAD - MLA_QK)))
    gqh, gkh = pad_head(norms['q_head_norm']), pad_head(norms['k_head_norm'])
    pb = p.astype(BF16)
    w = dict(plan.first_weights())
    dw, dn = {}, {}

    def ride(host, call):
        cargo = plan.cargo(host, dw)
        res, lands = call(cargo), None
        if cargo is not None:
            *res, lands = res
            res = res[0] if len(res) == 1 else tuple(res)
        w.update(plan.landed(host, lands))
        return res

    def ffn_fwd(h, tag):
        n, a, b, act = ride(tag + "_in_fwd", lambda cargo: _swiglu_fwd(
            h, norms[tag + '_norm'], w[tag + '_w_in'], tag + "_in_fwd", cargo))
        out = ride(tag + "_out_fwd", lambda cargo: mm(
            act, w[tag + '_w_out'], mode='nn', out_dtype=F32, name=tag + "_out_fwd", res=h, alpha=0.5, cargo=cargo))
        return out, (n, a, b, act)

    h1, ffn1_saved = ffn_fwd(x, 'ffn1')
    u = _rmsnorm_fwd(h1, norms['mix_norm'], "mix_norm_fwd")
    proj = mm(u, w['w_in'], mode='nt', out_dtype=BF16, name="proj_fwd")
    cqn, ckvn = _latent_fwd(proj, norms['q_latent_norm'], norms['kv_latent_norm'])
    qraw = mm(cqn, w['w_q_up'], mode='nt', out_dtype=F32, name="q_up_fwd")
    kvraw = mm(ckvn, w['w_kv_up'], mode='nn', out_dtype=F32, name="kv_up_fwd")
    qh, kh, kvb = _headprep_fwd(qraw, kvraw, proj, cosf, sin_a, sin_b, gqh, gkh)
    o_mla, lse = ride("mla_fwd", lambda cargo: _mla_fwd(qh, kh, kvb, cargo))
    o_sb, rtot, sb_first = _sb_fwd(proj)
    merged, bm, bs = _merge_fwd(o_mla, w['w_branch_mla'], o_sb, w['w_branch_sb'], proj)
    h2 = mm(merged, w['w_out'], mode='nn', out_dtype=F32, name="mix_out_fwd", res=h1)
    h3, ffn2_saved = ffn_fwd(h2, 'ffn2')
    n3, dh4, dzg, dpp, loss_lanes = _ple_loss(h3, norms['ple_norm'], w['w_ple_gate'], pb, w['w_ple_proj'], tgt)

    dw['w_ple_gate'] = mm(n3, dzg, mode='tn', out_dtype=BF16, name="ple_gate_dw")
    dw['w_ple_proj'] = mm(pb, dpp, mode='tn', out_dtype=BF16, name="ple_proj_dw")
    dh3, dhb3, dn['ple_norm'] = _matmul_norm_bwd(
        dzg, w['w_ple_gate'], h3, norms['ple_norm'], dh4, mode='nt', name="ple_gate_dx", out_scale=0.5)

    def ffn_bwd(h, dh, dhb, saved, tag, out_scale):
        n, a, b, act = saved
        dw[tag + '_w_out'] = mm(act, dhb, mode='tn', out_dtype=BF16, name=tag + "_out_dw", tm=1408)
        dab = _swiglu_bwd(dhb, w[tag + '_w_out'], a, b, tag + "_out_dx")
        dw[tag + '_w_in'] = ride(tag + "_in_dw", lambda cargo: mm(
            dab, n, mode='tn', out_dtype=BF16, name=tag + "_in_dw", tm=1408, cargo=cargo))
        dh_prev, dhb_prev, dn[tag + '_norm'] = ride(tag + "_in_dx", lambda cargo: _matmul_norm_bwd(
            dab, w[tag + '_w_in'], h, norms[tag + '_norm'], dh, mode='nn', name=tag + "_in_dx", out_scale=out_scale,
            cargo=cargo))
        return dh_prev, dhb_prev

    dh2, dhb2 = ffn_bwd(h2, dh3, dhb3, ffn2_saved, 'ffn2', 1.0)
    dw['w_out'] = mm(merged, dhb2, mode='tn', out_dtype=BF16, name="mix_out_dw")
    dbm, dbs, dgates = _merge_bwd(dhb2, w['w_out'], proj, bm, bs)
    dw['w_branch_mla'] = mm(o_mla, dbm, mode='tn', out_dtype=BF16, name="branch_mla_dw")
    dw['w_branch_sb'] = mm(o_sb, dbs, mode='tn', out_dtype=BF16, name="branch_sb_dw")
    do_mla = mm(dbm, w['w_branch_mla'], mode='nt', out_dtype=BF16, name="branch_mla_dx")
    do_sb = mm(dbs, w['w_branch_sb'], mode='nt', out_dtype=BF16, name="branch_sb_dx")
    dqh, dkh, dvp = ride("mla_bwd", lambda cargo: _mla_bwd(qh, kh, kvb, o_mla, do_mla, lse, cargo))
    dsq, dsk, dsv = _sb_bwd(proj, do_sb, rtot, sb_first)
    dqraw, dkvraw, dkr, dgq, dgk = _headprep_bwd(dqh, dkh, dvp, qraw, kvraw, proj, cosf, sin_a, sin_b, gqh, gkh)
    dn['q_head_norm'], dn['k_head_norm'] = dgq[:, :MLA_QK], dgk[:, :MLA_QK]
    dw['w_q_up'] = mm(dqraw, cqn, mode='tn', out_dtype=BF16, name="q_up_dw")
    dw['w_kv_up'] = mm(ckvn, dkvraw, mode='tn', out_dtype=BF16, name="kv_up_dw")
    dcqn = mm(dqraw, w['w_q_up'], mode='nn', out_dtype=F32, name="q_up_dx")
    dckvn = mm(dkvraw, w['w_kv_up'], mode='nt', out_dtype=F32, name="kv_up_dx")
    dproj, dn['q_latent_norm'], dn['kv_latent_norm'] = _latent_bwd(
        dcqn, dckvn, proj, dkr, norms['q_latent_norm'], norms['kv_latent_norm'], dgates, dsq, dsk, dsv)
    dw['w_in'] = ride("proj_dw", lambda cargo: mm(dproj, u, mode='tn', out_dtype=BF16, name="proj_dw", tm=1536, cargo=cargo))
    dh1, dhb1, dn['mix_norm'] = ride("proj_dx", lambda cargo: _matmul_norm_bwd(
        dproj, w['w_in'], h1, norms['mix_norm'], dh2, mode='nn', name="proj_dx", out_scale=0.5, cargo=cargo))
    dx, _ = ffn_bwd(x, dh1, dhb1, ffn1_saved, 'ffn1', 1.0)
    return dx, loss_lanes, dw, dn


MESH = pl.DeviceIdType.MESH
HBM_SPEC = pl.BlockSpec(memory_space=pl.ANY)


def _position():
    return lax.axis_index("x"), lax.axis_index("y"), lax.axis_index("c")


def _index(px, py, pc):
    return 4 * px + 2 * py + pc


def _all_gather(shards):
    n = len(shards)

    def body(*refs):
        ins, outs = refs[:n], refs[n:2 * n]
        send_sems, recv_sems, local_sems = refs[2 * n:]
        x, y, c = _position()
        me, sibling = (x, y, c), (x, y, 1 - c)
        chips = [(1 - x, y), (x, 1 - y), (1 - x, 1 - y)]

        def copy(a, k, block, to, own=False):
            dst = outs[a].at[_index(*block)]
            return pltpu.make_async_remote_copy(
                src_ref=ins[a] if own else dst, dst_ref=dst,
                send_sem=send_sems.at[a, k], recv_sem=recv_sems.at[a, k], device_id=to, device_id_type=MESH)

        mine = [pltpu.make_async_copy(ins[a], outs[a].at[_index(*me)], local_sems.at[a]) for a in range(n)]
        for cp in mine:
            cp.start()
        first = []
        for a in range(n):
            first.append(copy(a, 0, me, sibling, own=True))
            first += [copy(a, 1 + j, me, (*chip, c), own=True) for j, chip in enumerate(chips)]
        for cp in first:
            cp.start()
        passed = []
        for j, chip in enumerate(chips):
            for a in range(n):
                copy(a, 1 + j, (*chip, c), me).wait_recv()
                fwd = copy(a, 4 + j, (*chip, c), sibling)
                fwd.start()
                passed.append(fwd)
        for a in range(n):
            copy(a, 0, sibling, me).wait_recv()
            for j, chip in enumerate(chips):
                copy(a, 4 + j, (*chip, 1 - c), me).wait_recv()
        for cp in first + passed:
            cp.wait_send()
        for cp in mine:
            cp.wait()

    return pl.pallas_call(
        body, name="weights_all_gather",
        in_specs=[HBM_SPEC] * n, out_specs=[HBM_SPEC] * n,
        out_shape=[jax.ShapeDtypeStruct((N_DEV,) + s.shape, s.dtype) for s in shards],
        scratch_shapes=[pltpu.SemaphoreType.DMA((n, 7)), pltpu.SemaphoreType.DMA((n, 7)), pltpu.SemaphoreType.DMA((n,))],
    )(*shards)


def _exchange(parts):
    n = len(parts)
    masks = [(mx, my, mc) for mx in (0, 1) for my in (0, 1) for mc in (0, 1)][1:]

    def body(*refs):
        ins, outs = refs[:n], refs[n:2 * n]
        send_sems, recv_sems, local_sems = refs[2 * n:]
        x, y, c = _position()
        me = _index(x, y, c)

        def peer_of(mask):
            mx, my, mc = mask
            return (x + mx - 2 * x * mx, y + my - 2 * y * my, c + mc - 2 * c * mc)

        def copy(a, k):
            peer = peer_of(masks[k])
            return pltpu.make_async_remote_copy(
                src_ref=ins[a].at[_index(*peer)], dst_ref=outs[a].at[me],
                send_sem=send_sems.at[a, k], recv_sem=recv_sems.at[a, k], device_id=peer, device_id_type=MESH)

        def landed(a, k):
            peer = peer_of(masks[k])
            return pltpu.make_async_remote_copy(
                src_ref=ins[a].at[me], dst_ref=outs[a].at[_index(*peer)],
                send_sem=send_sems.at[a, k], recv_sem=recv_sems.at[a, k], device_id=peer, device_id_type=MESH)

        mine = [pltpu.make_async_copy(ins[a].at[me], outs[a].at[me], local_sems.at[a]) for a in range(n)]
        for cp in mine:
            cp.start()
        sent = [copy(a, k) for k in range(7) for a in range(n)]
        for cp in sent:
            cp.start()
        for k in range(7):
            for a in range(n):
                landed(a, k).wait_recv()
        for cp in sent:
            cp.wait_send()
        for cp in mine:
            cp.wait()

    return pl.pallas_call(
        body, name="grads_exchange",
        in_specs=[HBM_SPEC] * n, out_specs=[HBM_SPEC] * n,
        out_shape=[jax.ShapeDtypeStruct(s.shape, s.dtype) for s in parts],
        scratch_shapes=[pltpu.SemaphoreType.DMA((n, 7)), pltpu.SemaphoreType.DMA((n, 7)), pltpu.SemaphoreType.DMA((n,))],
    )(*parts)


PEER_MASKS = [(mx, my, mc) for mx in (0, 1) for my in (0, 1) for mc in (0, 1)][1:]


def _peer(mask):
    x, y, c = _position()
    mx, my, mc = mask
    return (x + mx - 2 * x * mx, y + my - 2 * y * my, c + mc - 2 * c * mc)


class _Cargo:
    def __init__(self, srcs, scatter):
        self.srcs, self.scatter, self.n = list(srcs), scatter, len(srcs)

    def specs(self):
        return [HBM_SPEC] * self.n

    def out_shape(self):
        return [jax.ShapeDtypeStruct(s.shape if self.scatter else (N_DEV,) + s.shape, s.dtype) for s in self.srcs]

    def scratch(self):
        per_copy = pltpu.SemaphoreType.DMA((self.n, len(PEER_MASKS)))
        return [per_copy, per_copy, pltpu.SemaphoreType.DMA((self.n,))]

    def _mine(self, src_refs, a, to):
        return src_refs[a].at[to] if self.scatter else src_refs[a]

    def _shard_copy(self, src_refs, land_refs, sems, a, k, block, to, own=False):
        dst = land_refs[a].at[_index(*block)]
        return pltpu.make_async_remote_copy(
            src_ref=src_refs[a] if own else dst, dst_ref=dst,
            send_sem=sems[0].at[a, k], recv_sem=sems[1].at[a, k], device_id=to, device_id_type=MESH)

    def _first_hops(self, src_refs, land_refs, sems):
        x, y, c = _position()
        chips = [(1 - x, y), (x, 1 - y), (1 - x, 1 - y)]
        hops = []
        for a in range(self.n):
            hops.append(self._shard_copy(src_refs, land_refs, sems, a, 0, (x, y, c), (x, y, 1 - c), own=True))
            hops += [self._shard_copy(src_refs, land_refs, sems, a, 1 + j, (x, y, c), (*chip, c), own=True)
                     for j, chip in enumerate(chips)]
        return hops, chips

    def start(self, src_refs, land_refs, sems):
        send, recv, local = sems
        me = _index(*_position())
        for a in range(self.n):
            pltpu.make_async_copy(self._mine(src_refs, a, me), land_refs[a].at[me], local.at[a]).start()
        if not self.scatter:
            for cp in self._first_hops(src_refs, land_refs, sems)[0]:
                cp.start()
            return
        for k, mask in enumerate(PEER_MASKS):
            peer = _peer(mask)
            for a in range(self.n):
                pltpu.make_async_remote_copy(
                    src_ref=self._mine(src_refs, a, _index(*peer)), dst_ref=land_refs[a].at[me],
                    send_sem=send.at[a, k], recv_sem=recv.at[a, k], device_id=peer, device_id_type=MESH).start()

    def _wait_gathered(self, src_refs, land_refs, sems):
        x, y, c = _position()
        me, sibling = (x, y, c), (x, y, 1 - c)
        first, chips = self._first_hops(src_refs, land_refs, sems)
        passed = []
        for j, chip in enumerate(chips):
            for a in range(self.n):
                self._shard_copy(src_refs, land_refs, sems, a, 1 + j, (*chip, c), me).wait_recv()
                passed.append(self._shard_copy(src_refs, land_refs, sems, a, 4 + j, (*chip, c), sibling))
                passed[-1].start()
        for a in range(self.n):
            self._shard_copy(src_refs, land_refs, sems, a, 0, sibling, me).wait_recv()
            for j, chip in enumerate(chips):
                self._shard_copy(src_refs, land_refs, sems, a, 4 + j, (*chip, 1 - c), me).wait_recv()
        for cp in first + passed:
            cp.wait_send()

    def wait(self, src_refs, land_refs, sems):
        send, recv, local = sems
        me = _index(*_position())
        if not self.scatter:
            self._wait_gathered(src_refs, land_refs, sems)
        for k, mask in enumerate(PEER_MASKS if self.scatter else []):
            peer = _peer(mask)
            there = _index(*peer)
            for a in range(self.n):
                pltpu.make_async_remote_copy(
                    src_ref=self._mine(src_refs, a, me), dst_ref=land_refs[a].at[there],
                    send_sem=send.at[a, k], recv_sem=recv.at[a, k], device_id=peer, device_id_type=MESH).wait_recv()
                pltpu.make_async_remote_copy(
                    src_ref=self._mine(src_refs, a, there), dst_ref=land_refs[a].at[me],
                    send_sem=send.at[a, k], recv_sem=recv.at[a, k], device_id=peer, device_id_type=MESH).wait_send()
        for a in range(self.n):
            pltpu.make_async_copy(self._mine(src_refs, a, me), land_refs[a].at[me], local.at[a]).wait()


def _with_cargo(cargo, refs, n_in, n_out, steps, counts, compute):
    if cargo is None:
        compute(refs)
        return
    n = cargo.n
    src_refs = refs[n_in:n_in + n]
    land_refs = refs[n_in + n + n_out:n_in + 2 * n + n_out]
    sems = refs[-3:]
    first = functools.reduce(jnp.logical_and, [s == 0 for s in steps])
    last = functools.reduce(jnp.logical_and, [s == c - 1 for s, c in zip(steps, counts)])

    @pl.when(first)
    def _():
        cargo.start(src_refs, land_refs, sems)

    compute(refs[:n_in] + refs[n_in + n:n_in + n + n_out] + refs[n_in + 2 * n + n_out:-3])

    @pl.when(last)
    def _():
        cargo.wait(src_refs, land_refs, sems)


def _adamw(parts, w, m, v, name):
    r, c = w.shape
    tr = next((t for t in (512, 384, 352, 256, 128) if r % t == 0), r) if r > 512 else r
    tc = c if tr < r or r <= 512 else 256
    assert r % tr == 0 and c % tc == 0
    bc1 = 1.0 - ADAM_B1 ** ADAM_STEP
    bc2 = 1.0 - ADAM_B2 ** ADAM_STEP

    def body(p_ref, w_ref, m_ref, v_ref, g_ref, d_ref, nm_ref, nv_ref):
        g = p_ref[0].astype(F32)
        for s in range(1, N_DEV):
            g = g + p_ref[s].astype(F32)
        nm = ADAM_B1 * m_ref[...] + (1.0 - ADAM_B1) * g
        nv = ADAM_B2 * v_ref[...] + (1.0 - ADAM_B2) * (g * g)
        g_ref[...] = g
        nm_ref[...] = nm
        nv_ref[...] = nv
        d_ref[...] = -ADAM_LR * ((nm / bc1) / (jnp.sqrt(nv / bc2) + ADAM_EPS) + ADAM_WD * w_ref[...])

    tile = pl.BlockSpec((tr, tc), lambda i, j: (i, j))
    out = jax.ShapeDtypeStruct((r, c), F32)
    return pl.pallas_call(
        body, name=name, grid=(r // tr, c // tc),
        in_specs=[pl.BlockSpec((N_DEV, tr, tc), lambda i, j: (0, i, j)), tile, tile, tile],
        out_specs=[tile] * 4, out_shape=[out] * 4,
        compiler_params=_params(("parallel", "parallel")),
    )(parts, w, m, v)


GATHER_FIRST = ['ffn1_w_in']
RIDES = {
    'ffn1_in_fwd': ('weights', ['ffn1_w_out', 'w_in']),
    'ffn1_out_fwd': ('weights', ['w_q_up', 'w_kv_up', 'w_branch_mla', 'w_branch_sb', 'w_out']),
    'mla_fwd': ('weights', ['ffn2_w_in', 'ffn2_w_out', 'w_ple_gate', 'w_ple_proj']),
    'mla_bwd': ('grads', ['w_ple_gate', 'w_ple_proj', 'ffn2_w_out', 'ffn2_w_in', 'w_out', 'w_branch_mla', 'w_branch_sb']),
    'proj_dw': ('grads', ['w_q_up', 'w_kv_up']),
    'proj_dx': ('grads', ['w_in']),
    'ffn1_in_dw': ('grads', ['ffn1_w_out']),
    'ffn1_in_dx': ('grads', ['ffn1_w_in']),
}


class _Plan:
    def __init__(self, shards):
        self.shards = shards
        self.received = {}

    def first_weights(self):
        gathered = _all_gather([self.shards[n] for n in GATHER_FIRST])
        return {n: _layout_weight(n, g) for n, g in zip(GATHER_FIRST, gathered)}

    def cargo(self, host, dw):
        if host not in RIDES:
            return None
        kind, names = RIDES[host]
        if kind == 'weights':
            return _Cargo([self.shards[n] for n in names], False)
        return _Cargo([_unlayout_grad(n, dw.pop(n)) for n in names], True)

    def landed(self, host, lands):
        if host not in RIDES:
            return {}
        kind, names = RIDES[host]
        if kind == 'weights':
            return {n: _layout_weight(n, land) for n, land in zip(names, lands)}
        self.received.update(zip(names, lands))
        return {}


def _pack_small(vecs):
    flat = jnp.concatenate([v.reshape(-1) for v in vecs])
    return jnp.pad(flat, (0, SMALL_ROWS * 128 - flat.shape[0])).reshape(SMALL_ROWS, 128)


def _unpack_small(packed, sizes):
    flat = packed.reshape(-1)
    out, at = [], 0
    for n in sizes:
        out.append(flat[at:at + n])
        at += n
    return out


def kernel(x, p, positions, ffn1_norm, ffn1_w_in, ffn1_w_out, mix_norm, w_in, q_latent_norm, w_q_up, kv_latent_norm, w_kv_up, q_head_norm, k_head_norm, w_branch_mla, w_branch_sb, w_out, ffn2_norm, ffn2_w_in, ffn2_w_out, ple_norm, w_ple_gate, w_ple_proj, loss_target, m_ffn1_norm, m_ffn1_w_in, m_ffn1_w_out, m_mix_norm, m_w_in, m_q_latent_norm, m_w_q_up, m_kv_latent_norm, m_w_kv_up, m_q_head_norm, m_k_head_norm, m_w_branch_mla, m_w_branch_sb, m_w_out, m_ffn2_norm, m_ffn2_w_in, m_ffn2_w_out, m_ple_norm, m_w_ple_gate, m_w_ple_proj, v_ffn1_norm, v_ffn1_w_in, v_ffn1_w_out, v_mix_norm, v_w_in, v_q_latent_norm, v_w_q_up, v_kv_latent_norm, v_w_kv_up, v_q_head_norm, v_k_head_norm, v_w_branch_mla, v_w_branch_sb, v_w_out, v_ffn2_norm, v_ffn2_w_in, v_ffn2_w_out, v_ple_norm, v_w_ple_gate, v_w_ple_proj):
    given = dict(locals())
    wts = {n: given[n] for n in WEIGHTS}
    mom = {n: given['m_' + n] for n in WEIGHTS}
    var = {n: given['v_' + n] for n in WEIGHTS}

    def local(a, n):
        return jnp.swapaxes(a[0], 0, 1) if n in TRANSPOSED else a[0]

    plan = _Plan({n: local(wts[n], n).astype(BF16) for n in MATS})
    norms = {n: wts[n] for n in NORMS}
    dx, loss_lanes, dw, dn = _local_step(x[0], p[0, 0], positions[0], loss_target[0], norms, plan)
    assert not dw

    out = {}
    for n in MATS:
        res = _adamw(plan.received[n], local(wts[n], n), local(mom[n], n), local(var[n], n), "adamw_" + n)
        out[n] = [local(r[None], n)[None] for r in res]
    small = _pack_small([dn[n] for n in NORMS] + [0.5 / D_MODEL * jnp.sum(loss_lanes)[None]])
    small_parts = _exchange([jnp.broadcast_to(small[None], (N_DEV, SMALL_ROWS, 128))])[0]
    sizes = [wts[n].shape[1] for n in NORMS]
    pack = lambda d: _pack_small([d[n] for n in NORMS])
    small_res = _adamw(small_parts, pack(wts), pack(mom), pack(var), "adamw_norms")
    loss = small_res[0].reshape(-1)[sum(sizes)]
    for i, res in enumerate(small_res):
        for n, vec in zip(NORMS, _unpack_small(res, sizes)):
            out.setdefault(n, [None] * 4)[i] = vec[None]

    return (loss, dx[None], *[out[n][0] for n in WEIGHTS], *[out[n][1] for n in WEIGHTS],
            *[out[n][2] for n in WEIGHTS], *[out[n][3] for n in WEIGHTS])
```

```python
import functools
import math

import jax
import jax.numpy as jnp
from jax import lax
from jax.experimental import pallas as pl
from jax.experimental.pallas import tpu as pltpu

F32 = jnp.float32
BF16 = jnp.bfloat16

N_DEV = 8
D_MODEL = 1024
D_FF = 2816
PLE_DIM = 256
NORM_EPS = 1e-6
N_HEADS = 8
HEAD_PAD = 128
MLA_NOPE = 64
MLA_ROPE = 32
MLA_QK = 96
Q_LORA = 384
KV_LORA = 256
SB_DIM = 64
SB_WIDTH = 512
ROPE_BASE = 10000.0
IN_COLS = 4256

PROJ_W = 4608
P_CQ, P_CKV, P_KR, P_GM, P_GS, P_SBQ, P_SBK, P_SBV = 0, 384, 640, 1024, 2048, 3072, 3584, 4096

ADAM_LR, ADAM_B1, ADAM_B2, ADAM_EPS, ADAM_WD, ADAM_STEP = 0.001, 0.9, 0.999, 1e-08, 0.01, 10

VMEM_LIMIT = 52 * 1024 * 1024
MATMUL_VMEM = 40 * 1024 * 1024

WEIGHTS = ['ffn1_norm', 'ffn1_w_in', 'ffn1_w_out', 'mix_norm', 'w_in', 'q_latent_norm', 'w_q_up',
           'kv_latent_norm', 'w_kv_up', 'q_head_norm', 'k_head_norm', 'w_branch_mla', 'w_branch_sb',
           'w_out', 'ffn2_norm', 'ffn2_w_in', 'ffn2_w_out', 'ple_norm', 'w_ple_gate', 'w_ple_proj']
NORMS = ['ffn1_norm', 'mix_norm', 'q_latent_norm', 'kv_latent_norm', 'q_head_norm', 'k_head_norm',
         'ffn2_norm', 'ple_norm']
MATS = [n for n in WEIGHTS if n not in NORMS]
SMALL_ROWS = 48

NT_DIMS = (((1,), (1,)), ((), ()))
NN_DIMS = (((1,), (0,)), ((), ()))
TN_DIMS = (((0,), (0,)), ((), ()))


def _params(sem=None, vmem=VMEM_LIMIT):
    return pltpu.CompilerParams(dimension_semantics=sem, vmem_limit_bytes=vmem)


def _pick(n, cap):
    if n <= cap:
        return n
    best = None
    for t in range(128, cap + 1, 128):
        if n % t == 0:
            best = t
    assert best is not None, (n, cap)
    return best


def _dot(a, b, dims):
    return lax.dot_general(a, b, dims, preferred_element_type=F32)


def _matmul(a, b, *, mode, out_dtype, name, tm=None, tn=None, tk=None, res=None, alpha=1.0, cargo=None):
    if mode == 'nn':
        (m, k), (k2, n) = a.shape, b.shape
    elif mode == 'nt':
        (m, k), (n, k2) = a.shape, b.shape
    else:
        (k, m), (k2, n) = a.shape, b.shape
    assert k == k2, (name, a.shape, b.shape)
    has_res = res is not None
    tn = tn or _pick(n, 512)

    def vmem(tm_, tk_):
        io = 2 * 2 * (tm_ * tk_ + tk_ * tn) + 2 * tm_ * tn * (jnp.dtype(out_dtype).itemsize + 4 * has_res)
        return io + (4 * tm_ * tn if tk_ < k else 0)

    tries = [(tm_, tk_) for tk_ in ([tk] if tk else [k, _pick(k, 2048)])
             for tm_ in ([tm] if tm else [_pick(m, 2048), _pick(m, 1024), _pick(m, 512)])]
    tm, tk = next((c for c in tries if vmem(*c) <= MATMUL_VMEM), tries[-1])
    assert m % tm == 0 and n % tn == 0 and k % tk == 0, (name, m, n, k, tm, tn, tk)
    nk = k // tk
    dims = {'nn': NN_DIMS, 'nt': NT_DIMS, 'tn': TN_DIMS}[mode]

    def epilogue(acc, r_ref, o_ref):
        if alpha != 1.0:
            acc = acc * alpha
        if has_res:
            acc = r_ref[...] + acc
        o_ref[...] = acc.astype(out_dtype)

    grid = (m // tm, n // tn, nk)

    def body(*refs):
        steps = [pl.program_id(d) for d in range(3)]

        def compute(own):
            a_ref, b_ref = own[0], own[1]
            r_ref = own[2] if has_res else None
            o_ref = own[2 + has_res]
            if nk == 1:
                epilogue(_dot(a_ref[...], b_ref[...], dims), r_ref, o_ref)
                return
            acc_ref = own[-1]

            @pl.when(steps[2] == 0)
            def _():
                acc_ref[...] = jnp.zeros_like(acc_ref)

            acc_ref[...] += _dot(a_ref[...], b_ref[...], dims)

            @pl.when(steps[2] == nk - 1)
            def _():
                epilogue(acc_ref[...], r_ref, o_ref)

        _with_cargo(cargo, refs, 2 + has_res, 1, steps, grid, compute)

    if mode == 'tn':
        a_spec = pl.BlockSpec((tk, tm), lambda i, j, kk: (kk, i))
    else:
        a_spec = pl.BlockSpec((tm, tk), lambda i, j, kk: (i, kk))
    if mode == 'nt':
        b_spec = pl.BlockSpec((tn, tk), lambda i, j, kk: (j, kk))
    else:
        b_spec = pl.BlockSpec((tk, tn), lambda i, j, kk: (kk, j))
    o_spec = pl.BlockSpec((tm, tn), lambda i, j, kk: (i, j))
    in_specs = [a_spec, b_spec] + ([o_spec] if has_res else [])
    args = (a, b) + ((res,) if has_res else ())
    out_shape = jax.ShapeDtypeStruct((m, n), out_dtype)
    scratch = [pltpu.VMEM((tm, tn), F32)] if nk > 1 else []
    if cargo is None:
        return pl.pallas_call(
            body, name=name, grid=grid, in_specs=in_specs, out_specs=o_spec, out_shape=out_shape,
            scratch_shapes=scratch, compiler_params=_params(("parallel", "parallel", "arbitrary")),
        )(*args)
    outs = pl.pallas_call(
        body, name=name, grid=grid, in_specs=in_specs + cargo.specs(), out_specs=[o_spec] + cargo.specs(),
        out_shape=[out_shape] + cargo.out_shape(), scratch_shapes=scratch + cargo.scratch(),
        compiler_params=_params(("arbitrary", "arbitrary", "arbitrary")),
    )(*args, *cargo.srcs)
    return outs[0], list(outs[1:])


def _row_tile(t, cap=512):
    return min(t, cap)


def _rms(x, width):
    return lax.rsqrt(jnp.sum(x * x, axis=-1, keepdims=True) * (1.0 / width) + NORM_EPS)


def _rmsnorm_fwd(x, g, name, cargo=None):
    t, d = x.shape
    tr = _row_tile(t)
    grid = (t // tr,)

    def body(*refs):
        def compute(own):
            x_ref, g_ref, o_ref = own
            xv = x_ref[...]
            o_ref[...] = ((xv * _rms(xv, d)) * g_ref[...]).astype(BF16)

        _with_cargo(cargo, refs, 2, 1, [pl.program_id(0)], grid, compute)

    extra = cargo.specs() if cargo else []
    outs = pl.pallas_call(
        body, name=name, grid=grid,
        in_specs=[pl.BlockSpec((tr, d), lambda i: (i, 0)), pl.BlockSpec((1, d), lambda i: (0, 0))] + extra,
        out_specs=[pl.BlockSpec((tr, d), lambda i: (i, 0))] + extra,
        out_shape=[jax.ShapeDtypeStruct((t, d), BF16)] + (cargo.out_shape() if cargo else []),
        scratch_shapes=cargo.scratch() if cargo else [],
        compiler_params=_params(("arbitrary",) if cargo else ("parallel",)),
    )(x, g, *(cargo.srcs if cargo else []))
    return (outs[0], list(outs[1:])) if cargo else outs[0]


def _matmul_norm_bwd(a, b, x, g, dh_in, *, mode, name, out_scale, cargo=None):
    m, k = a.shape
    d = x.shape[1]
    tn = _pick(d, 512)

    def vmem(tm_):
        return 2 * 2 * (tm_ * k + k * tn) + tm_ * d * (4 + 2 * (4 + 4) + 2 * (4 + 2))

    tm = next((c for c in (_pick(m, 1024), _pick(m, 512), _pick(m, 256)) if vmem(c) <= MATMUL_VMEM), _pick(m, 256))
    grid = (m // tm, d // tn)
    dims = {'nn': NN_DIMS, 'nt': NT_DIMS}[mode]

    def body(*refs):
        steps = [pl.program_id(0), pl.program_id(1)]

        def compute(own):
            a_ref, b_ref, x_ref, g_ref, dhin_ref, dh_ref, dhb_ref, dg_ref, dn_ref = own
            for jj in range(grid[1]):
                @pl.when(steps[1] == jj)
                def _(jj=jj):
                    dn_ref[:, jj * tn:(jj + 1) * tn] = _dot(a_ref[...], b_ref[...], dims)

            @pl.when(steps[1] == grid[1] - 1)
            def _():
                xv = x_ref[...]
                dnv = dn_ref[...]
                r = _rms(xv, d)
                y = xv * r
                dy = dnv * g_ref[...]
                dh = dhin_ref[...] + r * (dy - y * (jnp.sum(dy * y, axis=-1, keepdims=True) * (1.0 / d)))
                dh_ref[...] = dh
                dhb_ref[...] = (dh * out_scale).astype(BF16)
                part = jnp.sum(dnv * y, axis=0, keepdims=True)

                @pl.when(steps[0] == 0)
                def _():
                    dg_ref[...] = part

                @pl.when(steps[0] > 0)
                def _():
                    dg_ref[...] += part

        _with_cargo(cargo, refs, 5, 3, steps, grid, compute)

    b_spec = pl.BlockSpec((k, tn), lambda i, j: (0, j)) if mode == 'nn' else pl.BlockSpec((tn, k), lambda i, j: (j, 0))
    row = pl.BlockSpec((tm, d), lambda i, j: (i, 0))
    vec = pl.BlockSpec((1, d), lambda i, j: (0, 0))
    extra = cargo.specs() if cargo else []
    outs = pl.pallas_call(
        body, name=name, grid=grid,
        in_specs=[pl.BlockSpec((tm, k), lambda i, j: (i, 0)), b_spec, row, vec, row] + extra,
        out_specs=[row, row, vec] + extra,
        out_shape=[jax.ShapeDtypeStruct((m, d), F32), jax.ShapeDtypeStruct((m, d), BF16),
                   jax.ShapeDtypeStruct((1, d), F32)] + (cargo.out_shape() if cargo else []),
        scratch_shapes=[pltpu.VMEM((tm, d), F32)] + (cargo.scratch() if cargo else []),
        compiler_params=_params(("arbitrary", "arbitrary")),
    )(a, b, x, g, dh_in, *(cargo.srcs if cargo else []))
    return (outs[0], outs[1], outs[2], list(outs[3:])) if cargo else outs


def _sigmoid(x):
    return 1.0 / (1.0 + jnp.exp(-x))


SWIGLU_CHUNK = 256
SWIGLU_COLS = 1408


def _chunks(width):
    return [slice(lo, min(lo + SWIGLU_CHUNK, width)) for lo in range(0, width, SWIGLU_CHUNK)]


def _swiglu_fwd(h, g, wt_in, name, cargo=None):
    t = h.shape[0]
    tm = _pick(t, 1024)
    grid = (t // tm, D_FF // SWIGLU_COLS)
    normed = g is None

    def body(*refs):
        steps = [pl.program_id(0), pl.program_id(1)]

        def compute(own):
            if normed:
                h_ref, wa_ref, wb_ref, a_ref, b_ref, act_ref = own
                nv = h_ref[...]
            else:
                h_ref, g_ref, wa_ref, wb_ref, n_ref, a_ref, b_ref, act_ref, n_sc = own

                @pl.when(steps[1] == 0)
                def _():
                    xv = h_ref[...]
                    n_sc[...] = ((xv * _rms(xv, D_MODEL)) * g_ref[...]).astype(BF16)
                    n_ref[...] = n_sc[...]

                nv = n_sc[...]
            for cols in _chunks(SWIGLU_COLS):
                a = _dot(nv, wa_ref[cols, :], NT_DIMS)
                b = _dot(nv, wb_ref[cols, :], NT_DIMS)
                a_ref[:, cols] = a.astype(BF16)
                b_ref[:, cols] = b.astype(BF16)
                act_ref[:, cols] = (a * _sigmoid(a) * b).astype(BF16)

        _with_cargo(cargo, refs, 3 if normed else 4, 3 if normed else 4, steps, grid, compute)

    half = D_FF // SWIGLU_COLS
    row = pl.BlockSpec((tm, D_MODEL), lambda i, j: (i, 0))
    tile = pl.BlockSpec((tm, SWIGLU_COLS), lambda i, j: (i, j))
    out = jax.ShapeDtypeStruct((t, D_FF), BF16)
    extra = cargo.specs() if cargo else []
    weights = [pl.BlockSpec((SWIGLU_COLS, D_MODEL), lambda i, j: (j, 0)),
               pl.BlockSpec((SWIGLU_COLS, D_MODEL), lambda i, j: (half + j, 0))]
    gain = [] if normed else [pl.BlockSpec((1, D_MODEL), lambda i, j: (0, 0))]
    outs = pl.pallas_call(
        body, name=name, grid=grid,
        in_specs=[row] + gain + weights + extra,
        out_specs=([] if normed else [row]) + [tile, tile, tile] + extra,
        out_shape=([] if normed else [jax.ShapeDtypeStruct((t, D_MODEL), BF16)]) + [out, out, out]
        + (cargo.out_shape() if cargo else []),
        scratch_shapes=([] if normed else [pltpu.VMEM((tm, D_MODEL), BF16)]) + (cargo.scratch() if cargo else []),
        compiler_params=_params(("arbitrary", "arbitrary")),
    )(h, *([] if normed else [g]), wt_in, wt_in, *(cargo.srcs if cargo else []))
    outs = ([h] if normed else []) + list(outs)
    return (outs[0], outs[1], outs[2], outs[3], list(outs[4:])) if cargo else outs


def _swiglu_bwd(dh, w_out, a, b, name):
    t = a.shape[0]
    tr = _row_tile(t, 512)

    def body(d_ref, w_ref, a_ref, b_ref, o_ref):
        dhv = d_ref[...]
        for cols in _chunks(D_FF):
            dv = _dot(dhv, w_ref[cols, :], NT_DIMS)
            av = a_ref[:, cols].astype(F32)
            s = _sigmoid(av)
            o_ref[:, cols] = (dv * b_ref[:, cols].astype(F32) * s * (1.0 + av * (1.0 - s))).astype(BF16)
            o_ref[:, slice(D_FF + cols.start, D_FF + cols.stop)] = (dv * av * s).astype(BF16)

    row = pl.BlockSpec((tr, D_FF), lambda i: (i, 0))
    return pl.pallas_call(
        body, name=name, grid=(t // tr,),
        in_specs=[pl.BlockSpec((tr, D_MODEL), lambda i: (i, 0)), pl.BlockSpec((D_FF, D_MODEL), lambda i: (0, 0)), row, row],
        out_specs=pl.BlockSpec((tr, 2 * D_FF), lambda i: (i, 0)),
        out_shape=jax.ShapeDtypeStruct((t, 2 * D_FF), BF16),
        compiler_params=_params(("parallel",)),
    )(dh, w_out, a, b)


def _latent_fwd(proj, gq, gkv):
    t = proj.shape[0]
    tr = _row_tile(t)

    def body(p_ref, gq_ref, gkv_ref, cq_ref, ckv_ref):
        cq = p_ref[:, P_CQ:P_CQ + Q_LORA].astype(F32)
        ckv = p_ref[:, P_CKV:P_CKV + KV_LORA].astype(F32)
        cq_ref[...] = ((cq * _rms(cq, Q_LORA)) * gq_ref[...]).astype(BF16)
        ckv_ref[...] = ((ckv * _rms(ckv, KV_LORA)) * gkv_ref[...]).astype(BF16)

    return pl.pallas_call(
        body, name="latent_fwd", grid=(t // tr,),
        in_specs=[pl.BlockSpec((tr, 1024), lambda i: (i, 0)), pl.BlockSpec((1, Q_LORA), lambda i: (0, 0)),
                  pl.BlockSpec((1, KV_LORA), lambda i: (0, 0))],
        out_specs=[pl.BlockSpec((tr, Q_LORA), lambda i: (i, 0)), pl.BlockSpec((tr, KV_LORA), lambda i: (i, 0))],
        out_shape=[jax.ShapeDtypeStruct((t, Q_LORA), BF16), jax.ShapeDtypeStruct((t, KV_LORA), BF16)],
        compiler_params=_params(("parallel",)),
    )(proj, gq, gkv)


def _latent_bwd(dcqn, dckvn, proj, dkr, gq, gkv, dgates, dsq, dsk, dsv):
    t = proj.shape[0]
    tr = _row_tile(t, 256)

    def norm_bwd(dn, x, g, width):
        r = _rms(x, width)
        y = x * r
        dy = dn * g
        dx = r * (dy - y * (jnp.sum(dy * y, axis=-1, keepdims=True) * (1.0 / width)))
        return dx, jnp.sum(dn * y, axis=0, keepdims=True)

    def body(dcq_ref, dckv_ref, p_ref, dkr_ref, gq_ref, gkv_ref, dg_ref, dsq_ref, dsk_ref, dsv_ref,
             o_ref, dgq_ref, dgkv_ref):
        i = pl.program_id(0)
        o_ref[:, P_GM:P_SBQ] = dg_ref[...]
        for at, ref in ((P_SBQ, dsq_ref), (P_SBK, dsk_ref), (P_SBV, dsv_ref)):
            o_ref[:, at:at + SB_WIDTH] = ref[...].astype(BF16)
        dcq, pq = norm_bwd(dcq_ref[...], p_ref[:, P_CQ:P_CQ + Q_LORA].astype(F32), gq_ref[...], Q_LORA)
        dckv, pkv = norm_bwd(dckv_ref[...], p_ref[:, P_CKV:P_CKV + KV_LORA].astype(F32), gkv_ref[...], KV_LORA)
        o_ref[:, P_CQ:P_CQ + Q_LORA] = dcq.astype(BF16)
        o_ref[:, P_CKV:P_CKV + KV_LORA] = dckv.astype(BF16)
        o_ref[:, P_KR:P_KR + 128] = dkr_ref[...].astype(BF16)
        o_ref[:, P_KR + 128:1024] = jnp.zeros((tr, 1024 - P_KR - 128), BF16)

        @pl.when(i == 0)
        def _():
            dgq_ref[...] = pq
            dgkv_ref[...] = pkv

        @pl.when(i > 0)
        def _():
            dgq_ref[...] += pq
            dgkv_ref[...] += pkv

    def row(w):
        return pl.BlockSpec((tr, w), lambda i: (i, 0))

    def vec(w):
        return pl.BlockSpec((1, w), lambda i: (0, 0))

    return pl.pallas_call(
        body, name="latent_bwd", grid=(t // tr,),
        in_specs=[row(Q_LORA), row(KV_LORA), row(1024), row(128), vec(Q_LORA), vec(KV_LORA),
                  row(2 * D_MODEL), row(SB_WIDTH), row(SB_WIDTH), row(SB_WIDTH)],
        out_specs=[row(PROJ_W), vec(Q_LORA), vec(KV_LORA)],
        out_shape=[jax.ShapeDtypeStruct((t, PROJ_W), BF16), jax.ShapeDtypeStruct((1, Q_LORA), F32),
                   jax.ShapeDtypeStruct((1, KV_LORA), F32)],
        compiler_params=_params(("arbitrary",)),
    )(dcqn, dckvn, proj, dkr, gq, gkv, dgates, dsq, dsk, dsv)


def _rope(y, cosf, sin_a, sin_b):
    return y * cosf + pltpu.roll(y, 112, 1) * sin_a + pltpu.roll(y, 16, 1) * sin_b


def _rope_t(d, cosf, sin_a, sin_b):
    return d * cosf + pltpu.roll(d * sin_a, 16, 1) + pltpu.roll(d * sin_b, 112, 1)


def _headprep_fwd(qraw, kvraw, proj, cosf, sin_a, sin_b, gqh, gkh):
    t = qraw.shape[0]
    tr = _row_tile(t, 256)

    def body(q_ref, kv_ref, kr_ref, c_ref, sa_ref, sb_ref, gq_ref, gk_ref, qh_ref, kh_ref, kvb_ref):
        cv, sa, sb = c_ref[...], sa_ref[...], sb_ref[...]
        kr = kr_ref[...].astype(F32)
        lane = lax.broadcasted_iota(jnp.int32, (tr, HEAD_PAD), 1)
        heads = [slice(h * HEAD_PAD, (h + 1) * HEAD_PAD) for h in range(N_HEADS)]
        xqs = [q_ref[:, cols] for cols in heads]
        kvs = [kv_ref[:, cols] for cols in heads]
        xks = [jnp.where(lane < MLA_NOPE, kvh, kr) for kvh in kvs]
        rqs = [_rms(x, MLA_QK) for x in xqs]
        rks = [_rms(x, MLA_QK) for x in xks]
        gq = gq_ref[...] * MLA_Q_SCALE
        yqs = [(x * r) * gq for x, r in zip(xqs, rqs)]
        yks = [(x * r) * gk_ref[...] for x, r in zip(xks, rks)]
        for cols, yq, yk, kvh in zip(heads, yqs, yks, kvs):
            qh_ref[:, cols] = _rope(yq, cv, sa, sb).astype(BF16)
            kh_ref[:, cols] = _rope(yk, cv, sa, sb).astype(BF16)
            kvb_ref[:, cols] = jnp.where(lane < MLA_NOPE, 1.0, kvh).astype(BF16)

    wide = pl.BlockSpec((tr, 1024), lambda i: (i, 0))
    lanes = pl.BlockSpec((tr, HEAD_PAD), lambda i: (i, 0))
    vec = pl.BlockSpec((1, HEAD_PAD), lambda i: (0, 0))
    return pl.pallas_call(
        body, name="headprep_fwd", grid=(t // tr,),
        in_specs=[wide, wide, pl.BlockSpec((tr, HEAD_PAD), lambda i: (i, P_KR // HEAD_PAD)), lanes, lanes, lanes, vec, vec],
        out_specs=[wide, wide, wide],
        out_shape=[jax.ShapeDtypeStruct((t, 1024), BF16)] * 3,
        compiler_params=_params(("parallel",)),
    )(qraw, kvraw, proj, cosf, sin_a, sin_b, gqh, gkh)


def _headprep_bwd(dqh, dkh, dvp, qraw, kvraw, proj, cosf, sin_a, sin_b, gqh, gkh):
    t = qraw.shape[0]
    tr = _row_tile(t, 256)

    def body(dq_ref, dk_ref, dv_ref, q_ref, kv_ref, kr_ref, c_ref, sa_ref, sb_ref, gq_ref, gk_ref,
             dqr_ref, dkvr_ref, dkr_ref, dgq_ref, dgk_ref):
        i = pl.program_id(0)
        cv, sa, sb = c_ref[...], sa_ref[...], sb_ref[...]
        kr = kr_ref[...].astype(F32)
        lane = lax.broadcasted_iota(jnp.int32, (tr, HEAD_PAD), 1)
        heads = [slice(h * HEAD_PAD, (h + 1) * HEAD_PAD) for h in range(N_HEADS)]
        xs = [q_ref[:, cols] for cols in heads] + [jnp.where(lane < MLA_NOPE, kv_ref[:, cols], kr) for cols in heads]
        gs = [gq_ref[...]] * N_HEADS + [gk_ref[...]] * N_HEADS
        dns = [_rope_t(ref[:, cols], cv, sa, sb) for ref in (dq_ref, dk_ref) for cols in heads]
        rs = [_rms(x, MLA_QK) for x in xs]
        ys = [x * r for x, r in zip(xs, rs)]
        dys = [dn * g for dn, g in zip(dns, gs)]
        means = [jnp.sum(dy * y, axis=-1, keepdims=True) * (1.0 / MLA_QK) for dy, y in zip(dys, ys)]
        dxs = [r * (dy - y * m) for r, dy, y, m in zip(rs, dys, ys, means)]
        parts = [jnp.sum(dn * y, axis=0, keepdims=True) for dn, y in zip(dns, ys)]
        dkr = jnp.zeros((tr, HEAD_PAD), F32)
        pq = jnp.zeros((1, HEAD_PAD), F32)
        pk = jnp.zeros((1, HEAD_PAD), F32)
        for h, cols in enumerate(heads):
            dqr_ref[:, cols] = dxs[h].astype(BF16)
            dxk = dxs[N_HEADS + h]
            dkvr_ref[:, cols] = jnp.where(lane < MLA_NOPE, dxk, dv_ref[:, cols]).astype(BF16)
            dkr = dkr + jnp.where(lane < MLA_NOPE, 0.0, dxk)
            pq = pq + parts[h]
            pk = pk + parts[N_HEADS + h]
        dkr_ref[...] = dkr

        @pl.when(i == 0)
        def _():
            dgq_ref[...] = pq
            dgk_ref[...] = pk

        @pl.when(i > 0)
        def _():
            dgq_ref[...] += pq
            dgk_ref[...] += pk

    wide = pl.BlockSpec((tr, 1024), lambda i: (i, 0))
    lanes = pl.BlockSpec((tr, HEAD_PAD), lambda i: (i, 0))
    vec = pl.BlockSpec((1, HEAD_PAD), lambda i: (0, 0))
    return pl.pallas_call(
        body, name="headprep_bwd", grid=(t // tr,),
        in_specs=[wide, wide, wide, wide, wide, pl.BlockSpec((tr, HEAD_PAD), lambda i: (i, P_KR // HEAD_PAD)),
                  lanes, lanes, lanes, vec, vec],
        out_specs=[wide, wide, lanes, vec, vec],
        out_shape=[jax.ShapeDtypeStruct((t, 1024), BF16), jax.ShapeDtypeStruct((t, 1024), BF16),
                   jax.ShapeDtypeStruct((t, HEAD_PAD), F32), jax.ShapeDtypeStruct((1, HEAD_PAD), F32),
                   jax.ShapeDtypeStruct((1, HEAD_PAD), F32)],
        compiler_params=_params(("arbitrary",)),
    )(dqh, dkh, dvp, qraw, kvraw, proj, cosf, sin_a, sin_b, gqh, gkh)


def _merge_fwd(o_mla, w_mla, o_sb, w_sb, proj):
    t = proj.shape[0]
    tr = _row_tile(t, 512)

    def body(om_ref, wm_ref, os_ref, ws_ref, gm_ref, gs_ref, o_ref, bm_ref, bs_ref):
        omv, osv = om_ref[...], os_ref[...]
        for cols in _chunks(D_MODEL):
            bm = _dot(omv, wm_ref[:, cols], NN_DIMS)
            bs = _dot(osv, ws_ref[:, cols], NN_DIMS)
            bm_ref[:, cols] = bm
            bs_ref[:, cols] = bs
            gm = _sigmoid(gm_ref[:, cols].astype(F32))
            gs = _sigmoid(gs_ref[:, cols].astype(F32))
            o_ref[:, cols] = (gm * bm + gs * bs).astype(BF16)

    row = pl.BlockSpec((tr, 1024), lambda i: (i, 0))
    f32 = jax.ShapeDtypeStruct((t, 1024), F32)
    return pl.pallas_call(
        body, name="merge_fwd", grid=(t // tr,),
        in_specs=[row, pl.BlockSpec(w_mla.shape, lambda i: (0, 0)),
                  pl.BlockSpec((tr, SB_WIDTH), lambda i: (i, 0)), pl.BlockSpec(w_sb.shape, lambda i: (0, 0)),
                  pl.BlockSpec((tr, 1024), lambda i: (i, P_GM // 1024)), pl.BlockSpec((tr, 1024), lambda i: (i, P_GS // 1024))],
        out_specs=[row, row, row], out_shape=[jax.ShapeDtypeStruct((t, 1024), BF16), f32, f32],
        compiler_params=_params(("parallel",)),
    )(o_mla, w_mla, o_sb, w_sb, proj, proj)


def _merge_bwd(dh, w_out, proj, bm, bs):
    t = proj.shape[0]
    tr = _row_tile(t, 512)

    def body(d_ref, w_ref, gm_ref, gs_ref, bm_ref, bs_ref, dbm_ref, dbs_ref, dg_ref):
        dhv = d_ref[...]
        for cols in _chunks(D_MODEL):
            dm = _dot(dhv, w_ref[cols, :], NT_DIMS)
            gm = _sigmoid(gm_ref[:, cols].astype(F32))
            gs = _sigmoid(gs_ref[:, cols].astype(F32))
            dbm_ref[:, cols] = (dm * gm).astype(BF16)
            dbs_ref[:, cols] = (dm * gs).astype(BF16)
            dg_ref[:, cols] = (dm * bm_ref[:, cols] * gm * (1.0 - gm)).astype(BF16)
            dg_ref[:, slice(D_MODEL + cols.start, D_MODEL + cols.stop)] = (dm * bs_ref[:, cols] * gs * (1.0 - gs)).astype(BF16)

    row = pl.BlockSpec((tr, 1024), lambda i: (i, 0))
    return pl.pallas_call(
        body, name="mix_out_dx", grid=(t // tr,),
        in_specs=[row, pl.BlockSpec((D_MODEL, D_MODEL), lambda i: (0, 0)),
                  pl.BlockSpec((tr, 1024), lambda i: (i, P_GM // 1024)),
                  pl.BlockSpec((tr, 1024), lambda i: (i, P_GS // 1024)), row, row],
        out_specs=[row, row, pl.BlockSpec((tr, 2048), lambda i: (i, 0))],
        out_shape=[jax.ShapeDtypeStruct((t, 1024), BF16), jax.ShapeDtypeStruct((t, 1024), BF16),
                   jax.ShapeDtypeStruct((t, 2048), BF16)],
        compiler_params=_params(("parallel",)),
    )(dh, w_out, proj, proj, bm, bs)


def _ple_loss(h3, g, w_gate, pb, w_proj, tgt):
    t = h3.shape[0]
    tr = _row_tile(t, 512)

    def body(h_ref, g_ref, wg_ref, p_ref, wp_ref, t_ref, n_ref, dh_ref, dz_ref, dp_ref, l_ref):
        i = pl.program_id(0)
        xv = h_ref[...]
        nv = ((xv * _rms(xv, D_MODEL)) * g_ref[...]).astype(BF16)
        n_ref[...] = nv
        pv = p_ref[...]
        part = jnp.zeros((1, 128), F32)
        for cols in _chunks(D_MODEL):
            pg = _sigmoid(_dot(nv, wg_ref[:, cols], NN_DIMS))
            ppv = _dot(pv, wp_ref[:, cols], NN_DIMS)
            diff = (h_ref[:, cols] + pg * ppv) - t_ref[:, cols]
            dh = diff * (1.0 / D_MODEL)
            dh_ref[:, cols] = dh
            dp_ref[:, cols] = (dh * pg).astype(BF16)
            dz_ref[:, cols] = (dh * ppv * pg * (1.0 - pg)).astype(BF16)
            sq = jnp.sum(diff * diff, axis=0, keepdims=True)
            for c in range(sq.shape[1] // 128):
                part = part + sq[:, c * 128:(c + 1) * 128]

        @pl.when(i == 0)
        def _():
            l_ref[...] = part

        @pl.when(i > 0)
        def _():
            l_ref[...] += part

    row = pl.BlockSpec((tr, 1024), lambda i: (i, 0))
    return pl.pallas_call(
        body, name="ple_loss", grid=(t // tr,),
        in_specs=[row, pl.BlockSpec((1, D_MODEL), lambda i: (0, 0)), pl.BlockSpec((D_MODEL, D_MODEL), lambda i: (0, 0)),
                  pl.BlockSpec((tr, PLE_DIM), lambda i: (i, 0)), pl.BlockSpec((PLE_DIM, D_MODEL), lambda i: (0, 0)), row],
        out_specs=[row, row, row, row, pl.BlockSpec((1, 128), lambda i: (0, 0))],
        out_shape=[jax.ShapeDtypeStruct((t, 1024), BF16), jax.ShapeDtypeStruct((t, 1024), F32),
                   jax.ShapeDtypeStruct((t, 1024), BF16), jax.ShapeDtypeStruct((t, 1024), BF16),
                   jax.ShapeDtypeStruct((1, 128), F32)],
        compiler_params=_params(("arbitrary",)),
    )(h3, g, w_gate, pb, w_proj, tgt)


ATT_BLOCK = 256
MLA_Q_SCALE = math.log2(math.e) / math.sqrt(MLA_QK)
MLA_Q_BLOCK = 512
MLA_FWD_COLS = 8
MLA_BWD_COLS = 4
SB_FWD_COLS = 4
SB_BWD_COLS = 2
SB_BLOCK = 256


def _split_bf16(x):
    hi = x.astype(BF16)
    return hi, (x - hi.astype(F32)).astype(BF16)


def _tri(kind, n):
    r = lax.broadcasted_iota(jnp.int32, (n, n), 0)
    c = lax.broadcasted_iota(jnp.int32, (n, n), 1)
    cond = {'gt': r > c, 'le': r <= c, 'lt': r < c}[kind]
    return jnp.where(cond, 1.0, 0.0).astype(BF16)


def _causal(strict, n=ATT_BLOCK):
    r = lax.broadcasted_iota(jnp.int32, (n, n), 0)
    c = lax.broadcasted_iota(jnp.int32, (n, n), 1)
    return (c < r) if strict else (c <= r)


def _below_diagonal(rows):
    r = lax.broadcasted_iota(jnp.int32, (rows, ATT_BLOCK), 0)
    c = lax.broadcasted_iota(jnp.int32, (rows, ATT_BLOCK), 1)
    return c <= r


def _lanes(c):
    return slice(c * HEAD_PAD, (c + 1) * HEAD_PAD)


def _row_block(j, n=ATT_BLOCK):
    return pl.ds(pl.multiple_of(j * n, n), n)


def _rows(ref, j, c, n=ATT_BLOCK):
    return ref[_row_block(j, n), _lanes(c)]


def _mla_fwd(qh, kh, kvb, cargo=None):
    t = qh.shape[0]
    bq = min(MLA_Q_BLOCK, t)
    per_q = bq // ATT_BLOCK
    ncol = MLA_FWD_COLS
    grid = (N_HEADS // ncol, t // bq)

    def body(*refs):
        steps = [pl.program_id(0), pl.program_id(1)]
        _with_cargo(cargo, refs, 3, 2, steps, grid, lambda own: work(steps[1], *own))

    def work(i, q_ref, k_ref, v_ref, o_ref, lse_ref):
        qs = [q_ref[:, _lanes(c)] for c in range(ncol)]

        def step(j, carry, top):
            cols = range(ncol)
            first = top or 0
            scores = [_dot(qs[c][first:], _rows(k_ref, j, c), NT_DIMS) for c in cols]
            ms, ps, alphas = [], [], []
            for c in cols:
                s = scores[c]
                if top is not None:
                    s = jnp.where(_below_diagonal(bq - first), s, -1e30)
                m_old = carry[c][0][first:]
                m_new = jnp.maximum(m_old, jnp.max(s, axis=-1, keepdims=True))
                ps.append(jnp.exp2(s - m_new).astype(BF16))
                alphas.append(jnp.exp2(m_old - m_new))
                ms.append(m_new)
            accs = [alphas[c] * carry[c][1][first:] + _dot(ps[c], _rows(v_ref, j, c), NN_DIMS) for c in cols]
            if first:
                ms = [jnp.concatenate([carry[c][0][:first], ms[c]], axis=0) for c in cols]
                accs = [jnp.concatenate([carry[c][1][:first], accs[c]], axis=0) for c in cols]
            return tuple(zip(ms, accs))

        init = tuple((jnp.full((bq, 1), -1e30, F32), jnp.zeros((bq, HEAD_PAD), F32)) for _ in range(ncol))
        carry = lax.fori_loop(0, i * per_q, lambda j, cr: step(j, cr, None), init)
        for d in range(per_q):
            carry = step(i * per_q + d, carry, d * ATT_BLOCK)
        for c, (m, acc) in enumerate(carry):
            l = acc[:, 0:1]
            o_ref[:, _lanes(c)] = (acc / l).astype(BF16)
            lse_ref[c] = m + jnp.log2(l)

    width = ncol * HEAD_PAD
    full = pl.BlockSpec((t, width), lambda h, i: (0, h))
    blk = pl.BlockSpec((bq, width), lambda h, i: (i, h))
    extra = cargo.specs() if cargo else []
    outs = pl.pallas_call(
        body, name="mla_fwd", grid=grid,
        in_specs=[blk, full, full] + extra,
        out_specs=[blk, pl.BlockSpec((ncol, bq, 1), lambda h, i: (h, i, 0))] + extra,
        out_shape=[jax.ShapeDtypeStruct((t, N_HEADS * HEAD_PAD), BF16), jax.ShapeDtypeStruct((N_HEADS, t, 1), F32)]
        + (cargo.out_shape() if cargo else []),
        scratch_shapes=cargo.scratch() if cargo else [],
        compiler_params=_params(("arbitrary", "arbitrary")),
    )(qh, kh, kvb, *(cargo.srcs if cargo else []))
    return (outs[0], outs[1], list(outs[2:])) if cargo else outs


def _mla_bwd(qh, kh, kvb, o, do, lse, cargo=None):
    t = qh.shape[0]
    bq = min(MLA_Q_BLOCK, t)
    per_q = bq // ATT_BLOCK
    ncol = MLA_BWD_COLS
    width = ncol * HEAD_PAD
    grid = (N_HEADS // ncol, t // bq)

    def body(*refs):
        steps = [pl.program_id(0), pl.program_id(1)]
        _with_cargo(cargo, refs, 6, 3, steps, grid, lambda own: work(steps[0], steps[1], *own))

    def work(h, i, q_ref, k_ref, v_ref, o_ref, do_ref, lse_ref, dq_ref, dk_hbm, dv_hbm, dk_ref, dv_ref, out_sems):

        @pl.when(i == 0)
        def _():
            dk_ref[...] = jnp.zeros_like(dk_ref)
            dv_ref[...] = jnp.zeros_like(dv_ref)

        qs = [q_ref[:, _lanes(c)] for c in range(ncol)]
        dos = [do_ref[:, _lanes(c)] for c in range(ncol)]
        deltas = [jnp.sum(dos[c].astype(F32) * o_ref[:, _lanes(c)].astype(F32), axis=-1, keepdims=True)
                  for c in range(ncol)]
        lses = [lse_ref[c] for c in range(ncol)]

        def step(j, dqs, top):
            cols = range(ncol)
            first = top or 0
            kbs = [_rows(k_ref, j, c) for c in cols]
            scores = [_dot(qs[c][first:], kbs[c], NT_DIMS) for c in cols]
            dps = [_dot(dos[c][first:], _rows(v_ref, j, c), NT_DIMS) for c in cols]
            pbs, dss = [], []
            for c in cols:
                p = jnp.exp2(scores[c] - lses[c][first:])
                if top is not None:
                    p = jnp.where(_below_diagonal(bq - first), p, 0.0)
                pbs.append(p.astype(BF16))
                dss.append((p * (dps[c] - deltas[c][first:])).astype(BF16))
            for c in cols:
                dv_ref[_row_block(j), _lanes(c)] += _dot(pbs[c], dos[c][first:], TN_DIMS)
                dk_ref[_row_block(j), _lanes(c)] += _dot(dss[c], qs[c][first:], TN_DIMS)
            new = [dqs[c][first:] + _dot(dss[c], kbs[c], NN_DIMS) for c in cols]
            if first:
                new = [jnp.concatenate([dqs[c][:first], new[c]], axis=0) for c in cols]
            return tuple(new)

        init = tuple(jnp.zeros((bq, HEAD_PAD), F32) for _ in range(ncol))
        dqs = lax.fori_loop(0, i * per_q, lambda j, cr: step(j, cr, None), init)
        for d in range(per_q):
            dqs = step(i * per_q + d, dqs, d * ATT_BLOCK)
        for c, dq in enumerate(dqs):
            dq_ref[:, _lanes(c)] = dq * (1.0 / math.sqrt(MLA_QK))

        @pl.when(i == grid[1] - 1)
        def _():
            dk_ref[...] = dk_ref[...] * math.log(2.0)
            cols = pl.ds(pl.multiple_of(h * width, width), width)
            out = [pltpu.make_async_copy(dk_ref, dk_hbm.at[:, cols], out_sems.at[0]),
                   pltpu.make_async_copy(dv_ref, dv_hbm.at[:, cols], out_sems.at[1])]
            for cp in out:
                cp.start()
            for cp in out:
                cp.wait()

    full = pl.BlockSpec((t, width), lambda h, i: (0, h))
    blk = pl.BlockSpec((bq, width), lambda h, i: (i, h))
    wide = jax.ShapeDtypeStruct((t, N_HEADS * HEAD_PAD), F32)
    extra = cargo.specs() if cargo else []
    outs = pl.pallas_call(
        body, name="mla_bwd", grid=grid,
        in_specs=[blk, full, full, blk, blk, pl.BlockSpec((ncol, bq, 1), lambda h, i: (h, i, 0))] + extra,
        out_specs=[blk, HBM_SPEC, HBM_SPEC] + extra,
        out_shape=[wide, wide, wide] + (cargo.out_shape() if cargo else []),
        scratch_shapes=[pltpu.VMEM((t, width), F32), pltpu.VMEM((t, width), F32), pltpu.SemaphoreType.DMA((2,))]
        + (cargo.scratch() if cargo else []),
        compiler_params=_params(("arbitrary", "arbitrary")),
    )(qh, kh, kvb, o, do, lse, *(cargo.srcs if cargo else []))
    return (outs[0], outs[1], outs[2], list(outs[3:])) if cargo else outs


def _head_only(x, lane, u):
    return jnp.where((lane >= u * SB_DIM) & (lane < (u + 1) * SB_DIM), x, jnp.zeros_like(x))


SB_DEAD = -104.0


def _log_sigmoids(z):
    e = jnp.exp(-jnp.abs(z))
    lg = jnp.log(1.0 + e)
    ls_pos = jnp.minimum(z, 0.0) - lg
    return ls_pos, ls_pos - z, e


def _sb_fwd(proj):
    t = proj.shape[0]
    bq, ncol = SB_BLOCK, SB_FWD_COLS
    nq = t // bq
    scale = 1.0 / math.sqrt(SB_DIM)
    pairs = SB_WIDTH // HEAD_PAD

    def body(q_ref, k_ref, v_ref, o_ref, r_ref, first_ref):
        g, i = pl.program_id(0), pl.program_id(1)
        lane = lax.broadcasted_iota(jnp.int32, (bq, HEAD_PAD), 1)
        upper = _tri('gt', bq)
        chains = [(c, u) for c in range(ncol) for u in range(2)]
        qms = [_head_only(q_ref[:, _lanes(c)], lane, u) * scale for c, u in chains]

        def step(j, carry, masked):
            ids = range(len(chains))
            zs = [_dot(qms[n], _rows(k_ref, j, chains[n][0], bq), NT_DIMS) for n in ids]
            pos, neg, parts = [], [], []
            for n in ids:
                ls_pos, ls_neg, _ = _log_sigmoids(zs[n])
                if masked:
                    ls_neg = jnp.where(_causal(True, bq), ls_neg, 0.0)
                pos.append(ls_pos)
                neg.append(ls_neg)
                parts.append(_split_bf16(ls_neg))
            suffix = [_dot(parts[n][0], upper, NN_DIMS) + _dot(parts[n][1], upper, NN_DIMS) for n in ids]
            weights = []
            for n in ids:
                a = jnp.exp(pos[n] + suffix[n] + carry[n][0])
                if masked:
                    a = jnp.where(_causal(True, bq), a, 0.0)
                weights.append(a.astype(BF16))
            return tuple((carry[n][0] + jnp.sum(neg[n], axis=-1, keepdims=True),
                          carry[n][1] + _dot(weights[n], _rows(v_ref, j, chains[n][0], bq), NN_DIMS)) for n in ids)

        init = tuple((jnp.zeros((bq, 1), F32), jnp.zeros((bq, HEAD_PAD), F32)) for _ in chains)
        carry = step(i, init, True)

        def more(state):
            s, cr = state
            live = cr[0][0]
            for n in range(1, len(chains)):
                live = jnp.maximum(live, cr[n][0])
            return jnp.logical_and(s < i, jnp.max(live) > SB_DEAD)

        walked, carry = lax.while_loop(more, lambda st: (st[0] + 1, step(i - 1 - st[0], st[1], False)),
                                       (jnp.int32(0), carry))
        first_ref[g * nq + i] = i - walked
        for n, (c, u) in enumerate(chains):
            r_ref[2 * c + u] = carry[n][0]
        for c in range(ncol):
            o_ref[:, _lanes(c)] = jnp.where(lane < SB_DIM, carry[2 * c][1], carry[2 * c + 1][1]).astype(BF16)

    width = ncol * HEAD_PAD

    def full(c0):
        return pl.BlockSpec((t, width), lambda g, i: (0, c0 // width + g))

    return pl.pallas_call(
        body, name="sb_fwd", grid=(pairs // ncol, t // bq),
        in_specs=[pl.BlockSpec((bq, width), lambda g, i: (i, P_SBQ // width + g)), full(P_SBK), full(P_SBV)],
        out_specs=[pl.BlockSpec((bq, width), lambda g, i: (i, g)),
                   pl.BlockSpec((2 * ncol, bq, 1), lambda g, i: (g, i, 0)),
                   pl.BlockSpec(memory_space=pltpu.SMEM)],
        out_shape=[jax.ShapeDtypeStruct((t, SB_WIDTH), BF16), jax.ShapeDtypeStruct((N_HEADS, t, 1), F32),
                   jax.ShapeDtypeStruct((pairs // ncol * nq,), jnp.int32)],
        compiler_params=_params(("arbitrary", "arbitrary")),
    )(proj, proj, proj)


def _sb_bwd(proj, do, rtot, first):
    t = proj.shape[0]
    bq, ncol = SB_BLOCK, SB_BWD_COLS
    nq = t // bq
    scale = 1.0 / math.sqrt(SB_DIM)
    pairs = SB_WIDTH // HEAD_PAD

    def body(first_ref, q_ref, k_ref, v_ref, do_ref, r_ref, dq_ref, dk_ref, dv_ref):
        g, i = pl.program_id(0), pl.program_id(1)

        @pl.when(i == 0)
        def _():
            dk_ref[...] = jnp.zeros_like(dk_ref)
            dv_ref[...] = jnp.zeros_like(dv_ref)

        lane = lax.broadcasted_iota(jnp.int32, (bq, HEAD_PAD), 1)
        incl = _tri('le', bq)
        excl = _tri('lt', bq)
        chains = [(c, u) for c in range(ncol) for u in range(2)]
        qms = [_head_only(q_ref[:, _lanes(c)], lane, u) * scale for c, u in chains]
        doms = [_head_only(do_ref[:, _lanes(c)], lane, u) for c, u in chains]
        rts = [r_ref[2 * c + u] for c, u in chains]

        def step(j, carry, masked):
            ids = range(len(chains))
            kbs = [_rows(k_ref, j, c, bq) for c in range(ncol)]
            zs =[_dot(qms[n], kbs[chains[n][0]], NT_DIMS) for n in ids]
            das = [_dot(doms[n], _rows(v_ref, j, chains[n][0], bq), NT_DIMS) for n in ids]
            pos, neg, sigs, parts = [], [], [], []
            for n in ids:
                ls_pos, ls_neg, e = _log_sigmoids(zs[n])
                if masked:
                    ls_neg = jnp.where(_causal(True, bq), ls_neg, 0.0)
                pos.append(ls_pos)
                neg.append(ls_neg)
                sigs.append(jnp.where(zs[n] >= 0.0, 1.0, e) * pl.reciprocal(1.0 + e, approx=True))
                parts.append(_split_bf16(ls_neg))
            prefix = [_dot(parts[n][0], incl, NN_DIMS) + _dot(parts[n][1], incl, NN_DIMS) for n in ids]
            evs, eparts, dvs = [], [], []
            for n in ids:
                a = jnp.exp(pos[n] + (rts[n] - (carry[n][0] + prefix[n])))
                if masked:
                    a = jnp.where(_causal(True, bq), a, 0.0)
                dvs.append(_dot(a.astype(BF16), doms[n], TN_DIMS))
                evs.append(a * das[n])
                eparts.append(evs[n].astype(BF16))
            before = [_dot(eparts[n], excl, NN_DIMS) for n in ids]
            out, dks = [], []
            for n in ids:
                dz = evs[n] - sigs[n] * (evs[n] + (carry[n][1] + before[n]))
                if masked:
                    dz = jnp.where(_causal(True, bq), dz, 0.0)
                dzb = dz.astype(BF16)
                dks.append(_dot(dzb, qms[n], TN_DIMS))
                out.append((carry[n][0] + jnp.sum(neg[n], axis=-1, keepdims=True),
                            carry[n][1] + jnp.sum(evs[n], axis=-1, keepdims=True),
                            carry[n][2] + _dot(dzb, kbs[chains[n][0]], NN_DIMS)))
            for c in range(ncol):
                dv_ref[_row_block(j, bq), _lanes(c)] += dvs[2 * c] + dvs[2 * c + 1]
                dk_ref[_row_block(j, bq), _lanes(c)] += dks[2 * c] + dks[2 * c + 1]
            return tuple(out)

        init = tuple((jnp.zeros((bq, 1), F32), jnp.zeros((bq, 1), F32), jnp.zeros((bq, HEAD_PAD), F32)) for _ in chains)
        start = first_ref[(g * ncol // SB_FWD_COLS) * nq + i]
        carry = lax.fori_loop(start, i, lambda j, cr: step(j, cr, False), init)
        carry = step(i, carry, True)
        for c in range(ncol):
            dq_ref[:, _lanes(c)] = jnp.where(lane < SB_DIM, carry[2 * c][2], carry[2 * c + 1][2]) * scale

    width = ncol * HEAD_PAD

    def full(c0):
        return pl.BlockSpec((t, width), lambda g, i, first: (0, c0 // width + g))

    blk = pl.BlockSpec((bq, width), lambda g, i, first: (i, g))
    acc = pl.BlockSpec((t, width), lambda g, i, first: (0, g))
    wide = jax.ShapeDtypeStruct((t, SB_WIDTH), F32)
    return pl.pallas_call(
        body, name="sb_bwd",
        grid_spec=pltpu.PrefetchScalarGridSpec(
            num_scalar_prefetch=1, grid=(pairs // ncol, nq),
            in_specs=[pl.BlockSpec((bq, width), lambda g, i, first: (i, P_SBQ // width + g)), full(P_SBK), full(P_SBV),
                      blk, pl.BlockSpec((2 * ncol, bq, 1), lambda g, i, first: (g, i, 0))],
            out_specs=[blk, acc, acc]),
        out_shape=[wide, wide, wide],
        compiler_params=_params(("arbitrary", "arbitrary")),
    )(first, proj, proj, proj, do, rtot)


def _cols_to_full(g):
    n, r, c = g.shape
    return jnp.transpose(g, (1, 0, 2)).reshape(r, n * c)


def _full_to_cols(w):
    r, c = w.shape
    return jnp.transpose(w.reshape(r, N_DEV, c // N_DEV), (1, 0, 2))


TRANSPOSED = ('ffn1_w_in', 'ffn2_w_in', 'w_in', 'w_q_up')


def _layout_weight(name, g):
    if name in ('ffn1_w_out', 'ffn2_w_out', 'w_out', 'w_ple_gate', 'ffn1_w_in', 'ffn2_w_in'):
        return g.reshape(g.shape[0] * g.shape[1], g.shape[2])
    if name == 'w_in':
        wt = g.reshape(IN_COLS, D_MODEL)
        z = lambda n: jnp.zeros((n, D_MODEL), BF16)
        return jnp.concatenate([wt[0:640], z(64), wt[640:672], z(32), z(256), wt[2208:4256], wt[672:2208]], axis=0)
    if name == 'w_q_up':
        return jnp.pad(g, ((0, 0), (0, HEAD_PAD - MLA_QK), (0, 0))).reshape(N_HEADS * HEAD_PAD, Q_LORA)
    if name == 'w_branch_mla':
        bm = _cols_to_full(g).reshape(N_HEADS, MLA_NOPE, D_MODEL)
        return jnp.pad(bm, ((0, 0), (HEAD_PAD - MLA_NOPE, 0), (0, 0))).reshape(N_HEADS * HEAD_PAD, D_MODEL)
    return _cols_to_full(g)


def _unlayout_grad(name, d):
    if name == 'w_in':
        d = jnp.concatenate([d[0:640], d[704:736], d[P_SBQ:PROJ_W], d[P_GM:P_SBQ]], axis=0)
    if name == 'w_q_up':
        return d.reshape(N_HEADS, HEAD_PAD, Q_LORA)[:, :MLA_QK, :]
    if name in ('ffn1_w_out', 'ffn2_w_out', 'w_out', 'w_ple_gate', 'ffn1_w_in', 'ffn2_w_in', 'w_in'):
        return d.reshape(N_DEV, d.shape[0] // N_DEV, d.shape[1])
    if name == 'w_branch_mla':
        d = d.reshape(N_HEADS, HEAD_PAD, D_MODEL)[:, HEAD_PAD - MLA_NOPE:, :].reshape(SB_WIDTH, D_MODEL)
    return _full_to_cols(d)


def _rope_tables(positions):
    half = MLA_ROPE // 2
    inv_freq = ROPE_BASE ** (-jnp.arange(0, MLA_ROPE, 2, dtype=F32) / MLA_ROPE)
    ang = positions.astype(F32)[:, None] * inv_freq
    cos, sin = jnp.cos(ang), jnp.sin(ang)
    t = positions.shape[0]
    ones = lambda n: jnp.ones((t, n), F32)
    zeros = lambda n: jnp.zeros((t, n), F32)
    cosf = jnp.concatenate([ones(MLA_NOPE), cos, cos, ones(HEAD_PAD - MLA_QK)], axis=1)
    sin_a = jnp.concatenate([zeros(MLA_NOPE), -sin, zeros(half), zeros(HEAD_PAD - MLA_QK)], axis=1)
    sin_b = jnp.concatenate([zeros(MLA_NOPE), zeros(half), sin, zeros(HEAD_PAD - MLA_QK)], axis=1)
    return cosf, sin_a, sin_b


def _local_step(x, p, positions, tgt, norms, plan):
    mm = _matmul
    cosf, sin_a, sin_b = _rope_tables(positions)
    pad_head = lambda g: jnp.pad(g, ((0, 0), (0, HEAD_PAD - MLA_QK)))
    gqh, gkh = pad_head(norms['q_head_norm']), pad_head(norms['k_head_norm'])
    pb = p.astype(BF16)
    w = dict(plan.first_weights())
    dw, dn = {}, {}

    def ride(host, call):
        cargo = plan.cargo(host, dw)
        res, lands = call(cargo), None
        if cargo is not None:
            *res, lands = res
            res = res[0] if len(res) == 1 else tuple(res)
        w.update(plan.landed(host, lands))
        return res

    def ffn_fwd(h, tag):
        hin, gain = h, norms[tag + '_norm']
        if tag + "_norm_fwd" in RIDES:
            hin = ride(tag + "_norm_fwd", lambda cargo: _rmsnorm_fwd(h, gain, tag + "_norm_fwd", cargo))
            gain = None
        n, a, b, act = ride(tag + "_in_fwd", lambda cargo: _swiglu_fwd(
            hin, gain, w[tag + '_w_in'], tag + "_in_fwd", cargo))
        out = ride(tag + "_out_fwd", lambda cargo: mm(
            act, w[tag + '_w_out'], mode='nn', out_dtype=F32, name=tag + "_out_fwd", res=h, alpha=0.5, cargo=cargo))
        return out, (n, a, b, act)

    h1, ffn1_saved = ffn_fwd(x, 'ffn1')
    u = _rmsnorm_fwd(h1, norms['mix_norm'], "mix_norm_fwd")
    proj = mm(u, w['w_in'], mode='nt', out_dtype=BF16, name="proj_fwd")
    cqn, ckvn = _latent_fwd(proj, norms['q_latent_norm'], norms['kv_latent_norm'])
    qraw = mm(cqn, w['w_q_up'], mode='nt', out_dtype=F32, name="q_up_fwd")
    kvraw = mm(ckvn, w['w_kv_up'], mode='nn', out_dtype=F32, name="kv_up_fwd")
    qh, kh, kvb = _headprep_fwd(qraw, kvraw, proj, cosf, sin_a, sin_b, gqh, gkh)
    o_mla, lse = ride("mla_fwd", lambda cargo: _mla_fwd(qh, kh, kvb, cargo))
    o_sb, rtot, sb_first = _sb_fwd(proj)
    merged, bm, bs = _merge_fwd(o_mla, w['w_branch_mla'], o_sb, w['w_branch_sb'], proj)
    h2 = mm(merged, w['w_out'], mode='nn', out_dtype=F32, name="mix_out_fwd", res=h1)
    h3, ffn2_saved = ffn_fwd(h2, 'ffn2')
    n3, dh4, dzg, dpp, loss_lanes = _ple_loss(h3, norms['ple_norm'], w['w_ple_gate'], pb, w['w_ple_proj'], tgt)

    dw['w_ple_gate'] = mm(n3, dzg, mode='tn', out_dtype=BF16, name="ple_gate_dw")
    dw['w_ple_proj'] = mm(pb, dpp, mode='tn', out_dtype=BF16, name="ple_proj_dw")
    dh3, dhb3, dn['ple_norm'] = _matmul_norm_bwd(
        dzg, w['w_ple_gate'], h3, norms['ple_norm'], dh4, mode='nt', name="ple_gate_dx", out_scale=0.5)

    def ffn_bwd(h, dh, dhb, saved, tag, out_scale):
        n, a, b, act = saved
        dw[tag + '_w_out'] = mm(act, dhb, mode='tn', out_dtype=BF16, name=tag + "_out_dw", tm=1408)
        dab = _swiglu_bwd(dhb, w[tag + '_w_out'], a, b, tag + "_out_dx")
        dw[tag + '_w_in'] = ride(tag + "_in_dw", lambda cargo: mm(
            dab, n, mode='tn', out_dtype=BF16, name=tag + "_in_dw", tm=1408, cargo=cargo))
        dh_prev, dhb_prev, dn[tag + '_norm'] = ride(tag + "_in_dx", lambda cargo: _matmul_norm_bwd(
            dab, w[tag + '_w_in'], h, norms[tag + '_norm'], dh, mode='nn', name=tag + "_in_dx", out_scale=out_scale,
            cargo=cargo))
        return dh_prev, dhb_prev

    dh2, dhb2 = ffn_bwd(h2, dh3, dhb3, ffn2_saved, 'ffn2', 1.0)
    dw['w_out'] = mm(merged, dhb2, mode='tn', out_dtype=BF16, name="mix_out_dw")
    dbm, dbs, dgates = _merge_bwd(dhb2, w['w_out'], proj, bm, bs)
    dw['w_branch_mla'] = mm(o_mla, dbm, mode='tn', out_dtype=BF16, name="branch_mla_dw")
    dw['w_branch_sb'] = mm(o_sb, dbs, mode='tn', out_dtype=BF16, name="branch_sb_dw")
    do_mla = mm(dbm, w['w_branch_mla'], mode='nt', out_dtype=BF16, name="branch_mla_dx")
    do_sb = mm(dbs, w['w_branch_sb'], mode='nt', out_dtype=BF16, name="branch_sb_dx")
    dqh, dkh, dvp = ride("mla_bwd", lambda cargo: _mla_bwd(qh, kh, kvb, o_mla, do_mla, lse, cargo))
    dsq, dsk, dsv = _sb_bwd(proj, do_sb, rtot, sb_first)
    dqraw, dkvraw, dkr, dgq, dgk = _headprep_bwd(dqh, dkh, dvp, qraw, kvraw, proj, cosf, sin_a, sin_b, gqh, gkh)
    dn['q_head_norm'], dn['k_head_norm'] = dgq[:, :MLA_QK], dgk[:, :MLA_QK]
    dw['w_q_up'] = mm(dqraw, cqn, mode='tn', out_dtype=BF16, name="q_up_dw")
    dw['w_kv_up'] = mm(ckvn, dkvraw, mode='tn', out_dtype=BF16, name="kv_up_dw")
    dcqn = mm(dqraw, w['w_q_up'], mode='nn', out_dtype=F32, name="q_up_dx")
    dckvn = mm(dkvraw, w['w_kv_up'], mode='nt', out_dtype=F32, name="kv_up_dx")
    dproj, dn['q_latent_norm'], dn['kv_latent_norm'] = _latent_bwd(
        dcqn, dckvn, proj, dkr, norms['q_latent_norm'], norms['kv_latent_norm'], dgates, dsq, dsk, dsv)
    dw['w_in'] = ride("proj_dw", lambda cargo: mm(dproj, u, mode='tn', out_dtype=BF16, name="proj_dw", tm=1536, cargo=cargo))
    dh1, dhb1, dn['mix_norm'] = ride("proj_dx", lambda cargo: _matmul_norm_bwd(
        dproj, w['w_in'], h1, norms['mix_norm'], dh2, mode='nn', name="proj_dx", out_scale=0.5, cargo=cargo))
    dx, _ = ffn_bwd(x, dh1, dhb1, ffn1_saved, 'ffn1', 1.0)
    return dx, loss_lanes, dw, dn


MESH = pl.DeviceIdType.MESH
HBM_SPEC = pl.BlockSpec(memory_space=pl.ANY)


def _position():
    return lax.axis_index("x"), lax.axis_index("y"), lax.axis_index("c")


def _index(px, py, pc):
    return 4 * px + 2 * py + pc


def _all_gather(shards):
    n = len(shards)

    def body(*refs):
        ins, outs = refs[:n], refs[n:2 * n]
        send_sems, recv_sems, local_sems = refs[2 * n:]
        x, y, c = _position()
        me, sibling = (x, y, c), (x, y, 1 - c)
        chips = [(1 - x, y), (x, 1 - y), (1 - x, 1 - y)]

        def copy(a, k, block, to, own=False):
            dst = outs[a].at[_index(*block)]
            return pltpu.make_async_remote_copy(
                src_ref=ins[a] if own else dst, dst_ref=dst,
                send_sem=send_sems.at[a, k], recv_sem=recv_sems.at[a, k], device_id=to, device_id_type=MESH)

        mine = [pltpu.make_async_copy(ins[a], outs[a].at[_index(*me)], local_sems.at[a]) for a in range(n)]
        for cp in mine:
            cp.start()
        first = []
        for a in range(n):
            first.append(copy(a, 0, me, sibling, own=True))
            first += [copy(a, 1 + j, me, (*chip, c), own=True) for j, chip in enumerate(chips)]
        for cp in first:
            cp.start()
        passed = []
        for j, chip in enumerate(chips):
            for a in range(n):
                copy(a, 1 + j, (*chip, c), me).wait_recv()
                fwd = copy(a, 4 + j, (*chip, c), sibling)
                fwd.start()
                passed.append(fwd)
        for a in range(n):
            copy(a, 0, sibling, me).wait_recv()
            for j, chip in enumerate(chips):
                copy(a, 4 + j, (*chip, 1 - c), me).wait_recv()
        for cp in first + passed:
            cp.wait_send()
        for cp in mine:
            cp.wait()

    return pl.pallas_call(
        body, name="weights_all_gather",
        in_specs=[HBM_SPEC] * n, out_specs=[HBM_SPEC] * n,
        out_shape=[jax.ShapeDtypeStruct((N_DEV,) + s.shape, s.dtype) for s in shards],
        scratch_shapes=[pltpu.SemaphoreType.DMA((n, 7)), pltpu.SemaphoreType.DMA((n, 7)), pltpu.SemaphoreType.DMA((n,))],
    )(*shards)


def _exchange(parts):
    n = len(parts)
    masks = [(mx, my, mc) for mx in (0, 1) for my in (0, 1) for mc in (0, 1)][1:]

    def body(*refs):
        ins, outs = refs[:n], refs[n:2 * n]
        send_sems, recv_sems, local_sems = refs[2 * n:]
        x, y, c = _position()
        me = _index(x, y, c)

        def peer_of(mask):
            mx, my, mc = mask
            return (x + mx - 2 * x * mx, y + my - 2 * y * my, c + mc - 2 * c * mc)

        def copy(a, k):
            peer = peer_of(masks[k])
            return pltpu.make_async_remote_copy(
                src_ref=ins[a].at[_index(*peer)], dst_ref=outs[a].at[me],
                send_sem=send_sems.at[a, k], recv_sem=recv_sems.at[a, k], device_id=peer, device_id_type=MESH)

        def landed(a, k):
            peer = peer_of(masks[k])
            return pltpu.make_async_remote_copy(
                src_ref=ins[a].at[me], dst_ref=outs[a].at[_index(*peer)],
                send_sem=send_sems.at[a, k], recv_sem=recv_sems.at[a, k], device_id=peer, device_id_type=MESH)

        mine = [pltpu.make_async_copy(ins[a].at[me], outs[a].at[me], local_sems.at[a]) for a in range(n)]
        for cp in mine:
            cp.start()
        sent = [copy(a, k) for k in range(7) for a in range(n)]
        for cp in sent:
            cp.start()
        for k in range(7):
            for a in range(n):
                landed(a, k).wait_recv()
        for cp in sent:
            cp.wait_send()
        for cp in mine:
            cp.wait()

    return pl.pallas_call(
        body, name="grads_exchange",
        in_specs=[HBM_SPEC] * n, out_specs=[HBM_SPEC] * n,
        out_shape=[jax.ShapeDtypeStruct(s.shape, s.dtype) for s in parts],
        scratch_shapes=[pltpu.SemaphoreType.DMA((n, 7)), pltpu.SemaphoreType.DMA((n, 7)), pltpu.SemaphoreType.DMA((n,))],
    )(*parts)


PEER_MASKS = [(mx, my, mc) for mx in (0, 1) for my in (0, 1) for mc in (0, 1)][1:]


def _peer(mask):
    x, y, c = _position()
    mx, my, mc = mask
    return (x + mx - 2 * x * mx, y + my - 2 * y * my, c + mc - 2 * c * mc)


class _Cargo:
    def __init__(self, srcs, scatter):
        self.srcs, self.scatter, self.n = list(srcs), scatter, len(srcs)

    def specs(self):
        return [HBM_SPEC] * self.n

    def out_shape(self):
        return [jax.ShapeDtypeStruct(s.shape if self.scatter else (N_DEV,) + s.shape, s.dtype) for s in self.srcs]

    def scratch(self):
        per_copy = pltpu.SemaphoreType.DMA((self.n, len(PEER_MASKS)))
        return [per_copy, per_copy, pltpu.SemaphoreType.DMA((self.n,))]

    def _mine(self, src_refs, a, to):
        return src_refs[a].at[to] if self.scatter else src_refs[a]

    def _shard_copy(self, src_refs, land_refs, sems, a, k, block, to, own=False):
        dst = land_refs[a].at[_index(*block)]
        return pltpu.make_async_remote_copy(
            src_ref=src_refs[a] if own else dst, dst_ref=dst,
            send_sem=sems[0].at[a, k], recv_sem=sems[1].at[a, k], device_id=to, device_id_type=MESH)

    def _first_hops(self, src_refs, land_refs, sems):
        x, y, c = _position()
        chips = [(1 - x, y), (x, 1 - y), (1 - x, 1 - y)]
        hops = []
        for a in range(self.n):
            hops.append(self._shard_copy(src_refs, land_refs, sems, a, 0, (x, y, c), (x, y, 1 - c), own=True))
            hops += [self._shard_copy(src_refs, land_refs, sems, a, 1 + j, (x, y, c), (*chip, c), own=True)
                     for j, chip in enumerate(chips)]
        return hops, chips

    def start(self, src_refs, land_refs, sems):
        send, recv, local = sems
        me = _index(*_position())
        for a in range(self.n):
            pltpu.make_async_copy(self._mine(src_refs, a, me), land_refs[a].at[me], local.at[a]).start()
        if not self.scatter:
            for cp in self._first_hops(src_refs, land_refs, sems)[0]:
                cp.start()
            return
        for k, mask in enumerate(PEER_MASKS):
            peer = _peer(mask)
            for a in range(self.n):
                pltpu.make_async_remote_copy(
                    src_ref=self._mine(src_refs, a, _index(*peer)), dst_ref=land_refs[a].at[me],
                    send_sem=send.at[a, k], recv_sem=recv.at[a, k], device_id=peer, device_id_type=MESH).start()

    def _wait_gathered(self, src_refs, land_refs, sems):
        x, y, c = _position()
        me, sibling = (x, y, c), (x, y, 1 - c)
        first, chips = self._first_hops(src_refs, land_refs, sems)
        passed = []
        for j, chip in enumerate(chips):
            for a in range(self.n):
                self._shard_copy(src_refs, land_refs, sems, a, 1 + j, (*chip, c), me).wait_recv()
                passed.append(self._shard_copy(src_refs, land_refs, sems, a, 4 + j, (*chip, c), sibling))
                passed[-1].start()
        for a in range(self.n):
            self._shard_copy(src_refs, land_refs, sems, a, 0, sibling, me).wait_recv()
            for j, chip in enumerate(chips):
                self._shard_copy(src_refs, land_refs, sems, a, 4 + j, (*chip, 1 - c), me).wait_recv()
        for cp in first + passed:
            cp.wait_send()

    def wait(self, src_refs, land_refs, sems):
        send, recv, local = sems
        me = _index(*_position())
        if not self.scatter:
            self._wait_gathered(src_refs, land_refs, sems)
        for k, mask in enumerate(PEER_MASKS if self.scatter else []):
            peer = _peer(mask)
            there = _index(*peer)
            for a in range(self.n):
                pltpu.make_async_remote_copy(
                    src_ref=self._mine(src_refs, a, me), dst_ref=land_refs[a].at[there],
                    send_sem=send.at[a, k], recv_sem=recv.at[a, k], device_id=peer, device_id_type=MESH).wait_recv()
                pltpu.make_async_remote_copy(
                    src_ref=self._mine(src_refs, a, there), dst_ref=land_refs[a].at[me],
                    send_sem=send.at[a, k], recv_sem=recv.at[a, k], device_id=peer, device_id_type=MESH).wait_send()
        for a in range(self.n):
            pltpu.make_async_copy(self._mine(src_refs, a, me), land_refs[a].at[me], local.at[a]).wait()


def _with_cargo(cargo, refs, n_in, n_out, steps, counts, compute):
    if cargo is None:
        compute(refs)
        return
    n = cargo.n
    src_refs = refs[n_in:n_in + n]
    land_refs = refs[n_in + n + n_out:n_in + 2 * n + n_out]
    sems = refs[-3:]
    first = functools.reduce(jnp.logical_and, [s == 0 for s in steps])
    last = functools.reduce(jnp.logical_and, [s == c - 1 for s, c in zip(steps, counts)])

    @pl.when(first)
    def _():
        cargo.start(src_refs, land_refs, sems)

    compute(refs[:n_in] + refs[n_in + n:n_in + n + n_out] + refs[n_in + 2 * n + n_out:-3])

    @pl.when(last)
    def _():
        cargo.wait(src_refs, land_refs, sems)


def _adamw(parts, w, m, v, name):
    r, c = w.shape
    tr = next((t for t in (512, 384, 352, 256, 128) if r % t == 0), r) if r > 512 else r
    tc = c if tr < r or r <= 512 else 256
    assert r % tr == 0 and c % tc == 0
    bc1 = 1.0 - ADAM_B1 ** ADAM_STEP
    bc2 = 1.0 - ADAM_B2 ** ADAM_STEP

    def body(p_ref, w_ref, m_ref, v_ref, g_ref, d_ref, nm_ref, nv_ref):
        g = p_ref[0].astype(F32)
        for s in range(1, N_DEV):
            g = g + p_ref[s].astype(F32)
        nm = ADAM_B1 * m_ref[...] + (1.0 - ADAM_B1) * g
        nv = ADAM_B2 * v_ref[...] + (1.0 - ADAM_B2) * (g * g)
        g_ref[...] = g
        nm_ref[...] = nm
        nv_ref[...] = nv
        d_ref[...] = -ADAM_LR * ((nm / bc1) / (jnp.sqrt(nv / bc2) + ADAM_EPS) + ADAM_WD * w_ref[...])

    tile = pl.BlockSpec((tr, tc), lambda i, j: (i, j))
    out = jax.ShapeDtypeStruct((r, c), F32)
    return pl.pallas_call(
        body, name=name, grid=(r // tr, c // tc),
        in_specs=[pl.BlockSpec((N_DEV, tr, tc), lambda i, j: (0, i, j)), tile, tile, tile],
        out_specs=[tile] * 4, out_shape=[out] * 4,
        compiler_params=_params(("parallel", "parallel")),
    )(parts, w, m, v)


GATHER_FIRST = []
RIDES = {
    'ffn1_norm_fwd': ('weights', ['ffn1_w_in']),
    'ffn1_in_fwd': ('weights', ['ffn1_w_out', 'w_in']),
    'ffn1_out_fwd': ('weights', ['w_q_up', 'w_kv_up', 'w_branch_mla', 'w_branch_sb', 'w_out']),
    'mla_fwd': ('weights', ['ffn2_w_in', 'ffn2_w_out', 'w_ple_gate', 'w_ple_proj']),
    'mla_bwd': ('grads', ['w_ple_gate', 'w_ple_proj', 'ffn2_w_out', 'ffn2_w_in', 'w_out', 'w_branch_mla', 'w_branch_sb']),
    'proj_dw': ('grads', ['w_q_up', 'w_kv_up']),
    'proj_dx': ('grads', ['w_in']),
    'ffn1_in_dw': ('grads', ['ffn1_w_out']),
    'ffn1_in_dx': ('grads', ['ffn1_w_in']),
}


class _Plan:
    def __init__(self, shards):
        self.shards = shards
        self.received = {}

    def first_weights(self):
        if not GATHER_FIRST:
            return {}
        gathered = _all_gather([self.shards[n] for n in GATHER_FIRST])
        return {n: _layout_weight(n, g) for n, g in zip(GATHER_FIRST, gathered)}

    def cargo(self, host, dw):
        if host not in RIDES:
            return None
        kind, names = RIDES[host]
        if kind == 'weights':
            return _Cargo([self.shards[n] for n in names], False)
        return _Cargo([_unlayout_grad(n, dw.pop(n)) for n in names], True)

    def landed(self, host, lands):
        if host not in RIDES:
            return {}
        kind, names = RIDES[host]
        if kind == 'weights':
            return {n: _layout_weight(n, land) for n, land in zip(names, lands)}
        self.received.update(zip(names, lands))
        return {}


def _pack_small(vecs):
    flat = jnp.concatenate([v.reshape(-1) for v in vecs])
    return jnp.pad(flat, (0, SMALL_ROWS * 128 - flat.shape[0])).reshape(SMALL_ROWS, 128)


def _unpack_small(packed, sizes):
    flat = packed.reshape(-1)
    out, at = [], 0
    for n in sizes:
        out.append(flat[at:at + n])
        at += n
    return out


def kernel(x, p, positions, ffn1_norm, ffn1_w_in, ffn1_w_out, mix_norm, w_in, q_latent_norm, w_q_up, kv_latent_norm, w_kv_up, q_head_norm, k_head_norm, w_branch_mla, w_branch_sb, w_out, ffn2_norm, ffn2_w_in, ffn2_w_out, ple_norm, w_ple_gate, w_ple_proj, loss_target, m_ffn1_norm, m_ffn1_w_in, m_ffn1_w_out, m_mix_norm, m_w_in, m_q_latent_norm, m_w_q_up, m_kv_latent_norm, m_w_kv_up, m_q_head_norm, m_k_head_norm, m_w_branch_mla, m_w_branch_sb, m_w_out, m_ffn2_norm, m_ffn2_w_in, m_ffn2_w_out, m_ple_norm, m_w_ple_gate, m_w_ple_proj, v_ffn1_norm, v_ffn1_w_in, v_ffn1_w_out, v_mix_norm, v_w_in, v_q_latent_norm, v_w_q_up, v_kv_latent_norm, v_w_kv_up, v_q_head_norm, v_k_head_norm, v_w_branch_mla, v_w_branch_sb, v_w_out, v_ffn2_norm, v_ffn2_w_in, v_ffn2_w_out, v_ple_norm, v_w_ple_gate, v_w_ple_proj):
    given = dict(locals())
    wts = {n: given[n] for n in WEIGHTS}
    mom = {n: given['m_' + n] for n in WEIGHTS}
    var = {n: given['v_' + n] for n in WEIGHTS}

    def local(a, n):
        return jnp.swapaxes(a[0], 0, 1) if n in TRANSPOSED else a[0]

    plan = _Plan({n: local(wts[n], n).astype(BF16) for n in MATS})
    norms = {n: wts[n] for n in NORMS}
    dx, loss_lanes, dw, dn = _local_step(x[0], p[0, 0], positions[0], loss_target[0], norms, plan)
    assert not dw

    out = {}
    for n in MATS:
        res = _adamw(plan.received[n], local(wts[n], n), local(mom[n], n), local(var[n], n), "adamw_" + n)
        out[n] = [local(r[None], n)[None] for r in res]
    small = _pack_small([dn[n] for n in NORMS] + [0.5 / D_MODEL * jnp.sum(loss_lanes)[None]])
    small_parts = _exchange([jnp.broadcast_to(small[None], (N_DEV, SMALL_ROWS, 128))])[0]
    sizes = [wts[n].shape[1] for n in NORMS]
    pack = lambda d: _pack_small([d[n] for n in NORMS])
    small_res = _adamw(small_parts, pack(wts), pack(mom), pack(var), "adamw_norms")
    loss = small_res[0].reshape(-1)[sum(sizes)]
    for i, res in enumerate(small_res):
        for n, vec in zip(NORMS, _unpack_small(res, sizes)):
            out.setdefault(n, [None] * 4)[i] = vec[None]

    return (loss, dx[None], *[out[n][0] for n in WEIGHTS], *[out[n][1] for n in WEIGHTS],
            *[out[n][2] for n in WEIGHTS], *[out[n][3] for n in WEIGHTS])
```

```python
import functools
import math

import jax
import jax.numpy as jnp
from jax import lax
from jax.experimental import pallas as pl
from jax.experimental.pallas import tpu as pltpu

F32 = jnp.float32
BF16 = jnp.bfloat16

N_DEV = 8
D_MODEL = 1024
D_FF = 2816
PLE_DIM = 256
NORM_EPS = 1e-6
N_HEADS = 8
HEAD_PAD = 128
MLA_NOPE = 64
MLA_ROPE = 32
MLA_QK = 96
Q_LORA = 384
KV_LORA = 256
SB_DIM = 64
SB_WIDTH = 512
ROPE_BASE = 10000.0
IN_COLS = 4256

PROJ_W = 4608
P_CQ, P_CKV, P_KR, P_GM, P_GS, P_SBQ, P_SBK, P_SBV = 0, 384, 640, 1024, 2048, 3072, 3584, 4096

ADAM_LR, ADAM_B1, ADAM_B2, ADAM_EPS, ADAM_WD, ADAM_STEP = 0.001, 0.9, 0.999, 1e-08, 0.01, 10

VMEM_LIMIT = 52 * 1024 * 1024
MATMUL_VMEM = 40 * 1024 * 1024

WEIGHTS = ['ffn1_norm', 'ffn1_w_in', 'ffn1_w_out', 'mix_norm', 'w_in', 'q_latent_norm', 'w_q_up',
           'kv_latent_norm', 'w_kv_up', 'q_head_norm', 'k_head_norm', 'w_branch_mla', 'w_branch_sb',
           'w_out', 'ffn2_norm', 'ffn2_w_in', 'ffn2_w_out', 'ple_norm', 'w_ple_gate', 'w_ple_proj']
NORMS = ['ffn1_norm', 'mix_norm', 'q_latent_norm', 'kv_latent_norm', 'q_head_norm', 'k_head_norm',
         'ffn2_norm', 'ple_norm']
MATS = [n for n in WEIGHTS if n not in NORMS]
SMALL_ROWS = 48

NT_DIMS = (((1,), (1,)), ((), ()))
NN_DIMS = (((1,), (0,)), ((), ()))
TN_DIMS = (((0,), (0,)), ((), ()))


def _params(sem=None, vmem=VMEM_LIMIT):
    return pltpu.CompilerParams(dimension_semantics=sem, vmem_limit_bytes=vmem)


def _pick(n, cap):
    if n <= cap:
        return n
    best = None
    for t in range(128, cap + 1, 128):
        if n % t == 0:
            best = t
    assert best is not None, (n, cap)
    return best


def _dot(a, b, dims):
    return lax.dot_general(a, b, dims, preferred_element_type=F32)


def _matmul(a, b, *, mode, out_dtype, name, tm=None, tn=None, tk=None, res=None, alpha=1.0, cargo=None):
    if mode == 'nn':
        (m, k), (k2, n) = a.shape, b.shape
    elif mode == 'nt':
        (m, k), (n, k2) = a.shape, b.shape
    else:
        (k, m), (k2, n) = a.shape, b.shape
    assert k == k2, (name, a.shape, b.shape)
    has_res = res is not None
    tn = tn or _pick(n, 512)

    def vmem(tm_, tk_):
        io = 2 * 2 * (tm_ * tk_ + tk_ * tn) + 2 * tm_ * tn * (jnp.dtype(out_dtype).itemsize + 4 * has_res)
        return io + (4 * tm_ * tn if tk_ < k else 0)

    tries = [(tm_, tk_) for tk_ in ([tk] if tk else [k, _pick(k, 2048)])
             for tm_ in ([tm] if tm else [_pick(m, 2048), _pick(m, 1024), _pick(m, 512)])]
    tm, tk = next((c for c in tries if vmem(*c) <= MATMUL_VMEM), tries[-1])
    assert m % tm == 0 and n % tn == 0 and k % tk == 0, (name, m, n, k, tm, tn, tk)
    nk = k // tk
    dims = {'nn': NN_DIMS, 'nt': NT_DIMS, 'tn': TN_DIMS}[mode]

    def epilogue(acc, r_ref, o_ref):
        if alpha != 1.0:
            acc = acc * alpha
        if has_res:
            acc = r_ref[...] + acc
        o_ref[...] = acc.astype(out_dtype)

    grid = (m // tm, n // tn, nk)

    def body(*refs):
        steps = [pl.program_id(d) for d in range(3)]

        def compute(own):
            a_ref, b_ref = own[0], own[1]
            r_ref = own[2] if has_res else None
            o_ref = own[2 + has_res]
            if nk == 1:
                epilogue(_dot(a_ref[...], b_ref[...], dims), r_ref, o_ref)
                return
            acc_ref = own[-1]

            @pl.when(steps[2] == 0)
            def _():
                acc_ref[...] = jnp.zeros_like(acc_ref)

            acc_ref[...] += _dot(a_ref[...], b_ref[...], dims)

            @pl.when(steps[2] == nk - 1)
            def _():
                epilogue(acc_ref[...], r_ref, o_ref)

        _with_cargo(cargo, refs, 2 + has_res, 1, steps, grid, compute)

    if mode == 'tn':
        a_spec = pl.BlockSpec((tk, tm), lambda i, j, kk: (kk, i))
    else:
        a_spec = pl.BlockSpec((tm, tk), lambda i, j, kk: (i, kk))
    if mode == 'nt':
        b_spec = pl.BlockSpec((tn, tk), lambda i, j, kk: (j, kk))
    else:
        b_spec = pl.BlockSpec((tk, tn), lambda i, j, kk: (kk, j))
    o_spec = pl.BlockSpec((tm, tn), lambda i, j, kk: (i, j))
    in_specs = [a_spec, b_spec] + ([o_spec] if has_res else [])
    args = (a, b) + ((res,) if has_res else ())
    out_shape = jax.ShapeDtypeStruct((m, n), out_dtype)
    scratch = [pltpu.VMEM((tm, tn), F32)] if nk > 1 else []
    if cargo is None:
        return pl.pallas_call(
            body, name=name, grid=grid, in_specs=in_specs, out_specs=o_spec, out_shape=out_shape,
            scratch_shapes=scratch, compiler_params=_params(("parallel", "parallel", "arbitrary")),
        )(*args)
    outs = pl.pallas_call(
        body, name=name, grid=grid, in_specs=in_specs + cargo.specs(), out_specs=[o_spec] + cargo.specs(),
        out_shape=[out_shape] + cargo.out_shape(), scratch_shapes=scratch + cargo.scratch(),
        compiler_params=_params(("arbitrary", "arbitrary", "arbitrary")),
    )(*args, *cargo.srcs)
    return outs[0], list(outs[1:])


def _row_tile(t, cap=512):
    return min(t, cap)


def _rms(x, width):
    return lax.rsqrt(jnp.sum(x * x, axis=-1, keepdims=True) * (1.0 / width) + NORM_EPS)


def _rmsnorm_fwd(x, g, name, cargo=None):
    t, d = x.shape
    tr = _row_tile(t)
    grid = (t // tr,)

    def body(*refs):
        def compute(own):
            x_ref, g_ref, o_ref = own
            xv = x_ref[...]
            o_ref[...] = ((xv * _rms(xv, d)) * g_ref[...]).astype(BF16)

        _with_cargo(cargo, refs, 2, 1, [pl.program_id(0)], grid, compute)

    extra = cargo.specs() if cargo else []
    outs = pl.pallas_call(
        body, name=name, grid=grid,
        in_specs=[pl.BlockSpec((tr, d), lambda i: (i, 0)), pl.BlockSpec((1, d), lambda i: (0, 0))] + extra,
        out_specs=[pl.BlockSpec((tr, d), lambda i: (i, 0))] + extra,
        out_shape=[jax.ShapeDtypeStruct((t, d), BF16)] + (cargo.out_shape() if cargo else []),
        scratch_shapes=cargo.scratch() if cargo else [],
        compiler_params=_params(("arbitrary",) if cargo else ("parallel",)),
    )(x, g, *(cargo.srcs if cargo else []))
    return (outs[0], list(outs[1:])) if cargo else outs[0]


def _matmul_norm_bwd(a, b, x, g, dh_in, *, mode, name, out_scale, cargo=None):
    m, k = a.shape
    d = x.shape[1]
    tn = _pick(d, 512)

    def vmem(tm_):
        return 2 * 2 * (tm_ * k + k * tn) + tm_ * d * (4 + 2 * (4 + 4) + 2 * (4 + 2))

    tm = next((c for c in (_pick(m, 1024), _pick(m, 512), _pick(m, 256)) if vmem(c) <= MATMUL_VMEM), _pick(m, 256))
    grid = (m // tm, d // tn)
    dims = {'nn': NN_DIMS, 'nt': NT_DIMS}[mode]

    def body(*refs):
        steps = [pl.program_id(0), pl.program_id(1)]

        def compute(own):
            a_ref, b_ref, x_ref, g_ref, dhin_ref, dh_ref, dhb_ref, dg_ref, dn_ref = own
            for jj in range(grid[1]):
                @pl.when(steps[1] == jj)
                def _(jj=jj):
                    dn_ref[:, jj * tn:(jj + 1) * tn] = _dot(a_ref[...], b_ref[...], dims)

            @pl.when(steps[1] == grid[1] - 1)
            def _():
                xv = x_ref[...]
                dnv = dn_ref[...]
                r = _rms(xv, d)
                y = xv * r
                dy = dnv * g_ref[...]
                dh = dhin_ref[...] + r * (dy - y * (jnp.sum(dy * y, axis=-1, keepdims=True) * (1.0 / d)))
                dh_ref[...] = dh
                dhb_ref[...] = (dh * out_scale).astype(BF16)
                part = jnp.sum(dnv * y, axis=0, keepdims=True)

                @pl.when(steps[0] == 0)
                def _():
                    dg_ref[...] = part

                @pl.when(steps[0] > 0)
                def _():
                    dg_ref[...] += part

        _with_cargo(cargo, refs, 5, 3, steps, grid, compute)

    b_spec = pl.BlockSpec((k, tn), lambda i, j: (0, j)) if mode == 'nn' else pl.BlockSpec((tn, k), lambda i, j: (j, 0))
    row = pl.BlockSpec((tm, d), lambda i, j: (i, 0))
    vec = pl.BlockSpec((1, d), lambda i, j: (0, 0))
    extra = cargo.specs() if cargo else []
    outs = pl.pallas_call(
        body, name=name, grid=grid,
        in_specs=[pl.BlockSpec((tm, k), lambda i, j: (i, 0)), b_spec, row, vec, row] + extra,
        out_specs=[row, row, vec] + extra,
        out_shape=[jax.ShapeDtypeStruct((m, d), F32), jax.ShapeDtypeStruct((m, d), BF16),
                   jax.ShapeDtypeStruct((1, d), F32)] + (cargo.out_shape() if cargo else []),
        scratch_shapes=[pltpu.VMEM((tm, d), F32)] + (cargo.scratch() if cargo else []),
        compiler_params=_params(("arbitrary", "arbitrary")),
    )(a, b, x, g, dh_in, *(cargo.srcs if cargo else []))
    return (outs[0], outs[1], outs[2], list(outs[3:])) if cargo else outs


def _sigmoid(x):
    return 1.0 / (1.0 + jnp.exp(-x))


SWIGLU_CHUNK = 256
SWIGLU_COLS = 1408


def _chunks(width):
    return [slice(lo, min(lo + SWIGLU_CHUNK, width)) for lo in range(0, width, SWIGLU_CHUNK)]


def _swiglu_fwd(h, g, wt_in, name, cargo=None):
    t = h.shape[0]
    tm = _pick(t, 1024)
    grid = (t // tm, D_FF // SWIGLU_COLS)
    normed = g is None

    def body(*refs):
        steps = [pl.program_id(0), pl.program_id(1)]

        def compute(own):
            if normed:
                h_ref, wa_ref, wb_ref, a_ref, b_ref, act_ref = own
                nv = h_ref[...]
            else:
                h_ref, g_ref, wa_ref, wb_ref, n_ref, a_ref, b_ref, act_ref, n_sc = own

                @pl.when(steps[1] == 0)
                def _():
                    xv = h_ref[...]
                    n_sc[...] = ((xv * _rms(xv, D_MODEL)) * g_ref[...]).astype(BF16)
                    n_ref[...] = n_sc[...]

                nv = n_sc[...]
            for cols in _chunks(SWIGLU_COLS):
                a = _dot(nv, wa_ref[cols, :], NT_DIMS)
                b = _dot(nv, wb_ref[cols, :], NT_DIMS)
                a_ref[:, cols] = a.astype(BF16)
                b_ref[:, cols] = b.astype(BF16)
                act_ref[:, cols] = (a * _sigmoid(a) * b).astype(BF16)

        _with_cargo(cargo, refs, 3 if normed else 4, 3 if normed else 4, steps, grid, compute)

    half = D_FF // SWIGLU_COLS
    row = pl.BlockSpec((tm, D_MODEL), lambda i, j: (i, 0))
    tile = pl.BlockSpec((tm, SWIGLU_COLS), lambda i, j: (i, j))
    out = jax.ShapeDtypeStruct((t, D_FF), BF16)
    extra = cargo.specs() if cargo else []
    weights = [pl.BlockSpec((SWIGLU_COLS, D_MODEL), lambda i, j: (j, 0)),
               pl.BlockSpec((SWIGLU_COLS, D_MODEL), lambda i, j: (half + j, 0))]
    gain = [] if normed else [pl.BlockSpec((1, D_MODEL), lambda i, j: (0, 0))]
    outs = pl.pallas_call(
        body, name=name, grid=grid,
        in_specs=[row] + gain + weights + extra,
        out_specs=([] if normed else [row]) + [tile, tile, tile] + extra,
        out_shape=([] if normed else [jax.ShapeDtypeStruct((t, D_MODEL), BF16)]) + [out, out, out]
        + (cargo.out_shape() if cargo else []),
        scratch_shapes=([] if normed else [pltpu.VMEM((tm, D_MODEL), BF16)]) + (cargo.scratch() if cargo else []),
        compiler_params=_params(("arbitrary", "arbitrary")),
    )(h, *([] if normed else [g]), wt_in, wt_in, *(cargo.srcs if cargo else []))
    outs = ([h] if normed else []) + list(outs)
    return (outs[0], outs[1], outs[2], outs[3], list(outs[4:])) if cargo else outs


def _swiglu_bwd(dh, w_out, a, b, name):
    t = a.shape[0]
    tr = _row_tile(t, 512)

    def body(d_ref, w_ref, a_ref, b_ref, o_ref):
        dhv = d_ref[...]
        for cols in _chunks(D_FF):
            dv = _dot(dhv, w_ref[cols, :], NT_DIMS)
            av = a_ref[:, cols].astype(F32)
            s = _sigmoid(av)
            o_ref[:, cols] = (dv * b_ref[:, cols].astype(F32) * s * (1.0 + av * (1.0 - s))).astype(BF16)
            o_ref[:, slice(D_FF + cols.start, D_FF + cols.stop)] = (dv * av * s).astype(BF16)

    row = pl.BlockSpec((tr, D_FF), lambda i: (i, 0))
    return pl.pallas_call(
        body, name=name, grid=(t // tr,),
        in_specs=[pl.BlockSpec((tr, D_MODEL), lambda i: (i, 0)), pl.BlockSpec((D_FF, D_MODEL), lambda i: (0, 0)), row, row],
        out_specs=pl.BlockSpec((tr, 2 * D_FF), lambda i: (i, 0)),
        out_shape=jax.ShapeDtypeStruct((t, 2 * D_FF), BF16),
        compiler_params=_params(("parallel",)),
    )(dh, w_out, a, b)


def _latent_fwd(proj, gq, gkv):
    t = proj.shape[0]
    tr = _row_tile(t)

    def body(p_ref, gq_ref, gkv_ref, cq_ref, ckv_ref):
        cq = p_ref[:, P_CQ:P_CQ + Q_LORA].astype(F32)
        ckv = p_ref[:, P_CKV:P_CKV + KV_LORA].astype(F32)
        cq_ref[...] = ((cq * _rms(cq, Q_LORA)) * gq_ref[...]).astype(BF16)
        ckv_ref[...] = ((ckv * _rms(ckv, KV_LORA)) * gkv_ref[...]).astype(BF16)

    return pl.pallas_call(
        body, name="latent_fwd", grid=(t // tr,),
        in_specs=[pl.BlockSpec((tr, 1024), lambda i: (i, 0)), pl.BlockSpec((1, Q_LORA), lambda i: (0, 0)),
                  pl.BlockSpec((1, KV_LORA), lambda i: (0, 0))],
        out_specs=[pl.BlockSpec((tr, Q_LORA), lambda i: (i, 0)), pl.BlockSpec((tr, KV_LORA), lambda i: (i, 0))],
        out_shape=[jax.ShapeDtypeStruct((t, Q_LORA), BF16), jax.ShapeDtypeStruct((t, KV_LORA), BF16)],
        compiler_params=_params(("parallel",)),
    )(proj, gq, gkv)


def _latent_bwd(dcqn, dckvn, proj, dkr, gq, gkv, dgates, dsq, dsk, dsv):
    t = proj.shape[0]
    tr = _row_tile(t, 256)

    def norm_bwd(dn, x, g, width):
        r = _rms(x, width)
        y = x * r
        dy = dn * g
        dx = r * (dy - y * (jnp.sum(dy * y, axis=-1, keepdims=True) * (1.0 / width)))
        return dx, jnp.sum(dn * y, axis=0, keepdims=True)

    def body(dcq_ref, dckv_ref, p_ref, dkr_ref, gq_ref, gkv_ref, dg_ref, dsq_ref, dsk_ref, dsv_ref,
             o_ref, dgq_ref, dgkv_ref):
        i = pl.program_id(0)
        o_ref[:, P_GM:P_SBQ] = dg_ref[...]
        for at, ref in ((P_SBQ, dsq_ref), (P_SBK, dsk_ref), (P_SBV, dsv_ref)):
            o_ref[:, at:at + SB_WIDTH] = ref[...].astype(BF16)
        dcq, pq = norm_bwd(dcq_ref[...], p_ref[:, P_CQ:P_CQ + Q_LORA].astype(F32), gq_ref[...], Q_LORA)
        dckv, pkv = norm_bwd(dckv_ref[...], p_ref[:, P_CKV:P_CKV + KV_LORA].astype(F32), gkv_ref[...], KV_LORA)
        o_ref[:, P_CQ:P_CQ + Q_LORA] = dcq.astype(BF16)
        o_ref[:, P_CKV:P_CKV + KV_LORA] = dckv.astype(BF16)
        o_ref[:, P_KR:P_KR + 128] = dkr_ref[...].astype(BF16)
        o_ref[:, P_KR + 128:1024] = jnp.zeros((tr, 1024 - P_KR - 128), BF16)

        @pl.when(i == 0)
        def _():
            dgq_ref[...] = pq
            dgkv_ref[...] = pkv

        @pl.when(i > 0)
        def _():
            dgq_ref[...] += pq
            dgkv_ref[...] += pkv

    def row(w):
        return pl.BlockSpec((tr, w), lambda i: (i, 0))

    def vec(w):
        return pl.BlockSpec((1, w), lambda i: (0, 0))

    return pl.pallas_call(
        body, name="latent_bwd", grid=(t // tr,),
        in_specs=[row(Q_LORA), row(KV_LORA), row(1024), row(128), vec(Q_LORA), vec(KV_LORA),
                  row(2 * D_MODEL), row(SB_WIDTH), row(SB_WIDTH), row(SB_WIDTH)],
        out_specs=[row(PROJ_W), vec(Q_LORA), vec(KV_LORA)],
        out_shape=[jax.ShapeDtypeStruct((t, PROJ_W), BF16), jax.ShapeDtypeStruct((1, Q_LORA), F32),
                   jax.ShapeDtypeStruct((1, KV_LORA), F32)],
        compiler_params=_params(("arbitrary",)),
    )(dcqn, dckvn, proj, dkr, gq, gkv, dgates, dsq, dsk, dsv)


def _rope(y, cosf, sin_a, sin_b):
    return y * cosf + pltpu.roll(y, 112, 1) * sin_a + pltpu.roll(y, 16, 1) * sin_b


def _rope_t(d, cosf, sin_a, sin_b):
    return d * cosf + pltpu.roll(d * sin_a, 16, 1) + pltpu.roll(d * sin_b, 112, 1)


def _headprep_fwd(qraw, kvraw, proj, cosf, sin_a, sin_b, gqh, gkh):
    t = qraw.shape[0]
    tr = _row_tile(t, 256)

    def body(q_ref, kv_ref, kr_ref, c_ref, sa_ref, sb_ref, gq_ref, gk_ref, qh_ref, kh_ref, kvb_ref):
        cv, sa, sb = c_ref[...], sa_ref[...], sb_ref[...]
        kr = kr_ref[...].astype(F32)
        lane = lax.broadcasted_iota(jnp.int32, (tr, HEAD_PAD), 1)
        heads = [slice(h * HEAD_PAD, (h + 1) * HEAD_PAD) for h in range(N_HEADS)]
        xqs = [q_ref[:, cols] for cols in heads]
        kvs = [kv_ref[:, cols] for cols in heads]
        xks = [jnp.where(lane < MLA_NOPE, kvh, kr) for kvh in kvs]
        rqs = [_rms(x, MLA_QK) for x in xqs]
        rks = [_rms(x, MLA_QK) for x in xks]
        gq = gq_ref[...] * MLA_Q_SCALE
        yqs = [(x * r) * gq for x, r in zip(xqs, rqs)]
        yks = [(x * r) * gk_ref[...] for x, r in zip(xks, rks)]
        for cols, yq, yk, kvh in zip(heads, yqs, yks, kvs):
            qh_ref[:, cols] = _rope(yq, cv, sa, sb).astype(BF16)
            kh_ref[:, cols] = _rope(yk, cv, sa, sb).astype(BF16)
            kvb_ref[:, cols] = jnp.where(lane < MLA_NOPE, 1.0, kvh).astype(BF16)

    wide = pl.BlockSpec((tr, 1024), lambda i: (i, 0))
    lanes = pl.BlockSpec((tr, HEAD_PAD), lambda i: (i, 0))
    vec = pl.BlockSpec((1, HEAD_PAD), lambda i: (0, 0))
    return pl.pallas_call(
        body, name="headprep_fwd", grid=(t // tr,),
        in_specs=[wide, wide, pl.BlockSpec((tr, HEAD_PAD), lambda i: (i, P_KR // HEAD_PAD)), lanes, lanes, lanes, vec, vec],
        out_specs=[wide, wide, wide],
        out_shape=[jax.ShapeDtypeStruct((t, 1024), BF16)] * 3,
        compiler_params=_params(("parallel",)),
    )(qraw, kvraw, proj, cosf, sin_a, sin_b, gqh, gkh)


def _headprep_bwd(dqh, dkh, dvp, qraw, kvraw, proj, cosf, sin_a, sin_b, gqh, gkh):
    t = qraw.shape[0]
    tr = _row_tile(t, 256)

    def body(dq_ref, dk_ref, dv_ref, q_ref, kv_ref, kr_ref, c_ref, sa_ref, sb_ref, gq_ref, gk_ref,
             dqr_ref, dkvr_ref, dkr_ref, dgq_ref, dgk_ref):
        i = pl.program_id(0)
        cv, sa, sb = c_ref[...], sa_ref[...], sb_ref[...]
        kr = kr_ref[...].astype(F32)
        lane = lax.broadcasted_iota(jnp.int32, (tr, HEAD_PAD), 1)
        heads = [slice(h * HEAD_PAD, (h + 1) * HEAD_PAD) for h in range(N_HEADS)]
        xs = [q_ref[:, cols] for cols in heads] + [jnp.where(lane < MLA_NOPE, kv_ref[:, cols], kr) for cols in heads]
        gs = [gq_ref[...]] * N_HEADS + [gk_ref[...]] * N_HEADS
        dns = [_rope_t(ref[:, cols], cv, sa, sb) for ref in (dq_ref, dk_ref) for cols in heads]
        rs = [_rms(x, MLA_QK) for x in xs]
        ys = [x * r for x, r in zip(xs, rs)]
        dys = [dn * g for dn, g in zip(dns, gs)]
        means = [jnp.sum(dy * y, axis=-1, keepdims=True) * (1.0 / MLA_QK) for dy, y in zip(dys, ys)]
        dxs = [r * (dy - y * m) for r, dy, y, m in zip(rs, dys, ys, means)]
        parts = [jnp.sum(dn * y, axis=0, keepdims=True) for dn, y in zip(dns, ys)]
        dkr = jnp.zeros((tr, HEAD_PAD), F32)
        pq = jnp.zeros((1, HEAD_PAD), F32)
        pk = jnp.zeros((1, HEAD_PAD), F32)
        for h, cols in enumerate(heads):
            dqr_ref[:, cols] = dxs[h].astype(BF16)
            dxk = dxs[N_HEADS + h]
            dkvr_ref[:, cols] = jnp.where(lane < MLA_NOPE, dxk, dv_ref[:, cols]).astype(BF16)
            dkr = dkr + jnp.where(lane < MLA_NOPE, 0.0, dxk)
            pq = pq + parts[h]
            pk = pk + parts[N_HEADS + h]
        dkr_ref[...] = dkr

        @pl.when(i == 0)
        def _():
            dgq_ref[...] = pq
            dgk_ref[...] = pk

        @pl.when(i > 0)
        def _():
            dgq_ref[...] += pq
            dgk_ref[...] += pk

    wide = pl.BlockSpec((tr, 1024), lambda i: (i, 0))
    lanes = pl.BlockSpec((tr, HEAD_PAD), lambda i: (i, 0))
    vec = pl.BlockSpec((1, HEAD_PAD), lambda i: (0, 0))
    return pl.pallas_call(
        body, name="headprep_bwd", grid=(t // tr,),
        in_specs=[wide, wide, wide, wide, wide, pl.BlockSpec((tr, HEAD_PAD), lambda i: (i, P_KR // HEAD_PAD)),
                  lanes, lanes, lanes, vec, vec],
        out_specs=[wide, wide, lanes, vec, vec],
        out_shape=[jax.ShapeDtypeStruct((t, 1024), BF16), jax.ShapeDtypeStruct((t, 1024), BF16),
                   jax.ShapeDtypeStruct((t, HEAD_PAD), F32), jax.ShapeDtypeStruct((1, HEAD_PAD), F32),
                   jax.ShapeDtypeStruct((1, HEAD_PAD), F32)],
        compiler_params=_params(("arbitrary",)),
    )(dqh, dkh, dvp, qraw, kvraw, proj, cosf, sin_a, sin_b, gqh, gkh)


def _merge_fwd(o_mla, w_mla, o_sb, w_sb, proj):
    t = proj.shape[0]
    tr = _row_tile(t, 512)

    def body(om_ref, wm_ref, os_ref, ws_ref, gm_ref, gs_ref, o_ref, bm_ref, bs_ref):
        omv, osv = om_ref[...], os_ref[...]
        for cols in _chunks(D_MODEL):
            bm = _dot(omv, wm_ref[:, cols], NN_DIMS)
            bs = _dot(osv, ws_ref[:, cols], NN_DIMS)
            bm_ref[:, cols] = bm
            bs_ref[:, cols] = bs
            gm = _sigmoid(gm_ref[:, cols].astype(F32))
            gs = _sigmoid(gs_ref[:, cols].astype(F32))
            o_ref[:, cols] = (gm * bm + gs * bs).astype(BF16)

    row = pl.BlockSpec((tr, 1024), lambda i: (i, 0))
    f32 = jax.ShapeDtypeStruct((t, 1024), F32)
    return pl.pallas_call(
        body, name="merge_fwd", grid=(t // tr,),
        in_specs=[row, pl.BlockSpec(w_mla.shape, lambda i: (0, 0)),
                  pl.BlockSpec((tr, SB_WIDTH), lambda i: (i, 0)), pl.BlockSpec(w_sb.shape, lambda i: (0, 0)),
                  pl.BlockSpec((tr, 1024), lambda i: (i, P_GM // 1024)), pl.BlockSpec((tr, 1024), lambda i: (i, P_GS // 1024))],
        out_specs=[row, row, row], out_shape=[jax.ShapeDtypeStruct((t, 1024), BF16), f32, f32],
        compiler_params=_params(("parallel",)),
    )(o_mla, w_mla, o_sb, w_sb, proj, proj)


def _merge_bwd(dh, w_out, proj, bm, bs):
    t = proj.shape[0]
    tr = _row_tile(t, 512)

    def body(d_ref, w_ref, gm_ref, gs_ref, bm_ref, bs_ref, dbm_ref, dbs_ref, dg_ref):
        dhv = d_ref[...]
        for cols in _chunks(D_MODEL):
            dm = _dot(dhv, w_ref[cols, :], NT_DIMS)
            gm = _sigmoid(gm_ref[:, cols].astype(F32))
            gs = _sigmoid(gs_ref[:, cols].astype(F32))
            dbm_ref[:, cols] = (dm * gm).astype(BF16)
            dbs_ref[:, cols] = (dm * gs).astype(BF16)
            dg_ref[:, cols] = (dm * bm_ref[:, cols] * gm * (1.0 - gm)).astype(BF16)
            dg_ref[:, slice(D_MODEL + cols.start, D_MODEL + cols.stop)] = (dm * bs_ref[:, cols] * gs * (1.0 - gs)).astype(BF16)

    row = pl.BlockSpec((tr, 1024), lambda i: (i, 0))
    return pl.pallas_call(
        body, name="mix_out_dx", grid=(t // tr,),
        in_specs=[row, pl.BlockSpec((D_MODEL, D_MODEL), lambda i: (0, 0)),
                  pl.BlockSpec((tr, 1024), lambda i: (i, P_GM // 1024)),
                  pl.BlockSpec((tr, 1024), lambda i: (i, P_GS // 1024)), row, row],
        out_specs=[row, row, pl.BlockSpec((tr, 2048), lambda i: (i, 0))],
        out_shape=[jax.ShapeDtypeStruct((t, 1024), BF16), jax.ShapeDtypeStruct((t, 1024), BF16),
                   jax.ShapeDtypeStruct((t, 2048), BF16)],
        compiler_params=_params(("parallel",)),
    )(dh, w_out, proj, proj, bm, bs)


def _ple_loss(h3, g, w_gate, pb, w_proj, tgt):
    t = h3.shape[0]
    tr = _row_tile(t, 512)

    def body(h_ref, g_ref, wg_ref, p_ref, wp_ref, t_ref, n_ref, dh_ref, dz_ref, dp_ref, l_ref):
        i = pl.program_id(0)
        xv = h_ref[...]
        nv = ((xv * _rms(xv, D_MODEL)) * g_ref[...]).astype(BF16)
        n_ref[...] = nv
        pv = p_ref[...]
        part = jnp.zeros((1, 128), F32)
        for cols in _chunks(D_MODEL):
            pg = _sigmoid(_dot(nv, wg_ref[:, cols], NN_DIMS))
            ppv = _dot(pv, wp_ref[:, cols], NN_DIMS)
            diff = (h_ref[:, cols] + pg * ppv) - t_ref[:, cols]
            dh = diff * (1.0 / D_MODEL)
            dh_ref[:, cols] = dh
            dp_ref[:, cols] = (dh * pg).astype(BF16)
            dz_ref[:, cols] = (dh * ppv * pg * (1.0 - pg)).astype(BF16)
            sq = jnp.sum(diff * diff, axis=0, keepdims=True)
            for c in range(sq.shape[1] // 128):
                part = part + sq[:, c * 128:(c + 1) * 128]

        @pl.when(i == 0)
        def _():
            l_ref[...] = part

        @pl.when(i > 0)
        def _():
            l_ref[...] += part

    row = pl.BlockSpec((tr, 1024), lambda i: (i, 0))
    return pl.pallas_call(
        body, name="ple_loss", grid=(t // tr,),
        in_specs=[row, pl.BlockSpec((1, D_MODEL), lambda i: (0, 0)), pl.BlockSpec((D_MODEL, D_MODEL), lambda i: (0, 0)),
                  pl.BlockSpec((tr, PLE_DIM), lambda i: (i, 0)), pl.BlockSpec((PLE_DIM, D_MODEL), lambda i: (0, 0)), row],
        out_specs=[row, row, row, row, pl.BlockSpec((1, 128), lambda i: (0, 0))],
        out_shape=[jax.ShapeDtypeStruct((t, 1024), BF16), jax.ShapeDtypeStruct((t, 1024), F32),
                   jax.ShapeDtypeStruct((t, 1024), BF16), jax.ShapeDtypeStruct((t, 1024), BF16),
                   jax.ShapeDtypeStruct((1, 128), F32)],
        compiler_params=_params(("arbitrary",)),
    )(h3, g, w_gate, pb, w_proj, tgt)


ATT_BLOCK = 256
MLA_Q_SCALE = math.log2(math.e) / math.sqrt(MLA_QK)
MLA_Q_BLOCK = 512
MLA_FWD_COLS = 8
MLA_BWD_COLS = 4
SB_FWD_COLS = 4
SB_BWD_COLS = 2
SB_BLOCK = 256


def _split_bf16(x):
    hi = x.astype(BF16)
    return hi, (x - hi.astype(F32)).astype(BF16)


def _tri(kind, n):
    r = lax.broadcasted_iota(jnp.int32, (n, n), 0)
    c = lax.broadcasted_iota(jnp.int32, (n, n), 1)
    cond = {'gt': r > c, 'le': r <= c, 'lt': r < c}[kind]
    return jnp.where(cond, 1.0, 0.0).astype(BF16)


def _causal(strict, n=ATT_BLOCK):
    r = lax.broadcasted_iota(jnp.int32, (n, n), 0)
    c = lax.broadcasted_iota(jnp.int32, (n, n), 1)
    return (c < r) if strict else (c <= r)


def _below_diagonal(rows):
    r = lax.broadcasted_iota(jnp.int32, (rows, ATT_BLOCK), 0)
    c = lax.broadcasted_iota(jnp.int32, (rows, ATT_BLOCK), 1)
    return c <= r


def _lanes(c):
    return slice(c * HEAD_PAD, (c + 1) * HEAD_PAD)


def _row_block(j, n=ATT_BLOCK):
    return pl.ds(pl.multiple_of(j * n, n), n)


def _rows(ref, j, c, n=ATT_BLOCK):
    return ref[_row_block(j, n), _lanes(c)]


def _mla_fwd(qh, kh, kvb, cargo=None):
    t = qh.shape[0]
    bq = min(MLA_Q_BLOCK, t)
    per_q = bq // ATT_BLOCK
    ncol = MLA_FWD_COLS
    grid = (N_HEADS // ncol, t // bq)

    def body(*refs):
        steps = [pl.program_id(0), pl.program_id(1)]
        _with_cargo(cargo, refs, 3, 2, steps, grid, lambda own: work(steps[1], *own))

    def work(i, q_ref, k_ref, v_ref, o_ref, lse_ref):
        qs = [q_ref[:, _lanes(c)] for c in range(ncol)]

        def step(j, carry, top):
            cols = range(ncol)
            first = top or 0
            scores = [_dot(qs[c][first:], _rows(k_ref, j, c), NT_DIMS) for c in cols]
            ms, ps, alphas = [], [], []
            for c in cols:
                s = scores[c]
                if top is not None:
                    s = jnp.where(_below_diagonal(bq - first), s, -1e30)
                m_old = carry[c][0][first:]
                m_new = jnp.maximum(m_old, jnp.max(s, axis=-1, keepdims=True))
                ps.append(jnp.exp2(s - m_new).astype(BF16))
                alphas.append(jnp.exp2(m_old - m_new))
                ms.append(m_new)
            accs = [alphas[c] * carry[c][1][first:] + _dot(ps[c], _rows(v_ref, j, c), NN_DIMS) for c in cols]
            if first:
                ms = [jnp.concatenate([carry[c][0][:first], ms[c]], axis=0) for c in cols]
                accs = [jnp.concatenate([carry[c][1][:first], accs[c]], axis=0) for c in cols]
            return tuple(zip(ms, accs))

        init = tuple((jnp.full((bq, 1), -1e30, F32), jnp.zeros((bq, HEAD_PAD), F32)) for _ in range(ncol))
        carry = lax.fori_loop(0, i * per_q, lambda j, cr: step(j, cr, None), init)
        for d in range(per_q):
            carry = step(i * per_q + d, carry, d * ATT_BLOCK)
        for c, (m, acc) in enumerate(carry):
            l = acc[:, 0:1]
            o_ref[:, _lanes(c)] = (acc / l).astype(BF16)
            lse_ref[c] = m + jnp.log2(l)

    width = ncol * HEAD_PAD
    full = pl.BlockSpec((t, width), lambda h, i: (0, h))
    blk = pl.BlockSpec((bq, width), lambda h, i: (i, h))
    extra = cargo.specs() if cargo else []
    outs = pl.pallas_call(
        body, name="mla_fwd", grid=grid,
        in_specs=[blk, full, full] + extra,
        out_specs=[blk, pl.BlockSpec((ncol, bq, 1), lambda h, i: (h, i, 0))] + extra,
        out_shape=[jax.ShapeDtypeStruct((t, N_HEADS * HEAD_PAD), BF16), jax.ShapeDtypeStruct((N_HEADS, t, 1), F32)]
        + (cargo.out_shape() if cargo else []),
        scratch_shapes=cargo.scratch() if cargo else [],
        compiler_params=_params(("arbitrary", "arbitrary")),
    )(qh, kh, kvb, *(cargo.srcs if cargo else []))
    return (outs[0], outs[1], list(outs[2:])) if cargo else outs


def _mla_bwd(qh, kh, kvb, o, do, lse, cargo=None):
    t = qh.shape[0]
    bq = min(MLA_Q_BLOCK, t)
    per_q = bq // ATT_BLOCK
    ncol = MLA_BWD_COLS
    width = ncol * HEAD_PAD
    grid = (N_HEADS // ncol, t // bq)

    def body(*refs):
        steps = [pl.program_id(0), pl.program_id(1)]
        _with_cargo(cargo, refs, 6, 3, steps, grid, lambda own: work(steps[0], steps[1], *own))

    def work(h, i, q_ref, k_ref, v_ref, o_ref, do_ref, lse_ref, dq_ref, dk_hbm, dv_hbm, dk_ref, dv_ref, out_sems):

        @pl.when(i == 0)
        def _():
            dk_ref[...] = jnp.zeros_like(dk_ref)
            dv_ref[...] = jnp.zeros_like(dv_ref)

        qs = [q_ref[:, _lanes(c)] for c in range(ncol)]
        dos = [do_ref[:, _lanes(c)] for c in range(ncol)]
        deltas = [jnp.sum(dos[c].astype(F32) * o_ref[:, _lanes(c)].astype(F32), axis=-1, keepdims=True)
                  for c in range(ncol)]
        lses = [lse_ref[c] for c in range(ncol)]

        def step(j, dqs, top):
            cols = range(ncol)
            first = top or 0
            kbs = [_rows(k_ref, j, c) for c in cols]
            scores = [_dot(qs[c][first:], kbs[c], NT_DIMS) for c in cols]
            dps = [_dot(dos[c][first:], _rows(v_ref, j, c), NT_DIMS) for c in cols]
            pbs, dss = [], []
            for c in cols:
                p = jnp.exp2(scores[c] - lses[c][first:])
                if top is not None:
                    p = jnp.where(_below_diagonal(bq - first), p, 0.0)
                pbs.append(p.astype(BF16))
                dss.append((p * (dps[c] - deltas[c][first:])).astype(BF16))
            for c in cols:
                dv_ref[_row_block(j), _lanes(c)] += _dot(pbs[c], dos[c][first:], TN_DIMS)
                dk_ref[_row_block(j), _lanes(c)] += _dot(dss[c], qs[c][first:], TN_DIMS)
            new = [dqs[c][first:] + _dot(dss[c], kbs[c], NN_DIMS) for c in cols]
            if first:
                new = [jnp.concatenate([dqs[c][:first], new[c]], axis=0) for c in cols]
            return tuple(new)

        init = tuple(jnp.zeros((bq, HEAD_PAD), F32) for _ in range(ncol))
        dqs = lax.fori_loop(0, i * per_q, lambda j, cr: step(j, cr, None), init)
        for d in range(per_q):
            dqs = step(i * per_q + d, dqs, d * ATT_BLOCK)
        for c, dq in enumerate(dqs):
            dq_ref[:, _lanes(c)] = dq * (1.0 / math.sqrt(MLA_QK))

        @pl.when(i == grid[1] - 1)
        def _():
            dk_ref[...] = dk_ref[...] * math.log(2.0)
            cols = pl.ds(pl.multiple_of(h * width, width), width)
            out = [pltpu.make_async_copy(dk_ref, dk_hbm.at[:, cols], out_sems.at[0]),
                   pltpu.make_async_copy(dv_ref, dv_hbm.at[:, cols], out_sems.at[1])]
            for cp in out:
                cp.start()
            for cp in out:
                cp.wait()

    full = pl.BlockSpec((t, width), lambda h, i: (0, h))
    blk = pl.BlockSpec((bq, width), lambda h, i: (i, h))
    wide = jax.ShapeDtypeStruct((t, N_HEADS * HEAD_PAD), F32)
    extra = cargo.specs() if cargo else []
    outs = pl.pallas_call(
        body, name="mla_bwd", grid=grid,
        in_specs=[blk, full, full, blk, blk, pl.BlockSpec((ncol, bq, 1), lambda h, i: (h, i, 0))] + extra,
        out_specs=[blk, HBM_SPEC, HBM_SPEC] + extra,
        out_shape=[wide, wide, wide] + (cargo.out_shape() if cargo else []),
        scratch_shapes=[pltpu.VMEM((t, width), F32), pltpu.VMEM((t, width), F32), pltpu.SemaphoreType.DMA((2,))]
        + (cargo.scratch() if cargo else []),
        compiler_params=_params(("arbitrary", "arbitrary")),
    )(qh, kh, kvb, o, do, lse, *(cargo.srcs if cargo else []))
    return (outs[0], outs[1], outs[2], list(outs[3:])) if cargo else outs


def _head_only(x, lane, u):
    return jnp.where((lane >= u * SB_DIM) & (lane < (u + 1) * SB_DIM), x, jnp.zeros_like(x))


SB_DEAD = -104.0


def _log_sigmoids(z):
    e = jnp.exp(-jnp.abs(z))
    lg = jnp.log(1.0 + e)
    ls_pos = jnp.minimum(z, 0.0) - lg
    return ls_pos, ls_pos - z, e


def _sb_fwd(proj):
    t = proj.shape[0]
    bq, ncol = SB_BLOCK, SB_FWD_COLS
    nq = t // bq
    scale = 1.0 / math.sqrt(SB_DIM)
    pairs = SB_WIDTH // HEAD_PAD

    def body(q_ref, k_ref, v_ref, o_ref, r_ref, first_ref):
        g, i = pl.program_id(0), pl.program_id(1)
        lane = lax.broadcasted_iota(jnp.int32, (bq, HEAD_PAD), 1)
        upper = _tri('gt', bq)
        chains = [(c, u) for c in range(ncol) for u in range(2)]
        qms = [_head_only(q_ref[:, _lanes(c)], lane, u) * scale for c, u in chains]

        def step(j, carry, masked):
            ids = range(len(chains))
            zs = [_dot(qms[n], _rows(k_ref, j, chains[n][0], bq), NT_DIMS) for n in ids]
            pos, neg, parts = [], [], []
            for n in ids:
                ls_pos, ls_neg, _ = _log_sigmoids(zs[n])
                if masked:
                    ls_neg = jnp.where(_causal(True, bq), ls_neg, 0.0)
                pos.append(ls_pos)
                neg.append(ls_neg)
                parts.append(_split_bf16(ls_neg))
            suffix = [_dot(parts[n][0], upper, NN_DIMS) + _dot(parts[n][1], upper, NN_DIMS) for n in ids]
            weights = []
            for n in ids:
                a = jnp.exp(pos[n] + suffix[n] + carry[n][0])
                if masked:
                    a = jnp.where(_causal(True, bq), a, 0.0)
                weights.append(a.astype(BF16))
            return tuple((carry[n][0] + jnp.sum(neg[n], axis=-1, keepdims=True),
                          carry[n][1] + _dot(weights[n], _rows(v_ref, j, chains[n][0], bq), NN_DIMS)) for n in ids)

        init = tuple((jnp.zeros((bq, 1), F32), jnp.zeros((bq, HEAD_PAD), F32)) for _ in chains)
        carry = step(i, init, True)

        def more(state):
            s, cr = state
            live = cr[0][0]
            for n in range(1, len(chains)):
                live = jnp.maximum(live, cr[n][0])
            return jnp.logical_and(s < i, jnp.max(live) > SB_DEAD)

        walked, carry = lax.while_loop(more, lambda st: (st[0] + 1, step(i - 1 - st[0], st[1], False)),
                                       (jnp.int32(0), carry))
        first_ref[g * nq + i] = i - walked
        for n, (c, u) in enumerate(chains):
            r_ref[2 * c + u] = carry[n][0]
        for c in range(ncol):
            o_ref[:, _lanes(c)] = jnp.where(lane < SB_DIM, carry[2 * c][1], carry[2 * c + 1][1]).astype(BF16)

    width = ncol * HEAD_PAD

    def full(c0):
        return pl.BlockSpec((t, width), lambda g, i: (0, c0 // width + g))

    return pl.pallas_call(
        body, name="sb_fwd", grid=(pairs // ncol, t // bq),
        in_specs=[pl.BlockSpec((bq, width), lambda g, i: (i, P_SBQ // width + g)), full(P_SBK), full(P_SBV)],
        out_specs=[pl.BlockSpec((bq, width), lambda g, i: (i, g)),
                   pl.BlockSpec((2 * ncol, bq, 1), lambda g, i: (g, i, 0)),
                   pl.BlockSpec(memory_space=pltpu.SMEM)],
        out_shape=[jax.ShapeDtypeStruct((t, SB_WIDTH), BF16), jax.ShapeDtypeStruct((N_HEADS, t, 1), F32),
                   jax.ShapeDtypeStruct((pairs // ncol * nq,), jnp.int32)],
        compiler_params=_params(("arbitrary", "arbitrary")),
    )(proj, proj, proj)


def _sb_bwd(proj, do, rtot, first):
    t = proj.shape[0]
    bq, ncol = SB_BLOCK, SB_BWD_COLS
    nq = t // bq
    scale = 1.0 / math.sqrt(SB_DIM)
    pairs = SB_WIDTH // HEAD_PAD

    def body(first_ref, q_ref, k_ref, v_ref, do_ref, r_ref, dq_ref, dk_ref, dv_ref):
        g, i = pl.program_id(0), pl.program_id(1)

        @pl.when(i == 0)
        def _():
            dk_ref[...] = jnp.zeros_like(dk_ref)
            dv_ref[...] = jnp.zeros_like(dv_ref)

        lane = lax.broadcasted_iota(jnp.int32, (bq, HEAD_PAD), 1)
        incl = _tri('le', bq)
        excl = _tri('lt', bq)
        chains = [(c, u) for c in range(ncol) for u in range(2)]
        qms = [_head_only(q_ref[:, _lanes(c)], lane, u) * scale for c, u in chains]
        doms = [_head_only(do_ref[:, _lanes(c)], lane, u) for c, u in chains]
        rts = [r_ref[2 * c + u] for c, u in chains]

        def step(j, carry, masked):
            ids = range(len(chains))
            kbs = [_rows(k_ref, j, c, bq) for c in range(ncol)]
            zs =[_dot(qms[n], kbs[chains[n][0]], NT_DIMS) for n in ids]
            das = [_dot(doms[n], _rows(v_ref, j, chains[n][0], bq), NT_DIMS) for n in ids]
            pos, neg, sigs, parts = [], [], [], []
            for n in ids:
                ls_pos, ls_neg, e = _log_sigmoids(zs[n])
                if masked:
                    ls_neg = jnp.where(_causal(True, bq), ls_neg, 0.0)
                pos.append(ls_pos)
                neg.append(ls_neg)
                sigs.append(jnp.where(zs[n] >= 0.0, 1.0, e) * pl.reciprocal(1.0 + e, approx=True))
                parts.append(_split_bf16(ls_neg))
            prefix = [_dot(parts[n][0], incl, NN_DIMS) + _dot(parts[n][1], incl, NN_DIMS) for n in ids]
            evs, eparts, dvs = [], [], []
            for n in ids:
                a = jnp.exp(pos[n] + (rts[n] - (carry[n][0] + prefix[n])))
                if masked:
                    a = jnp.where(_causal(True, bq), a, 0.0)
                dvs.append(_dot(a.astype(BF16), doms[n], TN_DIMS))
                evs.append(a * das[n])
                eparts.append(evs[n].astype(BF16))
            before = [_dot(eparts[n], excl, NN_DIMS) for n in ids]
            out, dks = [], []
            for n in ids:
                dz = evs[n] - sigs[n] * (evs[n] + (carry[n][1] + before[n]))
                if masked:
                    dz = jnp.where(_causal(True, bq), dz, 0.0)
                dzb = dz.astype(BF16)
                dks.append(_dot(dzb, qms[n], TN_DIMS))
                out.append((carry[n][0] + jnp.sum(neg[n], axis=-1, keepdims=True),
                            carry[n][1] + jnp.sum(evs[n], axis=-1, keepdims=True),
                            carry[n][2] + _dot(dzb, kbs[chains[n][0]], NN_DIMS)))
            for c in range(ncol):
                dv_ref[_row_block(j, bq), _lanes(c)] += dvs[2 * c] + dvs[2 * c + 1]
                dk_ref[_row_block(j, bq), _lanes(c)] += dks[2 * c] + dks[2 * c + 1]
            return tuple(out)

        init = tuple((jnp.zeros((bq, 1), F32), jnp.zeros((bq, 1), F32), jnp.zeros((bq, HEAD_PAD), F32)) for _ in chains)
        start = first_ref[(g * ncol // SB_FWD_COLS) * nq + i]
        carry = lax.fori_loop(start, i, lambda j, cr: step(j, cr, False), init)
        carry = step(i, carry, True)
        for c in range(ncol):
            dq_ref[:, _lanes(c)] = jnp.where(lane < SB_DIM, carry[2 * c][2], carry[2 * c + 1][2]) * scale

    width = ncol * HEAD_PAD

    def full(c0):
        return pl.BlockSpec((t, width), lambda g, i, first: (0, c0 // width + g))

    blk = pl.BlockSpec((bq, width), lambda g, i, first: (i, g))
    acc = pl.BlockSpec((t, width), lambda g, i, first: (0, g))
    wide = jax.ShapeDtypeStruct((t, SB_WIDTH), F32)
    return pl.pallas_call(
        body, name="sb_bwd",
        grid_spec=pltpu.PrefetchScalarGridSpec(
            num_scalar_prefetch=1, grid=(pairs // ncol, nq),
            in_specs=[pl.BlockSpec((bq, width), lambda g, i, first: (i, P_SBQ // width + g)), full(P_SBK), full(P_SBV),
                      blk, pl.BlockSpec((2 * ncol, bq, 1), lambda g, i, first: (g, i, 0))],
            out_specs=[blk, acc, acc]),
        out_shape=[wide, wide, wide],
        compiler_params=_params(("arbitrary", "arbitrary")),
    )(first, proj, proj, proj, do, rtot)


def _cols_to_full(g):
    n, r, c = g.shape
    return jnp.transpose(g, (1, 0, 2)).reshape(r, n * c)


def _full_to_cols(w):
    r, c = w.shape
    return jnp.transpose(w.reshape(r, N_DEV, c // N_DEV), (1, 0, 2))


TRANSPOSED = ('ffn1_w_in', 'ffn2_w_in', 'w_in', 'w_q_up')


def _layout_weight(name, g):
    if name in ('ffn1_w_out', 'ffn2_w_out', 'w_out', 'w_ple_gate', 'ffn1_w_in', 'ffn2_w_in'):
        return g.reshape(g.shape[0] * g.shape[1], g.shape[2])
    if name == 'w_in':
        wt = g.reshape(IN_COLS, D_MODEL)
        z = lambda n: jnp.zeros((n, D_MODEL), BF16)
        return jnp.concatenate([wt[0:640], z(64), wt[640:672], z(32), z(256), wt[2208:4256], wt[672:2208]], axis=0)
    if name == 'w_q_up':
        return jnp.pad(g, ((0, 0), (0, HEAD_PAD - MLA_QK), (0, 0))).reshape(N_HEADS * HEAD_PAD, Q_LORA)
    if name == 'w_branch_mla':
        bm = _cols_to_full(g).reshape(N_HEADS, MLA_NOPE, D_MODEL)
        return jnp.pad(bm, ((0, 0), (HEAD_PAD - MLA_NOPE, 0), (0, 0))).reshape(N_HEADS * HEAD_PAD, D_MODEL)
    return _cols_to_full(g)


def _unlayout_grad(name, d):
    if name == 'w_in':
        d = jnp.concatenate([d[0:640], d[704:736], d[P_SBQ:PROJ_W], d[P_GM:P_SBQ]], axis=0)
    if name == 'w_q_up':
        return d.reshape(N_HEADS, HEAD_PAD, Q_LORA)[:, :MLA_QK, :]
    if name in ('ffn1_w_out', 'ffn2_w_out', 'w_out', 'w_ple_gate', 'ffn1_w_in', 'ffn2_w_in', 'w_in'):
        return d.reshape(N_DEV, d.shape[0] // N_DEV, d.shape[1])
    if name == 'w_branch_mla':
        d = d.reshape(N_HEADS, HEAD_PAD, D_MODEL)[:, HEAD_PAD - MLA_NOPE:, :].reshape(SB_WIDTH, D_MODEL)
    return _full_to_cols(d)


def _rope_tables(positions):
    half = MLA_ROPE // 2
    inv_freq = ROPE_BASE ** (-jnp.arange(0, MLA_ROPE, 2, dtype=F32) / MLA_ROPE)
    ang = positions.astype(F32)[:, None] * inv_freq
    cos, sin = jnp.cos(ang), jnp.sin(ang)
    t = positions.shape[0]
    ones = lambda n: jnp.ones((t, n), F32)
    zeros = lambda n: jnp.zeros((t, n), F32)
    cosf = jnp.concatenate([ones(MLA_NOPE), cos, cos, ones(HEAD_PAD - MLA_QK)], axis=1)
    sin_a = jnp.concatenate([zeros(MLA_NOPE), -sin, zeros(half), zeros(HEAD_PAD - MLA_QK)], axis=1)
    sin_b = jnp.concatenate([zeros(MLA_NOPE), zeros(half), sin, zeros(HEAD_PAD - MLA_QK)], axis=1)
    return cosf, sin_a, sin_b


def _local_step(x, p, positions, tgt, norms, plan):
    mm = _matmul
    cosf, sin_a, sin_b = _rope_tables(positions)
    pad_head = lambda g: jnp.pad(g, ((0, 0), (0, HEAD_PAD - MLA_QK)))
    gqh, gkh = pad_head(norms['q_head_norm']), pad_head(norms['k_head_norm'])
    pb = p.astype(BF16)
    w = dict(plan.first_weights())
    dw, dn = {}, {}

    def ride(host, call):
        cargo = plan.cargo(host, dw)
        res, lands = call(cargo), None
        if cargo is not None:
            *res, lands = res
            res = res[0] if len(res) == 1 else tuple(res)
        w.update(plan.landed(host, lands))
        return res

    def ffn_fwd(h, tag):
        hin, gain = h, norms[tag + '_norm']
        if tag + "_norm_fwd" in RIDES:
            hin = ride(tag + "_norm_fwd", lambda cargo: _rmsnorm_fwd(h, gain, tag + "_norm_fwd", cargo))
            gain = None
        n, a, b, act = ride(tag + "_in_fwd", lambda cargo: _swiglu_fwd(
            hin, gain, w[tag + '_w_in'], tag + "_in_fwd", cargo))
        out = ride(tag + "_out_fwd", lambda cargo: mm(
            act, w[tag + '_w_out'], mode='nn', out_dtype=F32, name=tag + "_out_fwd", res=h, alpha=0.5, cargo=cargo))
        return out, (n, a, b, act)

    h1, ffn1_saved = ffn_fwd(x, 'ffn1')
    u = _rmsnorm_fwd(h1, norms['mix_norm'], "mix_norm_fwd")
    proj = ride("proj_fwd", lambda cargo: mm(u, w['w_in'], mode='nt', out_dtype=BF16, name="proj_fwd", cargo=cargo))
    cqn, ckvn = _latent_fwd(proj, norms['q_latent_norm'], norms['kv_latent_norm'])
    qraw = mm(cqn, w['w_q_up'], mode='nt', out_dtype=F32, name="q_up_fwd")
    kvraw = mm(ckvn, w['w_kv_up'], mode='nn', out_dtype=F32, name="kv_up_fwd")
    qh, kh, kvb = _headprep_fwd(qraw, kvraw, proj, cosf, sin_a, sin_b, gqh, gkh)
    o_mla, lse = ride("mla_fwd", lambda cargo: _mla_fwd(qh, kh, kvb, cargo))
    o_sb, rtot, sb_first = _sb_fwd(proj)
    merged, bm, bs = _merge_fwd(o_mla, w['w_branch_mla'], o_sb, w['w_branch_sb'], proj)
    h2 = mm(merged, w['w_out'], mode='nn', out_dtype=F32, name="mix_out_fwd", res=h1)
    h3, ffn2_saved = ffn_fwd(h2, 'ffn2')
    n3, dh4, dzg, dpp, loss_lanes = _ple_loss(h3, norms['ple_norm'], w['w_ple_gate'], pb, w['w_ple_proj'], tgt)

    dw['w_ple_gate'] = mm(n3, dzg, mode='tn', out_dtype=BF16, name="ple_gate_dw")
    dw['w_ple_proj'] = mm(pb, dpp, mode='tn', out_dtype=BF16, name="ple_proj_dw")
    dh3, dhb3, dn['ple_norm'] = _matmul_norm_bwd(
        dzg, w['w_ple_gate'], h3, norms['ple_norm'], dh4, mode='nt', name="ple_gate_dx", out_scale=0.5)

    def ffn_bwd(h, dh, dhb, saved, tag, out_scale):
        n, a, b, act = saved
        dw[tag + '_w_out'] = mm(act, dhb, mode='tn', out_dtype=BF16, name=tag + "_out_dw", tm=1408)
        dab = _swiglu_bwd(dhb, w[tag + '_w_out'], a, b, tag + "_out_dx")
        dw[tag + '_w_in'] = ride(tag + "_in_dw", lambda cargo: mm(
            dab, n, mode='tn', out_dtype=BF16, name=tag + "_in_dw", tm=1408, cargo=cargo))
        dh_prev, dhb_prev, dn[tag + '_norm'] = ride(tag + "_in_dx", lambda cargo: _matmul_norm_bwd(
            dab, w[tag + '_w_in'], h, norms[tag + '_norm'], dh, mode='nn', name=tag + "_in_dx", out_scale=out_scale,
            cargo=cargo))
        return dh_prev, dhb_prev

    dh2, dhb2 = ffn_bwd(h2, dh3, dhb3, ffn2_saved, 'ffn2', 1.0)
    dw['w_out'] = mm(merged, dhb2, mode='tn', out_dtype=BF16, name="mix_out_dw")
    dbm, dbs, dgates = _merge_bwd(dhb2, w['w_out'], proj, bm, bs)
    dw['w_branch_mla'] = mm(o_mla, dbm, mode='tn', out_dtype=BF16, name="branch_mla_dw")
    dw['w_branch_sb'] = mm(o_sb, dbs, mode='tn', out_dtype=BF16, name="branch_sb_dw")
    do_mla = mm(dbm, w['w_branch_mla'], mode='nt', out_dtype=BF16, name="branch_mla_dx")
    do_sb = mm(dbs, w['w_branch_sb'], mode='nt', out_dtype=BF16, name="branch_sb_dx")
    dqh, dkh, dvp = ride("mla_bwd", lambda cargo: _mla_bwd(qh, kh, kvb, o_mla, do_mla, lse, cargo))
    dsq, dsk, dsv = _sb_bwd(proj, do_sb, rtot, sb_first)
    dqraw, dkvraw, dkr, dgq, dgk = _headprep_bwd(dqh, dkh, dvp, qraw, kvraw, proj, cosf, sin_a, sin_b, gqh, gkh)
    dn['q_head_norm'], dn['k_head_norm'] = dgq[:, :MLA_QK], dgk[:, :MLA_QK]
    dw['w_q_up'] = mm(dqraw, cqn, mode='tn', out_dtype=BF16, name="q_up_dw")
    dw['w_kv_up'] = mm(ckvn, dkvraw, mode='tn', out_dtype=BF16, name="kv_up_dw")
    dcqn = mm(dqraw, w['w_q_up'], mode='nn', out_dtype=F32, name="q_up_dx")
    dckvn = mm(dkvraw, w['w_kv_up'], mode='nt', out_dtype=F32, name="kv_up_dx")
    dproj, dn['q_latent_norm'], dn['kv_latent_norm'] = _latent_bwd(
        dcqn, dckvn, proj, dkr, norms['q_latent_norm'], norms['kv_latent_norm'], dgates, dsq, dsk, dsv)
    dw['w_in'] = ride("proj_dw", lambda cargo: mm(dproj, u, mode='tn', out_dtype=BF16, name="proj_dw", tm=1536, cargo=cargo))
    dh1, dhb1, dn['mix_norm'] = ride("proj_dx", lambda cargo: _matmul_norm_bwd(
        dproj, w['w_in'], h1, norms['mix_norm'], dh2, mode='nn', name="proj_dx", out_scale=0.5, cargo=cargo))
    dx, _ = ffn_bwd(x, dh1, dhb1, ffn1_saved, 'ffn1', 1.0)
    return dx, loss_lanes, dw, dn


MESH = pl.DeviceIdType.MESH
HBM_SPEC = pl.BlockSpec(memory_space=pl.ANY)


def _position():
    return lax.axis_index("x"), lax.axis_index("y"), lax.axis_index("c")


def _index(px, py, pc):
    return 4 * px + 2 * py + pc


def _all_gather(shards):
    n = len(shards)

    def body(*refs):
        ins, outs = refs[:n], refs[n:2 * n]
        send_sems, recv_sems, local_sems = refs[2 * n:]
        x, y, c = _position()
        me, sibling = (x, y, c), (x, y, 1 - c)
        chips = [(1 - x, y), (x, 1 - y), (1 - x, 1 - y)]

        def copy(a, k, block, to, own=False):
            dst = outs[a].at[_index(*block)]
            return pltpu.make_async_remote_copy(
                src_ref=ins[a] if own else dst, dst_ref=dst,
                send_sem=send_sems.at[a, k], recv_sem=recv_sems.at[a, k], device_id=to, device_id_type=MESH)

        mine = [pltpu.make_async_copy(ins[a], outs[a].at[_index(*me)], local_sems.at[a]) for a in range(n)]
        for cp in mine:
            cp.start()
        first = []
        for a in range(n):
            first.append(copy(a, 0, me, sibling, own=True))
            first += [copy(a, 1 + j, me, (*chip, c), own=True) for j, chip in enumerate(chips)]
        for cp in first:
            cp.start()
        passed = []
        for j, chip in enumerate(chips):
            for a in range(n):
                copy(a, 1 + j, (*chip, c), me).wait_recv()
                fwd = copy(a, 4 + j, (*chip, c), sibling)
                fwd.start()
                passed.append(fwd)
        for a in range(n):
            copy(a, 0, sibling, me).wait_recv()
            for j, chip in enumerate(chips):
                copy(a, 4 + j, (*chip, 1 - c), me).wait_recv()
        for cp in first + passed:
            cp.wait_send()
        for cp in mine:
            cp.wait()

    return pl.pallas_call(
        body, name="weights_all_gather",
        in_specs=[HBM_SPEC] * n, out_specs=[HBM_SPEC] * n,
        out_shape=[jax.ShapeDtypeStruct((N_DEV,) + s.shape, s.dtype) for s in shards],
        scratch_shapes=[pltpu.SemaphoreType.DMA((n, 7)), pltpu.SemaphoreType.DMA((n, 7)), pltpu.SemaphoreType.DMA((n,))],
    )(*shards)


def _exchange(parts):
    n = len(parts)
    masks = [(mx, my, mc) for mx in (0, 1) for my in (0, 1) for mc in (0, 1)][1:]

    def body(*refs):
        ins, outs = refs[:n], refs[n:2 * n]
        send_sems, recv_sems, local_sems = refs[2 * n:]
        x, y, c = _position()
        me = _index(x, y, c)

        def peer_of(mask):
            mx, my, mc = mask
            return (x + mx - 2 * x * mx, y + my - 2 * y * my, c + mc - 2 * c * mc)

        def copy(a, k):
            peer = peer_of(masks[k])
            return pltpu.make_async_remote_copy(
                src_ref=ins[a].at[_index(*peer)], dst_ref=outs[a].at[me],
                send_sem=send_sems.at[a, k], recv_sem=recv_sems.at[a, k], device_id=peer, device_id_type=MESH)

        def landed(a, k):
            peer = peer_of(masks[k])
            return pltpu.make_async_remote_copy(
                src_ref=ins[a].at[me], dst_ref=outs[a].at[_index(*peer)],
                send_sem=send_sems.at[a, k], recv_sem=recv_sems.at[a, k], device_id=peer, device_id_type=MESH)

        mine = [pltpu.make_async_copy(ins[a].at[me], outs[a].at[me], local_sems.at[a]) for a in range(n)]
        for cp in mine:
            cp.start()
        sent = [copy(a, k) for k in range(7) for a in range(n)]
        for cp in sent:
            cp.start()
        for k in range(7):
            for a in range(n):
                landed(a, k).wait_recv()
        for cp in sent:
            cp.wait_send()
        for cp in mine:
            cp.wait()

    return pl.pallas_call(
        body, name="grads_exchange",
        in_specs=[HBM_SPEC] * n, out_specs=[HBM_SPEC] * n,
        out_shape=[jax.ShapeDtypeStruct(s.shape, s.dtype) for s in parts],
        scratch_shapes=[pltpu.SemaphoreType.DMA((n, 7)), pltpu.SemaphoreType.DMA((n, 7)), pltpu.SemaphoreType.DMA((n,))],
    )(*parts)


PEER_MASKS = [(mx, my, mc) for mx in (0, 1) for my in (0, 1) for mc in (0, 1)][1:]


def _peer(mask):
    x, y, c = _position()
    mx, my, mc = mask
    return (x + mx - 2 * x * mx, y + my - 2 * y * my, c + mc - 2 * c * mc)


class _Cargo:
    def __init__(self, srcs, scatter):
        self.srcs, self.scatter, self.n = list(srcs), scatter, len(srcs)

    def specs(self):
        return [HBM_SPEC] * self.n

    def out_shape(self):
        return [jax.ShapeDtypeStruct(s.shape if self.scatter else (N_DEV,) + s.shape, s.dtype) for s in self.srcs]

    def scratch(self):
        per_copy = pltpu.SemaphoreType.DMA((self.n, len(PEER_MASKS)))
        return [per_copy, per_copy, pltpu.SemaphoreType.DMA((self.n,))]

    def _mine(self, src_refs, a, to):
        return src_refs[a].at[to] if self.scatter else src_refs[a]

    def _shard_copy(self, src_refs, land_refs, sems, a, k, block, to, own=False):
        dst = land_refs[a].at[_index(*block)]
        return pltpu.make_async_remote_copy(
            src_ref=src_refs[a] if own else dst, dst_ref=dst,
            send_sem=sems[0].at[a, k], recv_sem=sems[1].at[a, k], device_id=to, device_id_type=MESH)

    def _first_hops(self, src_refs, land_refs, sems):
        x, y, c = _position()
        chips = [(1 - x, y), (x, 1 - y), (1 - x, 1 - y)]
        hops = []
        for a in range(self.n):
            hops.append(self._shard_copy(src_refs, land_refs, sems, a, 0, (x, y, c), (x, y, 1 - c), own=True))
            hops += [self._shard_copy(src_refs, land_refs, sems, a, 1 + j, (x, y, c), (*chip, c), own=True)
                     for j, chip in enumerate(chips)]
        return hops, chips

    def start(self, src_refs, land_refs, sems):
        send, recv, local = sems
        me = _index(*_position())
        for a in range(self.n):
            pltpu.make_async_copy(self._mine(src_refs, a, me), land_refs[a].at[me], local.at[a]).start()
        if not self.scatter:
            for cp in self._first_hops(src_refs, land_refs, sems)[0]:
                cp.start()
            return
        for k, mask in enumerate(PEER_MASKS):
            peer = _peer(mask)
            for a in range(self.n):
                pltpu.make_async_remote_copy(
                    src_ref=self._mine(src_refs, a, _index(*peer)), dst_ref=land_refs[a].at[me],
                    send_sem=send.at[a, k], recv_sem=recv.at[a, k], device_id=peer, device_id_type=MESH).start()

    def _wait_gathered(self, src_refs, land_refs, sems):
        x, y, c = _position()
        me, sibling = (x, y, c), (x, y, 1 - c)
        first, chips = self._first_hops(src_refs, land_refs, sems)
        passed = []
        for j, chip in enumerate(chips):
            for a in range(self.n):
                self._shard_copy(src_refs, land_refs, sems, a, 1 + j, (*chip, c), me).wait_recv()
                passed.append(self._shard_copy(src_refs, land_refs, sems, a, 4 + j, (*chip, c), sibling))
                passed[-1].start()
        for a in range(self.n):
            self._shard_copy(src_refs, land_refs, sems, a, 0, sibling, me).wait_recv()
            for j, chip in enumerate(chips):
                self._shard_copy(src_refs, land_refs, sems, a, 4 + j, (*chip, 1 - c), me).wait_recv()
        for cp in first + passed:
            cp.wait_send()

    def wait(self, src_refs, land_refs, sems):
        send, recv, local = sems
        me = _index(*_position())
        if not self.scatter:
            self._wait_gathered(src_refs, land_refs, sems)
        for k, mask in enumerate(PEER_MASKS if self.scatter else []):
            peer = _peer(mask)
            there = _index(*peer)
            for a in range(self.n):
                pltpu.make_async_remote_copy(
                    src_ref=self._mine(src_refs, a, me), dst_ref=land_refs[a].at[there],
                    send_sem=send.at[a, k], recv_sem=recv.at[a, k], device_id=peer, device_id_type=MESH).wait_recv()
                pltpu.make_async_remote_copy(
                    src_ref=self._mine(src_refs, a, there), dst_ref=land_refs[a].at[me],
                    send_sem=send.at[a, k], recv_sem=recv.at[a, k], device_id=peer, device_id_type=MESH).wait_send()
        for a in range(self.n):
            pltpu.make_async_copy(self._mine(src_refs, a, me), land_refs[a].at[me], local.at[a]).wait()


def _with_cargo(cargo, refs, n_in, n_out, steps, counts, compute):
    if cargo is None:
        compute(refs)
        return
    n = cargo.n
    src_refs = refs[n_in:n_in + n]
    land_refs = refs[n_in + n + n_out:n_in + 2 * n + n_out]
    sems = refs[-3:]
    first = functools.reduce(jnp.logical_and, [s == 0 for s in steps])
    last = functools.reduce(jnp.logical_and, [s == c - 1 for s, c in zip(steps, counts)])

    @pl.when(first)
    def _():
        cargo.start(src_refs, land_refs, sems)

    compute(refs[:n_in] + refs[n_in + n:n_in + n + n_out] + refs[n_in + 2 * n + n_out:-3])

    @pl.when(last)
    def _():
        cargo.wait(src_refs, land_refs, sems)


def _adamw(parts, w, m, v, name):
    r, c = w.shape
    tr = next((t for t in (512, 384, 352, 256, 128) if r % t == 0), r) if r > 512 else r
    tc = c if tr < r or r <= 512 else 256
    assert r % tr == 0 and c % tc == 0
    bc1 = 1.0 - ADAM_B1 ** ADAM_STEP
    bc2 = 1.0 - ADAM_B2 ** ADAM_STEP

    def body(p_ref, w_ref, m_ref, v_ref, g_ref, d_ref, nm_ref, nv_ref):
        g = p_ref[0].astype(F32)
        for s in range(1, N_DEV):
            g = g + p_ref[s].astype(F32)
        nm = ADAM_B1 * m_ref[...] + (1.0 - ADAM_B1) * g
        nv = ADAM_B2 * v_ref[...] + (1.0 - ADAM_B2) * (g * g)
        g_ref[...] = g
        nm_ref[...] = nm
        nv_ref[...] = nv
        d_ref[...] = -ADAM_LR * ((nm / bc1) / (jnp.sqrt(nv / bc2) + ADAM_EPS) + ADAM_WD * w_ref[...])

    tile = pl.BlockSpec((tr, tc), lambda i, j: (i, j))
    out = jax.ShapeDtypeStruct((r, c), F32)
    return pl.pallas_call(
        body, name=name, grid=(r // tr, c // tc),
        in_specs=[pl.BlockSpec((N_DEV, tr, tc), lambda i, j: (0, i, j)), tile, tile, tile],
        out_specs=[tile] * 4, out_shape=[out] * 4,
        compiler_params=_params(("parallel", "parallel")),
    )(parts, w, m, v)


GATHER_FIRST = []
RIDES = {
    'ffn1_norm_fwd': ('weights', ['ffn1_w_in']),
    'ffn1_in_fwd': ('weights', ['ffn1_w_out']),
    'ffn1_out_fwd': ('weights', ['w_in']),
    'proj_fwd': ('weights', ['w_q_up', 'w_kv_up', 'w_branch_mla', 'w_branch_sb', 'w_out']),
    'mla_fwd': ('weights', ['ffn2_w_in', 'ffn2_w_out', 'w_ple_gate', 'w_ple_proj']),
    'mla_bwd': ('grads', ['w_ple_gate', 'w_ple_proj', 'ffn2_w_out', 'ffn2_w_in', 'w_out', 'w_branch_mla', 'w_branch_sb']),
    'proj_dw': ('grads', ['w_q_up', 'w_kv_up']),
    'proj_dx': ('grads', ['w_in']),
    'ffn1_in_dw': ('grads', ['ffn1_w_out']),
    'ffn1_in_dx': ('grads', ['ffn1_w_in']),
}


class _Plan:
    def __init__(self, shards):
        self.shards = shards
        self.received = {}

    def first_weights(self):
        if not GATHER_FIRST:
            return {}
        gathered = _all_gather([self.shards[n] for n in GATHER_FIRST])
        return {n: _layout_weight(n, g) for n, g in zip(GATHER_FIRST, gathered)}

    def cargo(self, host, dw):
        if host not in RIDES:
            return None
        kind, names = RIDES[host]
        if kind == 'weights':
            return _Cargo([self.shards[n] for n in names], False)
        return _Cargo([_unlayout_grad(n, dw.pop(n)) for n in names], True)

    def landed(self, host, lands):
        if host not in RIDES:
            return {}
        kind, names = RIDES[host]
        if kind == 'weights':
            return {n: _layout_weight(n, land) for n, land in zip(names, lands)}
        self.received.update(zip(names, lands))
        return {}


def _pack_small(vecs):
    flat = jnp.concatenate([v.reshape(-1) for v in vecs])
    return jnp.pad(flat, (0, SMALL_ROWS * 128 - flat.shape[0])).reshape(SMALL_ROWS, 128)


def _unpack_small(packed, sizes):
    flat = packed.reshape(-1)
    out, at = [], 0
    for n in sizes:
        out.append(flat[at:at + n])
        at += n
    return out


def kernel(x, p, positions, ffn1_norm, ffn1_w_in, ffn1_w_out, mix_norm, w_in, q_latent_norm, w_q_up, kv_latent_norm, w_kv_up, q_head_norm, k_head_norm, w_branch_mla, w_branch_sb, w_out, ffn2_norm, ffn2_w_in, ffn2_w_out, ple_norm, w_ple_gate, w_ple_proj, loss_target, m_ffn1_norm, m_ffn1_w_in, m_ffn1_w_out, m_mix_norm, m_w_in, m_q_latent_norm, m_w_q_up, m_kv_latent_norm, m_w_kv_up, m_q_head_norm, m_k_head_norm, m_w_branch_mla, m_w_branch_sb, m_w_out, m_ffn2_norm, m_ffn2_w_in, m_ffn2_w_out, m_ple_norm, m_w_ple_gate, m_w_ple_proj, v_ffn1_norm, v_ffn1_w_in, v_ffn1_w_out, v_mix_norm, v_w_in, v_q_latent_norm, v_w_q_up, v_kv_latent_norm, v_w_kv_up, v_q_head_norm, v_k_head_norm, v_w_branch_mla, v_w_branch_sb, v_w_out, v_ffn2_norm, v_ffn2_w_in, v_ffn2_w_out, v_ple_norm, v_w_ple_gate, v_w_ple_proj):
    given = dict(locals())
    wts = {n: given[n] for n in WEIGHTS}
    mom = {n: given['m_' + n] for n in WEIGHTS}
    var = {n: given['v_' + n] for n in WEIGHTS}

    def local(a, n):
        return jnp.swapaxes(a[0], 0, 1) if n in TRANSPOSED else a[0]

    plan = _Plan({n: local(wts[n], n).astype(BF16) for n in MATS})
    norms = {n: wts[n] for n in NORMS}
    dx, loss_lanes, dw, dn = _local_step(x[0], p[0, 0], positions[0], loss_target[0], norms, plan)
    assert not dw

    out = {}
    for n in MATS:
        res = _adamw(plan.received[n], local(wts[n], n), local(mom[n], n), local(var[n], n), "adamw_" + n)
        out[n] = [local(r[None], n)[None] for r in res]
    small = _pack_small([dn[n] for n in NORMS] + [0.5 / D_MODEL * jnp.sum(loss_lanes)[None]])
    small_parts = _exchange([jnp.broadcast_to(small[None], (N_DEV, SMALL_ROWS, 128))])[0]
    sizes = [wts[n].shape[1] for n in NORMS]
    pack = lambda d: _pack_small([d[n] for n in NORMS])
    small_res = _adamw(small_parts, pack(wts), pack(mom), pack(var), "adamw_norms")
    loss = small_res[0].reshape(-1)[sum(sizes)]
    for i, res in enumerate(small_res):
        for n, vec in zip(NORMS, _unpack_small(res, sizes)):
            out.setdefault(n, [None] * 4)[i] = vec[None]

    return (loss, dx[None], *[out[n][0] for n in WEIGHTS], *[out[n][1] for n in WEIGHTS],
            *[out[n][2] for n in WEIGHTS], *[out[n][3] for n in WEIGHTS])
```
